```python
import math
import jax
import jax.numpy as jnp
from jax import lax
import numpy as np

D_MODEL = 1024
BATCH = 16
SEQ = 256
DEPTH = 1
DEC_BATCH = 8
DEC_SEQ = 1024
PAST_LEN = 256

GRID_W = 64
POS_BASE = 10000.0
D_MIX = D_MODEL
D_HY = D_MIX // 2
D_S5 = D_MIX - D_HY
S5_CH = 16
S5_GROUPS = D_S5 // S5_CH
S5_STATE = 64
HY_ORDER = 2
HY_SHORT = 3
HY_BANDS = 16
HY_EMB = 1 + 2 * HY_BANDS
HY_HID = 64
HY_MIN_DECAY = math.log(1e-2) / 1.5
HY_MAX_DECAY = math.log(1e-2) / 0.3
N_EGROUPS = 4
N_EPG = 4
N_EXPERTS = N_EGROUPS * N_EPG
TOP_K_INNER = 2
D_EXPERT = 512
LN_EPS = 1e-5
F32 = jnp.float32

kernel_name = 'hyena_s5_hmoe_diffusion_step'


def _norm(x):
    xf = x.astype(F32)
    xc = xf - jnp.mean(xf, axis=-1, keepdims=True)
    return xc * lax.rsqrt(jnp.mean(xc * xc, axis=-1, keepdims=True) + LN_EPS)


def _ln_affine(x, g, b):
    return _norm(x) * g.astype(F32) + b.astype(F32)


def _rms(y):
    yf = y.astype(F32)
    return yf * lax.rsqrt(jnp.mean(yf * yf, axis=-1, keepdims=True) + LN_EPS)


def _grid_pos_embed(n_tokens):
    rows = n_tokens // GRID_W
    row = jnp.repeat(jnp.arange(rows, dtype=F32), GRID_W)
    col = jnp.tile(jnp.arange(GRID_W, dtype=F32), rows)
    quarter = D_MODEL // 4
    omega = 1.0 / (POS_BASE ** (jnp.arange(quarter, dtype=F32) / quarter))
    er = row[:, None] * omega
    ec = col[:, None] * omega
    return jnp.concatenate([jnp.sin(er), jnp.cos(er), jnp.sin(ec), jnp.cos(ec)], axis=-1)


def _hyena_pos_features(n_tok):
    t = jnp.linspace(0.0, 1.0, n_tok, dtype=F32)[:, None]
    w = 2.0 * math.pi * jnp.arange(n_tok, dtype=F32) / n_tok
    f = jnp.linspace(1e-4, HY_BANDS - 1, HY_BANDS, dtype=F32)
    ang = w[:, None] * f[None, :]
    z = jnp.concatenate([t, jnp.cos(ang), -jnp.sin(ang)], axis=-1)
    return z, t


def _hyena_filter_spectrum(n_tok, w1, b1, w2, b2, w3, freq):
    z, t = _hyena_pos_features(n_tok)
    fr = freq.astype(F32)
    h = jnp.sin(fr * (z @ w1.astype(F32) + b1.astype(F32)))
    h = jnp.sin(fr * (h @ w2.astype(F32) + b2.astype(F32)))
    h = (h @ w3.astype(F32)).reshape(n_tok, HY_ORDER, 2, D_HY)
    deltas = jnp.linspace(HY_MIN_DECAY, HY_MAX_DECAY, D_HY, dtype=F32)
    h = h * jnp.exp(-t[:, :, None, None] * jnp.abs(deltas))
    h_fwd = h[:, :, 0]
    h_bwd = h[:, :, 1]
    k = jnp.concatenate([h_fwd, jnp.zeros_like(h_fwd[:1]), h_bwd[:0:-1]], axis=0)
    return jnp.fft.rfft(k, axis=0)


def _fftconv(u, k_f, d):
    n_tok = u.shape[1]
    U = jnp.fft.rfft(u, n=2 * n_tok, axis=1)
    y = jnp.fft.irfft(U * k_f[None], n=2 * n_tok, axis=1)[:, :n_tok]
    return y + u * d.astype(F32)


def _hyena(proj, conv_w, conv_b, kf, fbias):
    n_tok = proj.shape[1]
    pad = HY_SHORT // 2
    p = jnp.pad(proj.astype(F32), ((0, 0), (pad, pad), (0, 0)))
    cw = conv_w.astype(F32)
    s = conv_b.astype(F32)
    for j in range(HY_SHORT):
        s = s + p[:, j:j + n_tok] * cw[j]
    v, x1, x2 = jnp.split(s, 3, axis=-1)
    z = x1 * _fftconv(v, kf[:, 0], fbias[0])
    return x2 * _fftconv(z, kf[:, 1], fbias[1])


def _zoh(a_re, a_im, log_dt, b_re, b_im):
    A = lax.complex(a_re.astype(F32), a_im.astype(F32))
    dt = jnp.exp(log_dt.astype(F32))[:, None]
    a_bar = jnp.exp(A * dt)
    Bm = lax.complex(b_re.astype(F32), b_im.astype(F32))
    b_bar = ((a_bar - 1.0) / A)[..., None] * Bm
    return a_bar, b_bar


def _ssm_binop(e1, e2):
    a1, b1 = e1
    a2, b2 = e2
    return a1 * a2, a2 * b1 + b2


def _s5_scan(a_bar, bu, h0):
    a = jnp.broadcast_to(a_bar, bu.shape)
    a_cum, xs = lax.associative_scan(_ssm_binop, (a, bu), axis=1)
    xs = xs + a_cum * h0[:, None]
    return xs, xs[:, -1]


def _s5(u, h0_f, h0_b, a_re, a_im, log_dt, b_re, b_im, c_re, c_im, d, w_glu, b_glu):
    bsz, n_tok, _ = u.shape
    uf = u.astype(F32).reshape(bsz, n_tok, S5_GROUPS, S5_CH)
    uc = uf.astype(jnp.complex64)
    af, bf = _zoh(a_re[0], a_im[0], log_dt[0], b_re[0], b_im[0])
    ab, bb = _zoh(a_re[1], a_im[1], log_dt[1], b_re[1], b_im[1])
    xs_f, h_f = _s5_scan(af, jnp.einsum('blgh,gph->blgp', uc, bf), h0_f)
    xs_b_rev, h_b = _s5_scan(ab, jnp.einsum('blgh,gph->blgp', uc[:, ::-1], bb), h0_b)
    states = xs_f + xs_b_rev[:, ::-1]
    C = lax.complex(c_re.astype(F32), c_im.astype(F32))
    y = jnp.einsum('blgp,ghp->blgh', states, C).real + uf * d.astype(F32).reshape(S5_GROUPS, S5_CH)
    y = y.reshape(bsz, n_tok, D_S5)
    y = jax.nn.gelu(y) * jax.nn.sigmoid(y @ w_glu.astype(F32) + b_glu.astype(F32))
    return y, h_f, h_b


def _hier_moe(h, w_r1, b_r1, w_r2, b_r2, w_gate, w_up, w_down):
    bsz, n_tok, dm = h.shape
    t = h.reshape(bsz * n_tok, dm)
    tf = t.astype(F32)
    n = t.shape[0]
    rows = jnp.arange(n)
    l1 = tf @ w_r1.astype(F32) + b_r1.astype(F32)
    p1 = jax.nn.softmax(l1, axis=-1)
    grp = jnp.argmax(l1, axis=-1)
    p_grp = p1[rows, grp][:, None]
    l2_all = jnp.einsum('td,gde->tge', tf, w_r2.astype(F32)) + b_r2.astype(F32)
    l2 = l2_all[rows, grp]
    top_v, top_i = lax.top_k(l2, TOP_K_INNER)
    w_sel = jax.nn.softmax(top_v, axis=-1) * p_grp
    eid = grp[:, None] * N_EPG + top_i
    gates = jnp.sum(jax.nn.one_hot(eid, N_EXPERTS, dtype=F32) * w_sel[..., None], axis=1)
    a = jnp.einsum('td,edf->tef', t, w_gate)
    u = jnp.einsum('td,edf->tef', t, w_up)
    hid = jax.nn.silu(a) * u * gates[:, :, None].astype(t.dtype)
    y = jnp.einsum('tef,efd->td', hid, w_down)
    return y.reshape(bsz, n_tok, dm)


def _layer(x, cond, h0_f, h0_b, alpha, lp):
    n_tok = x.shape[1]
    mod = (cond @ lp['w_ada'].astype(F32) + lp['b_ada'].astype(F32))[:, None, :]
    sh1, sc1, g1, sh2, sc2, g2 = jnp.split(mod, 6, axis=-1)
    h = (_norm(x) * (1.0 + sc1) + sh1).astype(x.dtype)
    proj = h @ lp['w_in']
    kf = _hyena_filter_spectrum(n_tok, lp['hy_f_w1'], lp['hy_f_b1'], lp['hy_f_w2'],
                                lp['hy_f_b2'], lp['hy_f_w3'], lp['hy_freq'])
    y_hy = _hyena(proj[..., :3 * D_HY], lp['hy_conv_w'], lp['hy_conv_b'], kf, lp['hy_fbias'])
    y_s5, h_f, h_b = _s5(proj[..., 3 * D_HY:], h0_f, h0_b, lp['s5_a_re'], lp['s5_a_im'],
                         lp['s5_log_dt'], lp['s5_b_re'], lp['s5_b_im'], lp['s5_c_re'],
                         lp['s5_c_im'], lp['s5_d'], lp['s5_w_glu'], lp['s5_b_glu'])
    mixed = jnp.concatenate([_rms(y_hy), _rms(y_s5)], axis=-1) * lp['out_norm_g'].astype(F32)
    o = mixed.astype(x.dtype) @ lp['w_out']
    x = _ln_affine(alpha * x.astype(F32) + g1 * o.astype(F32), lp['ln1_g'], lp['ln1_b']).astype(x.dtype)
    h = (_norm(x) * (1.0 + sc2) + sh2).astype(x.dtype)
    f = _hier_moe(h, lp['moe_w_r1'], lp['moe_b_r1'], lp['moe_w_r2'], lp['moe_b_r2'],
                  lp['moe_w_gate'], lp['moe_w_up'], lp['moe_w_down'])
    x = _ln_affine(alpha * x.astype(F32) + g2 * f.astype(F32), lp['ln2_g'], lp['ln2_b']).astype(x.dtype)
    return x, h_f, h_b


def setup_inputs(seed: int = 0) -> dict:
    key = jax.random.key(seed)
    ks = iter(jax.random.split(key, 48))

    def nrm(shape, scale):
        return scale * jax.random.normal(next(ks), shape, F32)

    beta = (8.0 * DEPTH) ** -0.25
    n_idx = jnp.arange(S5_STATE, dtype=F32)
    return {
        'x_prompt': nrm((BATCH, SEQ, D_MODEL), 1.0),
        'x_sample': nrm((DEC_BATCH, DEC_SEQ, D_MODEL), 1.0),
        'state_s5_re': nrm((DEC_BATCH, DEPTH, 2, S5_GROUPS, S5_STATE), 0.3),
        'state_s5_im': nrm((DEC_BATCH, DEPTH, 2, S5_GROUPS, S5_STATE), 0.3),
        'c': nrm((DEC_BATCH, D_MODEL), 1.0),
        'c_ctx': nrm((D_MODEL,), 1.0),
        'w_ada': nrm((DEPTH, D_MODEL, 6 * D_MODEL), D_MODEL ** -0.5),
        'b_ada': nrm((DEPTH, 6 * D_MODEL), 0.02),
        'w_in': nrm((DEPTH, D_MODEL, 3 * D_HY + D_S5), D_MODEL ** -0.5),
        'hy_conv_w': nrm((DEPTH, HY_SHORT, 3 * D_HY), HY_SHORT ** -0.5),
        'hy_conv_b': nrm((DEPTH, 3 * D_HY), 0.01),
        'hy_f_w1': nrm((DEPTH, HY_EMB, HY_HID), HY_EMB ** -0.5),
        'hy_f_b1': nrm((DEPTH, HY_HID), 0.02),
        'hy_f_w2': nrm((DEPTH, HY_HID, HY_HID), HY_HID ** -0.5),
        'hy_f_b2': nrm((DEPTH, HY_HID), 0.02),
        'hy_f_w3': nrm((DEPTH, HY_HID, HY_ORDER * 2 * D_HY), 0.3 * HY_HID ** -0.5),
        'hy_freq': 1.0 + nrm((DEPTH, HY_HID), 0.02),
        'hy_fbias': nrm((DEPTH, HY_ORDER, D_HY), 1.0),
        's5_a_re': -0.5 + nrm((DEPTH, 2, S5_GROUPS, S5_STATE), 0.01),
        's5_a_im': math.pi * n_idx + nrm((DEPTH, 2, S5_GROUPS, S5_STATE), 0.01),
        's5_log_dt': jax.random.uniform(next(ks), (DEPTH, 2, S5_GROUPS), F32,
                                        minval=math.log(1e-3), maxval=math.log(1e-1)),
        's5_b_re': nrm((DEPTH, 2, S5_GROUPS, S5_STATE, S5_CH), (2.0 * S5_CH) ** -0.5),
        's5_b_im': nrm((DEPTH, 2, S5_GROUPS, S5_STATE, S5_CH), (2.0 * S5_CH) ** -0.5),
        's5_c_re': nrm((DEPTH, S5_GROUPS, S5_CH, S5_STATE), (2.0 * S5_STATE) ** -0.5),
        's5_c_im': nrm((DEPTH, S5_GROUPS, S5_CH, S5_STATE), (2.0 * S5_STATE) ** -0.5),
        's5_d': nrm((DEPTH, D_S5), 1.0),
        's5_w_glu': nrm((DEPTH, D_S5, D_S5), D_S5 ** -0.5),
        's5_b_glu': nrm((DEPTH, D_S5), 0.02),
        'out_norm_g': 1.0 + nrm((DEPTH, D_MIX), 0.02),
        'w_out': nrm((DEPTH, D_MIX, D_MODEL), beta * D_MIX ** -0.5),
        'ln1_g': 1.0 + nrm((DEPTH, D_MODEL), 0.02),
        'ln1_b': nrm((DEPTH, D_MODEL), 0.02),
        'moe_w_r1': nrm((DEPTH, D_MODEL, N_EGROUPS), D_MODEL ** -0.5),
        'moe_b_r1': nrm((DEPTH, N_EGROUPS), 0.01),
        'moe_w_r2': nrm((DEPTH, N_EGROUPS, D_MODEL, N_EPG), D_MODEL ** -0.5),
        'moe_b_r2': nrm((DEPTH, N_EGROUPS, N_EPG), 0.01),
        'moe_w_gate': nrm((DEPTH, N_EXPERTS, D_MODEL, D_EXPERT), D_MODEL ** -0.5),
        'moe_w_up': nrm((DEPTH, N_EXPERTS, D_MODEL, D_EXPERT), D_MODEL ** -0.5),
        'moe_w_down': nrm((DEPTH, N_EXPERTS, D_EXPERT, D_MODEL), beta * D_EXPERT ** -0.5),
        'ln2_g': 1.0 + nrm((DEPTH, D_MODEL), 0.02),
        'ln2_b': nrm((DEPTH, D_MODEL), 0.02),
    }


def reference(x_prompt, x_sample, state_s5_re, state_s5_im, c, c_ctx, w_ada, b_ada, w_in,
              hy_conv_w, hy_conv_b, hy_f_w1, hy_f_b1, hy_f_w2, hy_f_b2, hy_f_w3, hy_freq,
              hy_fbias, s5_a_re, s5_a_im, s5_log_dt, s5_b_re, s5_b_im, s5_c_re, s5_c_im, s5_d,
              s5_w_glu, s5_b_glu, out_norm_g, w_out, ln1_g, ln1_b, moe_w_r1, moe_b_r1,
              moe_w_r2, moe_b_r2, moe_w_gate, moe_w_up, moe_w_down, ln2_g, ln2_b):
    alpha = (2.0 * DEPTH) ** 0.25
    xp = x_prompt
    xs = (x_sample.astype(F32) + _grid_pos_embed(x_sample.shape[1])).astype(x_sample.dtype)
    ctx_cond = jax.nn.silu(c_ctx.astype(F32))[None, :]
    lat_cond = jax.nn.silu(c.astype(F32))
    new_re = []
    new_im = []
    for l in range(DEPTH):
        lp = {
            'w_ada': w_ada[l], 'b_ada': b_ada[l], 'w_in': w_in[l],
            'hy_conv_w': hy_conv_w[l], 'hy_conv_b': hy_conv_b[l],
            'hy_f_w1': hy_f_w1[l], 'hy_f_b1': hy_f_b1[l], 'hy_f_w2': hy_f_w2[l],
            'hy_f_b2': hy_f_b2[l], 'hy_f_w3': hy_f_w3[l], 'hy_freq': hy_freq[l],
            'hy_fbias': hy_fbias[l],
            's5_a_re': s5_a_re[l], 's5_a_im': s5_a_im[l], 's5_log_dt': s5_log_dt[l],
            's5_b_re': s5_b_re[l], 's5_b_im': s5_b_im[l], 's5_c_re': s5_c_re[l],
            's5_c_im': s5_c_im[l], 's5_d': s5_d[l], 's5_w_glu': s5_w_glu[l],
            's5_b_glu': s5_b_glu[l], 'out_norm_g': out_norm_g[l], 'w_out': w_out[l],
            'ln1_g': ln1_g[l], 'ln1_b': ln1_b[l],
            'moe_w_r1': moe_w_r1[l], 'moe_b_r1': moe_b_r1[l], 'moe_w_r2': moe_w_r2[l],
            'moe_b_r2': moe_b_r2[l], 'moe_w_gate': moe_w_gate[l], 'moe_w_up': moe_w_up[l],
            'moe_w_down': moe_w_down[l], 'ln2_g': ln2_g[l], 'ln2_b': ln2_b[l],
        }
        zero = jnp.zeros((xp.shape[0], S5_GROUPS, S5_STATE), jnp.complex64)
        xp, hf_ctx, hb_ctx = _layer(xp, ctx_cond, zero, zero, alpha, lp)
        st = jnp.stack([hf_ctx, hb_ctx], axis=1)
        new_re.append(st.real)
        new_im.append(st.imag)
        h0f = lax.complex(state_s5_re[:, l, 0].astype(F32), state_s5_im[:, l, 0].astype(F32))
        h0b = lax.complex(state_s5_re[:, l, 1].astype(F32), state_s5_im[:, l, 1].astype(F32))
        xs, _, _ = _layer(xs, lat_cond, h0f, h0b, alpha, lp)
    new_state_s5_re = jnp.stack(new_re, axis=1)
    new_state_s5_im = jnp.stack(new_im, axis=1)
    return (xp, xs, new_state_s5_re, new_state_s5_im)
```

```python
import functools
import math

import numpy as np
import jax
import jax.numpy as jnp
from jax import lax
from jax.experimental import pallas as pl
from jax.experimental.pallas import tpu as pltpu

F32 = jnp.float32
BF16 = jnp.bfloat16

D_MODEL = 1024
DEPTH = 1
GRID_W = 64
POS_BASE = 10000.0
D_HY = 512
D_S5 = 512
S5_CH = 16
S5_GROUPS = 32
S5_STATE = 64
S5_CHUNK = 16
S5_ROW = S5_CHUNK * S5_CH
HY_BANDS = 16
HY_EMB = 1 + 2 * HY_BANDS
HY_HID = 64
HY_MIN_DECAY = math.log(1e-2) / 1.5
HY_MAX_DECAY = math.log(1e-2) / 0.3
N_EGROUPS = 4
N_EPG = 4
N_EXPERTS = 16
D_EXPERT = 512
LN_EPS = 1e-5
ALPHA = (2.0 * DEPTH) ** 0.25
LANES = 128
HY_CW = 256
MOE_TM = 1024
VMEM_LIMIT = 60000 * 1024


def _cparams(sem):
    return pltpu.CompilerParams(dimension_semantics=sem, vmem_limit_bytes=VMEM_LIMIT)


def _split(x):
    hi = x.astype(BF16)
    lo = (x - hi.astype(F32)).astype(BF16)
    return hi, lo


def _dot(a, b):
    return jnp.dot(a, b, preferred_element_type=F32)


def _dot_t(a, b):
    return lax.dot_general(a, b, (((1,), (1,)), ((), ())), preferred_element_type=F32)


def _mm3(a, b):
    ah, al = _split(a)
    bh, bl = _split(b)
    return _dot(ah, bh) + _dot(al, bh) + _dot(ah, bl)


def _mm3_pre(ah, al, b):
    bh, bl = _split(b)
    return _dot(ah, bh) + _dot(al, bh) + _dot(ah, bl)


def _mm3_t(a, b):
    ah, al = _split(a)
    bh, bl = _split(b)
    return _dot_t(ah, bh) + _dot_t(al, bh) + _dot_t(ah, bl)


def _dot_hp(a, b):
    return jnp.dot(a, b, preferred_element_type=F32, precision=lax.Precision.HIGHEST)


def _norm(x):
    xc = x - jnp.mean(x, axis=-1, keepdims=True)
    return xc * lax.rsqrt(jnp.mean(xc * xc, axis=-1, keepdims=True) + LN_EPS)


def _rms(y):
    return y * lax.rsqrt(jnp.mean(y * y, axis=-1, keepdims=True) + LN_EPS)


def _ada_kernel(cond_ref, w_ref, b_ref, o_ref):
    c = jax.nn.silu(cond_ref[...])
    o_ref[...] = _mm3(c, w_ref[...]) + b_ref[...]


def _ada(cond, w_ada, b_ada):
    nb = cond.shape[0]
    n = w_ada.shape[1]
    tn = 1024
    return pl.pallas_call(
        _ada_kernel,
        grid=(n // tn,),
        in_specs=[pl.BlockSpec((nb, D_MODEL), lambda j: (0, 0)),
                  pl.BlockSpec((D_MODEL, tn), lambda j: (0, j)),
                  pl.BlockSpec((1, tn), lambda j: (0, j))],
        out_specs=pl.BlockSpec((nb, tn), lambda j: (0, j)),
        out_shape=jax.ShapeDtypeStruct((nb, n), F32),
        compiler_params=_cparams(("arbitrary",)),
        name="ada",
    )(cond, w_ada, b_ada.reshape(1, n))


def _dft_tables(n_tok):
    n = 2 * n_tok
    idx = np.arange(n_tok, dtype=np.int64)
    m = (idx[:, None] * idx[None, :]) % n
    ang = 2.0 * np.pi * m.astype(np.float64) / n
    cm = np.cos(ang)
    sm = -np.sin(ang)
    sm[0, :] = 1.0 - 2.0 * (idx % 2)
    return cm.astype(np.float32), sm.astype(np.float32)


def _filt_kernel(n_tok, z_ref, t_ref, w1_ref, b1_ref, w2_ref, b2_ref, fr_ref, w3f_ref, w3b_ref,
                 dl_ref, cmh_ref, cml_ref, smh_ref, sml_ref, a_ref, bz_ref, dd_ref):
    fr = fr_ref[...]
    h = jnp.sin(fr * (_dot_hp(z_ref[...], w1_ref[...]) + b1_ref[...]))
    h = jnp.sin(fr * (_dot_hp(h, w2_ref[...]) + b2_ref[...]))
    decay = jnp.exp(-t_ref[...] * dl_ref[...])
    row = lax.broadcasted_iota(jnp.int32, decay.shape, 0)
    hf = _dot_hp(h, w3f_ref[...]) * decay
    hb = jnp.where(row == 0, 0.0, _dot_hp(h, w3b_ref[...]) * decay)
    p = hf + hb
    q = hf - hb
    k_re = _mm3_pre(cmh_ref[...], cml_ref[...], p)
    k_im = _mm3_pre(smh_ref[...], sml_ref[...], q)
    sign = jnp.where(row % 2 == 0, 1.0, -1.0)
    nyq = jnp.sum(p * sign, axis=0, keepdims=True)
    inv_n = 1.0 / (2 * n_tok)
    w = jnp.where(row == 0, inv_n, 2.0 * inv_n)
    a_ref[...] = w * k_re
    bz_ref[...] = jnp.where(row == 0, 0.0, w * k_im)
    dd_ref[...] = jnp.where(row == 0, nyq * inv_n, w * k_re)


def _hyena_filters(n_tok, tabs, hy_f_w1, hy_f_b1, hy_f_w2, hy_f_b2, hy_f_w3, hy_freq):
    cmh, cml, smh, sml = tabs[:4]
    t = jnp.linspace(0.0, 1.0, n_tok, dtype=F32)[:, None]
    wv = 2.0 * math.pi * jnp.arange(n_tok, dtype=F32) / n_tok
    fb = jnp.linspace(1e-4, HY_BANDS - 1, HY_BANDS, dtype=F32)
    ang = wv[:, None] * fb[None, :]
    z = jnp.concatenate([t, jnp.cos(ang), -jnp.sin(ang)], axis=-1)
    z = jnp.pad(z, ((0, 0), (0, LANES - HY_EMB)))
    w1 = jnp.pad(hy_f_w1, ((0, LANES - HY_EMB), (0, 0)))
    deltas = jnp.abs(jnp.linspace(HY_MIN_DECAY, HY_MAX_DECAY, D_HY, dtype=F32))[None, :]
    ncb = D_HY // HY_CW
    full = lambda j: (0, 0)
    out_sd = jax.ShapeDtypeStruct((n_tok, 2 * D_HY), F32)
    mat = pl.BlockSpec((n_tok, n_tok), full, pipeline_mode=pl.Buffered(1))
    return pl.pallas_call(
        functools.partial(_filt_kernel, n_tok),
        grid=(2 * ncb,),
        in_specs=[pl.BlockSpec((n_tok, LANES), full),
                  pl.BlockSpec((n_tok, 1), full),
                  pl.BlockSpec((LANES, HY_HID), full),
                  pl.BlockSpec((1, HY_HID), full),
                  pl.BlockSpec((HY_HID, HY_HID), full),
                  pl.BlockSpec((1, HY_HID), full),
                  pl.BlockSpec((1, HY_HID), full),
                  pl.BlockSpec((HY_HID, HY_CW), lambda j: (0, 2 * ncb * (j // ncb) + j % ncb)),
                  pl.BlockSpec((HY_HID, HY_CW), lambda j: (0, 2 * ncb * (j // ncb) + ncb + j % ncb)),
                  pl.BlockSpec((1, HY_CW), lambda j: (0, j % ncb)),
                  mat, mat, mat, mat],
        out_specs=[pl.BlockSpec((n_tok, HY_CW), lambda j: (0, j))] * 3,
        out_shape=[out_sd] * 3,
        compiler_params=_cparams(("arbitrary",)),
        name=f"filt{n_tok}",
    )(z, t, w1, hy_f_b1.reshape(1, -1), hy_f_w2, hy_f_b2.reshape(1, -1), hy_freq.reshape(1, -1),
      hy_f_w3, hy_f_w3, deltas, cmh, cml, smh, sml)


def _hyena_kernel(pv_ref, p1_ref, p2_ref, cwv_ref, cw1_ref, cw2_ref, cbv_ref, cb1_ref, cb2_ref,
                  fbias_ref, cmh_ref, cml_ref, smh_ref, sml_ref, sth_ref, stl_ref,
                  a0_ref, b0_ref, d0_ref, a1_ref, b1_ref, d1_ref, o_ref):
    n_tok = pv_ref.shape[1]
    row = lax.broadcasted_iota(jnp.int32, (n_tok, pv_ref.shape[2]), 0)

    def short_conv(p_ref, cw_ref, cb_ref):
        p = p_ref[0]
        prev = jnp.where(row == 0, 0.0, pltpu.roll(p, 1, axis=0))
        nxt = jnp.where(row == n_tok - 1, 0.0, pltpu.roll(p, n_tok - 1, axis=0))
        return cb_ref[...] + prev * cw_ref[0:1, :] + p * cw_ref[1:2, :] + nxt * cw_ref[2:3, :]

    cmh, cml = cmh_ref[...], cml_ref[...]

    def fftconv(u, a_ref, b_ref, d_ref, skip):
        u_re = _mm3_pre(cmh, cml, u)
        u_im = _mm3_pre(smh_ref[...], sml_ref[...], u)
        a, bz, dd = a_ref[...], b_ref[...], d_ref[...]
        y_re = u_re * a - u_im * bz
        y_im = u_re * bz + u_im * dd
        y = _mm3_pre(cmh, cml, y_re) + _mm3_pre(sth_ref[...], stl_ref[...], y_im)
        return y + u * skip

    v = short_conv(pv_ref, cwv_ref, cbv_ref)
    x1 = short_conv(p1_ref, cw1_ref, cb1_ref)
    z = x1 * fftconv(v, a0_ref, b0_ref, d0_ref, fbias_ref[0:1, :])
    x2 = short_conv(p2_ref, cw2_ref, cb2_ref)
    o_ref[0] = x2 * fftconv(z, a1_ref, b1_ref, d1_ref, fbias_ref[1:2, :])


def _hyena(proj_hy, tabs, filt, hy_conv_w, hy_conv_b, hy_fbias):
    bsz, n_tok, _ = proj_hy.shape
    ncb = D_HY // HY_CW
    cmh, cml, smh, sml, sth, stl = tabs
    fa, fbz, fdd = filt
    cb = hy_conv_b.reshape(1, -1)
    mat = pl.BlockSpec((n_tok, n_tok), lambda b, c: (0, 0), pipeline_mode=pl.Buffered(1))

    def pspec(k):
        return pl.BlockSpec((1, n_tok, HY_CW), lambda b, c: (b, 0, k * ncb + c))

    def cwspec(k):
        return pl.BlockSpec((3, HY_CW), lambda b, c: (0, k * ncb + c))

    def cbspec(k):
        return pl.BlockSpec((1, HY_CW), lambda b, c: (0, k * ncb + c))

    def fspec(o):
        return pl.BlockSpec((n_tok, HY_CW), lambda b, c: (0, o * ncb + c))

    return pl.pallas_call(
        _hyena_kernel,
        grid=(bsz, ncb),
        in_specs=[pspec(0), pspec(1), pspec(2), cwspec(0), cwspec(1), cwspec(2),
                  cbspec(0), cbspec(1), cbspec(2),
                  pl.BlockSpec((2, HY_CW), lambda b, c: (0, c)),
                  mat, mat, mat, mat, mat, mat,
                  fspec(0), fspec(0), fspec(0), fspec(1), fspec(1), fspec(1)],
        out_specs=pl.BlockSpec((1, n_tok, HY_CW), lambda b, c: (b, 0, c)),
        out_shape=jax.ShapeDtypeStruct((bsz, n_tok, D_HY), F32),
        compiler_params=_cparams(("arbitrary", "arbitrary")),
        name=f"hyena{n_tok}",
    )(proj_hy, proj_hy, proj_hy, hy_conv_w, hy_conv_w, hy_conv_w, cb, cb, cb, hy_fbias,
      cmh, cml, smh, sml, sth, stl, fa, fbz, fdd, fa, fbz, fdd)


def _s5ops_kernel(are_ref, aim_ref, ldt_ref, btr_ref, bti_ref, cre_ref, cim_ref,
                  er_ref, ei_ref, gr_ref, gi_ref, kf_ref, kb_ref, atr_ref, ati_ref):
    a_re, a_im = are_ref[0], aim_ref[0]
    dt = jnp.exp(ldt_ref[0])
    mag = jnp.exp(a_re * dt)
    ab_re = mag * jnp.cos(a_im * dt)
    ab_im = mag * jnp.sin(a_im * dt)
    n_re, n_im = ab_re - 1.0, ab_im
    den = a_re * a_re + a_im * a_im
    q_re = (n_re * a_re + n_im * a_im) / den
    q_im = (n_im * a_re - n_re * a_im) / den
    bt_re, bt_im = btr_ref[0], bti_ref[0]
    bb_re = q_re * bt_re - q_im * bt_im
    bb_im = q_re * bt_im + q_im * bt_re
    c_re, c_im = cre_ref[0], cim_ref[0]
    pw = [(jnp.ones_like(ab_re), jnp.zeros_like(ab_re))]
    for _ in range(S5_CHUNK):
        pr, pi = pw[-1]
        pw.append((pr * ab_re - pi * ab_im, pr * ab_im + pi * ab_re))
    lane = lax.broadcasted_iota(jnp.int32, ab_re.shape, 1)
    fwd = lane < S5_STATE
    for s in range(S5_CHUNK):
        e_re = jnp.where(fwd, pw[S5_CHUNK - 1 - s][0], pw[s][0])
        e_im = jnp.where(fwd, pw[S5_CHUNK - 1 - s][1], pw[s][1])
        er_ref[0, pl.ds(S5_CH * s, S5_CH), :] = e_re * bb_re - e_im * bb_im
        ei_ref[0, pl.ds(S5_CH * s, S5_CH), :] = e_re * bb_im + e_im * bb_re
        g_re = jnp.where(fwd, pw[s + 1][0], pw[S5_CHUNK - s][0])
        g_im = jnp.where(fwd, pw[s + 1][1], pw[S5_CHUNK - s][1])
        gr_ref[0, pl.ds(S5_CH * s, S5_CH), :] = c_re * g_re - c_im * g_im
        gi_ref[0, pl.ds(S5_CH * s, S5_CH), :] = -(c_re * g_im + c_im * g_re)
    er, ei = er_ref[0], ei_ref[0]
    lane2 = lax.broadcasted_iota(jnp.int32, er.shape, 1)
    f2 = lane2 < S5_STATE
    zero = jnp.zeros_like(er)

    def dot_hp_t(a, b):
        return lax.dot_general(a, b, (((1,), (1,)), ((), ())), preferred_element_type=F32,
                               precision=lax.Precision.HIGHEST)

    kf_ref[0] = dot_hp_t(jnp.where(f2, er, zero), c_re) - dot_hp_t(jnp.where(f2, ei, zero), c_im)
    kb_ref[0] = dot_hp_t(jnp.where(f2, zero, er), c_re) - dot_hp_t(jnp.where(f2, zero, ei), c_im)
    atr_ref[0] = pw[S5_CHUNK][0]
    ati_ref[0] = pw[S5_CHUNK][1]


def _s5_operators(s5_a_re, s5_a_im, s5_log_dt, s5_b_re, s5_b_im, s5_c_re, s5_c_im, s5_d):
    g, p, h = S5_GROUPS, S5_STATE, S5_CH
    cat = lambda x: jnp.concatenate([x[0], x[1]], axis=-1)
    a_re = cat(s5_a_re).reshape(g, 1, 2 * p)
    a_im = cat(s5_a_im).reshape(g, 1, 2 * p)
    ldt = cat(jnp.broadcast_to(s5_log_dt[:, :, None], (2, g, p))).reshape(g, 1, 2 * p)
    bt_re = cat(jnp.swapaxes(s5_b_re, -1, -2))
    bt_im = cat(jnp.swapaxes(s5_b_im, -1, -2))
    c_re = jnp.concatenate([s5_c_re, s5_c_re], axis=-1)
    c_im = jnp.concatenate([s5_c_im, s5_c_im], axis=-1)
    vec = pl.BlockSpec((1, 1, 2 * p), lambda i: (i, 0, 0))
    hp = pl.BlockSpec((1, h, 2 * p), lambda i: (i, 0, 0))
    big = pl.BlockSpec((1, S5_ROW, 2 * p), lambda i: (i, 0, 0))
    kk = pl.BlockSpec((1, S5_ROW, h), lambda i: (i, 0, 0))
    big_sd = jax.ShapeDtypeStruct((g, S5_ROW, 2 * p), F32)
    kk_sd = jax.ShapeDtypeStruct((g, S5_ROW, h), F32)
    vec_sd = jax.ShapeDtypeStruct((g, 1, 2 * p), F32)
    er, ei, gr, gi, kf, kb, at_re, at_im = pl.pallas_call(
        _s5ops_kernel,
        grid=(g,),
        in_specs=[vec, vec, vec, hp, hp, hp, hp],
        out_specs=[big, big, big, big, kk, kk, vec, vec],
        out_shape=[big_sd, big_sd, big_sd, big_sd, kk_sd, kk_sd, vec_sd, vec_sd],
        compiler_params=_cparams(("arbitrary",)),
        name="s5ops",
    )(a_re, a_im, ldt, bt_re, bt_im, c_re, c_im)
    kf_lag = kf.reshape(g, S5_CHUNK, h, h)[:, ::-1]
    kb_lag = kb.reshape(g, S5_CHUNK, h, h)
    lag = np.arange(S5_CHUNK)[None, :] - np.arange(S5_CHUNK)[:, None]
    mf = kf_lag[:, np.clip(lag, 0, S5_CHUNK - 1)] * jnp.asarray(lag >= 0, F32)[None, :, :, None, None]
    mb = kb_lag[:, np.clip(-lag, 0, S5_CHUNK - 1)] * jnp.asarray(lag <= 0, F32)[None, :, :, None, None]
    mt = (mf + mb).transpose(0, 1, 3, 2, 4).reshape(g, S5_ROW, S5_ROW)
    dvec = jnp.tile(s5_d.reshape(g, 1, h), (1, S5_CHUNK, 1)).reshape(g, S5_ROW)
    mt = mt + jnp.eye(S5_ROW, dtype=F32)[None] * dvec[:, None, :]
    return mt, er, ei, gr, gi, at_re, at_im


def _s5_kernel(bsz, n_chunks, u_ref, mt_ref, er_ref, ei_ref, gr_ref, gi_ref, atr_ref, ati_ref,
               h0r_ref, h0i_ref, y_ref, fr_ref, fi_ref, sr_ref, si_ref, xr_ref, xi_ref):
    uh, ul = _split(u_ref[0])
    sr_ref[...] = _mm3_pre(uh, ul, er_ref[0])
    si_ref[...] = _mm3_pre(uh, ul, ei_ref[0])
    at_re, at_im = atr_ref[0], ati_ref[0]
    lane = lax.broadcasted_iota(jnp.int32, (bsz, 2 * S5_STATE), 1)
    fwd = lane < S5_STATE

    def step(i, carry):
        x_re, x_im = carry
        rf = pl.multiple_of(i * bsz, bsz)
        rb = pl.multiple_of((n_chunks - 1 - i) * bsz, bsz)
        xr_ref[pl.ds(rf, bsz), 0:S5_STATE] = x_re[:, 0:S5_STATE]
        xi_ref[pl.ds(rf, bsz), 0:S5_STATE] = x_im[:, 0:S5_STATE]
        xr_ref[pl.ds(rb, bsz), S5_STATE:] = x_re[:, S5_STATE:]
        xi_ref[pl.ds(rb, bsz), S5_STATE:] = x_im[:, S5_STATE:]
        s_re = jnp.where(fwd, sr_ref[pl.ds(rf, bsz), :], sr_ref[pl.ds(rb, bsz), :])
        s_im = jnp.where(fwd, si_ref[pl.ds(rf, bsz), :], si_ref[pl.ds(rb, bsz), :])
        return (at_re * x_re - at_im * x_im + s_re, at_re * x_im + at_im * x_re + s_im)

    x_re, x_im = lax.fori_loop(0, n_chunks, step, (h0r_ref[0], h0i_ref[0]))
    fr_ref[0] = x_re
    fi_ref[0] = x_im
    y_ref[0] = (_mm3_pre(uh, ul, mt_ref[0]) + _mm3_t(xr_ref[...], gr_ref[0])
                + _mm3_t(xi_ref[...], gi_ref[0]))


def _s5(u, ops, h0_re, h0_im):
    bsz, n_tok, _ = u.shape
    mt, er, ei, gr, gi, at_re, at_im = ops
    g, p, h = S5_GROUPS, S5_STATE, S5_CH
    nc = n_tok // S5_CHUNK
    rows = nc * bsz
    ug = u.reshape(bsz, nc, S5_CHUNK, g, h).transpose(3, 1, 0, 2, 4).reshape(g, rows, S5_ROW)
    gspec = lambda shape: pl.BlockSpec((1,) + shape, lambda i: (i, 0, 0))
    y, f_re, f_im = pl.pallas_call(
        functools.partial(_s5_kernel, bsz, nc),
        grid=(g,),
        in_specs=[gspec((rows, S5_ROW)), gspec((S5_ROW, S5_ROW)), gspec((S5_ROW, 2 * p)),
                  gspec((S5_ROW, 2 * p)), gspec((S5_ROW, 2 * p)), gspec((S5_ROW, 2 * p)),
                  gspec((1, 2 * p)), gspec((1, 2 * p)), gspec((bsz, 2 * p)), gspec((bsz, 2 * p))],
        out_specs=[gspec((rows, S5_ROW)), gspec((bsz, 2 * p)), gspec((bsz, 2 * p))],
        out_shape=[jax.ShapeDtypeStruct((g, rows, S5_ROW), F32),
                   jax.ShapeDtypeStruct((g, bsz, 2 * p), F32),
                   jax.ShapeDtypeStruct((g, bsz, 2 * p), F32)],
        scratch_shapes=[pltpu.VMEM((rows, 2 * p), F32)] * 4,
        compiler_params=_cparams(("arbitrary",)),
        name=f"s5_{n_tok}",
    )(ug, mt, er, ei, gr, gi, at_re, at_im, h0_re, h0_im)
    y = y.reshape(g, nc, bsz, S5_CHUNK, h).transpose(2, 1, 3, 0, 4).reshape(bsz, n_tok, D_S5)
    return y, f_re, f_im


def _in_kernel(has_pos, *refs):
    if has_pos:
        x_ref, pos_ref, mod_ref, w_ref, hy_ref, s5_ref = refs
        x = x_ref[0] + pos_ref[...]
    else:
        x_ref, mod_ref, w_ref, hy_ref, s5_ref = refs
        x = x_ref[0]
    sh1 = mod_ref[0, :, 0:D_MODEL]
    sc1 = mod_ref[0, :, D_MODEL:2 * D_MODEL]
    h = _norm(x) * (1.0 + sc1) + sh1
    proj = _dot(h.astype(BF16), w_ref[...])
    hy_ref[0] = proj[:, :3 * D_HY]
    s5_ref[0] = proj[:, 3 * D_HY:]


def _in_proj(x3, pos, mod3, w_in_bf, tm):
    nb, lt, _ = x3.shape
    has_pos = pos is not None
    per_batch = mod3.shape[0] > 1
    midx = (lambda b, i: (b, 0, 0)) if per_batch else (lambda b, i: (0, 0, 0))
    in_specs = [pl.BlockSpec((1, tm, D_MODEL), lambda b, i: (b, i, 0))]
    args = [x3]
    if has_pos:
        in_specs.append(pl.BlockSpec((tm, D_MODEL), lambda b, i: (i, 0)))
        args.append(pos)
    in_specs += [pl.BlockSpec((1, 1, 6 * D_MODEL), midx),
                 pl.BlockSpec((D_MODEL, 3 * D_HY + D_S5), lambda b, i: (0, 0))]
    args += [mod3, w_in_bf]
    return pl.pallas_call(
        functools.partial(_in_kernel, has_pos),
        grid=(nb, lt // tm),
        in_specs=in_specs,
        out_specs=[pl.BlockSpec((1, tm, 3 * D_HY), lambda b, i: (b, i, 0)),
                   pl.BlockSpec((1, tm, D_S5), lambda b, i: (b, i, 0))],
        out_shape=[jax.ShapeDtypeStruct((nb, lt, 3 * D_HY), F32),
                   jax.ShapeDtypeStruct((nb, lt, D_S5), F32)],
        compiler_params=_cparams(("arbitrary", "arbitrary")),
        name=f"in_proj{nb}",
    )(*args)


def _route(logits):
    lane = lax.broadcasted_iota(jnp.int32, logits.shape, 1)
    lane_f = lane.astype(F32)
    neg = -jnp.inf
    big = float(LANES)
    m1 = (lane >= N_EXPERTS) & (lane < N_EXPERTS + N_EGROUPS)
    l1 = jnp.where(m1, logits, neg)
    top1 = jnp.max(l1, axis=-1, keepdims=True)
    grp = jnp.min(jnp.where(l1 == top1, lane_f, big), axis=-1, keepdims=True) - float(N_EXPERTS)
    den = jnp.sum(jnp.where(m1, jnp.exp(logits - top1), 0.0), axis=-1, keepdims=True)
    p_grp = 1.0 / den
    lo = grp * float(N_EPG)
    m2 = (lane_f >= lo) & (lane_f < lo + float(N_EPG))
    l2 = jnp.where(m2, logits, neg)
    v1 = jnp.max(l2, axis=-1, keepdims=True)
    i1 = jnp.min(jnp.where(l2 == v1, lane_f, big), axis=-1, keepdims=True)
    l2b = jnp.where(lane_f == i1, neg, l2)
    v2 = jnp.max(l2b, axis=-1, keepdims=True)
    i2 = jnp.min(jnp.where(l2b == v2, lane_f, big), axis=-1, keepdims=True)
    e = jnp.exp(v2 - v1)
    w1 = 1.0 / (1.0 + e)
    w2 = e / (1.0 + e)
    return (jnp.where(lane_f == i1, w1 * p_grp, 0.0) + jnp.where(lane_f == i2, w2 * p_grp, 0.0))


def _out_kernel(has_pos, *refs):
    if has_pos:
        x_ref, pos_ref = refs[:2]
        rest = refs[2:]
        x = x_ref[0] + pos_ref[...]
    else:
        x_ref = refs[0]
        rest = refs[1:]
        x = x_ref[0]
    (yhy_ref, ys5_ref, mod_ref, wglu_ref, bglu_ref, ong_ref, wout_ref, ln1g_ref, ln1b_ref,
     wrh_ref, wrl_ref, br_ref, x1_ref, h2_ref, gate_ref) = rest
    y = ys5_ref[0]
    s5 = jax.nn.gelu(y) * jax.nn.sigmoid(_dot(y.astype(BF16), wglu_ref[...]) + bglu_ref[...])
    m_hy = _rms(yhy_ref[0]) * ong_ref[:, 0:D_HY]
    m_s5 = _rms(s5) * ong_ref[:, D_HY:]
    o = (_dot(m_hy.astype(BF16), wout_ref[0:D_HY, :]) + _dot(m_s5.astype(BF16), wout_ref[D_HY:, :]))
    g1 = mod_ref[0, :, 2 * D_MODEL:3 * D_MODEL]
    sh2 = mod_ref[0, :, 3 * D_MODEL:4 * D_MODEL]
    sc2 = mod_ref[0, :, 4 * D_MODEL:5 * D_MODEL]
    x1 = _norm(ALPHA * x + g1 * o) * ln1g_ref[...] + ln1b_ref[...]
    x1_ref[0] = x1
    h2 = _norm(x1) * (1.0 + sc2) + sh2
    h2_ref[0] = h2.astype(BF16)
    hh, hl = _split(h2)
    logits = (_dot(hh, wrh_ref[...]) + _dot(hl, wrh_ref[...]) + _dot(hh, wrl_ref[...]) + br_ref[...])
    gate_ref[0] = _route(logits)


def _out_proj(x3, pos, yhy3, ys53, mod3, wglu_bf, bglu, ong, wout_bf, ln1g, ln1b, wr_hi, wr_lo, br, tm):
    nb, lt, _ = x3.shape
    has_pos = pos is not None
    per_batch = mod3.shape[0] > 1
    midx = (lambda b, i: (b, 0, 0)) if per_batch else (lambda b, i: (0, 0, 0))
    tok = lambda w: pl.BlockSpec((1, tm, w), lambda b, i: (b, i, 0))
    full = lambda shape: pl.BlockSpec(shape, lambda b, i: (0,) * len(shape))
    in_specs = [tok(D_MODEL)]
    args = [x3]
    if has_pos:
        in_specs.append(pl.BlockSpec((tm, D_MODEL), lambda b, i: (i, 0)))
        args.append(pos)
    in_specs += [tok(D_HY), tok(D_S5), pl.BlockSpec((1, 1, 6 * D_MODEL), midx),
                 full((D_S5, D_S5)), full((1, D_S5)), full((1, D_MODEL)), full((D_MODEL, D_MODEL)),
                 full((1, D_MODEL)), full((1, D_MODEL)), full((D_MODEL, LANES)), full((D_MODEL, LANES)),
                 full((1, LANES))]
    args += [yhy3, ys53, mod3, wglu_bf, bglu, ong, wout_bf, ln1g, ln1b, wr_hi, wr_lo, br]
    return pl.pallas_call(
        functools.partial(_out_kernel, has_pos),
        grid=(nb, lt // tm),
        in_specs=in_specs,
        out_specs=[tok(D_MODEL), tok(D_MODEL), tok(LANES)],
        out_shape=[jax.ShapeDtypeStruct((nb, lt, D_MODEL), F32),
                   jax.ShapeDtypeStruct((nb, lt, D_MODEL), BF16),
                   jax.ShapeDtypeStruct((nb, lt, LANES), F32)],
        compiler_params=_cparams(("arbitrary", "arbitrary")),
        name=f"out_proj{nb}",
    )(*args)


def _moe_kernel(h_ref, gate_ref, x1_ref, mod_ref, wg_ref, wu_ref, wd_ref, ln2g_ref, ln2b_ref,
                o_ref, acc_ref):
    e = pl.program_id(2)

    @pl.when(e == 0)
    def _():
        acc_ref[...] = jnp.zeros_like(acc_ref)

    h = h_ref[0]
    a = _dot(h, wg_ref[0])
    u = _dot(h, wu_ref[0])
    gates = gate_ref[0]
    lane = lax.broadcasted_iota(jnp.int32, gates.shape, 1)
    g = jnp.sum(jnp.where(lane == e, gates, 0.0), axis=-1, keepdims=True)
    hid = jax.nn.silu(a) * u * g
    acc_ref[...] += _dot(hid.astype(BF16), wd_ref[0])

    @pl.when(e == N_EXPERTS - 1)
    def _():
        g2 = mod_ref[0, :, 5 * D_MODEL:6 * D_MODEL]
        o_ref[0] = _norm(ALPHA * x1_ref[0] + g2 * acc_ref[...]) * ln2g_ref[...] + ln2b_ref[...]


def _moe(h2, gates, x1, mod3, wg_bf, wu_bf, wd_bf, ln2g, ln2b, tm):
    nb, lt, _ = x1.shape
    per_batch = mod3.shape[0] > 1
    midx = (lambda b, i, e: (b, 0, 0)) if per_batch else (lambda b, i, e: (0, 0, 0))
    tok = lambda w: pl.BlockSpec((1, tm, w), lambda b, i, e: (b, i, 0))
    return pl.pallas_call(
        _moe_kernel,
        grid=(nb, lt // tm, N_EXPERTS),
        in_specs=[tok(D_MODEL), tok(LANES), tok(D_MODEL), pl.BlockSpec((1, 1, 6 * D_MODEL), midx),
                  pl.BlockSpec((1, D_MODEL, D_EXPERT), lambda b, i, e: (e, 0, 0)),
                  pl.BlockSpec((1, D_MODEL, D_EXPERT), lambda b, i, e: (e, 0, 0)),
                  pl.BlockSpec((1, D_EXPERT, D_MODEL), lambda b, i, e: (e, 0, 0)),
                  pl.BlockSpec((1, D_MODEL), lambda b, i, e: (0, 0)),
                  pl.BlockSpec((1, D_MODEL), lambda b, i, e: (0, 0))],
        out_specs=tok(D_MODEL),
        out_shape=jax.ShapeDtypeStruct((nb, lt, D_MODEL), F32),
        scratch_shapes=[pltpu.VMEM((tm, D_MODEL), F32)],
        compiler_params=_cparams(("arbitrary", "arbitrary", "arbitrary")),
        name=f"moe{nb}",
    )(h2, gates, x1, mod3, wg_bf, wu_bf, wd_bf, ln2g, ln2b)


def _grid_pos_embed(n_tokens):
    rows = n_tokens // GRID_W
    row = jnp.repeat(jnp.arange(rows, dtype=F32), GRID_W)
    col = jnp.tile(jnp.arange(GRID_W, dtype=F32), rows)
    quarter = D_MODEL // 4
    omega = 1.0 / (POS_BASE ** (jnp.arange(quarter, dtype=F32) / quarter))
    er = row[:, None] * omega
    ec = col[:, None] * omega
    return jnp.concatenate([jnp.sin(er), jnp.cos(er), jnp.sin(ec), jnp.cos(ec)], axis=-1)


def _tables(n_tok):
    cm, sm = _dft_tables(n_tok)
    out = []
    for t in (cm, sm, np.ascontiguousarray(sm.T)):
        hi, lo = _split(jnp.asarray(t))
        out += [hi, lo]
    return tuple(out)


def _path(x, pos, mod3, h0_re, h0_im, tabs, filt, s5ops, wts, tm):
    bsz, n_tok, _ = x.shape
    shared = mod3.shape[0] == 1
    x3 = x.reshape(1, bsz * n_tok, D_MODEL) if shared else x
    proj_hy, u_s5 = _in_proj(x3, pos, mod3, wts['w_in'], tm)
    y_hy = _hyena(proj_hy.reshape(bsz, n_tok, 3 * D_HY), tabs, filt,
                  wts['hy_conv_w'], wts['hy_conv_b'], wts['hy_fbias'])
    y_s5, f_re, f_im = _s5(u_s5.reshape(bsz, n_tok, D_S5), s5ops, h0_re, h0_im)
    x1, h2, gates = _out_proj(x3, pos, y_hy.reshape(x3.shape[0], -1, D_HY),
                              y_s5.reshape(x3.shape[0], -1, D_S5), mod3,
                              wts['w_glu'], wts['b_glu'], wts['out_norm_g'], wts['w_out'],
                              wts['ln1_g'], wts['ln1_b'], wts['wr_hi'], wts['wr_lo'], wts['br'], tm)
    out = _moe(h2, gates, x1, mod3, wts['w_gate'], wts['w_up'], wts['w_down'],
               wts['ln2_g'], wts['ln2_b'], MOE_TM)
    return out.reshape(bsz, n_tok, D_MODEL), f_re, f_im


def kernel(x_prompt, x_sample, state_s5_re, state_s5_im, c, c_ctx, w_ada, b_ada, w_in, hy_conv_w, hy_conv_b, hy_f_w1, hy_f_b1, hy_f_w2, hy_f_b2, hy_f_w3, hy_freq, hy_fbias, s5_a_re, s5_a_im, s5_log_dt, s5_b_re, s5_b_im, s5_c_re, s5_c_im, s5_d, s5_w_glu, s5_b_glu, out_norm_g, w_out, ln1_g, ln1_b, moe_w_r1, moe_b_r1, moe_w_r2, moe_b_r2, moe_w_gate, moe_w_up, moe_w_down, ln2_g, ln2_b):
    b_ctx, l_ctx, _ = x_prompt.shape
    b_lat, l_lat, _ = x_sample.shape
    g, p = S5_GROUPS, S5_STATE
    assert w_ada.shape[0] == 1, "single-layer trunk"
    l = 0

    nrow = 16
    cond = jnp.concatenate([c_ctx[None, :], c, jnp.zeros((nrow - 1 - b_lat, D_MODEL), F32)], axis=0)
    mod = _ada(cond, w_ada[l], b_ada[l])
    mod_ctx = mod[0:1].reshape(1, 1, 6 * D_MODEL)
    mod_lat = mod[1:1 + b_lat].reshape(b_lat, 1, 6 * D_MODEL)

    wr = jnp.concatenate([moe_w_r2[l].transpose(1, 0, 2).reshape(D_MODEL, N_EXPERTS), moe_w_r1[l]], axis=1)
    wr = jnp.pad(wr, ((0, 0), (0, LANES - wr.shape[1])))
    br = jnp.concatenate([moe_b_r2[l].reshape(-1), moe_b_r1[l]])
    br = jnp.pad(br, (0, LANES - br.shape[0])).reshape(1, LANES)
    wr_hi, wr_lo = _split(wr)

    wts = {
        'w_in': w_in[l].astype(BF16), 'hy_conv_w': hy_conv_w[l], 'hy_conv_b': hy_conv_b[l],
        'hy_fbias': hy_fbias[l], 'w_glu': s5_w_glu[l].astype(BF16), 'b_glu': s5_b_glu[l].reshape(1, -1),
        'out_norm_g': out_norm_g[l].reshape(1, -1), 'w_out': w_out[l].astype(BF16),
        'ln1_g': ln1_g[l].reshape(1, -1), 'ln1_b': ln1_b[l].reshape(1, -1),
        'wr_hi': wr_hi, 'wr_lo': wr_lo, 'br': br,
        'w_gate': moe_w_gate[l].astype(BF16), 'w_up': moe_w_up[l].astype(BF16),
        'w_down': moe_w_down[l].astype(BF16),
        'ln2_g': ln2_g[l].reshape(1, -1), 'ln2_b': ln2_b[l].reshape(1, -1),
    }

    s5ops = _s5_operators(s5_a_re[l], s5_a_im[l], s5_log_dt[l], s5_b_re[l], s5_b_im[l],
                          s5_c_re[l], s5_c_im[l], s5_d[l])
    tabs_ctx = _tables(l_ctx)
    tabs_lat = _tables(l_lat)
    filt_args = (hy_f_w1[l], hy_f_b1[l], hy_f_w2[l], hy_f_b2[l], hy_f_w3[l], hy_freq[l])
    filt_ctx = _hyena_filters(l_ctx, tabs_ctx, *filt_args)
    filt_lat = _hyena_filters(l_lat, tabs_lat, *filt_args)

    zero = jnp.zeros((g, b_ctx, 2 * p), F32)
    y_ctx, f_re, f_im = _path(x_prompt, None, mod_ctx, zero, zero, tabs_ctx, filt_ctx, s5ops, wts, 512)
    unpack = lambda f: f.reshape(g, b_ctx, 2, p).transpose(1, 2, 0, 3)[:, None]
    new_re, new_im = unpack(f_re), unpack(f_im)

    pack = lambda s: s[:, l].transpose(2, 0, 1, 3).reshape(g, b_lat, 2 * p)
    pos = _grid_pos_embed(l_lat)
    y_lat, _, _ = _path(x_sample, pos, mod_lat, pack(state_s5_re), pack(state_s5_im),
                        tabs_lat, filt_lat, s5ops, wts, 512)
    return (y_ctx, y_lat, new_re, new_im)
```

```python
import functools
import math

import numpy as np
import jax
import jax.numpy as jnp
from jax import lax
from jax.experimental import pallas as pl
from jax.experimental.pallas import tpu as pltpu

F32 = jnp.float32
BF16 = jnp.bfloat16

D_MODEL = 1024
DEPTH = 1
GRID_W = 64
POS_BASE = 10000.0
D_HY = 512
D_S5 = 512
S5_CH = 16
S5_GROUPS = 32
S5_STATE = 64
S5_CHUNK = 16
S5_ROW = S5_CHUNK * S5_CH
HY_BANDS = 16
HY_EMB = 1 + 2 * HY_BANDS
HY_HID = 64
HY_MIN_DECAY = math.log(1e-2) / 1.5
HY_MAX_DECAY = math.log(1e-2) / 0.3
N_EGROUPS = 4
N_EPG = 4
N_EXPERTS = 16
D_EXPERT = 512
LN_EPS = 1e-5
ALPHA = (2.0 * DEPTH) ** 0.25
LANES = 128
S5_GB = LANES // S5_CH
HY_CW = 256
MOE_TM = 1024
VMEM_LIMIT = 60000 * 1024


def _cparams(sem):
    return pltpu.CompilerParams(dimension_semantics=sem, vmem_limit_bytes=VMEM_LIMIT)


def _split(x):
    hi = x.astype(BF16)
    lo = (x - hi.astype(F32)).astype(BF16)
    return hi, lo


def _dot(a, b):
    return jnp.dot(a, b, preferred_element_type=F32)


def _dot_t(a, b):
    return lax.dot_general(a, b, (((1,), (1,)), ((), ())), preferred_element_type=F32)


def _mm3(a, b):
    ah, al = _split(a)
    bh, bl = _split(b)
    return _dot(ah, bh) + _dot(al, bh) + _dot(ah, bl)


def _mm3_pre(ah, al, b):
    bh, bl = _split(b)
    return _dot(ah, bh) + _dot(al, bh) + _dot(ah, bl)


def _mm3_t(a, b):
    ah, al = _split(a)
    bh, bl = _split(b)
    return _dot_t(ah, bh) + _dot_t(al, bh) + _dot_t(ah, bl)


def _dot_hp(a, b):
    return jnp.dot(a, b, preferred_element_type=F32, precision=lax.Precision.HIGHEST)


def _norm(x):
    xc = x - jnp.mean(x, axis=-1, keepdims=True)
    return xc * lax.rsqrt(jnp.mean(xc * xc, axis=-1, keepdims=True) + LN_EPS)


def _rms(y):
    return y * lax.rsqrt(jnp.mean(y * y, axis=-1, keepdims=True) + LN_EPS)


def _ada_kernel(cond_ref, w_ref, b_ref, o_ref):
    c = jax.nn.silu(cond_ref[...])
    o_ref[...] = _mm3(c, w_ref[...]) + b_ref[...]


def _ada(cond, w_ada, b_ada):
    nb = cond.shape[0]
    n = w_ada.shape[1]
    tn = 1024
    return pl.pallas_call(
        _ada_kernel,
        grid=(n // tn,),
        in_specs=[pl.BlockSpec((nb, D_MODEL), lambda j: (0, 0)),
                  pl.BlockSpec((D_MODEL, tn), lambda j: (0, j)),
                  pl.BlockSpec((1, tn), lambda j: (0, j))],
        out_specs=pl.BlockSpec((nb, tn), lambda j: (0, j)),
        out_shape=jax.ShapeDtypeStruct((nb, n), F32),
        compiler_params=_cparams(("arbitrary",)),
        name="ada",
    )(cond, w_ada, b_ada.reshape(1, n))


def _dft_tables(n_tok):
    n = 2 * n_tok
    idx = np.arange(n_tok, dtype=np.int64)
    m = (idx[:, None] * idx[None, :]) % n
    ang = 2.0 * np.pi * m.astype(np.float64) / n
    cm = np.cos(ang)
    sm = -np.sin(ang)
    sm[0, :] = 1.0 - 2.0 * (idx % 2)
    return cm.astype(np.float32), sm.astype(np.float32)


def _filt_kernel(n_tok, z_ref, t_ref, w1_ref, b1_ref, w2_ref, b2_ref, fr_ref, w3f_ref, w3b_ref,
                 dl_ref, cmh_ref, cml_ref, smh_ref, sml_ref, a_ref, bz_ref, dd_ref):
    fr = fr_ref[...]
    h = jnp.sin(fr * (_dot_hp(z_ref[...], w1_ref[...]) + b1_ref[...]))
    h = jnp.sin(fr * (_dot_hp(h, w2_ref[...]) + b2_ref[...]))
    decay = jnp.exp(-t_ref[...] * dl_ref[...])
    row = lax.broadcasted_iota(jnp.int32, decay.shape, 0)
    hf = _dot_hp(h, w3f_ref[...]) * decay
    hb = jnp.where(row == 0, 0.0, _dot_hp(h, w3b_ref[...]) * decay)
    p = hf + hb
    q = hf - hb
    k_re = _mm3_pre(cmh_ref[...], cml_ref[...], p)
    k_im = _mm3_pre(smh_ref[...], sml_ref[...], q)
    sign = jnp.where(row % 2 == 0, 1.0, -1.0)
    nyq = jnp.sum(p * sign, axis=0, keepdims=True)
    inv_n = 1.0 / (2 * n_tok)
    w = jnp.where(row == 0, inv_n, 2.0 * inv_n)
    a_ref[...] = w * k_re
    bz_ref[...] = jnp.where(row == 0, 0.0, w * k_im)
    dd_ref[...] = jnp.where(row == 0, nyq * inv_n, w * k_re)


def _hyena_filters(n_tok, tabs, hy_f_w1, hy_f_b1, hy_f_w2, hy_f_b2, hy_f_w3, hy_freq):
    cmh, cml, smh, sml = tabs[:4]
    t = jnp.linspace(0.0, 1.0, n_tok, dtype=F32)[:, None]
    wv = 2.0 * math.pi * jnp.arange(n_tok, dtype=F32) / n_tok
    fb = jnp.linspace(1e-4, HY_BANDS - 1, HY_BANDS, dtype=F32)
    ang = wv[:, None] * fb[None, :]
    z = jnp.concatenate([t, jnp.cos(ang), -jnp.sin(ang)], axis=-1)
    z = jnp.pad(z, ((0, 0), (0, LANES - HY_EMB)))
    w1 = jnp.pad(hy_f_w1, ((0, LANES - HY_EMB), (0, 0)))
    deltas = jnp.abs(jnp.linspace(HY_MIN_DECAY, HY_MAX_DECAY, D_HY, dtype=F32))[None, :]
    ncb = D_HY // HY_CW
    full = lambda j: (0, 0)
    out_sd = jax.ShapeDtypeStruct((n_tok, 2 * D_HY), F32)
    mat = pl.BlockSpec((n_tok, n_tok), full, pipeline_mode=pl.Buffered(1))
    return pl.pallas_call(
        functools.partial(_filt_kernel, n_tok),
        grid=(2 * ncb,),
        in_specs=[pl.BlockSpec((n_tok, LANES), full),
                  pl.BlockSpec((n_tok, 1), full),
                  pl.BlockSpec((LANES, HY_HID), full),
                  pl.BlockSpec((1, HY_HID), full),
                  pl.BlockSpec((HY_HID, HY_HID), full),
                  pl.BlockSpec((1, HY_HID), full),
                  pl.BlockSpec((1, HY_HID), full),
                  pl.BlockSpec((HY_HID, HY_CW), lambda j: (0, 2 * ncb * (j // ncb) + j % ncb)),
                  pl.BlockSpec((HY_HID, HY_CW), lambda j: (0, 2 * ncb * (j // ncb) + ncb + j % ncb)),
                  pl.BlockSpec((1, HY_CW), lambda j: (0, j % ncb)),
                  mat, mat, mat, mat],
        out_specs=[pl.BlockSpec((n_tok, HY_CW), lambda j: (0, j))] * 3,
        out_shape=[out_sd] * 3,
        compiler_params=_cparams(("arbitrary",)),
        name=f"filt{n_tok}",
    )(z, t, w1, hy_f_b1.reshape(1, -1), hy_f_w2, hy_f_b2.reshape(1, -1), hy_freq.reshape(1, -1),
      hy_f_w3, hy_f_w3, deltas, cmh, cml, smh, sml)


def _hyena_kernel(pv_ref, p1_ref, p2_ref, cwv_ref, cw1_ref, cw2_ref, cbv_ref, cb1_ref, cb2_ref,
                  fbias_ref, cm_ref, sm_ref, st_ref,
                  a0_ref, b0_ref, d0_ref, a1_ref, b1_ref, d1_ref, o_ref):
    n_tok = pv_ref.shape[1]
    row = lax.broadcasted_iota(jnp.int32, (n_tok, pv_ref.shape[2]), 0)

    def short_conv(p_ref, cw_ref, cb_ref):
        p = p_ref[0]
        prev = jnp.where(row == 0, 0.0, pltpu.roll(p, 1, axis=0))
        nxt = jnp.where(row == n_tok - 1, 0.0, pltpu.roll(p, n_tok - 1, axis=0))
        return cb_ref[...] + prev * cw_ref[0:1, :] + p * cw_ref[1:2, :] + nxt * cw_ref[2:3, :]

    cm = cm_ref[...]

    def fftconv(u, a_ref, b_ref, d_ref, skip):
        ub = u.astype(BF16)
        u_re = _dot(cm, ub)
        u_im = _dot(sm_ref[...], ub)
        a, bz, dd = a_ref[...], b_ref[...], d_ref[...]
        y_re = u_re * a - u_im * bz
        y_im = u_re * bz + u_im * dd
        y = _dot(cm, y_re.astype(BF16)) + _dot(st_ref[...], y_im.astype(BF16))
        return y + u * skip

    v = short_conv(pv_ref, cwv_ref, cbv_ref)
    x1 = short_conv(p1_ref, cw1_ref, cb1_ref)
    z = x1 * fftconv(v, a0_ref, b0_ref, d0_ref, fbias_ref[0:1, :])
    x2 = short_conv(p2_ref, cw2_ref, cb2_ref)
    o_ref[0] = x2 * fftconv(z, a1_ref, b1_ref, d1_ref, fbias_ref[1:2, :])


def _hyena(proj_hy, tabs, filt, hy_conv_w, hy_conv_b, hy_fbias):
    bsz, n_tok, _ = proj_hy.shape
    ncb = D_HY // HY_CW
    cmh, _, smh, _, sth, _ = tabs
    fa, fbz, fdd = filt
    cb = hy_conv_b.reshape(1, -1)
    mat = pl.BlockSpec((n_tok, n_tok), lambda b, c: (0, 0), pipeline_mode=pl.Buffered(1))

    def pspec(k):
        return pl.BlockSpec((1, n_tok, HY_CW), lambda b, c: (b, 0, k * ncb + c))

    def cwspec(k):
        return pl.BlockSpec((3, HY_CW), lambda b, c: (0, k * ncb + c))

    def cbspec(k):
        return pl.BlockSpec((1, HY_CW), lambda b, c: (0, k * ncb + c))

    def fspec(o):
        return pl.BlockSpec((n_tok, HY_CW), lambda b, c: (0, o * ncb + c))

    return pl.pallas_call(
        _hyena_kernel,
        grid=(bsz, ncb),
        in_specs=[pspec(0), pspec(1), pspec(2), cwspec(0), cwspec(1), cwspec(2),
                  cbspec(0), cbspec(1), cbspec(2),
                  pl.BlockSpec((2, HY_CW), lambda b, c: (0, c)),
                  mat, mat, mat,
                  fspec(0), fspec(0), fspec(0), fspec(1), fspec(1), fspec(1)],
        out_specs=pl.BlockSpec((1, n_tok, HY_CW), lambda b, c: (b, 0, c)),
        out_shape=jax.ShapeDtypeStruct((bsz, n_tok, D_HY), F32),
        compiler_params=_cparams(("arbitrary", "arbitrary")),
        name=f"hyena{n_tok}",
    )(proj_hy, proj_hy, proj_hy, hy_conv_w, hy_conv_w, hy_conv_w, cb, cb, cb, hy_fbias,
      cmh, smh, sth, fa, fbz, fdd, fa, fbz, fdd)


def _s5ops_kernel(are_ref, aim_ref, ldt_ref, btr_ref, bti_ref, cre_ref, cim_ref,
                  er_ref, ei_ref, gr_ref, gi_ref, kf_ref, kb_ref, atr_ref, ati_ref):
    a_re, a_im = are_ref[0], aim_ref[0]
    dt = jnp.exp(ldt_ref[0])
    mag = jnp.exp(a_re * dt)
    ab_re = mag * jnp.cos(a_im * dt)
    ab_im = mag * jnp.sin(a_im * dt)
    n_re, n_im = ab_re - 1.0, ab_im
    den = a_re * a_re + a_im * a_im
    q_re = (n_re * a_re + n_im * a_im) / den
    q_im = (n_im * a_re - n_re * a_im) / den
    bt_re, bt_im = btr_ref[0], bti_ref[0]
    bb_re = q_re * bt_re - q_im * bt_im
    bb_im = q_re * bt_im + q_im * bt_re
    c_re, c_im = cre_ref[0], cim_ref[0]
    pw = [(jnp.ones_like(ab_re), jnp.zeros_like(ab_re))]
    for _ in range(S5_CHUNK):
        pr, pi = pw[-1]
        pw.append((pr * ab_re - pi * ab_im, pr * ab_im + pi * ab_re))
    lane = lax.broadcasted_iota(jnp.int32, ab_re.shape, 1)
    fwd = lane < S5_STATE
    for s in range(S5_CHUNK):
        e_re = jnp.where(fwd, pw[S5_CHUNK - 1 - s][0], pw[s][0])
        e_im = jnp.where(fwd, pw[S5_CHUNK - 1 - s][1], pw[s][1])
        er_ref[0, pl.ds(S5_CH * s, S5_CH), :] = e_re * bb_re - e_im * bb_im
        ei_ref[0, pl.ds(S5_CH * s, S5_CH), :] = e_re * bb_im + e_im * bb_re
        g_re = jnp.where(fwd, pw[s + 1][0], pw[S5_CHUNK - s][0])
        g_im = jnp.where(fwd, pw[s + 1][1], pw[S5_CHUNK - s][1])
        gr_ref[0, pl.ds(S5_CH * s, S5_CH), :] = c_re * g_re - c_im * g_im
        gi_ref[0, pl.ds(S5_CH * s, S5_CH), :] = -(c_re * g_im + c_im * g_re)
    er, ei = er_ref[0], ei_ref[0]
    lane2 = lax.broadcasted_iota(jnp.int32, er.shape, 1)
    f2 = lane2 < S5_STATE
    zero = jnp.zeros_like(er)

    def dot_hp_t(a, b):
        return lax.dot_general(a, b, (((1,), (1,)), ((), ())), preferred_element_type=F32,
                               precision=lax.Precision.HIGHEST)

    kf_ref[0] = dot_hp_t(jnp.where(f2, er, zero), c_re) - dot_hp_t(jnp.where(f2, ei, zero), c_im)
    kb_ref[0] = dot_hp_t(jnp.where(f2, zero, er), c_re) - dot_hp_t(jnp.where(f2, zero, ei), c_im)
    atr_ref[0] = pw[S5_CHUNK][0]
    ati_ref[0] = pw[S5_CHUNK][1]


def _s5_operators(s5_a_re, s5_a_im, s5_log_dt, s5_b_re, s5_b_im, s5_c_re, s5_c_im, s5_d):
    g, p, h = S5_GROUPS, S5_STATE, S5_CH
    cat = lambda x: jnp.concatenate([x[0], x[1]], axis=-1)
    a_re = cat(s5_a_re).reshape(g, 1, 2 * p)
    a_im = cat(s5_a_im).reshape(g, 1, 2 * p)
    ldt = cat(jnp.broadcast_to(s5_log_dt[:, :, None], (2, g, p))).reshape(g, 1, 2 * p)
    bt_re = cat(jnp.swapaxes(s5_b_re, -1, -2))
    bt_im = cat(jnp.swapaxes(s5_b_im, -1, -2))
    c_re = jnp.concatenate([s5_c_re, s5_c_re], axis=-1)
    c_im = jnp.concatenate([s5_c_im, s5_c_im], axis=-1)
    vec = pl.BlockSpec((1, 1, 2 * p), lambda i: (i, 0, 0))
    hp = pl.BlockSpec((1, h, 2 * p), lambda i: (i, 0, 0))
    big = pl.BlockSpec((1, S5_ROW, 2 * p), lambda i: (i, 0, 0))
    kk = pl.BlockSpec((1, S5_ROW, h), lambda i: (i, 0, 0))
    big_sd = jax.ShapeDtypeStruct((g, S5_ROW, 2 * p), F32)
    kk_sd = jax.ShapeDtypeStruct((g, S5_ROW, h), F32)
    vec_sd = jax.ShapeDtypeStruct((g, 1, 2 * p), F32)
    er, ei, gr, gi, kf, kb, at_re, at_im = pl.pallas_call(
        _s5ops_kernel,
        grid=(g,),
        in_specs=[vec, vec, vec, hp, hp, hp, hp],
        out_specs=[big, big, big, big, kk, kk, vec, vec],
        out_shape=[big_sd, big_sd, big_sd, big_sd, kk_sd, kk_sd, vec_sd, vec_sd],
        compiler_params=_cparams(("arbitrary",)),
        name="s5ops",
    )(a_re, a_im, ldt, bt_re, bt_im, c_re, c_im)
    kf_lag = kf.reshape(g, S5_CHUNK, h, h)[:, ::-1]
    kb_lag = kb.reshape(g, S5_CHUNK, h, h)
    lag = np.arange(S5_CHUNK)[None, :] - np.arange(S5_CHUNK)[:, None]
    mf = kf_lag[:, np.clip(lag, 0, S5_CHUNK - 1)] * jnp.asarray(lag >= 0, F32)[None, :, :, None, None]
    mb = kb_lag[:, np.clip(-lag, 0, S5_CHUNK - 1)] * jnp.asarray(lag <= 0, F32)[None, :, :, None, None]
    mt = (mf + mb).transpose(0, 1, 3, 2, 4).reshape(g, S5_ROW, S5_ROW)
    dvec = jnp.tile(s5_d.reshape(g, 1, h), (1, S5_CHUNK, 1)).reshape(g, S5_ROW)
    mt = mt + jnp.eye(S5_ROW, dtype=F32)[None] * dvec[:, None, :]
    return mt, er, ei, gr, gi, at_re, at_im


def _s5_kernel(bsz, n_chunks, u_ref, mt_ref, er_ref, ei_ref, gr_ref, gi_ref, atr_ref, ati_ref,
               h0r_ref, h0i_ref, y_ref, fr_ref, fi_ref, u8_ref, y8_ref, sr_ref, si_ref,
               xfr_ref, xfi_ref, xbr_ref, xbi_ref):
    nc = n_chunks
    steps_per_vreg = LANES // S5_CH

    def to_chunks(b, carry):
        r0 = pl.multiple_of(b * nc, 8)
        for t in range(S5_CHUNK):
            a = u_ref[b, pl.ds(t, nc, stride=S5_CHUNK), :]
            off = S5_CH * (t % steps_per_vreg)
            base = LANES * (t // steps_per_vreg)
            for k in range(S5_GB):
                shift = (off - S5_CH * k) % LANES
                r = pltpu.roll(a, shift, axis=1) if shift else a
                u8_ref[k, pl.ds(r0, nc), base + off:base + off + S5_CH] = r[:, off:off + S5_CH]
        return carry

    lax.fori_loop(0, bsz, to_chunks, 0)

    lane = lax.broadcasted_iota(jnp.int32, (bsz, 2 * S5_STATE), 1)
    fwd = lane < S5_STATE
    lane_all = lax.broadcasted_iota(jnp.int32, (bsz * nc, 2 * S5_STATE), 1)
    fwd_all = lane_all < S5_STATE

    def group(k, carry):
        uh, ul = _split(u8_ref[k])
        sr_ref[...] = _mm3_pre(uh, ul, er_ref[k])
        si_ref[...] = _mm3_pre(uh, ul, ei_ref[k])
        at_re, at_im = atr_ref[k], ati_ref[k]

        def step(i, xc):
            x_re, x_im = xc
            rf = pl.ds(i, bsz, stride=nc)
            rb = pl.ds(nc - 1 - i, bsz, stride=nc)
            xfr_ref[rf, :] = x_re
            xfi_ref[rf, :] = x_im
            xbr_ref[rb, :] = x_re
            xbi_ref[rb, :] = x_im
            s_re = jnp.where(fwd, sr_ref[rf, :], sr_ref[rb, :])
            s_im = jnp.where(fwd, si_ref[rf, :], si_ref[rb, :])
            return (at_re * x_re - at_im * x_im + s_re, at_re * x_im + at_im * x_re + s_im)

        x_re, x_im = lax.fori_loop(0, nc, step, (h0r_ref[k], h0i_ref[k]))
        fr_ref[k] = x_re
        fi_ref[k] = x_im
        xp_re = jnp.where(fwd_all, xfr_ref[...], xbr_ref[...])
        xp_im = jnp.where(fwd_all, xfi_ref[...], xbi_ref[...])
        y8_ref[k] = (_mm3_pre(uh, ul, mt_ref[k]) + _mm3_t(xp_re, gr_ref[k]) + _mm3_t(xp_im, gi_ref[k]))
        return carry

    lax.fori_loop(0, S5_GB, group, 0)

    lane_tok = lax.broadcasted_iota(jnp.int32, (nc, LANES), 1)

    def to_tokens(b, carry):
        r0 = pl.multiple_of(b * nc, 8)
        for t in range(S5_CHUNK):
            off = S5_CH * (t % steps_per_vreg)
            base = LANES * (t // steps_per_vreg)
            acc = None
            for k in range(S5_GB):
                src = y8_ref[k, pl.ds(r0, nc), base:base + LANES]
                shift = (S5_CH * k - off) % LANES
                r = pltpu.roll(src, shift, axis=1) if shift else src
                sel = (lane_tok >= S5_CH * k) & (lane_tok < S5_CH * (k + 1))
                acc = jnp.where(sel, r, 0.0 if acc is None else acc)
            y_ref[b, pl.ds(t, nc, stride=S5_CHUNK), :] = acc
        return carry

    lax.fori_loop(0, bsz, to_tokens, 0)


def _s5(u, ops, h0_re, h0_im):
    bsz, n_tok, _ = u.shape
    mt, er, ei, gr, gi, at_re, at_im = ops
    g, p = S5_GROUPS, S5_STATE
    nc = n_tok // S5_CHUNK
    rows = nc * bsz
    tok = pl.BlockSpec((bsz, n_tok, LANES), lambda j: (0, 0, j))
    gspec = lambda shape: pl.BlockSpec((S5_GB,) + shape, lambda j: (j, 0, 0))
    return pl.pallas_call(
        functools.partial(_s5_kernel, bsz, nc),
        grid=(g // S5_GB,),
        in_specs=[tok, gspec((S5_ROW, S5_ROW)), gspec((S5_ROW, 2 * p)),
                  gspec((S5_ROW, 2 * p)), gspec((S5_ROW, 2 * p)), gspec((S5_ROW, 2 * p)),
                  gspec((1, 2 * p)), gspec((1, 2 * p)), gspec((bsz, 2 * p)), gspec((bsz, 2 * p))],
        out_specs=[tok, gspec((bsz, 2 * p)), gspec((bsz, 2 * p))],
        out_shape=[jax.ShapeDtypeStruct((bsz, n_tok, D_S5), F32),
                   jax.ShapeDtypeStruct((g, bsz, 2 * p), F32),
                   jax.ShapeDtypeStruct((g, bsz, 2 * p), F32)],
        scratch_shapes=([pltpu.VMEM((S5_GB, rows, S5_ROW), F32)] * 2
                        + [pltpu.VMEM((rows, 2 * p), F32)] * 6),
        compiler_params=_cparams(("arbitrary",)),
        name=f"s5_{n_tok}",
    )(u, mt, er, ei, gr, gi, at_re, at_im, h0_re, h0_im)


def _in_kernel(has_pos, *refs):
    if has_pos:
        x_ref, pos_ref, mod_ref, w_ref, hy_ref, s5_ref = refs
        x = x_ref[0] + pos_ref[...]
    else:
        x_ref, mod_ref, w_ref, hy_ref, s5_ref = refs
        x = x_ref[0]
    sh1 = mod_ref[0, :, 0:D_MODEL]
    sc1 = mod_ref[0, :, D_MODEL:2 * D_MODEL]
    h = _norm(x) * (1.0 + sc1) + sh1
    proj = _dot(h.astype(BF16), w_ref[...])
    hy_ref[0] = proj[:, :3 * D_HY]
    s5_ref[0] = proj[:, 3 * D_HY:]


def _in_proj(x3, pos, mod3, w_in_bf, tm):
    nb, lt, _ = x3.shape
    has_pos = pos is not None
    per_batch = mod3.shape[0] > 1
    midx = (lambda b, i: (b, 0, 0)) if per_batch else (lambda b, i: (0, 0, 0))
    in_specs = [pl.BlockSpec((1, tm, D_MODEL), lambda b, i: (b, i, 0))]
    args = [x3]
    if has_pos:
        in_specs.append(pl.BlockSpec((tm, D_MODEL), lambda b, i: (i, 0)))
        args.append(pos)
    in_specs += [pl.BlockSpec((1, 1, 6 * D_MODEL), midx),
                 pl.BlockSpec((D_MODEL, 3 * D_HY + D_S5), lambda b, i: (0, 0))]
    args += [mod3, w_in_bf]
    return pl.pallas_call(
        functools.partial(_in_kernel, has_pos),
        grid=(nb, lt // tm),
        in_specs=in_specs,
        out_specs=[pl.BlockSpec((1, tm, 3 * D_HY), lambda b, i: (b, i, 0)),
                   pl.BlockSpec((1, tm, D_S5), lambda b, i: (b, i, 0))],
        out_shape=[jax.ShapeDtypeStruct((nb, lt, 3 * D_HY), F32),
                   jax.ShapeDtypeStruct((nb, lt, D_S5), F32)],
        compiler_params=_cparams(("arbitrary", "arbitrary")),
        name=f"in_proj{nb}",
    )(*args)


def _route(logits):
    lane = lax.broadcasted_iota(jnp.int32, logits.shape, 1)
    lane_f = lane.astype(F32)
    neg = -jnp.inf
    big = float(LANES)
    m1 = (lane >= N_EXPERTS) & (lane < N_EXPERTS + N_EGROUPS)
    l1 = jnp.where(m1, logits, neg)
    top1 = jnp.max(l1, axis=-1, keepdims=True)
    grp = jnp.min(jnp.where(l1 == top1, lane_f, big), axis=-1, keepdims=True) - float(N_EXPERTS)
    den = jnp.sum(jnp.where(m1, jnp.exp(logits - top1), 0.0), axis=-1, keepdims=True)
    p_grp = 1.0 / den
    lo = grp * float(N_EPG)
    m2 = (lane_f >= lo) & (lane_f < lo + float(N_EPG))
    l2 = jnp.where(m2, logits, neg)
    v1 = jnp.max(l2, axis=-1, keepdims=True)
    i1 = jnp.min(jnp.where(l2 == v1, lane_f, big), axis=-1, keepdims=True)
    l2b = jnp.where(lane_f == i1, neg, l2)
    v2 = jnp.max(l2b, axis=-1, keepdims=True)
    i2 = jnp.min(jnp.where(l2b == v2, lane_f, big), axis=-1, keepdims=True)
    e = jnp.exp(v2 - v1)
    w1 = 1.0 / (1.0 + e)
    w2 = e / (1.0 + e)
    return (jnp.where(lane_f == i1, w1 * p_grp, 0.0) + jnp.where(lane_f == i2, w2 * p_grp, 0.0))


def _out_kernel(has_pos, *refs):
    if has_pos:
        x_ref, pos_ref = refs[:2]
        rest = refs[2:]
        x = x_ref[0] + pos_ref[...]
    else:
        x_ref = refs[0]
        rest = refs[1:]
        x = x_ref[0]
    (yhy_ref, ys5_ref, mod_ref, wglu_ref, bglu_ref, ong_ref, wout_ref, ln1g_ref, ln1b_ref,
     wrh_ref, wrl_ref, br_ref, x1_ref, h2_ref, gate_ref) = rest
    y = ys5_ref[0]
    s5 = jax.nn.gelu(y) * jax.nn.sigmoid(_dot(y.astype(BF16), wglu_ref[...]) + bglu_ref[...])
    m_hy = _rms(yhy_ref[0]) * ong_ref[:, 0:D_HY]
    m_s5 = _rms(s5) * ong_ref[:, D_HY:]
    o = (_dot(m_hy.astype(BF16), wout_ref[0:D_HY, :]) + _dot(m_s5.astype(BF16), wout_ref[D_HY:, :]))
    g1 = mod_ref[0, :, 2 * D_MODEL:3 * D_MODEL]
    sh2 = mod_ref[0, :, 3 * D_MODEL:4 * D_MODEL]
    sc2 = mod_ref[0, :, 4 * D_MODEL:5 * D_MODEL]
    x1 = _norm(ALPHA * x + g1 * o) * ln1g_ref[...] + ln1b_ref[...]
    x1_ref[0] = x1
    h2 = _norm(x1) * (1.0 + sc2) + sh2
    h2_ref[0] = h2.astype(BF16)
    hh, hl = _split(h2)
    logits = (_dot(hh, wrh_ref[...]) + _dot(hl, wrh_ref[...]) + _dot(hh, wrl_ref[...]) + br_ref[...])
    gate_ref[0] = _route(logits)


def _out_proj(x3, pos, yhy3, ys53, mod3, wglu_bf, bglu, ong, wout_bf, ln1g, ln1b, wr_hi, wr_lo, br, tm):
    nb, lt, _ = x3.shape
    has_pos = pos is not None
    per_batch = mod3.shape[0] > 1
    midx = (lambda b, i: (b, 0, 0)) if per_batch else (lambda b, i: (0, 0, 0))
    tok = lambda w: pl.BlockSpec((1, tm, w), lambda b, i: (b, i, 0))
    full = lambda shape: pl.BlockSpec(shape, lambda b, i: (0,) * len(shape))
    in_specs = [tok(D_MODEL)]
    args = [x3]
    if has_pos:
        in_specs.append(pl.BlockSpec((tm, D_MODEL), lambda b, i: (i, 0)))
        args.append(pos)
    in_specs += [tok(D_HY), tok(D_S5), pl.BlockSpec((1, 1, 6 * D_MODEL), midx),
                 full((D_S5, D_S5)), full((1, D_S5)), full((1, D_MODEL)), full((D_MODEL, D_MODEL)),
                 full((1, D_MODEL)), full((1, D_MODEL)), full((D_MODEL, LANES)), full((D_MODEL, LANES)),
                 full((1, LANES))]
    args += [yhy3, ys53, mod3, wglu_bf, bglu, ong, wout_bf, ln1g, ln1b, wr_hi, wr_lo, br]
    return pl.pallas_call(
        functools.partial(_out_kernel, has_pos),
        grid=(nb, lt // tm),
        in_specs=in_specs,
        out_specs=[tok(D_MODEL), tok(D_MODEL), tok(LANES)],
        out_shape=[jax.ShapeDtypeStruct((nb, lt, D_MODEL), F32),
                   jax.ShapeDtypeStruct((nb, lt, D_MODEL), BF16),
                   jax.ShapeDtypeStruct((nb, lt, LANES), F32)],
        compiler_params=_cparams(("arbitrary", "arbitrary")),
        name=f"out_proj{nb}",
    )(*args)


def _moe_kernel(h_ref, gate_ref, x1_ref, mod_ref, wg_ref, wu_ref, wd_ref, ln2g_ref, ln2b_ref,
                o_ref, acc_ref):
    e = pl.program_id(2)

    @pl.when(e == 0)
    def _():
        acc_ref[...] = jnp.zeros_like(acc_ref)

    h = h_ref[0]
    a = _dot(h, wg_ref[0])
    u = _dot(h, wu_ref[0])
    gates = gate_ref[0]
    lane = lax.broadcasted_iota(jnp.int32, gates.shape, 1)
    g = jnp.sum(jnp.where(lane == e, gates, 0.0), axis=-1, keepdims=True)
    hid = jax.nn.silu(a) * u * g
    acc_ref[...] += _dot(hid.astype(BF16), wd_ref[0])

    @pl.when(e == N_EXPERTS - 1)
    def _():
        g2 = mod_ref[0, :, 5 * D_MODEL:6 * D_MODEL]
        o_ref[0] = _norm(ALPHA * x1_ref[0] + g2 * acc_ref[...]) * ln2g_ref[...] + ln2b_ref[...]


def _moe(h2, gates, x1, mod3, wg_bf, wu_bf, wd_bf, ln2g, ln2b, tm):
    nb, lt, _ = x1.shape
    per_batch = mod3.shape[0] > 1
    midx = (lambda b, i, e: (b, 0, 0)) if per_batch else (lambda b, i, e: (0, 0, 0))
    tok = lambda w: pl.BlockSpec((1, tm, w), lambda b, i, e: (b, i, 0))
    return pl.pallas_call(
        _moe_kernel,
        grid=(nb, lt // tm, N_EXPERTS),
        in_specs=[tok(D_MODEL), tok(LANES), tok(D_MODEL), pl.BlockSpec((1, 1, 6 * D_MODEL), midx),
                  pl.BlockSpec((1, D_MODEL, D_EXPERT), lambda b, i, e: (e, 0, 0)),
                  pl.BlockSpec((1, D_MODEL, D_EXPERT), lambda b, i, e: (e, 0, 0)),
                  pl.BlockSpec((1, D_EXPERT, D_MODEL), lambda b, i, e: (e, 0, 0)),
                  pl.BlockSpec((1, D_MODEL), lambda b, i, e: (0, 0)),
                  pl.BlockSpec((1, D_MODEL), lambda b, i, e: (0, 0))],
        out_specs=tok(D_MODEL),
        out_shape=jax.ShapeDtypeStruct((nb, lt, D_MODEL), F32),
        scratch_shapes=[pltpu.VMEM((tm, D_MODEL), F32)],
        compiler_params=_cparams(("arbitrary", "arbitrary", "arbitrary")),
        name=f"moe{nb}",
    )(h2, gates, x1, mod3, wg_bf, wu_bf, wd_bf, ln2g, ln2b)


def _grid_pos_embed(n_tokens):
    rows = n_tokens // GRID_W
    row = jnp.repeat(jnp.arange(rows, dtype=F32), GRID_W)
    col = jnp.tile(jnp.arange(GRID_W, dtype=F32), rows)
    quarter = D_MODEL // 4
    omega = 1.0 / (POS_BASE ** (jnp.arange(quarter, dtype=F32) / quarter))
    er = row[:, None] * omega
    ec = col[:, None] * omega
    return jnp.concatenate([jnp.sin(er), jnp.cos(er), jnp.sin(ec), jnp.cos(ec)], axis=-1)


def _tables(n_tok):
    cm, sm = _dft_tables(n_tok)
    out = []
    for t in (cm, sm, np.ascontiguousarray(sm.T)):
        hi, lo = _split(jnp.asarray(t))
        out += [hi, lo]
    return tuple(out)


def _path(x, pos, mod3, h0_re, h0_im, tabs, filt, s5ops, wts, tm):
    bsz, n_tok, _ = x.shape
    shared = mod3.shape[0] == 1
    x3 = x.reshape(1, bsz * n_tok, D_MODEL) if shared else x
    proj_hy, u_s5 = _in_proj(x3, pos, mod3, wts['w_in'], tm)
    y_hy = _hyena(proj_hy.reshape(bsz, n_tok, 3 * D_HY), tabs, filt,
                  wts['hy_conv_w'], wts['hy_conv_b'], wts['hy_fbias'])
    y_s5, f_re, f_im = _s5(u_s5.reshape(bsz, n_tok, D_S5), s5ops, h0_re, h0_im)
    x1, h2, gates = _out_proj(x3, pos, y_hy.reshape(x3.shape[0], -1, D_HY),
                              y_s5.reshape(x3.shape[0], -1, D_S5), mod3,
                              wts['w_glu'], wts['b_glu'], wts['out_norm_g'], wts['w_out'],
                              wts['ln1_g'], wts['ln1_b'], wts['wr_hi'], wts['wr_lo'], wts['br'], tm)
    out = _moe(h2, gates, x1, mod3, wts['w_gate'], wts['w_up'], wts['w_down'],
               wts['ln2_g'], wts['ln2_b'], MOE_TM)
    return out.reshape(bsz, n_tok, D_MODEL), f_re, f_im


def kernel(x_prompt, x_sample, state_s5_re, state_s5_im, c, c_ctx, w_ada, b_ada, w_in, hy_conv_w, hy_conv_b, hy_f_w1, hy_f_b1, hy_f_w2, hy_f_b2, hy_f_w3, hy_freq, hy_fbias, s5_a_re, s5_a_im, s5_log_dt, s5_b_re, s5_b_im, s5_c_re, s5_c_im, s5_d, s5_w_glu, s5_b_glu, out_norm_g, w_out, ln1_g, ln1_b, moe_w_r1, moe_b_r1, moe_w_r2, moe_b_r2, moe_w_gate, moe_w_up, moe_w_down, ln2_g, ln2_b):
    b_ctx, l_ctx, _ = x_prompt.shape
    b_lat, l_lat, _ = x_sample.shape
    g, p = S5_GROUPS, S5_STATE
    assert w_ada.shape[0] == 1, "single-layer trunk"
    l = 0

    nrow = 16
    cond = jnp.concatenate([c_ctx[None, :], c, jnp.zeros((nrow - 1 - b_lat, D_MODEL), F32)], axis=0)
    mod = _ada(cond, w_ada[l], b_ada[l])
    mod_ctx = mod[0:1].reshape(1, 1, 6 * D_MODEL)
    mod_lat = mod[1:1 + b_lat].reshape(b_lat, 1, 6 * D_MODEL)

    wr = jnp.concatenate([moe_w_r2[l].transpose(1, 0, 2).reshape(D_MODEL, N_EXPERTS), moe_w_r1[l]], axis=1)
    wr = jnp.pad(wr, ((0, 0), (0, LANES - wr.shape[1])))
    br = jnp.concatenate([moe_b_r2[l].reshape(-1), moe_b_r1[l]])
    br = jnp.pad(br, (0, LANES - br.shape[0])).reshape(1, LANES)
    wr_hi, wr_lo = _split(wr)

    wts = {
        'w_in': w_in[l].astype(BF16), 'hy_conv_w': hy_conv_w[l], 'hy_conv_b': hy_conv_b[l],
        'hy_fbias': hy_fbias[l], 'w_glu': s5_w_glu[l].astype(BF16), 'b_glu': s5_b_glu[l].reshape(1, -1),
        'out_norm_g': out_norm_g[l].reshape(1, -1), 'w_out': w_out[l].astype(BF16),
        'ln1_g': ln1_g[l].reshape(1, -1), 'ln1_b': ln1_b[l].reshape(1, -1),
        'wr_hi': wr_hi, 'wr_lo': wr_lo, 'br': br,
        'w_gate': moe_w_gate[l].astype(BF16), 'w_up': moe_w_up[l].astype(BF16),
        'w_down': moe_w_down[l].astype(BF16),
        'ln2_g': ln2_g[l].reshape(1, -1), 'ln2_b': ln2_b[l].reshape(1, -1),
    }

    s5ops = _s5_operators(s5_a_re[l], s5_a_im[l], s5_log_dt[l], s5_b_re[l], s5_b_im[l],
                          s5_c_re[l], s5_c_im[l], s5_d[l])
    tabs_ctx = _tables(l_ctx)
    tabs_lat = _tables(l_lat)
    filt_args = (hy_f_w1[l], hy_f_b1[l], hy_f_w2[l], hy_f_b2[l], hy_f_w3[l], hy_freq[l])
    filt_ctx = _hyena_filters(l_ctx, tabs_ctx, *filt_args)
    filt_lat = _hyena_filters(l_lat, tabs_lat, *filt_args)

    zero = jnp.zeros((g, b_ctx, 2 * p), F32)
    y_ctx, f_re, f_im = _path(x_prompt, None, mod_ctx, zero, zero, tabs_ctx, filt_ctx, s5ops, wts, 512)
    unpack = lambda f: f.reshape(g, b_ctx, 2, p).transpose(1, 2, 0, 3)[:, None]
    new_re, new_im = unpack(f_re), unpack(f_im)

    pack = lambda s: s[:, l].transpose(2, 0, 1, 3).reshape(g, b_lat, 2 * p)
    pos = _grid_pos_embed(l_lat)
    y_lat, _, _ = _path(x_sample, pos, mod_lat, pack(state_s5_re), pack(state_s5_im),
                        tabs_lat, filt_lat, s5ops, wts, 512)
    return (y_ctx, y_lat, new_re, new_im)
```

```python
import functools
import math

import numpy as np
import jax
import jax.numpy as jnp
from jax import lax
from jax.experimental import pallas as pl
from jax.experimental.pallas import tpu as pltpu

F32 = jnp.float32
BF16 = jnp.bfloat16

D_MODEL = 1024
DEPTH = 1
GRID_W = 64
POS_BASE = 10000.0
D_HY = 512
D_S5 = 512
S5_CH = 16
S5_GROUPS = 32
S5_STATE = 64
S5_CHUNK = 16
S5_ROW = S5_CHUNK * S5_CH
HY_BANDS = 16
HY_EMB = 1 + 2 * HY_BANDS
HY_HID = 64
HY_MIN_DECAY = math.log(1e-2) / 1.5
HY_MAX_DECAY = math.log(1e-2) / 0.3
N_EGROUPS = 4
N_EPG = 4
N_EXPERTS = 16
D_EXPERT = 512
LN_EPS = 1e-5
ALPHA = (2.0 * DEPTH) ** 0.25
LANES = 128
S5_GB = LANES // S5_CH
HY_CW = 256
MOE_ST = 256
MOE_SLOTS = 384
MOE_UNIT = 16
MOE_TM = 512
VMEM_LIMIT = 60000 * 1024


def _cparams(sem):
    return pltpu.CompilerParams(dimension_semantics=sem, vmem_limit_bytes=VMEM_LIMIT)


def _split(x):
    hi = x.astype(BF16)
    lo = (x - hi.astype(F32)).astype(BF16)
    return hi, lo


def _dot(a, b):
    return jnp.dot(a, b, preferred_element_type=F32)


def _dot_t(a, b):
    return lax.dot_general(a, b, (((1,), (1,)), ((), ())), preferred_element_type=F32)


def _mm3(a, b):
    ah, al = _split(a)
    bh, bl = _split(b)
    return _dot(ah, bh) + _dot(al, bh) + _dot(ah, bl)


def _mm3_pre(ah, al, b):
    bh, bl = _split(b)
    return _dot(ah, bh) + _dot(al, bh) + _dot(ah, bl)


def _mm3_t(a, b):
    ah, al = _split(a)
    bh, bl = _split(b)
    return _dot_t(ah, bh) + _dot_t(al, bh) + _dot_t(ah, bl)


def _dot_hp(a, b):
    return jnp.dot(a, b, preferred_element_type=F32, precision=lax.Precision.HIGHEST)


def _norm(x):
    xc = x - jnp.mean(x, axis=-1, keepdims=True)
    return xc * lax.rsqrt(jnp.mean(xc * xc, axis=-1, keepdims=True) + LN_EPS)


def _rms(y):
    return y * lax.rsqrt(jnp.mean(y * y, axis=-1, keepdims=True) + LN_EPS)


def _ada_kernel(cond_ref, w_ref, b_ref, o_ref):
    c = jax.nn.silu(cond_ref[...])
    o_ref[...] = _mm3(c, w_ref[...]) + b_ref[...]


def _ada(cond, w_ada, b_ada):
    nb = cond.shape[0]
    n = w_ada.shape[1]
    tn = 1024
    return pl.pallas_call(
        _ada_kernel,
        grid=(n // tn,),
        in_specs=[pl.BlockSpec((nb, D_MODEL), lambda j: (0, 0)),
                  pl.BlockSpec((D_MODEL, tn), lambda j: (0, j)),
                  pl.BlockSpec((1, tn), lambda j: (0, j))],
        out_specs=pl.BlockSpec((nb, tn), lambda j: (0, j)),
        out_shape=jax.ShapeDtypeStruct((nb, n), F32),
        compiler_params=_cparams(("arbitrary",)),
        name="ada",
    )(cond, w_ada, b_ada.reshape(1, n))


def _dft_tables(n_tok):
    n = 2 * n_tok
    idx = np.arange(n_tok, dtype=np.int64)
    m = (idx[:, None] * idx[None, :]) % n
    ang = 2.0 * np.pi * m.astype(np.float64) / n
    cm = np.cos(ang)
    sm = -np.sin(ang)
    sm[0, :] = 1.0 - 2.0 * (idx % 2)
    return cm.astype(np.float32), sm.astype(np.float32)


def _filt_kernel(n_tok, z_ref, t_ref, w1_ref, b1_ref, w2_ref, b2_ref, fr_ref, w3f_ref, w3b_ref,
                 dl_ref, cmh_ref, cml_ref, smh_ref, sml_ref, a_ref, bz_ref, dd_ref):
    fr = fr_ref[...]
    h = jnp.sin(fr * (_dot_hp(z_ref[...], w1_ref[...]) + b1_ref[...]))
    h = jnp.sin(fr * (_dot_hp(h, w2_ref[...]) + b2_ref[...]))
    decay = jnp.exp(-t_ref[...] * dl_ref[...])
    row = lax.broadcasted_iota(jnp.int32, decay.shape, 0)
    hf = _dot_hp(h, w3f_ref[...]) * decay
    hb = jnp.where(row == 0, 0.0, _dot_hp(h, w3b_ref[...]) * decay)
    p = hf + hb
    q = hf - hb
    k_re = _mm3_pre(cmh_ref[...], cml_ref[...], p)
    k_im = _mm3_pre(smh_ref[...], sml_ref[...], q)
    sign = jnp.where(row % 2 == 0, 1.0, -1.0)
    nyq = jnp.sum(p * sign, axis=0, keepdims=True)
    inv_n = 1.0 / (2 * n_tok)
    w = jnp.where(row == 0, inv_n, 2.0 * inv_n)
    a_ref[...] = w * k_re
    bz_ref[...] = jnp.where(row == 0, 0.0, w * k_im)
    dd_ref[...] = jnp.where(row == 0, nyq * inv_n, w * k_re)


def _hyena_filters(n_tok, tabs, hy_f_w1, hy_f_b1, hy_f_w2, hy_f_b2, hy_f_w3, hy_freq):
    cmh, cml, smh, sml = tabs[:4]
    t = jnp.linspace(0.0, 1.0, n_tok, dtype=F32)[:, None]
    wv = 2.0 * math.pi * jnp.arange(n_tok, dtype=F32) / n_tok
    fb = jnp.linspace(1e-4, HY_BANDS - 1, HY_BANDS, dtype=F32)
    ang = wv[:, None] * fb[None, :]
    z = jnp.concatenate([t, jnp.cos(ang), -jnp.sin(ang)], axis=-1)
    z = jnp.pad(z, ((0, 0), (0, LANES - HY_EMB)))
    w1 = jnp.pad(hy_f_w1, ((0, LANES - HY_EMB), (0, 0)))
    deltas = jnp.abs(jnp.linspace(HY_MIN_DECAY, HY_MAX_DECAY, D_HY, dtype=F32))[None, :]
    ncb = D_HY // HY_CW
    full = lambda j: (0, 0)
    out_sd = jax.ShapeDtypeStruct((n_tok, 2 * D_HY), F32)
    mat = pl.BlockSpec((n_tok, n_tok), full, pipeline_mode=pl.Buffered(1))
    return pl.pallas_call(
        functools.partial(_filt_kernel, n_tok),
        grid=(2 * ncb,),
        in_specs=[pl.BlockSpec((n_tok, LANES), full),
                  pl.BlockSpec((n_tok, 1), full),
                  pl.BlockSpec((LANES, HY_HID), full),
                  pl.BlockSpec((1, HY_HID), full),
                  pl.BlockSpec((HY_HID, HY_HID), full),
                  pl.BlockSpec((1, HY_HID), full),
                  pl.BlockSpec((1, HY_HID), full),
                  pl.BlockSpec((HY_HID, HY_CW), lambda j: (0, 2 * ncb * (j // ncb) + j % ncb)),
                  pl.BlockSpec((HY_HID, HY_CW), lambda j: (0, 2 * ncb * (j // ncb) + ncb + j % ncb)),
                  pl.BlockSpec((1, HY_CW), lambda j: (0, j % ncb)),
                  mat, mat, mat, mat],
        out_specs=[pl.BlockSpec((n_tok, HY_CW), lambda j: (0, j))] * 3,
        out_shape=[out_sd] * 3,
        compiler_params=_cparams(("arbitrary",)),
        name=f"filt{n_tok}",
    )(z, t, w1, hy_f_b1.reshape(1, -1), hy_f_w2, hy_f_b2.reshape(1, -1), hy_freq.reshape(1, -1),
      hy_f_w3, hy_f_w3, deltas, cmh, cml, smh, sml)


def _hyena_kernel(pv_ref, p1_ref, p2_ref, cwv_ref, cw1_ref, cw2_ref, cbv_ref, cb1_ref, cb2_ref,
                  fbias_ref, cm_ref, sm_ref, st_ref,
                  a0_ref, b0_ref, d0_ref, a1_ref, b1_ref, d1_ref, o_ref):
    n_tok = pv_ref.shape[1]
    row = lax.broadcasted_iota(jnp.int32, (n_tok, pv_ref.shape[2]), 0)

    def short_conv(p_ref, cw_ref, cb_ref):
        p = p_ref[0]
        prev = jnp.where(row == 0, 0.0, pltpu.roll(p, 1, axis=0))
        nxt = jnp.where(row == n_tok - 1, 0.0, pltpu.roll(p, n_tok - 1, axis=0))
        return cb_ref[...] + prev * cw_ref[0:1, :] + p * cw_ref[1:2, :] + nxt * cw_ref[2:3, :]

    cm = cm_ref[...]

    def fftconv(u, a_ref, b_ref, d_ref, skip):
        ub = u.astype(BF16)
        u_re = _dot(cm, ub)
        u_im = _dot(sm_ref[...], ub)
        a, bz, dd = a_ref[...], b_ref[...], d_ref[...]
        y_re = u_re * a - u_im * bz
        y_im = u_re * bz + u_im * dd
        y = _dot(cm, y_re.astype(BF16)) + _dot(st_ref[...], y_im.astype(BF16))
        return y + u * skip

    v = short_conv(pv_ref, cwv_ref, cbv_ref)
    x1 = short_conv(p1_ref, cw1_ref, cb1_ref)
    z = x1 * fftconv(v, a0_ref, b0_ref, d0_ref, fbias_ref[0:1, :])
    x2 = short_conv(p2_ref, cw2_ref, cb2_ref)
    o_ref[0] = x2 * fftconv(z, a1_ref, b1_ref, d1_ref, fbias_ref[1:2, :])


def _hyena(proj_hy, tabs, filt, hy_conv_w, hy_conv_b, hy_fbias):
    bsz, n_tok, _ = proj_hy.shape
    ncb = D_HY // HY_CW
    cmh, _, smh, _, sth, _ = tabs
    fa, fbz, fdd = filt
    cb = hy_conv_b.reshape(1, -1)
    mat = pl.BlockSpec((n_tok, n_tok), lambda b, c: (0, 0), pipeline_mode=pl.Buffered(1))

    def pspec(k):
        return pl.BlockSpec((1, n_tok, HY_CW), lambda b, c: (b, 0, k * ncb + c))

    def cwspec(k):
        return pl.BlockSpec((3, HY_CW), lambda b, c: (0, k * ncb + c))

    def cbspec(k):
        return pl.BlockSpec((1, HY_CW), lambda b, c: (0, k * ncb + c))

    def fspec(o):
        return pl.BlockSpec((n_tok, HY_CW), lambda b, c: (0, o * ncb + c))

    return pl.pallas_call(
        _hyena_kernel,
        grid=(bsz, ncb),
        in_specs=[pspec(0), pspec(1), pspec(2), cwspec(0), cwspec(1), cwspec(2),
                  cbspec(0), cbspec(1), cbspec(2),
                  pl.BlockSpec((2, HY_CW), lambda b, c: (0, c)),
                  mat, mat, mat,
                  fspec(0), fspec(0), fspec(0), fspec(1), fspec(1), fspec(1)],
        out_specs=pl.BlockSpec((1, n_tok, HY_CW), lambda b, c: (b, 0, c)),
        out_shape=jax.ShapeDtypeStruct((bsz, n_tok, D_HY), F32),
        compiler_params=_cparams(("arbitrary", "arbitrary")),
        name=f"hyena{n_tok}",
    )(proj_hy, proj_hy, proj_hy, hy_conv_w, hy_conv_w, hy_conv_w, cb, cb, cb, hy_fbias,
      cmh, smh, sth, fa, fbz, fdd, fa, fbz, fdd)


def _s5ops_kernel(are_ref, aim_ref, ldt_ref, btr_ref, bti_ref, cre_ref, cim_ref,
                  er_ref, ei_ref, gr_ref, gi_ref, kf_ref, kb_ref, atr_ref, ati_ref):
    a_re, a_im = are_ref[0], aim_ref[0]
    dt = jnp.exp(ldt_ref[0])
    mag = jnp.exp(a_re * dt)
    ab_re = mag * jnp.cos(a_im * dt)
    ab_im = mag * jnp.sin(a_im * dt)
    n_re, n_im = ab_re - 1.0, ab_im
    den = a_re * a_re + a_im * a_im
    q_re = (n_re * a_re + n_im * a_im) / den
    q_im = (n_im * a_re - n_re * a_im) / den
    bt_re, bt_im = btr_ref[0], bti_ref[0]
    bb_re = q_re * bt_re - q_im * bt_im
    bb_im = q_re * bt_im + q_im * bt_re
    c_re, c_im = cre_ref[0], cim_ref[0]
    pw = [(jnp.ones_like(ab_re), jnp.zeros_like(ab_re))]
    for _ in range(S5_CHUNK):
        pr, pi = pw[-1]
        pw.append((pr * ab_re - pi * ab_im, pr * ab_im + pi * ab_re))
    lane = lax.broadcasted_iota(jnp.int32, ab_re.shape, 1)
    fwd = lane < S5_STATE
    for s in range(S5_CHUNK):
        e_re = jnp.where(fwd, pw[S5_CHUNK - 1 - s][0], pw[s][0])
        e_im = jnp.where(fwd, pw[S5_CHUNK - 1 - s][1], pw[s][1])
        er_ref[0, pl.ds(S5_CH * s, S5_CH), :] = e_re * bb_re - e_im * bb_im
        ei_ref[0, pl.ds(S5_CH * s, S5_CH), :] = e_re * bb_im + e_im * bb_re
        g_re = jnp.where(fwd, pw[s + 1][0], pw[S5_CHUNK - s][0])
        g_im = jnp.where(fwd, pw[s + 1][1], pw[S5_CHUNK - s][1])
        gr_ref[0, pl.ds(S5_CH * s, S5_CH), :] = c_re * g_re - c_im * g_im
        gi_ref[0, pl.ds(S5_CH * s, S5_CH), :] = -(c_re * g_im + c_im * g_re)
    er, ei = er_ref[0], ei_ref[0]
    lane2 = lax.broadcasted_iota(jnp.int32, er.shape, 1)
    f2 = lane2 < S5_STATE
    zero = jnp.zeros_like(er)

    def dot_hp_t(a, b):
        return lax.dot_general(a, b, (((1,), (1,)), ((), ())), preferred_element_type=F32,
                               precision=lax.Precision.HIGHEST)

    kf_ref[0] = dot_hp_t(jnp.where(f2, er, zero), c_re) - dot_hp_t(jnp.where(f2, ei, zero), c_im)
    kb_ref[0] = dot_hp_t(jnp.where(f2, zero, er), c_re) - dot_hp_t(jnp.where(f2, zero, ei), c_im)
    atr_ref[0] = pw[S5_CHUNK][0]
    ati_ref[0] = pw[S5_CHUNK][1]


def _s5_operators(s5_a_re, s5_a_im, s5_log_dt, s5_b_re, s5_b_im, s5_c_re, s5_c_im, s5_d):
    g, p, h = S5_GROUPS, S5_STATE, S5_CH
    cat = lambda x: jnp.concatenate([x[0], x[1]], axis=-1)
    a_re = cat(s5_a_re).reshape(g, 1, 2 * p)
    a_im = cat(s5_a_im).reshape(g, 1, 2 * p)
    ldt = cat(jnp.broadcast_to(s5_log_dt[:, :, None], (2, g, p))).reshape(g, 1, 2 * p)
    bt_re = cat(jnp.swapaxes(s5_b_re, -1, -2))
    bt_im = cat(jnp.swapaxes(s5_b_im, -1, -2))
    c_re = jnp.concatenate([s5_c_re, s5_c_re], axis=-1)
    c_im = jnp.concatenate([s5_c_im, s5_c_im], axis=-1)
    vec = pl.BlockSpec((1, 1, 2 * p), lambda i: (i, 0, 0))
    hp = pl.BlockSpec((1, h, 2 * p), lambda i: (i, 0, 0))
    big = pl.BlockSpec((1, S5_ROW, 2 * p), lambda i: (i, 0, 0))
    kk = pl.BlockSpec((1, S5_ROW, h), lambda i: (i, 0, 0))
    big_sd = jax.ShapeDtypeStruct((g, S5_ROW, 2 * p), F32)
    kk_sd = jax.ShapeDtypeStruct((g, S5_ROW, h), F32)
    vec_sd = jax.ShapeDtypeStruct((g, 1, 2 * p), F32)
    er, ei, gr, gi, kf, kb, at_re, at_im = pl.pallas_call(
        _s5ops_kernel,
        grid=(g,),
        in_specs=[vec, vec, vec, hp, hp, hp, hp],
        out_specs=[big, big, big, big, kk, kk, vec, vec],
        out_shape=[big_sd, big_sd, big_sd, big_sd, kk_sd, kk_sd, vec_sd, vec_sd],
        compiler_params=_cparams(("arbitrary",)),
        name="s5ops",
    )(a_re, a_im, ldt, bt_re, bt_im, c_re, c_im)
    kf_lag = kf.reshape(g, S5_CHUNK, h, h)[:, ::-1]
    kb_lag = kb.reshape(g, S5_CHUNK, h, h)
    lag = np.arange(S5_CHUNK)[None, :] - np.arange(S5_CHUNK)[:, None]
    mf = kf_lag[:, np.clip(lag, 0, S5_CHUNK - 1)] * jnp.asarray(lag >= 0, F32)[None, :, :, None, None]
    mb = kb_lag[:, np.clip(-lag, 0, S5_CHUNK - 1)] * jnp.asarray(lag <= 0, F32)[None, :, :, None, None]
    mt = (mf + mb).transpose(0, 1, 3, 2, 4).reshape(g, S5_ROW, S5_ROW)
    dvec = jnp.tile(s5_d.reshape(g, 1, h), (1, S5_CHUNK, 1)).reshape(g, S5_ROW)
    mt = mt + jnp.eye(S5_ROW, dtype=F32)[None] * dvec[:, None, :]
    return mt, er, ei, gr, gi, at_re, at_im


def _s5_kernel(bsz, n_chunks, u_ref, mt_ref, er_ref, ei_ref, gr_ref, gi_ref, atr_ref, ati_ref,
               h0r_ref, h0i_ref, y_ref, fr_ref, fi_ref, u8_ref, y8_ref, sr_ref, si_ref,
               xfr_ref, xfi_ref, xbr_ref, xbi_ref):
    nc = n_chunks
    steps_per_vreg = LANES // S5_CH

    def to_chunks(b, carry):
        r0 = pl.multiple_of(b * nc, 8)
        for t in range(S5_CHUNK):
            a = u_ref[b, pl.ds(t, nc, stride=S5_CHUNK), :]
            off = S5_CH * (t % steps_per_vreg)
            base = LANES * (t // steps_per_vreg)
            for k in range(S5_GB):
                shift = (off - S5_CH * k) % LANES
                r = pltpu.roll(a, shift, axis=1) if shift else a
                u8_ref[k, pl.ds(r0, nc), base + off:base + off + S5_CH] = r[:, off:off + S5_CH]
        return carry

    lax.fori_loop(0, bsz, to_chunks, 0)

    lane = lax.broadcasted_iota(jnp.int32, (bsz, 2 * S5_STATE), 1)
    fwd = lane < S5_STATE
    lane_all = lax.broadcasted_iota(jnp.int32, (bsz * nc, 2 * S5_STATE), 1)
    fwd_all = lane_all < S5_STATE

    def group(k, carry):
        uh, ul = _split(u8_ref[k])
        sr_ref[...] = _mm3_pre(uh, ul, er_ref[k])
        si_ref[...] = _mm3_pre(uh, ul, ei_ref[k])
        at_re, at_im = atr_ref[k], ati_ref[k]

        def step(i, xc):
            x_re, x_im = xc
            rf = pl.ds(i, bsz, stride=nc)
            rb = pl.ds(nc - 1 - i, bsz, stride=nc)
            xfr_ref[rf, :] = x_re
            xfi_ref[rf, :] = x_im
            xbr_ref[rb, :] = x_re
            xbi_ref[rb, :] = x_im
            s_re = jnp.where(fwd, sr_ref[rf, :], sr_ref[rb, :])
            s_im = jnp.where(fwd, si_ref[rf, :], si_ref[rb, :])
            return (at_re * x_re - at_im * x_im + s_re, at_re * x_im + at_im * x_re + s_im)

        x_re, x_im = lax.fori_loop(0, nc, step, (h0r_ref[k], h0i_ref[k]))
        fr_ref[k] = x_re
        fi_ref[k] = x_im
        xp_re = jnp.where(fwd_all, xfr_ref[...], xbr_ref[...])
        xp_im = jnp.where(fwd_all, xfi_ref[...], xbi_ref[...])
        y8_ref[k] = (_mm3_pre(uh, ul, mt_ref[k]) + _mm3_t(xp_re, gr_ref[k]) + _mm3_t(xp_im, gi_ref[k]))
        return carry

    lax.fori_loop(0, S5_GB, group, 0)

    lane_tok = lax.broadcasted_iota(jnp.int32, (nc, LANES), 1)

    def to_tokens(b, carry):
        r0 = pl.multiple_of(b * nc, 8)
        for t in range(S5_CHUNK):
            off = S5_CH * (t % steps_per_vreg)
            base = LANES * (t // steps_per_vreg)
            acc = None
            for k in range(S5_GB):
                src = y8_ref[k, pl.ds(r0, nc), base:base + LANES]
                shift = (S5_CH * k - off) % LANES
                r = pltpu.roll(src, shift, axis=1) if shift else src
                sel = (lane_tok >= S5_CH * k) & (lane_tok < S5_CH * (k + 1))
                acc = jnp.where(sel, r, 0.0 if acc is None else acc)
            y_ref[b, pl.ds(t, nc, stride=S5_CHUNK), :] = acc
        return carry

    lax.fori_loop(0, bsz, to_tokens, 0)


def _s5(u, ops, h0_re, h0_im):
    bsz, n_tok, _ = u.shape
    mt, er, ei, gr, gi, at_re, at_im = ops
    g, p = S5_GROUPS, S5_STATE
    nc = n_tok // S5_CHUNK
    rows = nc * bsz
    tok = pl.BlockSpec((bsz, n_tok, LANES), lambda j: (0, 0, j))
    gspec = lambda shape: pl.BlockSpec((S5_GB,) + shape, lambda j: (j, 0, 0))
    return pl.pallas_call(
        functools.partial(_s5_kernel, bsz, nc),
        grid=(g // S5_GB,),
        in_specs=[tok, gspec((S5_ROW, S5_ROW)), gspec((S5_ROW, 2 * p)),
                  gspec((S5_ROW, 2 * p)), gspec((S5_ROW, 2 * p)), gspec((S5_ROW, 2 * p)),
                  gspec((1, 2 * p)), gspec((1, 2 * p)), gspec((bsz, 2 * p)), gspec((bsz, 2 * p))],
        out_specs=[tok, gspec((bsz, 2 * p)), gspec((bsz, 2 * p))],
        out_shape=[jax.ShapeDtypeStruct((bsz, n_tok, D_S5), F32),
                   jax.ShapeDtypeStruct((g, bsz, 2 * p), F32),
                   jax.ShapeDtypeStruct((g, bsz, 2 * p), F32)],
        scratch_shapes=([pltpu.VMEM((S5_GB, rows, S5_ROW), F32)] * 2
                        + [pltpu.VMEM((rows, 2 * p), F32)] * 6),
        compiler_params=_cparams(("arbitrary",)),
        name=f"s5_{n_tok}",
    )(u, mt, er, ei, gr, gi, at_re, at_im, h0_re, h0_im)


def _in_kernel(has_pos, *refs):
    if has_pos:
        x_ref, pos_ref, mod_ref, w_ref, hy_ref, s5_ref = refs
        x = x_ref[0] + pos_ref[...]
    else:
        x_ref, mod_ref, w_ref, hy_ref, s5_ref = refs
        x = x_ref[0]
    sh1 = mod_ref[0, :, 0:D_MODEL]
    sc1 = mod_ref[0, :, D_MODEL:2 * D_MODEL]
    h = _norm(x) * (1.0 + sc1) + sh1
    proj = _dot(h.astype(BF16), w_ref[...])
    hy_ref[0] = proj[:, :3 * D_HY]
    s5_ref[0] = proj[:, 3 * D_HY:]


def _in_proj(x3, pos, mod3, w_in_bf, tm):
    nb, lt, _ = x3.shape
    has_pos = pos is not None
    per_batch = mod3.shape[0] > 1
    midx = (lambda b, i: (b, 0, 0)) if per_batch else (lambda b, i: (0, 0, 0))
    in_specs = [pl.BlockSpec((1, tm, D_MODEL), lambda b, i: (b, i, 0))]
    args = [x3]
    if has_pos:
        in_specs.append(pl.BlockSpec((tm, D_MODEL), lambda b, i: (i, 0)))
        args.append(pos)
    in_specs += [pl.BlockSpec((1, 1, 6 * D_MODEL), midx),
                 pl.BlockSpec((D_MODEL, 3 * D_HY + D_S5), lambda b, i: (0, 0))]
    args += [mod3, w_in_bf]
    return pl.pallas_call(
        functools.partial(_in_kernel, has_pos),
        grid=(nb, lt // tm),
        in_specs=in_specs,
        out_specs=[pl.BlockSpec((1, tm, 3 * D_HY), lambda b, i: (b, i, 0)),
                   pl.BlockSpec((1, tm, D_S5), lambda b, i: (b, i, 0))],
        out_shape=[jax.ShapeDtypeStruct((nb, lt, 3 * D_HY), F32),
                   jax.ShapeDtypeStruct((nb, lt, D_S5), F32)],
        compiler_params=_cparams(("arbitrary", "arbitrary")),
        name=f"in_proj{nb}",
    )(*args)


def _route(logits):
    lane = lax.broadcasted_iota(jnp.int32, logits.shape, 1)
    lane_f = lane.astype(F32)
    neg = -jnp.inf
    big = float(LANES)
    m1 = (lane >= N_EXPERTS) & (lane < N_EXPERTS + N_EGROUPS)
    l1 = jnp.where(m1, logits, neg)
    top1 = jnp.max(l1, axis=-1, keepdims=True)
    grp = jnp.min(jnp.where(l1 == top1, lane_f, big), axis=-1, keepdims=True) - float(N_EXPERTS)
    den = jnp.sum(jnp.where(m1, jnp.exp(logits - top1), 0.0), axis=-1, keepdims=True)
    p_grp = 1.0 / den
    lo = grp * float(N_EPG)
    m2 = (lane_f >= lo) & (lane_f < lo + float(N_EPG))
    l2 = jnp.where(m2, logits, neg)
    v1 = jnp.max(l2, axis=-1, keepdims=True)
    i1 = jnp.min(jnp.where(l2 == v1, lane_f, big), axis=-1, keepdims=True)
    l2b = jnp.where(lane_f == i1, neg, l2)
    v2 = jnp.max(l2b, axis=-1, keepdims=True)
    i2 = jnp.min(jnp.where(l2b == v2, lane_f, big), axis=-1, keepdims=True)
    e = jnp.exp(v2 - v1)
    w1 = 1.0 / (1.0 + e)
    w2 = e / (1.0 + e)
    gates = jnp.where(lane_f == i1, w1 * p_grp, 0.0) + jnp.where(lane_f == i2, w2 * p_grp, 0.0)
    return jnp.where(lane_f == grp + float(N_EXPERTS), 1.0, gates)


def _out_kernel(has_pos, *refs):
    if has_pos:
        x_ref, pos_ref = refs[:2]
        rest = refs[2:]
        x = x_ref[0] + pos_ref[...]
    else:
        x_ref = refs[0]
        rest = refs[1:]
        x = x_ref[0]
    (yhy_ref, ys5_ref, mod_ref, wglu_ref, bglu_ref, ong_ref, wout_ref, ln1g_ref, ln1b_ref,
     wrh_ref, wrl_ref, br_ref, x1_ref, h2_ref, gate_ref) = rest
    y = ys5_ref[0]
    s5 = jax.nn.gelu(y) * jax.nn.sigmoid(_dot(y.astype(BF16), wglu_ref[...]) + bglu_ref[...])
    m_hy = _rms(yhy_ref[0]) * ong_ref[:, 0:D_HY]
    m_s5 = _rms(s5) * ong_ref[:, D_HY:]
    o = (_dot(m_hy.astype(BF16), wout_ref[0:D_HY, :]) + _dot(m_s5.astype(BF16), wout_ref[D_HY:, :]))
    g1 = mod_ref[0, :, 2 * D_MODEL:3 * D_MODEL]
    sh2 = mod_ref[0, :, 3 * D_MODEL:4 * D_MODEL]
    sc2 = mod_ref[0, :, 4 * D_MODEL:5 * D_MODEL]
    x1 = _norm(ALPHA * x + g1 * o) * ln1g_ref[...] + ln1b_ref[...]
    x1_ref[0] = x1
    h2 = _norm(x1) * (1.0 + sc2) + sh2
    h2_ref[0] = h2.astype(BF16)
    hh, hl = _split(h2)
    logits = (_dot(hh, wrh_ref[...]) + _dot(hl, wrh_ref[...]) + _dot(hh, wrl_ref[...]) + br_ref[...])
    gate_ref[0] = _route(logits)


def _out_proj(x3, pos, yhy3, ys53, mod3, wglu_bf, bglu, ong, wout_bf, ln1g, ln1b, wr_hi, wr_lo, br, tm):
    nb, lt, _ = x3.shape
    has_pos = pos is not None
    per_batch = mod3.shape[0] > 1
    midx = (lambda b, i: (b, 0, 0)) if per_batch else (lambda b, i: (0, 0, 0))
    tok = lambda w: pl.BlockSpec((1, tm, w), lambda b, i: (b, i, 0))
    full = lambda shape: pl.BlockSpec(shape, lambda b, i: (0,) * len(shape))
    in_specs = [tok(D_MODEL)]
    args = [x3]
    if has_pos:
        in_specs.append(pl.BlockSpec((tm, D_MODEL), lambda b, i: (i, 0)))
        args.append(pos)
    in_specs += [tok(D_HY), tok(D_S5), pl.BlockSpec((1, 1, 6 * D_MODEL), midx),
                 full((D_S5, D_S5)), full((1, D_S5)), full((1, D_MODEL)), full((D_MODEL, D_MODEL)),
                 full((1, D_MODEL)), full((1, D_MODEL)), full((D_MODEL, LANES)), full((D_MODEL, LANES)),
                 full((1, LANES))]
    args += [yhy3, ys53, mod3, wglu_bf, bglu, ong, wout_bf, ln1g, ln1b, wr_hi, wr_lo, br]
    return pl.pallas_call(
        functools.partial(_out_kernel, has_pos),
        grid=(nb, lt // tm),
        in_specs=in_specs,
        out_specs=[tok(D_MODEL), tok(D_MODEL), tok(LANES)],
        out_shape=[jax.ShapeDtypeStruct((nb, lt, D_MODEL), F32),
                   jax.ShapeDtypeStruct((nb, lt, D_MODEL), BF16),
                   jax.ShapeDtypeStruct((nb, lt, LANES), F32)],
        compiler_params=_cparams(("arbitrary", "arbitrary")),
        name=f"out_proj{nb}",
    )(*args)


def _perm_t(gates, loc_ref, s):
    n = gates.shape[0]
    lane = lax.broadcasted_iota(jnp.int32, gates.shape, 1)
    oh = jnp.where((lane >= N_EXPERTS) & (lane < N_EXPERTS + N_EGROUPS), gates, 0.0)
    r = lax.broadcasted_iota(jnp.int32, (n, n), 0)
    c = lax.broadcasted_iota(jnp.int32, (n, n), 1)
    earlier = jnp.where(c < r, 1.0, 0.0).astype(BF16)
    cum = _dot(earlier, oh.astype(BF16))
    rank = jnp.sum(cum * oh, axis=-1, keepdims=True)
    lane1 = lax.broadcasted_iota(jnp.int32, (1, LANES), 1)
    locv = jnp.zeros((1, LANES), F32)
    for grp in range(N_EGROUPS):
        locv = jnp.where(lane1 == N_EXPERTS + grp, loc_ref[N_EGROUPS * s + grp].astype(F32), locv)
    dest = rank + jnp.sum(oh * locv, axis=-1, keepdims=True)
    slot = lax.broadcasted_iota(jnp.int32, (n, MOE_SLOTS), 1).astype(F32)
    return jnp.where(slot == dest, 1.0, 0.0)


def _segment_copies(s, loc_ref, len_ref, off_ref, make):
    for grp in range(N_EGROUPS):
        loc = loc_ref[N_EGROUPS * s + grp]
        off = off_ref[N_EGROUPS * s + grp]
        n_units = len_ref[N_EGROUPS * s + grp] // MOE_UNIT

        def body(i, carry):
            make(pl.multiple_of(loc + MOE_UNIT * i, MOE_UNIT), pl.multiple_of(off + MOE_UNIT * i, MOE_UNIT))
            return carry

        lax.fori_loop(0, n_units, body, 0)


def _moe_sort_kernel(loc_ref, len_ref, off_ref, h_ref, gate_ref, xs_in, gs_in, xs_hbm, gs_hbm,
                     xs_v, gs_v, sem):
    del xs_in, gs_in
    s = pl.program_id(0)
    gates = gate_ref[...]
    p = _perm_t(gates, loc_ref, s).T.astype(BF16)
    xs_v[...] = _dot(p, h_ref[...]).astype(BF16)
    g_hi = gates.astype(BF16)
    r1 = gates - g_hi.astype(F32)
    g_mid = r1.astype(BF16)
    g_lo = (r1 - g_mid.astype(F32)).astype(BF16)
    gs_v[...] = _dot(p, g_hi) + _dot(p, g_mid) + _dot(p, g_lo)

    def x_copy(lr, gr):
        return pltpu.make_async_copy(xs_v.at[pl.ds(lr, MOE_UNIT), :], xs_hbm.at[pl.ds(gr, MOE_UNIT), :],
                                     sem.at[0])

    def g_copy(lr, gr):
        return pltpu.make_async_copy(gs_v.at[pl.ds(lr, MOE_UNIT), :], gs_hbm.at[pl.ds(gr, MOE_UNIT), :],
                                     sem.at[1])

    def start(lr, gr):
        x_copy(lr, gr).start()
        g_copy(lr, gr).start()

    def wait(lr, gr):
        x_copy(lr, gr).wait()
        g_copy(lr, gr).wait()

    _segment_copies(s, loc_ref, len_ref, off_ref, start)
    _segment_copies(s, loc_ref, len_ref, off_ref, wait)


def _moe_expert_kernel(bg_ref, nb_ref, xs_ref, gs_ref, wg_ref, wu_ref, wd_ref, o_ref):
    i = pl.program_id(0)

    @pl.when(i < nb_ref[0])
    def _():
        grp = bg_ref[i]
        x = xs_ref[...]
        gates = gs_ref[...]
        lane = lax.broadcasted_iota(jnp.int32, gates.shape, 1)
        acc = jnp.zeros(o_ref.shape, F32)
        for e in range(N_EPG):
            a = _dot(x, wg_ref[e])
            u = _dot(x, wu_ref[e])
            ge = jnp.sum(jnp.where(lane == N_EPG * grp + e, gates, 0.0), axis=-1, keepdims=True)
            hid = jax.nn.silu(a) * u * ge
            acc = acc + _dot(hid.astype(BF16), wd_ref[e])
        o_ref[...] = acc

    @pl.when(i >= nb_ref[0])
    def _():
        o_ref[...] = jnp.zeros_like(o_ref)


def _moe_combine_kernel(loc_ref, len_ref, off_ref, gate_ref, x1_ref, mod_ref, ln2g_ref, ln2b_ref, o_hbm,
                        ctx_ref, lat_ref, o_v, sem, *, n_ctx_tiles):
    s = pl.program_id(0)

    @pl.when(s == 0)
    def _():
        o_v[...] = jnp.zeros_like(o_v)

    def o_copy(lr, gr):
        return pltpu.make_async_copy(o_hbm.at[pl.ds(gr, MOE_UNIT), :], o_v.at[pl.ds(lr, MOE_UNIT), :],
                                     sem.at[0])

    _segment_copies(s, loc_ref, len_ref, off_ref, lambda lr, gr: o_copy(lr, gr).start())
    pt = _perm_t(gate_ref[...], loc_ref, s).astype(BF16)
    _segment_copies(s, loc_ref, len_ref, off_ref, lambda lr, gr: o_copy(lr, gr).wait())
    oh, ol = _split(o_v[...])
    f = _dot(pt, oh) + _dot(pt, ol)
    g2 = mod_ref[0, :, 5 * D_MODEL:6 * D_MODEL]
    x2 = _norm(ALPHA * x1_ref[...] + g2 * f) * ln2g_ref[...] + ln2b_ref[...]

    @pl.when(s < n_ctx_tiles)
    def _():
        ctx_ref[...] = x2

    @pl.when(s >= n_ctx_tiles)
    def _():
        lat_ref[...] = x2


def _moe_plan(gates_all, n_blocks):
    n_tiles = gates_all.shape[0] // MOE_ST
    oh = gates_all[:, N_EXPERTS:N_EXPERTS + N_EGROUPS]
    cnt = jnp.sum(oh.reshape(n_tiles, MOE_ST, N_EGROUPS), axis=1).astype(jnp.int32)
    len16 = ((cnt + MOE_UNIT - 1) // MOE_UNIT) * MOE_UNIT
    loc = jnp.cumsum(len16, axis=1) - len16
    rows_g = jnp.sum(len16, axis=0)
    reg_g = ((rows_g + MOE_TM - 1) // MOE_TM) * MOE_TM
    reg_start = jnp.cumsum(reg_g) - reg_g
    off = reg_start[None, :] + jnp.cumsum(len16, axis=0) - len16
    blk_end = jnp.cumsum(reg_g // MOE_TM)
    bi = jnp.arange(n_blocks, dtype=jnp.int32)
    blk_group = jnp.minimum(jnp.sum((bi[:, None] >= blk_end[None, :]).astype(jnp.int32), axis=1),
                            N_EGROUPS - 1)
    flat = lambda a: a.reshape(-1).astype(jnp.int32)
    return flat(loc), flat(len16), flat(off), blk_group.astype(jnp.int32), blk_end[-1:].astype(jnp.int32)


def _moe(h2_all, gates_all, x1_all, mod, wg_bf, wu_bf, wd_bf, ln2g, ln2b, n_ctx, tokens_per_mod_row):
    n_tok = h2_all.shape[0]
    n_tiles = n_tok // MOE_ST
    n_ctx_tiles = n_ctx // MOE_ST
    max_rows = n_tok + n_tiles * N_EGROUPS * (MOE_UNIT - 1) + N_EGROUPS * (MOE_TM - 1)
    n_blocks = -(-max_rows // MOE_TM)
    n_rows = n_blocks * MOE_TM
    loc, len16, off, blk_group, n_used = _moe_plan(gates_all, n_blocks)

    tile = lambda w: pl.BlockSpec((MOE_ST, w), lambda s, *_: (s, 0))
    anyspec = pl.BlockSpec(memory_space=pl.ANY)
    xs, gs = pl.pallas_call(
        _moe_sort_kernel,
        grid_spec=pltpu.PrefetchScalarGridSpec(
            num_scalar_prefetch=3, grid=(n_tiles,),
            in_specs=[tile(D_MODEL), tile(LANES), anyspec, anyspec],
            out_specs=[anyspec, anyspec],
            scratch_shapes=[pltpu.VMEM((MOE_SLOTS, D_MODEL), BF16), pltpu.VMEM((MOE_SLOTS, LANES), F32),
                            pltpu.SemaphoreType.DMA((2,))]),
        out_shape=[jax.ShapeDtypeStruct((n_rows, D_MODEL), BF16),
                   jax.ShapeDtypeStruct((n_rows, LANES), F32)],
        input_output_aliases={5: 0, 6: 1},
        compiler_params=_cparams(("arbitrary",)),
        name="moe_sort",
    )(loc, len16, off, h2_all, gates_all, jnp.zeros((n_rows, D_MODEL), BF16), jnp.zeros((n_rows, LANES), F32))

    blk = lambda w: pl.BlockSpec((MOE_TM, w), lambda i, bg, nb: (jnp.minimum(i, nb[0] - 1), 0))
    wspec = lambda a, b: pl.BlockSpec((N_EPG, a, b), lambda i, bg, nb: (bg[i], 0, 0))
    o_sorted = pl.pallas_call(
        _moe_expert_kernel,
        grid_spec=pltpu.PrefetchScalarGridSpec(
            num_scalar_prefetch=2, grid=(n_blocks,),
            in_specs=[blk(D_MODEL), blk(LANES), wspec(D_MODEL, D_EXPERT), wspec(D_MODEL, D_EXPERT),
                      wspec(D_EXPERT, D_MODEL)],
            out_specs=pl.BlockSpec((MOE_TM, D_MODEL), lambda i, bg, nb: (i, 0))),
        out_shape=jax.ShapeDtypeStruct((n_rows, D_MODEL), F32),
        compiler_params=_cparams(("arbitrary",)),
        name="moe_experts",
    )(blk_group, n_used, xs, gs, wg_bf, wu_bf, wd_bf)

    lat_per_row = tokens_per_mod_row // MOE_ST

    def mod_idx(s, *_):
        return (jnp.where(s < n_ctx_tiles, 0, 1 + (s - n_ctx_tiles) // lat_per_row), 0, 0)

    vec = pl.BlockSpec((1, D_MODEL), lambda s, *_: (0, 0))
    return pl.pallas_call(
        functools.partial(_moe_combine_kernel, n_ctx_tiles=n_ctx_tiles),
        grid_spec=pltpu.PrefetchScalarGridSpec(
            num_scalar_prefetch=3, grid=(n_tiles,),
            in_specs=[tile(LANES), tile(D_MODEL), pl.BlockSpec((1, 1, 6 * D_MODEL), mod_idx), vec, vec,
                      anyspec],
            out_specs=[pl.BlockSpec((MOE_ST, D_MODEL), lambda s, *_: (jnp.minimum(s, n_ctx_tiles - 1), 0)),
                       pl.BlockSpec((MOE_ST, D_MODEL), lambda s, *_: (jnp.maximum(s - n_ctx_tiles, 0), 0))],
            scratch_shapes=[pltpu.VMEM((MOE_SLOTS, D_MODEL), F32), pltpu.SemaphoreType.DMA((1,))]),
        out_shape=[jax.ShapeDtypeStruct((n_ctx, D_MODEL), F32),
                   jax.ShapeDtypeStruct((n_tok - n_ctx, D_MODEL), F32)],
        compiler_params=_cparams(("arbitrary",)),
        name="moe_combine",
    )(loc, len16, off, gates_all, x1_all, mod.reshape(mod.shape[0], 1, 6 * D_MODEL), ln2g, ln2b, o_sorted)


def _grid_pos_embed(n_tokens):
    rows = n_tokens // GRID_W
    row = jnp.repeat(jnp.arange(rows, dtype=F32), GRID_W)
    col = jnp.tile(jnp.arange(GRID_W, dtype=F32), rows)
    quarter = D_MODEL // 4
    omega = 1.0 / (POS_BASE ** (jnp.arange(quarter, dtype=F32) / quarter))
    er = row[:, None] * omega
    ec = col[:, None] * omega
    return jnp.concatenate([jnp.sin(er), jnp.cos(er), jnp.sin(ec), jnp.cos(ec)], axis=-1)


def _tables(n_tok):
    cm, sm = _dft_tables(n_tok)
    out = []
    for t in (cm, sm, np.ascontiguousarray(sm.T)):
        hi, lo = _split(jnp.asarray(t))
        out += [hi, lo]
    return tuple(out)


def _path(x, pos, mod3, h0_re, h0_im, tabs, filt, s5ops, wts, tm):
    bsz, n_tok, _ = x.shape
    shared = mod3.shape[0] == 1
    x3 = x.reshape(1, bsz * n_tok, D_MODEL) if shared else x
    proj_hy, u_s5 = _in_proj(x3, pos, mod3, wts['w_in'], tm)
    y_hy = _hyena(proj_hy.reshape(bsz, n_tok, 3 * D_HY), tabs, filt,
                  wts['hy_conv_w'], wts['hy_conv_b'], wts['hy_fbias'])
    y_s5, f_re, f_im = _s5(u_s5.reshape(bsz, n_tok, D_S5), s5ops, h0_re, h0_im)
    x1, h2, gates = _out_proj(x3, pos, y_hy.reshape(x3.shape[0], -1, D_HY),
                              y_s5.reshape(x3.shape[0], -1, D_S5), mod3,
                              wts['w_glu'], wts['b_glu'], wts['out_norm_g'], wts['w_out'],
                              wts['ln1_g'], wts['ln1_b'], wts['wr_hi'], wts['wr_lo'], wts['br'], tm)
    flat = lambda a: a.reshape(bsz * n_tok, a.shape[-1])
    return flat(x1), flat(h2), flat(gates), f_re, f_im


def kernel(x_prompt, x_sample, state_s5_re, state_s5_im, c, c_ctx, w_ada, b_ada, w_in, hy_conv_w, hy_conv_b, hy_f_w1, hy_f_b1, hy_f_w2, hy_f_b2, hy_f_w3, hy_freq, hy_fbias, s5_a_re, s5_a_im, s5_log_dt, s5_b_re, s5_b_im, s5_c_re, s5_c_im, s5_d, s5_w_glu, s5_b_glu, out_norm_g, w_out, ln1_g, ln1_b, moe_w_r1, moe_b_r1, moe_w_r2, moe_b_r2, moe_w_gate, moe_w_up, moe_w_down, ln2_g, ln2_b):
    b_ctx, l_ctx, _ = x_prompt.shape
    b_lat, l_lat, _ = x_sample.shape
    g, p = S5_GROUPS, S5_STATE
    assert w_ada.shape[0] == 1, "single-layer trunk"
    l = 0

    nrow = 16
    cond = jnp.concatenate([c_ctx[None, :], c, jnp.zeros((nrow - 1 - b_lat, D_MODEL), F32)], axis=0)
    mod = _ada(cond, w_ada[l], b_ada[l])
    mod_ctx = mod[0:1].reshape(1, 1, 6 * D_MODEL)
    mod_lat = mod[1:1 + b_lat].reshape(b_lat, 1, 6 * D_MODEL)

    wr = jnp.concatenate([moe_w_r2[l].transpose(1, 0, 2).reshape(D_MODEL, N_EXPERTS), moe_w_r1[l]], axis=1)
    wr = jnp.pad(wr, ((0, 0), (0, LANES - wr.shape[1])))
    br = jnp.concatenate([moe_b_r2[l].reshape(-1), moe_b_r1[l]])
    br = jnp.pad(br, (0, LANES - br.shape[0])).reshape(1, LANES)
    wr_hi, wr_lo = _split(wr)

    wts = {
        'w_in': w_in[l].astype(BF16), 'hy_conv_w': hy_conv_w[l], 'hy_conv_b': hy_conv_b[l],
        'hy_fbias': hy_fbias[l], 'w_glu': s5_w_glu[l].astype(BF16), 'b_glu': s5_b_glu[l].reshape(1, -1),
        'out_norm_g': out_norm_g[l].reshape(1, -1), 'w_out': w_out[l].astype(BF16),
        'ln1_g': ln1_g[l].reshape(1, -1), 'ln1_b': ln1_b[l].reshape(1, -1),
        'wr_hi': wr_hi, 'wr_lo': wr_lo, 'br': br,
        'w_gate': moe_w_gate[l].astype(BF16), 'w_up': moe_w_up[l].astype(BF16),
        'w_down': moe_w_down[l].astype(BF16),
        'ln2_g': ln2_g[l].reshape(1, -1), 'ln2_b': ln2_b[l].reshape(1, -1),
    }

    s5ops = _s5_operators(s5_a_re[l], s5_a_im[l], s5_log_dt[l], s5_b_re[l], s5_b_im[l],
                          s5_c_re[l], s5_c_im[l], s5_d[l])
    tabs_ctx = _tables(l_ctx)
    tabs_lat = _tables(l_lat)
    filt_args = (hy_f_w1[l], hy_f_b1[l], hy_f_w2[l], hy_f_b2[l], hy_f_w3[l], hy_freq[l])
    filt_ctx = _hyena_filters(l_ctx, tabs_ctx, *filt_args)
    filt_lat = _hyena_filters(l_lat, tabs_lat, *filt_args)

    zero = jnp.zeros((g, b_ctx, 2 * p), F32)
    x1_c, h2_c, gt_c, f_re, f_im = _path(x_prompt, None, mod_ctx, zero, zero, tabs_ctx, filt_ctx, s5ops,
                                         wts, 512)
    unpack = lambda f: f.reshape(g, b_ctx, 2, p).transpose(1, 2, 0, 3)[:, None]
    new_re, new_im = unpack(f_re), unpack(f_im)

    pack = lambda s: s[:, l].transpose(2, 0, 1, 3).reshape(g, b_lat, 2 * p)
    pos = _grid_pos_embed(l_lat)
    x1_l, h2_l, gt_l, _, _ = _path(x_sample, pos, mod_lat, pack(state_s5_re), pack(state_s5_im),
                                   tabs_lat, filt_lat, s5ops, wts, 512)

    cat = lambda a, b: jnp.concatenate([a, b], axis=0)
    y_ctx, y_lat = _moe(cat(h2_c, h2_l), cat(gt_c, gt_l), cat(x1_c, x1_l), mod,
                        wts['w_gate'], wts['w_up'], wts['w_down'], wts['ln2_g'], wts['ln2_b'],
                        b_ctx * l_ctx, l_lat)
    return (y_ctx.reshape(x_prompt.shape), y_lat.reshape(x_sample.shape), new_re, new_im)
```

```python
import functools
import math

import numpy as np
import jax
import jax.numpy as jnp
from jax import lax
from jax.experimental import pallas as pl
from jax.experimental.pallas import tpu as pltpu

F32 = jnp.float32
BF16 = jnp.bfloat16

D_MODEL = 1024
DEPTH = 1
GRID_W = 64
POS_BASE = 10000.0
D_HY = 512
D_S5 = 512
S5_CH = 16
S5_GROUPS = 32
S5_STATE = 64
S5_CHUNK = 16
S5_ROW = S5_CHUNK * S5_CH
HY_BANDS = 16
HY_EMB = 1 + 2 * HY_BANDS
HY_HID = 64
HY_MIN_DECAY = math.log(1e-2) / 1.5
HY_MAX_DECAY = math.log(1e-2) / 0.3
N_EGROUPS = 4
N_EPG = 4
N_EXPERTS = 16
D_EXPERT = 512
LN_EPS = 1e-5
ALPHA = (2.0 * DEPTH) ** 0.25
LANES = 128
S5_GB = LANES // S5_CH
HY_CW = 256
MOE_ST = 256
MOE_SLOTS = 384
MOE_UNIT = 16
MOE_TM = 512
VMEM_LIMIT = 60000 * 1024


def _cparams(sem):
    return pltpu.CompilerParams(dimension_semantics=sem, vmem_limit_bytes=VMEM_LIMIT)


def _split(x):
    hi = x.astype(BF16)
    lo = (x - hi.astype(F32)).astype(BF16)
    return hi, lo


def _dot(a, b):
    return jnp.dot(a, b, preferred_element_type=F32)


def _dot_t(a, b):
    return lax.dot_general(a, b, (((1,), (1,)), ((), ())), preferred_element_type=F32)


def _mm3(a, b):
    ah, al = _split(a)
    bh, bl = _split(b)
    return _dot(ah, bh) + _dot(al, bh) + _dot(ah, bl)


def _mm3_pre(ah, al, b):
    bh, bl = _split(b)
    return _dot(ah, bh) + _dot(al, bh) + _dot(ah, bl)


def _mm3_t(a, b):
    ah, al = _split(a)
    bh, bl = _split(b)
    return _dot_t(ah, bh) + _dot_t(al, bh) + _dot_t(ah, bl)


def _dot_hp(a, b):
    return jnp.dot(a, b, preferred_element_type=F32, precision=lax.Precision.HIGHEST)


def _norm(x):
    xc = x - jnp.mean(x, axis=-1, keepdims=True)
    return xc * lax.rsqrt(jnp.mean(xc * xc, axis=-1, keepdims=True) + LN_EPS)


def _rms(y):
    return y * lax.rsqrt(jnp.mean(y * y, axis=-1, keepdims=True) + LN_EPS)


def _ada_kernel(cond_ref, w_ref, b_ref, o_ref):
    c = jax.nn.silu(cond_ref[...])
    o_ref[...] = _mm3(c, w_ref[...]) + b_ref[...]


def _ada(cond, w_ada, b_ada):
    nb = cond.shape[0]
    n = w_ada.shape[1]
    tn = 1024
    return pl.pallas_call(
        _ada_kernel,
        grid=(n // tn,),
        in_specs=[pl.BlockSpec((nb, D_MODEL), lambda j: (0, 0)),
                  pl.BlockSpec((D_MODEL, tn), lambda j: (0, j)),
                  pl.BlockSpec((1, tn), lambda j: (0, j))],
        out_specs=pl.BlockSpec((nb, tn), lambda j: (0, j)),
        out_shape=jax.ShapeDtypeStruct((nb, n), F32),
        compiler_params=_cparams(("arbitrary",)),
        name="ada",
    )(cond, w_ada, b_ada.reshape(1, n))


def _dft_tables(n_tok):
    n = 2 * n_tok
    idx = np.arange(n_tok, dtype=np.int64)
    m = (idx[:, None] * idx[None, :]) % n
    ang = 2.0 * np.pi * m.astype(np.float64) / n
    cm = np.cos(ang)
    sm = -np.sin(ang)
    sm[0, :] = 1.0 - 2.0 * (idx % 2)
    return cm.astype(np.float32), sm.astype(np.float32)


def _filt_kernel(n_tok, z_ref, t_ref, w1_ref, b1_ref, w2_ref, b2_ref, fr_ref, w3f_ref, w3b_ref,
                 dl_ref, cmh_ref, cml_ref, smh_ref, sml_ref, a_ref, bz_ref, dd_ref):
    fr = fr_ref[...]
    h = jnp.sin(fr * (_dot_hp(z_ref[...], w1_ref[...]) + b1_ref[...]))
    h = jnp.sin(fr * (_dot_hp(h, w2_ref[...]) + b2_ref[...]))
    decay = jnp.exp(-t_ref[...] * dl_ref[...])
    row = lax.broadcasted_iota(jnp.int32, decay.shape, 0)
    hf = _dot_hp(h, w3f_ref[...]) * decay
    hb = jnp.where(row == 0, 0.0, _dot_hp(h, w3b_ref[...]) * decay)
    p = hf + hb
    q = hf - hb
    k_re = _mm3_pre(cmh_ref[...], cml_ref[...], p)
    k_im = _mm3_pre(smh_ref[...], sml_ref[...], q)
    sign = jnp.where(row % 2 == 0, 1.0, -1.0)
    nyq = jnp.sum(p * sign, axis=0, keepdims=True)
    inv_n = 1.0 / (2 * n_tok)
    w = jnp.where(row == 0, inv_n, 2.0 * inv_n)
    a_ref[...] = w * k_re
    bz_ref[...] = jnp.where(row == 0, 0.0, w * k_im)
    dd_ref[...] = jnp.where(row == 0, nyq * inv_n, w * k_re)


def _hyena_filters(n_tok, tabs, hy_f_w1, hy_f_b1, hy_f_w2, hy_f_b2, hy_f_w3, hy_freq):
    cmh, cml, smh, sml = tabs[:4]
    t = jnp.linspace(0.0, 1.0, n_tok, dtype=F32)[:, None]
    wv = 2.0 * math.pi * jnp.arange(n_tok, dtype=F32) / n_tok
    fb = jnp.linspace(1e-4, HY_BANDS - 1, HY_BANDS, dtype=F32)
    ang = wv[:, None] * fb[None, :]
    z = jnp.concatenate([t, jnp.cos(ang), -jnp.sin(ang)], axis=-1)
    z = jnp.pad(z, ((0, 0), (0, LANES - HY_EMB)))
    w1 = jnp.pad(hy_f_w1, ((0, LANES - HY_EMB), (0, 0)))
    deltas = jnp.abs(jnp.linspace(HY_MIN_DECAY, HY_MAX_DECAY, D_HY, dtype=F32))[None, :]
    ncb = D_HY // HY_CW
    full = lambda j: (0, 0)
    out_sd = jax.ShapeDtypeStruct((n_tok, 2 * D_HY), F32)
    mat = pl.BlockSpec((n_tok, n_tok), full, pipeline_mode=pl.Buffered(1))
    return pl.pallas_call(
        functools.partial(_filt_kernel, n_tok),
        grid=(2 * ncb,),
        in_specs=[pl.BlockSpec((n_tok, LANES), full),
                  pl.BlockSpec((n_tok, 1), full),
                  pl.BlockSpec((LANES, HY_HID), full),
                  pl.BlockSpec((1, HY_HID), full),
                  pl.BlockSpec((HY_HID, HY_HID), full),
                  pl.BlockSpec((1, HY_HID), full),
                  pl.BlockSpec((1, HY_HID), full),
                  pl.BlockSpec((HY_HID, HY_CW), lambda j: (0, 2 * ncb * (j // ncb) + j % ncb)),
                  pl.BlockSpec((HY_HID, HY_CW), lambda j: (0, 2 * ncb * (j // ncb) + ncb + j % ncb)),
                  pl.BlockSpec((1, HY_CW), lambda j: (0, j % ncb)),
                  mat, mat, mat, mat],
        out_specs=[pl.BlockSpec((n_tok, HY_CW), lambda j: (0, j))] * 3,
        out_shape=[out_sd] * 3,
        compiler_params=_cparams(("arbitrary",)),
        name=f"filt{n_tok}",
    )(z, t, w1, hy_f_b1.reshape(1, -1), hy_f_w2, hy_f_b2.reshape(1, -1), hy_freq.reshape(1, -1),
      hy_f_w3, hy_f_w3, deltas, cmh, cml, smh, sml)


def _hyena_kernel(pv_ref, p1_ref, p2_ref, cwv_ref, cw1_ref, cw2_ref, cbv_ref, cb1_ref, cb2_ref,
                  fbias_ref, cm_ref, sm_ref, st_ref,
                  a0_ref, b0_ref, d0_ref, a1_ref, b1_ref, d1_ref, o_ref):
    n_tok = pv_ref.shape[1]
    row = lax.broadcasted_iota(jnp.int32, (n_tok, pv_ref.shape[2]), 0)

    def short_conv(p_ref, cw_ref, cb_ref):
        p = p_ref[0]
        prev = jnp.where(row == 0, 0.0, pltpu.roll(p, 1, axis=0))
        nxt = jnp.where(row == n_tok - 1, 0.0, pltpu.roll(p, n_tok - 1, axis=0))
        return cb_ref[...] + prev * cw_ref[0:1, :] + p * cw_ref[1:2, :] + nxt * cw_ref[2:3, :]

    cm = cm_ref[...]

    def fftconv(u, a_ref, b_ref, d_ref, skip):
        ub = u.astype(BF16)
        u_re = _dot(cm, ub)
        u_im = _dot(sm_ref[...], ub)
        a, bz, dd = a_ref[...], b_ref[...], d_ref[...]
        y_re = u_re * a - u_im * bz
        y_im = u_re * bz + u_im * dd
        y = _dot(cm, y_re.astype(BF16)) + _dot(st_ref[...], y_im.astype(BF16))
        return y + u * skip

    v = short_conv(pv_ref, cwv_ref, cbv_ref)
    x1 = short_conv(p1_ref, cw1_ref, cb1_ref)
    z = x1 * fftconv(v, a0_ref, b0_ref, d0_ref, fbias_ref[0:1, :])
    x2 = short_conv(p2_ref, cw2_ref, cb2_ref)
    o_ref[0] = x2 * fftconv(z, a1_ref, b1_ref, d1_ref, fbias_ref[1:2, :])


def _hyena(proj_hy, tabs, filt, hy_conv_w, hy_conv_b, hy_fbias):
    bsz, n_tok, _ = proj_hy.shape
    ncb = D_HY // HY_CW
    cmh, _, smh, _, sth, _ = tabs
    fa, fbz, fdd = filt
    cb = hy_conv_b.reshape(1, -1)
    mat = pl.BlockSpec((n_tok, n_tok), lambda b, c: (0, 0), pipeline_mode=pl.Buffered(1))

    def pspec(k):
        return pl.BlockSpec((1, n_tok, HY_CW), lambda b, c: (b, 0, k * ncb + c))

    def cwspec(k):
        return pl.BlockSpec((3, HY_CW), lambda b, c: (0, k * ncb + c))

    def cbspec(k):
        return pl.BlockSpec((1, HY_CW), lambda b, c: (0, k * ncb + c))

    def fspec(o):
        return pl.BlockSpec((n_tok, HY_CW), lambda b, c: (0, o * ncb + c))

    return pl.pallas_call(
        _hyena_kernel,
        grid=(bsz, ncb),
        in_specs=[pspec(0), pspec(1), pspec(2), cwspec(0), cwspec(1), cwspec(2),
                  cbspec(0), cbspec(1), cbspec(2),
                  pl.BlockSpec((2, HY_CW), lambda b, c: (0, c)),
                  mat, mat, mat,
                  fspec(0), fspec(0), fspec(0), fspec(1), fspec(1), fspec(1)],
        out_specs=pl.BlockSpec((1, n_tok, HY_CW), lambda b, c: (b, 0, c)),
        out_shape=jax.ShapeDtypeStruct((bsz, n_tok, D_HY), F32),
        compiler_params=_cparams(("arbitrary", "arbitrary")),
        name=f"hyena{n_tok}",
    )(proj_hy, proj_hy, proj_hy, hy_conv_w, hy_conv_w, hy_conv_w, cb, cb, cb, hy_fbias,
      cmh, smh, sth, fa, fbz, fdd, fa, fbz, fdd)


def _s5ops_kernel(are_ref, aim_ref, ldt_ref, btr_ref, bti_ref, cre_ref, cim_ref,
                  er_ref, ei_ref, gr_ref, gi_ref, kf_ref, kb_ref, atr_ref, ati_ref):
    a_re, a_im = are_ref[0], aim_ref[0]
    dt = jnp.exp(ldt_ref[0])
    mag = jnp.exp(a_re * dt)
    ab_re = mag * jnp.cos(a_im * dt)
    ab_im = mag * jnp.sin(a_im * dt)
    n_re, n_im = ab_re - 1.0, ab_im
    den = a_re * a_re + a_im * a_im
    q_re = (n_re * a_re + n_im * a_im) / den
    q_im = (n_im * a_re - n_re * a_im) / den
    bt_re, bt_im = btr_ref[0], bti_ref[0]
    bb_re = q_re * bt_re - q_im * bt_im
    bb_im = q_re * bt_im + q_im * bt_re
    c_re, c_im = cre_ref[0], cim_ref[0]
    pw = [(jnp.ones_like(ab_re), jnp.zeros_like(ab_re))]
    for _ in range(S5_CHUNK):
        pr, pi = pw[-1]
        pw.append((pr * ab_re - pi * ab_im, pr * ab_im + pi * ab_re))
    lane = lax.broadcasted_iota(jnp.int32, ab_re.shape, 1)
    fwd = lane < S5_STATE
    for s in range(S5_CHUNK):
        e_re = jnp.where(fwd, pw[S5_CHUNK - 1 - s][0], pw[s][0])
        e_im = jnp.where(fwd, pw[S5_CHUNK - 1 - s][1], pw[s][1])
        er_ref[0, pl.ds(S5_CH * s, S5_CH), :] = e_re * bb_re - e_im * bb_im
        ei_ref[0, pl.ds(S5_CH * s, S5_CH), :] = e_re * bb_im + e_im * bb_re
        g_re = jnp.where(fwd, pw[s + 1][0], pw[S5_CHUNK - s][0])
        g_im = jnp.where(fwd, pw[s + 1][1], pw[S5_CHUNK - s][1])
        gr_ref[0, pl.ds(S5_CH * s, S5_CH), :] = c_re * g_re - c_im * g_im
        gi_ref[0, pl.ds(S5_CH * s, S5_CH), :] = -(c_re * g_im + c_im * g_re)
    er, ei = er_ref[0], ei_ref[0]
    lane2 = lax.broadcasted_iota(jnp.int32, er.shape, 1)
    f2 = lane2 < S5_STATE
    zero = jnp.zeros_like(er)

    def dot_hp_t(a, b):
        return lax.dot_general(a, b, (((1,), (1,)), ((), ())), preferred_element_type=F32,
                               precision=lax.Precision.HIGHEST)

    kf_ref[0] = dot_hp_t(jnp.where(f2, er, zero), c_re) - dot_hp_t(jnp.where(f2, ei, zero), c_im)
    kb_ref[0] = dot_hp_t(jnp.where(f2, zero, er), c_re) - dot_hp_t(jnp.where(f2, zero, ei), c_im)
    atr_ref[0] = pw[S5_CHUNK][0]
    ati_ref[0] = pw[S5_CHUNK][1]


def _s5_operators(s5_a_re, s5_a_im, s5_log_dt, s5_b_re, s5_b_im, s5_c_re, s5_c_im, s5_d):
    g, p, h = S5_GROUPS, S5_STATE, S5_CH
    cat = lambda x: jnp.concatenate([x[0], x[1]], axis=-1)
    a_re = cat(s5_a_re).reshape(g, 1, 2 * p)
    a_im = cat(s5_a_im).reshape(g, 1, 2 * p)
    ldt = cat(jnp.broadcast_to(s5_log_dt[:, :, None], (2, g, p))).reshape(g, 1, 2 * p)
    bt_re = cat(jnp.swapaxes(s5_b_re, -1, -2))
    bt_im = cat(jnp.swapaxes(s5_b_im, -1, -2))
    c_re = jnp.concatenate([s5_c_re, s5_c_re], axis=-1)
    c_im = jnp.concatenate([s5_c_im, s5_c_im], axis=-1)
    vec = pl.BlockSpec((1, 1, 2 * p), lambda i: (i, 0, 0))
    hp = pl.BlockSpec((1, h, 2 * p), lambda i: (i, 0, 0))
    big = pl.BlockSpec((1, S5_ROW, 2 * p), lambda i: (i, 0, 0))
    kk = pl.BlockSpec((1, S5_ROW, h), lambda i: (i, 0, 0))
    big_sd = jax.ShapeDtypeStruct((g, S5_ROW, 2 * p), F32)
    kk_sd = jax.ShapeDtypeStruct((g, S5_ROW, h), F32)
    vec_sd = jax.ShapeDtypeStruct((g, 1, 2 * p), F32)
    er, ei, gr, gi, kf, kb, at_re, at_im = pl.pallas_call(
        _s5ops_kernel,
        grid=(g,),
        in_specs=[vec, vec, vec, hp, hp, hp, hp],
        out_specs=[big, big, big, big, kk, kk, vec, vec],
        out_shape=[big_sd, big_sd, big_sd, big_sd, kk_sd, kk_sd, vec_sd, vec_sd],
        compiler_params=_cparams(("arbitrary",)),
        name="s5ops",
    )(a_re, a_im, ldt, bt_re, bt_im, c_re, c_im)
    kf_lag = kf.reshape(g, S5_CHUNK, h, h)[:, ::-1]
    kb_lag = kb.reshape(g, S5_CHUNK, h, h)
    lag = np.arange(S5_CHUNK)[None, :] - np.arange(S5_CHUNK)[:, None]
    mf = kf_lag[:, np.clip(lag, 0, S5_CHUNK - 1)] * jnp.asarray(lag >= 0, F32)[None, :, :, None, None]
    mb = kb_lag[:, np.clip(-lag, 0, S5_CHUNK - 1)] * jnp.asarray(lag <= 0, F32)[None, :, :, None, None]
    mt = (mf + mb).transpose(0, 1, 3, 2, 4).reshape(g, S5_ROW, S5_ROW)
    dvec = jnp.tile(s5_d.reshape(g, 1, h), (1, S5_CHUNK, 1)).reshape(g, S5_ROW)
    mt = mt + jnp.eye(S5_ROW, dtype=F32)[None] * dvec[:, None, :]
    return mt, er, ei, gr, gi, at_re, at_im


def _s5_kernel(bsz, n_chunks, u_ref, mt_ref, er_ref, ei_ref, gr_ref, gi_ref, atr_ref, ati_ref,
               h0r_ref, h0i_ref, y_ref, fr_ref, fi_ref, u8_ref, y8_ref, sr_ref, si_ref,
               xfr_ref, xfi_ref, xbr_ref, xbi_ref):
    nc = n_chunks
    steps_per_vreg = LANES // S5_CH

    def to_chunks(b, carry):
        r0 = pl.multiple_of(b * nc, 8)
        for t in range(S5_CHUNK):
            a = u_ref[b, pl.ds(t, nc, stride=S5_CHUNK), :]
            off = S5_CH * (t % steps_per_vreg)
            base = LANES * (t // steps_per_vreg)
            for k in range(S5_GB):
                shift = (off - S5_CH * k) % LANES
                r = pltpu.roll(a, shift, axis=1) if shift else a
                u8_ref[k, pl.ds(r0, nc), base + off:base + off + S5_CH] = r[:, off:off + S5_CH]
        return carry

    lax.fori_loop(0, bsz, to_chunks, 0)

    lane = lax.broadcasted_iota(jnp.int32, (bsz, 2 * S5_STATE), 1)
    fwd = lane < S5_STATE
    lane_all = lax.broadcasted_iota(jnp.int32, (bsz * nc, 2 * S5_STATE), 1)
    fwd_all = lane_all < S5_STATE

    def group(k, carry):
        uh, ul = _split(u8_ref[k])
        sr_ref[...] = _mm3_pre(uh, ul, er_ref[k])
        si_ref[...] = _mm3_pre(uh, ul, ei_ref[k])
        at_re, at_im = atr_ref[k], ati_ref[k]

        def step(i, xc):
            x_re, x_im = xc
            rf = pl.ds(i, bsz, stride=nc)
            rb = pl.ds(nc - 1 - i, bsz, stride=nc)
            xfr_ref[rf, :] = x_re
            xfi_ref[rf, :] = x_im
            xbr_ref[rb, :] = x_re
            xbi_ref[rb, :] = x_im
            s_re = jnp.where(fwd, sr_ref[rf, :], sr_ref[rb, :])
            s_im = jnp.where(fwd, si_ref[rf, :], si_ref[rb, :])
            return (at_re * x_re - at_im * x_im + s_re, at_re * x_im + at_im * x_re + s_im)

        x_re, x_im = lax.fori_loop(0, nc, step, (h0r_ref[k], h0i_ref[k]))
        fr_ref[k] = x_re
        fi_ref[k] = x_im
        xp_re = jnp.where(fwd_all, xfr_ref[...], xbr_ref[...])
        xp_im = jnp.where(fwd_all, xfi_ref[...], xbi_ref[...])
        y8_ref[k] = (_mm3_pre(uh, ul, mt_ref[k]) + _mm3_t(xp_re, gr_ref[k]) + _mm3_t(xp_im, gi_ref[k]))
        return carry

    lax.fori_loop(0, S5_GB, group, 0)

    lane_tok = lax.broadcasted_iota(jnp.int32, (nc, LANES), 1)

    def to_tokens(b, carry):
        r0 = pl.multiple_of(b * nc, 8)
        for t in range(S5_CHUNK):
            off = S5_CH * (t % steps_per_vreg)
            base = LANES * (t // steps_per_vreg)
            acc = None
            for k in range(S5_GB):
                src = y8_ref[k, pl.ds(r0, nc), base:base + LANES]
                shift = (S5_CH * k - off) % LANES
                r = pltpu.roll(src, shift, axis=1) if shift else src
                sel = (lane_tok >= S5_CH * k) & (lane_tok < S5_CH * (k + 1))
                acc = jnp.where(sel, r, 0.0 if acc is None else acc)
            y_ref[b, pl.ds(t, nc, stride=S5_CHUNK), :] = acc
        return carry

    lax.fori_loop(0, bsz, to_tokens, 0)


def _s5(u, ops, h0_re, h0_im):
    bsz, n_tok, _ = u.shape
    mt, er, ei, gr, gi, at_re, at_im = ops
    g, p = S5_GROUPS, S5_STATE
    nc = n_tok // S5_CHUNK
    rows = nc * bsz
    tok = pl.BlockSpec((bsz, n_tok, LANES), lambda j: (0, 0, j))
    gspec = lambda shape: pl.BlockSpec((S5_GB,) + shape, lambda j: (j, 0, 0))
    return pl.pallas_call(
        functools.partial(_s5_kernel, bsz, nc),
        grid=(g // S5_GB,),
        in_specs=[tok, gspec((S5_ROW, S5_ROW)), gspec((S5_ROW, 2 * p)),
                  gspec((S5_ROW, 2 * p)), gspec((S5_ROW, 2 * p)), gspec((S5_ROW, 2 * p)),
                  gspec((1, 2 * p)), gspec((1, 2 * p)), gspec((bsz, 2 * p)), gspec((bsz, 2 * p))],
        out_specs=[tok, gspec((bsz, 2 * p)), gspec((bsz, 2 * p))],
        out_shape=[jax.ShapeDtypeStruct((bsz, n_tok, D_S5), F32),
                   jax.ShapeDtypeStruct((g, bsz, 2 * p), F32),
                   jax.ShapeDtypeStruct((g, bsz, 2 * p), F32)],
        scratch_shapes=([pltpu.VMEM((S5_GB, rows, S5_ROW), F32)] * 2
                        + [pltpu.VMEM((rows, 2 * p), F32)] * 6),
        compiler_params=_cparams(("arbitrary",)),
        name=f"s5_{n_tok}",
    )(u, mt, er, ei, gr, gi, at_re, at_im, h0_re, h0_im)


def _in_kernel(has_pos, *refs):
    if has_pos:
        x_ref, pos_ref, mod_ref, w_ref, hy_ref, s5_ref = refs
        x = x_ref[0] + pos_ref[...]
    else:
        x_ref, mod_ref, w_ref, hy_ref, s5_ref = refs
        x = x_ref[0]
    sh1 = mod_ref[0, :, 0:D_MODEL]
    sc1 = mod_ref[0, :, D_MODEL:2 * D_MODEL]
    h = _norm(x) * (1.0 + sc1) + sh1
    proj = _dot(h.astype(BF16), w_ref[...])
    hy_ref[0] = proj[:, :3 * D_HY]
    s5_ref[0] = proj[:, 3 * D_HY:]


def _in_proj(x3, pos, mod3, w_in_bf, tm):
    nb, lt, _ = x3.shape
    has_pos = pos is not None
    per_batch = mod3.shape[0] > 1
    midx = (lambda b, i: (b, 0, 0)) if per_batch else (lambda b, i: (0, 0, 0))
    in_specs = [pl.BlockSpec((1, tm, D_MODEL), lambda b, i: (b, i, 0))]
    args = [x3]
    if has_pos:
        in_specs.append(pl.BlockSpec((tm, D_MODEL), lambda b, i: (i, 0)))
        args.append(pos)
    in_specs += [pl.BlockSpec((1, 1, 6 * D_MODEL), midx),
                 pl.BlockSpec((D_MODEL, 3 * D_HY + D_S5), lambda b, i: (0, 0))]
    args += [mod3, w_in_bf]
    return pl.pallas_call(
        functools.partial(_in_kernel, has_pos),
        grid=(nb, lt // tm),
        in_specs=in_specs,
        out_specs=[pl.BlockSpec((1, tm, 3 * D_HY), lambda b, i: (b, i, 0)),
                   pl.BlockSpec((1, tm, D_S5), lambda b, i: (b, i, 0))],
        out_shape=[jax.ShapeDtypeStruct((nb, lt, 3 * D_HY), F32),
                   jax.ShapeDtypeStruct((nb, lt, D_S5), F32)],
        compiler_params=_cparams(("arbitrary", "arbitrary")),
        name=f"in_proj{nb}",
    )(*args)


def _route(logits):
    lane = lax.broadcasted_iota(jnp.int32, logits.shape, 1)
    lane_f = lane.astype(F32)
    neg = -jnp.inf
    big = float(LANES)
    m1 = (lane >= N_EXPERTS) & (lane < N_EXPERTS + N_EGROUPS)
    l1 = jnp.where(m1, logits, neg)
    top1 = jnp.max(l1, axis=-1, keepdims=True)
    grp = jnp.min(jnp.where(l1 == top1, lane_f, big), axis=-1, keepdims=True) - float(N_EXPERTS)
    den = jnp.sum(jnp.where(m1, jnp.exp(logits - top1), 0.0), axis=-1, keepdims=True)
    p_grp = 1.0 / den
    lo = grp * float(N_EPG)
    m2 = (lane_f >= lo) & (lane_f < lo + float(N_EPG))
    l2 = jnp.where(m2, logits, neg)
    v1 = jnp.max(l2, axis=-1, keepdims=True)
    i1 = jnp.min(jnp.where(l2 == v1, lane_f, big), axis=-1, keepdims=True)
    l2b = jnp.where(lane_f == i1, neg, l2)
    v2 = jnp.max(l2b, axis=-1, keepdims=True)
    i2 = jnp.min(jnp.where(l2b == v2, lane_f, big), axis=-1, keepdims=True)
    e = jnp.exp(v2 - v1)
    w1 = 1.0 / (1.0 + e)
    w2 = e / (1.0 + e)
    gates = jnp.where(lane_f == i1, w1 * p_grp, 0.0) + jnp.where(lane_f == i2, w2 * p_grp, 0.0)
    return jnp.where(lane_f == grp + float(N_EXPERTS), 1.0, gates)


def _out_kernel(n_ctx_blocks, xc_ref, xl_ref, pos_ref, yhyc_ref, yhyl_ref, ys5c_ref, ys5l_ref, mod_ref,
                wglu_ref, bglu_ref, ong_ref, wout_ref, ln1g_ref, ln1b_ref, wrh_ref, wrl_ref, br_ref,
                x1_ref, h2_ref, gate_ref):
    is_ctx = pl.program_id(0) < n_ctx_blocks
    x = jnp.where(is_ctx, xc_ref[...], xl_ref[...] + pos_ref[...])
    y = jnp.where(is_ctx, ys5c_ref[...], ys5l_ref[...])
    y_hy = jnp.where(is_ctx, yhyc_ref[...], yhyl_ref[...])
    s5 = jax.nn.gelu(y) * jax.nn.sigmoid(_dot(y.astype(BF16), wglu_ref[...]) + bglu_ref[...])
    m_hy = _rms(y_hy) * ong_ref[:, 0:D_HY]
    m_s5 = _rms(s5) * ong_ref[:, D_HY:]
    o = (_dot(m_hy.astype(BF16), wout_ref[0:D_HY, :]) + _dot(m_s5.astype(BF16), wout_ref[D_HY:, :]))
    g1 = mod_ref[0, :, 2 * D_MODEL:3 * D_MODEL]
    sh2 = mod_ref[0, :, 3 * D_MODEL:4 * D_MODEL]
    sc2 = mod_ref[0, :, 4 * D_MODEL:5 * D_MODEL]
    x1 = _norm(ALPHA * x + g1 * o) * ln1g_ref[...] + ln1b_ref[...]
    x1_ref[...] = x1
    h2 = _norm(x1) * (1.0 + sc2) + sh2
    h2_ref[...] = h2.astype(BF16)
    hh, hl = _split(h2)
    logits = (_dot(hh, wrh_ref[...]) + _dot(hl, wrh_ref[...]) + _dot(hh, wrl_ref[...]) + br_ref[...])
    gate_ref[...] = _route(logits)


def _out_proj(xc, xl, pos, yhy_c, yhy_l, ys5_c, ys5_l, mod, wglu_bf, bglu, ong, wout_bf, ln1g, ln1b,
              wr_hi, wr_lo, br, tm):
    n_ctx, n_lat = xc.shape[0], xl.shape[0]
    l_lat = pos.shape[0]
    ncb, nlb, npb = n_ctx // tm, n_lat // tm, l_lat // tm
    ctx = lambda w: pl.BlockSpec((tm, w), lambda i: (jnp.minimum(i, ncb - 1), 0))
    lat = lambda w: pl.BlockSpec((tm, w), lambda i: (jnp.maximum(i - ncb, 0), 0))
    full = lambda shape: pl.BlockSpec(shape, lambda i: (0,) * len(shape))
    out = lambda w: pl.BlockSpec((tm, w), lambda i: (i, 0))
    mod_idx = lambda i: (jnp.where(i < ncb, 0, 1 + jnp.maximum(i - ncb, 0) // npb), 0, 0)
    n_all = n_ctx + n_lat
    return pl.pallas_call(
        functools.partial(_out_kernel, ncb),
        grid=(ncb + nlb,),
        in_specs=[ctx(D_MODEL), lat(D_MODEL),
                  pl.BlockSpec((tm, D_MODEL), lambda i: (jnp.maximum(i - ncb, 0) % npb, 0)),
                  ctx(D_HY), lat(D_HY), ctx(D_S5), lat(D_S5),
                  pl.BlockSpec((1, 1, 6 * D_MODEL), mod_idx),
                  full((D_S5, D_S5)), full((1, D_S5)), full((1, D_MODEL)), full((D_MODEL, D_MODEL)),
                  full((1, D_MODEL)), full((1, D_MODEL)), full((D_MODEL, LANES)), full((D_MODEL, LANES)),
                  full((1, LANES))],
        out_specs=[out(D_MODEL), out(D_MODEL), out(LANES)],
        out_shape=[jax.ShapeDtypeStruct((n_all, D_MODEL), F32),
                   jax.ShapeDtypeStruct((n_all, D_MODEL), BF16),
                   jax.ShapeDtypeStruct((n_all, LANES), F32)],
        compiler_params=_cparams(("arbitrary",)),
        name="out_proj",
    )(xc, xl, pos, yhy_c, yhy_l, ys5_c, ys5_l, mod.reshape(mod.shape[0], 1, 6 * D_MODEL),
      wglu_bf, bglu, ong, wout_bf, ln1g, ln1b, wr_hi, wr_lo, br)


def _perm_t(gates, loc_ref, s):
    n = gates.shape[0]
    lane = lax.broadcasted_iota(jnp.int32, gates.shape, 1)
    oh = jnp.where((lane >= N_EXPERTS) & (lane < N_EXPERTS + N_EGROUPS), gates, 0.0)
    r = lax.broadcasted_iota(jnp.int32, (n, n), 0)
    c = lax.broadcasted_iota(jnp.int32, (n, n), 1)
    earlier = jnp.where(c < r, 1.0, 0.0).astype(BF16)
    cum = _dot(earlier, oh.astype(BF16))
    rank = jnp.sum(cum * oh, axis=-1, keepdims=True)
    lane1 = lax.broadcasted_iota(jnp.int32, (1, LANES), 1)
    locv = jnp.zeros((1, LANES), F32)
    for grp in range(N_EGROUPS):
        locv = jnp.where(lane1 == N_EXPERTS + grp, loc_ref[N_EGROUPS * s + grp].astype(F32), locv)
    dest = rank + jnp.sum(oh * locv, axis=-1, keepdims=True)
    slot = lax.broadcasted_iota(jnp.int32, (n, MOE_SLOTS), 1).astype(F32)
    return jnp.where(slot == dest, 1.0, 0.0)


def _segment_copies(s, loc_ref, len_ref, off_ref, make):
    for grp in range(N_EGROUPS):
        loc = loc_ref[N_EGROUPS * s + grp]
        off = off_ref[N_EGROUPS * s + grp]
        n_units = len_ref[N_EGROUPS * s + grp] // MOE_UNIT

        def body(i, carry):
            make(pl.multiple_of(loc + MOE_UNIT * i, MOE_UNIT), pl.multiple_of(off + MOE_UNIT * i, MOE_UNIT))
            return carry

        lax.fori_loop(0, n_units, body, 0)


def _moe_sort_kernel(loc_ref, len_ref, off_ref, h_ref, gate_ref, xs_in, gs_in, xs_hbm, gs_hbm,
                     xs_v, gs_v, sem):
    del xs_in, gs_in
    s = pl.program_id(0)
    slot = s % 2
    gates = gate_ref[...]
    p = _perm_t(gates, loc_ref, s).T.astype(BF16)
    xs_v[slot] = _dot(p, h_ref[...]).astype(BF16)
    g_hi = gates.astype(BF16)
    r1 = gates - g_hi.astype(F32)
    g_mid = r1.astype(BF16)
    g_lo = (r1 - g_mid.astype(F32)).astype(BF16)
    gs_v[slot] = _dot(p, g_hi) + _dot(p, g_mid) + _dot(p, g_lo)

    def copies(buf):
        def x_copy(lr, gr):
            return pltpu.make_async_copy(xs_v.at[buf, pl.ds(lr, MOE_UNIT), :],
                                         xs_hbm.at[pl.ds(gr, MOE_UNIT), :], sem.at[0, buf])

        def g_copy(lr, gr):
            return pltpu.make_async_copy(gs_v.at[buf, pl.ds(lr, MOE_UNIT), :],
                                         gs_hbm.at[pl.ds(gr, MOE_UNIT), :], sem.at[1, buf])

        def start(lr, gr):
            x_copy(lr, gr).start()
            g_copy(lr, gr).start()

        def wait(lr, gr):
            x_copy(lr, gr).wait()
            g_copy(lr, gr).wait()

        return start, wait

    _segment_copies(s, loc_ref, len_ref, off_ref, copies(slot)[0])

    @pl.when(s > 0)
    def _():
        _segment_copies(s - 1, loc_ref, len_ref, off_ref, copies(1 - slot)[1])

    @pl.when(s == pl.num_programs(0) - 1)
    def _():
        _segment_copies(s, loc_ref, len_ref, off_ref, copies(slot)[1])


def _moe_expert_kernel(bg_ref, nb_ref, xs_ref, gs_ref, wg_ref, wu_ref, wd_ref, o_ref):
    i = pl.program_id(0)

    @pl.when(i < nb_ref[0])
    def _():
        grp = bg_ref[i]
        x = xs_ref[...]
        gates = gs_ref[...]
        lane = lax.broadcasted_iota(jnp.int32, gates.shape, 1)
        acc = jnp.zeros(o_ref.shape, F32)
        for e in range(N_EPG):
            a = _dot(x, wg_ref[e])
            u = _dot(x, wu_ref[e])
            ge = jnp.sum(jnp.where(lane == N_EPG * grp + e, gates, 0.0), axis=-1, keepdims=True)
            hid = jax.nn.silu(a) * u * ge
            acc = acc + _dot(hid.astype(BF16), wd_ref[e])
        o_ref[...] = acc

    @pl.when(i >= nb_ref[0])
    def _():
        o_ref[...] = jnp.zeros_like(o_ref)


def _moe_combine_kernel(loc_ref, len_ref, off_ref, gate_ref, x1_ref, mod_ref, ln2g_ref, ln2b_ref, o_hbm,
                        ctx_ref, lat_ref, o_v, sem, *, n_ctx_tiles):
    s = pl.program_id(0)
    slot = s % 2

    def copies(buf):
        def o_copy(lr, gr):
            return pltpu.make_async_copy(o_hbm.at[pl.ds(gr, MOE_UNIT), :],
                                         o_v.at[buf, pl.ds(lr, MOE_UNIT), :], sem.at[buf])

        return (lambda lr, gr: o_copy(lr, gr).start()), (lambda lr, gr: o_copy(lr, gr).wait())

    @pl.when(s == 0)
    def _():
        o_v[...] = jnp.zeros_like(o_v)
        _segment_copies(s, loc_ref, len_ref, off_ref, copies(slot)[0])

    @pl.when(s + 1 < pl.num_programs(0))
    def _():
        _segment_copies(s + 1, loc_ref, len_ref, off_ref, copies(1 - slot)[0])

    pt = _perm_t(gate_ref[...], loc_ref, s).astype(BF16)
    _segment_copies(s, loc_ref, len_ref, off_ref, copies(slot)[1])
    oh, ol = _split(o_v[slot])
    f = _dot(pt, oh) + _dot(pt, ol)
    g2 = mod_ref[0, :, 5 * D_MODEL:6 * D_MODEL]
    x2 = _norm(ALPHA * x1_ref[...] + g2 * f) * ln2g_ref[...] + ln2b_ref[...]

    @pl.when(s < n_ctx_tiles)
    def _():
        ctx_ref[...] = x2

    @pl.when(s >= n_ctx_tiles)
    def _():
        lat_ref[...] = x2


def _moe_plan(gates_all, n_blocks):
    n_tiles = gates_all.shape[0] // MOE_ST
    oh = gates_all[:, N_EXPERTS:N_EXPERTS + N_EGROUPS]
    cnt = jnp.sum(oh.reshape(n_tiles, MOE_ST, N_EGROUPS), axis=1).astype(jnp.int32)
    len16 = ((cnt + MOE_UNIT - 1) // MOE_UNIT) * MOE_UNIT
    loc = jnp.cumsum(len16, axis=1) - len16
    rows_g = jnp.sum(len16, axis=0)
    reg_g = ((rows_g + MOE_TM - 1) // MOE_TM) * MOE_TM
    reg_start = jnp.cumsum(reg_g) - reg_g
    off = reg_start[None, :] + jnp.cumsum(len16, axis=0) - len16
    blk_end = jnp.cumsum(reg_g // MOE_TM)
    bi = jnp.arange(n_blocks, dtype=jnp.int32)
    blk_group = jnp.minimum(jnp.sum((bi[:, None] >= blk_end[None, :]).astype(jnp.int32), axis=1),
                            N_EGROUPS - 1)
    flat = lambda a: a.reshape(-1).astype(jnp.int32)
    return flat(loc), flat(len16), flat(off), blk_group.astype(jnp.int32), blk_end[-1:].astype(jnp.int32)


def _moe(h2_all, gates_all, x1_all, mod, wg_bf, wu_bf, wd_bf, ln2g, ln2b, n_ctx, tokens_per_mod_row):
    n_tok = h2_all.shape[0]
    n_tiles = n_tok // MOE_ST
    n_ctx_tiles = n_ctx // MOE_ST
    max_rows = n_tok + n_tiles * N_EGROUPS * (MOE_UNIT - 1) + N_EGROUPS * (MOE_TM - 1)
    n_blocks = -(-max_rows // MOE_TM)
    n_rows = n_blocks * MOE_TM
    loc, len16, off, blk_group, n_used = _moe_plan(gates_all, n_blocks)

    tile = lambda w: pl.BlockSpec((MOE_ST, w), lambda s, *_: (s, 0))
    anyspec = pl.BlockSpec(memory_space=pl.ANY)
    xs, gs = pl.pallas_call(
        _moe_sort_kernel,
        grid_spec=pltpu.PrefetchScalarGridSpec(
            num_scalar_prefetch=3, grid=(n_tiles,),
            in_specs=[tile(D_MODEL), tile(LANES), anyspec, anyspec],
            out_specs=[anyspec, anyspec],
            scratch_shapes=[pltpu.VMEM((2, MOE_SLOTS, D_MODEL), BF16), pltpu.VMEM((2, MOE_SLOTS, LANES), F32),
                            pltpu.SemaphoreType.DMA((2, 2))]),
        out_shape=[jax.ShapeDtypeStruct((n_rows, D_MODEL), BF16),
                   jax.ShapeDtypeStruct((n_rows, LANES), F32)],
        input_output_aliases={5: 0, 6: 1},
        compiler_params=_cparams(("arbitrary",)),
        name="moe_sort",
    )(loc, len16, off, h2_all, gates_all, jnp.zeros((n_rows, D_MODEL), BF16), jnp.zeros((n_rows, LANES), F32))

    blk = lambda w: pl.BlockSpec((MOE_TM, w), lambda i, bg, nb: (jnp.minimum(i, nb[0] - 1), 0))
    wspec = lambda a, b: pl.BlockSpec((N_EPG, a, b), lambda i, bg, nb: (bg[i], 0, 0))
    o_sorted = pl.pallas_call(
        _moe_expert_kernel,
        grid_spec=pltpu.PrefetchScalarGridSpec(
            num_scalar_prefetch=2, grid=(n_blocks,),
            in_specs=[blk(D_MODEL), blk(LANES), wspec(D_MODEL, D_EXPERT), wspec(D_MODEL, D_EXPERT),
                      wspec(D_EXPERT, D_MODEL)],
            out_specs=pl.BlockSpec((MOE_TM, D_MODEL), lambda i, bg, nb: (i, 0))),
        out_shape=jax.ShapeDtypeStruct((n_rows, D_MODEL), F32),
        compiler_params=_cparams(("arbitrary",)),
        name="moe_experts",
    )(blk_group, n_used, xs, gs, wg_bf, wu_bf, wd_bf)

    lat_per_row = tokens_per_mod_row // MOE_ST

    def mod_idx(s, *_):
        return (jnp.where(s < n_ctx_tiles, 0, 1 + (s - n_ctx_tiles) // lat_per_row), 0, 0)

    vec = pl.BlockSpec((1, D_MODEL), lambda s, *_: (0, 0))
    return pl.pallas_call(
        functools.partial(_moe_combine_kernel, n_ctx_tiles=n_ctx_tiles),
        grid_spec=pltpu.PrefetchScalarGridSpec(
            num_scalar_prefetch=3, grid=(n_tiles,),
            in_specs=[tile(LANES), tile(D_MODEL), pl.BlockSpec((1, 1, 6 * D_MODEL), mod_idx), vec, vec,
                      anyspec],
            out_specs=[pl.BlockSpec((MOE_ST, D_MODEL), lambda s, *_: (jnp.minimum(s, n_ctx_tiles - 1), 0)),
                       pl.BlockSpec((MOE_ST, D_MODEL), lambda s, *_: (jnp.maximum(s - n_ctx_tiles, 0), 0))],
            scratch_shapes=[pltpu.VMEM((2, MOE_SLOTS, D_MODEL), F32), pltpu.SemaphoreType.DMA((2,))]),
        out_shape=[jax.ShapeDtypeStruct((n_ctx, D_MODEL), F32),
                   jax.ShapeDtypeStruct((n_tok - n_ctx, D_MODEL), F32)],
        compiler_params=_cparams(("arbitrary",)),
        name="moe_combine",
    )(loc, len16, off, gates_all, x1_all, mod.reshape(mod.shape[0], 1, 6 * D_MODEL), ln2g, ln2b, o_sorted)


def _grid_pos_embed(n_tokens):
    rows = n_tokens // GRID_W
    row = jnp.repeat(jnp.arange(rows, dtype=F32), GRID_W)
    col = jnp.tile(jnp.arange(GRID_W, dtype=F32), rows)
    quarter = D_MODEL // 4
    omega = 1.0 / (POS_BASE ** (jnp.arange(quarter, dtype=F32) / quarter))
    er = row[:, None] * omega
    ec = col[:, None] * omega
    return jnp.concatenate([jnp.sin(er), jnp.cos(er), jnp.sin(ec), jnp.cos(ec)], axis=-1)


def _tables(n_tok):
    cm, sm = _dft_tables(n_tok)
    out = []
    for t in (cm, sm, np.ascontiguousarray(sm.T)):
        hi, lo = _split(jnp.asarray(t))
        out += [hi, lo]
    return tuple(out)


def _mixers(x, pos, mod3, h0_re, h0_im, tabs, filt, s5ops, wts, tm):
    bsz, n_tok, _ = x.shape
    shared = mod3.shape[0] == 1
    x3 = x.reshape(1, bsz * n_tok, D_MODEL) if shared else x
    proj_hy, u_s5 = _in_proj(x3, pos, mod3, wts['w_in'], tm)
    y_hy = _hyena(proj_hy.reshape(bsz, n_tok, 3 * D_HY), tabs, filt,
                  wts['hy_conv_w'], wts['hy_conv_b'], wts['hy_fbias'])
    y_s5, f_re, f_im = _s5(u_s5.reshape(bsz, n_tok, D_S5), s5ops, h0_re, h0_im)
    return y_hy.reshape(bsz * n_tok, D_HY), y_s5.reshape(bsz * n_tok, D_S5), f_re, f_im


def kernel(x_prompt, x_sample, state_s5_re, state_s5_im, c, c_ctx, w_ada, b_ada, w_in, hy_conv_w, hy_conv_b, hy_f_w1, hy_f_b1, hy_f_w2, hy_f_b2, hy_f_w3, hy_freq, hy_fbias, s5_a_re, s5_a_im, s5_log_dt, s5_b_re, s5_b_im, s5_c_re, s5_c_im, s5_d, s5_w_glu, s5_b_glu, out_norm_g, w_out, ln1_g, ln1_b, moe_w_r1, moe_b_r1, moe_w_r2, moe_b_r2, moe_w_gate, moe_w_up, moe_w_down, ln2_g, ln2_b):
    b_ctx, l_ctx, _ = x_prompt.shape
    b_lat, l_lat, _ = x_sample.shape
    g, p = S5_GROUPS, S5_STATE
    assert w_ada.shape[0] == 1, "single-layer trunk"
    l = 0

    nrow = 16
    cond = jnp.concatenate([c_ctx[None, :], c, jnp.zeros((nrow - 1 - b_lat, D_MODEL), F32)], axis=0)
    mod = _ada(cond, w_ada[l], b_ada[l])
    mod_ctx = mod[0:1].reshape(1, 1, 6 * D_MODEL)
    mod_lat = mod[1:1 + b_lat].reshape(b_lat, 1, 6 * D_MODEL)

    wr = jnp.concatenate([moe_w_r2[l].transpose(1, 0, 2).reshape(D_MODEL, N_EXPERTS), moe_w_r1[l]], axis=1)
    wr = jnp.pad(wr, ((0, 0), (0, LANES - wr.shape[1])))
    br = jnp.concatenate([moe_b_r2[l].reshape(-1), moe_b_r1[l]])
    br = jnp.pad(br, (0, LANES - br.shape[0])).reshape(1, LANES)
    wr_hi, wr_lo = _split(wr)

    wts = {
        'w_in': w_in[l].astype(BF16), 'hy_conv_w': hy_conv_w[l], 'hy_conv_b': hy_conv_b[l],
        'hy_fbias': hy_fbias[l], 'w_glu': s5_w_glu[l].astype(BF16), 'b_glu': s5_b_glu[l].reshape(1, -1),
        'out_norm_g': out_norm_g[l].reshape(1, -1), 'w_out': w_out[l].astype(BF16),
        'ln1_g': ln1_g[l].reshape(1, -1), 'ln1_b': ln1_b[l].reshape(1, -1),
        'wr_hi': wr_hi, 'wr_lo': wr_lo, 'br': br,
        'w_gate': moe_w_gate[l].astype(BF16), 'w_up': moe_w_up[l].astype(BF16),
        'w_down': moe_w_down[l].astype(BF16),
        'ln2_g': ln2_g[l].reshape(1, -1), 'ln2_b': ln2_b[l].reshape(1, -1),
    }

    s5ops = _s5_operators(s5_a_re[l], s5_a_im[l], s5_log_dt[l], s5_b_re[l], s5_b_im[l],
                          s5_c_re[l], s5_c_im[l], s5_d[l])
    tabs_ctx = _tables(l_ctx)
    tabs_lat = _tables(l_lat)
    filt_args = (hy_f_w1[l], hy_f_b1[l], hy_f_w2[l], hy_f_b2[l], hy_f_w3[l], hy_freq[l])
    filt_ctx = _hyena_filters(l_ctx, tabs_ctx, *filt_args)
    filt_lat = _hyena_filters(l_lat, tabs_lat, *filt_args)

    zero = jnp.zeros((g, b_ctx, 2 * p), F32)
    yhy_c, ys5_c, f_re, f_im = _mixers(x_prompt, None, mod_ctx, zero, zero, tabs_ctx, filt_ctx, s5ops, wts, 512)
    unpack = lambda f: f.reshape(g, b_ctx, 2, p).transpose(1, 2, 0, 3)[:, None]
    new_re, new_im = unpack(f_re), unpack(f_im)

    pack = lambda s: s[:, l].transpose(2, 0, 1, 3).reshape(g, b_lat, 2 * p)
    pos = _grid_pos_embed(l_lat)
    yhy_l, ys5_l, _, _ = _mixers(x_sample, pos, mod_lat, pack(state_s5_re), pack(state_s5_im),
                                 tabs_lat, filt_lat, s5ops, wts, 512)

    n_ctx = b_ctx * l_ctx
    x1_all, h2_all, gates_all = _out_proj(
        x_prompt.reshape(n_ctx, D_MODEL), x_sample.reshape(b_lat * l_lat, D_MODEL), pos,
        yhy_c, yhy_l, ys5_c, ys5_l, mod, wts['w_glu'], wts['b_glu'], wts['out_norm_g'], wts['w_out'],
        wts['ln1_g'], wts['ln1_b'], wts['wr_hi'], wts['wr_lo'], wts['br'], 512)
    y_ctx, y_lat = _moe(h2_all, gates_all, x1_all, mod,
                        wts['w_gate'], wts['w_up'], wts['w_down'], wts['ln2_g'], wts['ln2_b'],
                        n_ctx, l_lat)
    return (y_ctx.reshape(x_prompt.shape), y_lat.reshape(x_sample.shape), new_re, new_im)
```

```python
import functools
import math

import numpy as np
import jax
import jax.numpy as jnp
from jax import lax
from jax.experimental import pallas as pl
from jax.experimental.pallas import tpu as pltpu

F32 = jnp.float32
BF16 = jnp.bfloat16

D_MODEL = 1024
DEPTH = 1
GRID_W = 64
POS_BASE = 10000.0
D_HY = 512
D_S5 = 512
S5_CH = 16
S5_GROUPS = 32
S5_STATE = 64
S5_CHUNK = 16
S5_ROW = S5_CHUNK * S5_CH
HY_BANDS = 16
HY_EMB = 1 + 2 * HY_BANDS
HY_HID = 64
HY_MIN_DECAY = math.log(1e-2) / 1.5
HY_MAX_DECAY = math.log(1e-2) / 0.3
N_EGROUPS = 4
N_EPG = 4
N_EXPERTS = 16
D_EXPERT = 512
LN_EPS = 1e-5
ALPHA = (2.0 * DEPTH) ** 0.25
LANES = 128
S5_GB = LANES // S5_CH
HY_CW = 256
MOE_ST = 256
MOE_SLOTS = 384
MOE_UNIT = 16
MOE_TM = 512
VMEM_LIMIT = 60000 * 1024


def _cparams(sem):
    return pltpu.CompilerParams(dimension_semantics=sem, vmem_limit_bytes=VMEM_LIMIT)


def _split(x):
    hi = x.astype(BF16)
    lo = (x - hi.astype(F32)).astype(BF16)
    return hi, lo


def _dot(a, b):
    return jnp.dot(a, b, preferred_element_type=F32)


def _dot_t(a, b):
    return lax.dot_general(a, b, (((1,), (1,)), ((), ())), preferred_element_type=F32)


def _mm3(a, b):
    ah, al = _split(a)
    bh, bl = _split(b)
    return _dot(ah, bh) + _dot(al, bh) + _dot(ah, bl)


def _mm3_pre(ah, al, b):
    bh, bl = _split(b)
    return _dot(ah, bh) + _dot(al, bh) + _dot(ah, bl)


def _mm3_t(a, b):
    ah, al = _split(a)
    bh, bl = _split(b)
    return _dot_t(ah, bh) + _dot_t(al, bh) + _dot_t(ah, bl)


def _dot_hp(a, b):
    return jnp.dot(a, b, preferred_element_type=F32, precision=lax.Precision.HIGHEST)


def _norm(x):
    xc = x - jnp.mean(x, axis=-1, keepdims=True)
    return xc * lax.rsqrt(jnp.mean(xc * xc, axis=-1, keepdims=True) + LN_EPS)


def _rms(y):
    return y * lax.rsqrt(jnp.mean(y * y, axis=-1, keepdims=True) + LN_EPS)


def _ada_kernel(cond_ref, w_ref, b_ref, o_ref):
    c = jax.nn.silu(cond_ref[...])
    o_ref[...] = _mm3(c, w_ref[...]) + b_ref[...]


def _ada(cond, w_ada, b_ada):
    nb = cond.shape[0]
    n = w_ada.shape[1]
    tn = 1024
    return pl.pallas_call(
        _ada_kernel,
        grid=(n // tn,),
        in_specs=[pl.BlockSpec((nb, D_MODEL), lambda j: (0, 0)),
                  pl.BlockSpec((D_MODEL, tn), lambda j: (0, j)),
                  pl.BlockSpec((1, tn), lambda j: (0, j))],
        out_specs=pl.BlockSpec((nb, tn), lambda j: (0, j)),
        out_shape=jax.ShapeDtypeStruct((nb, n), F32),
        compiler_params=_cparams(("arbitrary",)),
        name="ada",
    )(cond, w_ada, b_ada.reshape(1, n))


def _dft_tables(n_tok):
    n = 2 * n_tok
    idx = np.arange(n_tok, dtype=np.int64)
    m = (idx[:, None] * idx[None, :]) % n
    ang = 2.0 * np.pi * m.astype(np.float64) / n
    cm = np.cos(ang)
    sm = -np.sin(ang)
    sm[0, :] = 1.0 - 2.0 * (idx % 2)
    return cm.astype(np.float32), sm.astype(np.float32)


def _filt_kernel(n_tok, z_ref, t_ref, w1_ref, b1_ref, w2_ref, b2_ref, fr_ref, w3f_ref, w3b_ref,
                 dl_ref, cmh_ref, cml_ref, smh_ref, sml_ref, a_ref, bz_ref, dd_ref):
    fr = fr_ref[...]
    h = jnp.sin(fr * (_dot_hp(z_ref[...], w1_ref[...]) + b1_ref[...]))
    h = jnp.sin(fr * (_dot_hp(h, w2_ref[...]) + b2_ref[...]))
    decay = jnp.exp(-t_ref[...] * dl_ref[...])
    row = lax.broadcasted_iota(jnp.int32, decay.shape, 0)
    hf = _dot_hp(h, w3f_ref[...]) * decay
    hb = jnp.where(row == 0, 0.0, _dot_hp(h, w3b_ref[...]) * decay)
    p = hf + hb
    q = hf - hb
    k_re = _mm3_pre(cmh_ref[...], cml_ref[...], p)
    k_im = _mm3_pre(smh_ref[...], sml_ref[...], q)
    sign = jnp.where(row % 2 == 0, 1.0, -1.0)
    nyq = jnp.sum(p * sign, axis=0, keepdims=True)
    inv_n = 1.0 / (2 * n_tok)
    w = jnp.where(row == 0, inv_n, 2.0 * inv_n)
    a_ref[...] = w * k_re
    bz_ref[...] = jnp.where(row == 0, 0.0, w * k_im)
    dd_ref[...] = jnp.where(row == 0, nyq * inv_n, w * k_re)


def _hyena_filters(n_tok, tabs, hy_f_w1, hy_f_b1, hy_f_w2, hy_f_b2, hy_f_w3, hy_freq):
    cmh, cml, smh, sml = tabs[:4]
    t = jnp.linspace(0.0, 1.0, n_tok, dtype=F32)[:, None]
    wv = 2.0 * math.pi * jnp.arange(n_tok, dtype=F32) / n_tok
    fb = jnp.linspace(1e-4, HY_BANDS - 1, HY_BANDS, dtype=F32)
    ang = wv[:, None] * fb[None, :]
    z = jnp.concatenate([t, jnp.cos(ang), -jnp.sin(ang)], axis=-1)
    z = jnp.pad(z, ((0, 0), (0, LANES - HY_EMB)))
    w1 = jnp.pad(hy_f_w1, ((0, LANES - HY_EMB), (0, 0)))
    deltas = jnp.abs(jnp.linspace(HY_MIN_DECAY, HY_MAX_DECAY, D_HY, dtype=F32))[None, :]
    ncb = D_HY // HY_CW
    full = lambda j: (0, 0)
    out_sd = jax.ShapeDtypeStruct((n_tok, 2 * D_HY), F32)
    mat = pl.BlockSpec((n_tok, n_tok), full, pipeline_mode=pl.Buffered(1))
    return pl.pallas_call(
        functools.partial(_filt_kernel, n_tok),
        grid=(2 * ncb,),
        in_specs=[pl.BlockSpec((n_tok, LANES), full),
                  pl.BlockSpec((n_tok, 1), full),
                  pl.BlockSpec((LANES, HY_HID), full),
                  pl.BlockSpec((1, HY_HID), full),
                  pl.BlockSpec((HY_HID, HY_HID), full),
                  pl.BlockSpec((1, HY_HID), full),
                  pl.BlockSpec((1, HY_HID), full),
                  pl.BlockSpec((HY_HID, HY_CW), lambda j: (0, 2 * ncb * (j // ncb) + j % ncb)),
                  pl.BlockSpec((HY_HID, HY_CW), lambda j: (0, 2 * ncb * (j // ncb) + ncb + j % ncb)),
                  pl.BlockSpec((1, HY_CW), lambda j: (0, j % ncb)),
                  mat, mat, mat, mat],
        out_specs=[pl.BlockSpec((n_tok, HY_CW), lambda j: (0, j))] * 3,
        out_shape=[out_sd] * 3,
        compiler_params=_cparams(("arbitrary",)),
        name=f"filt{n_tok}",
    )(z, t, w1, hy_f_b1.reshape(1, -1), hy_f_w2, hy_f_b2.reshape(1, -1), hy_freq.reshape(1, -1),
      hy_f_w3, hy_f_w3, deltas, cmh, cml, smh, sml)


def _hyena_kernel(pv_ref, p1_ref, p2_ref, cwv_ref, cw1_ref, cw2_ref, cbv_ref, cb1_ref, cb2_ref,
                  fbias_ref, cm_ref, sm_ref, st_ref,
                  a0_ref, b0_ref, d0_ref, a1_ref, b1_ref, d1_ref, o_ref):
    n_tok = pv_ref.shape[1]
    row = lax.broadcasted_iota(jnp.int32, (n_tok, pv_ref.shape[2]), 0)

    def short_conv(p_ref, cw_ref, cb_ref):
        p = p_ref[0]
        prev = jnp.where(row == 0, 0.0, pltpu.roll(p, 1, axis=0))
        nxt = jnp.where(row == n_tok - 1, 0.0, pltpu.roll(p, n_tok - 1, axis=0))
        return cb_ref[...] + prev * cw_ref[0:1, :] + p * cw_ref[1:2, :] + nxt * cw_ref[2:3, :]

    cm = cm_ref[...]

    def fftconv(u, a_ref, b_ref, d_ref, skip):
        ub = u.astype(BF16)
        u_re = _dot(cm, ub)
        u_im = _dot(sm_ref[...], ub)
        a, bz, dd = a_ref[...], b_ref[...], d_ref[...]
        y_re = u_re * a - u_im * bz
        y_im = u_re * bz + u_im * dd
        y = _dot(cm, y_re.astype(BF16)) + _dot(st_ref[...], y_im.astype(BF16))
        return y + u * skip

    v = short_conv(pv_ref, cwv_ref, cbv_ref)
    x1 = short_conv(p1_ref, cw1_ref, cb1_ref)
    z = x1 * fftconv(v, a0_ref, b0_ref, d0_ref, fbias_ref[0:1, :])
    x2 = short_conv(p2_ref, cw2_ref, cb2_ref)
    o_ref[0] = x2 * fftconv(z, a1_ref, b1_ref, d1_ref, fbias_ref[1:2, :])


def _hyena(proj_hy, tabs, filt, hy_conv_w, hy_conv_b, hy_fbias):
    bsz, n_tok, _ = proj_hy.shape
    ncb = D_HY // HY_CW
    cmh, _, smh, _, sth, _ = tabs
    fa, fbz, fdd = filt
    cb = hy_conv_b.reshape(1, -1)
    mat = pl.BlockSpec((n_tok, n_tok), lambda b, c: (0, 0), pipeline_mode=pl.Buffered(1))

    def pspec(k):
        return pl.BlockSpec((1, n_tok, HY_CW), lambda b, c: (b, 0, k * ncb + c))

    def cwspec(k):
        return pl.BlockSpec((3, HY_CW), lambda b, c: (0, k * ncb + c))

    def cbspec(k):
        return pl.BlockSpec((1, HY_CW), lambda b, c: (0, k * ncb + c))

    def fspec(o):
        return pl.BlockSpec((n_tok, HY_CW), lambda b, c: (0, o * ncb + c))

    return pl.pallas_call(
        _hyena_kernel,
        grid=(bsz, ncb),
        in_specs=[pspec(0), pspec(1), pspec(2), cwspec(0), cwspec(1), cwspec(2),
                  cbspec(0), cbspec(1), cbspec(2),
                  pl.BlockSpec((2, HY_CW), lambda b, c: (0, c)),
                  mat, mat, mat,
                  fspec(0), fspec(0), fspec(0), fspec(1), fspec(1), fspec(1)],
        out_specs=pl.BlockSpec((1, n_tok, HY_CW), lambda b, c: (b, 0, c)),
        out_shape=jax.ShapeDtypeStruct((bsz, n_tok, D_HY), F32),
        compiler_params=_cparams(("arbitrary", "arbitrary")),
        name=f"hyena{n_tok}",
    )(proj_hy, proj_hy, proj_hy, hy_conv_w, hy_conv_w, hy_conv_w, cb, cb, cb, hy_fbias,
      cmh, smh, sth, fa, fbz, fdd, fa, fbz, fdd)


def _s5ops_kernel(are_ref, aim_ref, ldt_ref, btr_ref, bti_ref, cre_ref, cim_ref, d_ref,
                  mt_ref, erh_ref, erl_ref, eih_ref, eil_ref, gr_ref, gi_ref, atr_ref, ati_ref,
                  er_ref, ei_ref):
    a_re, a_im = are_ref[0], aim_ref[0]
    dt = jnp.exp(ldt_ref[0])
    mag = jnp.exp(a_re * dt)
    ab_re = mag * jnp.cos(a_im * dt)
    ab_im = mag * jnp.sin(a_im * dt)
    n_re, n_im = ab_re - 1.0, ab_im
    den = a_re * a_re + a_im * a_im
    q_re = (n_re * a_re + n_im * a_im) / den
    q_im = (n_im * a_re - n_re * a_im) / den
    bt_re, bt_im = btr_ref[0], bti_ref[0]
    bb_re = q_re * bt_re - q_im * bt_im
    bb_im = q_re * bt_im + q_im * bt_re
    c_re, c_im = cre_ref[0, 0:S5_CH, :], cim_ref[0, 0:S5_CH, :]
    pw = [(jnp.ones_like(ab_re), jnp.zeros_like(ab_re))]
    for _ in range(S5_CHUNK):
        pr, pi = pw[-1]
        pw.append((pr * ab_re - pi * ab_im, pr * ab_im + pi * ab_re))
    lane = lax.broadcasted_iota(jnp.int32, ab_re.shape, 1)
    fwd = lane < S5_STATE
    for s in range(S5_CHUNK):
        e_re = jnp.where(fwd, pw[S5_CHUNK - 1 - s][0], pw[s][0])
        e_im = jnp.where(fwd, pw[S5_CHUNK - 1 - s][1], pw[s][1])
        er_ref[pl.ds(S5_CH * s, S5_CH), :] = e_re * bb_re - e_im * bb_im
        ei_ref[pl.ds(S5_CH * s, S5_CH), :] = e_re * bb_im + e_im * bb_re
        g_re = jnp.where(fwd, pw[s + 1][0], pw[S5_CHUNK - s][0])
        g_im = jnp.where(fwd, pw[s + 1][1], pw[S5_CHUNK - s][1])
        gr_ref[0, pl.ds(S5_CH * s, S5_CH), :] = (c_re * g_re - c_im * g_im).astype(BF16)
        gi_ref[0, pl.ds(S5_CH * s, S5_CH), :] = (-(c_re * g_im + c_im * g_re)).astype(BF16)
    atr_ref[0] = pw[S5_CHUNK][0]
    ati_ref[0] = pw[S5_CHUNK][1]
    er, ei = er_ref[...], ei_ref[...]
    erh_ref[0], erl_ref[0] = _split(er)
    eih_ref[0], eil_ref[0] = _split(ei)
    lane2 = lax.broadcasted_iota(jnp.int32, er.shape, 1)
    row2 = lax.broadcasted_iota(jnp.int32, er.shape, 0)
    f2 = lane2 < S5_STATE
    zero = jnp.zeros_like(er)

    def dot_hp_t(a, b):
        return lax.dot_general(a, b, (((1,), (1,)), ((), ())), preferred_element_type=F32,
                               precision=lax.Precision.HIGHEST)

    cp_re, cp_im = cre_ref[0], cim_ref[0]
    kf = dot_hp_t(jnp.where(f2, er, zero), cp_re) - dot_hp_t(jnp.where(f2, ei, zero), cp_im)
    kb = dot_hp_t(jnp.where(f2, zero, er), cp_re) - dot_hp_t(jnp.where(f2, zero, ei), cp_im)
    d_row = d_ref[0]
    steps_per_vreg = LANES // S5_CH
    for half in range(S5_CHUNK // steps_per_vreg):
        acc = zero
        for tt in range(steps_per_vreg):
            t = half * steps_per_vreg + tt
            nf = S5_CH * (S5_CHUNK - 1 - t)
            nb = S5_CH * t
            col_f = jnp.concatenate([kf[nf:], zero[:nf]], axis=0) if nf else kf
            col_b = jnp.concatenate([zero[:nb], kb[:S5_ROW - nb]], axis=0) if nb else kb
            diag = jnp.where((row2 // S5_CH == t) & (row2 % S5_CH == lane2), d_row, 0.0)
            col = col_f + col_b + diag
            r = pltpu.roll(col, S5_CH * tt, axis=1) if tt else col
            acc = jnp.where((lane2 >= S5_CH * tt) & (lane2 < S5_CH * (tt + 1)), r, acc)
        mt_ref[0, :, LANES * half:LANES * (half + 1)] = acc.astype(BF16)


def _s5_operators(s5_a_re, s5_a_im, s5_log_dt, s5_b_re, s5_b_im, s5_c_re, s5_c_im, s5_d):
    g, p, h = S5_GROUPS, S5_STATE, S5_CH
    cat = lambda x: jnp.concatenate([x[0], x[1]], axis=-1)
    a_re = cat(s5_a_re).reshape(g, 1, 2 * p)
    a_im = cat(s5_a_im).reshape(g, 1, 2 * p)
    ldt = cat(jnp.broadcast_to(s5_log_dt[:, :, None], (2, g, p))).reshape(g, 1, 2 * p)
    bt_re = cat(jnp.swapaxes(s5_b_re, -1, -2))
    bt_im = cat(jnp.swapaxes(s5_b_im, -1, -2))
    cpad = lambda c: jnp.pad(jnp.concatenate([c, c], axis=-1), ((0, 0), (0, LANES - h), (0, 0)))
    c_re, c_im = cpad(s5_c_re), cpad(s5_c_im)
    d_row = jnp.pad(s5_d.reshape(g, 1, h), ((0, 0), (0, 0), (0, LANES - h)))
    vec = pl.BlockSpec((1, 1, 2 * p), lambda i: (i, 0, 0))
    hp = pl.BlockSpec((1, h, 2 * p), lambda i: (i, 0, 0))
    sq = pl.BlockSpec((1, LANES, 2 * p), lambda i: (i, 0, 0))
    big = pl.BlockSpec((1, S5_ROW, 2 * p), lambda i: (i, 0, 0))
    mts = pl.BlockSpec((1, S5_ROW, S5_ROW), lambda i: (i, 0, 0))
    big_sd = jax.ShapeDtypeStruct((g, S5_ROW, 2 * p), BF16)
    vec_sd = jax.ShapeDtypeStruct((g, 1, 2 * p), F32)
    return pl.pallas_call(
        _s5ops_kernel,
        grid=(g,),
        in_specs=[vec, vec, vec, hp, hp, sq, sq, vec],
        out_specs=[mts, big, big, big, big, big, big, vec, vec],
        out_shape=[jax.ShapeDtypeStruct((g, S5_ROW, S5_ROW), BF16)] + [big_sd] * 6 + [vec_sd, vec_sd],
        scratch_shapes=[pltpu.VMEM((S5_ROW, 2 * p), F32)] * 2,
        compiler_params=_cparams(("arbitrary",)),
        name="s5ops",
    )(a_re, a_im, ldt, bt_re, bt_im, c_re, c_im, d_row)


def _block_transpose(xs):
    n = len(xs)
    lane = lax.broadcasted_iota(jnp.int32, xs[0].shape, 1)
    xs = list(xs)
    d = n // 2
    while d:
        keep = ((lane // S5_CH) & d) == 0
        for i in range(n):
            if i & d:
                continue
            lo, hi = xs[i], xs[i + d]
            xs[i] = jnp.where(keep, lo, pltpu.roll(hi, S5_CH * d, axis=1))
            xs[i + d] = jnp.where(keep, pltpu.roll(lo, LANES - S5_CH * d, axis=1), hi)
        d //= 2
    return xs


def _s5_kernel(bsz, n_chunks, u_ref, mt_ref, erh_ref, erl_ref, eih_ref, eil_ref, gr_ref, gi_ref,
               atr_ref, ati_ref, h0r_ref, h0i_ref, y_ref, fr_ref, fi_ref,
               ua_ref, ub_ref, ya_ref, yb_ref, sr_ref, si_ref, xfr_ref, xfi_ref, xbr_ref, xbi_ref):
    nc = n_chunks
    spv = LANES // S5_CH
    rsub = min(nc, 32)

    def to_chunks(b, carry):
        for half, dst in ((0, ua_ref), (1, ub_ref)):
            for r0 in range(0, nc, rsub):
                xs = [u_ref[b, pl.ds(S5_CHUNK * r0 + half * spv + tt, rsub, stride=S5_CHUNK), :]
                      for tt in range(spv)]
                for k, blk in enumerate(_block_transpose(xs)):
                    dst[k, pl.ds(r0 * bsz + b, rsub, stride=bsz), :] = blk
        return carry

    lax.fori_loop(0, bsz, to_chunks, 0)

    lane = lax.broadcasted_iota(jnp.int32, (bsz, 2 * S5_STATE), 1)
    fwd = lane < S5_STATE
    lane_all = lax.broadcasted_iota(jnp.int32, (bsz * nc, 2 * S5_STATE), 1)
    fwd_all = lane_all < S5_STATE

    def group(k, carry):
        u = jnp.concatenate([ua_ref[k], ub_ref[k]], axis=1)
        uh, ul = _split(u)
        sr_ref[...] = _dot(uh, erh_ref[k]) + _dot(ul, erh_ref[k]) + _dot(uh, erl_ref[k])
        si_ref[...] = _dot(uh, eih_ref[k]) + _dot(ul, eih_ref[k]) + _dot(uh, eil_ref[k])
        at_re, at_im = atr_ref[k], ati_ref[k]

        def step(i, xc):
            x_re, x_im = xc
            rf = pl.ds(pl.multiple_of(i * bsz, bsz), bsz)
            rb = pl.ds(pl.multiple_of((nc - 1 - i) * bsz, bsz), bsz)
            xfr_ref[rf, :] = x_re
            xfi_ref[rf, :] = x_im
            xbr_ref[rb, :] = x_re
            xbi_ref[rb, :] = x_im
            s_re = jnp.where(fwd, sr_ref[rf, :], sr_ref[rb, :])
            s_im = jnp.where(fwd, si_ref[rf, :], si_ref[rb, :])
            return (at_re * x_re - at_im * x_im + s_re, at_re * x_im + at_im * x_re + s_im)

        x_re, x_im = lax.fori_loop(0, nc, step, (h0r_ref[k], h0i_ref[k]))
        fr_ref[k] = x_re
        fi_ref[k] = x_im
        xp_re = jnp.where(fwd_all, xfr_ref[...], xbr_ref[...]).astype(BF16)
        xp_im = jnp.where(fwd_all, xfi_ref[...], xbi_ref[...]).astype(BF16)
        y = _dot(uh, mt_ref[k]) + _dot_t(xp_re, gr_ref[k]) + _dot_t(xp_im, gi_ref[k])
        ya_ref[k] = y[:, :LANES]
        yb_ref[k] = y[:, LANES:]
        return carry

    lax.fori_loop(0, S5_GB, group, 0)

    def to_tokens(b, carry):
        for half, src in ((0, ya_ref), (1, yb_ref)):
            for r0 in range(0, nc, rsub):
                ys = [src[k, pl.ds(r0 * bsz + b, rsub, stride=bsz), :] for k in range(S5_GB)]
                for tt, blk in enumerate(_block_transpose(ys)):
                    y_ref[b, pl.ds(S5_CHUNK * r0 + half * spv + tt, rsub, stride=S5_CHUNK), :] = blk
        return carry

    lax.fori_loop(0, bsz, to_tokens, 0)


def _s5(u, ops, h0_re, h0_im):
    bsz, n_tok, _ = u.shape
    g, p = S5_GROUPS, S5_STATE
    nc = n_tok // S5_CHUNK
    rows = nc * bsz
    tok = pl.BlockSpec((bsz, n_tok, LANES), lambda j: (0, 0, j))
    gspec = lambda shape: pl.BlockSpec((S5_GB,) + shape, lambda j: (j, 0, 0))
    op = gspec((S5_ROW, 2 * p))
    return pl.pallas_call(
        functools.partial(_s5_kernel, bsz, nc),
        grid=(g // S5_GB,),
        in_specs=[tok, gspec((S5_ROW, S5_ROW)), op, op, op, op, op, op,
                  gspec((1, 2 * p)), gspec((1, 2 * p)), gspec((bsz, 2 * p)), gspec((bsz, 2 * p))],
        out_specs=[tok, gspec((bsz, 2 * p)), gspec((bsz, 2 * p))],
        out_shape=[jax.ShapeDtypeStruct((bsz, n_tok, D_S5), F32),
                   jax.ShapeDtypeStruct((g, bsz, 2 * p), F32),
                   jax.ShapeDtypeStruct((g, bsz, 2 * p), F32)],
        scratch_shapes=([pltpu.VMEM((S5_GB, rows, LANES), F32)] * 4
                        + [pltpu.VMEM((rows, 2 * p), F32)] * 6),
        compiler_params=_cparams(("arbitrary",)),
        name=f"s5_{n_tok}",
    )(u, *ops, h0_re, h0_im)


def _in_kernel(has_pos, *refs):
    if has_pos:
        x_ref, pos_ref, mod_ref, w_ref, hy_ref, s5_ref = refs
        x = x_ref[0] + pos_ref[...]
    else:
        x_ref, mod_ref, w_ref, hy_ref, s5_ref = refs
        x = x_ref[0]
    sh1 = mod_ref[0, :, 0:D_MODEL]
    sc1 = mod_ref[0, :, D_MODEL:2 * D_MODEL]
    h = _norm(x) * (1.0 + sc1) + sh1
    proj = _dot(h.astype(BF16), w_ref[...])
    hy_ref[0] = proj[:, :3 * D_HY]
    s5_ref[0] = proj[:, 3 * D_HY:]


def _in_proj(x3, pos, mod3, w_in_bf, tm):
    nb, lt, _ = x3.shape
    has_pos = pos is not None
    per_batch = mod3.shape[0] > 1
    midx = (lambda b, i: (b, 0, 0)) if per_batch else (lambda b, i: (0, 0, 0))
    in_specs = [pl.BlockSpec((1, tm, D_MODEL), lambda b, i: (b, i, 0))]
    args = [x3]
    if has_pos:
        in_specs.append(pl.BlockSpec((tm, D_MODEL), lambda b, i: (i, 0)))
        args.append(pos)
    in_specs += [pl.BlockSpec((1, 1, 6 * D_MODEL), midx),
                 pl.BlockSpec((D_MODEL, 3 * D_HY + D_S5), lambda b, i: (0, 0))]
    args += [mod3, w_in_bf]
    return pl.pallas_call(
        functools.partial(_in_kernel, has_pos),
        grid=(nb, lt // tm),
        in_specs=in_specs,
        out_specs=[pl.BlockSpec((1, tm, 3 * D_HY), lambda b, i: (b, i, 0)),
                   pl.BlockSpec((1, tm, D_S5), lambda b, i: (b, i, 0))],
        out_shape=[jax.ShapeDtypeStruct((nb, lt, 3 * D_HY), F32),
                   jax.ShapeDtypeStruct((nb, lt, D_S5), F32)],
        compiler_params=_cparams(("arbitrary", "arbitrary")),
        name=f"in_proj{nb}",
    )(*args)


def _route(logits):
    lane = lax.broadcasted_iota(jnp.int32, logits.shape, 1)
    lane_f = lane.astype(F32)
    neg = -jnp.inf
    big = float(LANES)
    m1 = (lane >= N_EXPERTS) & (lane < N_EXPERTS + N_EGROUPS)
    l1 = jnp.where(m1, logits, neg)
    top1 = jnp.max(l1, axis=-1, keepdims=True)
    grp = jnp.min(jnp.where(l1 == top1, lane_f, big), axis=-1, keepdims=True) - float(N_EXPERTS)
    den = jnp.sum(jnp.where(m1, jnp.exp(logits - top1), 0.0), axis=-1, keepdims=True)
    p_grp = 1.0 / den
    lo = grp * float(N_EPG)
    m2 = (lane_f >= lo) & (lane_f < lo + float(N_EPG))
    l2 = jnp.where(m2, logits, neg)
    v1 = jnp.max(l2, axis=-1, keepdims=True)
    i1 = jnp.min(jnp.where(l2 == v1, lane_f, big), axis=-1, keepdims=True)
    l2b = jnp.where(lane_f == i1, neg, l2)
    v2 = jnp.max(l2b, axis=-1, keepdims=True)
    i2 = jnp.min(jnp.where(l2b == v2, lane_f, big), axis=-1, keepdims=True)
    e = jnp.exp(v2 - v1)
    w1 = 1.0 / (1.0 + e)
    w2 = e / (1.0 + e)
    gates = jnp.where(lane_f == i1, w1 * p_grp, 0.0) + jnp.where(lane_f == i2, w2 * p_grp, 0.0)
    return jnp.where(lane_f == grp + float(N_EXPERTS), 1.0, gates)


def _out_kernel(n_ctx_blocks, xc_ref, xl_ref, pos_ref, yhyc_ref, yhyl_ref, ys5c_ref, ys5l_ref, mod_ref,
                wglu_ref, bglu_ref, ong_ref, wout_ref, ln1g_ref, ln1b_ref, wrh_ref, wrl_ref, br_ref,
                x1_ref, h2_ref, gate_ref):
    is_ctx = pl.program_id(0) < n_ctx_blocks
    x = jnp.where(is_ctx, xc_ref[...], xl_ref[...] + pos_ref[...])
    y = jnp.where(is_ctx, ys5c_ref[...], ys5l_ref[...])
    y_hy = jnp.where(is_ctx, yhyc_ref[...], yhyl_ref[...])
    s5 = jax.nn.gelu(y) * jax.nn.sigmoid(_dot(y.astype(BF16), wglu_ref[...]) + bglu_ref[...])
    m_hy = _rms(y_hy) * ong_ref[:, 0:D_HY]
    m_s5 = _rms(s5) * ong_ref[:, D_HY:]
    o = (_dot(m_hy.astype(BF16), wout_ref[0:D_HY, :]) + _dot(m_s5.astype(BF16), wout_ref[D_HY:, :]))
    g1 = mod_ref[0, :, 2 * D_MODEL:3 * D_MODEL]
    sh2 = mod_ref[0, :, 3 * D_MODEL:4 * D_MODEL]
    sc2 = mod_ref[0, :, 4 * D_MODEL:5 * D_MODEL]
    x1 = _norm(ALPHA * x + g1 * o) * ln1g_ref[...] + ln1b_ref[...]
    x1_ref[...] = x1
    h2 = _norm(x1) * (1.0 + sc2) + sh2
    h2_ref[...] = h2.astype(BF16)
    hh, hl = _split(h2)
    logits = (_dot(hh, wrh_ref[...]) + _dot(hl, wrh_ref[...]) + _dot(hh, wrl_ref[...]) + br_ref[...])
    gate_ref[...] = _route(logits)


def _out_proj(xc, xl, pos, yhy_c, yhy_l, ys5_c, ys5_l, mod, wglu_bf, bglu, ong, wout_bf, ln1g, ln1b,
              wr_hi, wr_lo, br, tm):
    n_ctx, n_lat = xc.shape[0], xl.shape[0]
    l_lat = pos.shape[0]
    ncb, nlb, npb = n_ctx // tm, n_lat // tm, l_lat // tm
    ctx = lambda w: pl.BlockSpec((tm, w), lambda i: (jnp.minimum(i, ncb - 1), 0))
    lat = lambda w: pl.BlockSpec((tm, w), lambda i: (jnp.maximum(i - ncb, 0), 0))
    full = lambda shape: pl.BlockSpec(shape, lambda i: (0,) * len(shape))
    out = lambda w: pl.BlockSpec((tm, w), lambda i: (i, 0))
    mod_idx = lambda i: (jnp.where(i < ncb, 0, 1 + jnp.maximum(i - ncb, 0) // npb), 0, 0)
    n_all = n_ctx + n_lat
    return pl.pallas_call(
        functools.partial(_out_kernel, ncb),
        grid=(ncb + nlb,),
        in_specs=[ctx(D_MODEL), lat(D_MODEL),
                  pl.BlockSpec((tm, D_MODEL), lambda i: (jnp.maximum(i - ncb, 0) % npb, 0)),
                  ctx(D_HY), lat(D_HY), ctx(D_S5), lat(D_S5),
                  pl.BlockSpec((1, 1, 6 * D_MODEL), mod_idx),
                  full((D_S5, D_S5)), full((1, D_S5)), full((1, D_MODEL)), full((D_MODEL, D_MODEL)),
                  full((1, D_MODEL)), full((1, D_MODEL)), full((D_MODEL, LANES)), full((D_MODEL, LANES)),
                  full((1, LANES))],
        out_specs=[out(D_MODEL), out(D_MODEL), out(LANES)],
        out_shape=[jax.ShapeDtypeStruct((n_all, D_MODEL), F32),
                   jax.ShapeDtypeStruct((n_all, D_MODEL), BF16),
                   jax.ShapeDtypeStruct((n_all, LANES), F32)],
        compiler_params=_cparams(("arbitrary",)),
        name="out_proj",
    )(xc, xl, pos, yhy_c, yhy_l, ys5_c, ys5_l, mod.reshape(mod.shape[0], 1, 6 * D_MODEL),
      wglu_bf, bglu, ong, wout_bf, ln1g, ln1b, wr_hi, wr_lo, br)


def _perm_t(gates, loc_ref, s):
    n = gates.shape[0]
    lane = lax.broadcasted_iota(jnp.int32, gates.shape, 1)
    oh = jnp.where((lane >= N_EXPERTS) & (lane < N_EXPERTS + N_EGROUPS), gates, 0.0)
    r = lax.broadcasted_iota(jnp.int32, (n, n), 0)
    c = lax.broadcasted_iota(jnp.int32, (n, n), 1)
    earlier = jnp.where(c < r, 1.0, 0.0).astype(BF16)
    cum = _dot(earlier, oh.astype(BF16))
    rank = jnp.sum(cum * oh, axis=-1, keepdims=True)
    lane1 = lax.broadcasted_iota(jnp.int32, (1, LANES), 1)
    locv = jnp.zeros((1, LANES), F32)
    for grp in range(N_EGROUPS):
        locv = jnp.where(lane1 == N_EXPERTS + grp, loc_ref[N_EGROUPS * s + grp].astype(F32), locv)
    dest = rank + jnp.sum(oh * locv, axis=-1, keepdims=True)
    slot = lax.broadcasted_iota(jnp.int32, (n, MOE_SLOTS), 1).astype(F32)
    return jnp.where(slot == dest, 1.0, 0.0)


def _segment_copies(s, loc_ref, len_ref, off_ref, make):
    for grp in range(N_EGROUPS):
        loc = loc_ref[N_EGROUPS * s + grp]
        off = off_ref[N_EGROUPS * s + grp]
        n_units = len_ref[N_EGROUPS * s + grp] // MOE_UNIT

        def body(i, carry):
            make(pl.multiple_of(loc + MOE_UNIT * i, MOE_UNIT), pl.multiple_of(off + MOE_UNIT * i, MOE_UNIT))
            return carry

        lax.fori_loop(0, n_units, body, 0)


def _moe_sort_kernel(loc_ref, len_ref, off_ref, h_ref, gate_ref, xs_in, gs_in, xs_hbm, gs_hbm,
                     xs_v, gs_v, sem):
    del xs_in, gs_in
    s = pl.program_id(0)
    slot = s % 2
    gates = gate_ref[...]
    p = _perm_t(gates, loc_ref, s).T.astype(BF16)
    xs_v[slot] = _dot(p, h_ref[...]).astype(BF16)
    g_hi = gates.astype(BF16)
    r1 = gates - g_hi.astype(F32)
    g_mid = r1.astype(BF16)
    g_lo = (r1 - g_mid.astype(F32)).astype(BF16)
    gs_v[slot] = _dot(p, g_hi) + _dot(p, g_mid) + _dot(p, g_lo)

    def copies(buf):
        def x_copy(lr, gr):
            return pltpu.make_async_copy(xs_v.at[buf, pl.ds(lr, MOE_UNIT), :],
                                         xs_hbm.at[pl.ds(gr, MOE_UNIT), :], sem.at[0, buf])

        def g_copy(lr, gr):
            return pltpu.make_async_copy(gs_v.at[buf, pl.ds(lr, MOE_UNIT), :],
                                         gs_hbm.at[pl.ds(gr, MOE_UNIT), :], sem.at[1, buf])

        def start(lr, gr):
            x_copy(lr, gr).start()
            g_copy(lr, gr).start()

        def wait(lr, gr):
            x_copy(lr, gr).wait()
            g_copy(lr, gr).wait()

        return start, wait

    _segment_copies(s, loc_ref, len_ref, off_ref, copies(slot)[0])

    @pl.when(s > 0)
    def _():
        _segment_copies(s - 1, loc_ref, len_ref, off_ref, copies(1 - slot)[1])

    @pl.when(s == pl.num_programs(0) - 1)
    def _():
        _segment_copies(s, loc_ref, len_ref, off_ref, copies(slot)[1])


def _moe_expert_kernel(bg_ref, nb_ref, xs_ref, gs_ref, wg_ref, wu_ref, wd_ref, o_ref):
    i = pl.program_id(0)

    @pl.when(i < nb_ref[0])
    def _():
        grp = bg_ref[i]
        x = xs_ref[...]
        gates = gs_ref[...]
        lane = lax.broadcasted_iota(jnp.int32, gates.shape, 1)
        acc = jnp.zeros(o_ref.shape, F32)
        for e in range(N_EPG):
            a = _dot(x, wg_ref[e])
            u = _dot(x, wu_ref[e])
            ge = jnp.sum(jnp.where(lane == N_EPG * grp + e, gates, 0.0), axis=-1, keepdims=True)
            hid = jax.nn.silu(a) * u * ge
            acc = acc + _dot(hid.astype(BF16), wd_ref[e])
        o_ref[...] = acc

    @pl.when(i >= nb_ref[0])
    def _():
        o_ref[...] = jnp.zeros_like(o_ref)


def _moe_combine_kernel(loc_ref, len_ref, off_ref, gate_ref, x1_ref, mod_ref, ln2g_ref, ln2b_ref, o_hbm,
                        ctx_ref, lat_ref, o_v, sem, *, n_ctx_tiles):
    s = pl.program_id(0)
    slot = s % 2

    def copies(buf):
        def o_copy(lr, gr):
            return pltpu.make_async_copy(o_hbm.at[pl.ds(gr, MOE_UNIT), :],
                                         o_v.at[buf, pl.ds(lr, MOE_UNIT), :], sem.at[buf])

        return (lambda lr, gr: o_copy(lr, gr).start()), (lambda lr, gr: o_copy(lr, gr).wait())

    @pl.when(s == 0)
    def _():
        o_v[...] = jnp.zeros_like(o_v)
        _segment_copies(s, loc_ref, len_ref, off_ref, copies(slot)[0])

    @pl.when(s + 1 < pl.num_programs(0))
    def _():
        _segment_copies(s + 1, loc_ref, len_ref, off_ref, copies(1 - slot)[0])

    pt = _perm_t(gate_ref[...], loc_ref, s).astype(BF16)
    _segment_copies(s, loc_ref, len_ref, off_ref, copies(slot)[1])
    oh, ol = _split(o_v[slot])
    f = _dot(pt, oh) + _dot(pt, ol)
    g2 = mod_ref[0, :, 5 * D_MODEL:6 * D_MODEL]
    x2 = _norm(ALPHA * x1_ref[...] + g2 * f) * ln2g_ref[...] + ln2b_ref[...]

    @pl.when(s < n_ctx_tiles)
    def _():
        ctx_ref[...] = x2

    @pl.when(s >= n_ctx_tiles)
    def _():
        lat_ref[...] = x2


def _moe_plan(gates_all, n_blocks):
    n_tiles = gates_all.shape[0] // MOE_ST
    oh = gates_all[:, N_EXPERTS:N_EXPERTS + N_EGROUPS]
    cnt = jnp.sum(oh.reshape(n_tiles, MOE_ST, N_EGROUPS), axis=1).astype(jnp.int32)
    len16 = ((cnt + MOE_UNIT - 1) // MOE_UNIT) * MOE_UNIT
    loc = jnp.cumsum(len16, axis=1) - len16
    rows_g = jnp.sum(len16, axis=0)
    reg_g = ((rows_g + MOE_TM - 1) // MOE_TM) * MOE_TM
    reg_start = jnp.cumsum(reg_g) - reg_g
    off = reg_start[None, :] + jnp.cumsum(len16, axis=0) - len16
    blk_end = jnp.cumsum(reg_g // MOE_TM)
    bi = jnp.arange(n_blocks, dtype=jnp.int32)
    blk_group = jnp.minimum(jnp.sum((bi[:, None] >= blk_end[None, :]).astype(jnp.int32), axis=1),
                            N_EGROUPS - 1)
    flat = lambda a: a.reshape(-1).astype(jnp.int32)
    return flat(loc), flat(len16), flat(off), blk_group.astype(jnp.int32), blk_end[-1:].astype(jnp.int32)


def _moe(h2_all, gates_all, x1_all, mod, wg_bf, wu_bf, wd_bf, ln2g, ln2b, n_ctx, tokens_per_mod_row):
    n_tok = h2_all.shape[0]
    n_tiles = n_tok // MOE_ST
    n_ctx_tiles = n_ctx // MOE_ST
    max_rows = n_tok + n_tiles * N_EGROUPS * (MOE_UNIT - 1) + N_EGROUPS * (MOE_TM - 1)
    n_blocks = -(-max_rows // MOE_TM)
    n_rows = n_blocks * MOE_TM
    loc, len16, off, blk_group, n_used = _moe_plan(gates_all, n_blocks)

    tile = lambda w: pl.BlockSpec((MOE_ST, w), lambda s, *_: (s, 0))
    anyspec = pl.BlockSpec(memory_space=pl.ANY)
    xs, gs = pl.pallas_call(
        _moe_sort_kernel,
        grid_spec=pltpu.PrefetchScalarGridSpec(
            num_scalar_prefetch=3, grid=(n_tiles,),
            in_specs=[tile(D_MODEL), tile(LANES), anyspec, anyspec],
            out_specs=[anyspec, anyspec],
            scratch_shapes=[pltpu.VMEM((2, MOE_SLOTS, D_MODEL), BF16), pltpu.VMEM((2, MOE_SLOTS, LANES), F32),
                            pltpu.SemaphoreType.DMA((2, 2))]),
        out_shape=[jax.ShapeDtypeStruct((n_rows, D_MODEL), BF16),
                   jax.ShapeDtypeStruct((n_rows, LANES), F32)],
        input_output_aliases={5: 0, 6: 1},
        compiler_params=_cparams(("arbitrary",)),
        name="moe_sort",
    )(loc, len16, off, h2_all, gates_all, jnp.zeros((n_rows, D_MODEL), BF16), jnp.zeros((n_rows, LANES), F32))

    blk = lambda w: pl.BlockSpec((MOE_TM, w), lambda i, bg, nb: (jnp.minimum(i, nb[0] - 1), 0))
    wspec = lambda a, b: pl.BlockSpec((N_EPG, a, b), lambda i, bg, nb: (bg[i], 0, 0))
    o_sorted = pl.pallas_call(
        _moe_expert_kernel,
        grid_spec=pltpu.PrefetchScalarGridSpec(
            num_scalar_prefetch=2, grid=(n_blocks,),
            in_specs=[blk(D_MODEL), blk(LANES), wspec(D_MODEL, D_EXPERT), wspec(D_MODEL, D_EXPERT),
                      wspec(D_EXPERT, D_MODEL)],
            out_specs=pl.BlockSpec((MOE_TM, D_MODEL), lambda i, bg, nb: (i, 0))),
        out_shape=jax.ShapeDtypeStruct((n_rows, D_MODEL), F32),
        compiler_params=_cparams(("arbitrary",)),
        name="moe_experts",
    )(blk_group, n_used, xs, gs, wg_bf, wu_bf, wd_bf)

    lat_per_row = tokens_per_mod_row // MOE_ST

    def mod_idx(s, *_):
        return (jnp.where(s < n_ctx_tiles, 0, 1 + (s - n_ctx_tiles) // lat_per_row), 0, 0)

    vec = pl.BlockSpec((1, D_MODEL), lambda s, *_: (0, 0))
    return pl.pallas_call(
        functools.partial(_moe_combine_kernel, n_ctx_tiles=n_ctx_tiles),
        grid_spec=pltpu.PrefetchScalarGridSpec(
            num_scalar_prefetch=3, grid=(n_tiles,),
            in_specs=[tile(LANES), tile(D_MODEL), pl.BlockSpec((1, 1, 6 * D_MODEL), mod_idx), vec, vec,
                      anyspec],
            out_specs=[pl.BlockSpec((MOE_ST, D_MODEL), lambda s, *_: (jnp.minimum(s, n_ctx_tiles - 1), 0)),
                       pl.BlockSpec((MOE_ST, D_MODEL), lambda s, *_: (jnp.maximum(s - n_ctx_tiles, 0), 0))],
            scratch_shapes=[pltpu.VMEM((2, MOE_SLOTS, D_MODEL), F32), pltpu.SemaphoreType.DMA((2,))]),
        out_shape=[jax.ShapeDtypeStruct((n_ctx, D_MODEL), F32),
                   jax.ShapeDtypeStruct((n_tok - n_ctx, D_MODEL), F32)],
        compiler_params=_cparams(("arbitrary",)),
        name="moe_combine",
    )(loc, len16, off, gates_all, x1_all, mod.reshape(mod.shape[0], 1, 6 * D_MODEL), ln2g, ln2b, o_sorted)


def _grid_pos_embed(n_tokens):
    rows = n_tokens // GRID_W
    row = jnp.repeat(jnp.arange(rows, dtype=F32), GRID_W)
    col = jnp.tile(jnp.arange(GRID_W, dtype=F32), rows)
    quarter = D_MODEL // 4
    omega = 1.0 / (POS_BASE ** (jnp.arange(quarter, dtype=F32) / quarter))
    er = row[:, None] * omega
    ec = col[:, None] * omega
    return jnp.concatenate([jnp.sin(er), jnp.cos(er), jnp.sin(ec), jnp.cos(ec)], axis=-1)


def _tables(n_tok):
    cm, sm = _dft_tables(n_tok)
    out = []
    for t in (cm, sm, np.ascontiguousarray(sm.T)):
        hi, lo = _split(jnp.asarray(t))
        out += [hi, lo]
    return tuple(out)


def _mixers(x, pos, mod3, h0_re, h0_im, tabs, filt, s5ops, wts, tm):
    bsz, n_tok, _ = x.shape
    shared = mod3.shape[0] == 1
    x3 = x.reshape(1, bsz * n_tok, D_MODEL) if shared else x
    proj_hy, u_s5 = _in_proj(x3, pos, mod3, wts['w_in'], tm)
    y_hy = _hyena(proj_hy.reshape(bsz, n_tok, 3 * D_HY), tabs, filt,
                  wts['hy_conv_w'], wts['hy_conv_b'], wts['hy_fbias'])
    y_s5, f_re, f_im = _s5(u_s5.reshape(bsz, n_tok, D_S5), s5ops, h0_re, h0_im)
    return y_hy.reshape(bsz * n_tok, D_HY), y_s5.reshape(bsz * n_tok, D_S5), f_re, f_im


def kernel(x_prompt, x_sample, state_s5_re, state_s5_im, c, c_ctx, w_ada, b_ada, w_in, hy_conv_w, hy_conv_b, hy_f_w1, hy_f_b1, hy_f_w2, hy_f_b2, hy_f_w3, hy_freq, hy_fbias, s5_a_re, s5_a_im, s5_log_dt, s5_b_re, s5_b_im, s5_c_re, s5_c_im, s5_d, s5_w_glu, s5_b_glu, out_norm_g, w_out, ln1_g, ln1_b, moe_w_r1, moe_b_r1, moe_w_r2, moe_b_r2, moe_w_gate, moe_w_up, moe_w_down, ln2_g, ln2_b):
    b_ctx, l_ctx, _ = x_prompt.shape
    b_lat, l_lat, _ = x_sample.shape
    g, p = S5_GROUPS, S5_STATE
    assert w_ada.shape[0] == 1, "single-layer trunk"
    l = 0

    nrow = 16
    cond = jnp.concatenate([c_ctx[None, :], c, jnp.zeros((nrow - 1 - b_lat, D_MODEL), F32)], axis=0)
    mod = _ada(cond, w_ada[l], b_ada[l])
    mod_ctx = mod[0:1].reshape(1, 1, 6 * D_MODEL)
    mod_lat = mod[1:1 + b_lat].reshape(b_lat, 1, 6 * D_MODEL)

    wr = jnp.concatenate([moe_w_r2[l].transpose(1, 0, 2).reshape(D_MODEL, N_EXPERTS), moe_w_r1[l]], axis=1)
    wr = jnp.pad(wr, ((0, 0), (0, LANES - wr.shape[1])))
    br = jnp.concatenate([moe_b_r2[l].reshape(-1), moe_b_r1[l]])
    br = jnp.pad(br, (0, LANES - br.shape[0])).reshape(1, LANES)
    wr_hi, wr_lo = _split(wr)

    wts = {
        'w_in': w_in[l].astype(BF16), 'hy_conv_w': hy_conv_w[l], 'hy_conv_b': hy_conv_b[l],
        'hy_fbias': hy_fbias[l], 'w_glu': s5_w_glu[l].astype(BF16), 'b_glu': s5_b_glu[l].reshape(1, -1),
        'out_norm_g': out_norm_g[l].reshape(1, -1), 'w_out': w_out[l].astype(BF16),
        'ln1_g': ln1_g[l].reshape(1, -1), 'ln1_b': ln1_b[l].reshape(1, -1),
        'wr_hi': wr_hi, 'wr_lo': wr_lo, 'br': br,
        'w_gate': moe_w_gate[l].astype(BF16), 'w_up': moe_w_up[l].astype(BF16),
        'w_down': moe_w_down[l].astype(BF16),
        'ln2_g': ln2_g[l].reshape(1, -1), 'ln2_b': ln2_b[l].reshape(1, -1),
    }

    s5ops = _s5_operators(s5_a_re[l], s5_a_im[l], s5_log_dt[l], s5_b_re[l], s5_b_im[l],
                          s5_c_re[l], s5_c_im[l], s5_d[l])
    tabs_ctx = _tables(l_ctx)
    tabs_lat = _tables(l_lat)
    filt_args = (hy_f_w1[l], hy_f_b1[l], hy_f_w2[l], hy_f_b2[l], hy_f_w3[l], hy_freq[l])
    filt_ctx = _hyena_filters(l_ctx, tabs_ctx, *filt_args)
    filt_lat = _hyena_filters(l_lat, tabs_lat, *filt_args)

    zero = jnp.zeros((g, b_ctx, 2 * p), F32)
    yhy_c, ys5_c, f_re, f_im = _mixers(x_prompt, None, mod_ctx, zero, zero, tabs_ctx, filt_ctx, s5ops, wts, 512)
    unpack = lambda f: f.reshape(g, b_ctx, 2, p).transpose(1, 2, 0, 3)[:, None]
    new_re, new_im = unpack(f_re), unpack(f_im)

    pack = lambda s: s[:, l].transpose(2, 0, 1, 3).reshape(g, b_lat, 2 * p)
    pos = _grid_pos_embed(l_lat)
    yhy_l, ys5_l, _, _ = _mixers(x_sample, pos, mod_lat, pack(state_s5_re), pack(state_s5_im),
                                 tabs_lat, filt_lat, s5ops, wts, 512)

    n_ctx = b_ctx * l_ctx
    x1_all, h2_all, gates_all = _out_proj(
        x_prompt.reshape(n_ctx, D_MODEL), x_sample.reshape(b_lat * l_lat, D_MODEL), pos,
        yhy_c, yhy_l, ys5_c, ys5_l, mod, wts['w_glu'], wts['b_glu'], wts['out_norm_g'], wts['w_out'],
        wts['ln1_g'], wts['ln1_b'], wts['wr_hi'], wts['wr_lo'], wts['br'], 512)
    y_ctx, y_lat = _moe(h2_all, gates_all, x1_all, mod,
                        wts['w_gate'], wts['w_up'], wts['w_down'], wts['ln2_g'], wts['ln2_b'],
                        n_ctx, l_lat)
    return (y_ctx.reshape(x_prompt.shape), y_lat.reshape(x_sample.shape), new_re, new_im)
```

```python
import functools
import math

import numpy as np
import jax
import jax.numpy as jnp
from jax import lax
from jax.experimental import pallas as pl
from jax.experimental.pallas import tpu as pltpu

F32 = jnp.float32
BF16 = jnp.bfloat16

D_MODEL = 1024
DEPTH = 1
GRID_W = 64
POS_BASE = 10000.0
D_HY = 512
D_S5 = 512
S5_CH = 16
S5_GROUPS = 32
S5_STATE = 64
S5_CHUNK = 16
S5_ROW = S5_CHUNK * S5_CH
HY_BANDS = 16
HY_EMB = 1 + 2 * HY_BANDS
HY_HID = 64
HY_MIN_DECAY = math.log(1e-2) / 1.5
HY_MAX_DECAY = math.log(1e-2) / 0.3
N_EGROUPS = 4
N_EPG = 4
N_EXPERTS = 16
D_EXPERT = 512
LN_EPS = 1e-5
ALPHA = (2.0 * DEPTH) ** 0.25
LANES = 128
S5_GB = LANES // S5_CH
HY_CW = 512
MOE_ST = 256
MOE_SLOTS = 384
MOE_UNIT = 16
MOE_TM = 512
VMEM_LIMIT = 60000 * 1024


def _cparams(sem):
    return pltpu.CompilerParams(dimension_semantics=sem, vmem_limit_bytes=VMEM_LIMIT)


def _split(x):
    hi = x.astype(BF16)
    lo = (x - hi.astype(F32)).astype(BF16)
    return hi, lo


def _dot(a, b):
    return jnp.dot(a, b, preferred_element_type=F32)


def _dot_t(a, b):
    return lax.dot_general(a, b, (((1,), (1,)), ((), ())), preferred_element_type=F32)


def _mm3(a, b):
    ah, al = _split(a)
    bh, bl = _split(b)
    return _dot(ah, bh) + _dot(al, bh) + _dot(ah, bl)


def _mm3_pre(ah, al, b):
    bh, bl = _split(b)
    return _dot(ah, bh) + _dot(al, bh) + _dot(ah, bl)


def _mm3_t(a, b):
    ah, al = _split(a)
    bh, bl = _split(b)
    return _dot_t(ah, bh) + _dot_t(al, bh) + _dot_t(ah, bl)


def _dot_hp(a, b):
    return jnp.dot(a, b, preferred_element_type=F32, precision=lax.Precision.HIGHEST)


def _norm(x):
    xc = x - jnp.mean(x, axis=-1, keepdims=True)
    return xc * lax.rsqrt(jnp.mean(xc * xc, axis=-1, keepdims=True) + LN_EPS)


def _rms(y):
    return y * lax.rsqrt(jnp.mean(y * y, axis=-1, keepdims=True) + LN_EPS)


def _ada_kernel(cond_ref, w_ref, b_ref, o_ref):
    c = jax.nn.silu(cond_ref[...])
    o_ref[...] = _mm3(c, w_ref[...]) + b_ref[...]


def _ada(cond, w_ada, b_ada):
    nb = cond.shape[0]
    n = w_ada.shape[1]
    tn = 1024
    return pl.pallas_call(
        _ada_kernel,
        grid=(n // tn,),
        in_specs=[pl.BlockSpec((nb, D_MODEL), lambda j: (0, 0)),
                  pl.BlockSpec((D_MODEL, tn), lambda j: (0, j)),
                  pl.BlockSpec((1, tn), lambda j: (0, j))],
        out_specs=pl.BlockSpec((nb, tn), lambda j: (0, j)),
        out_shape=jax.ShapeDtypeStruct((nb, n), F32),
        compiler_params=_cparams(("arbitrary",)),
        name="ada",
    )(cond, w_ada, b_ada.reshape(1, n))


def _dft_tables(n_tok):
    n = 2 * n_tok
    idx = np.arange(n_tok, dtype=np.int64)
    m = (idx[:, None] * idx[None, :]) % n
    ang = 2.0 * np.pi * m.astype(np.float64) / n
    cm = np.cos(ang)
    sm = -np.sin(ang)
    sm[0, :] = 1.0 - 2.0 * (idx % 2)
    return cm.astype(np.float32), sm.astype(np.float32)


def _filt_kernel(n_tok, z_ref, t_ref, w1_ref, b1_ref, w2_ref, b2_ref, fr_ref, w3f_ref, w3b_ref,
                 dl_ref, cmh_ref, cml_ref, smh_ref, sml_ref, a_ref, bz_ref, dd_ref):
    fr = fr_ref[...]
    h = jnp.sin(fr * (_dot_hp(z_ref[...], w1_ref[...]) + b1_ref[...]))
    h = jnp.sin(fr * (_dot_hp(h, w2_ref[...]) + b2_ref[...]))
    decay = jnp.exp(-t_ref[...] * dl_ref[...])
    row = lax.broadcasted_iota(jnp.int32, decay.shape, 0)
    hf = _dot_hp(h, w3f_ref[...]) * decay
    hb = jnp.where(row == 0, 0.0, _dot_hp(h, w3b_ref[...]) * decay)
    p = hf + hb
    q = hf - hb
    k_re = _mm3_pre(cmh_ref[...], cml_ref[...], p)
    k_im = _mm3_pre(smh_ref[...], sml_ref[...], q)
    sign = jnp.where(row % 2 == 0, 1.0, -1.0)
    nyq = jnp.sum(p * sign, axis=0, keepdims=True)
    inv_n = 1.0 / (2 * n_tok)
    w = jnp.where(row == 0, inv_n, 2.0 * inv_n)
    a_ref[...] = w * k_re
    bz_ref[...] = jnp.where(row == 0, 0.0, w * k_im)
    dd_ref[...] = jnp.where(row == 0, nyq * inv_n, w * k_re)


def _hyena_filters(n_tok, tabs, hy_f_w1, hy_f_b1, hy_f_w2, hy_f_b2, hy_f_w3, hy_freq):
    cmh, cml, smh, sml = tabs[:4]
    t = jnp.linspace(0.0, 1.0, n_tok, dtype=F32)[:, None]
    wv = 2.0 * math.pi * jnp.arange(n_tok, dtype=F32) / n_tok
    fb = jnp.linspace(1e-4, HY_BANDS - 1, HY_BANDS, dtype=F32)
    ang = wv[:, None] * fb[None, :]
    z = jnp.concatenate([t, jnp.cos(ang), -jnp.sin(ang)], axis=-1)
    z = jnp.pad(z, ((0, 0), (0, LANES - HY_EMB)))
    w1 = jnp.pad(hy_f_w1, ((0, LANES - HY_EMB), (0, 0)))
    deltas = jnp.abs(jnp.linspace(HY_MIN_DECAY, HY_MAX_DECAY, D_HY, dtype=F32))[None, :]
    ncb = D_HY // HY_CW
    full = lambda j: (0, 0)
    out_sd = jax.ShapeDtypeStruct((n_tok, 2 * D_HY), F32)
    mat = pl.BlockSpec((n_tok, n_tok), full, pipeline_mode=pl.Buffered(1))
    return pl.pallas_call(
        functools.partial(_filt_kernel, n_tok),
        grid=(2 * ncb,),
        in_specs=[pl.BlockSpec((n_tok, LANES), full),
                  pl.BlockSpec((n_tok, 1), full),
                  pl.BlockSpec((LANES, HY_HID), full),
                  pl.BlockSpec((1, HY_HID), full),
                  pl.BlockSpec((HY_HID, HY_HID), full),
                  pl.BlockSpec((1, HY_HID), full),
                  pl.BlockSpec((1, HY_HID), full),
                  pl.BlockSpec((HY_HID, HY_CW), lambda j: (0, 2 * ncb * (j // ncb) + j % ncb)),
                  pl.BlockSpec((HY_HID, HY_CW), lambda j: (0, 2 * ncb * (j // ncb) + ncb + j % ncb)),
                  pl.BlockSpec((1, HY_CW), lambda j: (0, j % ncb)),
                  mat, mat, mat, mat],
        out_specs=[pl.BlockSpec((n_tok, HY_CW), lambda j: (0, j))] * 3,
        out_shape=[out_sd] * 3,
        compiler_params=_cparams(("arbitrary",)),
        name=f"filt{n_tok}",
    )(z, t, w1, hy_f_b1.reshape(1, -1), hy_f_w2, hy_f_b2.reshape(1, -1), hy_freq.reshape(1, -1),
      hy_f_w3, hy_f_w3, deltas, cmh, cml, smh, sml)


def _hyena_kernel(pv_ref, p1_ref, p2_ref, cwv_ref, cw1_ref, cw2_ref, cbv_ref, cb1_ref, cb2_ref,
                  fbias_ref, cm_ref, sm_ref, st_ref,
                  a0_ref, b0_ref, d0_ref, a1_ref, b1_ref, d1_ref, o_ref):
    n_tok = pv_ref.shape[1]
    row = lax.broadcasted_iota(jnp.int32, (n_tok, pv_ref.shape[2]), 0)

    def short_conv(p_ref, cw_ref, cb_ref):
        p = p_ref[0]
        prev = jnp.where(row == 0, 0.0, pltpu.roll(p, 1, axis=0))
        nxt = jnp.where(row == n_tok - 1, 0.0, pltpu.roll(p, n_tok - 1, axis=0))
        return cb_ref[...] + prev * cw_ref[0:1, :] + p * cw_ref[1:2, :] + nxt * cw_ref[2:3, :]

    cm = cm_ref[...]

    def fftconv(u, a_ref, b_ref, d_ref, skip):
        ub = u.astype(BF16)
        u_re = _dot(cm, ub)
        u_im = _dot(sm_ref[...], ub)
        a, bz, dd = a_ref[...], b_ref[...], d_ref[...]
        y_re = u_re * a - u_im * bz
        y_im = u_re * bz + u_im * dd
        y = _dot(cm, y_re.astype(BF16)) + _dot(st_ref[...], y_im.astype(BF16))
        return y + u * skip

    v = short_conv(pv_ref, cwv_ref, cbv_ref)
    x1 = short_conv(p1_ref, cw1_ref, cb1_ref)
    z = x1 * fftconv(v, a0_ref, b0_ref, d0_ref, fbias_ref[0:1, :])
    x2 = short_conv(p2_ref, cw2_ref, cb2_ref)
    o_ref[0] = x2 * fftconv(z, a1_ref, b1_ref, d1_ref, fbias_ref[1:2, :])


def _hyena(proj_hy, tabs, filt, hy_conv_w, hy_conv_b, hy_fbias):
    bsz, n_tok, _ = proj_hy.shape
    ncb = D_HY // HY_CW
    cmh, _, smh, _, sth, _ = tabs
    fa, fbz, fdd = filt
    cb = hy_conv_b.reshape(1, -1)
    mat = pl.BlockSpec((n_tok, n_tok), lambda b, c: (0, 0), pipeline_mode=pl.Buffered(1))

    def pspec(k):
        return pl.BlockSpec((1, n_tok, HY_CW), lambda b, c: (b, 0, k * ncb + c))

    def cwspec(k):
        return pl.BlockSpec((3, HY_CW), lambda b, c: (0, k * ncb + c))

    def cbspec(k):
        return pl.BlockSpec((1, HY_CW), lambda b, c: (0, k * ncb + c))

    def fspec(o):
        mode = pl.Buffered(1) if ncb == 1 else None
        return pl.BlockSpec((n_tok, HY_CW), lambda b, c: (0, o * ncb + c), pipeline_mode=mode)

    return pl.pallas_call(
        _hyena_kernel,
        grid=(bsz, ncb),
        in_specs=[pspec(0), pspec(1), pspec(2), cwspec(0), cwspec(1), cwspec(2),
                  cbspec(0), cbspec(1), cbspec(2),
                  pl.BlockSpec((2, HY_CW), lambda b, c: (0, c)),
                  mat, mat, mat,
                  fspec(0), fspec(0), fspec(0), fspec(1), fspec(1), fspec(1)],
        out_specs=pl.BlockSpec((1, n_tok, HY_CW), lambda b, c: (b, 0, c)),
        out_shape=jax.ShapeDtypeStruct((bsz, n_tok, D_HY), F32),
        compiler_params=_cparams(("arbitrary", "arbitrary")),
        name=f"hyena{n_tok}",
    )(proj_hy, proj_hy, proj_hy, hy_conv_w, hy_conv_w, hy_conv_w, cb, cb, cb, hy_fbias,
      cmh, smh, sth, fa, fbz, fdd, fa, fbz, fdd)


def _s5ops_kernel(are_ref, aim_ref, ldt_ref, btr_ref, bti_ref, cre_ref, cim_ref, d_ref,
                  mt_ref, erh_ref, erl_ref, eih_ref, eil_ref, gr_ref, gi_ref, atr_ref, ati_ref,
                  er_ref, ei_ref):
    a_re, a_im = are_ref[0], aim_ref[0]
    dt = jnp.exp(ldt_ref[0])
    mag = jnp.exp(a_re * dt)
    ab_re = mag * jnp.cos(a_im * dt)
    ab_im = mag * jnp.sin(a_im * dt)
    n_re, n_im = ab_re - 1.0, ab_im
    den = a_re * a_re + a_im * a_im
    q_re = (n_re * a_re + n_im * a_im) / den
    q_im = (n_im * a_re - n_re * a_im) / den
    bt_re, bt_im = btr_ref[0], bti_ref[0]
    bb_re = q_re * bt_re - q_im * bt_im
    bb_im = q_re * bt_im + q_im * bt_re
    c_re, c_im = cre_ref[0, 0:S5_CH, :], cim_ref[0, 0:S5_CH, :]
    pw = [(jnp.ones_like(ab_re), jnp.zeros_like(ab_re))]
    for _ in range(S5_CHUNK):
        pr, pi = pw[-1]
        pw.append((pr * ab_re - pi * ab_im, pr * ab_im + pi * ab_re))
    lane = lax.broadcasted_iota(jnp.int32, ab_re.shape, 1)
    fwd = lane < S5_STATE
    for s in range(S5_CHUNK):
        e_re = jnp.where(fwd, pw[S5_CHUNK - 1 - s][0], pw[s][0])
        e_im = jnp.where(fwd, pw[S5_CHUNK - 1 - s][1], pw[s][1])
        er_ref[pl.ds(S5_CH * s, S5_CH), :] = e_re * bb_re - e_im * bb_im
        ei_ref[pl.ds(S5_CH * s, S5_CH), :] = e_re * bb_im + e_im * bb_re
        g_re = jnp.where(fwd, pw[s + 1][0], pw[S5_CHUNK - s][0])
        g_im = jnp.where(fwd, pw[s + 1][1], pw[S5_CHUNK - s][1])
        gr_ref[0, pl.ds(S5_CH * s, S5_CH), :] = (c_re * g_re - c_im * g_im).astype(BF16)
        gi_ref[0, pl.ds(S5_CH * s, S5_CH), :] = (-(c_re * g_im + c_im * g_re)).astype(BF16)
    atr_ref[0] = pw[S5_CHUNK][0]
    ati_ref[0] = pw[S5_CHUNK][1]
    er, ei = er_ref[...], ei_ref[...]
    erh_ref[0], erl_ref[0] = _split(er)
    eih_ref[0], eil_ref[0] = _split(ei)
    lane2 = lax.broadcasted_iota(jnp.int32, er.shape, 1)
    row2 = lax.broadcasted_iota(jnp.int32, er.shape, 0)
    f2 = lane2 < S5_STATE
    zero = jnp.zeros_like(er)

    def dot_hp_t(a, b):
        return lax.dot_general(a, b, (((1,), (1,)), ((), ())), preferred_element_type=F32,
                               precision=lax.Precision.HIGHEST)

    cp_re, cp_im = cre_ref[0], cim_ref[0]
    kf = dot_hp_t(jnp.where(f2, er, zero), cp_re) - dot_hp_t(jnp.where(f2, ei, zero), cp_im)
    kb = dot_hp_t(jnp.where(f2, zero, er), cp_re) - dot_hp_t(jnp.where(f2, zero, ei), cp_im)
    d_row = d_ref[0]
    steps_per_vreg = LANES // S5_CH
    for half in range(S5_CHUNK // steps_per_vreg):
        acc = zero
        for tt in range(steps_per_vreg):
            t = half * steps_per_vreg + tt
            nf = S5_CH * (S5_CHUNK - 1 - t)
            nb = S5_CH * t
            col_f = jnp.concatenate([kf[nf:], zero[:nf]], axis=0) if nf else kf
            col_b = jnp.concatenate([zero[:nb], kb[:S5_ROW - nb]], axis=0) if nb else kb
            diag = jnp.where((row2 // S5_CH == t) & (row2 % S5_CH == lane2), d_row, 0.0)
            col = col_f + col_b + diag
            r = pltpu.roll(col, S5_CH * tt, axis=1) if tt else col
            acc = jnp.where((lane2 >= S5_CH * tt) & (lane2 < S5_CH * (tt + 1)), r, acc)
        mt_ref[0, :, LANES * half:LANES * (half + 1)] = acc.astype(BF16)


def _s5_operators(s5_a_re, s5_a_im, s5_log_dt, s5_b_re, s5_b_im, s5_c_re, s5_c_im, s5_d):
    g, p, h = S5_GROUPS, S5_STATE, S5_CH
    cat = lambda x: jnp.concatenate([x[0], x[1]], axis=-1)
    a_re = cat(s5_a_re).reshape(g, 1, 2 * p)
    a_im = cat(s5_a_im).reshape(g, 1, 2 * p)
    ldt = cat(jnp.broadcast_to(s5_log_dt[:, :, None], (2, g, p))).reshape(g, 1, 2 * p)
    bt_re = cat(jnp.swapaxes(s5_b_re, -1, -2))
    bt_im = cat(jnp.swapaxes(s5_b_im, -1, -2))
    cpad = lambda c: jnp.pad(jnp.concatenate([c, c], axis=-1), ((0, 0), (0, LANES - h), (0, 0)))
    c_re, c_im = cpad(s5_c_re), cpad(s5_c_im)
    d_row = jnp.pad(s5_d.reshape(g, 1, h), ((0, 0), (0, 0), (0, LANES - h)))
    vec = pl.BlockSpec((1, 1, 2 * p), lambda i: (i, 0, 0))
    hp = pl.BlockSpec((1, h, 2 * p), lambda i: (i, 0, 0))
    sq = pl.BlockSpec((1, LANES, 2 * p), lambda i: (i, 0, 0))
    big = pl.BlockSpec((1, S5_ROW, 2 * p), lambda i: (i, 0, 0))
    mts = pl.BlockSpec((1, S5_ROW, S5_ROW), lambda i: (i, 0, 0))
    big_sd = jax.ShapeDtypeStruct((g, S5_ROW, 2 * p), BF16)
    vec_sd = jax.ShapeDtypeStruct((g, 1, 2 * p), F32)
    return pl.pallas_call(
        _s5ops_kernel,
        grid=(g,),
        in_specs=[vec, vec, vec, hp, hp, sq, sq, vec],
        out_specs=[mts, big, big, big, big, big, big, vec, vec],
        out_shape=[jax.ShapeDtypeStruct((g, S5_ROW, S5_ROW), BF16)] + [big_sd] * 6 + [vec_sd, vec_sd],
        scratch_shapes=[pltpu.VMEM((S5_ROW, 2 * p), F32)] * 2,
        compiler_params=_cparams(("arbitrary",)),
        name="s5ops",
    )(a_re, a_im, ldt, bt_re, bt_im, c_re, c_im, d_row)


def _block_transpose(xs):
    n = len(xs)
    lane = lax.broadcasted_iota(jnp.int32, xs[0].shape, 1)
    xs = list(xs)
    d = n // 2
    while d:
        keep = ((lane // S5_CH) & d) == 0
        for i in range(n):
            if i & d:
                continue
            lo, hi = xs[i], xs[i + d]
            xs[i] = jnp.where(keep, lo, pltpu.roll(hi, S5_CH * d, axis=1))
            xs[i + d] = jnp.where(keep, pltpu.roll(lo, LANES - S5_CH * d, axis=1), hi)
        d //= 2
    return xs


def _s5_kernel(bsz, n_chunks, u_ref, mt_ref, erh_ref, erl_ref, eih_ref, eil_ref, gr_ref, gi_ref,
               atr_ref, ati_ref, h0r_ref, h0i_ref, y_ref, fr_ref, fi_ref,
               ua_ref, ub_ref, ya_ref, yb_ref, sr_ref, si_ref, xfr_ref, xfi_ref, xbr_ref, xbi_ref):
    nc = n_chunks
    spv = LANES // S5_CH
    rsub = min(nc, 32)

    def to_chunks(b, carry):
        for half, dst in ((0, ua_ref), (1, ub_ref)):
            for r0 in range(0, nc, rsub):
                xs = [u_ref[b, pl.ds(S5_CHUNK * r0 + half * spv + tt, rsub, stride=S5_CHUNK), :]
                      for tt in range(spv)]
                for k, blk in enumerate(_block_transpose(xs)):
                    dst[k, pl.ds(r0 * bsz + b, rsub, stride=bsz), :] = blk
        return carry

    lax.fori_loop(0, bsz, to_chunks, 0)

    lane = lax.broadcasted_iota(jnp.int32, (bsz, 2 * S5_STATE), 1)
    fwd = lane < S5_STATE
    lane_all = lax.broadcasted_iota(jnp.int32, (bsz * nc, 2 * S5_STATE), 1)
    fwd_all = lane_all < S5_STATE

    def group(k, carry):
        u = jnp.concatenate([ua_ref[k], ub_ref[k]], axis=1)
        uh, ul = _split(u)
        sr_ref[...] = _dot(uh, erh_ref[k]) + _dot(ul, erh_ref[k]) + _dot(uh, erl_ref[k])
        si_ref[...] = _dot(uh, eih_ref[k]) + _dot(ul, eih_ref[k]) + _dot(uh, eil_ref[k])
        at_re, at_im = atr_ref[k], ati_ref[k]

        def step(i, xc):
            x_re, x_im = xc
            rf = pl.ds(pl.multiple_of(i * bsz, bsz), bsz)
            rb = pl.ds(pl.multiple_of((nc - 1 - i) * bsz, bsz), bsz)
            xfr_ref[rf, :] = x_re
            xfi_ref[rf, :] = x_im
            xbr_ref[rb, :] = x_re
            xbi_ref[rb, :] = x_im
            s_re = jnp.where(fwd, sr_ref[rf, :], sr_ref[rb, :])
            s_im = jnp.where(fwd, si_ref[rf, :], si_ref[rb, :])
            return (at_re * x_re - at_im * x_im + s_re, at_re * x_im + at_im * x_re + s_im)

        x_re, x_im = lax.fori_loop(0, nc, step, (h0r_ref[k], h0i_ref[k]))
        fr_ref[k] = x_re
        fi_ref[k] = x_im
        xp_re = jnp.where(fwd_all, xfr_ref[...], xbr_ref[...]).astype(BF16)
        xp_im = jnp.where(fwd_all, xfi_ref[...], xbi_ref[...]).astype(BF16)
        y = _dot(uh, mt_ref[k]) + _dot_t(xp_re, gr_ref[k]) + _dot_t(xp_im, gi_ref[k])
        ya_ref[k] = y[:, :LANES]
        yb_ref[k] = y[:, LANES:]
        return carry

    lax.fori_loop(0, S5_GB, group, 0)

    def to_tokens(b, carry):
        for half, src in ((0, ya_ref), (1, yb_ref)):
            for r0 in range(0, nc, rsub):
                ys = [src[k, pl.ds(r0 * bsz + b, rsub, stride=bsz), :] for k in range(S5_GB)]
                for tt, blk in enumerate(_block_transpose(ys)):
                    y_ref[b, pl.ds(S5_CHUNK * r0 + half * spv + tt, rsub, stride=S5_CHUNK), :] = blk
        return carry

    lax.fori_loop(0, bsz, to_tokens, 0)


def _s5(u, ops, h0_re, h0_im):
    bsz, n_tok, _ = u.shape
    g, p = S5_GROUPS, S5_STATE
    nc = n_tok // S5_CHUNK
    rows = nc * bsz
    tok = pl.BlockSpec((bsz, n_tok, LANES), lambda j: (0, 0, j))
    gspec = lambda shape: pl.BlockSpec((S5_GB,) + shape, lambda j: (j, 0, 0))
    op = gspec((S5_ROW, 2 * p))
    return pl.pallas_call(
        functools.partial(_s5_kernel, bsz, nc),
        grid=(g // S5_GB,),
        in_specs=[tok, gspec((S5_ROW, S5_ROW)), op, op, op, op, op, op,
                  gspec((1, 2 * p)), gspec((1, 2 * p)), gspec((bsz, 2 * p)), gspec((bsz, 2 * p))],
        out_specs=[tok, gspec((bsz, 2 * p)), gspec((bsz, 2 * p))],
        out_shape=[jax.ShapeDtypeStruct((bsz, n_tok, D_S5), F32),
                   jax.ShapeDtypeStruct((g, bsz, 2 * p), F32),
                   jax.ShapeDtypeStruct((g, bsz, 2 * p), F32)],
        scratch_shapes=([pltpu.VMEM((S5_GB, rows, LANES), F32)] * 4
                        + [pltpu.VMEM((rows, 2 * p), F32)] * 6),
        compiler_params=_cparams(("arbitrary",)),
        name=f"s5_{n_tok}",
    )(u, *ops, h0_re, h0_im)


def _in_kernel(has_pos, *refs):
    if has_pos:
        x_ref, pos_ref, mod_ref, w_ref, hy_ref, s5_ref = refs
        x = x_ref[0] + pos_ref[...]
    else:
        x_ref, mod_ref, w_ref, hy_ref, s5_ref = refs
        x = x_ref[0]
    sh1 = mod_ref[0, :, 0:D_MODEL]
    sc1 = mod_ref[0, :, D_MODEL:2 * D_MODEL]
    h = _norm(x) * (1.0 + sc1) + sh1
    proj = _dot(h.astype(BF16), w_ref[...])
    hy_ref[0] = proj[:, :3 * D_HY]
    s5_ref[0] = proj[:, 3 * D_HY:]


def _in_proj(x3, pos, mod3, w_in_bf, tm):
    nb, lt, _ = x3.shape
    has_pos = pos is not None
    per_batch = mod3.shape[0] > 1
    midx = (lambda b, i: (b, 0, 0)) if per_batch else (lambda b, i: (0, 0, 0))
    in_specs = [pl.BlockSpec((1, tm, D_MODEL), lambda b, i: (b, i, 0))]
    args = [x3]
    if has_pos:
        in_specs.append(pl.BlockSpec((tm, D_MODEL), lambda b, i: (i, 0)))
        args.append(pos)
    in_specs += [pl.BlockSpec((1, 1, 6 * D_MODEL), midx),
                 pl.BlockSpec((D_MODEL, 3 * D_HY + D_S5), lambda b, i: (0, 0))]
    args += [mod3, w_in_bf]
    return pl.pallas_call(
        functools.partial(_in_kernel, has_pos),
        grid=(nb, lt // tm),
        in_specs=in_specs,
        out_specs=[pl.BlockSpec((1, tm, 3 * D_HY), lambda b, i: (b, i, 0)),
                   pl.BlockSpec((1, tm, D_S5), lambda b, i: (b, i, 0))],
        out_shape=[jax.ShapeDtypeStruct((nb, lt, 3 * D_HY), F32),
                   jax.ShapeDtypeStruct((nb, lt, D_S5), F32)],
        compiler_params=_cparams(("arbitrary", "arbitrary")),
        name=f"in_proj{nb}",
    )(*args)


def _route(logits):
    lane = lax.broadcasted_iota(jnp.int32, logits.shape, 1)
    lane_f = lane.astype(F32)
    neg = -jnp.inf
    big = float(LANES)
    m1 = (lane >= N_EXPERTS) & (lane < N_EXPERTS + N_EGROUPS)
    l1 = jnp.where(m1, logits, neg)
    top1 = jnp.max(l1, axis=-1, keepdims=True)
    grp = jnp.min(jnp.where(l1 == top1, lane_f, big), axis=-1, keepdims=True) - float(N_EXPERTS)
    den = jnp.sum(jnp.where(m1, jnp.exp(logits - top1), 0.0), axis=-1, keepdims=True)
    p_grp = 1.0 / den
    lo = grp * float(N_EPG)
    m2 = (lane_f >= lo) & (lane_f < lo + float(N_EPG))
    l2 = jnp.where(m2, logits, neg)
    v1 = jnp.max(l2, axis=-1, keepdims=True)
    i1 = jnp.min(jnp.where(l2 == v1, lane_f, big), axis=-1, keepdims=True)
    l2b = jnp.where(lane_f == i1, neg, l2)
    v2 = jnp.max(l2b, axis=-1, keepdims=True)
    i2 = jnp.min(jnp.where(l2b == v2, lane_f, big), axis=-1, keepdims=True)
    e = jnp.exp(v2 - v1)
    w1 = 1.0 / (1.0 + e)
    w2 = e / (1.0 + e)
    gates = jnp.where(lane_f == i1, w1 * p_grp, 0.0) + jnp.where(lane_f == i2, w2 * p_grp, 0.0)
    return jnp.where(lane_f == grp + float(N_EXPERTS), 1.0, gates)


def _out_kernel(n_ctx_blocks, xc_ref, xl_ref, pos_ref, yhyc_ref, yhyl_ref, ys5c_ref, ys5l_ref, mod_ref,
                wglu_ref, bglu_ref, ong_ref, wout_ref, ln1g_ref, ln1b_ref, wrh_ref, wrl_ref, br_ref,
                x1_ref, h2_ref, gate_ref):
    is_ctx = pl.program_id(0) < n_ctx_blocks
    x = jnp.where(is_ctx, xc_ref[...], xl_ref[...] + pos_ref[...])
    y = jnp.where(is_ctx, ys5c_ref[...], ys5l_ref[...])
    y_hy = jnp.where(is_ctx, yhyc_ref[...], yhyl_ref[...])
    s5 = jax.nn.gelu(y) * jax.nn.sigmoid(_dot(y.astype(BF16), wglu_ref[...]) + bglu_ref[...])
    m_hy = _rms(y_hy) * ong_ref[:, 0:D_HY]
    m_s5 = _rms(s5) * ong_ref[:, D_HY:]
    o = (_dot(m_hy.astype(BF16), wout_ref[0:D_HY, :]) + _dot(m_s5.astype(BF16), wout_ref[D_HY:, :]))
    g1 = mod_ref[0, :, 2 * D_MODEL:3 * D_MODEL]
    sh2 = mod_ref[0, :, 3 * D_MODEL:4 * D_MODEL]
    sc2 = mod_ref[0, :, 4 * D_MODEL:5 * D_MODEL]
    x1 = _norm(ALPHA * x + g1 * o) * ln1g_ref[...] + ln1b_ref[...]
    x1_ref[...] = x1
    h2 = _norm(x1) * (1.0 + sc2) + sh2
    h2_ref[...] = h2.astype(BF16)
    hh, hl = _split(h2)
    logits = (_dot(hh, wrh_ref[...]) + _dot(hl, wrh_ref[...]) + _dot(hh, wrl_ref[...]) + br_ref[...])
    gate_ref[...] = _route(logits)


def _out_proj(xc, xl, pos, yhy_c, yhy_l, ys5_c, ys5_l, mod, wglu_bf, bglu, ong, wout_bf, ln1g, ln1b,
              wr_hi, wr_lo, br, tm):
    n_ctx, n_lat = xc.shape[0], xl.shape[0]
    l_lat = pos.shape[0]
    ncb, nlb, npb = n_ctx // tm, n_lat // tm, l_lat // tm
    ctx = lambda w: pl.BlockSpec((tm, w), lambda i: (jnp.minimum(i, ncb - 1), 0))
    lat = lambda w: pl.BlockSpec((tm, w), lambda i: (jnp.maximum(i - ncb, 0), 0))
    full = lambda shape: pl.BlockSpec(shape, lambda i: (0,) * len(shape))
    out = lambda w: pl.BlockSpec((tm, w), lambda i: (i, 0))
    mod_idx = lambda i: (jnp.where(i < ncb, 0, 1 + jnp.maximum(i - ncb, 0) // npb), 0, 0)
    n_all = n_ctx + n_lat
    return pl.pallas_call(
        functools.partial(_out_kernel, ncb),
        grid=(ncb + nlb,),
        in_specs=[ctx(D_MODEL), lat(D_MODEL),
                  pl.BlockSpec((tm, D_MODEL), lambda i: (jnp.maximum(i - ncb, 0) % npb, 0)),
                  ctx(D_HY), lat(D_HY), ctx(D_S5), lat(D_S5),
                  pl.BlockSpec((1, 1, 6 * D_MODEL), mod_idx),
                  full((D_S5, D_S5)), full((1, D_S5)), full((1, D_MODEL)), full((D_MODEL, D_MODEL)),
                  full((1, D_MODEL)), full((1, D_MODEL)), full((D_MODEL, LANES)), full((D_MODEL, LANES)),
                  full((1, LANES))],
        out_specs=[out(D_MODEL), out(D_MODEL), out(LANES)],
        out_shape=[jax.ShapeDtypeStruct((n_all, D_MODEL), F32),
                   jax.ShapeDtypeStruct((n_all, D_MODEL), BF16),
                   jax.ShapeDtypeStruct((n_all, LANES), F32)],
        compiler_params=_cparams(("arbitrary",)),
        name="out_proj",
    )(xc, xl, pos, yhy_c, yhy_l, ys5_c, ys5_l, mod.reshape(mod.shape[0], 1, 6 * D_MODEL),
      wglu_bf, bglu, ong, wout_bf, ln1g, ln1b, wr_hi, wr_lo, br)


def _perm_t(gates, loc_ref, s):
    n = gates.shape[0]
    lane = lax.broadcasted_iota(jnp.int32, gates.shape, 1)
    oh = jnp.where((lane >= N_EXPERTS) & (lane < N_EXPERTS + N_EGROUPS), gates, 0.0)
    r = lax.broadcasted_iota(jnp.int32, (n, n), 0)
    c = lax.broadcasted_iota(jnp.int32, (n, n), 1)
    earlier = jnp.where(c < r, 1.0, 0.0).astype(BF16)
    cum = _dot(earlier, oh.astype(BF16))
    rank = jnp.sum(cum * oh, axis=-1, keepdims=True)
    lane1 = lax.broadcasted_iota(jnp.int32, (1, LANES), 1)
    locv = jnp.zeros((1, LANES), F32)
    for grp in range(N_EGROUPS):
        locv = jnp.where(lane1 == N_EXPERTS + grp, loc_ref[N_EGROUPS * s + grp].astype(F32), locv)
    dest = rank + jnp.sum(oh * locv, axis=-1, keepdims=True)
    slot = lax.broadcasted_iota(jnp.int32, (n, MOE_SLOTS), 1).astype(F32)
    return jnp.where(slot == dest, 1.0, 0.0)


def _segment_copies(s, loc_ref, len_ref, off_ref, make):
    for grp in range(N_EGROUPS):
        loc = loc_ref[N_EGROUPS * s + grp]
        off = off_ref[N_EGROUPS * s + grp]
        n_units = len_ref[N_EGROUPS * s + grp] // MOE_UNIT

        def body(i, carry):
            make(pl.multiple_of(loc + MOE_UNIT * i, MOE_UNIT), pl.multiple_of(off + MOE_UNIT * i, MOE_UNIT))
            return carry

        lax.fori_loop(0, n_units, body, 0)


def _moe_sort_kernel(loc_ref, len_ref, off_ref, h_ref, gate_ref, xs_in, gs_in, xs_hbm, gs_hbm,
                     xs_v, gs_v, sem):
    del xs_in, gs_in
    s = pl.program_id(0)
    slot = s % 2
    gates = gate_ref[...]
    p = _perm_t(gates, loc_ref, s).T.astype(BF16)
    xs_v[slot] = _dot(p, h_ref[...]).astype(BF16)
    g_hi = gates.astype(BF16)
    r1 = gates - g_hi.astype(F32)
    g_mid = r1.astype(BF16)
    g_lo = (r1 - g_mid.astype(F32)).astype(BF16)
    gs_v[slot] = _dot(p, g_hi) + _dot(p, g_mid) + _dot(p, g_lo)

    def copies(buf):
        def x_copy(lr, gr):
            return pltpu.make_async_copy(xs_v.at[buf, pl.ds(lr, MOE_UNIT), :],
                                         xs_hbm.at[pl.ds(gr, MOE_UNIT), :], sem.at[0, buf])

        def g_copy(lr, gr):
            return pltpu.make_async_copy(gs_v.at[buf, pl.ds(lr, MOE_UNIT), :],
                                         gs_hbm.at[pl.ds(gr, MOE_UNIT), :], sem.at[1, buf])

        def start(lr, gr):
            x_copy(lr, gr).start()
            g_copy(lr, gr).start()

        def wait(lr, gr):
            x_copy(lr, gr).wait()
            g_copy(lr, gr).wait()

        return start, wait

    _segment_copies(s, loc_ref, len_ref, off_ref, copies(slot)[0])

    @pl.when(s > 0)
    def _():
        _segment_copies(s - 1, loc_ref, len_ref, off_ref, copies(1 - slot)[1])

    @pl.when(s == pl.num_programs(0) - 1)
    def _():
        _segment_copies(s, loc_ref, len_ref, off_ref, copies(slot)[1])


def _moe_expert_kernel(bg_ref, nb_ref, xs_ref, gs_ref, wg_ref, wu_ref, wd_ref, o_ref):
    i = pl.program_id(0)

    @pl.when(i < nb_ref[0])
    def _():
        grp = bg_ref[i]
        x = xs_ref[...]
        gates = gs_ref[...]
        lane = lax.broadcasted_iota(jnp.int32, gates.shape, 1)
        acc = jnp.zeros(o_ref.shape, F32)
        for e in range(N_EPG):
            a = _dot(x, wg_ref[e])
            u = _dot(x, wu_ref[e])
            ge = jnp.sum(jnp.where(lane == N_EPG * grp + e, gates, 0.0), axis=-1, keepdims=True)
            hid = jax.nn.silu(a) * u * ge
            acc = acc + _dot(hid.astype(BF16), wd_ref[e])
        o_ref[...] = acc

    @pl.when(i >= nb_ref[0])
    def _():
        o_ref[...] = jnp.zeros_like(o_ref)


def _moe_combine_kernel(loc_ref, len_ref, off_ref, gate_ref, x1_ref, mod_ref, ln2g_ref, ln2b_ref, o_hbm,
                        ctx_ref, lat_ref, o_v, sem, *, n_ctx_tiles):
    s = pl.program_id(0)
    slot = s % 2

    def copies(buf):
        def o_copy(lr, gr):
            return pltpu.make_async_copy(o_hbm.at[pl.ds(gr, MOE_UNIT), :],
                                         o_v.at[buf, pl.ds(lr, MOE_UNIT), :], sem.at[buf])

        return (lambda lr, gr: o_copy(lr, gr).start()), (lambda lr, gr: o_copy(lr, gr).wait())

    @pl.when(s == 0)
    def _():
        o_v[...] = jnp.zeros_like(o_v)
        _segment_copies(s, loc_ref, len_ref, off_ref, copies(slot)[0])

    @pl.when(s + 1 < pl.num_programs(0))
    def _():
        _segment_copies(s + 1, loc_ref, len_ref, off_ref, copies(1 - slot)[0])

    pt = _perm_t(gate_ref[...], loc_ref, s).astype(BF16)
    _segment_copies(s, loc_ref, len_ref, off_ref, copies(slot)[1])
    oh, ol = _split(o_v[slot])
    f = _dot(pt, oh) + _dot(pt, ol)
    g2 = mod_ref[0, :, 5 * D_MODEL:6 * D_MODEL]
    x2 = _norm(ALPHA * x1_ref[...] + g2 * f) * ln2g_ref[...] + ln2b_ref[...]

    @pl.when(s < n_ctx_tiles)
    def _():
        ctx_ref[...] = x2

    @pl.when(s >= n_ctx_tiles)
    def _():
        lat_ref[...] = x2


def _moe_plan(gates_all, n_blocks):
    n_tiles = gates_all.shape[0] // MOE_ST
    oh = gates_all[:, N_EXPERTS:N_EXPERTS + N_EGROUPS]
    cnt = jnp.sum(oh.reshape(n_tiles, MOE_ST, N_EGROUPS), axis=1).astype(jnp.int32)
    len16 = ((cnt + MOE_UNIT - 1) // MOE_UNIT) * MOE_UNIT
    loc = jnp.cumsum(len16, axis=1) - len16
    rows_g = jnp.sum(len16, axis=0)
    reg_g = ((rows_g + MOE_TM - 1) // MOE_TM) * MOE_TM
    reg_start = jnp.cumsum(reg_g) - reg_g
    off = reg_start[None, :] + jnp.cumsum(len16, axis=0) - len16
    blk_end = jnp.cumsum(reg_g // MOE_TM)
    bi = jnp.arange(n_blocks, dtype=jnp.int32)
    blk_group = jnp.minimum(jnp.sum((bi[:, None] >= blk_end[None, :]).astype(jnp.int32), axis=1),
                            N_EGROUPS - 1)
    flat = lambda a: a.reshape(-1).astype(jnp.int32)
    return flat(loc), flat(len16), flat(off), blk_group.astype(jnp.int32), blk_end[-1:].astype(jnp.int32)


def _moe(h2_all, gates_all, x1_all, mod, wg_bf, wu_bf, wd_bf, ln2g, ln2b, n_ctx, tokens_per_mod_row):
    n_tok = h2_all.shape[0]
    n_tiles = n_tok // MOE_ST
    n_ctx_tiles = n_ctx // MOE_ST
    max_rows = n_tok + n_tiles * N_EGROUPS * (MOE_UNIT - 1) + N_EGROUPS * (MOE_TM - 1)
    n_blocks = -(-max_rows // MOE_TM)
    n_rows = n_blocks * MOE_TM
    loc, len16, off, blk_group, n_used = _moe_plan(gates_all, n_blocks)

    tile = lambda w: pl.BlockSpec((MOE_ST, w), lambda s, *_: (s, 0))
    anyspec = pl.BlockSpec(memory_space=pl.ANY)
    xs, gs = pl.pallas_call(
        _moe_sort_kernel,
        grid_spec=pltpu.PrefetchScalarGridSpec(
            num_scalar_prefetch=3, grid=(n_tiles,),
            in_specs=[tile(D_MODEL), tile(LANES), anyspec, anyspec],
            out_specs=[anyspec, anyspec],
            scratch_shapes=[pltpu.VMEM((2, MOE_SLOTS, D_MODEL), BF16), pltpu.VMEM((2, MOE_SLOTS, LANES), F32),
                            pltpu.SemaphoreType.DMA((2, 2))]),
        out_shape=[jax.ShapeDtypeStruct((n_rows, D_MODEL), BF16),
                   jax.ShapeDtypeStruct((n_rows, LANES), F32)],
        input_output_aliases={5: 0, 6: 1},
        compiler_params=_cparams(("arbitrary",)),
        name="moe_sort",
    )(loc, len16, off, h2_all, gates_all, jnp.zeros((n_rows, D_MODEL), BF16), jnp.zeros((n_rows, LANES), F32))

    blk = lambda w: pl.BlockSpec((MOE_TM, w), lambda i, bg, nb: (jnp.minimum(i, nb[0] - 1), 0))
    wspec = lambda a, b: pl.BlockSpec((N_EPG, a, b), lambda i, bg, nb: (bg[i], 0, 0))
    o_sorted = pl.pallas_call(
        _moe_expert_kernel,
        grid_spec=pltpu.PrefetchScalarGridSpec(
            num_scalar_prefetch=2, grid=(n_blocks,),
            in_specs=[blk(D_MODEL), blk(LANES), wspec(D_MODEL, D_EXPERT), wspec(D_MODEL, D_EXPERT),
                      wspec(D_EXPERT, D_MODEL)],
            out_specs=pl.BlockSpec((MOE_TM, D_MODEL), lambda i, bg, nb: (i, 0))),
        out_shape=jax.ShapeDtypeStruct((n_rows, D_MODEL), F32),
        compiler_params=_cparams(("arbitrary",)),
        name="moe_experts",
    )(blk_group, n_used, xs, gs, wg_bf, wu_bf, wd_bf)

    lat_per_row = tokens_per_mod_row // MOE_ST

    def mod_idx(s, *_):
        return (jnp.where(s < n_ctx_tiles, 0, 1 + (s - n_ctx_tiles) // lat_per_row), 0, 0)

    vec = pl.BlockSpec((1, D_MODEL), lambda s, *_: (0, 0))
    return pl.pallas_call(
        functools.partial(_moe_combine_kernel, n_ctx_tiles=n_ctx_tiles),
        grid_spec=pltpu.PrefetchScalarGridSpec(
            num_scalar_prefetch=3, grid=(n_tiles,),
            in_specs=[tile(LANES), tile(D_MODEL), pl.BlockSpec((1, 1, 6 * D_MODEL), mod_idx), vec, vec,
                      anyspec],
            out_specs=[pl.BlockSpec((MOE_ST, D_MODEL), lambda s, *_: (jnp.minimum(s, n_ctx_tiles - 1), 0)),
                       pl.BlockSpec((MOE_ST, D_MODEL), lambda s, *_: (jnp.maximum(s - n_ctx_tiles, 0), 0))],
            scratch_shapes=[pltpu.VMEM((2, MOE_SLOTS, D_MODEL), F32), pltpu.SemaphoreType.DMA((2,))]),
        out_shape=[jax.ShapeDtypeStruct((n_ctx, D_MODEL), F32),
                   jax.ShapeDtypeStruct((n_tok - n_ctx, D_MODEL), F32)],
        compiler_params=_cparams(("arbitrary",)),
        name="moe_combine",
    )(loc, len16, off, gates_all, x1_all, mod.reshape(mod.shape[0], 1, 6 * D_MODEL), ln2g, ln2b, o_sorted)


def _grid_pos_embed(n_tokens):
    rows = n_tokens // GRID_W
    row = jnp.repeat(jnp.arange(rows, dtype=F32), GRID_W)
    col = jnp.tile(jnp.arange(GRID_W, dtype=F32), rows)
    quarter = D_MODEL // 4
    omega = 1.0 / (POS_BASE ** (jnp.arange(quarter, dtype=F32) / quarter))
    er = row[:, None] * omega
    ec = col[:, None] * omega
    return jnp.concatenate([jnp.sin(er), jnp.cos(er), jnp.sin(ec), jnp.cos(ec)], axis=-1)


def _tables(n_tok):
    cm, sm = _dft_tables(n_tok)
    out = []
    for t in (cm, sm, np.ascontiguousarray(sm.T)):
        hi, lo = _split(jnp.asarray(t))
        out += [hi, lo]
    return tuple(out)


def _mixers(x, pos, mod3, h0_re, h0_im, tabs, filt, s5ops, wts, tm):
    bsz, n_tok, _ = x.shape
    shared = mod3.shape[0] == 1
    x3 = x.reshape(1, bsz * n_tok, D_MODEL) if shared else x
    proj_hy, u_s5 = _in_proj(x3, pos, mod3, wts['w_in'], tm)
    y_hy = _hyena(proj_hy.reshape(bsz, n_tok, 3 * D_HY), tabs, filt,
                  wts['hy_conv_w'], wts['hy_conv_b'], wts['hy_fbias'])
    y_s5, f_re, f_im = _s5(u_s5.reshape(bsz, n_tok, D_S5), s5ops, h0_re, h0_im)
    return y_hy.reshape(bsz * n_tok, D_HY), y_s5.reshape(bsz * n_tok, D_S5), f_re, f_im


def kernel(x_prompt, x_sample, state_s5_re, state_s5_im, c, c_ctx, w_ada, b_ada, w_in, hy_conv_w, hy_conv_b, hy_f_w1, hy_f_b1, hy_f_w2, hy_f_b2, hy_f_w3, hy_freq, hy_fbias, s5_a_re, s5_a_im, s5_log_dt, s5_b_re, s5_b_im, s5_c_re, s5_c_im, s5_d, s5_w_glu, s5_b_glu, out_norm_g, w_out, ln1_g, ln1_b, moe_w_r1, moe_b_r1, moe_w_r2, moe_b_r2, moe_w_gate, moe_w_up, moe_w_down, ln2_g, ln2_b):
    b_ctx, l_ctx, _ = x_prompt.shape
    b_lat, l_lat, _ = x_sample.shape
    g, p = S5_GROUPS, S5_STATE
    assert w_ada.shape[0] == 1, "single-layer trunk"
    l = 0

    nrow = 16
    cond = jnp.concatenate([c_ctx[None, :], c, jnp.zeros((nrow - 1 - b_lat, D_MODEL), F32)], axis=0)
    mod = _ada(cond, w_ada[l], b_ada[l])
    mod_ctx = mod[0:1].reshape(1, 1, 6 * D_MODEL)
    mod_lat = mod[1:1 + b_lat].reshape(b_lat, 1, 6 * D_MODEL)

    wr = jnp.concatenate([moe_w_r2[l].transpose(1, 0, 2).reshape(D_MODEL, N_EXPERTS), moe_w_r1[l]], axis=1)
    wr = jnp.pad(wr, ((0, 0), (0, LANES - wr.shape[1])))
    br = jnp.concatenate([moe_b_r2[l].reshape(-1), moe_b_r1[l]])
    br = jnp.pad(br, (0, LANES - br.shape[0])).reshape(1, LANES)
    wr_hi, wr_lo = _split(wr)

    wts = {
        'w_in': w_in[l].astype(BF16), 'hy_conv_w': hy_conv_w[l], 'hy_conv_b': hy_conv_b[l],
        'hy_fbias': hy_fbias[l], 'w_glu': s5_w_glu[l].astype(BF16), 'b_glu': s5_b_glu[l].reshape(1, -1),
        'out_norm_g': out_norm_g[l].reshape(1, -1), 'w_out': w_out[l].astype(BF16),
        'ln1_g': ln1_g[l].reshape(1, -1), 'ln1_b': ln1_b[l].reshape(1, -1),
        'wr_hi': wr_hi, 'wr_lo': wr_lo, 'br': br,
        'w_gate': moe_w_gate[l].astype(BF16), 'w_up': moe_w_up[l].astype(BF16),
        'w_down': moe_w_down[l].astype(BF16),
        'ln2_g': ln2_g[l].reshape(1, -1), 'ln2_b': ln2_b[l].reshape(1, -1),
    }

    s5ops = _s5_operators(s5_a_re[l], s5_a_im[l], s5_log_dt[l], s5_b_re[l], s5_b_im[l],
                          s5_c_re[l], s5_c_im[l], s5_d[l])
    tabs_ctx = _tables(l_ctx)
    tabs_lat = _tables(l_lat)
    filt_args = (hy_f_w1[l], hy_f_b1[l], hy_f_w2[l], hy_f_b2[l], hy_f_w3[l], hy_freq[l])
    filt_ctx = _hyena_filters(l_ctx, tabs_ctx, *filt_args)
    filt_lat = _hyena_filters(l_lat, tabs_lat, *filt_args)

    zero = jnp.zeros((g, b_ctx, 2 * p), F32)
    yhy_c, ys5_c, f_re, f_im = _mixers(x_prompt, None, mod_ctx, zero, zero, tabs_ctx, filt_ctx, s5ops, wts, 512)
    unpack = lambda f: f.reshape(g, b_ctx, 2, p).transpose(1, 2, 0, 3)[:, None]
    new_re, new_im = unpack(f_re), unpack(f_im)

    pack = lambda s: s[:, l].transpose(2, 0, 1, 3).reshape(g, b_lat, 2 * p)
    pos = _grid_pos_embed(l_lat)
    yhy_l, ys5_l, _, _ = _mixers(x_sample, pos, mod_lat, pack(state_s5_re), pack(state_s5_im),
                                 tabs_lat, filt_lat, s5ops, wts, 512)

    n_ctx = b_ctx * l_ctx
    x1_all, h2_all, gates_all = _out_proj(
        x_prompt.reshape(n_ctx, D_MODEL), x_sample.reshape(b_lat * l_lat, D_MODEL), pos,
        yhy_c, yhy_l, ys5_c, ys5_l, mod, wts['w_glu'], wts['b_glu'], wts['out_norm_g'], wts['w_out'],
        wts['ln1_g'], wts['ln1_b'], wts['wr_hi'], wts['wr_lo'], wts['br'], 512)
    y_ctx, y_lat = _moe(h2_all, gates_all, x1_all, mod,
                        wts['w_gate'], wts['w_up'], wts['w_down'], wts['ln2_g'], wts['ln2_b'],
                        n_ctx, l_lat)
    return (y_ctx.reshape(x_prompt.shape), y_lat.reshape(x_sample.shape), new_re, new_im)
```

```python
import functools
import math

import numpy as np
import jax
import jax.numpy as jnp
from jax import lax
from jax.experimental import pallas as pl
from jax.experimental.pallas import tpu as pltpu

F32 = jnp.float32
BF16 = jnp.bfloat16

D_MODEL = 1024
DEPTH = 1
GRID_W = 64
POS_BASE = 10000.0
D_HY = 512
D_S5 = 512
S5_CH = 16
S5_GROUPS = 32
S5_STATE = 64
S5_CHUNK = 16
S5_ROW = S5_CHUNK * S5_CH
HY_BANDS = 16
HY_EMB = 1 + 2 * HY_BANDS
HY_HID = 64
HY_MIN_DECAY = math.log(1e-2) / 1.5
HY_MAX_DECAY = math.log(1e-2) / 0.3
N_EGROUPS = 4
N_EPG = 4
N_EXPERTS = 16
D_EXPERT = 512
LN_EPS = 1e-5
ALPHA = (2.0 * DEPTH) ** 0.25
LANES = 128
S5_GB = LANES // S5_CH
HY_CW = 512
MOE_ST = 256
MOE_SLOTS = 384
MOE_UNIT = 16
MOE_TM = 512
VMEM_LIMIT = 60000 * 1024


def _cparams(sem):
    return pltpu.CompilerParams(dimension_semantics=sem, vmem_limit_bytes=VMEM_LIMIT)


def _split(x):
    hi = x.astype(BF16)
    lo = (x - hi.astype(F32)).astype(BF16)
    return hi, lo


def _dot(a, b):
    return jnp.dot(a, b, preferred_element_type=F32)


def _dot_t(a, b):
    return lax.dot_general(a, b, (((1,), (1,)), ((), ())), preferred_element_type=F32)


def _mm3(a, b):
    ah, al = _split(a)
    bh, bl = _split(b)
    return _dot(ah, bh) + _dot(al, bh) + _dot(ah, bl)


def _mm3_pre(ah, al, b):
    bh, bl = _split(b)
    return _dot(ah, bh) + _dot(al, bh) + _dot(ah, bl)


def _mm3_t(a, b):
    ah, al = _split(a)
    bh, bl = _split(b)
    return _dot_t(ah, bh) + _dot_t(al, bh) + _dot_t(ah, bl)


def _dot_hp(a, b):
    return jnp.dot(a, b, preferred_element_type=F32, precision=lax.Precision.HIGHEST)


def _norm(x):
    xc = x - jnp.mean(x, axis=-1, keepdims=True)
    return xc * lax.rsqrt(jnp.mean(xc * xc, axis=-1, keepdims=True) + LN_EPS)


def _rms(y):
    return y * lax.rsqrt(jnp.mean(y * y, axis=-1, keepdims=True) + LN_EPS)


def _ada_kernel(cond_ref, w_ref, b_ref, o_ref):
    c = jax.nn.silu(cond_ref[...])
    o_ref[...] = _mm3(c, w_ref[...]) + b_ref[...]


def _ada(cond, w_ada, b_ada):
    nb = cond.shape[0]
    n = w_ada.shape[1]
    tn = 1024
    return pl.pallas_call(
        _ada_kernel,
        grid=(n // tn,),
        in_specs=[pl.BlockSpec((nb, D_MODEL), lambda j: (0, 0)),
                  pl.BlockSpec((D_MODEL, tn), lambda j: (0, j)),
                  pl.BlockSpec((1, tn), lambda j: (0, j))],
        out_specs=pl.BlockSpec((nb, tn), lambda j: (0, j)),
        out_shape=jax.ShapeDtypeStruct((nb, n), F32),
        compiler_params=_cparams(("arbitrary",)),
        name="ada",
    )(cond, w_ada, b_ada.reshape(1, n))


def _dft_tables(n_tok):
    n = 2 * n_tok
    idx = np.arange(n_tok, dtype=np.int64)
    m = (idx[:, None] * idx[None, :]) % n
    ang = 2.0 * np.pi * m.astype(np.float64) / n
    cm = np.cos(ang)
    sm = -np.sin(ang)
    sm[0, :] = 1.0 - 2.0 * (idx % 2)
    return cm.astype(np.float32), sm.astype(np.float32)


def _filt_kernel(n_tok, z_ref, t_ref, w1_ref, b1_ref, w2_ref, b2_ref, fr_ref, w3f_ref, w3b_ref,
                 dl_ref, cmh_ref, cml_ref, smh_ref, sml_ref, a_ref, bz_ref, dd_ref):
    fr = fr_ref[...]
    h = jnp.sin(fr * (_dot_hp(z_ref[...], w1_ref[...]) + b1_ref[...]))
    h = jnp.sin(fr * (_dot_hp(h, w2_ref[...]) + b2_ref[...]))
    decay = jnp.exp(-t_ref[...] * dl_ref[...])
    row = lax.broadcasted_iota(jnp.int32, decay.shape, 0)
    hf = _dot_hp(h, w3f_ref[...]) * decay
    hb = jnp.where(row == 0, 0.0, _dot_hp(h, w3b_ref[...]) * decay)
    p = hf + hb
    q = hf - hb
    k_re = _mm3_pre(cmh_ref[...], cml_ref[...], p)
    k_im = _mm3_pre(smh_ref[...], sml_ref[...], q)
    sign = jnp.where(row % 2 == 0, 1.0, -1.0)
    nyq = jnp.sum(p * sign, axis=0, keepdims=True)
    inv_n = 1.0 / (2 * n_tok)
    w = jnp.where(row == 0, inv_n, 2.0 * inv_n)
    a_ref[...] = w * k_re
    bz_ref[...] = jnp.where(row == 0, 0.0, w * k_im)
    dd_ref[...] = jnp.where(row == 0, nyq * inv_n, w * k_re)


def _hyena_filters(n_tok, tabs, hy_f_w1, hy_f_b1, hy_f_w2, hy_f_b2, hy_f_w3, hy_freq):
    cmh, cml, smh, sml = tabs[:4]
    t = jnp.linspace(0.0, 1.0, n_tok, dtype=F32)[:, None]
    wv = 2.0 * math.pi * jnp.arange(n_tok, dtype=F32) / n_tok
    fb = jnp.linspace(1e-4, HY_BANDS - 1, HY_BANDS, dtype=F32)
    ang = wv[:, None] * fb[None, :]
    z = jnp.concatenate([t, jnp.cos(ang), -jnp.sin(ang)], axis=-1)
    z = jnp.pad(z, ((0, 0), (0, LANES - HY_EMB)))
    w1 = jnp.pad(hy_f_w1, ((0, LANES - HY_EMB), (0, 0)))
    deltas = jnp.abs(jnp.linspace(HY_MIN_DECAY, HY_MAX_DECAY, D_HY, dtype=F32))[None, :]
    ncb = D_HY // HY_CW
    full = lambda j: (0, 0)
    out_sd = jax.ShapeDtypeStruct((n_tok, 2 * D_HY), F32)
    mat = pl.BlockSpec((n_tok, n_tok), full, pipeline_mode=pl.Buffered(1))
    return pl.pallas_call(
        functools.partial(_filt_kernel, n_tok),
        grid=(2 * ncb,),
        in_specs=[pl.BlockSpec((n_tok, LANES), full),
                  pl.BlockSpec((n_tok, 1), full),
                  pl.BlockSpec((LANES, HY_HID), full),
                  pl.BlockSpec((1, HY_HID), full),
                  pl.BlockSpec((HY_HID, HY_HID), full),
                  pl.BlockSpec((1, HY_HID), full),
                  pl.BlockSpec((1, HY_HID), full),
                  pl.BlockSpec((HY_HID, HY_CW), lambda j: (0, 2 * ncb * (j // ncb) + j % ncb)),
                  pl.BlockSpec((HY_HID, HY_CW), lambda j: (0, 2 * ncb * (j // ncb) + ncb + j % ncb)),
                  pl.BlockSpec((1, HY_CW), lambda j: (0, j % ncb)),
                  mat, mat, mat, mat],
        out_specs=[pl.BlockSpec((n_tok, HY_CW), lambda j: (0, j))] * 3,
        out_shape=[out_sd] * 3,
        compiler_params=_cparams(("arbitrary",)),
        name=f"filt{n_tok}",
    )(z, t, w1, hy_f_b1.reshape(1, -1), hy_f_w2, hy_f_b2.reshape(1, -1), hy_freq.reshape(1, -1),
      hy_f_w3, hy_f_w3, deltas, cmh, cml, smh, sml)


def _hyena_kernel(pv_ref, p1_ref, p2_ref, cwv_ref, cw1_ref, cw2_ref, cbv_ref, cb1_ref, cb2_ref,
                  fbias_ref, cm_ref, sm_ref, st_ref,
                  a0_ref, b0_ref, d0_ref, a1_ref, b1_ref, d1_ref, o_ref):
    n_tok = pv_ref.shape[1]
    row = lax.broadcasted_iota(jnp.int32, (n_tok, pv_ref.shape[2]), 0)

    def short_conv(p_ref, cw_ref, cb_ref):
        p = p_ref[0]
        prev = jnp.where(row == 0, 0.0, pltpu.roll(p, 1, axis=0))
        nxt = jnp.where(row == n_tok - 1, 0.0, pltpu.roll(p, n_tok - 1, axis=0))
        return cb_ref[...] + prev * cw_ref[0:1, :] + p * cw_ref[1:2, :] + nxt * cw_ref[2:3, :]

    cm = cm_ref[...]

    def fftconv(u, a_ref, b_ref, d_ref, skip):
        ub = u.astype(BF16)
        u_re = _dot(cm, ub)
        u_im = _dot(sm_ref[...], ub)
        a, bz, dd = a_ref[...], b_ref[...], d_ref[...]
        y_re = u_re * a - u_im * bz
        y_im = u_re * bz + u_im * dd
        y = _dot(cm, y_re.astype(BF16)) + _dot(st_ref[...], y_im.astype(BF16))
        return y + u * skip

    v = short_conv(pv_ref, cwv_ref, cbv_ref)
    x1 = short_conv(p1_ref, cw1_ref, cb1_ref)
    z = x1 * fftconv(v, a0_ref, b0_ref, d0_ref, fbias_ref[0:1, :])
    x2 = short_conv(p2_ref, cw2_ref, cb2_ref)
    o_ref[0] = x2 * fftconv(z, a1_ref, b1_ref, d1_ref, fbias_ref[1:2, :])


def _hyena(proj_hy, tabs, filt, hy_conv_w, hy_conv_b, hy_fbias):
    bsz, n_tok, _ = proj_hy.shape
    ncb = D_HY // HY_CW
    cmh, _, smh, _, sth, _ = tabs
    fa, fbz, fdd = filt
    cb = hy_conv_b.reshape(1, -1)
    mat = pl.BlockSpec((n_tok, n_tok), lambda b, c: (0, 0), pipeline_mode=pl.Buffered(1))

    def pspec(k):
        return pl.BlockSpec((1, n_tok, HY_CW), lambda b, c: (b, 0, k * ncb + c))

    def cwspec(k):
        return pl.BlockSpec((3, HY_CW), lambda b, c: (0, k * ncb + c))

    def cbspec(k):
        return pl.BlockSpec((1, HY_CW), lambda b, c: (0, k * ncb + c))

    def fspec(o):
        mode = pl.Buffered(1) if ncb == 1 else None
        return pl.BlockSpec((n_tok, HY_CW), lambda b, c: (0, o * ncb + c), pipeline_mode=mode)

    return pl.pallas_call(
        _hyena_kernel,
        grid=(bsz, ncb),
        in_specs=[pspec(0), pspec(1), pspec(2), cwspec(0), cwspec(1), cwspec(2),
                  cbspec(0), cbspec(1), cbspec(2),
                  pl.BlockSpec((2, HY_CW), lambda b, c: (0, c)),
                  mat, mat, mat,
                  fspec(0), fspec(0), fspec(0), fspec(1), fspec(1), fspec(1)],
        out_specs=pl.BlockSpec((1, n_tok, HY_CW), lambda b, c: (b, 0, c)),
        out_shape=jax.ShapeDtypeStruct((bsz, n_tok, D_HY), F32),
        compiler_params=_cparams(("arbitrary", "arbitrary")),
        name=f"hyena{n_tok}",
    )(proj_hy, proj_hy, proj_hy, hy_conv_w, hy_conv_w, hy_conv_w, cb, cb, cb, hy_fbias,
      cmh, smh, sth, fa, fbz, fdd, fa, fbz, fdd)


def _s5ops_kernel(are_ref, aim_ref, ldt_ref, btr_ref, bti_ref, cre_ref, cim_ref, d_ref,
                  mt_ref, erh_ref, erl_ref, eih_ref, eil_ref, gr_ref, gi_ref, atr_ref, ati_ref,
                  er_ref, ei_ref):
    a_re, a_im = are_ref[0], aim_ref[0]
    dt = jnp.exp(ldt_ref[0])
    mag = jnp.exp(a_re * dt)
    ab_re = mag * jnp.cos(a_im * dt)
    ab_im = mag * jnp.sin(a_im * dt)
    n_re, n_im = ab_re - 1.0, ab_im
    den = a_re * a_re + a_im * a_im
    q_re = (n_re * a_re + n_im * a_im) / den
    q_im = (n_im * a_re - n_re * a_im) / den
    bt_re, bt_im = btr_ref[0], bti_ref[0]
    bb_re = q_re * bt_re - q_im * bt_im
    bb_im = q_re * bt_im + q_im * bt_re
    c_re, c_im = cre_ref[0, 0:S5_CH, :], cim_ref[0, 0:S5_CH, :]
    pw = [(jnp.ones_like(ab_re), jnp.zeros_like(ab_re))]
    for _ in range(S5_CHUNK):
        pr, pi = pw[-1]
        pw.append((pr * ab_re - pi * ab_im, pr * ab_im + pi * ab_re))
    lane = lax.broadcasted_iota(jnp.int32, ab_re.shape, 1)
    fwd = lane < S5_STATE
    for s in range(S5_CHUNK):
        e_re = jnp.where(fwd, pw[S5_CHUNK - 1 - s][0], pw[s][0])
        e_im = jnp.where(fwd, pw[S5_CHUNK - 1 - s][1], pw[s][1])
        er_ref[pl.ds(S5_CH * s, S5_CH), :] = e_re * bb_re - e_im * bb_im
        ei_ref[pl.ds(S5_CH * s, S5_CH), :] = e_re * bb_im + e_im * bb_re
        g_re = jnp.where(fwd, pw[s + 1][0], pw[S5_CHUNK - s][0])
        g_im = jnp.where(fwd, pw[s + 1][1], pw[S5_CHUNK - s][1])
        gr_ref[0, pl.ds(S5_CH * s, S5_CH), :] = (c_re * g_re - c_im * g_im).astype(BF16)
        gi_ref[0, pl.ds(S5_CH * s, S5_CH), :] = (-(c_re * g_im + c_im * g_re)).astype(BF16)
    atr_ref[0] = pw[S5_CHUNK][0]
    ati_ref[0] = pw[S5_CHUNK][1]
    er, ei = er_ref[...], ei_ref[...]
    erh_ref[0], erl_ref[0] = _split(er)
    eih_ref[0], eil_ref[0] = _split(ei)
    lane2 = lax.broadcasted_iota(jnp.int32, er.shape, 1)
    row2 = lax.broadcasted_iota(jnp.int32, er.shape, 0)
    f2 = lane2 < S5_STATE
    zero = jnp.zeros_like(er)

    def dot_hp_t(a, b):
        return lax.dot_general(a, b, (((1,), (1,)), ((), ())), preferred_element_type=F32,
                               precision=lax.Precision.HIGHEST)

    cp_re, cp_im = cre_ref[0], cim_ref[0]
    kf = dot_hp_t(jnp.where(f2, er, zero), cp_re) - dot_hp_t(jnp.where(f2, ei, zero), cp_im)
    kb = dot_hp_t(jnp.where(f2, zero, er), cp_re) - dot_hp_t(jnp.where(f2, zero, ei), cp_im)
    d_row = d_ref[0]
    steps_per_vreg = LANES // S5_CH
    for half in range(S5_CHUNK // steps_per_vreg):
        acc = zero
        for tt in range(steps_per_vreg):
            t = half * steps_per_vreg + tt
            nf = S5_CH * (S5_CHUNK - 1 - t)
            nb = S5_CH * t
            col_f = jnp.concatenate([kf[nf:], zero[:nf]], axis=0) if nf else kf
            col_b = jnp.concatenate([zero[:nb], kb[:S5_ROW - nb]], axis=0) if nb else kb
            diag = jnp.where((row2 // S5_CH == t) & (row2 % S5_CH == lane2), d_row, 0.0)
            col = col_f + col_b + diag
            r = pltpu.roll(col, S5_CH * tt, axis=1) if tt else col
            acc = jnp.where((lane2 >= S5_CH * tt) & (lane2 < S5_CH * (tt + 1)), r, acc)
        mt_ref[0, :, LANES * half:LANES * (half + 1)] = acc.astype(BF16)


def _s5_operators(s5_a_re, s5_a_im, s5_log_dt, s5_b_re, s5_b_im, s5_c_re, s5_c_im, s5_d):
    g, p, h = S5_GROUPS, S5_STATE, S5_CH
    cat = lambda x: jnp.concatenate([x[0], x[1]], axis=-1)
    a_re = cat(s5_a_re).reshape(g, 1, 2 * p)
    a_im = cat(s5_a_im).reshape(g, 1, 2 * p)
    ldt = cat(jnp.broadcast_to(s5_log_dt[:, :, None], (2, g, p))).reshape(g, 1, 2 * p)
    bt_re = cat(jnp.swapaxes(s5_b_re, -1, -2))
    bt_im = cat(jnp.swapaxes(s5_b_im, -1, -2))
    cpad = lambda c: jnp.pad(jnp.concatenate([c, c], axis=-1), ((0, 0), (0, LANES - h), (0, 0)))
    c_re, c_im = cpad(s5_c_re), cpad(s5_c_im)
    d_row = jnp.pad(s5_d.reshape(g, 1, h), ((0, 0), (0, 0), (0, LANES - h)))
    vec = pl.BlockSpec((1, 1, 2 * p), lambda i: (i, 0, 0))
    hp = pl.BlockSpec((1, h, 2 * p), lambda i: (i, 0, 0))
    sq = pl.BlockSpec((1, LANES, 2 * p), lambda i: (i, 0, 0))
    big = pl.BlockSpec((1, S5_ROW, 2 * p), lambda i: (i, 0, 0))
    mts = pl.BlockSpec((1, S5_ROW, S5_ROW), lambda i: (i, 0, 0))
    big_sd = jax.ShapeDtypeStruct((g, S5_ROW, 2 * p), BF16)
    vec_sd = jax.ShapeDtypeStruct((g, 1, 2 * p), F32)
    return pl.pallas_call(
        _s5ops_kernel,
        grid=(g,),
        in_specs=[vec, vec, vec, hp, hp, sq, sq, vec],
        out_specs=[mts, big, big, big, big, big, big, vec, vec],
        out_shape=[jax.ShapeDtypeStruct((g, S5_ROW, S5_ROW), BF16)] + [big_sd] * 6 + [vec_sd, vec_sd],
        scratch_shapes=[pltpu.VMEM((S5_ROW, 2 * p), F32)] * 2,
        compiler_params=_cparams(("arbitrary",)),
        name="s5ops",
    )(a_re, a_im, ldt, bt_re, bt_im, c_re, c_im, d_row)


def _block_transpose(xs):
    n = len(xs)
    lane = lax.broadcasted_iota(jnp.int32, xs[0].shape, 1)
    xs = list(xs)
    d = n // 2
    while d:
        keep = ((lane // S5_CH) & d) == 0
        for i in range(n):
            if i & d:
                continue
            lo, hi = xs[i], xs[i + d]
            xs[i] = jnp.where(keep, lo, pltpu.roll(hi, S5_CH * d, axis=1))
            xs[i + d] = jnp.where(keep, pltpu.roll(lo, LANES - S5_CH * d, axis=1), hi)
        d //= 2
    return xs


def _s5_kernel(bsz, n_chunks, u_ref, mt_ref, erh_ref, erl_ref, eih_ref, eil_ref, gr_ref, gi_ref,
               atr_ref, ati_ref, h0r_ref, h0i_ref, y_ref, fr_ref, fi_ref,
               ua_ref, ub_ref, ya_ref, yb_ref, sr_ref, si_ref, xfr_ref, xfi_ref, xbr_ref, xbi_ref):
    nc = n_chunks
    spv = LANES // S5_CH
    rsub = min(nc, 32)

    def to_chunks(b, carry):
        for half, dst in ((0, ua_ref), (1, ub_ref)):
            for r0 in range(0, nc, rsub):
                xs = [u_ref[b, pl.ds(S5_CHUNK * r0 + half * spv + tt, rsub, stride=S5_CHUNK), :]
                      for tt in range(spv)]
                for k, blk in enumerate(_block_transpose(xs)):
                    dst[k, pl.ds(r0 * bsz + b, rsub, stride=bsz), :] = blk
        return carry

    lax.fori_loop(0, bsz, to_chunks, 0)

    lane = lax.broadcasted_iota(jnp.int32, (bsz, 2 * S5_STATE), 1)
    fwd = lane < S5_STATE
    lane_all = lax.broadcasted_iota(jnp.int32, (bsz * nc, 2 * S5_STATE), 1)
    fwd_all = lane_all < S5_STATE

    def group(k, carry):
        u = jnp.concatenate([ua_ref[k], ub_ref[k]], axis=1)
        uh, ul = _split(u)
        sr_ref[...] = _dot(uh, erh_ref[k]) + _dot(ul, erh_ref[k]) + _dot(uh, erl_ref[k])
        si_ref[...] = _dot(uh, eih_ref[k]) + _dot(ul, eih_ref[k]) + _dot(uh, eil_ref[k])
        at_re, at_im = atr_ref[k], ati_ref[k]

        def step(i, xc):
            x_re, x_im = xc
            rf = pl.ds(pl.multiple_of(i * bsz, bsz), bsz)
            rb = pl.ds(pl.multiple_of((nc - 1 - i) * bsz, bsz), bsz)
            xfr_ref[rf, :] = x_re
            xfi_ref[rf, :] = x_im
            xbr_ref[rb, :] = x_re
            xbi_ref[rb, :] = x_im
            s_re = jnp.where(fwd, sr_ref[rf, :], sr_ref[rb, :])
            s_im = jnp.where(fwd, si_ref[rf, :], si_ref[rb, :])
            return (at_re * x_re - at_im * x_im + s_re, at_re * x_im + at_im * x_re + s_im)

        x_re, x_im = lax.fori_loop(0, nc, step, (h0r_ref[k], h0i_ref[k]))
        fr_ref[k] = x_re
        fi_ref[k] = x_im
        xp_re = jnp.where(fwd_all, xfr_ref[...], xbr_ref[...]).astype(BF16)
        xp_im = jnp.where(fwd_all, xfi_ref[...], xbi_ref[...]).astype(BF16)
        y = _dot(uh, mt_ref[k]) + _dot_t(xp_re, gr_ref[k]) + _dot_t(xp_im, gi_ref[k])
        ya_ref[k] = y[:, :LANES]
        yb_ref[k] = y[:, LANES:]
        return carry

    lax.fori_loop(0, S5_GB, group, 0)

    def to_tokens(b, carry):
        for half, src in ((0, ya_ref), (1, yb_ref)):
            for r0 in range(0, nc, rsub):
                ys = [src[k, pl.ds(r0 * bsz + b, rsub, stride=bsz), :] for k in range(S5_GB)]
                for tt, blk in enumerate(_block_transpose(ys)):
                    y_ref[b, pl.ds(S5_CHUNK * r0 + half * spv + tt, rsub, stride=S5_CHUNK), :] = blk
        return carry

    lax.fori_loop(0, bsz, to_tokens, 0)


def _s5(u, ops, h0_re, h0_im):
    bsz, n_tok, _ = u.shape
    g, p = S5_GROUPS, S5_STATE
    nc = n_tok // S5_CHUNK
    rows = nc * bsz
    tok = pl.BlockSpec((bsz, n_tok, LANES), lambda j: (0, 0, j))
    gspec = lambda shape: pl.BlockSpec((S5_GB,) + shape, lambda j: (j, 0, 0))
    op = gspec((S5_ROW, 2 * p))
    return pl.pallas_call(
        functools.partial(_s5_kernel, bsz, nc),
        grid=(g // S5_GB,),
        in_specs=[tok, gspec((S5_ROW, S5_ROW)), op, op, op, op, op, op,
                  gspec((1, 2 * p)), gspec((1, 2 * p)), gspec((bsz, 2 * p)), gspec((bsz, 2 * p))],
        out_specs=[tok, gspec((bsz, 2 * p)), gspec((bsz, 2 * p))],
        out_shape=[jax.ShapeDtypeStruct((bsz, n_tok, D_S5), F32),
                   jax.ShapeDtypeStruct((g, bsz, 2 * p), F32),
                   jax.ShapeDtypeStruct((g, bsz, 2 * p), F32)],
        scratch_shapes=([pltpu.VMEM((S5_GB, rows, LANES), F32)] * 4
                        + [pltpu.VMEM((rows, 2 * p), F32)] * 6),
        compiler_params=_cparams(("arbitrary",)),
        name=f"s5_{n_tok}",
    )(u, *ops, h0_re, h0_im)


def _in_kernel(has_pos, *refs):
    if has_pos:
        x_ref, pos_ref, mod_ref, w_ref, hy_ref, s5_ref = refs
        x = x_ref[0] + pos_ref[...]
    else:
        x_ref, mod_ref, w_ref, hy_ref, s5_ref = refs
        x = x_ref[0]
    sh1 = mod_ref[0, :, 0:D_MODEL]
    sc1 = mod_ref[0, :, D_MODEL:2 * D_MODEL]
    h = _norm(x) * (1.0 + sc1) + sh1
    proj = _dot(h.astype(BF16), w_ref[...])
    hy_ref[0] = proj[:, :3 * D_HY]
    s5_ref[0] = proj[:, 3 * D_HY:]


def _in_proj(x3, pos, mod3, w_in_bf, tm):
    nb, lt, _ = x3.shape
    has_pos = pos is not None
    per_batch = mod3.shape[0] > 1
    midx = (lambda b, i: (b, 0, 0)) if per_batch else (lambda b, i: (0, 0, 0))
    in_specs = [pl.BlockSpec((1, tm, D_MODEL), lambda b, i: (b, i, 0))]
    args = [x3]
    if has_pos:
        in_specs.append(pl.BlockSpec((tm, D_MODEL), lambda b, i: (i, 0)))
        args.append(pos)
    in_specs += [pl.BlockSpec((1, 1, 6 * D_MODEL), midx),
                 pl.BlockSpec((D_MODEL, 3 * D_HY + D_S5), lambda b, i: (0, 0))]
    args += [mod3, w_in_bf]
    return pl.pallas_call(
        functools.partial(_in_kernel, has_pos),
        grid=(nb, lt // tm),
        in_specs=in_specs,
        out_specs=[pl.BlockSpec((1, tm, 3 * D_HY), lambda b, i: (b, i, 0)),
                   pl.BlockSpec((1, tm, D_S5), lambda b, i: (b, i, 0))],
        out_shape=[jax.ShapeDtypeStruct((nb, lt, 3 * D_HY), F32),
                   jax.ShapeDtypeStruct((nb, lt, D_S5), F32)],
        compiler_params=_cparams(("arbitrary", "arbitrary")),
        name=f"in_proj{nb}",
    )(*args)


def _route(logits):
    lane = lax.broadcasted_iota(jnp.int32, logits.shape, 1)
    lane_f = lane.astype(F32)
    neg = -jnp.inf
    big = float(LANES)
    m1 = (lane >= N_EXPERTS) & (lane < N_EXPERTS + N_EGROUPS)
    l1 = jnp.where(m1, logits, neg)
    top1 = jnp.max(l1, axis=-1, keepdims=True)
    grp = jnp.min(jnp.where(l1 == top1, lane_f, big), axis=-1, keepdims=True) - float(N_EXPERTS)
    den = jnp.sum(jnp.where(m1, jnp.exp(logits - top1), 0.0), axis=-1, keepdims=True)
    p_grp = 1.0 / den
    lo = grp * float(N_EPG)
    m2 = (lane_f >= lo) & (lane_f < lo + float(N_EPG))
    l2 = jnp.where(m2, logits, neg)
    v1 = jnp.max(l2, axis=-1, keepdims=True)
    i1 = jnp.min(jnp.where(l2 == v1, lane_f, big), axis=-1, keepdims=True)
    l2b = jnp.where(lane_f == i1, neg, l2)
    v2 = jnp.max(l2b, axis=-1, keepdims=True)
    i2 = jnp.min(jnp.where(l2b == v2, lane_f, big), axis=-1, keepdims=True)
    e = jnp.exp(v2 - v1)
    w1 = 1.0 / (1.0 + e)
    w2 = e / (1.0 + e)
    gates = jnp.where(lane_f == i1, w1 * p_grp, 0.0) + jnp.where(lane_f == i2, w2 * p_grp, 0.0)
    return jnp.where(lane_f == grp + float(N_EXPERTS), 1.0, gates)


def _out_kernel(n_ctx_blocks, xc_ref, xl_ref, pos_ref, yhyc_ref, yhyl_ref, ys5c_ref, ys5l_ref, mod_ref,
                wglu_ref, bglu_ref, ong_ref, wout_ref, ln1g_ref, ln1b_ref, wrh_ref, wrl_ref, br_ref,
                x1_ref, h2_ref, gate_ref):
    is_ctx = pl.program_id(0) < n_ctx_blocks
    x = jnp.where(is_ctx, xc_ref[...], xl_ref[...] + pos_ref[...])
    y = jnp.where(is_ctx, ys5c_ref[...], ys5l_ref[...])
    y_hy = jnp.where(is_ctx, yhyc_ref[...], yhyl_ref[...])
    s5 = jax.nn.gelu(y) * jax.nn.sigmoid(_dot(y.astype(BF16), wglu_ref[...]) + bglu_ref[...])
    m_hy = _rms(y_hy) * ong_ref[:, 0:D_HY]
    m_s5 = _rms(s5) * ong_ref[:, D_HY:]
    o = (_dot(m_hy.astype(BF16), wout_ref[0:D_HY, :]) + _dot(m_s5.astype(BF16), wout_ref[D_HY:, :]))
    g1 = mod_ref[0, :, 2 * D_MODEL:3 * D_MODEL]
    sh2 = mod_ref[0, :, 3 * D_MODEL:4 * D_MODEL]
    sc2 = mod_ref[0, :, 4 * D_MODEL:5 * D_MODEL]
    x1 = _norm(ALPHA * x + g1 * o) * ln1g_ref[...] + ln1b_ref[...]
    x1_ref[...] = x1
    h2 = _norm(x1) * (1.0 + sc2) + sh2
    h2_ref[...] = h2.astype(BF16)
    hh, hl = _split(h2)
    logits = (_dot(hh, wrh_ref[...]) + _dot(hl, wrh_ref[...]) + _dot(hh, wrl_ref[...]) + br_ref[...])
    gate_ref[...] = _route(logits)


def _out_proj(xc, xl, pos, yhy_c, yhy_l, ys5_c, ys5_l, mod, wglu_bf, bglu, ong, wout_bf, ln1g, ln1b,
              wr_hi, wr_lo, br, tm):
    n_ctx, n_lat = xc.shape[0], xl.shape[0]
    l_lat = pos.shape[0]
    ncb, nlb, npb = n_ctx // tm, n_lat // tm, l_lat // tm
    ctx = lambda w: pl.BlockSpec((tm, w), lambda i: (jnp.minimum(i, ncb - 1), 0))
    lat = lambda w: pl.BlockSpec((tm, w), lambda i: (jnp.maximum(i - ncb, 0), 0))
    full = lambda shape: pl.BlockSpec(shape, lambda i: (0,) * len(shape))
    out = lambda w: pl.BlockSpec((tm, w), lambda i: (i, 0))
    mod_idx = lambda i: (jnp.where(i < ncb, 0, 1 + jnp.maximum(i - ncb, 0) // npb), 0, 0)
    n_all = n_ctx + n_lat
    return pl.pallas_call(
        functools.partial(_out_kernel, ncb),
        grid=(ncb + nlb,),
        in_specs=[ctx(D_MODEL), lat(D_MODEL),
                  pl.BlockSpec((tm, D_MODEL), lambda i: (jnp.maximum(i - ncb, 0) % npb, 0)),
                  ctx(D_HY), lat(D_HY), ctx(D_S5), lat(D_S5),
                  pl.BlockSpec((1, 1, 6 * D_MODEL), mod_idx),
                  full((D_S5, D_S5)), full((1, D_S5)), full((1, D_MODEL)), full((D_MODEL, D_MODEL)),
                  full((1, D_MODEL)), full((1, D_MODEL)), full((D_MODEL, LANES)), full((D_MODEL, LANES)),
                  full((1, LANES))],
        out_specs=[out(D_MODEL), out(D_MODEL), out(LANES)],
        out_shape=[jax.ShapeDtypeStruct((n_all, D_MODEL), F32),
                   jax.ShapeDtypeStruct((n_all, D_MODEL), BF16),
                   jax.ShapeDtypeStruct((n_all, LANES), F32)],
        compiler_params=_cparams(("arbitrary",)),
        name="out_proj",
    )(xc, xl, pos, yhy_c, yhy_l, ys5_c, ys5_l, mod.reshape(mod.shape[0], 1, 6 * D_MODEL),
      wglu_bf, bglu, ong, wout_bf, ln1g, ln1b, wr_hi, wr_lo, br)


def _perm_t(gates, loc_ref, s):
    n = gates.shape[0]
    lane = lax.broadcasted_iota(jnp.int32, gates.shape, 1)
    oh = jnp.where((lane >= N_EXPERTS) & (lane < N_EXPERTS + N_EGROUPS), gates, 0.0)
    r = lax.broadcasted_iota(jnp.int32, (n, n), 0)
    c = lax.broadcasted_iota(jnp.int32, (n, n), 1)
    earlier = jnp.where(c < r, 1.0, 0.0).astype(BF16)
    cum = _dot(earlier, oh.astype(BF16))
    rank = jnp.sum(cum * oh, axis=-1, keepdims=True)
    lane1 = lax.broadcasted_iota(jnp.int32, (1, LANES), 1)
    locv = jnp.zeros((1, LANES), F32)
    for grp in range(N_EGROUPS):
        locv = jnp.where(lane1 == N_EXPERTS + grp, loc_ref[N_EGROUPS * s + grp].astype(F32), locv)
    dest = rank + jnp.sum(oh * locv, axis=-1, keepdims=True)
    slot = lax.broadcasted_iota(jnp.int32, (n, MOE_SLOTS), 1).astype(F32)
    return jnp.where(slot == dest, 1.0, 0.0)


def _segment_copies(s, loc_ref, len_ref, off_ref, make):
    for grp in range(N_EGROUPS):
        loc = loc_ref[N_EGROUPS * s + grp]
        off = off_ref[N_EGROUPS * s + grp]
        n_units = len_ref[N_EGROUPS * s + grp] // MOE_UNIT

        def body(i, carry):
            make(pl.multiple_of(loc + MOE_UNIT * i, MOE_UNIT), pl.multiple_of(off + MOE_UNIT * i, MOE_UNIT))
            return carry

        lax.fori_loop(0, n_units, body, 0)


def _moe_sort_kernel(loc_ref, len_ref, off_ref, h_ref, gate_ref, xs_in, gs_in, xs_hbm, gs_hbm,
                     xs_v, gs_v, sem):
    del xs_in, gs_in
    s = pl.program_id(0)
    slot = s % 2
    gates = gate_ref[...]
    p = _perm_t(gates, loc_ref, s).T.astype(BF16)
    xs_v[slot] = _dot(p, h_ref[...]).astype(BF16)
    g_hi = gates.astype(BF16)
    r1 = gates - g_hi.astype(F32)
    g_mid = r1.astype(BF16)
    g_lo = (r1 - g_mid.astype(F32)).astype(BF16)
    gs_v[slot] = _dot(p, g_hi) + _dot(p, g_mid) + _dot(p, g_lo)

    def copies(buf):
        def x_copy(lr, gr):
            return pltpu.make_async_copy(xs_v.at[buf, pl.ds(lr, MOE_UNIT), :],
                                         xs_hbm.at[pl.ds(gr, MOE_UNIT), :], sem.at[0, buf])

        def g_copy(lr, gr):
            return pltpu.make_async_copy(gs_v.at[buf, pl.ds(lr, MOE_UNIT), :],
                                         gs_hbm.at[pl.ds(gr, MOE_UNIT), :], sem.at[1, buf])

        def start(lr, gr):
            x_copy(lr, gr).start()
            g_copy(lr, gr).start()

        def wait(lr, gr):
            x_copy(lr, gr).wait()
            g_copy(lr, gr).wait()

        return start, wait

    _segment_copies(s, loc_ref, len_ref, off_ref, copies(slot)[0])

    @pl.when(s > 0)
    def _():
        _segment_copies(s - 1, loc_ref, len_ref, off_ref, copies(1 - slot)[1])

    @pl.when(s == pl.num_programs(0) - 1)
    def _():
        _segment_copies(s, loc_ref, len_ref, off_ref, copies(slot)[1])


def _moe_expert_kernel(bg_ref, nb_ref, xs_ref, gs_ref, wg_ref, wu_ref, wd_ref, o_ref):
    i = pl.program_id(0)

    @pl.when(i < nb_ref[0])
    def _():
        grp = bg_ref[i]
        x = xs_ref[...]
        gates = gs_ref[...]
        lane = lax.broadcasted_iota(jnp.int32, gates.shape, 1)
        acc = jnp.zeros(o_ref.shape, F32)
        for e in range(N_EPG):
            a = _dot(x, wg_ref[e])
            u = _dot(x, wu_ref[e])
            ge = jnp.sum(jnp.where(lane == N_EPG * grp + e, gates, 0.0), axis=-1, keepdims=True)
            hid = jax.nn.silu(a) * u * ge
            acc = acc + _dot(hid.astype(BF16), wd_ref[e])
        o_ref[...] = acc.astype(BF16)

    @pl.when(i >= nb_ref[0])
    def _():
        o_ref[...] = jnp.zeros_like(o_ref)


def _moe_combine_kernel(loc_ref, len_ref, off_ref, gate_ref, x1_ref, mod_ref, ln2g_ref, ln2b_ref, o_hbm,
                        ctx_ref, lat_ref, o_v, sem, *, n_ctx_tiles):
    s = pl.program_id(0)
    slot = s % 2

    def copies(buf):
        def o_copy(lr, gr):
            return pltpu.make_async_copy(o_hbm.at[pl.ds(gr, MOE_UNIT), :],
                                         o_v.at[buf, pl.ds(lr, MOE_UNIT), :], sem.at[buf])

        return (lambda lr, gr: o_copy(lr, gr).start()), (lambda lr, gr: o_copy(lr, gr).wait())

    @pl.when(s == 0)
    def _():
        o_v[...] = jnp.zeros_like(o_v)
        _segment_copies(s, loc_ref, len_ref, off_ref, copies(slot)[0])

    @pl.when(s + 1 < pl.num_programs(0))
    def _():
        _segment_copies(s + 1, loc_ref, len_ref, off_ref, copies(1 - slot)[0])

    pt = _perm_t(gate_ref[...], loc_ref, s).astype(BF16)
    _segment_copies(s, loc_ref, len_ref, off_ref, copies(slot)[1])
    f = _dot(pt, o_v[slot])
    g2 = mod_ref[0, :, 5 * D_MODEL:6 * D_MODEL]
    x2 = _norm(ALPHA * x1_ref[...] + g2 * f) * ln2g_ref[...] + ln2b_ref[...]

    @pl.when(s < n_ctx_tiles)
    def _():
        ctx_ref[...] = x2

    @pl.when(s >= n_ctx_tiles)
    def _():
        lat_ref[...] = x2


def _moe_plan(gates_all, n_blocks):
    n_tiles = gates_all.shape[0] // MOE_ST
    oh = gates_all[:, N_EXPERTS:N_EXPERTS + N_EGROUPS]
    cnt = jnp.sum(oh.reshape(n_tiles, MOE_ST, N_EGROUPS), axis=1).astype(jnp.int32)
    len16 = ((cnt + MOE_UNIT - 1) // MOE_UNIT) * MOE_UNIT
    loc = jnp.cumsum(len16, axis=1) - len16
    rows_g = jnp.sum(len16, axis=0)
    reg_g = ((rows_g + MOE_TM - 1) // MOE_TM) * MOE_TM
    reg_start = jnp.cumsum(reg_g) - reg_g
    off = reg_start[None, :] + jnp.cumsum(len16, axis=0) - len16
    blk_end = jnp.cumsum(reg_g // MOE_TM)
    bi = jnp.arange(n_blocks, dtype=jnp.int32)
    blk_group = jnp.minimum(jnp.sum((bi[:, None] >= blk_end[None, :]).astype(jnp.int32), axis=1),
                            N_EGROUPS - 1)
    flat = lambda a: a.reshape(-1).astype(jnp.int32)
    return flat(loc), flat(len16), flat(off), blk_group.astype(jnp.int32), blk_end[-1:].astype(jnp.int32)


def _moe(h2_all, gates_all, x1_all, mod, wg_bf, wu_bf, wd_bf, ln2g, ln2b, n_ctx, tokens_per_mod_row):
    n_tok = h2_all.shape[0]
    n_tiles = n_tok // MOE_ST
    n_ctx_tiles = n_ctx // MOE_ST
    max_rows = n_tok + n_tiles * N_EGROUPS * (MOE_UNIT - 1) + N_EGROUPS * (MOE_TM - 1)
    n_blocks = -(-max_rows // MOE_TM)
    n_rows = n_blocks * MOE_TM
    loc, len16, off, blk_group, n_used = _moe_plan(gates_all, n_blocks)

    tile = lambda w: pl.BlockSpec((MOE_ST, w), lambda s, *_: (s, 0))
    anyspec = pl.BlockSpec(memory_space=pl.ANY)
    xs, gs = pl.pallas_call(
        _moe_sort_kernel,
        grid_spec=pltpu.PrefetchScalarGridSpec(
            num_scalar_prefetch=3, grid=(n_tiles,),
            in_specs=[tile(D_MODEL), tile(LANES), anyspec, anyspec],
            out_specs=[anyspec, anyspec],
            scratch_shapes=[pltpu.VMEM((2, MOE_SLOTS, D_MODEL), BF16), pltpu.VMEM((2, MOE_SLOTS, LANES), F32),
                            pltpu.SemaphoreType.DMA((2, 2))]),
        out_shape=[jax.ShapeDtypeStruct((n_rows, D_MODEL), BF16),
                   jax.ShapeDtypeStruct((n_rows, LANES), F32)],
        input_output_aliases={5: 0, 6: 1},
        compiler_params=_cparams(("arbitrary",)),
        name="moe_sort",
    )(loc, len16, off, h2_all, gates_all, jnp.zeros((n_rows, D_MODEL), BF16), jnp.zeros((n_rows, LANES), F32))

    blk = lambda w: pl.BlockSpec((MOE_TM, w), lambda i, bg, nb: (jnp.minimum(i, nb[0] - 1), 0))
    wspec = lambda a, b: pl.BlockSpec((N_EPG, a, b), lambda i, bg, nb: (bg[i], 0, 0))
    o_sorted = pl.pallas_call(
        _moe_expert_kernel,
        grid_spec=pltpu.PrefetchScalarGridSpec(
            num_scalar_prefetch=2, grid=(n_blocks,),
            in_specs=[blk(D_MODEL), blk(LANES), wspec(D_MODEL, D_EXPERT), wspec(D_MODEL, D_EXPERT),
                      wspec(D_EXPERT, D_MODEL)],
            out_specs=pl.BlockSpec((MOE_TM, D_MODEL), lambda i, bg, nb: (i, 0))),
        out_shape=jax.ShapeDtypeStruct((n_rows, D_MODEL), BF16),
        compiler_params=_cparams(("arbitrary",)),
        name="moe_experts",
    )(blk_group, n_used, xs, gs, wg_bf, wu_bf, wd_bf)

    lat_per_row = tokens_per_mod_row // MOE_ST

    def mod_idx(s, *_):
        return (jnp.where(s < n_ctx_tiles, 0, 1 + (s - n_ctx_tiles) // lat_per_row), 0, 0)

    vec = pl.BlockSpec((1, D_MODEL), lambda s, *_: (0, 0))
    return pl.pallas_call(
        functools.partial(_moe_combine_kernel, n_ctx_tiles=n_ctx_tiles),
        grid_spec=pltpu.PrefetchScalarGridSpec(
            num_scalar_prefetch=3, grid=(n_tiles,),
            in_specs=[tile(LANES), tile(D_MODEL), pl.BlockSpec((1, 1, 6 * D_MODEL), mod_idx), vec, vec,
                      anyspec],
            out_specs=[pl.BlockSpec((MOE_ST, D_MODEL), lambda s, *_: (jnp.minimum(s, n_ctx_tiles - 1), 0)),
                       pl.BlockSpec((MOE_ST, D_MODEL), lambda s, *_: (jnp.maximum(s - n_ctx_tiles, 0), 0))],
            scratch_shapes=[pltpu.VMEM((2, MOE_SLOTS, D_MODEL), BF16), pltpu.SemaphoreType.DMA((2,))]),
        out_shape=[jax.ShapeDtypeStruct((n_ctx, D_MODEL), F32),
                   jax.ShapeDtypeStruct((n_tok - n_ctx, D_MODEL), F32)],
        compiler_params=_cparams(("arbitrary",)),
        name="moe_combine",
    )(loc, len16, off, gates_all, x1_all, mod.reshape(mod.shape[0], 1, 6 * D_MODEL), ln2g, ln2b, o_sorted)


def _grid_pos_embed(n_tokens):
    rows = n_tokens // GRID_W
    row = np.repeat(np.arange(rows, dtype=np.float64), GRID_W)
    col = np.tile(np.arange(GRID_W, dtype=np.float64), rows)
    quarter = D_MODEL // 4
    omega = 1.0 / (POS_BASE ** (np.arange(quarter, dtype=np.float64) / quarter))
    er = row[:, None] * omega
    ec = col[:, None] * omega
    return jnp.asarray(np.concatenate([np.sin(er), np.cos(er), np.sin(ec), np.cos(ec)], axis=-1), F32)


def _tables(n_tok):
    cm, sm = _dft_tables(n_tok)
    out = []
    for t in (cm, sm, np.ascontiguousarray(sm.T)):
        hi, lo = _split(jnp.asarray(t))
        out += [hi, lo]
    return tuple(out)


def _mixers(x, pos, mod3, h0_re, h0_im, tabs, filt, s5ops, wts, tm):
    bsz, n_tok, _ = x.shape
    shared = mod3.shape[0] == 1
    x3 = x.reshape(1, bsz * n_tok, D_MODEL) if shared else x
    proj_hy, u_s5 = _in_proj(x3, pos, mod3, wts['w_in'], tm)
    y_hy = _hyena(proj_hy.reshape(bsz, n_tok, 3 * D_HY), tabs, filt,
                  wts['hy_conv_w'], wts['hy_conv_b'], wts['hy_fbias'])
    y_s5, f_re, f_im = _s5(u_s5.reshape(bsz, n_tok, D_S5), s5ops, h0_re, h0_im)
    return y_hy.reshape(bsz * n_tok, D_HY), y_s5.reshape(bsz * n_tok, D_S5), f_re, f_im


def kernel(x_prompt, x_sample, state_s5_re, state_s5_im, c, c_ctx, w_ada, b_ada, w_in, hy_conv_w, hy_conv_b, hy_f_w1, hy_f_b1, hy_f_w2, hy_f_b2, hy_f_w3, hy_freq, hy_fbias, s5_a_re, s5_a_im, s5_log_dt, s5_b_re, s5_b_im, s5_c_re, s5_c_im, s5_d, s5_w_glu, s5_b_glu, out_norm_g, w_out, ln1_g, ln1_b, moe_w_r1, moe_b_r1, moe_w_r2, moe_b_r2, moe_w_gate, moe_w_up, moe_w_down, ln2_g, ln2_b):
    b_ctx, l_ctx, _ = x_prompt.shape
    b_lat, l_lat, _ = x_sample.shape
    g, p = S5_GROUPS, S5_STATE
    assert w_ada.shape[0] == 1, "single-layer trunk"
    l = 0

    nrow = 16
    cond = jnp.concatenate([c_ctx[None, :], c, jnp.zeros((nrow - 1 - b_lat, D_MODEL), F32)], axis=0)
    mod = _ada(cond, w_ada[l], b_ada[l])
    mod_ctx = mod[0:1].reshape(1, 1, 6 * D_MODEL)
    mod_lat = mod[1:1 + b_lat].reshape(b_lat, 1, 6 * D_MODEL)

    wr = jnp.concatenate([moe_w_r2[l].transpose(1, 0, 2).reshape(D_MODEL, N_EXPERTS), moe_w_r1[l]], axis=1)
    wr = jnp.pad(wr, ((0, 0), (0, LANES - wr.shape[1])))
    br = jnp.concatenate([moe_b_r2[l].reshape(-1), moe_b_r1[l]])
    br = jnp.pad(br, (0, LANES - br.shape[0])).reshape(1, LANES)
    wr_hi, wr_lo = _split(wr)

    wts = {
        'w_in': w_in[l].astype(BF16), 'hy_conv_w': hy_conv_w[l], 'hy_conv_b': hy_conv_b[l],
        'hy_fbias': hy_fbias[l], 'w_glu': s5_w_glu[l].astype(BF16), 'b_glu': s5_b_glu[l].reshape(1, -1),
        'out_norm_g': out_norm_g[l].reshape(1, -1), 'w_out': w_out[l].astype(BF16),
        'ln1_g': ln1_g[l].reshape(1, -1), 'ln1_b': ln1_b[l].reshape(1, -1),
        'wr_hi': wr_hi, 'wr_lo': wr_lo, 'br': br,
        'w_gate': moe_w_gate[l].astype(BF16), 'w_up': moe_w_up[l].astype(BF16),
        'w_down': moe_w_down[l].astype(BF16),
        'ln2_g': ln2_g[l].reshape(1, -1), 'ln2_b': ln2_b[l].reshape(1, -1),
    }

    s5ops = _s5_operators(s5_a_re[l], s5_a_im[l], s5_log_dt[l], s5_b_re[l], s5_b_im[l],
                          s5_c_re[l], s5_c_im[l], s5_d[l])
    tabs_ctx = _tables(l_ctx)
    tabs_lat = _tables(l_lat)
    filt_args = (hy_f_w1[l], hy_f_b1[l], hy_f_w2[l], hy_f_b2[l], hy_f_w3[l], hy_freq[l])
    filt_ctx = _hyena_filters(l_ctx, tabs_ctx, *filt_args)
    filt_lat = _hyena_filters(l_lat, tabs_lat, *filt_args)

    zero = jnp.zeros((g, b_ctx, 2 * p), F32)
    yhy_c, ys5_c, f_re, f_im = _mixers(x_prompt, None, mod_ctx, zero, zero, tabs_ctx, filt_ctx, s5ops, wts, 512)
    unpack = lambda f: f.reshape(g, b_ctx, 2, p).transpose(1, 2, 0, 3)[:, None]
    new_re, new_im = unpack(f_re), unpack(f_im)

    pack = lambda s: s[:, l].transpose(2, 0, 1, 3).reshape(g, b_lat, 2 * p)
    pos = _grid_pos_embed(l_lat)
    yhy_l, ys5_l, _, _ = _mixers(x_sample, pos, mod_lat, pack(state_s5_re), pack(state_s5_im),
                                 tabs_lat, filt_lat, s5ops, wts, 512)

    n_ctx = b_ctx * l_ctx
    x1_all, h2_all, gates_all = _out_proj(
        x_prompt.reshape(n_ctx, D_MODEL), x_sample.reshape(b_lat * l_lat, D_MODEL), pos,
        yhy_c, yhy_l, ys5_c, ys5_l, mod, wts['w_glu'], wts['b_glu'], wts['out_norm_g'], wts['w_out'],
        wts['ln1_g'], wts['ln1_b'], wts['wr_hi'], wts['wr_lo'], wts['br'], 512)
    y_ctx, y_lat = _moe(h2_all, gates_all, x1_all, mod,
                        wts['w_gate'], wts['w_up'], wts['w_down'], wts['ln2_g'], wts['ln2_b'],
                        n_ctx, l_lat)
    return (y_ctx.reshape(x_prompt.shape), y_lat.reshape(x_sample.shape), new_re, new_im)
```

```python
import functools
import math

import numpy as np
import jax
import jax.numpy as jnp
from jax import lax
from jax.experimental import pallas as pl
from jax.experimental.pallas import tpu as pltpu

F32 = jnp.float32
BF16 = jnp.bfloat16

D_MODEL = 1024
DEPTH = 1
GRID_W = 64
POS_BASE = 10000.0
D_HY = 512
D_S5 = 512
S5_CH = 16
S5_GROUPS = 32
S5_STATE = 64
S5_CHUNK = 16
S5_ROW = S5_CHUNK * S5_CH
HY_BANDS = 16
HY_EMB = 1 + 2 * HY_BANDS
HY_HID = 64
HY_MIN_DECAY = math.log(1e-2) / 1.5
HY_MAX_DECAY = math.log(1e-2) / 0.3
N_EGROUPS = 4
N_EPG = 4
N_EXPERTS = 16
D_EXPERT = 512
LN_EPS = 1e-5
ALPHA = (2.0 * DEPTH) ** 0.25
LANES = 128
S5_GB = LANES // S5_CH
HY_CW = 512
MOE_ST = 512
MOE_SLOTS = 640
MOE_UNIT = 16
MOE_TM = 512
VMEM_LIMIT = 60000 * 1024


def _cparams(sem):
    return pltpu.CompilerParams(dimension_semantics=sem, vmem_limit_bytes=VMEM_LIMIT)


def _split(x):
    hi = x.astype(BF16)
    lo = (x - hi.astype(F32)).astype(BF16)
    return hi, lo


def _dot(a, b):
    return jnp.dot(a, b, preferred_element_type=F32)


def _dot_t(a, b):
    return lax.dot_general(a, b, (((1,), (1,)), ((), ())), preferred_element_type=F32)


def _mm3(a, b):
    ah, al = _split(a)
    bh, bl = _split(b)
    return _dot(ah, bh) + _dot(al, bh) + _dot(ah, bl)


def _mm3_pre(ah, al, b):
    bh, bl = _split(b)
    return _dot(ah, bh) + _dot(al, bh) + _dot(ah, bl)


def _mm3_t(a, b):
    ah, al = _split(a)
    bh, bl = _split(b)
    return _dot_t(ah, bh) + _dot_t(al, bh) + _dot_t(ah, bl)


def _dot_hp(a, b):
    return jnp.dot(a, b, preferred_element_type=F32, precision=lax.Precision.HIGHEST)


def _norm(x):
    xc = x - jnp.mean(x, axis=-1, keepdims=True)
    return xc * lax.rsqrt(jnp.mean(xc * xc, axis=-1, keepdims=True) + LN_EPS)


def _rms(y):
    return y * lax.rsqrt(jnp.mean(y * y, axis=-1, keepdims=True) + LN_EPS)


def _ada_kernel(cond_ref, w_ref, b_ref, o_ref):
    c = jax.nn.silu(cond_ref[...])
    o_ref[...] = _mm3(c, w_ref[...]) + b_ref[...]


def _ada(cond, w_ada, b_ada):
    nb = cond.shape[0]
    n = w_ada.shape[1]
    tn = 1024
    return pl.pallas_call(
        _ada_kernel,
        grid=(n // tn,),
        in_specs=[pl.BlockSpec((nb, D_MODEL), lambda j: (0, 0)),
                  pl.BlockSpec((D_MODEL, tn), lambda j: (0, j)),
                  pl.BlockSpec((1, tn), lambda j: (0, j))],
        out_specs=pl.BlockSpec((nb, tn), lambda j: (0, j)),
        out_shape=jax.ShapeDtypeStruct((nb, n), F32),
        compiler_params=_cparams(("arbitrary",)),
        name="ada",
    )(cond, w_ada, b_ada.reshape(1, n))


def _dft_tables(n_tok):
    n = 2 * n_tok
    idx = np.arange(n_tok, dtype=np.int64)
    m = (idx[:, None] * idx[None, :]) % n
    ang = 2.0 * np.pi * m.astype(np.float64) / n
    cm = np.cos(ang)
    sm = -np.sin(ang)
    sm[0, :] = 1.0 - 2.0 * (idx % 2)
    return cm.astype(np.float32), sm.astype(np.float32)


def _filt_kernel(n_tok, z_ref, t_ref, w1_ref, b1_ref, w2_ref, b2_ref, fr_ref, w3f_ref, w3b_ref,
                 dl_ref, cmh_ref, cml_ref, smh_ref, sml_ref, a_ref, bz_ref, dd_ref):
    fr = fr_ref[...]
    h = jnp.sin(fr * (_dot_hp(z_ref[...], w1_ref[...]) + b1_ref[...]))
    h = jnp.sin(fr * (_dot_hp(h, w2_ref[...]) + b2_ref[...]))
    decay = jnp.exp(-t_ref[...] * dl_ref[...])
    row = lax.broadcasted_iota(jnp.int32, decay.shape, 0)
    hf = _dot_hp(h, w3f_ref[...]) * decay
    hb = jnp.where(row == 0, 0.0, _dot_hp(h, w3b_ref[...]) * decay)
    p = hf + hb
    q = hf - hb
    k_re = _mm3_pre(cmh_ref[...], cml_ref[...], p)
    k_im = _mm3_pre(smh_ref[...], sml_ref[...], q)
    sign = jnp.where(row % 2 == 0, 1.0, -1.0)
    nyq = jnp.sum(p * sign, axis=0, keepdims=True)
    inv_n = 1.0 / (2 * n_tok)
    w = jnp.where(row == 0, inv_n, 2.0 * inv_n)
    a_ref[...] = w * k_re
    bz_ref[...] = jnp.where(row == 0, 0.0, w * k_im)
    dd_ref[...] = jnp.where(row == 0, nyq * inv_n, w * k_re)


def _hyena_filters(n_tok, tabs, hy_f_w1, hy_f_b1, hy_f_w2, hy_f_b2, hy_f_w3, hy_freq):
    cmh, cml, smh, sml = tabs[:4]
    t = jnp.linspace(0.0, 1.0, n_tok, dtype=F32)[:, None]
    wv = 2.0 * math.pi * jnp.arange(n_tok, dtype=F32) / n_tok
    fb = jnp.linspace(1e-4, HY_BANDS - 1, HY_BANDS, dtype=F32)
    ang = wv[:, None] * fb[None, :]
    z = jnp.concatenate([t, jnp.cos(ang), -jnp.sin(ang)], axis=-1)
    z = jnp.pad(z, ((0, 0), (0, LANES - HY_EMB)))
    w1 = jnp.pad(hy_f_w1, ((0, LANES - HY_EMB), (0, 0)))
    deltas = jnp.abs(jnp.linspace(HY_MIN_DECAY, HY_MAX_DECAY, D_HY, dtype=F32))[None, :]
    ncb = D_HY // HY_CW
    full = lambda j: (0, 0)
    out_sd = jax.ShapeDtypeStruct((n_tok, 2 * D_HY), F32)
    mat = pl.BlockSpec((n_tok, n_tok), full, pipeline_mode=pl.Buffered(1))
    return pl.pallas_call(
        functools.partial(_filt_kernel, n_tok),
        grid=(2 * ncb,),
        in_specs=[pl.BlockSpec((n_tok, LANES), full),
                  pl.BlockSpec((n_tok, 1), full),
                  pl.BlockSpec((LANES, HY_HID), full),
                  pl.BlockSpec((1, HY_HID), full),
                  pl.BlockSpec((HY_HID, HY_HID), full),
                  pl.BlockSpec((1, HY_HID), full),
                  pl.BlockSpec((1, HY_HID), full),
                  pl.BlockSpec((HY_HID, HY_CW), lambda j: (0, 2 * ncb * (j // ncb) + j % ncb)),
                  pl.BlockSpec((HY_HID, HY_CW), lambda j: (0, 2 * ncb * (j // ncb) + ncb + j % ncb)),
                  pl.BlockSpec((1, HY_CW), lambda j: (0, j % ncb)),
                  mat, mat, mat, mat],
        out_specs=[pl.BlockSpec((n_tok, HY_CW), lambda j: (0, j))] * 3,
        out_shape=[out_sd] * 3,
        compiler_params=_cparams(("arbitrary",)),
        name=f"filt{n_tok}",
    )(z, t, w1, hy_f_b1.reshape(1, -1), hy_f_w2, hy_f_b2.reshape(1, -1), hy_freq.reshape(1, -1),
      hy_f_w3, hy_f_w3, deltas, cmh, cml, smh, sml)


def _hyena_kernel(pv_ref, p1_ref, p2_ref, cwv_ref, cw1_ref, cw2_ref, cbv_ref, cb1_ref, cb2_ref,
                  fbias_ref, cm_ref, sm_ref, st_ref,
                  a0_ref, b0_ref, d0_ref, a1_ref, b1_ref, d1_ref, o_ref):
    n_tok = pv_ref.shape[1]
    row = lax.broadcasted_iota(jnp.int32, (n_tok, pv_ref.shape[2]), 0)

    def short_conv(p_ref, cw_ref, cb_ref):
        p = p_ref[0]
        prev = jnp.where(row == 0, 0.0, pltpu.roll(p, 1, axis=0))
        nxt = jnp.where(row == n_tok - 1, 0.0, pltpu.roll(p, n_tok - 1, axis=0))
        return cb_ref[...] + prev * cw_ref[0:1, :] + p * cw_ref[1:2, :] + nxt * cw_ref[2:3, :]

    cm = cm_ref[...]

    def fftconv(u, a_ref, b_ref, d_ref, skip):
        ub = u.astype(BF16)
        u_re = _dot(cm, ub)
        u_im = _dot(sm_ref[...], ub)
        a, bz, dd = a_ref[...], b_ref[...], d_ref[...]
        y_re = u_re * a - u_im * bz
        y_im = u_re * bz + u_im * dd
        y = _dot(cm, y_re.astype(BF16)) + _dot(st_ref[...], y_im.astype(BF16))
        return y + u * skip

    v = short_conv(pv_ref, cwv_ref, cbv_ref)
    x1 = short_conv(p1_ref, cw1_ref, cb1_ref)
    z = x1 * fftconv(v, a0_ref, b0_ref, d0_ref, fbias_ref[0:1, :])
    x2 = short_conv(p2_ref, cw2_ref, cb2_ref)
    o_ref[0] = x2 * fftconv(z, a1_ref, b1_ref, d1_ref, fbias_ref[1:2, :])


def _hyena(proj_hy, tabs, filt, hy_conv_w, hy_conv_b, hy_fbias):
    bsz, n_tok, _ = proj_hy.shape
    ncb = D_HY // HY_CW
    cmh, _, smh, _, sth, _ = tabs
    fa, fbz, fdd = filt
    cb = hy_conv_b.reshape(1, -1)
    mat = pl.BlockSpec((n_tok, n_tok), lambda b, c: (0, 0), pipeline_mode=pl.Buffered(1))

    def pspec(k):
        return pl.BlockSpec((1, n_tok, HY_CW), lambda b, c: (b, 0, k * ncb + c))

    def cwspec(k):
        return pl.BlockSpec((3, HY_CW), lambda b, c: (0, k * ncb + c))

    def cbspec(k):
        return pl.BlockSpec((1, HY_CW), lambda b, c: (0, k * ncb + c))

    def fspec(o):
        mode = pl.Buffered(1) if ncb == 1 else None
        return pl.BlockSpec((n_tok, HY_CW), lambda b, c: (0, o * ncb + c), pipeline_mode=mode)

    return pl.pallas_call(
        _hyena_kernel,
        grid=(bsz, ncb),
        in_specs=[pspec(0), pspec(1), pspec(2), cwspec(0), cwspec(1), cwspec(2),
                  cbspec(0), cbspec(1), cbspec(2),
                  pl.BlockSpec((2, HY_CW), lambda b, c: (0, c)),
                  mat, mat, mat,
                  fspec(0), fspec(0), fspec(0), fspec(1), fspec(1), fspec(1)],
        out_specs=pl.BlockSpec((1, n_tok, HY_CW), lambda b, c: (b, 0, c)),
        out_shape=jax.ShapeDtypeStruct((bsz, n_tok, D_HY), F32),
        compiler_params=_cparams(("arbitrary", "arbitrary")),
        name=f"hyena{n_tok}",
    )(proj_hy, proj_hy, proj_hy, hy_conv_w, hy_conv_w, hy_conv_w, cb, cb, cb, hy_fbias,
      cmh, smh, sth, fa, fbz, fdd, fa, fbz, fdd)


def _s5ops_kernel(are_ref, aim_ref, ldt_ref, btr_ref, bti_ref, cre_ref, cim_ref, d_ref,
                  mt_ref, erh_ref, erl_ref, eih_ref, eil_ref, gr_ref, gi_ref, atr_ref, ati_ref,
                  er_ref, ei_ref):
    a_re, a_im = are_ref[0], aim_ref[0]
    dt = jnp.exp(ldt_ref[0])
    mag = jnp.exp(a_re * dt)
    ab_re = mag * jnp.cos(a_im * dt)
    ab_im = mag * jnp.sin(a_im * dt)
    n_re, n_im = ab_re - 1.0, ab_im
    den = a_re * a_re + a_im * a_im
    q_re = (n_re * a_re + n_im * a_im) / den
    q_im = (n_im * a_re - n_re * a_im) / den
    bt_re, bt_im = btr_ref[0], bti_ref[0]
    bb_re = q_re * bt_re - q_im * bt_im
    bb_im = q_re * bt_im + q_im * bt_re
    c_re, c_im = cre_ref[0, 0:S5_CH, :], cim_ref[0, 0:S5_CH, :]
    pw = [(jnp.ones_like(ab_re), jnp.zeros_like(ab_re))]
    for _ in range(S5_CHUNK):
        pr, pi = pw[-1]
        pw.append((pr * ab_re - pi * ab_im, pr * ab_im + pi * ab_re))
    lane = lax.broadcasted_iota(jnp.int32, ab_re.shape, 1)
    fwd = lane < S5_STATE
    for s in range(S5_CHUNK):
        e_re = jnp.where(fwd, pw[S5_CHUNK - 1 - s][0], pw[s][0])
        e_im = jnp.where(fwd, pw[S5_CHUNK - 1 - s][1], pw[s][1])
        er_ref[pl.ds(S5_CH * s, S5_CH), :] = e_re * bb_re - e_im * bb_im
        ei_ref[pl.ds(S5_CH * s, S5_CH), :] = e_re * bb_im + e_im * bb_re
        g_re = jnp.where(fwd, pw[s + 1][0], pw[S5_CHUNK - s][0])
        g_im = jnp.where(fwd, pw[s + 1][1], pw[S5_CHUNK - s][1])
        gr_ref[0, pl.ds(S5_CH * s, S5_CH), :] = (c_re * g_re - c_im * g_im).astype(BF16)
        gi_ref[0, pl.ds(S5_CH * s, S5_CH), :] = (-(c_re * g_im + c_im * g_re)).astype(BF16)
    atr_ref[0] = pw[S5_CHUNK][0]
    ati_ref[0] = pw[S5_CHUNK][1]
    er, ei = er_ref[...], ei_ref[...]
    erh_ref[0], erl_ref[0] = _split(er)
    eih_ref[0], eil_ref[0] = _split(ei)
    lane2 = lax.broadcasted_iota(jnp.int32, er.shape, 1)
    row2 = lax.broadcasted_iota(jnp.int32, er.shape, 0)
    f2 = lane2 < S5_STATE
    zero = jnp.zeros_like(er)

    def dot_hp_t(a, b):
        return lax.dot_general(a, b, (((1,), (1,)), ((), ())), preferred_element_type=F32,
                               precision=lax.Precision.HIGHEST)

    cp_re, cp_im = cre_ref[0], cim_ref[0]
    kf = dot_hp_t(jnp.where(f2, er, zero), cp_re) - dot_hp_t(jnp.where(f2, ei, zero), cp_im)
    kb = dot_hp_t(jnp.where(f2, zero, er), cp_re) - dot_hp_t(jnp.where(f2, zero, ei), cp_im)
    d_row = d_ref[0]
    steps_per_vreg = LANES // S5_CH
    for half in range(S5_CHUNK // steps_per_vreg):
        acc = zero
        for tt in range(steps_per_vreg):
            t = half * steps_per_vreg + tt
            nf = S5_CH * (S5_CHUNK - 1 - t)
            nb = S5_CH * t
            col_f = jnp.concatenate([kf[nf:], zero[:nf]], axis=0) if nf else kf
            col_b = jnp.concatenate([zero[:nb], kb[:S5_ROW - nb]], axis=0) if nb else kb
            diag = jnp.where((row2 // S5_CH == t) & (row2 % S5_CH == lane2), d_row, 0.0)
            col = col_f + col_b + diag
            r = pltpu.roll(col, S5_CH * tt, axis=1) if tt else col
            acc = jnp.where((lane2 >= S5_CH * tt) & (lane2 < S5_CH * (tt + 1)), r, acc)
        mt_ref[0, :, LANES * half:LANES * (half + 1)] = acc.astype(BF16)


def _s5_operators(s5_a_re, s5_a_im, s5_log_dt, s5_b_re, s5_b_im, s5_c_re, s5_c_im, s5_d):
    g, p, h = S5_GROUPS, S5_STATE, S5_CH
    cat = lambda x: jnp.concatenate([x[0], x[1]], axis=-1)
    a_re = cat(s5_a_re).reshape(g, 1, 2 * p)
    a_im = cat(s5_a_im).reshape(g, 1, 2 * p)
    ldt = cat(jnp.broadcast_to(s5_log_dt[:, :, None], (2, g, p))).reshape(g, 1, 2 * p)
    bt_re = cat(jnp.swapaxes(s5_b_re, -1, -2))
    bt_im = cat(jnp.swapaxes(s5_b_im, -1, -2))
    cpad = lambda c: jnp.pad(jnp.concatenate([c, c], axis=-1), ((0, 0), (0, LANES - h), (0, 0)))
    c_re, c_im = cpad(s5_c_re), cpad(s5_c_im)
    d_row = jnp.pad(s5_d.reshape(g, 1, h), ((0, 0), (0, 0), (0, LANES - h)))
    vec = pl.BlockSpec((1, 1, 2 * p), lambda i: (i, 0, 0))
    hp = pl.BlockSpec((1, h, 2 * p), lambda i: (i, 0, 0))
    sq = pl.BlockSpec((1, LANES, 2 * p), lambda i: (i, 0, 0))
    big = pl.BlockSpec((1, S5_ROW, 2 * p), lambda i: (i, 0, 0))
    mts = pl.BlockSpec((1, S5_ROW, S5_ROW), lambda i: (i, 0, 0))
    big_sd = jax.ShapeDtypeStruct((g, S5_ROW, 2 * p), BF16)
    vec_sd = jax.ShapeDtypeStruct((g, 1, 2 * p), F32)
    return pl.pallas_call(
        _s5ops_kernel,
        grid=(g,),
        in_specs=[vec, vec, vec, hp, hp, sq, sq, vec],
        out_specs=[mts, big, big, big, big, big, big, vec, vec],
        out_shape=[jax.ShapeDtypeStruct((g, S5_ROW, S5_ROW), BF16)] + [big_sd] * 6 + [vec_sd, vec_sd],
        scratch_shapes=[pltpu.VMEM((S5_ROW, 2 * p), F32)] * 2,
        compiler_params=_cparams(("arbitrary",)),
        name="s5ops",
    )(a_re, a_im, ldt, bt_re, bt_im, c_re, c_im, d_row)


def _block_transpose(xs):
    n = len(xs)
    lane = lax.broadcasted_iota(jnp.int32, xs[0].shape, 1)
    xs = list(xs)
    d = n // 2
    while d:
        keep = ((lane // S5_CH) & d) == 0
        for i in range(n):
            if i & d:
                continue
            lo, hi = xs[i], xs[i + d]
            xs[i] = jnp.where(keep, lo, pltpu.roll(hi, S5_CH * d, axis=1))
            xs[i + d] = jnp.where(keep, pltpu.roll(lo, LANES - S5_CH * d, axis=1), hi)
        d //= 2
    return xs


def _s5_kernel(bsz, n_chunks, u_ref, mt_ref, erh_ref, erl_ref, eih_ref, eil_ref, gr_ref, gi_ref,
               atr_ref, ati_ref, h0r_ref, h0i_ref, y_ref, fr_ref, fi_ref,
               ua_ref, ub_ref, ya_ref, yb_ref, sr_ref, si_ref, xfr_ref, xfi_ref, xbr_ref, xbi_ref):
    nc = n_chunks
    spv = LANES // S5_CH
    rsub = min(nc, 32)

    def to_chunks(b, carry):
        for half, dst in ((0, ua_ref), (1, ub_ref)):
            for r0 in range(0, nc, rsub):
                xs = [u_ref[b, pl.ds(S5_CHUNK * r0 + half * spv + tt, rsub, stride=S5_CHUNK), :]
                      for tt in range(spv)]
                for k, blk in enumerate(_block_transpose(xs)):
                    dst[k, pl.ds(r0 * bsz + b, rsub, stride=bsz), :] = blk
        return carry

    lax.fori_loop(0, bsz, to_chunks, 0)

    lane = lax.broadcasted_iota(jnp.int32, (bsz, 2 * S5_STATE), 1)
    fwd = lane < S5_STATE
    lane_all = lax.broadcasted_iota(jnp.int32, (bsz * nc, 2 * S5_STATE), 1)
    fwd_all = lane_all < S5_STATE

    def group(k, slot):
        u = jnp.concatenate([ua_ref[k], ub_ref[k]], axis=1)
        uh, ul = _split(u)
        sr_ref[slot] = _dot(uh, erh_ref[k]) + _dot(ul, erh_ref[k]) + _dot(uh, erl_ref[k])
        si_ref[slot] = _dot(uh, eih_ref[k]) + _dot(ul, eih_ref[k]) + _dot(uh, eil_ref[k])
        at_re, at_im = atr_ref[k], ati_ref[k]
        y_intra = _dot(uh, mt_ref[k])

        def step(i, xc):
            x_re, x_im = xc
            rf = pl.ds(pl.multiple_of(i * bsz, bsz), bsz)
            rb = pl.ds(pl.multiple_of((nc - 1 - i) * bsz, bsz), bsz)
            xfr_ref[slot, rf, :] = x_re
            xfi_ref[slot, rf, :] = x_im
            xbr_ref[slot, rb, :] = x_re
            xbi_ref[slot, rb, :] = x_im
            s_re = jnp.where(fwd, sr_ref[slot, rf, :], sr_ref[slot, rb, :])
            s_im = jnp.where(fwd, si_ref[slot, rf, :], si_ref[slot, rb, :])
            return (at_re * x_re - at_im * x_im + s_re, at_re * x_im + at_im * x_re + s_im)

        x_re, x_im = lax.fori_loop(0, nc, step, (h0r_ref[k], h0i_ref[k]), unroll=True)
        fr_ref[k] = x_re
        fi_ref[k] = x_im
        xp_re = jnp.where(fwd_all, xfr_ref[slot], xbr_ref[slot]).astype(BF16)
        xp_im = jnp.where(fwd_all, xfi_ref[slot], xbi_ref[slot]).astype(BF16)
        y = y_intra + _dot_t(xp_re, gr_ref[k]) + _dot_t(xp_im, gi_ref[k])
        ya_ref[k] = y[:, :LANES]
        yb_ref[k] = y[:, LANES:]

    def group_pair(j, carry):
        group(2 * j, 0)
        group(2 * j + 1, 1)
        return carry

    lax.fori_loop(0, S5_GB // 2, group_pair, 0)

    def to_tokens(b, carry):
        for half, src in ((0, ya_ref), (1, yb_ref)):
            for r0 in range(0, nc, rsub):
                ys = [src[k, pl.ds(r0 * bsz + b, rsub, stride=bsz), :] for k in range(S5_GB)]
                for tt, blk in enumerate(_block_transpose(ys)):
                    y_ref[b, pl.ds(S5_CHUNK * r0 + half * spv + tt, rsub, stride=S5_CHUNK), :] = blk
        return carry

    lax.fori_loop(0, bsz, to_tokens, 0)


def _s5(u, ops, h0_re, h0_im):
    bsz, n_tok, _ = u.shape
    g, p = S5_GROUPS, S5_STATE
    nc = n_tok // S5_CHUNK
    rows = nc * bsz
    tok = pl.BlockSpec((bsz, n_tok, LANES), lambda j: (0, 0, j))
    gspec = lambda shape: pl.BlockSpec((S5_GB,) + shape, lambda j: (j, 0, 0))
    op = gspec((S5_ROW, 2 * p))
    return pl.pallas_call(
        functools.partial(_s5_kernel, bsz, nc),
        grid=(g // S5_GB,),
        in_specs=[tok, gspec((S5_ROW, S5_ROW)), op, op, op, op, op, op,
                  gspec((1, 2 * p)), gspec((1, 2 * p)), gspec((bsz, 2 * p)), gspec((bsz, 2 * p))],
        out_specs=[tok, gspec((bsz, 2 * p)), gspec((bsz, 2 * p))],
        out_shape=[jax.ShapeDtypeStruct((bsz, n_tok, D_S5), F32),
                   jax.ShapeDtypeStruct((g, bsz, 2 * p), F32),
                   jax.ShapeDtypeStruct((g, bsz, 2 * p), F32)],
        scratch_shapes=([pltpu.VMEM((S5_GB, rows, LANES), F32)] * 4
                        + [pltpu.VMEM((2, rows, 2 * p), F32)] * 6),
        compiler_params=_cparams(("arbitrary",)),
        name=f"s5_{n_tok}",
    )(u, *ops, h0_re, h0_im)


def _in_kernel(has_pos, *refs):
    if has_pos:
        x_ref, pos_ref, mod_ref, w_ref, hy_ref, s5_ref = refs
        x = x_ref[0] + pos_ref[...]
    else:
        x_ref, mod_ref, w_ref, hy_ref, s5_ref = refs
        x = x_ref[0]
    sh1 = mod_ref[0, :, 0:D_MODEL]
    sc1 = mod_ref[0, :, D_MODEL:2 * D_MODEL]
    h = _norm(x) * (1.0 + sc1) + sh1
    proj = _dot(h.astype(BF16), w_ref[...])
    hy_ref[0] = proj[:, :3 * D_HY]
    s5_ref[0] = proj[:, 3 * D_HY:]


def _in_proj(x3, pos, mod3, w_in_bf, tm):
    nb, lt, _ = x3.shape
    has_pos = pos is not None
    per_batch = mod3.shape[0] > 1
    midx = (lambda b, i: (b, 0, 0)) if per_batch else (lambda b, i: (0, 0, 0))
    in_specs = [pl.BlockSpec((1, tm, D_MODEL), lambda b, i: (b, i, 0))]
    args = [x3]
    if has_pos:
        in_specs.append(pl.BlockSpec((tm, D_MODEL), lambda b, i: (i, 0)))
        args.append(pos)
    in_specs += [pl.BlockSpec((1, 1, 6 * D_MODEL), midx),
                 pl.BlockSpec((D_MODEL, 3 * D_HY + D_S5), lambda b, i: (0, 0))]
    args += [mod3, w_in_bf]
    return pl.pallas_call(
        functools.partial(_in_kernel, has_pos),
        grid=(nb, lt // tm),
        in_specs=in_specs,
        out_specs=[pl.BlockSpec((1, tm, 3 * D_HY), lambda b, i: (b, i, 0)),
                   pl.BlockSpec((1, tm, D_S5), lambda b, i: (b, i, 0))],
        out_shape=[jax.ShapeDtypeStruct((nb, lt, 3 * D_HY), F32),
                   jax.ShapeDtypeStruct((nb, lt, D_S5), F32)],
        compiler_params=_cparams(("arbitrary", "arbitrary")),
        name=f"in_proj{nb}",
    )(*args)


def _route(logits):
    lane = lax.broadcasted_iota(jnp.int32, logits.shape, 1)
    lane_f = lane.astype(F32)
    neg = -jnp.inf
    big = float(LANES)
    m1 = (lane >= N_EXPERTS) & (lane < N_EXPERTS + N_EGROUPS)
    l1 = jnp.where(m1, logits, neg)
    top1 = jnp.max(l1, axis=-1, keepdims=True)
    grp = jnp.min(jnp.where(l1 == top1, lane_f, big), axis=-1, keepdims=True) - float(N_EXPERTS)
    den = jnp.sum(jnp.where(m1, jnp.exp(logits - top1), 0.0), axis=-1, keepdims=True)
    p_grp = 1.0 / den
    lo = grp * float(N_EPG)
    m2 = (lane_f >= lo) & (lane_f < lo + float(N_EPG))
    l2 = jnp.where(m2, logits, neg)
    v1 = jnp.max(l2, axis=-1, keepdims=True)
    i1 = jnp.min(jnp.where(l2 == v1, lane_f, big), axis=-1, keepdims=True)
    l2b = jnp.where(lane_f == i1, neg, l2)
    v2 = jnp.max(l2b, axis=-1, keepdims=True)
    i2 = jnp.min(jnp.where(l2b == v2, lane_f, big), axis=-1, keepdims=True)
    e = jnp.exp(v2 - v1)
    w1 = 1.0 / (1.0 + e)
    w2 = e / (1.0 + e)
    gates = jnp.where(lane_f == i1, w1 * p_grp, 0.0) + jnp.where(lane_f == i2, w2 * p_grp, 0.0)
    return jnp.where(lane_f == grp + float(N_EXPERTS), 1.0, gates)


def _out_kernel(n_ctx_blocks, xc_ref, xl_ref, pos_ref, yhyc_ref, yhyl_ref, ys5c_ref, ys5l_ref, mod_ref,
                wglu_ref, bglu_ref, ong_ref, wout_ref, ln1g_ref, ln1b_ref, wrh_ref, wrl_ref, br_ref,
                x1_ref, h2_ref, gate_ref):
    is_ctx = pl.program_id(0) < n_ctx_blocks
    x = jnp.where(is_ctx, xc_ref[...], xl_ref[...] + pos_ref[...])
    y = jnp.where(is_ctx, ys5c_ref[...], ys5l_ref[...])
    y_hy = jnp.where(is_ctx, yhyc_ref[...], yhyl_ref[...])
    s5 = jax.nn.gelu(y) * jax.nn.sigmoid(_dot(y.astype(BF16), wglu_ref[...]) + bglu_ref[...])
    m_hy = _rms(y_hy) * ong_ref[:, 0:D_HY]
    m_s5 = _rms(s5) * ong_ref[:, D_HY:]
    o = (_dot(m_hy.astype(BF16), wout_ref[0:D_HY, :]) + _dot(m_s5.astype(BF16), wout_ref[D_HY:, :]))
    g1 = mod_ref[0, :, 2 * D_MODEL:3 * D_MODEL]
    sh2 = mod_ref[0, :, 3 * D_MODEL:4 * D_MODEL]
    sc2 = mod_ref[0, :, 4 * D_MODEL:5 * D_MODEL]
    x1 = _norm(ALPHA * x + g1 * o) * ln1g_ref[...] + ln1b_ref[...]
    x1_ref[...] = x1
    h2 = _norm(x1) * (1.0 + sc2) + sh2
    h2_ref[...] = h2.astype(BF16)
    hh, hl = _split(h2)
    logits = (_dot(hh, wrh_ref[...]) + _dot(hl, wrh_ref[...]) + _dot(hh, wrl_ref[...]) + br_ref[...])
    gate_ref[...] = _route(logits)


def _out_proj(xc, xl, pos, yhy_c, yhy_l, ys5_c, ys5_l, mod, wglu_bf, bglu, ong, wout_bf, ln1g, ln1b,
              wr_hi, wr_lo, br, tm):
    n_ctx, n_lat = xc.shape[0], xl.shape[0]
    l_lat = pos.shape[0]
    ncb, nlb, npb = n_ctx // tm, n_lat // tm, l_lat // tm
    ctx = lambda w: pl.BlockSpec((tm, w), lambda i: (jnp.minimum(i, ncb - 1), 0))
    lat = lambda w: pl.BlockSpec((tm, w), lambda i: (jnp.maximum(i - ncb, 0), 0))
    full = lambda shape: pl.BlockSpec(shape, lambda i: (0,) * len(shape))
    out = lambda w: pl.BlockSpec((tm, w), lambda i: (i, 0))
    mod_idx = lambda i: (jnp.where(i < ncb, 0, 1 + jnp.maximum(i - ncb, 0) // npb), 0, 0)
    n_all = n_ctx + n_lat
    return pl.pallas_call(
        functools.partial(_out_kernel, ncb),
        grid=(ncb + nlb,),
        in_specs=[ctx(D_MODEL), lat(D_MODEL),
                  pl.BlockSpec((tm, D_MODEL), lambda i: (jnp.maximum(i - ncb, 0) % npb, 0)),
                  ctx(D_HY), lat(D_HY), ctx(D_S5), lat(D_S5),
                  pl.BlockSpec((1, 1, 6 * D_MODEL), mod_idx),
                  full((D_S5, D_S5)), full((1, D_S5)), full((1, D_MODEL)), full((D_MODEL, D_MODEL)),
                  full((1, D_MODEL)), full((1, D_MODEL)), full((D_MODEL, LANES)), full((D_MODEL, LANES)),
                  full((1, LANES))],
        out_specs=[out(D_MODEL), out(D_MODEL), out(LANES)],
        out_shape=[jax.ShapeDtypeStruct((n_all, D_MODEL), F32),
                   jax.ShapeDtypeStruct((n_all, D_MODEL), BF16),
                   jax.ShapeDtypeStruct((n_all, LANES), F32)],
        compiler_params=_cparams(("arbitrary",)),
        name="out_proj",
    )(xc, xl, pos, yhy_c, yhy_l, ys5_c, ys5_l, mod.reshape(mod.shape[0], 1, 6 * D_MODEL),
      wglu_bf, bglu, ong, wout_bf, ln1g, ln1b, wr_hi, wr_lo, br)


def _perm_t(gates, loc_ref, s):
    n = gates.shape[0]
    lane = lax.broadcasted_iota(jnp.int32, gates.shape, 1)
    oh = jnp.where((lane >= N_EXPERTS) & (lane < N_EXPERTS + N_EGROUPS), gates, 0.0)
    r = lax.broadcasted_iota(jnp.int32, (n, n), 0)
    c = lax.broadcasted_iota(jnp.int32, (n, n), 1)
    earlier = jnp.where(c < r, 1.0, 0.0).astype(BF16)
    cum = _dot(earlier, oh.astype(BF16))
    rank = jnp.sum(cum * oh, axis=-1, keepdims=True)
    lane1 = lax.broadcasted_iota(jnp.int32, (1, LANES), 1)
    locv = jnp.zeros((1, LANES), F32)
    for grp in range(N_EGROUPS):
        locv = jnp.where(lane1 == N_EXPERTS + grp, loc_ref[N_EGROUPS * s + grp].astype(F32), locv)
    dest = rank + jnp.sum(oh * locv, axis=-1, keepdims=True)
    slot = lax.broadcasted_iota(jnp.int32, (n, MOE_SLOTS), 1).astype(F32)
    return jnp.where(slot == dest, 1.0, 0.0)


def _segment_copies(s, loc_ref, len_ref, off_ref, make):
    for grp in range(N_EGROUPS):
        loc = loc_ref[N_EGROUPS * s + grp]
        off = off_ref[N_EGROUPS * s + grp]
        n_units = len_ref[N_EGROUPS * s + grp] // MOE_UNIT

        def body(i, carry):
            make(pl.multiple_of(loc + MOE_UNIT * i, MOE_UNIT), pl.multiple_of(off + MOE_UNIT * i, MOE_UNIT))
            return carry

        lax.fori_loop(0, n_units, body, 0)


def _moe_sort_kernel(loc_ref, len_ref, off_ref, h_ref, gate_ref, xs_in, gs_in, xs_hbm, gs_hbm,
                     xs_v, gs_v, sem):
    del xs_in, gs_in
    s = pl.program_id(0)
    slot = s % 2
    gates = gate_ref[...]
    p = _perm_t(gates, loc_ref, s).T.astype(BF16)
    xs_v[slot] = _dot(p, h_ref[...]).astype(BF16)
    g_hi = gates.astype(BF16)
    r1 = gates - g_hi.astype(F32)
    g_mid = r1.astype(BF16)
    g_lo = (r1 - g_mid.astype(F32)).astype(BF16)
    gs_v[slot] = _dot(p, g_hi) + _dot(p, g_mid) + _dot(p, g_lo)

    def copies(buf):
        def x_copy(lr, gr):
            return pltpu.make_async_copy(xs_v.at[buf, pl.ds(lr, MOE_UNIT), :],
                                         xs_hbm.at[pl.ds(gr, MOE_UNIT), :], sem.at[0, buf])

        def g_copy(lr, gr):
            return pltpu.make_async_copy(gs_v.at[buf, pl.ds(lr, MOE_UNIT), :],
                                         gs_hbm.at[pl.ds(gr, MOE_UNIT), :], sem.at[1, buf])

        def start(lr, gr):
            x_copy(lr, gr).start()
            g_copy(lr, gr).start()

        def wait(lr, gr):
            x_copy(lr, gr).wait()
            g_copy(lr, gr).wait()

        return start, wait

    _segment_copies(s, loc_ref, len_ref, off_ref, copies(slot)[0])

    @pl.when(s > 0)
    def _():
        _segment_copies(s - 1, loc_ref, len_ref, off_ref, copies(1 - slot)[1])

    @pl.when(s == pl.num_programs(0) - 1)
    def _():
        _segment_copies(s, loc_ref, len_ref, off_ref, copies(slot)[1])


def _moe_expert_kernel(bg_ref, nb_ref, xs_ref, gs_ref, wg_ref, wu_ref, wd_ref, o_ref):
    i = pl.program_id(0)

    @pl.when(i < nb_ref[0])
    def _():
        grp = bg_ref[i]
        x = xs_ref[...]
        gates = gs_ref[...]
        lane = lax.broadcasted_iota(jnp.int32, gates.shape, 1)
        acc = jnp.zeros(o_ref.shape, F32)
        for e in range(N_EPG):
            a = _dot(x, wg_ref[e])
            u = _dot(x, wu_ref[e])
            ge = jnp.sum(jnp.where(lane == N_EPG * grp + e, gates, 0.0), axis=-1, keepdims=True)
            hid = jax.nn.silu(a) * u * ge
            acc = acc + _dot(hid.astype(BF16), wd_ref[e])
        o_ref[...] = acc.astype(BF16)

    @pl.when(i >= nb_ref[0])
    def _():
        o_ref[...] = jnp.zeros_like(o_ref)


def _moe_combine_kernel(loc_ref, len_ref, off_ref, gate_ref, x1_ref, mod_ref, ln2g_ref, ln2b_ref, o_hbm,
                        ctx_ref, lat_ref, o_v, sem, *, n_ctx_tiles):
    s = pl.program_id(0)
    slot = s % 2

    def copies(buf):
        def o_copy(lr, gr):
            return pltpu.make_async_copy(o_hbm.at[pl.ds(gr, MOE_UNIT), :],
                                         o_v.at[buf, pl.ds(lr, MOE_UNIT), :], sem.at[buf])

        return (lambda lr, gr: o_copy(lr, gr).start()), (lambda lr, gr: o_copy(lr, gr).wait())

    @pl.when(s == 0)
    def _():
        o_v[...] = jnp.zeros_like(o_v)
        _segment_copies(s, loc_ref, len_ref, off_ref, copies(slot)[0])

    @pl.when(s + 1 < pl.num_programs(0))
    def _():
        _segment_copies(s + 1, loc_ref, len_ref, off_ref, copies(1 - slot)[0])

    pt = _perm_t(gate_ref[...], loc_ref, s).astype(BF16)
    _segment_copies(s, loc_ref, len_ref, off_ref, copies(slot)[1])
    f = _dot(pt, o_v[slot])
    g2 = mod_ref[0, :, 5 * D_MODEL:6 * D_MODEL]
    x2 = _norm(ALPHA * x1_ref[...] + g2 * f) * ln2g_ref[...] + ln2b_ref[...]

    @pl.when(s < n_ctx_tiles)
    def _():
        ctx_ref[...] = x2

    @pl.when(s >= n_ctx_tiles)
    def _():
        lat_ref[...] = x2


def _moe_plan(gates_all, n_blocks):
    n_tiles = gates_all.shape[0] // MOE_ST
    oh = gates_all[:, N_EXPERTS:N_EXPERTS + N_EGROUPS]
    cnt = jnp.sum(oh.reshape(n_tiles, MOE_ST, N_EGROUPS), axis=1).astype(jnp.int32)
    len16 = ((cnt + MOE_UNIT - 1) // MOE_UNIT) * MOE_UNIT
    loc = jnp.cumsum(len16, axis=1) - len16
    rows_g = jnp.sum(len16, axis=0)
    reg_g = ((rows_g + MOE_TM - 1) // MOE_TM) * MOE_TM
    reg_start = jnp.cumsum(reg_g) - reg_g
    off = reg_start[None, :] + jnp.cumsum(len16, axis=0) - len16
    blk_end = jnp.cumsum(reg_g // MOE_TM)
    bi = jnp.arange(n_blocks, dtype=jnp.int32)
    blk_group = jnp.minimum(jnp.sum((bi[:, None] >= blk_end[None, :]).astype(jnp.int32), axis=1),
                            N_EGROUPS - 1)
    flat = lambda a: a.reshape(-1).astype(jnp.int32)
    return flat(loc), flat(len16), flat(off), blk_group.astype(jnp.int32), blk_end[-1:].astype(jnp.int32)


def _moe(h2_all, gates_all, x1_all, mod, wg_bf, wu_bf, wd_bf, ln2g, ln2b, n_ctx, tokens_per_mod_row):
    n_tok = h2_all.shape[0]
    n_tiles = n_tok // MOE_ST
    n_ctx_tiles = n_ctx // MOE_ST
    max_rows = n_tok + n_tiles * N_EGROUPS * (MOE_UNIT - 1) + N_EGROUPS * (MOE_TM - 1)
    n_blocks = -(-max_rows // MOE_TM)
    n_rows = n_blocks * MOE_TM
    loc, len16, off, blk_group, n_used = _moe_plan(gates_all, n_blocks)

    tile = lambda w: pl.BlockSpec((MOE_ST, w), lambda s, *_: (s, 0))
    anyspec = pl.BlockSpec(memory_space=pl.ANY)
    xs, gs = pl.pallas_call(
        _moe_sort_kernel,
        grid_spec=pltpu.PrefetchScalarGridSpec(
            num_scalar_prefetch=3, grid=(n_tiles,),
            in_specs=[tile(D_MODEL), tile(LANES), anyspec, anyspec],
            out_specs=[anyspec, anyspec],
            scratch_shapes=[pltpu.VMEM((2, MOE_SLOTS, D_MODEL), BF16), pltpu.VMEM((2, MOE_SLOTS, LANES), F32),
                            pltpu.SemaphoreType.DMA((2, 2))]),
        out_shape=[jax.ShapeDtypeStruct((n_rows, D_MODEL), BF16),
                   jax.ShapeDtypeStruct((n_rows, LANES), F32)],
        input_output_aliases={5: 0, 6: 1},
        compiler_params=_cparams(("arbitrary",)),
        name="moe_sort",
    )(loc, len16, off, h2_all, gates_all, jnp.zeros((n_rows, D_MODEL), BF16), jnp.zeros((n_rows, LANES), F32))

    blk = lambda w: pl.BlockSpec((MOE_TM, w), lambda i, bg, nb: (jnp.minimum(i, nb[0] - 1), 0))
    wspec = lambda a, b: pl.BlockSpec((N_EPG, a, b), lambda i, bg, nb: (bg[i], 0, 0))
    o_sorted = pl.pallas_call(
        _moe_expert_kernel,
        grid_spec=pltpu.PrefetchScalarGridSpec(
            num_scalar_prefetch=2, grid=(n_blocks,),
            in_specs=[blk(D_MODEL), blk(LANES), wspec(D_MODEL, D_EXPERT), wspec(D_MODEL, D_EXPERT),
                      wspec(D_EXPERT, D_MODEL)],
            out_specs=pl.BlockSpec((MOE_TM, D_MODEL), lambda i, bg, nb: (i, 0))),
        out_shape=jax.ShapeDtypeStruct((n_rows, D_MODEL), BF16),
        compiler_params=_cparams(("arbitrary",)),
        name="moe_experts",
    )(blk_group, n_used, xs, gs, wg_bf, wu_bf, wd_bf)

    lat_per_row = tokens_per_mod_row // MOE_ST

    def mod_idx(s, *_):
        return (jnp.where(s < n_ctx_tiles, 0, 1 + (s - n_ctx_tiles) // lat_per_row), 0, 0)

    vec = pl.BlockSpec((1, D_MODEL), lambda s, *_: (0, 0))
    return pl.pallas_call(
        functools.partial(_moe_combine_kernel, n_ctx_tiles=n_ctx_tiles),
        grid_spec=pltpu.PrefetchScalarGridSpec(
            num_scalar_prefetch=3, grid=(n_tiles,),
            in_specs=[tile(LANES), tile(D_MODEL), pl.BlockSpec((1, 1, 6 * D_MODEL), mod_idx), vec, vec,
                      anyspec],
            out_specs=[pl.BlockSpec((MOE_ST, D_MODEL), lambda s, *_: (jnp.minimum(s, n_ctx_tiles - 1), 0)),
                       pl.BlockSpec((MOE_ST, D_MODEL), lambda s, *_: (jnp.maximum(s - n_ctx_tiles, 0), 0))],
            scratch_shapes=[pltpu.VMEM((2, MOE_SLOTS, D_MODEL), BF16), pltpu.SemaphoreType.DMA((2,))]),
        out_shape=[jax.ShapeDtypeStruct((n_ctx, D_MODEL), F32),
                   jax.ShapeDtypeStruct((n_tok - n_ctx, D_MODEL), F32)],
        compiler_params=_cparams(("arbitrary",)),
        name="moe_combine",
    )(loc, len16, off, gates_all, x1_all, mod.reshape(mod.shape[0], 1, 6 * D_MODEL), ln2g, ln2b, o_sorted)


def _grid_pos_embed(n_tokens):
    rows = n_tokens // GRID_W
    row = np.repeat(np.arange(rows, dtype=np.float64), GRID_W)
    col = np.tile(np.arange(GRID_W, dtype=np.float64), rows)
    quarter = D_MODEL // 4
    omega = 1.0 / (POS_BASE ** (np.arange(quarter, dtype=np.float64) / quarter))
    er = row[:, None] * omega
    ec = col[:, None] * omega
    return jnp.asarray(np.concatenate([np.sin(er), np.cos(er), np.sin(ec), np.cos(ec)], axis=-1), F32)


def _tables(n_tok):
    cm, sm = _dft_tables(n_tok)
    out = []
    for t in (cm, sm, np.ascontiguousarray(sm.T)):
        hi, lo = _split(jnp.asarray(t))
        out += [hi, lo]
    return tuple(out)


def _mixers(x, pos, mod3, h0_re, h0_im, tabs, filt, s5ops, wts, tm):
    bsz, n_tok, _ = x.shape
    shared = mod3.shape[0] == 1
    x3 = x.reshape(1, bsz * n_tok, D_MODEL) if shared else x
    proj_hy, u_s5 = _in_proj(x3, pos, mod3, wts['w_in'], tm)
    y_hy = _hyena(proj_hy.reshape(bsz, n_tok, 3 * D_HY), tabs, filt,
                  wts['hy_conv_w'], wts['hy_conv_b'], wts['hy_fbias'])
    y_s5, f_re, f_im = _s5(u_s5.reshape(bsz, n_tok, D_S5), s5ops, h0_re, h0_im)
    return y_hy.reshape(bsz * n_tok, D_HY), y_s5.reshape(bsz * n_tok, D_S5), f_re, f_im


def kernel(x_prompt, x_sample, state_s5_re, state_s5_im, c, c_ctx, w_ada, b_ada, w_in, hy_conv_w, hy_conv_b, hy_f_w1, hy_f_b1, hy_f_w2, hy_f_b2, hy_f_w3, hy_freq, hy_fbias, s5_a_re, s5_a_im, s5_log_dt, s5_b_re, s5_b_im, s5_c_re, s5_c_im, s5_d, s5_w_glu, s5_b_glu, out_norm_g, w_out, ln1_g, ln1_b, moe_w_r1, moe_b_r1, moe_w_r2, moe_b_r2, moe_w_gate, moe_w_up, moe_w_down, ln2_g, ln2_b):
    b_ctx, l_ctx, _ = x_prompt.shape
    b_lat, l_lat, _ = x_sample.shape
    g, p = S5_GROUPS, S5_STATE
    assert w_ada.shape[0] == 1, "single-layer trunk"
    l = 0

    nrow = 16
    cond = jnp.concatenate([c_ctx[None, :], c, jnp.zeros((nrow - 1 - b_lat, D_MODEL), F32)], axis=0)
    mod = _ada(cond, w_ada[l], b_ada[l])
    mod_ctx = mod[0:1].reshape(1, 1, 6 * D_MODEL)
    mod_lat = mod[1:1 + b_lat].reshape(b_lat, 1, 6 * D_MODEL)

    wr = jnp.concatenate([moe_w_r2[l].transpose(1, 0, 2).reshape(D_MODEL, N_EXPERTS), moe_w_r1[l]], axis=1)
    wr = jnp.pad(wr, ((0, 0), (0, LANES - wr.shape[1])))
    br = jnp.concatenate([moe_b_r2[l].reshape(-1), moe_b_r1[l]])
    br = jnp.pad(br, (0, LANES - br.shape[0])).reshape(1, LANES)
    wr_hi, wr_lo = _split(wr)

    wts = {
        'w_in': w_in[l].astype(BF16), 'hy_conv_w': hy_conv_w[l], 'hy_conv_b': hy_conv_b[l],
        'hy_fbias': hy_fbias[l], 'w_glu': s5_w_glu[l].astype(BF16), 'b_glu': s5_b_glu[l].reshape(1, -1),
        'out_norm_g': out_norm_g[l].reshape(1, -1), 'w_out': w_out[l].astype(BF16),
        'ln1_g': ln1_g[l].reshape(1, -1), 'ln1_b': ln1_b[l].reshape(1, -1),
        'wr_hi': wr_hi, 'wr_lo': wr_lo, 'br': br,
        'w_gate': moe_w_gate[l].astype(BF16), 'w_up': moe_w_up[l].astype(BF16),
        'w_down': moe_w_down[l].astype(BF16),
        'ln2_g': ln2_g[l].reshape(1, -1), 'ln2_b': ln2_b[l].reshape(1, -1),
    }

    s5ops = _s5_operators(s5_a_re[l], s5_a_im[l], s5_log_dt[l], s5_b_re[l], s5_b_im[l],
                          s5_c_re[l], s5_c_im[l], s5_d[l])
    tabs_ctx = _tables(l_ctx)
    tabs_lat = _tables(l_lat)
    filt_args = (hy_f_w1[l], hy_f_b1[l], hy_f_w2[l], hy_f_b2[l], hy_f_w3[l], hy_freq[l])
    filt_ctx = _hyena_filters(l_ctx, tabs_ctx, *filt_args)
    filt_lat = _hyena_filters(l_lat, tabs_lat, *filt_args)

    zero = jnp.zeros((g, b_ctx, 2 * p), F32)
    yhy_c, ys5_c, f_re, f_im = _mixers(x_prompt, None, mod_ctx, zero, zero, tabs_ctx, filt_ctx, s5ops, wts, 512)
    unpack = lambda f: f.reshape(g, b_ctx, 2, p).transpose(1, 2, 0, 3)[:, None]
    new_re, new_im = unpack(f_re), unpack(f_im)

    pack = lambda s: s[:, l].transpose(2, 0, 1, 3).reshape(g, b_lat, 2 * p)
    pos = _grid_pos_embed(l_lat)
    yhy_l, ys5_l, _, _ = _mixers(x_sample, pos, mod_lat, pack(state_s5_re), pack(state_s5_im),
                                 tabs_lat, filt_lat, s5ops, wts, 512)

    n_ctx = b_ctx * l_ctx
    x1_all, h2_all, gates_all = _out_proj(
        x_prompt.reshape(n_ctx, D_MODEL), x_sample.reshape(b_lat * l_lat, D_MODEL), pos,
        yhy_c, yhy_l, ys5_c, ys5_l, mod, wts['w_glu'], wts['b_glu'], wts['out_norm_g'], wts['w_out'],
        wts['ln1_g'], wts['ln1_b'], wts['wr_hi'], wts['wr_lo'], wts['br'], 512)
    y_ctx, y_lat = _moe(h2_all, gates_all, x1_all, mod,
                        wts['w_gate'], wts['w_up'], wts['w_down'], wts['ln2_g'], wts['ln2_b'],
                        n_ctx, l_lat)
    return (y_ctx.reshape(x_prompt.shape), y_lat.reshape(x_sample.shape), new_re, new_im)
```

```python
import functools
import math

import numpy as np
import jax
import jax.numpy as jnp
from jax import lax
from jax.experimental import pallas as pl
from jax.experimental.pallas import tpu as pltpu

F32 = jnp.float32
BF16 = jnp.bfloat16

D_MODEL = 1024
DEPTH = 1
GRID_W = 64
POS_BASE = 10000.0
D_HY = 512
D_S5 = 512
S5_CH = 16
S5_GROUPS = 32
S5_STATE = 64
S5_CHUNK = 16
S5_ROW = S5_CHUNK * S5_CH
HY_BANDS = 16
HY_EMB = 1 + 2 * HY_BANDS
HY_HID = 64
HY_MIN_DECAY = math.log(1e-2) / 1.5
HY_MAX_DECAY = math.log(1e-2) / 0.3
N_EGROUPS = 4
N_EPG = 4
N_EXPERTS = 16
D_EXPERT = 512
LN_EPS = 1e-5
ALPHA = (2.0 * DEPTH) ** 0.25
LANES = 128
S5_GB = LANES // S5_CH
S5OPS_GB = 4
HY_CW = 512
MOE_ST = 512
MOE_SLOTS = 640
MOE_UNIT = 16
MOE_TM = 512
VMEM_LIMIT = 60000 * 1024


def _cparams(sem):
    return pltpu.CompilerParams(dimension_semantics=sem, vmem_limit_bytes=VMEM_LIMIT)


def _split(x):
    hi = x.astype(BF16)
    lo = (x - hi.astype(F32)).astype(BF16)
    return hi, lo


def _dot(a, b):
    return jnp.dot(a, b, preferred_element_type=F32)


def _dot_t(a, b):
    return lax.dot_general(a, b, (((1,), (1,)), ((), ())), preferred_element_type=F32)


def _mm3(a, b):
    ah, al = _split(a)
    bh, bl = _split(b)
    return _dot(ah, bh) + _dot(al, bh) + _dot(ah, bl)


def _dot_hp(a, b):
    return jnp.dot(a, b, preferred_element_type=F32, precision=lax.Precision.HIGHEST)


def _norm(x):
    xc = x - jnp.mean(x, axis=-1, keepdims=True)
    return xc * lax.rsqrt(jnp.mean(xc * xc, axis=-1, keepdims=True) + LN_EPS)


def _rms(y):
    return y * lax.rsqrt(jnp.mean(y * y, axis=-1, keepdims=True) + LN_EPS)


def _ada_kernel(cond_ref, w_ref, b_ref, o_ref):
    c = jax.nn.silu(cond_ref[...])
    o_ref[...] = _mm3(c, w_ref[...]) + b_ref[...]


def _ada(cond, w_ada, b_ada):
    nb = cond.shape[0]
    n = w_ada.shape[1]
    tn = 1024
    return pl.pallas_call(
        _ada_kernel,
        grid=(n // tn,),
        in_specs=[pl.BlockSpec((nb, D_MODEL), lambda j: (0, 0)),
                  pl.BlockSpec((D_MODEL, tn), lambda j: (0, j)),
                  pl.BlockSpec((1, tn), lambda j: (0, j))],
        out_specs=pl.BlockSpec((nb, tn), lambda j: (0, j)),
        out_shape=jax.ShapeDtypeStruct((nb, n), F32),
        compiler_params=_cparams(("arbitrary",)),
        name="ada",
    )(cond, w_ada, b_ada.reshape(1, n))


def _dft_tables(n_tok):
    n = 2 * n_tok
    idx = np.arange(n_tok, dtype=np.int64)
    m = (idx[:, None] * idx[None, :]) % n
    ang = 2.0 * np.pi * m.astype(np.float64) / n
    cm = np.cos(ang)
    sm = -np.sin(ang)
    sm[0, :] = 1.0 - 2.0 * (idx % 2)
    return cm.astype(np.float32), sm.astype(np.float32)


def _filt_kernel(n_tok, z_ref, t_ref, w1_ref, b1_ref, w2_ref, b2_ref, fr_ref, w3f_ref, w3b_ref,
                 dl_ref, cm_ref, sm_ref, a_ref, bz_ref, dd_ref):
    fr = fr_ref[...]
    h = jnp.sin(fr * (_dot_hp(z_ref[...], w1_ref[...]) + b1_ref[...]))
    h = jnp.sin(fr * (_dot_hp(h, w2_ref[...]) + b2_ref[...]))
    decay = jnp.exp(-t_ref[...] * dl_ref[...])
    row = lax.broadcasted_iota(jnp.int32, decay.shape, 0)
    hf = _dot_hp(h, w3f_ref[...]) * decay
    hb = jnp.where(row == 0, 0.0, _dot_hp(h, w3b_ref[...]) * decay)
    p = hf + hb
    q = hf - hb
    k_re = _dot(cm_ref[...], p.astype(BF16))
    k_im = _dot(sm_ref[...], q.astype(BF16))
    sign = jnp.where(row % 2 == 0, 1.0, -1.0)
    nyq = jnp.sum(p * sign, axis=0, keepdims=True)
    inv_n = 1.0 / (2 * n_tok)
    w = jnp.where(row == 0, inv_n, 2.0 * inv_n)
    a_ref[...] = w * k_re
    bz_ref[...] = jnp.where(row == 0, 0.0, w * k_im)
    dd_ref[...] = jnp.where(row == 0, nyq * inv_n, w * k_re)


def _hyena_filters(n_tok, tabs, hy_f_w1, hy_f_b1, hy_f_w2, hy_f_b2, hy_f_w3, hy_freq):
    cm, sm, _ = tabs
    t = jnp.linspace(0.0, 1.0, n_tok, dtype=F32)[:, None]
    wv = 2.0 * math.pi * jnp.arange(n_tok, dtype=F32) / n_tok
    fb = jnp.linspace(1e-4, HY_BANDS - 1, HY_BANDS, dtype=F32)
    ang = wv[:, None] * fb[None, :]
    z = jnp.concatenate([t, jnp.cos(ang), -jnp.sin(ang)], axis=-1)
    z = jnp.pad(z, ((0, 0), (0, LANES - HY_EMB)))
    w1 = jnp.pad(hy_f_w1, ((0, LANES - HY_EMB), (0, 0)))
    deltas = jnp.abs(jnp.linspace(HY_MIN_DECAY, HY_MAX_DECAY, D_HY, dtype=F32))[None, :]
    ncb = D_HY // HY_CW
    full = lambda j: (0, 0)
    out_sd = jax.ShapeDtypeStruct((n_tok, 2 * D_HY), F32)
    mat = pl.BlockSpec((n_tok, n_tok), full, pipeline_mode=pl.Buffered(1))
    return pl.pallas_call(
        functools.partial(_filt_kernel, n_tok),
        grid=(2 * ncb,),
        in_specs=[pl.BlockSpec((n_tok, LANES), full),
                  pl.BlockSpec((n_tok, 1), full),
                  pl.BlockSpec((LANES, HY_HID), full),
                  pl.BlockSpec((1, HY_HID), full),
                  pl.BlockSpec((HY_HID, HY_HID), full),
                  pl.BlockSpec((1, HY_HID), full),
                  pl.BlockSpec((1, HY_HID), full),
                  pl.BlockSpec((HY_HID, HY_CW), lambda j: (0, 2 * ncb * (j // ncb) + j % ncb)),
                  pl.BlockSpec((HY_HID, HY_CW), lambda j: (0, 2 * ncb * (j // ncb) + ncb + j % ncb)),
                  pl.BlockSpec((1, HY_CW), lambda j: (0, j % ncb)),
                  mat, mat],
        out_specs=[pl.BlockSpec((n_tok, HY_CW), lambda j: (0, j))] * 3,
        out_shape=[out_sd] * 3,
        compiler_params=_cparams(("arbitrary",)),
        name=f"filt{n_tok}",
    )(z, t, w1, hy_f_b1.reshape(1, -1), hy_f_w2, hy_f_b2.reshape(1, -1), hy_freq.reshape(1, -1),
      hy_f_w3, hy_f_w3, deltas, cm, sm)


def _hyena_kernel(pv_ref, p1_ref, p2_ref, cwv_ref, cw1_ref, cw2_ref, cbv_ref, cb1_ref, cb2_ref,
                  fbias_ref, cm_ref, sm_ref, st_ref,
                  a0_ref, b0_ref, d0_ref, a1_ref, b1_ref, d1_ref, o_ref):
    n_tok = pv_ref.shape[1]
    row = lax.broadcasted_iota(jnp.int32, (n_tok, pv_ref.shape[2]), 0)

    def short_conv(p_ref, cw_ref, cb_ref):
        p = p_ref[0]
        prev = jnp.where(row == 0, 0.0, pltpu.roll(p, 1, axis=0))
        nxt = jnp.where(row == n_tok - 1, 0.0, pltpu.roll(p, n_tok - 1, axis=0))
        return cb_ref[...] + prev * cw_ref[0:1, :] + p * cw_ref[1:2, :] + nxt * cw_ref[2:3, :]

    cm = cm_ref[...]

    def fftconv(u, a_ref, b_ref, d_ref, skip):
        ub = u.astype(BF16)
        u_re = _dot(cm, ub)
        u_im = _dot(sm_ref[...], ub)
        a, bz, dd = a_ref[...], b_ref[...], d_ref[...]
        y_re = u_re * a - u_im * bz
        y_im = u_re * bz + u_im * dd
        y = _dot(cm, y_re.astype(BF16)) + _dot(st_ref[...], y_im.astype(BF16))
        return y + u * skip

    v = short_conv(pv_ref, cwv_ref, cbv_ref)
    x1 = short_conv(p1_ref, cw1_ref, cb1_ref)
    z = x1 * fftconv(v, a0_ref, b0_ref, d0_ref, fbias_ref[0:1, :])
    x2 = short_conv(p2_ref, cw2_ref, cb2_ref)
    o_ref[0] = x2 * fftconv(z, a1_ref, b1_ref, d1_ref, fbias_ref[1:2, :])


def _hyena(proj_hy, tabs, filt, hy_conv_w, hy_conv_b, hy_fbias):
    bsz, n_tok, _ = proj_hy.shape
    ncb = D_HY // HY_CW
    cm, sm, st = tabs
    fa, fbz, fdd = filt
    cb = hy_conv_b.reshape(1, -1)
    mat = pl.BlockSpec((n_tok, n_tok), lambda b, c: (0, 0), pipeline_mode=pl.Buffered(1))

    def pspec(k):
        return pl.BlockSpec((1, n_tok, HY_CW), lambda b, c: (b, 0, k * ncb + c))

    def cwspec(k):
        return pl.BlockSpec((3, HY_CW), lambda b, c: (0, k * ncb + c))

    def cbspec(k):
        return pl.BlockSpec((1, HY_CW), lambda b, c: (0, k * ncb + c))

    def fspec(o):
        mode = pl.Buffered(1) if ncb == 1 else None
        return pl.BlockSpec((n_tok, HY_CW), lambda b, c: (0, o * ncb + c), pipeline_mode=mode)

    return pl.pallas_call(
        _hyena_kernel,
        grid=(bsz, ncb),
        in_specs=[pspec(0), pspec(1), pspec(2), cwspec(0), cwspec(1), cwspec(2),
                  cbspec(0), cbspec(1), cbspec(2),
                  pl.BlockSpec((2, HY_CW), lambda b, c: (0, c)),
                  mat, mat, mat,
                  fspec(0), fspec(0), fspec(0), fspec(1), fspec(1), fspec(1)],
        out_specs=pl.BlockSpec((1, n_tok, HY_CW), lambda b, c: (b, 0, c)),
        out_shape=jax.ShapeDtypeStruct((bsz, n_tok, D_HY), F32),
        compiler_params=_cparams(("arbitrary", "arbitrary")),
        name=f"hyena{n_tok}",
    )(proj_hy, proj_hy, proj_hy, hy_conv_w, hy_conv_w, hy_conv_w, cb, cb, cb, hy_fbias,
      cm, sm, st, fa, fbz, fdd, fa, fbz, fdd)


def _s5ops_kernel(*refs):
    for g in range(S5OPS_GB):
        _s5ops_group(g, *refs)


def _s5ops_group(g, are_ref, aim_ref, ldt_ref, btr_ref, bti_ref, cre_ref, cim_ref, d_ref,
                 mt_ref, erh_ref, erl_ref, eih_ref, eil_ref, gr_ref, gi_ref, atr_ref, ati_ref,
                 er_ref, ei_ref):
    a_re, a_im = are_ref[g], aim_ref[g]
    dt = jnp.exp(ldt_ref[g])
    mag = jnp.exp(a_re * dt)
    ab_re = mag * jnp.cos(a_im * dt)
    ab_im = mag * jnp.sin(a_im * dt)
    n_re, n_im = ab_re - 1.0, ab_im
    den = a_re * a_re + a_im * a_im
    q_re = (n_re * a_re + n_im * a_im) / den
    q_im = (n_im * a_re - n_re * a_im) / den
    bt_re, bt_im = btr_ref[g], bti_ref[g]
    bb_re = q_re * bt_re - q_im * bt_im
    bb_im = q_re * bt_im + q_im * bt_re
    c_re, c_im = cre_ref[g, 0:S5_CH, :], cim_ref[g, 0:S5_CH, :]
    pw = [(jnp.ones_like(ab_re), jnp.zeros_like(ab_re))]
    for _ in range(S5_CHUNK):
        pr, pi = pw[-1]
        pw.append((pr * ab_re - pi * ab_im, pr * ab_im + pi * ab_re))
    lane = lax.broadcasted_iota(jnp.int32, ab_re.shape, 1)
    fwd = lane < S5_STATE
    for s in range(S5_CHUNK):
        e_re = jnp.where(fwd, pw[S5_CHUNK - 1 - s][0], pw[s][0])
        e_im = jnp.where(fwd, pw[S5_CHUNK - 1 - s][1], pw[s][1])
        er_ref[g, pl.ds(S5_CH * s, S5_CH), :] = e_re * bb_re - e_im * bb_im
        ei_ref[g, pl.ds(S5_CH * s, S5_CH), :] = e_re * bb_im + e_im * bb_re
        g_re = jnp.where(fwd, pw[s + 1][0], pw[S5_CHUNK - s][0])
        g_im = jnp.where(fwd, pw[s + 1][1], pw[S5_CHUNK - s][1])
        gr_ref[g, pl.ds(S5_CH * s, S5_CH), :] = (c_re * g_re - c_im * g_im).astype(BF16)
        gi_ref[g, pl.ds(S5_CH * s, S5_CH), :] = (-(c_re * g_im + c_im * g_re)).astype(BF16)
    atr_ref[g] = pw[S5_CHUNK][0]
    ati_ref[g] = pw[S5_CHUNK][1]
    er, ei = er_ref[g], ei_ref[g]
    erh_ref[g], erl_ref[g] = _split(er)
    eih_ref[g], eil_ref[g] = _split(ei)
    lane2 = lax.broadcasted_iota(jnp.int32, er.shape, 1)
    row2 = lax.broadcasted_iota(jnp.int32, er.shape, 0)
    f2 = lane2 < S5_STATE
    zero = jnp.zeros_like(er)

    def dot_hp_t(a, b):
        return lax.dot_general(a, b, (((1,), (1,)), ((), ())), preferred_element_type=F32,
                               precision=lax.Precision.HIGHEST)

    cp_re, cp_im = cre_ref[g], cim_ref[g]
    kf = dot_hp_t(jnp.where(f2, er, zero), cp_re) - dot_hp_t(jnp.where(f2, ei, zero), cp_im)
    kb = dot_hp_t(jnp.where(f2, zero, er), cp_re) - dot_hp_t(jnp.where(f2, zero, ei), cp_im)
    d_row = d_ref[g]
    steps_per_vreg = LANES // S5_CH
    for half in range(S5_CHUNK // steps_per_vreg):
        acc = zero
        for tt in range(steps_per_vreg):
            t = half * steps_per_vreg + tt
            nf = S5_CH * (S5_CHUNK - 1 - t)
            nb = S5_CH * t
            col_f = jnp.concatenate([kf[nf:], zero[:nf]], axis=0) if nf else kf
            col_b = jnp.concatenate([zero[:nb], kb[:S5_ROW - nb]], axis=0) if nb else kb
            diag = jnp.where((row2 // S5_CH == t) & (row2 % S5_CH == lane2), d_row, 0.0)
            col = col_f + col_b + diag
            r = pltpu.roll(col, S5_CH * tt, axis=1) if tt else col
            acc = jnp.where((lane2 >= S5_CH * tt) & (lane2 < S5_CH * (tt + 1)), r, acc)
        mt_ref[g, :, LANES * half:LANES * (half + 1)] = acc.astype(BF16)


def _s5_operators(s5_a_re, s5_a_im, s5_log_dt, s5_b_re, s5_b_im, s5_c_re, s5_c_im, s5_d):
    g, p, h = S5_GROUPS, S5_STATE, S5_CH
    cat = lambda x: jnp.concatenate([x[0], x[1]], axis=-1)
    a_re = cat(s5_a_re).reshape(g, 1, 2 * p)
    a_im = cat(s5_a_im).reshape(g, 1, 2 * p)
    ldt = cat(jnp.broadcast_to(s5_log_dt[:, :, None], (2, g, p))).reshape(g, 1, 2 * p)
    bt_re = cat(jnp.swapaxes(s5_b_re, -1, -2))
    bt_im = cat(jnp.swapaxes(s5_b_im, -1, -2))
    cpad = lambda c: jnp.pad(jnp.concatenate([c, c], axis=-1), ((0, 0), (0, LANES - h), (0, 0)))
    c_re, c_im = cpad(s5_c_re), cpad(s5_c_im)
    d_row = jnp.pad(s5_d.reshape(g, 1, h), ((0, 0), (0, 0), (0, LANES - h)))
    vec = pl.BlockSpec((S5OPS_GB, 1, 2 * p), lambda i: (i, 0, 0))
    hp = pl.BlockSpec((S5OPS_GB, h, 2 * p), lambda i: (i, 0, 0))
    sq = pl.BlockSpec((S5OPS_GB, LANES, 2 * p), lambda i: (i, 0, 0))
    big = pl.BlockSpec((S5OPS_GB, S5_ROW, 2 * p), lambda i: (i, 0, 0))
    mts = pl.BlockSpec((S5OPS_GB, S5_ROW, S5_ROW), lambda i: (i, 0, 0))
    big_sd = jax.ShapeDtypeStruct((g, S5_ROW, 2 * p), BF16)
    vec_sd = jax.ShapeDtypeStruct((g, 1, 2 * p), F32)
    return pl.pallas_call(
        _s5ops_kernel,
        grid=(g // S5OPS_GB,),
        in_specs=[vec, vec, vec, hp, hp, sq, sq, vec],
        out_specs=[mts, big, big, big, big, big, big, vec, vec],
        out_shape=[jax.ShapeDtypeStruct((g, S5_ROW, S5_ROW), BF16)] + [big_sd] * 6 + [vec_sd, vec_sd],
        scratch_shapes=[pltpu.VMEM((S5OPS_GB, S5_ROW, 2 * p), F32)] * 2,
        compiler_params=_cparams(("arbitrary",)),
        name="s5ops",
    )(a_re, a_im, ldt, bt_re, bt_im, c_re, c_im, d_row)


def _block_transpose(xs):
    n = len(xs)
    lane = lax.broadcasted_iota(jnp.int32, xs[0].shape, 1)
    xs = list(xs)
    d = n // 2
    while d:
        keep = ((lane // S5_CH) & d) == 0
        for i in range(n):
            if i & d:
                continue
            lo, hi = xs[i], xs[i + d]
            xs[i] = jnp.where(keep, lo, pltpu.roll(hi, S5_CH * d, axis=1))
            xs[i + d] = jnp.where(keep, pltpu.roll(lo, LANES - S5_CH * d, axis=1), hi)
        d //= 2
    return xs


def _s5_kernel(bsz, n_chunks, u_ref, mt_ref, erh_ref, erl_ref, eih_ref, eil_ref, gr_ref, gi_ref,
               atr_ref, ati_ref, h0r_ref, h0i_ref, y_ref, fr_ref, fi_ref,
               ua_ref, ub_ref, ya_ref, yb_ref, sr_ref, si_ref, xfr_ref, xfi_ref, xbr_ref, xbi_ref):
    nc = n_chunks
    spv = LANES // S5_CH
    rsub = min(nc, 32)

    def to_chunks(b, carry):
        for half, dst in ((0, ua_ref), (1, ub_ref)):
            for r0 in range(0, nc, rsub):
                xs = [u_ref[b, pl.ds(S5_CHUNK * r0 + half * spv + tt, rsub, stride=S5_CHUNK), :]
                      for tt in range(spv)]
                for k, blk in enumerate(_block_transpose(xs)):
                    dst[k, pl.ds(r0 * bsz + b, rsub, stride=bsz), :] = blk
        return carry

    lax.fori_loop(0, bsz, to_chunks, 0)

    lane = lax.broadcasted_iota(jnp.int32, (bsz, 2 * S5_STATE), 1)
    fwd = lane < S5_STATE
    lane_all = lax.broadcasted_iota(jnp.int32, (bsz * nc, 2 * S5_STATE), 1)
    fwd_all = lane_all < S5_STATE

    def group(k, slot):
        u = jnp.concatenate([ua_ref[k], ub_ref[k]], axis=1)
        uh, ul = _split(u)
        sr_ref[slot] = _dot(uh, erh_ref[k]) + _dot(ul, erh_ref[k]) + _dot(uh, erl_ref[k])
        si_ref[slot] = _dot(uh, eih_ref[k]) + _dot(ul, eih_ref[k]) + _dot(uh, eil_ref[k])
        at_re, at_im = atr_ref[k], ati_ref[k]
        y_intra = _dot(uh, mt_ref[k])

        def step(i, xc):
            x_re, x_im = xc
            rf = pl.ds(pl.multiple_of(i * bsz, bsz), bsz)
            rb = pl.ds(pl.multiple_of((nc - 1 - i) * bsz, bsz), bsz)
            xfr_ref[slot, rf, :] = x_re
            xfi_ref[slot, rf, :] = x_im
            xbr_ref[slot, rb, :] = x_re
            xbi_ref[slot, rb, :] = x_im
            s_re = jnp.where(fwd, sr_ref[slot, rf, :], sr_ref[slot, rb, :])
            s_im = jnp.where(fwd, si_ref[slot, rf, :], si_ref[slot, rb, :])
            return (at_re * x_re - at_im * x_im + s_re, at_re * x_im + at_im * x_re + s_im)

        x_re, x_im = lax.fori_loop(0, nc, step, (h0r_ref[k], h0i_ref[k]), unroll=True)
        fr_ref[k] = x_re
        fi_ref[k] = x_im
        xp_re = jnp.where(fwd_all, xfr_ref[slot], xbr_ref[slot]).astype(BF16)
        xp_im = jnp.where(fwd_all, xfi_ref[slot], xbi_ref[slot]).astype(BF16)
        y = y_intra + _dot_t(xp_re, gr_ref[k]) + _dot_t(xp_im, gi_ref[k])
        ya_ref[k] = y[:, :LANES]
        yb_ref[k] = y[:, LANES:]

    def group_pair(j, carry):
        group(2 * j, 0)
        group(2 * j + 1, 1)
        return carry

    lax.fori_loop(0, S5_GB // 2, group_pair, 0)

    def to_tokens(b, carry):
        for half, src in ((0, ya_ref), (1, yb_ref)):
            for r0 in range(0, nc, rsub):
                ys = [src[k, pl.ds(r0 * bsz + b, rsub, stride=bsz), :] for k in range(S5_GB)]
                for tt, blk in enumerate(_block_transpose(ys)):
                    y_ref[b, pl.ds(S5_CHUNK * r0 + half * spv + tt, rsub, stride=S5_CHUNK), :] = blk
        return carry

    lax.fori_loop(0, bsz, to_tokens, 0)


def _s5(u, ops, h0_re, h0_im):
    bsz, n_tok, _ = u.shape
    g, p = S5_GROUPS, S5_STATE
    nc = n_tok // S5_CHUNK
    rows = nc * bsz
    tok = pl.BlockSpec((bsz, n_tok, LANES), lambda j: (0, 0, j))
    gspec = lambda shape: pl.BlockSpec((S5_GB,) + shape, lambda j: (j, 0, 0))
    op = gspec((S5_ROW, 2 * p))
    return pl.pallas_call(
        functools.partial(_s5_kernel, bsz, nc),
        grid=(g // S5_GB,),
        in_specs=[tok, gspec((S5_ROW, S5_ROW)), op, op, op, op, op, op,
                  gspec((1, 2 * p)), gspec((1, 2 * p)), gspec((bsz, 2 * p)), gspec((bsz, 2 * p))],
        out_specs=[tok, gspec((bsz, 2 * p)), gspec((bsz, 2 * p))],
        out_shape=[jax.ShapeDtypeStruct((bsz, n_tok, D_S5), F32),
                   jax.ShapeDtypeStruct((g, bsz, 2 * p), F32),
                   jax.ShapeDtypeStruct((g, bsz, 2 * p), F32)],
        scratch_shapes=([pltpu.VMEM((S5_GB, rows, LANES), F32)] * 4
                        + [pltpu.VMEM((2, rows, 2 * p), F32)] * 6),
        compiler_params=_cparams(("arbitrary",)),
        name=f"s5_{n_tok}",
    )(u, *ops, h0_re, h0_im)


def _in_kernel(has_pos, *refs):
    if has_pos:
        x_ref, pos_ref, mod_ref, w_ref, hy_ref, s5_ref = refs
        x = x_ref[0] + pos_ref[...]
    else:
        x_ref, mod_ref, w_ref, hy_ref, s5_ref = refs
        x = x_ref[0]
    sh1 = mod_ref[0, :, 0:D_MODEL]
    sc1 = mod_ref[0, :, D_MODEL:2 * D_MODEL]
    h = _norm(x) * (1.0 + sc1) + sh1
    proj = _dot(h.astype(BF16), w_ref[...])
    hy_ref[0] = proj[:, :3 * D_HY]
    s5_ref[0] = proj[:, 3 * D_HY:]


def _in_proj(x3, pos, mod3, w_in_bf, tm):
    nb, lt, _ = x3.shape
    has_pos = pos is not None
    per_batch = mod3.shape[0] > 1
    midx = (lambda b, i: (b, 0, 0)) if per_batch else (lambda b, i: (0, 0, 0))
    in_specs = [pl.BlockSpec((1, tm, D_MODEL), lambda b, i: (b, i, 0))]
    args = [x3]
    if has_pos:
        in_specs.append(pl.BlockSpec((tm, D_MODEL), lambda b, i: (i, 0)))
        args.append(pos)
    in_specs += [pl.BlockSpec((1, 1, 6 * D_MODEL), midx),
                 pl.BlockSpec((D_MODEL, 3 * D_HY + D_S5), lambda b, i: (0, 0))]
    args += [mod3, w_in_bf]
    return pl.pallas_call(
        functools.partial(_in_kernel, has_pos),
        grid=(nb, lt // tm),
        in_specs=in_specs,
        out_specs=[pl.BlockSpec((1, tm, 3 * D_HY), lambda b, i: (b, i, 0)),
                   pl.BlockSpec((1, tm, D_S5), lambda b, i: (b, i, 0))],
        out_shape=[jax.ShapeDtypeStruct((nb, lt, 3 * D_HY), F32),
                   jax.ShapeDtypeStruct((nb, lt, D_S5), F32)],
        compiler_params=_cparams(("arbitrary", "arbitrary")),
        name=f"in_proj{nb}",
    )(*args)


def _route(logits):
    lane = lax.broadcasted_iota(jnp.int32, logits.shape, 1)
    lane_f = lane.astype(F32)
    neg = -jnp.inf
    big = float(LANES)
    m1 = (lane >= N_EXPERTS) & (lane < N_EXPERTS + N_EGROUPS)
    l1 = jnp.where(m1, logits, neg)
    top1 = jnp.max(l1, axis=-1, keepdims=True)
    grp = jnp.min(jnp.where(l1 == top1, lane_f, big), axis=-1, keepdims=True) - float(N_EXPERTS)
    den = jnp.sum(jnp.where(m1, jnp.exp(logits - top1), 0.0), axis=-1, keepdims=True)
    p_grp = 1.0 / den
    lo = grp * float(N_EPG)
    m2 = (lane_f >= lo) & (lane_f < lo + float(N_EPG))
    l2 = jnp.where(m2, logits, neg)
    v1 = jnp.max(l2, axis=-1, keepdims=True)
    i1 = jnp.min(jnp.where(l2 == v1, lane_f, big), axis=-1, keepdims=True)
    l2b = jnp.where(lane_f == i1, neg, l2)
    v2 = jnp.max(l2b, axis=-1, keepdims=True)
    i2 = jnp.min(jnp.where(l2b == v2, lane_f, big), axis=-1, keepdims=True)
    e = jnp.exp(v2 - v1)
    w1 = 1.0 / (1.0 + e)
    w2 = e / (1.0 + e)
    gates = jnp.where(lane_f == i1, w1 * p_grp, 0.0) + jnp.where(lane_f == i2, w2 * p_grp, 0.0)
    return jnp.where(lane_f == grp + float(N_EXPERTS), 1.0, gates)


def _out_kernel(n_ctx_blocks, xc_ref, xl_ref, pos_ref, yhyc_ref, yhyl_ref, ys5c_ref, ys5l_ref, mod_ref,
                wglu_ref, bglu_ref, ong_ref, wout_ref, ln1g_ref, ln1b_ref, wrh_ref, wrl_ref, br_ref,
                x1_ref, h2_ref, gate_ref):
    is_ctx = pl.program_id(0) < n_ctx_blocks
    x = jnp.where(is_ctx, xc_ref[...], xl_ref[...] + pos_ref[...])
    y = jnp.where(is_ctx, ys5c_ref[...], ys5l_ref[...])
    y_hy = jnp.where(is_ctx, yhyc_ref[...], yhyl_ref[...])
    s5 = jax.nn.gelu(y) * jax.nn.sigmoid(_dot(y.astype(BF16), wglu_ref[...]) + bglu_ref[...])
    m_hy = _rms(y_hy) * ong_ref[:, 0:D_HY]
    m_s5 = _rms(s5) * ong_ref[:, D_HY:]
    o = (_dot(m_hy.astype(BF16), wout_ref[0:D_HY, :]) + _dot(m_s5.astype(BF16), wout_ref[D_HY:, :]))
    g1 = mod_ref[0, :, 2 * D_MODEL:3 * D_MODEL]
    sh2 = mod_ref[0, :, 3 * D_MODEL:4 * D_MODEL]
    sc2 = mod_ref[0, :, 4 * D_MODEL:5 * D_MODEL]
    x1 = _norm(ALPHA * x + g1 * o) * ln1g_ref[...] + ln1b_ref[...]
    x1_ref[...] = x1
    h2 = _norm(x1) * (1.0 + sc2) + sh2
    h2_ref[...] = h2.astype(BF16)
    hh, hl = _split(h2)
    logits = (_dot(hh, wrh_ref[...]) + _dot(hl, wrh_ref[...]) + _dot(hh, wrl_ref[...]) + br_ref[...])
    gate_ref[...] = _route(logits)


def _out_proj(xc, xl, pos, yhy_c, yhy_l, ys5_c, ys5_l, mod, wglu_bf, bglu, ong, wout_bf, ln1g, ln1b,
              wr_hi, wr_lo, br, tm):
    n_ctx, n_lat = xc.shape[0], xl.shape[0]
    l_lat = pos.shape[0]
    ncb, nlb, npb = n_ctx // tm, n_lat // tm, l_lat // tm
    ctx = lambda w: pl.BlockSpec((tm, w), lambda i: (jnp.minimum(i, ncb - 1), 0))
    lat = lambda w: pl.BlockSpec((tm, w), lambda i: (jnp.maximum(i - ncb, 0), 0))
    full = lambda shape: pl.BlockSpec(shape, lambda i: (0,) * len(shape))
    out = lambda w: pl.BlockSpec((tm, w), lambda i: (i, 0))
    mod_idx = lambda i: (jnp.where(i < ncb, 0, 1 + jnp.maximum(i - ncb, 0) // npb), 0, 0)
    n_all = n_ctx + n_lat
    return pl.pallas_call(
        functools.partial(_out_kernel, ncb),
        grid=(ncb + nlb,),
        in_specs=[ctx(D_MODEL), lat(D_MODEL),
                  pl.BlockSpec((tm, D_MODEL), lambda i: (jnp.maximum(i - ncb, 0) % npb, 0)),
                  ctx(D_HY), lat(D_HY), ctx(D_S5), lat(D_S5),
                  pl.BlockSpec((1, 1, 6 * D_MODEL), mod_idx),
                  full((D_S5, D_S5)), full((1, D_S5)), full((1, D_MODEL)), full((D_MODEL, D_MODEL)),
                  full((1, D_MODEL)), full((1, D_MODEL)), full((D_MODEL, LANES)), full((D_MODEL, LANES)),
                  full((1, LANES))],
        out_specs=[out(D_MODEL), out(D_MODEL), out(LANES)],
        out_shape=[jax.ShapeDtypeStruct((n_all, D_MODEL), F32),
                   jax.ShapeDtypeStruct((n_all, D_MODEL), BF16),
                   jax.ShapeDtypeStruct((n_all, LANES), F32)],
        compiler_params=_cparams(("arbitrary",)),
        name="out_proj",
    )(xc, xl, pos, yhy_c, yhy_l, ys5_c, ys5_l, mod.reshape(mod.shape[0], 1, 6 * D_MODEL),
      wglu_bf, bglu, ong, wout_bf, ln1g, ln1b, wr_hi, wr_lo, br)


def _perm_t(gates, loc_ref, s):
    n = gates.shape[0]
    lane = lax.broadcasted_iota(jnp.int32, gates.shape, 1)
    oh = jnp.where((lane >= N_EXPERTS) & (lane < N_EXPERTS + N_EGROUPS), gates, 0.0)
    r = lax.broadcasted_iota(jnp.int32, (n, n), 0)
    c = lax.broadcasted_iota(jnp.int32, (n, n), 1)
    earlier = jnp.where(c < r, 1.0, 0.0).astype(BF16)
    cum = _dot(earlier, oh.astype(BF16))
    rank = jnp.sum(cum * oh, axis=-1, keepdims=True)
    lane1 = lax.broadcasted_iota(jnp.int32, (1, LANES), 1)
    locv = jnp.zeros((1, LANES), F32)
    for grp in range(N_EGROUPS):
        locv = jnp.where(lane1 == N_EXPERTS + grp, loc_ref[N_EGROUPS * s + grp].astype(F32), locv)
    dest = rank + jnp.sum(oh * locv, axis=-1, keepdims=True)
    slot = lax.broadcasted_iota(jnp.int32, (n, MOE_SLOTS), 1).astype(F32)
    return jnp.where(slot == dest, 1.0, 0.0)


def _segment_copies(s, loc_ref, len_ref, off_ref, make):
    for grp in range(N_EGROUPS):
        loc = loc_ref[N_EGROUPS * s + grp]
        off = off_ref[N_EGROUPS * s + grp]
        n_units = len_ref[N_EGROUPS * s + grp] // MOE_UNIT

        def body(i, carry):
            make(pl.multiple_of(loc + MOE_UNIT * i, MOE_UNIT), pl.multiple_of(off + MOE_UNIT * i, MOE_UNIT))
            return carry

        lax.fori_loop(0, n_units, body, 0)


def _moe_sort_kernel(loc_ref, len_ref, off_ref, h_ref, gate_ref, xs_in, gs_in, xs_hbm, gs_hbm,
                     xs_v, gs_v, sem):
    del xs_in, gs_in
    s = pl.program_id(0)
    slot = s % 2
    gates = gate_ref[...]
    p = _perm_t(gates, loc_ref, s).T.astype(BF16)
    xs_v[slot] = _dot(p, h_ref[...]).astype(BF16)
    g_hi = gates.astype(BF16)
    r1 = gates - g_hi.astype(F32)
    g_mid = r1.astype(BF16)
    g_lo = (r1 - g_mid.astype(F32)).astype(BF16)
    gs_v[slot] = _dot(p, g_hi) + _dot(p, g_mid) + _dot(p, g_lo)

    def copies(buf):
        def x_copy(lr, gr):
            return pltpu.make_async_copy(xs_v.at[buf, pl.ds(lr, MOE_UNIT), :],
                                         xs_hbm.at[pl.ds(gr, MOE_UNIT), :], sem.at[0, buf])

        def g_copy(lr, gr):
            return pltpu.make_async_copy(gs_v.at[buf, pl.ds(lr, MOE_UNIT), :],
                                         gs_hbm.at[pl.ds(gr, MOE_UNIT), :], sem.at[1, buf])

        def start(lr, gr):
            x_copy(lr, gr).start()
            g_copy(lr, gr).start()

        def wait(lr, gr):
            x_copy(lr, gr).wait()
            g_copy(lr, gr).wait()

        return start, wait

    _segment_copies(s, loc_ref, len_ref, off_ref, copies(slot)[0])

    @pl.when(s > 0)
    def _():
        _segment_copies(s - 1, loc_ref, len_ref, off_ref, copies(1 - slot)[1])

    @pl.when(s == pl.num_programs(0) - 1)
    def _():
        _segment_copies(s, loc_ref, len_ref, off_ref, copies(slot)[1])


def _moe_expert_kernel(bg_ref, nb_ref, xs_ref, gs_ref, wg_ref, wu_ref, wd_ref, o_ref):
    i = pl.program_id(0)

    @pl.when(i < nb_ref[0])
    def _():
        grp = bg_ref[i]
        x = xs_ref[...]
        gates = gs_ref[...]
        lane = lax.broadcasted_iota(jnp.int32, gates.shape, 1)
        acc = jnp.zeros(o_ref.shape, F32)
        for e in range(N_EPG):
            a = _dot(x, wg_ref[e])
            u = _dot(x, wu_ref[e])
            ge = jnp.sum(jnp.where(lane == N_EPG * grp + e, gates, 0.0), axis=-1, keepdims=True)
            hid = jax.nn.silu(a) * u * ge
            acc = acc + _dot(hid.astype(BF16), wd_ref[e])
        o_ref[...] = acc.astype(BF16)

    @pl.when(i >= nb_ref[0])
    def _():
        o_ref[...] = jnp.zeros_like(o_ref)


def _moe_combine_kernel(loc_ref, len_ref, off_ref, gate_ref, x1_ref, mod_ref, ln2g_ref, ln2b_ref, o_hbm,
                        ctx_ref, lat_ref, o_v, sem, *, n_ctx_tiles):
    s = pl.program_id(0)
    slot = s % 2

    def copies(buf):
        def o_copy(lr, gr):
            return pltpu.make_async_copy(o_hbm.at[pl.ds(gr, MOE_UNIT), :],
                                         o_v.at[buf, pl.ds(lr, MOE_UNIT), :], sem.at[buf])

        return (lambda lr, gr: o_copy(lr, gr).start()), (lambda lr, gr: o_copy(lr, gr).wait())

    @pl.when(s == 0)
    def _():
        o_v[...] = jnp.zeros_like(o_v)
        _segment_copies(s, loc_ref, len_ref, off_ref, copies(slot)[0])

    @pl.when(s + 1 < pl.num_programs(0))
    def _():
        _segment_copies(s + 1, loc_ref, len_ref, off_ref, copies(1 - slot)[0])

    pt = _perm_t(gate_ref[...], loc_ref, s).astype(BF16)
    _segment_copies(s, loc_ref, len_ref, off_ref, copies(slot)[1])
    f = _dot(pt, o_v[slot])
    g2 = mod_ref[0, :, 5 * D_MODEL:6 * D_MODEL]
    x2 = _norm(ALPHA * x1_ref[...] + g2 * f) * ln2g_ref[...] + ln2b_ref[...]

    @pl.when(s < n_ctx_tiles)
    def _():
        ctx_ref[...] = x2

    @pl.when(s >= n_ctx_tiles)
    def _():
        lat_ref[...] = x2


def _moe_plan(gates_all, n_blocks):
    n_tiles = gates_all.shape[0] // MOE_ST
    oh = gates_all[:, N_EXPERTS:N_EXPERTS + N_EGROUPS]
    cnt = jnp.sum(oh.reshape(n_tiles, MOE_ST, N_EGROUPS), axis=1).astype(jnp.int32)
    len16 = ((cnt + MOE_UNIT - 1) // MOE_UNIT) * MOE_UNIT
    loc = jnp.cumsum(len16, axis=1) - len16
    rows_g = jnp.sum(len16, axis=0)
    reg_g = ((rows_g + MOE_TM - 1) // MOE_TM) * MOE_TM
    reg_start = jnp.cumsum(reg_g) - reg_g
    off = reg_start[None, :] + jnp.cumsum(len16, axis=0) - len16
    blk_end = jnp.cumsum(reg_g // MOE_TM)
    bi = jnp.arange(n_blocks, dtype=jnp.int32)
    blk_group = jnp.minimum(jnp.sum((bi[:, None] >= blk_end[None, :]).astype(jnp.int32), axis=1),
                            N_EGROUPS - 1)
    flat = lambda a: a.reshape(-1).astype(jnp.int32)
    return flat(loc), flat(len16), flat(off), blk_group.astype(jnp.int32), blk_end[-1:].astype(jnp.int32)


def _moe(h2_all, gates_all, x1_all, mod, wg_bf, wu_bf, wd_bf, ln2g, ln2b, n_ctx, tokens_per_mod_row):
    n_tok = h2_all.shape[0]
    n_tiles = n_tok // MOE_ST
    n_ctx_tiles = n_ctx // MOE_ST
    max_rows = n_tok + n_tiles * N_EGROUPS * (MOE_UNIT - 1) + N_EGROUPS * (MOE_TM - 1)
    n_blocks = -(-max_rows // MOE_TM)
    n_rows = n_blocks * MOE_TM
    loc, len16, off, blk_group, n_used = _moe_plan(gates_all, n_blocks)

    tile = lambda w: pl.BlockSpec((MOE_ST, w), lambda s, *_: (s, 0))
    anyspec = pl.BlockSpec(memory_space=pl.ANY)
    xs, gs = pl.pallas_call(
        _moe_sort_kernel,
        grid_spec=pltpu.PrefetchScalarGridSpec(
            num_scalar_prefetch=3, grid=(n_tiles,),
            in_specs=[tile(D_MODEL), tile(LANES), anyspec, anyspec],
            out_specs=[anyspec, anyspec],
            scratch_shapes=[pltpu.VMEM((2, MOE_SLOTS, D_MODEL), BF16), pltpu.VMEM((2, MOE_SLOTS, LANES), F32),
                            pltpu.SemaphoreType.DMA((2, 2))]),
        out_shape=[jax.ShapeDtypeStruct((n_rows, D_MODEL), BF16),
                   jax.ShapeDtypeStruct((n_rows, LANES), F32)],
        input_output_aliases={5: 0, 6: 1},
        compiler_params=_cparams(("arbitrary",)),
        name="moe_sort",
    )(loc, len16, off, h2_all, gates_all, jnp.zeros((n_rows, D_MODEL), BF16), jnp.zeros((n_rows, LANES), F32))

    blk = lambda w: pl.BlockSpec((MOE_TM, w), lambda i, bg, nb: (jnp.minimum(i, nb[0] - 1), 0))
    wspec = lambda a, b: pl.BlockSpec((N_EPG, a, b), lambda i, bg, nb: (bg[i], 0, 0))
    o_sorted = pl.pallas_call(
        _moe_expert_kernel,
        grid_spec=pltpu.PrefetchScalarGridSpec(
            num_scalar_prefetch=2, grid=(n_blocks,),
            in_specs=[blk(D_MODEL), blk(LANES), wspec(D_MODEL, D_EXPERT), wspec(D_MODEL, D_EXPERT),
                      wspec(D_EXPERT, D_MODEL)],
            out_specs=pl.BlockSpec((MOE_TM, D_MODEL), lambda i, bg, nb: (i, 0))),
        out_shape=jax.ShapeDtypeStruct((n_rows, D_MODEL), BF16),
        compiler_params=_cparams(("arbitrary",)),
        name="moe_experts",
    )(blk_group, n_used, xs, gs, wg_bf, wu_bf, wd_bf)

    lat_per_row = tokens_per_mod_row // MOE_ST

    def mod_idx(s, *_):
        return (jnp.where(s < n_ctx_tiles, 0, 1 + (s - n_ctx_tiles) // lat_per_row), 0, 0)

    vec = pl.BlockSpec((1, D_MODEL), lambda s, *_: (0, 0))
    return pl.pallas_call(
        functools.partial(_moe_combine_kernel, n_ctx_tiles=n_ctx_tiles),
        grid_spec=pltpu.PrefetchScalarGridSpec(
            num_scalar_prefetch=3, grid=(n_tiles,),
            in_specs=[tile(LANES), tile(D_MODEL), pl.BlockSpec((1, 1, 6 * D_MODEL), mod_idx), vec, vec,
                      anyspec],
            out_specs=[pl.BlockSpec((MOE_ST, D_MODEL), lambda s, *_: (jnp.minimum(s, n_ctx_tiles - 1), 0)),
                       pl.BlockSpec((MOE_ST, D_MODEL), lambda s, *_: (jnp.maximum(s - n_ctx_tiles, 0), 0))],
            scratch_shapes=[pltpu.VMEM((2, MOE_SLOTS, D_MODEL), BF16), pltpu.SemaphoreType.DMA((2,))]),
        out_shape=[jax.ShapeDtypeStruct((n_ctx, D_MODEL), F32),
                   jax.ShapeDtypeStruct((n_tok - n_ctx, D_MODEL), F32)],
        compiler_params=_cparams(("arbitrary",)),
        name="moe_combine",
    )(loc, len16, off, gates_all, x1_all, mod.reshape(mod.shape[0], 1, 6 * D_MODEL), ln2g, ln2b, o_sorted)


def _grid_pos_embed(n_tokens):
    rows = n_tokens // GRID_W
    row = np.repeat(np.arange(rows, dtype=np.float64), GRID_W)
    col = np.tile(np.arange(GRID_W, dtype=np.float64), rows)
    quarter = D_MODEL // 4
    omega = 1.0 / (POS_BASE ** (np.arange(quarter, dtype=np.float64) / quarter))
    er = row[:, None] * omega
    ec = col[:, None] * omega
    return jnp.asarray(np.concatenate([np.sin(er), np.cos(er), np.sin(ec), np.cos(ec)], axis=-1), F32)


def _tables(n_tok):
    cm, sm = _dft_tables(n_tok)
    return tuple(jnp.asarray(t).astype(BF16) for t in (cm, sm, np.ascontiguousarray(sm.T)))


def _mixers(x, pos, mod3, h0_re, h0_im, tabs, filt, s5ops, wts, tm):
    bsz, n_tok, _ = x.shape
    shared = mod3.shape[0] == 1
    x3 = x.reshape(1, bsz * n_tok, D_MODEL) if shared else x
    proj_hy, u_s5 = _in_proj(x3, pos, mod3, wts['w_in'], tm)
    y_hy = _hyena(proj_hy.reshape(bsz, n_tok, 3 * D_HY), tabs, filt,
                  wts['hy_conv_w'], wts['hy_conv_b'], wts['hy_fbias'])
    y_s5, f_re, f_im = _s5(u_s5.reshape(bsz, n_tok, D_S5), s5ops, h0_re, h0_im)
    return y_hy.reshape(bsz * n_tok, D_HY), y_s5.reshape(bsz * n_tok, D_S5), f_re, f_im


def kernel(x_prompt, x_sample, state_s5_re, state_s5_im, c, c_ctx, w_ada, b_ada, w_in, hy_conv_w, hy_conv_b, hy_f_w1, hy_f_b1, hy_f_w2, hy_f_b2, hy_f_w3, hy_freq, hy_fbias, s5_a_re, s5_a_im, s5_log_dt, s5_b_re, s5_b_im, s5_c_re, s5_c_im, s5_d, s5_w_glu, s5_b_glu, out_norm_g, w_out, ln1_g, ln1_b, moe_w_r1, moe_b_r1, moe_w_r2, moe_b_r2, moe_w_gate, moe_w_up, moe_w_down, ln2_g, ln2_b):
    b_ctx, l_ctx, _ = x_prompt.shape
    b_lat, l_lat, _ = x_sample.shape
    g, p = S5_GROUPS, S5_STATE
    assert w_ada.shape[0] == 1, "single-layer trunk"
    l = 0

    nrow = 16
    cond = jnp.concatenate([c_ctx[None, :], c, jnp.zeros((nrow - 1 - b_lat, D_MODEL), F32)], axis=0)
    mod = _ada(cond, w_ada[l], b_ada[l])
    mod_ctx = mod[0:1].reshape(1, 1, 6 * D_MODEL)
    mod_lat = mod[1:1 + b_lat].reshape(b_lat, 1, 6 * D_MODEL)

    wr = jnp.concatenate([moe_w_r2[l].transpose(1, 0, 2).reshape(D_MODEL, N_EXPERTS), moe_w_r1[l]], axis=1)
    wr = jnp.pad(wr, ((0, 0), (0, LANES - wr.shape[1])))
    br = jnp.concatenate([moe_b_r2[l].reshape(-1), moe_b_r1[l]])
    br = jnp.pad(br, (0, LANES - br.shape[0])).reshape(1, LANES)
    wr_hi, wr_lo = _split(wr)

    wts = {
        'w_in': w_in[l].astype(BF16), 'hy_conv_w': hy_conv_w[l], 'hy_conv_b': hy_conv_b[l],
        'hy_fbias': hy_fbias[l], 'w_glu': s5_w_glu[l].astype(BF16), 'b_glu': s5_b_glu[l].reshape(1, -1),
        'out_norm_g': out_norm_g[l].reshape(1, -1), 'w_out': w_out[l].astype(BF16),
        'ln1_g': ln1_g[l].reshape(1, -1), 'ln1_b': ln1_b[l].reshape(1, -1),
        'wr_hi': wr_hi, 'wr_lo': wr_lo, 'br': br,
        'w_gate': moe_w_gate[l].astype(BF16), 'w_up': moe_w_up[l].astype(BF16),
        'w_down': moe_w_down[l].astype(BF16),
        'ln2_g': ln2_g[l].reshape(1, -1), 'ln2_b': ln2_b[l].reshape(1, -1),
    }

    s5ops = _s5_operators(s5_a_re[l], s5_a_im[l], s5_log_dt[l], s5_b_re[l], s5_b_im[l],
                          s5_c_re[l], s5_c_im[l], s5_d[l])
    tabs_ctx = _tables(l_ctx)
    tabs_lat = _tables(l_lat)
    filt_args = (hy_f_w1[l], hy_f_b1[l], hy_f_w2[l], hy_f_b2[l], hy_f_w3[l], hy_freq[l])
    filt_ctx = _hyena_filters(l_ctx, tabs_ctx, *filt_args)
    filt_lat = _hyena_filters(l_lat, tabs_lat, *filt_args)

    zero = jnp.zeros((g, b_ctx, 2 * p), F32)
    yhy_c, ys5_c, f_re, f_im = _mixers(x_prompt, None, mod_ctx, zero, zero, tabs_ctx, filt_ctx, s5ops, wts, 512)
    unpack = lambda f: f.reshape(g, b_ctx, 2, p).transpose(1, 2, 0, 3)[:, None]
    new_re, new_im = unpack(f_re), unpack(f_im)

    pack = lambda s: s[:, l].transpose(2, 0, 1, 3).reshape(g, b_lat, 2 * p)
    pos = _grid_pos_embed(l_lat)
    yhy_l, ys5_l, _, _ = _mixers(x_sample, pos, mod_lat, pack(state_s5_re), pack(state_s5_im),
                                 tabs_lat, filt_lat, s5ops, wts, 512)

    n_ctx = b_ctx * l_ctx
    x1_all, h2_all, gates_all = _out_proj(
        x_prompt.reshape(n_ctx, D_MODEL), x_sample.reshape(b_lat * l_lat, D_MODEL), pos,
        yhy_c, yhy_l, ys5_c, ys5_l, mod, wts['w_glu'], wts['b_glu'], wts['out_norm_g'], wts['w_out'],
        wts['ln1_g'], wts['ln1_b'], wts['wr_hi'], wts['wr_lo'], wts['br'], 512)
    y_ctx, y_lat = _moe(h2_all, gates_all, x1_all, mod,
                        wts['w_gate'], wts['w_up'], wts['w_down'], wts['ln2_g'], wts['ln2_b'],
                        n_ctx, l_lat)
    return (y_ctx.reshape(x_prompt.shape), y_lat.reshape(x_sample.shape), new_re, new_im)
```

```python
import functools
import math

import numpy as np
import jax
import jax.numpy as jnp
from jax import lax
from jax.experimental import pallas as pl
from jax.experimental.pallas import tpu as pltpu

F32 = jnp.float32
BF16 = jnp.bfloat16

D_MODEL = 1024
DEPTH = 1
GRID_W = 64
POS_BASE = 10000.0
D_HY = 512
D_S5 = 512
S5_CH = 16
S5_GROUPS = 32
S5_STATE = 64
S5_CHUNK = 16
S5_ROW = S5_CHUNK * S5_CH
HY_BANDS = 16
HY_EMB = 1 + 2 * HY_BANDS
HY_HID = 64
HY_MIN_DECAY = math.log(1e-2) / 1.5
HY_MAX_DECAY = math.log(1e-2) / 0.3
N_EGROUPS = 4
N_EPG = 4
N_EXPERTS = 16
D_EXPERT = 512
LN_EPS = 1e-5
ALPHA = (2.0 * DEPTH) ** 0.25
LANES = 128
S5_GB = LANES // S5_CH
S5OPS_GB = 4
HY_CW = 512
MOE_ST = 512
MOE_SLOTS = 640
MOE_UNIT = 16
MOE_TM = 512
VMEM_LIMIT = 60000 * 1024


def _cparams(sem):
    return pltpu.CompilerParams(dimension_semantics=sem, vmem_limit_bytes=VMEM_LIMIT)


def _split(x):
    hi = x.astype(BF16)
    lo = (x - hi.astype(F32)).astype(BF16)
    return hi, lo


def _dot(a, b):
    return jnp.dot(a, b, preferred_element_type=F32)


def _dot_t(a, b):
    return lax.dot_general(a, b, (((1,), (1,)), ((), ())), preferred_element_type=F32)


def _mm3(a, b):
    ah, al = _split(a)
    bh, bl = _split(b)
    return _dot(ah, bh) + _dot(al, bh) + _dot(ah, bl)


def _dot_hp(a, b):
    return jnp.dot(a, b, preferred_element_type=F32, precision=lax.Precision.HIGHEST)


def _norm(x):
    xc = x - jnp.mean(x, axis=-1, keepdims=True)
    return xc * lax.rsqrt(jnp.mean(xc * xc, axis=-1, keepdims=True) + LN_EPS)


def _rms(y):
    return y * lax.rsqrt(jnp.mean(y * y, axis=-1, keepdims=True) + LN_EPS)


def _ada_kernel(cond_ref, w_ref, b_ref, o_ref):
    c = jax.nn.silu(cond_ref[...])
    o_ref[...] = _mm3(c, w_ref[...]) + b_ref[...]


def _ada(cond, w_ada, b_ada):
    nb = cond.shape[0]
    n = w_ada.shape[1]
    tn = 1024
    return pl.pallas_call(
        _ada_kernel,
        grid=(n // tn,),
        in_specs=[pl.BlockSpec((nb, D_MODEL), lambda j: (0, 0)),
                  pl.BlockSpec((D_MODEL, tn), lambda j: (0, j)),
                  pl.BlockSpec((1, tn), lambda j: (0, j))],
        out_specs=pl.BlockSpec((nb, tn), lambda j: (0, j)),
        out_shape=jax.ShapeDtypeStruct((nb, n), F32),
        compiler_params=_cparams(("arbitrary",)),
        name="ada",
    )(cond, w_ada, b_ada.reshape(1, n))


def _dft_tables(n_half):
    n = 2 * n_half
    idx = np.arange(n_half, dtype=np.int64)
    m = (idx[:, None] * idx[None, :]) % n
    ang = 2.0 * np.pi * m.astype(np.float64) / n
    cm = np.cos(ang)
    sm = -np.sin(ang)
    sm[0, :] = 1.0 - 2.0 * (idx % 2)
    return cm.astype(np.float32), sm.astype(np.float32)


def _tables(n_tok):
    n_half = n_tok // 2
    cm, sm = _dft_tables(n_half)
    mats = tuple(jnp.asarray(t).astype(BF16) for t in (cm, sm, np.ascontiguousarray(sm.T)))
    ang = np.pi * np.arange(n_half, dtype=np.float64) / n_tok
    tw = [np.broadcast_to(v[:, None], (n_half, HY_CW)).astype(np.float32) for v in (np.cos(ang), -np.sin(ang))]
    return mats + (jnp.asarray(tw[0]), jnp.asarray(tw[1]))


def _put_cols(ref, x):
    for j in range(ref.shape[0]):
        ref[j] = x[:, LANES * j:LANES * (j + 1)]


def _get_cols(ref):
    return jnp.concatenate([ref[j] for j in range(ref.shape[0])], axis=1)


def _get_parity(ref, parity):
    n_half = ref.shape[1] // 2
    return jnp.concatenate([ref[j, pl.ds(parity, n_half, stride=2), :] for j in range(ref.shape[0])], axis=1)


def _put_parity(ref, parity, x):
    n_half = ref.shape[1] // 2
    for j in range(ref.shape[0]):
        ref[j, pl.ds(parity, n_half, stride=2), :] = x[:, LANES * j:LANES * (j + 1)]


def _rfft_packed(x_ref, cm, sm, tw_re, tw_im, row0):
    xe = _get_parity(x_ref, 0).astype(BF16)
    xo = _get_parity(x_ref, 1).astype(BF16)
    e_re, e_im = _dot(cm, xe), _dot(sm, xe)
    o_re, o_im = _dot(cm, xo), _dot(sm, xo)
    t_re = tw_re * o_re - tw_im * o_im
    t_im = tw_re * o_im + tw_im * o_re
    a_im = jnp.where(row0, e_im, e_im + t_im)
    b_im = jnp.where(row0, -o_im, t_im - e_im)
    return e_re + t_re, a_im, e_re - t_re, b_im


def _irfft_packed(y_ref, ya_re, ya_im, yb_re, yb_im, cm, st, tw_re, tw_im, row0):
    p_e = ya_re + yb_re
    q_e = jnp.where(row0, ya_im, ya_im - yb_im)
    _put_parity(y_ref, 0, _dot(cm, p_e.astype(BF16)) + _dot(st, q_e.astype(BF16)))
    ra_re = ya_re * tw_re + ya_im * tw_im
    ra_im = ya_im * tw_re - ya_re * tw_im
    rb_re = yb_im * tw_im - yb_re * tw_re
    rb_im = -(yb_re * tw_im + yb_im * tw_re)
    p_o = ra_re + rb_re
    q_o = jnp.where(row0, -yb_im, ra_im - rb_im)
    _put_parity(y_ref, 1, _dot(cm, p_o.astype(BF16)) + _dot(st, q_o.astype(BF16)))


def _filt_kernel(n_tok, z_ref, t_ref, w1_ref, b1_ref, w2_ref, b2_ref, fr_ref, w3f_ref, w3b_ref,
                 dl_ref, cm_ref, sm_ref, twr_ref, twi_ref, kar_ref, kai_ref, kbr_ref, kbi_ref, p_ref, q_ref):
    fr = fr_ref[...]
    h = jnp.sin(fr * (_dot_hp(z_ref[...], w1_ref[...]) + b1_ref[...]))
    h = jnp.sin(fr * (_dot_hp(h, w2_ref[...]) + b2_ref[...]))
    decay = jnp.exp(-t_ref[...] * dl_ref[...])
    row = lax.broadcasted_iota(jnp.int32, decay.shape, 0)
    hf = _dot_hp(h, w3f_ref[...]) * decay
    hb = jnp.where(row == 0, 0.0, _dot_hp(h, w3b_ref[...]) * decay)
    _put_cols(p_ref, hf + hb)
    _put_cols(q_ref, hf - hb)
    cm, sm, tw_re, tw_im = cm_ref[...], sm_ref[...], twr_ref[...], twi_ref[...]
    row0 = lax.broadcasted_iota(jnp.int32, tw_re.shape, 0) == 0
    pa_re, pa_im, pb_re, _ = _rfft_packed(p_ref, cm, sm, tw_re, tw_im, row0)
    _, qa_im, _, qb_im = _rfft_packed(q_ref, cm, sm, tw_re, tw_im, row0)
    inv_n = 1.0 / (2 * n_tok)
    w_re = jnp.where(row0, inv_n, 2.0 * inv_n)
    kar_ref[...] = w_re * pa_re
    kbr_ref[...] = w_re * pb_re
    kai_ref[...] = (2.0 * inv_n) * jnp.where(row0, pa_im, qa_im)
    kbi_ref[...] = (2.0 * inv_n) * qb_im


def _hyena_filters(n_tok, tabs, hy_f_w1, hy_f_b1, hy_f_w2, hy_f_b2, hy_f_w3, hy_freq):
    cm, sm, _, tw_re, tw_im = tabs
    n_half = n_tok // 2
    t = jnp.linspace(0.0, 1.0, n_tok, dtype=F32)[:, None]
    wv = 2.0 * math.pi * jnp.arange(n_tok, dtype=F32) / n_tok
    fb = jnp.linspace(1e-4, HY_BANDS - 1, HY_BANDS, dtype=F32)
    ang = wv[:, None] * fb[None, :]
    z = jnp.concatenate([t, jnp.cos(ang), -jnp.sin(ang)], axis=-1)
    z = jnp.pad(z, ((0, 0), (0, LANES - HY_EMB)))
    w1 = jnp.pad(hy_f_w1, ((0, LANES - HY_EMB), (0, 0)))
    deltas = jnp.abs(jnp.linspace(HY_MIN_DECAY, HY_MAX_DECAY, D_HY, dtype=F32))[None, :]
    ncb = D_HY // HY_CW
    full = lambda j: (0, 0)
    out_sd = jax.ShapeDtypeStruct((n_half, 2 * D_HY), F32)
    mat = pl.BlockSpec((n_half, n_half), full, pipeline_mode=pl.Buffered(1))
    twb = pl.BlockSpec((n_half, HY_CW), full, pipeline_mode=pl.Buffered(1))
    return pl.pallas_call(
        functools.partial(_filt_kernel, n_tok),
        grid=(2 * ncb,),
        in_specs=[pl.BlockSpec((n_tok, LANES), full),
                  pl.BlockSpec((n_tok, 1), full),
                  pl.BlockSpec((LANES, HY_HID), full),
                  pl.BlockSpec((1, HY_HID), full),
                  pl.BlockSpec((HY_HID, HY_HID), full),
                  pl.BlockSpec((1, HY_HID), full),
                  pl.BlockSpec((1, HY_HID), full),
                  pl.BlockSpec((HY_HID, HY_CW), lambda j: (0, 2 * ncb * (j // ncb) + j % ncb)),
                  pl.BlockSpec((HY_HID, HY_CW), lambda j: (0, 2 * ncb * (j // ncb) + ncb + j % ncb)),
                  pl.BlockSpec((1, HY_CW), lambda j: (0, j % ncb)),
                  mat, mat, twb, twb],
        out_specs=[pl.BlockSpec((n_half, HY_CW), lambda j: (0, j))] * 4,
        out_shape=[out_sd] * 4,
        scratch_shapes=[pltpu.VMEM((HY_CW // LANES, n_tok, LANES), F32)] * 2,
        compiler_params=_cparams(("arbitrary",)),
        name=f"filt{n_tok}",
    )(z, t, w1, hy_f_b1.reshape(1, -1), hy_f_w2, hy_f_b2.reshape(1, -1), hy_freq.reshape(1, -1),
      hy_f_w3, hy_f_w3, deltas, cm, sm, tw_re, tw_im)


def _hyena_kernel(pv_ref, p1_ref, p2_ref, cwv_ref, cw1_ref, cw2_ref, cbv_ref, cb1_ref, cb2_ref,
                  fbias_ref, cm_ref, sm_ref, st_ref, twr_ref, twi_ref,
                  kar0_ref, kai0_ref, kbr0_ref, kbi0_ref, kar1_ref, kai1_ref, kbr1_ref, kbi1_ref,
                  o_ref, u_ref, y_ref):
    n_tok = pv_ref.shape[1]
    row = lax.broadcasted_iota(jnp.int32, (n_tok, pv_ref.shape[2]), 0)

    def short_conv(p_ref, cw_ref, cb_ref):
        p = p_ref[0]
        prev = jnp.where(row == 0, 0.0, pltpu.roll(p, 1, axis=0))
        nxt = jnp.where(row == n_tok - 1, 0.0, pltpu.roll(p, n_tok - 1, axis=0))
        return cb_ref[...] + prev * cw_ref[0:1, :] + p * cw_ref[1:2, :] + nxt * cw_ref[2:3, :]

    cm, sm, st, tw_re, tw_im = cm_ref[...], sm_ref[...], st_ref[...], twr_ref[...], twi_ref[...]
    row0 = lax.broadcasted_iota(jnp.int32, tw_re.shape, 0) == 0

    def fftconv(u, kar_ref, kai_ref, kbr_ref, kbi_ref, skip):
        _put_cols(u_ref, u)
        ua_re, ua_im, ub_re, ub_im = _rfft_packed(u_ref, cm, sm, tw_re, tw_im, row0)
        ka_re, ka_im, kb_re, kb_im = kar_ref[...], kai_ref[...], kbr_ref[...], kbi_ref[...]
        kaz = jnp.where(row0, 0.0, ka_im)
        kbz = jnp.where(row0, 0.0, kb_im)
        ya_re = ua_re * ka_re - ua_im * kaz
        yb_re = ub_re * kb_re - ub_im * kbz
        h_re = ua_im[0:1] * ka_im[0:1] - ub_im[0:1] * kb_im[0:1]
        h_im = ua_im[0:1] * kb_im[0:1] + ub_im[0:1] * ka_im[0:1]
        ya_im = jnp.where(row0, h_re, ua_re * ka_im + ua_im * ka_re)
        yb_im = jnp.where(row0, h_im, ub_re * kb_im + ub_im * kb_re)
        _irfft_packed(y_ref, ya_re, ya_im, yb_re, yb_im, cm, st, tw_re, tw_im, row0)
        return _get_cols(y_ref) + u * skip

    v = short_conv(pv_ref, cwv_ref, cbv_ref)
    x1 = short_conv(p1_ref, cw1_ref, cb1_ref)
    z = x1 * fftconv(v, kar0_ref, kai0_ref, kbr0_ref, kbi0_ref, fbias_ref[0:1, :])
    x2 = short_conv(p2_ref, cw2_ref, cb2_ref)
    o_ref[0] = x2 * fftconv(z, kar1_ref, kai1_ref, kbr1_ref, kbi1_ref, fbias_ref[1:2, :])


def _hyena(proj_hy, tabs, filt, hy_conv_w, hy_conv_b, hy_fbias):
    bsz, n_tok, _ = proj_hy.shape
    n_half = n_tok // 2
    ncb = D_HY // HY_CW
    cm, sm, st, tw_re, tw_im = tabs
    cb = hy_conv_b.reshape(1, -1)
    const = lambda shape: pl.BlockSpec(shape, lambda b, c: (0, 0), pipeline_mode=pl.Buffered(1))
    mat = const((n_half, n_half))
    twb = const((n_half, HY_CW))

    def pspec(k):
        return pl.BlockSpec((1, n_tok, HY_CW), lambda b, c: (b, 0, k * ncb + c))

    def cwspec(k):
        return pl.BlockSpec((3, HY_CW), lambda b, c: (0, k * ncb + c))

    def cbspec(k):
        return pl.BlockSpec((1, HY_CW), lambda b, c: (0, k * ncb + c))

    def fspec(o):
        mode = pl.Buffered(1) if ncb == 1 else None
        return pl.BlockSpec((n_half, HY_CW), lambda b, c: (0, o * ncb + c), pipeline_mode=mode)

    return pl.pallas_call(
        _hyena_kernel,
        grid=(bsz, ncb),
        in_specs=[pspec(0), pspec(1), pspec(2), cwspec(0), cwspec(1), cwspec(2),
                  cbspec(0), cbspec(1), cbspec(2),
                  pl.BlockSpec((2, HY_CW), lambda b, c: (0, c)),
                  mat, mat, mat, twb, twb] + [fspec(0)] * 4 + [fspec(1)] * 4,
        out_specs=pl.BlockSpec((1, n_tok, HY_CW), lambda b, c: (b, 0, c)),
        out_shape=jax.ShapeDtypeStruct((bsz, n_tok, D_HY), F32),
        scratch_shapes=[pltpu.VMEM((HY_CW // LANES, n_tok, LANES), F32)] * 2,
        compiler_params=_cparams(("arbitrary", "arbitrary")),
        name=f"hyena{n_tok}",
    )(proj_hy, proj_hy, proj_hy, hy_conv_w, hy_conv_w, hy_conv_w, cb, cb, cb, hy_fbias,
      cm, sm, st, tw_re, tw_im, *filt, *filt)


def _s5ops_kernel(*refs):
    for g in range(S5OPS_GB):
        _s5ops_group(g, *refs)


def _s5ops_group(g, are_ref, aim_ref, ldt_ref, btr_ref, bti_ref, cre_ref, cim_ref, d_ref,
                 mt_ref, erh_ref, erl_ref, eih_ref, eil_ref, gr_ref, gi_ref, atr_ref, ati_ref,
                 er_ref, ei_ref):
    a_re, a_im = are_ref[g], aim_ref[g]
    dt = jnp.exp(ldt_ref[g])
    mag = jnp.exp(a_re * dt)
    ab_re = mag * jnp.cos(a_im * dt)
    ab_im = mag * jnp.sin(a_im * dt)
    n_re, n_im = ab_re - 1.0, ab_im
    den = a_re * a_re + a_im * a_im
    q_re = (n_re * a_re + n_im * a_im) / den
    q_im = (n_im * a_re - n_re * a_im) / den
    bt_re, bt_im = btr_ref[g], bti_ref[g]
    bb_re = q_re * bt_re - q_im * bt_im
    bb_im = q_re * bt_im + q_im * bt_re
    c_re, c_im = cre_ref[g, 0:S5_CH, :], cim_ref[g, 0:S5_CH, :]
    pw = [(jnp.ones_like(ab_re), jnp.zeros_like(ab_re))]
    for _ in range(S5_CHUNK):
        pr, pi = pw[-1]
        pw.append((pr * ab_re - pi * ab_im, pr * ab_im + pi * ab_re))
    lane = lax.broadcasted_iota(jnp.int32, ab_re.shape, 1)
    fwd = lane < S5_STATE
    for s in range(S5_CHUNK):
        e_re = jnp.where(fwd, pw[S5_CHUNK - 1 - s][0], pw[s][0])
        e_im = jnp.where(fwd, pw[S5_CHUNK - 1 - s][1], pw[s][1])
        er_ref[g, pl.ds(S5_CH * s, S5_CH), :] = e_re * bb_re - e_im * bb_im
        ei_ref[g, pl.ds(S5_CH * s, S5_CH), :] = e_re * bb_im + e_im * bb_re
        g_re = jnp.where(fwd, pw[s + 1][0], pw[S5_CHUNK - s][0])
        g_im = jnp.where(fwd, pw[s + 1][1], pw[S5_CHUNK - s][1])
        gr_ref[g, pl.ds(S5_CH * s, S5_CH), :] = (c_re * g_re - c_im * g_im).astype(BF16)
        gi_ref[g, pl.ds(S5_CH * s, S5_CH), :] = (-(c_re * g_im + c_im * g_re)).astype(BF16)
    atr_ref[g] = pw[S5_CHUNK][0]
    ati_ref[g] = pw[S5_CHUNK][1]
    er, ei = er_ref[g], ei_ref[g]
    erh_ref[g], erl_ref[g] = _split(er)
    eih_ref[g], eil_ref[g] = _split(ei)
    lane2 = lax.broadcasted_iota(jnp.int32, er.shape, 1)
    row2 = lax.broadcasted_iota(jnp.int32, er.shape, 0)
    f2 = lane2 < S5_STATE
    zero = jnp.zeros_like(er)

    def dot_hp_t(a, b):
        return lax.dot_general(a, b, (((1,), (1,)), ((), ())), preferred_element_type=F32,
                               precision=lax.Precision.HIGHEST)

    cp_re, cp_im = cre_ref[g], cim_ref[g]
    kf = dot_hp_t(jnp.where(f2, er, zero), cp_re) - dot_hp_t(jnp.where(f2, ei, zero), cp_im)
    kb = dot_hp_t(jnp.where(f2, zero, er), cp_re) - dot_hp_t(jnp.where(f2, zero, ei), cp_im)
    d_row = d_ref[g]
    steps_per_vreg = LANES // S5_CH
    for half in range(S5_CHUNK // steps_per_vreg):
        acc = zero
        for tt in range(steps_per_vreg):
            t = half * steps_per_vreg + tt
            nf = S5_CH * (S5_CHUNK - 1 - t)
            nb = S5_CH * t
            col_f = jnp.concatenate([kf[nf:], zero[:nf]], axis=0) if nf else kf
            col_b = jnp.concatenate([zero[:nb], kb[:S5_ROW - nb]], axis=0) if nb else kb
            diag = jnp.where((row2 // S5_CH == t) & (row2 % S5_CH == lane2), d_row, 0.0)
            col = col_f + col_b + diag
            r = pltpu.roll(col, S5_CH * tt, axis=1) if tt else col
            acc = jnp.where((lane2 >= S5_CH * tt) & (lane2 < S5_CH * (tt + 1)), r, acc)
        mt_ref[g, :, LANES * half:LANES * (half + 1)] = acc.astype(BF16)


def _s5_operators(s5_a_re, s5_a_im, s5_log_dt, s5_b_re, s5_b_im, s5_c_re, s5_c_im, s5_d):
    g, p, h = S5_GROUPS, S5_STATE, S5_CH
    cat = lambda x: jnp.concatenate([x[0], x[1]], axis=-1)
    a_re = cat(s5_a_re).reshape(g, 1, 2 * p)
    a_im = cat(s5_a_im).reshape(g, 1, 2 * p)
    ldt = cat(jnp.broadcast_to(s5_log_dt[:, :, None], (2, g, p))).reshape(g, 1, 2 * p)
    bt_re = cat(jnp.swapaxes(s5_b_re, -1, -2))
    bt_im = cat(jnp.swapaxes(s5_b_im, -1, -2))
    cpad = lambda c: jnp.pad(jnp.concatenate([c, c], axis=-1), ((0, 0), (0, LANES - h), (0, 0)))
    c_re, c_im = cpad(s5_c_re), cpad(s5_c_im)
    d_row = jnp.pad(s5_d.reshape(g, 1, h), ((0, 0), (0, 0), (0, LANES - h)))
    vec = pl.BlockSpec((S5OPS_GB, 1, 2 * p), lambda i: (i, 0, 0))
    hp = pl.BlockSpec((S5OPS_GB, h, 2 * p), lambda i: (i, 0, 0))
    sq = pl.BlockSpec((S5OPS_GB, LANES, 2 * p), lambda i: (i, 0, 0))
    big = pl.BlockSpec((S5OPS_GB, S5_ROW, 2 * p), lambda i: (i, 0, 0))
    mts = pl.BlockSpec((S5OPS_GB, S5_ROW, S5_ROW), lambda i: (i, 0, 0))
    big_sd = jax.ShapeDtypeStruct((g, S5_ROW, 2 * p), BF16)
    vec_sd = jax.ShapeDtypeStruct((g, 1, 2 * p), F32)
    return pl.pallas_call(
        _s5ops_kernel,
        grid=(g // S5OPS_GB,),
        in_specs=[vec, vec, vec, hp, hp, sq, sq, vec],
        out_specs=[mts, big, big, big, big, big, big, vec, vec],
        out_shape=[jax.ShapeDtypeStruct((g, S5_ROW, S5_ROW), BF16)] + [big_sd] * 6 + [vec_sd, vec_sd],
        scratch_shapes=[pltpu.VMEM((S5OPS_GB, S5_ROW, 2 * p), F32)] * 2,
        compiler_params=_cparams(("arbitrary",)),
        name="s5ops",
    )(a_re, a_im, ldt, bt_re, bt_im, c_re, c_im, d_row)


def _block_transpose(xs):
    n = len(xs)
    lane = lax.broadcasted_iota(jnp.int32, xs[0].shape, 1)
    xs = list(xs)
    d = n // 2
    while d:
        keep = ((lane // S5_CH) & d) == 0
        for i in range(n):
            if i & d:
                continue
            lo, hi = xs[i], xs[i + d]
            xs[i] = jnp.where(keep, lo, pltpu.roll(hi, S5_CH * d, axis=1))
            xs[i + d] = jnp.where(keep, pltpu.roll(lo, LANES - S5_CH * d, axis=1), hi)
        d //= 2
    return xs


def _s5_kernel(bsz, n_chunks, u_ref, mt_ref, erh_ref, erl_ref, eih_ref, eil_ref, gr_ref, gi_ref,
               atr_ref, ati_ref, h0r_ref, h0i_ref, y_ref, fr_ref, fi_ref,
               ua_ref, ub_ref, ya_ref, yb_ref, sr_ref, si_ref, xfr_ref, xfi_ref, xbr_ref, xbi_ref):
    nc = n_chunks
    spv = LANES // S5_CH
    rsub = min(nc, 32)

    def to_chunks(b, carry):
        for half, dst in ((0, ua_ref), (1, ub_ref)):
            for r0 in range(0, nc, rsub):
                xs = [u_ref[b, pl.ds(S5_CHUNK * r0 + half * spv + tt, rsub, stride=S5_CHUNK), :]
                      for tt in range(spv)]
                for k, blk in enumerate(_block_transpose(xs)):
                    dst[k, pl.ds(r0 * bsz + b, rsub, stride=bsz), :] = blk
        return carry

    lax.fori_loop(0, bsz, to_chunks, 0)

    lane = lax.broadcasted_iota(jnp.int32, (bsz, 2 * S5_STATE), 1)
    fwd = lane < S5_STATE
    lane_all = lax.broadcasted_iota(jnp.int32, (bsz * nc, 2 * S5_STATE), 1)
    fwd_all = lane_all < S5_STATE

    def group(k, slot):
        u = jnp.concatenate([ua_ref[k], ub_ref[k]], axis=1)
        uh, ul = _split(u)
        sr_ref[slot] = _dot(uh, erh_ref[k]) + _dot(ul, erh_ref[k]) + _dot(uh, erl_ref[k])
        si_ref[slot] = _dot(uh, eih_ref[k]) + _dot(ul, eih_ref[k]) + _dot(uh, eil_ref[k])
        at_re, at_im = atr_ref[k], ati_ref[k]
        y_intra = _dot(uh, mt_ref[k])

        def step(i, xc):
            x_re, x_im = xc
            rf = pl.ds(pl.multiple_of(i * bsz, bsz), bsz)
            rb = pl.ds(pl.multiple_of((nc - 1 - i) * bsz, bsz), bsz)
            xfr_ref[slot, rf, :] = x_re
            xfi_ref[slot, rf, :] = x_im
            xbr_ref[slot, rb, :] = x_re
            xbi_ref[slot, rb, :] = x_im
            s_re = jnp.where(fwd, sr_ref[slot, rf, :], sr_ref[slot, rb, :])
            s_im = jnp.where(fwd, si_ref[slot, rf, :], si_ref[slot, rb, :])
            return (at_re * x_re - at_im * x_im + s_re, at_re * x_im + at_im * x_re + s_im)

        x_re, x_im = lax.fori_loop(0, nc, step, (h0r_ref[k], h0i_ref[k]), unroll=True)
        fr_ref[k] = x_re
        fi_ref[k] = x_im
        xp_re = jnp.where(fwd_all, xfr_ref[slot], xbr_ref[slot]).astype(BF16)
        xp_im = jnp.where(fwd_all, xfi_ref[slot], xbi_ref[slot]).astype(BF16)
        y = y_intra + _dot_t(xp_re, gr_ref[k]) + _dot_t(xp_im, gi_ref[k])
        ya_ref[k] = y[:, :LANES]
        yb_ref[k] = y[:, LANES:]

    def group_pair(j, carry):
        group(2 * j, 0)
        group(2 * j + 1, 1)
        return carry

    lax.fori_loop(0, S5_GB // 2, group_pair, 0)

    def to_tokens(b, carry):
        for half, src in ((0, ya_ref), (1, yb_ref)):
            for r0 in range(0, nc, rsub):
                ys = [src[k, pl.ds(r0 * bsz + b, rsub, stride=bsz), :] for k in range(S5_GB)]
                for tt, blk in enumerate(_block_transpose(ys)):
                    y_ref[b, pl.ds(S5_CHUNK * r0 + half * spv + tt, rsub, stride=S5_CHUNK), :] = blk
        return carry

    lax.fori_loop(0, bsz, to_tokens, 0)


def _s5(u, ops, h0_re, h0_im):
    bsz, n_tok, _ = u.shape
    g, p = S5_GROUPS, S5_STATE
    nc = n_tok // S5_CHUNK
    rows = nc * bsz
    tok = pl.BlockSpec((bsz, n_tok, LANES), lambda j: (0, 0, j))
    gspec = lambda shape: pl.BlockSpec((S5_GB,) + shape, lambda j: (j, 0, 0))
    op = gspec((S5_ROW, 2 * p))
    return pl.pallas_call(
        functools.partial(_s5_kernel, bsz, nc),
        grid=(g // S5_GB,),
        in_specs=[tok, gspec((S5_ROW, S5_ROW)), op, op, op, op, op, op,
                  gspec((1, 2 * p)), gspec((1, 2 * p)), gspec((bsz, 2 * p)), gspec((bsz, 2 * p))],
        out_specs=[tok, gspec((bsz, 2 * p)), gspec((bsz, 2 * p))],
        out_shape=[jax.ShapeDtypeStruct((bsz, n_tok, D_S5), F32),
                   jax.ShapeDtypeStruct((g, bsz, 2 * p), F32),
                   jax.ShapeDtypeStruct((g, bsz, 2 * p), F32)],
        scratch_shapes=([pltpu.VMEM((S5_GB, rows, LANES), F32)] * 4
                        + [pltpu.VMEM((2, rows, 2 * p), F32)] * 6),
        compiler_params=_cparams(("arbitrary",)),
        name=f"s5_{n_tok}",
    )(u, *ops, h0_re, h0_im)


def _in_kernel(has_pos, *refs):
    if has_pos:
        x_ref, pos_ref, mod_ref, w_ref, hy_ref, s5_ref = refs
        x = x_ref[0] + pos_ref[...]
    else:
        x_ref, mod_ref, w_ref, hy_ref, s5_ref = refs
        x = x_ref[0]
    sh1 = mod_ref[0, :, 0:D_MODEL]
    sc1 = mod_ref[0, :, D_MODEL:2 * D_MODEL]
    h = _norm(x) * (1.0 + sc1) + sh1
    proj = _dot(h.astype(BF16), w_ref[...])
    hy_ref[0] = proj[:, :3 * D_HY]
    s5_ref[0] = proj[:, 3 * D_HY:]


def _in_proj(x3, pos, mod3, w_in_bf, tm):
    nb, lt, _ = x3.shape
    has_pos = pos is not None
    per_batch = mod3.shape[0] > 1
    midx = (lambda b, i: (b, 0, 0)) if per_batch else (lambda b, i: (0, 0, 0))
    in_specs = [pl.BlockSpec((1, tm, D_MODEL), lambda b, i: (b, i, 0))]
    args = [x3]
    if has_pos:
        in_specs.append(pl.BlockSpec((tm, D_MODEL), lambda b, i: (i, 0)))
        args.append(pos)
    in_specs += [pl.BlockSpec((1, 1, 6 * D_MODEL), midx),
                 pl.BlockSpec((D_MODEL, 3 * D_HY + D_S5), lambda b, i: (0, 0))]
    args += [mod3, w_in_bf]
    return pl.pallas_call(
        functools.partial(_in_kernel, has_pos),
        grid=(nb, lt // tm),
        in_specs=in_specs,
        out_specs=[pl.BlockSpec((1, tm, 3 * D_HY), lambda b, i: (b, i, 0)),
                   pl.BlockSpec((1, tm, D_S5), lambda b, i: (b, i, 0))],
        out_shape=[jax.ShapeDtypeStruct((nb, lt, 3 * D_HY), F32),
                   jax.ShapeDtypeStruct((nb, lt, D_S5), F32)],
        compiler_params=_cparams(("arbitrary", "arbitrary")),
        name=f"in_proj{nb}",
    )(*args)


def _route(logits):
    lane = lax.broadcasted_iota(jnp.int32, logits.shape, 1)
    lane_f = lane.astype(F32)
    neg = -jnp.inf
    big = float(LANES)
    m1 = (lane >= N_EXPERTS) & (lane < N_EXPERTS + N_EGROUPS)
    l1 = jnp.where(m1, logits, neg)
    top1 = jnp.max(l1, axis=-1, keepdims=True)
    grp = jnp.min(jnp.where(l1 == top1, lane_f, big), axis=-1, keepdims=True) - float(N_EXPERTS)
    den = jnp.sum(jnp.where(m1, jnp.exp(logits - top1), 0.0), axis=-1, keepdims=True)
    p_grp = 1.0 / den
    lo = grp * float(N_EPG)
    m2 = (lane_f >= lo) & (lane_f < lo + float(N_EPG))
    l2 = jnp.where(m2, logits, neg)
    v1 = jnp.max(l2, axis=-1, keepdims=True)
    i1 = jnp.min(jnp.where(l2 == v1, lane_f, big), axis=-1, keepdims=True)
    l2b = jnp.where(lane_f == i1, neg, l2)
    v2 = jnp.max(l2b, axis=-1, keepdims=True)
    i2 = jnp.min(jnp.where(l2b == v2, lane_f, big), axis=-1, keepdims=True)
    e = jnp.exp(v2 - v1)
    w1 = 1.0 / (1.0 + e)
    w2 = e / (1.0 + e)
    gates = jnp.where(lane_f == i1, w1 * p_grp, 0.0) + jnp.where(lane_f == i2, w2 * p_grp, 0.0)
    return jnp.where(lane_f == grp + float(N_EXPERTS), 1.0, gates)


def _out_kernel(n_ctx_blocks, xc_ref, xl_ref, pos_ref, yhyc_ref, yhyl_ref, ys5c_ref, ys5l_ref, mod_ref,
                wglu_ref, bglu_ref, ong_ref, wout_ref, ln1g_ref, ln1b_ref, wrh_ref, wrl_ref, br_ref,
                x1_ref, h2_ref, gate_ref):
    is_ctx = pl.program_id(0) < n_ctx_blocks
    x = jnp.where(is_ctx, xc_ref[...], xl_ref[...] + pos_ref[...])
    y = jnp.where(is_ctx, ys5c_ref[...], ys5l_ref[...])
    y_hy = jnp.where(is_ctx, yhyc_ref[...], yhyl_ref[...])
    s5 = jax.nn.gelu(y) * jax.nn.sigmoid(_dot(y.astype(BF16), wglu_ref[...]) + bglu_ref[...])
    m_hy = _rms(y_hy) * ong_ref[:, 0:D_HY]
    m_s5 = _rms(s5) * ong_ref[:, D_HY:]
    o = (_dot(m_hy.astype(BF16), wout_ref[0:D_HY, :]) + _dot(m_s5.astype(BF16), wout_ref[D_HY:, :]))
    g1 = mod_ref[0, :, 2 * D_MODEL:3 * D_MODEL]
    sh2 = mod_ref[0, :, 3 * D_MODEL:4 * D_MODEL]
    sc2 = mod_ref[0, :, 4 * D_MODEL:5 * D_MODEL]
    x1 = _norm(ALPHA * x + g1 * o) * ln1g_ref[...] + ln1b_ref[...]
    x1_ref[...] = x1
    h2 = _norm(x1) * (1.0 + sc2) + sh2
    h2_ref[...] = h2.astype(BF16)
    hh, hl = _split(h2)
    logits = (_dot(hh, wrh_ref[...]) + _dot(hl, wrh_ref[...]) + _dot(hh, wrl_ref[...]) + br_ref[...])
    gate_ref[...] = _route(logits)


def _out_proj(xc, xl, pos, yhy_c, yhy_l, ys5_c, ys5_l, mod, wglu_bf, bglu, ong, wout_bf, ln1g, ln1b,
              wr_hi, wr_lo, br, tm):
    n_ctx, n_lat = xc.shape[0], xl.shape[0]
    l_lat = pos.shape[0]
    ncb, nlb, npb = n_ctx // tm, n_lat // tm, l_lat // tm
    ctx = lambda w: pl.BlockSpec((tm, w), lambda i: (jnp.minimum(i, ncb - 1), 0))
    lat = lambda w: pl.BlockSpec((tm, w), lambda i: (jnp.maximum(i - ncb, 0), 0))
    full = lambda shape: pl.BlockSpec(shape, lambda i: (0,) * len(shape))
    out = lambda w: pl.BlockSpec((tm, w), lambda i: (i, 0))
    mod_idx = lambda i: (jnp.where(i < ncb, 0, 1 + jnp.maximum(i - ncb, 0) // npb), 0, 0)
    n_all = n_ctx + n_lat
    return pl.pallas_call(
        functools.partial(_out_kernel, ncb),
        grid=(ncb + nlb,),
        in_specs=[ctx(D_MODEL), lat(D_MODEL),
                  pl.BlockSpec((tm, D_MODEL), lambda i: (jnp.maximum(i - ncb, 0) % npb, 0)),
                  ctx(D_HY), lat(D_HY), ctx(D_S5), lat(D_S5),
                  pl.BlockSpec((1, 1, 6 * D_MODEL), mod_idx),
                  full((D_S5, D_S5)), full((1, D_S5)), full((1, D_MODEL)), full((D_MODEL, D_MODEL)),
                  full((1, D_MODEL)), full((1, D_MODEL)), full((D_MODEL, LANES)), full((D_MODEL, LANES)),
                  full((1, LANES))],
        out_specs=[out(D_MODEL), out(D_MODEL), out(LANES)],
        out_shape=[jax.ShapeDtypeStruct((n_all, D_MODEL), F32),
                   jax.ShapeDtypeStruct((n_all, D_MODEL), BF16),
                   jax.ShapeDtypeStruct((n_all, LANES), F32)],
        compiler_params=_cparams(("arbitrary",)),
        name="out_proj",
    )(xc, xl, pos, yhy_c, yhy_l, ys5_c, ys5_l, mod.reshape(mod.shape[0], 1, 6 * D_MODEL),
      wglu_bf, bglu, ong, wout_bf, ln1g, ln1b, wr_hi, wr_lo, br)


def _perm_t(gates, loc_ref, s):
    n = gates.shape[0]
    lane = lax.broadcasted_iota(jnp.int32, gates.shape, 1)
    oh = jnp.where((lane >= N_EXPERTS) & (lane < N_EXPERTS + N_EGROUPS), gates, 0.0)
    r = lax.broadcasted_iota(jnp.int32, (n, n), 0)
    c = lax.broadcasted_iota(jnp.int32, (n, n), 1)
    earlier = jnp.where(c < r, 1.0, 0.0).astype(BF16)
    cum = _dot(earlier, oh.astype(BF16))
    rank = jnp.sum(cum * oh, axis=-1, keepdims=True)
    lane1 = lax.broadcasted_iota(jnp.int32, (1, LANES), 1)
    locv = jnp.zeros((1, LANES), F32)
    for grp in range(N_EGROUPS):
        locv = jnp.where(lane1 == N_EXPERTS + grp, loc_ref[N_EGROUPS * s + grp].astype(F32), locv)
    dest = rank + jnp.sum(oh * locv, axis=-1, keepdims=True)
    slot = lax.broadcasted_iota(jnp.int32, (n, MOE_SLOTS), 1).astype(F32)
    return jnp.where(slot == dest, 1.0, 0.0)


def _segment_copies(s, loc_ref, len_ref, off_ref, make):
    for grp in range(N_EGROUPS):
        loc = loc_ref[N_EGROUPS * s + grp]
        off = off_ref[N_EGROUPS * s + grp]
        n_units = len_ref[N_EGROUPS * s + grp] // MOE_UNIT

        def body(i, carry):
            make(pl.multiple_of(loc + MOE_UNIT * i, MOE_UNIT), pl.multiple_of(off + MOE_UNIT * i, MOE_UNIT))
            return carry

        lax.fori_loop(0, n_units, body, 0)


def _moe_sort_kernel(loc_ref, len_ref, off_ref, h_ref, gate_ref, xs_in, gs_in, xs_hbm, gs_hbm,
                     xs_v, gs_v, sem):
    del xs_in, gs_in
    s = pl.program_id(0)
    slot = s % 2
    gates = gate_ref[...]
    p = _perm_t(gates, loc_ref, s).T.astype(BF16)
    xs_v[slot] = _dot(p, h_ref[...]).astype(BF16)
    g_hi = gates.astype(BF16)
    r1 = gates - g_hi.astype(F32)
    g_mid = r1.astype(BF16)
    g_lo = (r1 - g_mid.astype(F32)).astype(BF16)
    gs_v[slot] = _dot(p, g_hi) + _dot(p, g_mid) + _dot(p, g_lo)

    def copies(buf):
        def x_copy(lr, gr):
            return pltpu.make_async_copy(xs_v.at[buf, pl.ds(lr, MOE_UNIT), :],
                                         xs_hbm.at[pl.ds(gr, MOE_UNIT), :], sem.at[0, buf])

        def g_copy(lr, gr):
            return pltpu.make_async_copy(gs_v.at[buf, pl.ds(lr, MOE_UNIT), :],
                                         gs_hbm.at[pl.ds(gr, MOE_UNIT), :], sem.at[1, buf])

        def start(lr, gr):
            x_copy(lr, gr).start()
            g_copy(lr, gr).start()

        def wait(lr, gr):
            x_copy(lr, gr).wait()
            g_copy(lr, gr).wait()

        return start, wait

    _segment_copies(s, loc_ref, len_ref, off_ref, copies(slot)[0])

    @pl.when(s > 0)
    def _():
        _segment_copies(s - 1, loc_ref, len_ref, off_ref, copies(1 - slot)[1])

    @pl.when(s == pl.num_programs(0) - 1)
    def _():
        _segment_copies(s, loc_ref, len_ref, off_ref, copies(slot)[1])


def _moe_expert_kernel(bg_ref, nb_ref, xs_ref, gs_ref, wg_ref, wu_ref, wd_ref, o_ref):
    i = pl.program_id(0)

    @pl.when(i < nb_ref[0])
    def _():
        grp = bg_ref[i]
        x = xs_ref[...]
        gates = gs_ref[...]
        lane = lax.broadcasted_iota(jnp.int32, gates.shape, 1)
        acc = jnp.zeros(o_ref.shape, F32)
        for e in range(N_EPG):
            a = _dot(x, wg_ref[e])
            u = _dot(x, wu_ref[e])
            ge = jnp.sum(jnp.where(lane == N_EPG * grp + e, gates, 0.0), axis=-1, keepdims=True)
            hid = jax.nn.silu(a) * u * ge
            acc = acc + _dot(hid.astype(BF16), wd_ref[e])
        o_ref[...] = acc.astype(BF16)

    @pl.when(i >= nb_ref[0])
    def _():
        o_ref[...] = jnp.zeros_like(o_ref)


def _moe_combine_kernel(loc_ref, len_ref, off_ref, gate_ref, x1_ref, mod_ref, ln2g_ref, ln2b_ref, o_hbm,
                        ctx_ref, lat_ref, o_v, sem, *, n_ctx_tiles):
    s = pl.program_id(0)
    slot = s % 2

    def copies(buf):
        def o_copy(lr, gr):
            return pltpu.make_async_copy(o_hbm.at[pl.ds(gr, MOE_UNIT), :],
                                         o_v.at[buf, pl.ds(lr, MOE_UNIT), :], sem.at[buf])

        return (lambda lr, gr: o_copy(lr, gr).start()), (lambda lr, gr: o_copy(lr, gr).wait())

    @pl.when(s == 0)
    def _():
        o_v[...] = jnp.zeros_like(o_v)
        _segment_copies(s, loc_ref, len_ref, off_ref, copies(slot)[0])

    @pl.when(s + 1 < pl.num_programs(0))
    def _():
        _segment_copies(s + 1, loc_ref, len_ref, off_ref, copies(1 - slot)[0])

    pt = _perm_t(gate_ref[...], loc_ref, s).astype(BF16)
    _segment_copies(s, loc_ref, len_ref, off_ref, copies(slot)[1])
    f = _dot(pt, o_v[slot])
    g2 = mod_ref[0, :, 5 * D_MODEL:6 * D_MODEL]
    x2 = _norm(ALPHA * x1_ref[...] + g2 * f) * ln2g_ref[...] + ln2b_ref[...]

    @pl.when(s < n_ctx_tiles)
    def _():
        ctx_ref[...] = x2

    @pl.when(s >= n_ctx_tiles)
    def _():
        lat_ref[...] = x2


def _moe_plan(gates_all, n_blocks):
    n_tiles = gates_all.shape[0] // MOE_ST
    oh = gates_all[:, N_EXPERTS:N_EXPERTS + N_EGROUPS]
    cnt = jnp.sum(oh.reshape(n_tiles, MOE_ST, N_EGROUPS), axis=1).astype(jnp.int32)
    len16 = ((cnt + MOE_UNIT - 1) // MOE_UNIT) * MOE_UNIT
    loc = jnp.cumsum(len16, axis=1) - len16
    rows_g = jnp.sum(len16, axis=0)
    reg_g = ((rows_g + MOE_TM - 1) // MOE_TM) * MOE_TM
    reg_start = jnp.cumsum(reg_g) - reg_g
    off = reg_start[None, :] + jnp.cumsum(len16, axis=0) - len16
    blk_end = jnp.cumsum(reg_g // MOE_TM)
    bi = jnp.arange(n_blocks, dtype=jnp.int32)
    blk_group = jnp.minimum(jnp.sum((bi[:, None] >= blk_end[None, :]).astype(jnp.int32), axis=1),
                            N_EGROUPS - 1)
    flat = lambda a: a.reshape(-1).astype(jnp.int32)
    return flat(loc), flat(len16), flat(off), blk_group.astype(jnp.int32), blk_end[-1:].astype(jnp.int32)


def _moe(h2_all, gates_all, x1_all, mod, wg_bf, wu_bf, wd_bf, ln2g, ln2b, n_ctx, tokens_per_mod_row):
    n_tok = h2_all.shape[0]
    n_tiles = n_tok // MOE_ST
    n_ctx_tiles = n_ctx // MOE_ST
    max_rows = n_tok + n_tiles * N_EGROUPS * (MOE_UNIT - 1) + N_EGROUPS * (MOE_TM - 1)
    n_blocks = -(-max_rows // MOE_TM)
    n_rows = n_blocks * MOE_TM
    loc, len16, off, blk_group, n_used = _moe_plan(gates_all, n_blocks)

    tile = lambda w: pl.BlockSpec((MOE_ST, w), lambda s, *_: (s, 0))
    anyspec = pl.BlockSpec(memory_space=pl.ANY)
    xs, gs = pl.pallas_call(
        _moe_sort_kernel,
        grid_spec=pltpu.PrefetchScalarGridSpec(
            num_scalar_prefetch=3, grid=(n_tiles,),
            in_specs=[tile(D_MODEL), tile(LANES), anyspec, anyspec],
            out_specs=[anyspec, anyspec],
            scratch_shapes=[pltpu.VMEM((2, MOE_SLOTS, D_MODEL), BF16), pltpu.VMEM((2, MOE_SLOTS, LANES), F32),
                            pltpu.SemaphoreType.DMA((2, 2))]),
        out_shape=[jax.ShapeDtypeStruct((n_rows, D_MODEL), BF16),
                   jax.ShapeDtypeStruct((n_rows, LANES), F32)],
        input_output_aliases={5: 0, 6: 1},
        compiler_params=_cparams(("arbitrary",)),
        name="moe_sort",
    )(loc, len16, off, h2_all, gates_all, jnp.zeros((n_rows, D_MODEL), BF16), jnp.zeros((n_rows, LANES), F32))

    blk = lambda w: pl.BlockSpec((MOE_TM, w), lambda i, bg, nb: (jnp.minimum(i, nb[0] - 1), 0))
    wspec = lambda a, b: pl.BlockSpec((N_EPG, a, b), lambda i, bg, nb: (bg[i], 0, 0))
    o_sorted = pl.pallas_call(
        _moe_expert_kernel,
        grid_spec=pltpu.PrefetchScalarGridSpec(
            num_scalar_prefetch=2, grid=(n_blocks,),
            in_specs=[blk(D_MODEL), blk(LANES), wspec(D_MODEL, D_EXPERT), wspec(D_MODEL, D_EXPERT),
                      wspec(D_EXPERT, D_MODEL)],
            out_specs=pl.BlockSpec((MOE_TM, D_MODEL), lambda i, bg, nb: (i, 0))),
        out_shape=jax.ShapeDtypeStruct((n_rows, D_MODEL), BF16),
        compiler_params=_cparams(("arbitrary",)),
        name="moe_experts",
    )(blk_group, n_used, xs, gs, wg_bf, wu_bf, wd_bf)

    lat_per_row = tokens_per_mod_row // MOE_ST

    def mod_idx(s, *_):
        return (jnp.where(s < n_ctx_tiles, 0, 1 + (s - n_ctx_tiles) // lat_per_row), 0, 0)

    vec = pl.BlockSpec((1, D_MODEL), lambda s, *_: (0, 0))
    return pl.pallas_call(
        functools.partial(_moe_combine_kernel, n_ctx_tiles=n_ctx_tiles),
        grid_spec=pltpu.PrefetchScalarGridSpec(
            num_scalar_prefetch=3, grid=(n_tiles,),
            in_specs=[tile(LANES), tile(D_MODEL), pl.BlockSpec((1, 1, 6 * D_MODEL), mod_idx), vec, vec,
                      anyspec],
            out_specs=[pl.BlockSpec((MOE_ST, D_MODEL), lambda s, *_: (jnp.minimum(s, n_ctx_tiles - 1), 0)),
                       pl.BlockSpec((MOE_ST, D_MODEL), lambda s, *_: (jnp.maximum(s - n_ctx_tiles, 0), 0))],
            scratch_shapes=[pltpu.VMEM((2, MOE_SLOTS, D_MODEL), BF16), pltpu.SemaphoreType.DMA((2,))]),
        out_shape=[jax.ShapeDtypeStruct((n_ctx, D_MODEL), F32),
                   jax.ShapeDtypeStruct((n_tok - n_ctx, D_MODEL), F32)],
        compiler_params=_cparams(("arbitrary",)),
        name="moe_combine",
    )(loc, len16, off, gates_all, x1_all, mod.reshape(mod.shape[0], 1, 6 * D_MODEL), ln2g, ln2b, o_sorted)


def _grid_pos_embed(n_tokens):
    rows = n_tokens // GRID_W
    row = np.repeat(np.arange(rows, dtype=np.float64), GRID_W)
    col = np.tile(np.arange(GRID_W, dtype=np.float64), rows)
    quarter = D_MODEL // 4
    omega = 1.0 / (POS_BASE ** (np.arange(quarter, dtype=np.float64) / quarter))
    er = row[:, None] * omega
    ec = col[:, None] * omega
    return jnp.asarray(np.concatenate([np.sin(er), np.cos(er), np.sin(ec), np.cos(ec)], axis=-1), F32)


def _mixers(x, pos, mod3, h0_re, h0_im, tabs, filt, s5ops, wts, tm):
    bsz, n_tok, _ = x.shape
    shared = mod3.shape[0] == 1
    x3 = x.reshape(1, bsz * n_tok, D_MODEL) if shared else x
    proj_hy, u_s5 = _in_proj(x3, pos, mod3, wts['w_in'], tm)
    y_hy = _hyena(proj_hy.reshape(bsz, n_tok, 3 * D_HY), tabs, filt,
                  wts['hy_conv_w'], wts['hy_conv_b'], wts['hy_fbias'])
    y_s5, f_re, f_im = _s5(u_s5.reshape(bsz, n_tok, D_S5), s5ops, h0_re, h0_im)
    return y_hy.reshape(bsz * n_tok, D_HY), y_s5.reshape(bsz * n_tok, D_S5), f_re, f_im


def kernel(x_prompt, x_sample, state_s5_re, state_s5_im, c, c_ctx, w_ada, b_ada, w_in, hy_conv_w, hy_conv_b, hy_f_w1, hy_f_b1, hy_f_w2, hy_f_b2, hy_f_w3, hy_freq, hy_fbias, s5_a_re, s5_a_im, s5_log_dt, s5_b_re, s5_b_im, s5_c_re, s5_c_im, s5_d, s5_w_glu, s5_b_glu, out_norm_g, w_out, ln1_g, ln1_b, moe_w_r1, moe_b_r1, moe_w_r2, moe_b_r2, moe_w_gate, moe_w_up, moe_w_down, ln2_g, ln2_b):
    b_ctx, l_ctx, _ = x_prompt.shape
    b_lat, l_lat, _ = x_sample.shape
    g, p = S5_GROUPS, S5_STATE
    assert w_ada.shape[0] == 1, "single-layer trunk"
    l = 0

    nrow = 16
    cond = jnp.concatenate([c_ctx[None, :], c, jnp.zeros((nrow - 1 - b_lat, D_MODEL), F32)], axis=0)
    mod = _ada(cond, w_ada[l], b_ada[l])
    mod_ctx = mod[0:1].reshape(1, 1, 6 * D_MODEL)
    mod_lat = mod[1:1 + b_lat].reshape(b_lat, 1, 6 * D_MODEL)

    wr = jnp.concatenate([moe_w_r2[l].transpose(1, 0, 2).reshape(D_MODEL, N_EXPERTS), moe_w_r1[l]], axis=1)
    wr = jnp.pad(wr, ((0, 0), (0, LANES - wr.shape[1])))
    br = jnp.concatenate([moe_b_r2[l].reshape(-1), moe_b_r1[l]])
    br = jnp.pad(br, (0, LANES - br.shape[0])).reshape(1, LANES)
    wr_hi, wr_lo = _split(wr)

    wts = {
        'w_in': w_in[l].astype(BF16), 'hy_conv_w': hy_conv_w[l], 'hy_conv_b': hy_conv_b[l],
        'hy_fbias': hy_fbias[l], 'w_glu': s5_w_glu[l].astype(BF16), 'b_glu': s5_b_glu[l].reshape(1, -1),
        'out_norm_g': out_norm_g[l].reshape(1, -1), 'w_out': w_out[l].astype(BF16),
        'ln1_g': ln1_g[l].reshape(1, -1), 'ln1_b': ln1_b[l].reshape(1, -1),
        'wr_hi': wr_hi, 'wr_lo': wr_lo, 'br': br,
        'w_gate': moe_w_gate[l].astype(BF16), 'w_up': moe_w_up[l].astype(BF16),
        'w_down': moe_w_down[l].astype(BF16),
        'ln2_g': ln2_g[l].reshape(1, -1), 'ln2_b': ln2_b[l].reshape(1, -1),
    }

    s5ops = _s5_operators(s5_a_re[l], s5_a_im[l], s5_log_dt[l], s5_b_re[l], s5_b_im[l],
                          s5_c_re[l], s5_c_im[l], s5_d[l])
    tabs_ctx = _tables(l_ctx)
    tabs_lat = _tables(l_lat)
    filt_args = (hy_f_w1[l], hy_f_b1[l], hy_f_w2[l], hy_f_b2[l], hy_f_w3[l], hy_freq[l])
    filt_ctx = _hyena_filters(l_ctx, tabs_ctx, *filt_args)
    filt_lat = _hyena_filters(l_lat, tabs_lat, *filt_args)

    zero = jnp.zeros((g, b_ctx, 2 * p), F32)
    yhy_c, ys5_c, f_re, f_im = _mixers(x_prompt, None, mod_ctx, zero, zero, tabs_ctx, filt_ctx, s5ops, wts, 512)
    unpack = lambda f: f.reshape(g, b_ctx, 2, p).transpose(1, 2, 0, 3)[:, None]
    new_re, new_im = unpack(f_re), unpack(f_im)

    pack = lambda s: s[:, l].transpose(2, 0, 1, 3).reshape(g, b_lat, 2 * p)
    pos = _grid_pos_embed(l_lat)
    yhy_l, ys5_l, _, _ = _mixers(x_sample, pos, mod_lat, pack(state_s5_re), pack(state_s5_im),
                                 tabs_lat, filt_lat, s5ops, wts, 512)

    n_ctx = b_ctx * l_ctx
    x1_all, h2_all, gates_all = _out_proj(
        x_prompt.reshape(n_ctx, D_MODEL), x_sample.reshape(b_lat * l_lat, D_MODEL), pos,
        yhy_c, yhy_l, ys5_c, ys5_l, mod, wts['w_glu'], wts['b_glu'], wts['out_norm_g'], wts['w_out'],
        wts['ln1_g'], wts['ln1_b'], wts['wr_hi'], wts['wr_lo'], wts['br'], 512)
    y_ctx, y_lat = _moe(h2_all, gates_all, x1_all, mod,
                        wts['w_gate'], wts['w_up'], wts['w_down'], wts['ln2_g'], wts['ln2_b'],
                        n_ctx, l_lat)
    return (y_ctx.reshape(x_prompt.shape), y_lat.reshape(x_sample.shape), new_re, new_im)
```

```python
import functools
import math

import numpy as np
import jax
import jax.numpy as jnp
from jax import lax
from jax.experimental import pallas as pl
from jax.experimental.pallas import tpu as pltpu

F32 = jnp.float32
BF16 = jnp.bfloat16

D_MODEL = 1024
DEPTH = 1
GRID_W = 64
POS_BASE = 10000.0
D_HY = 512
D_S5 = 512
S5_CH = 16
S5_GROUPS = 32
S5_STATE = 64
S5_CHUNK = 16
S5_ROW = S5_CHUNK * S5_CH
HY_BANDS = 16
HY_EMB = 1 + 2 * HY_BANDS
HY_HID = 64
HY_MIN_DECAY = math.log(1e-2) / 1.5
HY_MAX_DECAY = math.log(1e-2) / 0.3
N_EGROUPS = 4
N_EPG = 4
N_EXPERTS = 16
D_EXPERT = 512
LN_EPS = 1e-5
ALPHA = (2.0 * DEPTH) ** 0.25
LANES = 128
S5_GB = LANES // S5_CH
S5OPS_GB = 4
HY_CW = 512
MOE_ST = 512
MOE_SLOTS = 640
MOE_UNIT = 16
MOE_TM = 512
VMEM_LIMIT = 60000 * 1024


def _cparams(sem):
    return pltpu.CompilerParams(dimension_semantics=sem, vmem_limit_bytes=VMEM_LIMIT)


def _split(x):
    hi = x.astype(BF16)
    lo = (x - hi.astype(F32)).astype(BF16)
    return hi, lo


def _dot(a, b):
    return jnp.dot(a, b, preferred_element_type=F32)


def _dot_t(a, b):
    return lax.dot_general(a, b, (((1,), (1,)), ((), ())), preferred_element_type=F32)


def _mm3(a, b):
    ah, al = _split(a)
    bh, bl = _split(b)
    return _dot(ah, bh) + _dot(al, bh) + _dot(ah, bl)


def _dot_hp(a, b):
    return jnp.dot(a, b, preferred_element_type=F32, precision=lax.Precision.HIGHEST)


def _norm(x):
    xc = x - jnp.mean(x, axis=-1, keepdims=True)
    return xc * lax.rsqrt(jnp.mean(xc * xc, axis=-1, keepdims=True) + LN_EPS)


def _rms(y):
    return y * lax.rsqrt(jnp.mean(y * y, axis=-1, keepdims=True) + LN_EPS)


def _ada_kernel(cond_ref, w_ref, b_ref, o_ref):
    c = jax.nn.silu(cond_ref[...])
    o_ref[...] = _mm3(c, w_ref[...]) + b_ref[...]


def _ada(cond, w_ada, b_ada):
    nb = cond.shape[0]
    n = w_ada.shape[1]
    tn = 1024
    return pl.pallas_call(
        _ada_kernel,
        grid=(n // tn,),
        in_specs=[pl.BlockSpec((nb, D_MODEL), lambda j: (0, 0)),
                  pl.BlockSpec((D_MODEL, tn), lambda j: (0, j)),
                  pl.BlockSpec((1, tn), lambda j: (0, j))],
        out_specs=pl.BlockSpec((nb, tn), lambda j: (0, j)),
        out_shape=jax.ShapeDtypeStruct((nb, n), F32),
        compiler_params=_cparams(("arbitrary",)),
        name="ada",
    )(cond, w_ada, b_ada.reshape(1, n))


def _dft_tables(n_half):
    n = 2 * n_half
    idx = np.arange(n_half, dtype=np.int64)
    m = (idx[:, None] * idx[None, :]) % n
    ang = 2.0 * np.pi * m.astype(np.float64) / n
    cm = np.cos(ang)
    sm = -np.sin(ang)
    sm[0, :] = 1.0 - 2.0 * (idx % 2)
    return cm.astype(np.float32), sm.astype(np.float32)


def _tables(n_tok):
    n_half = n_tok // 2
    cm, sm = _dft_tables(n_half)
    mats = tuple(jnp.asarray(t).astype(BF16) for t in (cm, sm, np.ascontiguousarray(sm.T)))
    ang = np.pi * np.arange(n_half, dtype=np.float64) / n_tok
    tw = [np.broadcast_to(v[:, None], (n_half, HY_CW)).astype(np.float32) for v in (np.cos(ang), -np.sin(ang))]
    return mats + (jnp.asarray(tw[0]), jnp.asarray(tw[1]))


def _put_cols(ref, x):
    for j in range(ref.shape[0]):
        ref[j] = x[:, LANES * j:LANES * (j + 1)]


def _get_cols(ref):
    return jnp.concatenate([ref[j] for j in range(ref.shape[0])], axis=1)


def _get_parity(ref, parity):
    n_half = ref.shape[1] // 2
    return jnp.concatenate([ref[j, pl.ds(parity, n_half, stride=2), :] for j in range(ref.shape[0])], axis=1)


def _put_parity(ref, parity, x):
    n_half = ref.shape[1] // 2
    for j in range(ref.shape[0]):
        ref[j, pl.ds(parity, n_half, stride=2), :] = x[:, LANES * j:LANES * (j + 1)]


def _set_row0(x, v):
    first = lax.broadcasted_iota(jnp.int32, (8, x.shape[1]), 0) == 0
    return jnp.concatenate([jnp.where(first, v, x[:8]), x[8:]], axis=0)


def _rfft_packed(x_ref, cm, sm, tw_re, tw_im):
    xe = _get_parity(x_ref, 0).astype(BF16)
    xo = _get_parity(x_ref, 1).astype(BF16)
    e_re, e_im = _dot(cm, xe), _dot(sm, xe)
    o_re, o_im = _dot(cm, xo), _dot(sm, xo)
    t_re = tw_re * o_re - tw_im * o_im
    t_im = tw_re * o_im + tw_im * o_re
    a_im = _set_row0(e_im + t_im, e_im[0:1])
    b_im = _set_row0(t_im - e_im, -o_im[0:1])
    return e_re + t_re, a_im, e_re - t_re, b_im


def _irfft_packed(y_ref, ya_re, ya_im, yb_re, yb_im, cm, st, tw_re, tw_im):
    p_e = ya_re + yb_re
    q_e = _set_row0(ya_im - yb_im, ya_im[0:1])
    _put_parity(y_ref, 0, _dot(cm, p_e.astype(BF16)) + _dot(st, q_e.astype(BF16)))
    ra_re = ya_re * tw_re + ya_im * tw_im
    ra_im = ya_im * tw_re - ya_re * tw_im
    rb_re = yb_im * tw_im - yb_re * tw_re
    rb_im = -(yb_re * tw_im + yb_im * tw_re)
    p_o = ra_re + rb_re
    q_o = _set_row0(ra_im - rb_im, -yb_im[0:1])
    _put_parity(y_ref, 1, _dot(cm, p_o.astype(BF16)) + _dot(st, q_o.astype(BF16)))


def _filt_kernel(n_tok, z_ref, t_ref, w1_ref, b1_ref, w2_ref, b2_ref, fr_ref, w3f_ref, w3b_ref,
                 dl_ref, cm_ref, sm_ref, twr_ref, twi_ref, kar_ref, kai_ref, kbr_ref, kbi_ref, p_ref, q_ref):
    fr = fr_ref[...]
    h = jnp.sin(fr * (_dot_hp(z_ref[...], w1_ref[...]) + b1_ref[...]))
    h = jnp.sin(fr * (_dot_hp(h, w2_ref[...]) + b2_ref[...]))
    decay = jnp.exp(-t_ref[...] * dl_ref[...])
    row = lax.broadcasted_iota(jnp.int32, decay.shape, 0)
    hf = _dot_hp(h, w3f_ref[...]) * decay
    hb = jnp.where(row == 0, 0.0, _dot_hp(h, w3b_ref[...]) * decay)
    _put_cols(p_ref, hf + hb)
    _put_cols(q_ref, hf - hb)
    cm, sm, tw_re, tw_im = cm_ref[...], sm_ref[...], twr_ref[...], twi_ref[...]
    row0 = lax.broadcasted_iota(jnp.int32, tw_re.shape, 0) == 0
    pa_re, pa_im, pb_re, _ = _rfft_packed(p_ref, cm, sm, tw_re, tw_im)
    _, qa_im, _, qb_im = _rfft_packed(q_ref, cm, sm, tw_re, tw_im)
    inv_n = 1.0 / (2 * n_tok)
    w_re = jnp.where(row0, inv_n, 2.0 * inv_n)
    kar_ref[...] = w_re * pa_re
    kbr_ref[...] = w_re * pb_re
    kai_ref[...] = (2.0 * inv_n) * _set_row0(qa_im, pa_im[0:1])
    kbi_ref[...] = (2.0 * inv_n) * qb_im


def _hyena_filters(n_tok, tabs, hy_f_w1, hy_f_b1, hy_f_w2, hy_f_b2, hy_f_w3, hy_freq):
    cm, sm, _, tw_re, tw_im = tabs
    n_half = n_tok // 2
    t = jnp.linspace(0.0, 1.0, n_tok, dtype=F32)[:, None]
    wv = 2.0 * math.pi * jnp.arange(n_tok, dtype=F32) / n_tok
    fb = jnp.linspace(1e-4, HY_BANDS - 1, HY_BANDS, dtype=F32)
    ang = wv[:, None] * fb[None, :]
    z = jnp.concatenate([t, jnp.cos(ang), -jnp.sin(ang)], axis=-1)
    z = jnp.pad(z, ((0, 0), (0, LANES - HY_EMB)))
    w1 = jnp.pad(hy_f_w1, ((0, LANES - HY_EMB), (0, 0)))
    deltas = jnp.abs(jnp.linspace(HY_MIN_DECAY, HY_MAX_DECAY, D_HY, dtype=F32))[None, :]
    ncb = D_HY // HY_CW
    full = lambda j: (0, 0)
    out_sd = jax.ShapeDtypeStruct((n_half, 2 * D_HY), F32)
    mat = pl.BlockSpec((n_half, n_half), full, pipeline_mode=pl.Buffered(1))
    twb = pl.BlockSpec((n_half, HY_CW), full, pipeline_mode=pl.Buffered(1))
    return pl.pallas_call(
        functools.partial(_filt_kernel, n_tok),
        grid=(2 * ncb,),
        in_specs=[pl.BlockSpec((n_tok, LANES), full),
                  pl.BlockSpec((n_tok, 1), full),
                  pl.BlockSpec((LANES, HY_HID), full),
                  pl.BlockSpec((1, HY_HID), full),
                  pl.BlockSpec((HY_HID, HY_HID), full),
                  pl.BlockSpec((1, HY_HID), full),
                  pl.BlockSpec((1, HY_HID), full),
                  pl.BlockSpec((HY_HID, HY_CW), lambda j: (0, 2 * ncb * (j // ncb) + j % ncb)),
                  pl.BlockSpec((HY_HID, HY_CW), lambda j: (0, 2 * ncb * (j // ncb) + ncb + j % ncb)),
                  pl.BlockSpec((1, HY_CW), lambda j: (0, j % ncb)),
                  mat, mat, twb, twb],
        out_specs=[pl.BlockSpec((n_half, HY_CW), lambda j: (0, j))] * 4,
        out_shape=[out_sd] * 4,
        scratch_shapes=[pltpu.VMEM((HY_CW // LANES, n_tok, LANES), F32)] * 2,
        compiler_params=_cparams(("arbitrary",)),
        name=f"filt{n_tok}",
    )(z, t, w1, hy_f_b1.reshape(1, -1), hy_f_w2, hy_f_b2.reshape(1, -1), hy_freq.reshape(1, -1),
      hy_f_w3, hy_f_w3, deltas, cm, sm, tw_re, tw_im)


def _hyena_kernel(pv_ref, p1_ref, p2_ref, cwv_ref, cw1_ref, cw2_ref, cbv_ref, cb1_ref, cb2_ref,
                  fbias_ref, cm_ref, sm_ref, st_ref, twr_ref, twi_ref,
                  kar0_ref, kai0_ref, kbr0_ref, kbi0_ref, kar1_ref, kai1_ref, kbr1_ref, kbi1_ref,
                  o_ref, u_ref, y_ref):
    n_tok = pv_ref.shape[1]
    row = lax.broadcasted_iota(jnp.int32, (n_tok, pv_ref.shape[2]), 0)

    def short_conv(p_ref, cw_ref, cb_ref):
        p = p_ref[0]
        prev = jnp.where(row == 0, 0.0, pltpu.roll(p, 1, axis=0))
        nxt = jnp.where(row == n_tok - 1, 0.0, pltpu.roll(p, n_tok - 1, axis=0))
        return cb_ref[...] + prev * cw_ref[0:1, :] + p * cw_ref[1:2, :] + nxt * cw_ref[2:3, :]

    cm, sm, st, tw_re, tw_im = cm_ref[...], sm_ref[...], st_ref[...], twr_ref[...], twi_ref[...]

    def fftconv(u, kar_ref, kai_ref, kbr_ref, kbi_ref, skip):
        _put_cols(u_ref, u)
        ua_re, ua_im, ub_re, ub_im = _rfft_packed(u_ref, cm, sm, tw_re, tw_im)
        ka_re, ka_im, kb_re, kb_im = kar_ref[...], kai_ref[...], kbr_ref[...], kbi_ref[...]
        zero_row = jnp.zeros_like(ka_im[0:1])
        kaz = _set_row0(ka_im, zero_row)
        kbz = _set_row0(kb_im, zero_row)
        ya_re = ua_re * ka_re - ua_im * kaz
        yb_re = ub_re * kb_re - ub_im * kbz
        h_re = ua_im[0:1] * ka_im[0:1] - ub_im[0:1] * kb_im[0:1]
        h_im = ua_im[0:1] * kb_im[0:1] + ub_im[0:1] * ka_im[0:1]
        ya_im = _set_row0(ua_re * ka_im + ua_im * ka_re, h_re)
        yb_im = _set_row0(ub_re * kb_im + ub_im * kb_re, h_im)
        _irfft_packed(y_ref, ya_re, ya_im, yb_re, yb_im, cm, st, tw_re, tw_im)
        return _get_cols(y_ref) + u * skip

    v = short_conv(pv_ref, cwv_ref, cbv_ref)
    x1 = short_conv(p1_ref, cw1_ref, cb1_ref)
    z = x1 * fftconv(v, kar0_ref, kai0_ref, kbr0_ref, kbi0_ref, fbias_ref[0:1, :])
    x2 = short_conv(p2_ref, cw2_ref, cb2_ref)
    o_ref[0] = x2 * fftconv(z, kar1_ref, kai1_ref, kbr1_ref, kbi1_ref, fbias_ref[1:2, :])


def _hyena(proj_hy, tabs, filt, hy_conv_w, hy_conv_b, hy_fbias):
    bsz, n_tok, _ = proj_hy.shape
    n_half = n_tok // 2
    ncb = D_HY // HY_CW
    cm, sm, st, tw_re, tw_im = tabs
    cb = hy_conv_b.reshape(1, -1)
    const = lambda shape: pl.BlockSpec(shape, lambda b, c: (0, 0), pipeline_mode=pl.Buffered(1))
    mat = const((n_half, n_half))
    twb = const((n_half, HY_CW))

    def pspec(k):
        return pl.BlockSpec((1, n_tok, HY_CW), lambda b, c: (b, 0, k * ncb + c))

    def cwspec(k):
        return pl.BlockSpec((3, HY_CW), lambda b, c: (0, k * ncb + c))

    def cbspec(k):
        return pl.BlockSpec((1, HY_CW), lambda b, c: (0, k * ncb + c))

    def fspec(o):
        mode = pl.Buffered(1) if ncb == 1 else None
        return pl.BlockSpec((n_half, HY_CW), lambda b, c: (0, o * ncb + c), pipeline_mode=mode)

    return pl.pallas_call(
        _hyena_kernel,
        grid=(bsz, ncb),
        in_specs=[pspec(0), pspec(1), pspec(2), cwspec(0), cwspec(1), cwspec(2),
                  cbspec(0), cbspec(1), cbspec(2),
                  pl.BlockSpec((2, HY_CW), lambda b, c: (0, c)),
                  mat, mat, mat, twb, twb] + [fspec(0)] * 4 + [fspec(1)] * 4,
        out_specs=pl.BlockSpec((1, n_tok, HY_CW), lambda b, c: (b, 0, c)),
        out_shape=jax.ShapeDtypeStruct((bsz, n_tok, D_HY), F32),
        scratch_shapes=[pltpu.VMEM((HY_CW // LANES, n_tok, LANES), F32)] * 2,
        compiler_params=_cparams(("arbitrary", "arbitrary")),
        name=f"hyena{n_tok}",
    )(proj_hy, proj_hy, proj_hy, hy_conv_w, hy_conv_w, hy_conv_w, cb, cb, cb, hy_fbias,
      cm, sm, st, tw_re, tw_im, *filt, *filt)


def _s5ops_kernel(*refs):
    for g in range(S5OPS_GB):
        _s5ops_group(g, *refs)


def _s5ops_group(g, are_ref, aim_ref, ldt_ref, btr_ref, bti_ref, cre_ref, cim_ref, d_ref,
                 mt_ref, erh_ref, erl_ref, eih_ref, eil_ref, gr_ref, gi_ref, atr_ref, ati_ref,
                 er_ref, ei_ref):
    a_re, a_im = are_ref[g], aim_ref[g]
    dt = jnp.exp(ldt_ref[g])
    mag = jnp.exp(a_re * dt)
    ab_re = mag * jnp.cos(a_im * dt)
    ab_im = mag * jnp.sin(a_im * dt)
    n_re, n_im = ab_re - 1.0, ab_im
    den = a_re * a_re + a_im * a_im
    q_re = (n_re * a_re + n_im * a_im) / den
    q_im = (n_im * a_re - n_re * a_im) / den
    bt_re, bt_im = btr_ref[g], bti_ref[g]
    bb_re = q_re * bt_re - q_im * bt_im
    bb_im = q_re * bt_im + q_im * bt_re
    c_re, c_im = cre_ref[g, 0:S5_CH, :], cim_ref[g, 0:S5_CH, :]
    pw = [(jnp.ones_like(ab_re), jnp.zeros_like(ab_re))]
    for _ in range(S5_CHUNK):
        pr, pi = pw[-1]
        pw.append((pr * ab_re - pi * ab_im, pr * ab_im + pi * ab_re))
    lane = lax.broadcasted_iota(jnp.int32, ab_re.shape, 1)
    fwd = lane < S5_STATE
    for s in range(S5_CHUNK):
        e_re = jnp.where(fwd, pw[S5_CHUNK - 1 - s][0], pw[s][0])
        e_im = jnp.where(fwd, pw[S5_CHUNK - 1 - s][1], pw[s][1])
        er_ref[g, pl.ds(S5_CH * s, S5_CH), :] = e_re * bb_re - e_im * bb_im
        ei_ref[g, pl.ds(S5_CH * s, S5_CH), :] = e_re * bb_im + e_im * bb_re
        g_re = jnp.where(fwd, pw[s + 1][0], pw[S5_CHUNK - s][0])
        g_im = jnp.where(fwd, pw[s + 1][1], pw[S5_CHUNK - s][1])
        gr_ref[g, pl.ds(S5_CH * s, S5_CH), :] = (c_re * g_re - c_im * g_im).astype(BF16)
        gi_ref[g, pl.ds(S5_CH * s, S5_CH), :] = (-(c_re * g_im + c_im * g_re)).astype(BF16)
    atr_ref[g] = pw[S5_CHUNK][0]
    ati_ref[g] = pw[S5_CHUNK][1]
    er, ei = er_ref[g], ei_ref[g]
    erh_ref[g], erl_ref[g] = _split(er)
    eih_ref[g], eil_ref[g] = _split(ei)
    lane2 = lax.broadcasted_iota(jnp.int32, er.shape, 1)
    row2 = lax.broadcasted_iota(jnp.int32, er.shape, 0)
    f2 = lane2 < S5_STATE
    zero = jnp.zeros_like(er)

    def dot_hp_t(a, b):
        return lax.dot_general(a, b, (((1,), (1,)), ((), ())), preferred_element_type=F32,
                               precision=lax.Precision.HIGHEST)

    cp_re, cp_im = cre_ref[g], cim_ref[g]
    kf = dot_hp_t(jnp.where(f2, er, zero), cp_re) - dot_hp_t(jnp.where(f2, ei, zero), cp_im)
    kb = dot_hp_t(jnp.where(f2, zero, er), cp_re) - dot_hp_t(jnp.where(f2, zero, ei), cp_im)
    d_row = d_ref[g]
    steps_per_vreg = LANES // S5_CH
    for half in range(S5_CHUNK // steps_per_vreg):
        acc = zero
        for tt in range(steps_per_vreg):
            t = half * steps_per_vreg + tt
            nf = S5_CH * (S5_CHUNK - 1 - t)
            nb = S5_CH * t
            col_f = jnp.concatenate([kf[nf:], zero[:nf]], axis=0) if nf else kf
            col_b = jnp.concatenate([zero[:nb], kb[:S5_ROW - nb]], axis=0) if nb else kb
            diag = jnp.where((row2 // S5_CH == t) & (row2 % S5_CH == lane2), d_row, 0.0)
            col = col_f + col_b + diag
            r = pltpu.roll(col, S5_CH * tt, axis=1) if tt else col
            acc = jnp.where((lane2 >= S5_CH * tt) & (lane2 < S5_CH * (tt + 1)), r, acc)
        mt_ref[g, :, LANES * half:LANES * (half + 1)] = acc.astype(BF16)


def _s5_operators(s5_a_re, s5_a_im, s5_log_dt, s5_b_re, s5_b_im, s5_c_re, s5_c_im, s5_d):
    g, p, h = S5_GROUPS, S5_STATE, S5_CH
    cat = lambda x: jnp.concatenate([x[0], x[1]], axis=-1)
    a_re = cat(s5_a_re).reshape(g, 1, 2 * p)
    a_im = cat(s5_a_im).reshape(g, 1, 2 * p)
    ldt = cat(jnp.broadcast_to(s5_log_dt[:, :, None], (2, g, p))).reshape(g, 1, 2 * p)
    bt_re = cat(jnp.swapaxes(s5_b_re, -1, -2))
    bt_im = cat(jnp.swapaxes(s5_b_im, -1, -2))
    cpad = lambda c: jnp.pad(jnp.concatenate([c, c], axis=-1), ((0, 0), (0, LANES - h), (0, 0)))
    c_re, c_im = cpad(s5_c_re), cpad(s5_c_im)
    d_row = jnp.pad(s5_d.reshape(g, 1, h), ((0, 0), (0, 0), (0, LANES - h)))
    vec = pl.BlockSpec((S5OPS_GB, 1, 2 * p), lambda i: (i, 0, 0))
    hp = pl.BlockSpec((S5OPS_GB, h, 2 * p), lambda i: (i, 0, 0))
    sq = pl.BlockSpec((S5OPS_GB, LANES, 2 * p), lambda i: (i, 0, 0))
    big = pl.BlockSpec((S5OPS_GB, S5_ROW, 2 * p), lambda i: (i, 0, 0))
    mts = pl.BlockSpec((S5OPS_GB, S5_ROW, S5_ROW), lambda i: (i, 0, 0))
    big_sd = jax.ShapeDtypeStruct((g, S5_ROW, 2 * p), BF16)
    vec_sd = jax.ShapeDtypeStruct((g, 1, 2 * p), F32)
    return pl.pallas_call(
        _s5ops_kernel,
        grid=(g // S5OPS_GB,),
        in_specs=[vec, vec, vec, hp, hp, sq, sq, vec],
        out_specs=[mts, big, big, big, big, big, big, vec, vec],
        out_shape=[jax.ShapeDtypeStruct((g, S5_ROW, S5_ROW), BF16)] + [big_sd] * 6 + [vec_sd, vec_sd],
        scratch_shapes=[pltpu.VMEM((S5OPS_GB, S5_ROW, 2 * p), F32)] * 2,
        compiler_params=_cparams(("arbitrary",)),
        name="s5ops",
    )(a_re, a_im, ldt, bt_re, bt_im, c_re, c_im, d_row)


def _block_transpose(xs):
    n = len(xs)
    lane = lax.broadcasted_iota(jnp.int32, xs[0].shape, 1)
    xs = list(xs)
    d = n // 2
    while d:
        keep = ((lane // S5_CH) & d) == 0
        for i in range(n):
            if i & d:
                continue
            lo, hi = xs[i], xs[i + d]
            xs[i] = jnp.where(keep, lo, pltpu.roll(hi, S5_CH * d, axis=1))
            xs[i + d] = jnp.where(keep, pltpu.roll(lo, LANES - S5_CH * d, axis=1), hi)
        d //= 2
    return xs


def _s5_kernel(bsz, n_chunks, u_ref, mt_ref, erh_ref, erl_ref, eih_ref, eil_ref, gr_ref, gi_ref,
               atr_ref, ati_ref, h0r_ref, h0i_ref, y_ref, fr_ref, fi_ref,
               ua_ref, ub_ref, ya_ref, yb_ref, sr_ref, si_ref, xfr_ref, xfi_ref, xbr_ref, xbi_ref):
    nc = n_chunks
    spv = LANES // S5_CH
    rsub = min(nc, 32)

    def to_chunks(b, carry):
        for half, dst in ((0, ua_ref), (1, ub_ref)):
            for r0 in range(0, nc, rsub):
                xs = [u_ref[b, pl.ds(S5_CHUNK * r0 + half * spv + tt, rsub, stride=S5_CHUNK), :]
                      for tt in range(spv)]
                for k, blk in enumerate(_block_transpose(xs)):
                    dst[k, pl.ds(r0 * bsz + b, rsub, stride=bsz), :] = blk
        return carry

    lax.fori_loop(0, bsz, to_chunks, 0, unroll=2)

    lane = lax.broadcasted_iota(jnp.int32, (bsz, 2 * S5_STATE), 1)
    fwd = lane < S5_STATE
    lane_all = lax.broadcasted_iota(jnp.int32, (bsz * nc, 2 * S5_STATE), 1)
    fwd_all = lane_all < S5_STATE

    def group(k, slot):
        u = jnp.concatenate([ua_ref[k], ub_ref[k]], axis=1)
        uh, ul = _split(u)
        sr_ref[slot] = _dot(uh, erh_ref[k]) + _dot(ul, erh_ref[k]) + _dot(uh, erl_ref[k])
        si_ref[slot] = _dot(uh, eih_ref[k]) + _dot(ul, eih_ref[k]) + _dot(uh, eil_ref[k])
        at_re, at_im = atr_ref[k], ati_ref[k]
        y_intra = _dot(uh, mt_ref[k])

        def step(i, xc):
            x_re, x_im = xc
            rf = pl.ds(pl.multiple_of(i * bsz, bsz), bsz)
            rb = pl.ds(pl.multiple_of((nc - 1 - i) * bsz, bsz), bsz)
            xfr_ref[slot, rf, :] = x_re
            xfi_ref[slot, rf, :] = x_im
            xbr_ref[slot, rb, :] = x_re
            xbi_ref[slot, rb, :] = x_im
            s_re = jnp.where(fwd, sr_ref[slot, rf, :], sr_ref[slot, rb, :])
            s_im = jnp.where(fwd, si_ref[slot, rf, :], si_ref[slot, rb, :])
            return (at_re * x_re - at_im * x_im + s_re, at_re * x_im + at_im * x_re + s_im)

        x_re, x_im = lax.fori_loop(0, nc, step, (h0r_ref[k], h0i_ref[k]), unroll=True)
        fr_ref[k] = x_re
        fi_ref[k] = x_im
        xp_re = jnp.where(fwd_all, xfr_ref[slot], xbr_ref[slot]).astype(BF16)
        xp_im = jnp.where(fwd_all, xfi_ref[slot], xbi_ref[slot]).astype(BF16)
        y = y_intra + _dot_t(xp_re, gr_ref[k]) + _dot_t(xp_im, gi_ref[k])
        ya_ref[k] = y[:, :LANES]
        yb_ref[k] = y[:, LANES:]

    def group_pair(j, carry):
        group(2 * j, 0)
        group(2 * j + 1, 1)
        return carry

    lax.fori_loop(0, S5_GB // 2, group_pair, 0)

    def to_tokens(b, carry):
        for half, src in ((0, ya_ref), (1, yb_ref)):
            for r0 in range(0, nc, rsub):
                ys = [src[k, pl.ds(r0 * bsz + b, rsub, stride=bsz), :] for k in range(S5_GB)]
                for tt, blk in enumerate(_block_transpose(ys)):
                    y_ref[b, pl.ds(S5_CHUNK * r0 + half * spv + tt, rsub, stride=S5_CHUNK), :] = blk
        return carry

    lax.fori_loop(0, bsz, to_tokens, 0, unroll=2)


def _s5(u, ops, h0_re, h0_im):
    bsz, n_tok, _ = u.shape
    g, p = S5_GROUPS, S5_STATE
    nc = n_tok // S5_CHUNK
    rows = nc * bsz
    tok = pl.BlockSpec((bsz, n_tok, LANES), lambda j: (0, 0, j))
    gspec = lambda shape: pl.BlockSpec((S5_GB,) + shape, lambda j: (j, 0, 0))
    op = gspec((S5_ROW, 2 * p))
    return pl.pallas_call(
        functools.partial(_s5_kernel, bsz, nc),
        grid=(g // S5_GB,),
        in_specs=[tok, gspec((S5_ROW, S5_ROW)), op, op, op, op, op, op,
                  gspec((1, 2 * p)), gspec((1, 2 * p)), gspec((bsz, 2 * p)), gspec((bsz, 2 * p))],
        out_specs=[tok, gspec((bsz, 2 * p)), gspec((bsz, 2 * p))],
        out_shape=[jax.ShapeDtypeStruct((bsz, n_tok, D_S5), F32),
                   jax.ShapeDtypeStruct((g, bsz, 2 * p), F32),
                   jax.ShapeDtypeStruct((g, bsz, 2 * p), F32)],
        scratch_shapes=([pltpu.VMEM((S5_GB, rows, LANES), F32)] * 4
                        + [pltpu.VMEM((2, rows, 2 * p), F32)] * 6),
        compiler_params=_cparams(("arbitrary",)),
        name=f"s5_{n_tok}",
    )(u, *ops, h0_re, h0_im)


def _in_kernel(has_pos, *refs):
    if has_pos:
        x_ref, pos_ref, mod_ref, w_ref, hy_ref, s5_ref = refs
        x = x_ref[0] + pos_ref[...]
    else:
        x_ref, mod_ref, w_ref, hy_ref, s5_ref = refs
        x = x_ref[0]
    sh1 = mod_ref[0, :, 0:D_MODEL]
    sc1 = mod_ref[0, :, D_MODEL:2 * D_MODEL]
    h = _norm(x) * (1.0 + sc1) + sh1
    proj = _dot(h.astype(BF16), w_ref[...])
    hy_ref[0] = proj[:, :3 * D_HY]
    s5_ref[0] = proj[:, 3 * D_HY:]


def _in_proj(x3, pos, mod3, w_in_bf, tm):
    nb, lt, _ = x3.shape
    has_pos = pos is not None
    per_batch = mod3.shape[0] > 1
    midx = (lambda b, i: (b, 0, 0)) if per_batch else (lambda b, i: (0, 0, 0))
    in_specs = [pl.BlockSpec((1, tm, D_MODEL), lambda b, i: (b, i, 0))]
    args = [x3]
    if has_pos:
        in_specs.append(pl.BlockSpec((tm, D_MODEL), lambda b, i: (i, 0)))
        args.append(pos)
    in_specs += [pl.BlockSpec((1, 1, 6 * D_MODEL), midx),
                 pl.BlockSpec((D_MODEL, 3 * D_HY + D_S5), lambda b, i: (0, 0))]
    args += [mod3, w_in_bf]
    return pl.pallas_call(
        functools.partial(_in_kernel, has_pos),
        grid=(nb, lt // tm),
        in_specs=in_specs,
        out_specs=[pl.BlockSpec((1, tm, 3 * D_HY), lambda b, i: (b, i, 0)),
                   pl.BlockSpec((1, tm, D_S5), lambda b, i: (b, i, 0))],
        out_shape=[jax.ShapeDtypeStruct((nb, lt, 3 * D_HY), F32),
                   jax.ShapeDtypeStruct((nb, lt, D_S5), F32)],
        compiler_params=_cparams(("arbitrary", "arbitrary")),
        name=f"in_proj{nb}",
    )(*args)


def _route(logits):
    lane = lax.broadcasted_iota(jnp.int32, logits.shape, 1)
    lane_f = lane.astype(F32)
    neg = -jnp.inf
    big = float(LANES)
    m1 = (lane >= N_EXPERTS) & (lane < N_EXPERTS + N_EGROUPS)
    l1 = jnp.where(m1, logits, neg)
    top1 = jnp.max(l1, axis=-1, keepdims=True)
    grp = jnp.min(jnp.where(l1 == top1, lane_f, big), axis=-1, keepdims=True) - float(N_EXPERTS)
    den = jnp.sum(jnp.where(m1, jnp.exp(logits - top1), 0.0), axis=-1, keepdims=True)
    p_grp = 1.0 / den
    lo = grp * float(N_EPG)
    m2 = (lane_f >= lo) & (lane_f < lo + float(N_EPG))
    l2 = jnp.where(m2, logits, neg)
    v1 = jnp.max(l2, axis=-1, keepdims=True)
    i1 = jnp.min(jnp.where(l2 == v1, lane_f, big), axis=-1, keepdims=True)
    l2b = jnp.where(lane_f == i1, neg, l2)
    v2 = jnp.max(l2b, axis=-1, keepdims=True)
    i2 = jnp.min(jnp.where(l2b == v2, lane_f, big), axis=-1, keepdims=True)
    e = jnp.exp(v2 - v1)
    w1 = 1.0 / (1.0 + e)
    w2 = e / (1.0 + e)
    gates = jnp.where(lane_f == i1, w1 * p_grp, 0.0) + jnp.where(lane_f == i2, w2 * p_grp, 0.0)
    return jnp.where(lane_f == grp + float(N_EXPERTS), 1.0, gates)


def _out_kernel(n_ctx_blocks, xc_ref, xl_ref, pos_ref, yhyc_ref, yhyl_ref, ys5c_ref, ys5l_ref, mod_ref,
                wglu_ref, bglu_ref, ong_ref, wout_ref, ln1g_ref, ln1b_ref, wrh_ref, wrl_ref, br_ref,
                x1_ref, h2_ref, gate_ref, cnt_ref):
    is_ctx = pl.program_id(0) < n_ctx_blocks
    x = jnp.where(is_ctx, xc_ref[...], xl_ref[...] + pos_ref[...])
    y = jnp.where(is_ctx, ys5c_ref[...], ys5l_ref[...])
    y_hy = jnp.where(is_ctx, yhyc_ref[...], yhyl_ref[...])
    s5 = jax.nn.gelu(y) * jax.nn.sigmoid(_dot(y.astype(BF16), wglu_ref[...]) + bglu_ref[...])
    m_hy = _rms(y_hy) * ong_ref[:, 0:D_HY]
    m_s5 = _rms(s5) * ong_ref[:, D_HY:]
    o = (_dot(m_hy.astype(BF16), wout_ref[0:D_HY, :]) + _dot(m_s5.astype(BF16), wout_ref[D_HY:, :]))
    g1 = mod_ref[0, :, 2 * D_MODEL:3 * D_MODEL]
    sh2 = mod_ref[0, :, 3 * D_MODEL:4 * D_MODEL]
    sc2 = mod_ref[0, :, 4 * D_MODEL:5 * D_MODEL]
    x1 = _norm(ALPHA * x + g1 * o) * ln1g_ref[...] + ln1b_ref[...]
    x1_ref[...] = x1
    h2 = _norm(x1) * (1.0 + sc2) + sh2
    h2_ref[...] = h2.astype(BF16)
    hh, hl = _split(h2)
    logits = (_dot(hh, wrh_ref[...]) + _dot(hl, wrh_ref[...]) + _dot(hh, wrl_ref[...]) + br_ref[...])
    gates = _route(logits)
    gate_ref[...] = gates
    cnt_ref[0] = jnp.sum(gates, axis=0, keepdims=True)


def _out_proj(xc, xl, pos, yhy_c, yhy_l, ys5_c, ys5_l, mod, wglu_bf, bglu, ong, wout_bf, ln1g, ln1b,
              wr_hi, wr_lo, br, tm):
    n_ctx, n_lat = xc.shape[0], xl.shape[0]
    l_lat = pos.shape[0]
    ncb, nlb, npb = n_ctx // tm, n_lat // tm, l_lat // tm
    ctx = lambda w: pl.BlockSpec((tm, w), lambda i: (jnp.minimum(i, ncb - 1), 0))
    lat = lambda w: pl.BlockSpec((tm, w), lambda i: (jnp.maximum(i - ncb, 0), 0))
    full = lambda shape: pl.BlockSpec(shape, lambda i: (0,) * len(shape))
    out = lambda w: pl.BlockSpec((tm, w), lambda i: (i, 0))
    mod_idx = lambda i: (jnp.where(i < ncb, 0, 1 + jnp.maximum(i - ncb, 0) // npb), 0, 0)
    n_all = n_ctx + n_lat
    return pl.pallas_call(
        functools.partial(_out_kernel, ncb),
        grid=(ncb + nlb,),
        in_specs=[ctx(D_MODEL), lat(D_MODEL),
                  pl.BlockSpec((tm, D_MODEL), lambda i: (jnp.maximum(i - ncb, 0) % npb, 0)),
                  ctx(D_HY), lat(D_HY), ctx(D_S5), lat(D_S5),
                  pl.BlockSpec((1, 1, 6 * D_MODEL), mod_idx),
                  full((D_S5, D_S5)), full((1, D_S5)), full((1, D_MODEL)), full((D_MODEL, D_MODEL)),
                  full((1, D_MODEL)), full((1, D_MODEL)), full((D_MODEL, LANES)), full((D_MODEL, LANES)),
                  full((1, LANES))],
        out_specs=[out(D_MODEL), out(D_MODEL), out(LANES), pl.BlockSpec((1, 1, LANES), lambda i: (i, 0, 0))],
        out_shape=[jax.ShapeDtypeStruct((n_all, D_MODEL), F32),
                   jax.ShapeDtypeStruct((n_all, D_MODEL), BF16),
                   jax.ShapeDtypeStruct((n_all, LANES), F32),
                   jax.ShapeDtypeStruct((n_all // tm, 1, LANES), F32)],
        compiler_params=_cparams(("arbitrary",)),
        name="out_proj",
    )(xc, xl, pos, yhy_c, yhy_l, ys5_c, ys5_l, mod.reshape(mod.shape[0], 1, 6 * D_MODEL),
      wglu_bf, bglu, ong, wout_bf, ln1g, ln1b, wr_hi, wr_lo, br)


def _perm_t(gates, loc_ref, s):
    n = gates.shape[0]
    lane = lax.broadcasted_iota(jnp.int32, gates.shape, 1)
    oh = jnp.where((lane >= N_EXPERTS) & (lane < N_EXPERTS + N_EGROUPS), gates, 0.0)
    r = lax.broadcasted_iota(jnp.int32, (n, n), 0)
    c = lax.broadcasted_iota(jnp.int32, (n, n), 1)
    earlier = jnp.where(c < r, 1.0, 0.0).astype(BF16)
    cum = _dot(earlier, oh.astype(BF16))
    rank = jnp.sum(cum * oh, axis=-1, keepdims=True)
    lane1 = lax.broadcasted_iota(jnp.int32, (1, LANES), 1)
    locv = jnp.zeros((1, LANES), F32)
    for grp in range(N_EGROUPS):
        locv = jnp.where(lane1 == N_EXPERTS + grp, loc_ref[N_EGROUPS * s + grp].astype(F32), locv)
    dest = rank + jnp.sum(oh * locv, axis=-1, keepdims=True)
    slot = lax.broadcasted_iota(jnp.int32, (n, MOE_SLOTS), 1).astype(F32)
    return jnp.where(slot == dest, 1.0, 0.0)


def _segment_copies(s, loc_ref, len_ref, off_ref, make):
    for grp in range(N_EGROUPS):
        loc = loc_ref[N_EGROUPS * s + grp]
        off = off_ref[N_EGROUPS * s + grp]
        n_units = len_ref[N_EGROUPS * s + grp] // MOE_UNIT

        def body(i, carry):
            make(pl.multiple_of(loc + MOE_UNIT * i, MOE_UNIT), pl.multiple_of(off + MOE_UNIT * i, MOE_UNIT))
            return carry

        lax.fori_loop(0, n_units, body, 0)


def _moe_sort_kernel(loc_ref, len_ref, off_ref, h_ref, gate_ref, xs_in, gs_in, xs_hbm, gs_hbm,
                     xs_v, gs_v, sem):
    del xs_in, gs_in
    s = pl.program_id(0)
    slot = s % 2
    gates = gate_ref[...]
    p = _perm_t(gates, loc_ref, s).T.astype(BF16)
    xs_v[slot] = _dot(p, h_ref[...]).astype(BF16)
    g_hi = gates.astype(BF16)
    r1 = gates - g_hi.astype(F32)
    g_mid = r1.astype(BF16)
    g_lo = (r1 - g_mid.astype(F32)).astype(BF16)
    gs_v[slot] = _dot(p, g_hi) + _dot(p, g_mid) + _dot(p, g_lo)

    def copies(buf):
        def x_copy(lr, gr):
            return pltpu.make_async_copy(xs_v.at[buf, pl.ds(lr, MOE_UNIT), :],
                                         xs_hbm.at[pl.ds(gr, MOE_UNIT), :], sem.at[0, buf])

        def g_copy(lr, gr):
            return pltpu.make_async_copy(gs_v.at[buf, pl.ds(lr, MOE_UNIT), :],
                                         gs_hbm.at[pl.ds(gr, MOE_UNIT), :], sem.at[1, buf])

        def start(lr, gr):
            x_copy(lr, gr).start()
            g_copy(lr, gr).start()

        def wait(lr, gr):
            x_copy(lr, gr).wait()
            g_copy(lr, gr).wait()

        return start, wait

    _segment_copies(s, loc_ref, len_ref, off_ref, copies(slot)[0])

    @pl.when(s > 0)
    def _():
        _segment_copies(s - 1, loc_ref, len_ref, off_ref, copies(1 - slot)[1])

    @pl.when(s == pl.num_programs(0) - 1)
    def _():
        _segment_copies(s, loc_ref, len_ref, off_ref, copies(slot)[1])


def _moe_expert_kernel(bg_ref, nb_ref, xs_ref, gs_ref, wg_ref, wu_ref, wd_ref, o_ref):
    i = pl.program_id(0)

    @pl.when(i < nb_ref[0])
    def _():
        grp = bg_ref[i]
        x = xs_ref[...]
        gates = gs_ref[...]
        lane = lax.broadcasted_iota(jnp.int32, gates.shape, 1)
        acc = jnp.zeros(o_ref.shape, F32)
        for e in range(N_EPG):
            a = _dot(x, wg_ref[e])
            u = _dot(x, wu_ref[e])
            ge = jnp.sum(jnp.where(lane == N_EPG * grp + e, gates, 0.0), axis=-1, keepdims=True)
            hid = jax.nn.silu(a) * u * ge
            acc = acc + _dot(hid.astype(BF16), wd_ref[e])
        o_ref[...] = acc.astype(BF16)

    @pl.when(i >= nb_ref[0])
    def _():
        o_ref[...] = jnp.zeros_like(o_ref)


def _moe_combine_kernel(loc_ref, len_ref, off_ref, gate_ref, x1_ref, mod_ref, ln2g_ref, ln2b_ref, o_hbm,
                        ctx_ref, lat_ref, o_v, sem, *, n_ctx_tiles):
    s = pl.program_id(0)
    slot = s % 2

    def copies(buf):
        def o_copy(lr, gr):
            return pltpu.make_async_copy(o_hbm.at[pl.ds(gr, MOE_UNIT), :],
                                         o_v.at[buf, pl.ds(lr, MOE_UNIT), :], sem.at[buf])

        return (lambda lr, gr: o_copy(lr, gr).start()), (lambda lr, gr: o_copy(lr, gr).wait())

    @pl.when(s == 0)
    def _():
        o_v[...] = jnp.zeros_like(o_v)
        _segment_copies(s, loc_ref, len_ref, off_ref, copies(slot)[0])

    @pl.when(s + 1 < pl.num_programs(0))
    def _():
        _segment_copies(s + 1, loc_ref, len_ref, off_ref, copies(1 - slot)[0])

    pt = _perm_t(gate_ref[...], loc_ref, s).astype(BF16)
    _segment_copies(s, loc_ref, len_ref, off_ref, copies(slot)[1])
    f = _dot(pt, o_v[slot])
    g2 = mod_ref[0, :, 5 * D_MODEL:6 * D_MODEL]
    x2 = _norm(ALPHA * x1_ref[...] + g2 * f) * ln2g_ref[...] + ln2b_ref[...]

    @pl.when(s < n_ctx_tiles)
    def _():
        ctx_ref[...] = x2

    @pl.when(s >= n_ctx_tiles)
    def _():
        lat_ref[...] = x2


def _moe_plan(tile_counts, n_blocks):
    cnt = tile_counts[:, 0, N_EXPERTS:N_EXPERTS + N_EGROUPS].astype(jnp.int32)
    len16 = ((cnt + MOE_UNIT - 1) // MOE_UNIT) * MOE_UNIT
    loc = jnp.cumsum(len16, axis=1) - len16
    rows_g = jnp.sum(len16, axis=0)
    reg_g = ((rows_g + MOE_TM - 1) // MOE_TM) * MOE_TM
    reg_start = jnp.cumsum(reg_g) - reg_g
    off = reg_start[None, :] + jnp.cumsum(len16, axis=0) - len16
    blk_end = jnp.cumsum(reg_g // MOE_TM)
    bi = jnp.arange(n_blocks, dtype=jnp.int32)
    blk_group = jnp.minimum(jnp.sum((bi[:, None] >= blk_end[None, :]).astype(jnp.int32), axis=1),
                            N_EGROUPS - 1)
    flat = lambda a: a.reshape(-1).astype(jnp.int32)
    return flat(loc), flat(len16), flat(off), blk_group.astype(jnp.int32), blk_end[-1:].astype(jnp.int32)


def _moe(h2_all, gates_all, tile_counts, x1_all, mod, wg_bf, wu_bf, wd_bf, ln2g, ln2b, n_ctx,
         tokens_per_mod_row):
    n_tok = h2_all.shape[0]
    n_tiles = n_tok // MOE_ST
    n_ctx_tiles = n_ctx // MOE_ST
    max_rows = n_tok + n_tiles * N_EGROUPS * (MOE_UNIT - 1) + N_EGROUPS * (MOE_TM - 1)
    n_blocks = -(-max_rows // MOE_TM)
    n_rows = n_blocks * MOE_TM
    loc, len16, off, blk_group, n_used = _moe_plan(tile_counts, n_blocks)

    tile = lambda w: pl.BlockSpec((MOE_ST, w), lambda s, *_: (s, 0))
    anyspec = pl.BlockSpec(memory_space=pl.ANY)
    xs, gs = pl.pallas_call(
        _moe_sort_kernel,
        grid_spec=pltpu.PrefetchScalarGridSpec(
            num_scalar_prefetch=3, grid=(n_tiles,),
            in_specs=[tile(D_MODEL), tile(LANES), anyspec, anyspec],
            out_specs=[anyspec, anyspec],
            scratch_shapes=[pltpu.VMEM((2, MOE_SLOTS, D_MODEL), BF16), pltpu.VMEM((2, MOE_SLOTS, LANES), F32),
                            pltpu.SemaphoreType.DMA((2, 2))]),
        out_shape=[jax.ShapeDtypeStruct((n_rows, D_MODEL), BF16),
                   jax.ShapeDtypeStruct((n_rows, LANES), F32)],
        input_output_aliases={5: 0, 6: 1},
        compiler_params=_cparams(("arbitrary",)),
        name="moe_sort",
    )(loc, len16, off, h2_all, gates_all, jnp.zeros((n_rows, D_MODEL), BF16), jnp.zeros((n_rows, LANES), F32))

    blk = lambda w: pl.BlockSpec((MOE_TM, w), lambda i, bg, nb: (jnp.minimum(i, nb[0] - 1), 0))
    wspec = lambda a, b: pl.BlockSpec((N_EPG, a, b), lambda i, bg, nb: (bg[i], 0, 0))
    o_sorted = pl.pallas_call(
        _moe_expert_kernel,
        grid_spec=pltpu.PrefetchScalarGridSpec(
            num_scalar_prefetch=2, grid=(n_blocks,),
            in_specs=[blk(D_MODEL), blk(LANES), wspec(D_MODEL, D_EXPERT), wspec(D_MODEL, D_EXPERT),
                      wspec(D_EXPERT, D_MODEL)],
            out_specs=pl.BlockSpec((MOE_TM, D_MODEL), lambda i, bg, nb: (i, 0))),
        out_shape=jax.ShapeDtypeStruct((n_rows, D_MODEL), BF16),
        compiler_params=_cparams(("arbitrary",)),
        name="moe_experts",
    )(blk_group, n_used, xs, gs, wg_bf, wu_bf, wd_bf)

    lat_per_row = tokens_per_mod_row // MOE_ST

    def mod_idx(s, *_):
        return (jnp.where(s < n_ctx_tiles, 0, 1 + (s - n_ctx_tiles) // lat_per_row), 0, 0)

    vec = pl.BlockSpec((1, D_MODEL), lambda s, *_: (0, 0))
    return pl.pallas_call(
        functools.partial(_moe_combine_kernel, n_ctx_tiles=n_ctx_tiles),
        grid_spec=pltpu.PrefetchScalarGridSpec(
            num_scalar_prefetch=3, grid=(n_tiles,),
            in_specs=[tile(LANES), tile(D_MODEL), pl.BlockSpec((1, 1, 6 * D_MODEL), mod_idx), vec, vec,
                      anyspec],
            out_specs=[pl.BlockSpec((MOE_ST, D_MODEL), lambda s, *_: (jnp.minimum(s, n_ctx_tiles - 1), 0)),
                       pl.BlockSpec((MOE_ST, D_MODEL), lambda s, *_: (jnp.maximum(s - n_ctx_tiles, 0), 0))],
            scratch_shapes=[pltpu.VMEM((2, MOE_SLOTS, D_MODEL), BF16), pltpu.SemaphoreType.DMA((2,))]),
        out_shape=[jax.ShapeDtypeStruct((n_ctx, D_MODEL), F32),
                   jax.ShapeDtypeStruct((n_tok - n_ctx, D_MODEL), F32)],
        compiler_params=_cparams(("arbitrary",)),
        name="moe_combine",
    )(loc, len16, off, gates_all, x1_all, mod.reshape(mod.shape[0], 1, 6 * D_MODEL), ln2g, ln2b, o_sorted)


def _grid_pos_embed(n_tokens):
    rows = n_tokens // GRID_W
    row = np.repeat(np.arange(rows, dtype=np.float64), GRID_W)
    col = np.tile(np.arange(GRID_W, dtype=np.float64), rows)
    quarter = D_MODEL // 4
    omega = 1.0 / (POS_BASE ** (np.arange(quarter, dtype=np.float64) / quarter))
    er = row[:, None] * omega
    ec = col[:, None] * omega
    return jnp.asarray(np.concatenate([np.sin(er), np.cos(er), np.sin(ec), np.cos(ec)], axis=-1), F32)


def _mixers(x, pos, mod3, h0_re, h0_im, tabs, filt, s5ops, wts, tm):
    bsz, n_tok, _ = x.shape
    shared = mod3.shape[0] == 1
    x3 = x.reshape(1, bsz * n_tok, D_MODEL) if shared else x
    proj_hy, u_s5 = _in_proj(x3, pos, mod3, wts['w_in'], tm)
    y_hy = _hyena(proj_hy.reshape(bsz, n_tok, 3 * D_HY), tabs, filt,
                  wts['hy_conv_w'], wts['hy_conv_b'], wts['hy_fbias'])
    y_s5, f_re, f_im = _s5(u_s5.reshape(bsz, n_tok, D_S5), s5ops, h0_re, h0_im)
    return y_hy.reshape(bsz * n_tok, D_HY), y_s5.reshape(bsz * n_tok, D_S5), f_re, f_im


def kernel(x_prompt, x_sample, state_s5_re, state_s5_im, c, c_ctx, w_ada, b_ada, w_in, hy_conv_w, hy_conv_b, hy_f_w1, hy_f_b1, hy_f_w2, hy_f_b2, hy_f_w3, hy_freq, hy_fbias, s5_a_re, s5_a_im, s5_log_dt, s5_b_re, s5_b_im, s5_c_re, s5_c_im, s5_d, s5_w_glu, s5_b_glu, out_norm_g, w_out, ln1_g, ln1_b, moe_w_r1, moe_b_r1, moe_w_r2, moe_b_r2, moe_w_gate, moe_w_up, moe_w_down, ln2_g, ln2_b):
    b_ctx, l_ctx, _ = x_prompt.shape
    b_lat, l_lat, _ = x_sample.shape
    g, p = S5_GROUPS, S5_STATE
    assert w_ada.shape[0] == 1, "single-layer trunk"
    l = 0

    nrow = 16
    cond = jnp.concatenate([c_ctx[None, :], c, jnp.zeros((nrow - 1 - b_lat, D_MODEL), F32)], axis=0)
    mod = _ada(cond, w_ada[l], b_ada[l])
    mod_ctx = mod[0:1].reshape(1, 1, 6 * D_MODEL)
    mod_lat = mod[1:1 + b_lat].reshape(b_lat, 1, 6 * D_MODEL)

    wr = jnp.concatenate([moe_w_r2[l].transpose(1, 0, 2).reshape(D_MODEL, N_EXPERTS), moe_w_r1[l]], axis=1)
    wr = jnp.pad(wr, ((0, 0), (0, LANES - wr.shape[1])))
    br = jnp.concatenate([moe_b_r2[l].reshape(-1), moe_b_r1[l]])
    br = jnp.pad(br, (0, LANES - br.shape[0])).reshape(1, LANES)
    wr_hi, wr_lo = _split(wr)

    wts = {
        'w_in': w_in[l].astype(BF16), 'hy_conv_w': hy_conv_w[l], 'hy_conv_b': hy_conv_b[l],
        'hy_fbias': hy_fbias[l], 'w_glu': s5_w_glu[l].astype(BF16), 'b_glu': s5_b_glu[l].reshape(1, -1),
        'out_norm_g': out_norm_g[l].reshape(1, -1), 'w_out': w_out[l].astype(BF16),
        'ln1_g': ln1_g[l].reshape(1, -1), 'ln1_b': ln1_b[l].reshape(1, -1),
        'wr_hi': wr_hi, 'wr_lo': wr_lo, 'br': br,
        'w_gate': moe_w_gate[l].astype(BF16), 'w_up': moe_w_up[l].astype(BF16),
        'w_down': moe_w_down[l].astype(BF16),
        'ln2_g': ln2_g[l].reshape(1, -1), 'ln2_b': ln2_b[l].reshape(1, -1),
    }

    s5ops = _s5_operators(s5_a_re[l], s5_a_im[l], s5_log_dt[l], s5_b_re[l], s5_b_im[l],
                          s5_c_re[l], s5_c_im[l], s5_d[l])
    tabs_ctx = _tables(l_ctx)
    tabs_lat = _tables(l_lat)
    filt_args = (hy_f_w1[l], hy_f_b1[l], hy_f_w2[l], hy_f_b2[l], hy_f_w3[l], hy_freq[l])
    filt_ctx = _hyena_filters(l_ctx, tabs_ctx, *filt_args)
    filt_lat = _hyena_filters(l_lat, tabs_lat, *filt_args)

    zero = jnp.zeros((g, b_ctx, 2 * p), F32)
    yhy_c, ys5_c, f_re, f_im = _mixers(x_prompt, None, mod_ctx, zero, zero, tabs_ctx, filt_ctx, s5ops, wts, 512)
    unpack = lambda f: f.reshape(g, b_ctx, 2, p).transpose(1, 2, 0, 3)[:, None]
    new_re, new_im = unpack(f_re), unpack(f_im)

    pack = lambda s: s[:, l].transpose(2, 0, 1, 3).reshape(g, b_lat, 2 * p)
    pos = _grid_pos_embed(l_lat)
    yhy_l, ys5_l, _, _ = _mixers(x_sample, pos, mod_lat, pack(state_s5_re), pack(state_s5_im),
                                 tabs_lat, filt_lat, s5ops, wts, 512)

    n_ctx = b_ctx * l_ctx
    x1_all, h2_all, gates_all, tile_counts = _out_proj(
        x_prompt.reshape(n_ctx, D_MODEL), x_sample.reshape(b_lat * l_lat, D_MODEL), pos,
        yhy_c, yhy_l, ys5_c, ys5_l, mod, wts['w_glu'], wts['b_glu'], wts['out_norm_g'], wts['w_out'],
        wts['ln1_g'], wts['ln1_b'], wts['wr_hi'], wts['wr_lo'], wts['br'], MOE_ST)
    y_ctx, y_lat = _moe(h2_all, gates_all, tile_counts, x1_all, mod,
                        wts['w_gate'], wts['w_up'], wts['w_down'], wts['ln2_g'], wts['ln2_b'],
                        n_ctx, l_lat)
    return (y_ctx.reshape(x_prompt.shape), y_lat.reshape(x_sample.shape), new_re, new_im)
```

```python
import functools
import math

import numpy as np
import jax
import jax.numpy as jnp
from jax import lax
from jax.experimental import pallas as pl
from jax.experimental.pallas import tpu as pltpu

F32 = jnp.float32
BF16 = jnp.bfloat16

D_MODEL = 1024
DEPTH = 1
GRID_W = 64
POS_BASE = 10000.0
D_HY = 512
D_S5 = 512
S5_CH = 16
S5_GROUPS = 32
S5_STATE = 64
S5_CHUNK = 16
S5_ROW = S5_CHUNK * S5_CH
HY_BANDS = 16
HY_EMB = 1 + 2 * HY_BANDS
HY_HID = 64
HY_MIN_DECAY = math.log(1e-2) / 1.5
HY_MAX_DECAY = math.log(1e-2) / 0.3
N_EGROUPS = 4
N_EPG = 4
N_EXPERTS = 16
D_EXPERT = 512
LN_EPS = 1e-5
ALPHA = (2.0 * DEPTH) ** 0.25
LANES = 128
S5_GB = LANES // S5_CH
S5OPS_GB = 4
HY_CW = 512
MOE_ST = 512
MOE_SLOTS = 640
MOE_UNIT = 16
MOE_TM = 512
VMEM_LIMIT = 60000 * 1024


def _cparams(sem):
    return pltpu.CompilerParams(dimension_semantics=sem, vmem_limit_bytes=VMEM_LIMIT)


def _split(x):
    hi = x.astype(BF16)
    lo = (x - hi.astype(F32)).astype(BF16)
    return hi, lo


def _dot(a, b):
    return jnp.dot(a, b, preferred_element_type=F32)


def _dot_t(a, b):
    return lax.dot_general(a, b, (((1,), (1,)), ((), ())), preferred_element_type=F32)


def _mm3(a, b):
    ah, al = _split(a)
    bh, bl = _split(b)
    return _dot(ah, bh) + _dot(al, bh) + _dot(ah, bl)


def _mm3_t(a, b):
    ah, al = _split(a)
    bh, bl = _split(b)
    return _dot_t(ah, bh) + _dot_t(al, bh) + _dot_t(ah, bl)


def _norm(x):
    xc = x - jnp.mean(x, axis=-1, keepdims=True)
    return xc * lax.rsqrt(jnp.mean(xc * xc, axis=-1, keepdims=True) + LN_EPS)


def _rms(y):
    return y * lax.rsqrt(jnp.mean(y * y, axis=-1, keepdims=True) + LN_EPS)


def _ada_kernel(cond_ref, w_ref, b_ref, o_ref):
    c = jax.nn.silu(cond_ref[...])
    o_ref[...] = _mm3(c, w_ref[...]) + b_ref[...]


def _ada(cond, w_ada, b_ada):
    nb = cond.shape[0]
    n = w_ada.shape[1]
    tn = 1024
    return pl.pallas_call(
        _ada_kernel,
        grid=(n // tn,),
        in_specs=[pl.BlockSpec((nb, D_MODEL), lambda j: (0, 0)),
                  pl.BlockSpec((D_MODEL, tn), lambda j: (0, j)),
                  pl.BlockSpec((1, tn), lambda j: (0, j))],
        out_specs=pl.BlockSpec((nb, tn), lambda j: (0, j)),
        out_shape=jax.ShapeDtypeStruct((nb, n), F32),
        compiler_params=_cparams(("arbitrary",)),
        name="ada",
    )(cond, w_ada, b_ada.reshape(1, n))


def _dft_tables(n_half):
    n = 2 * n_half
    idx = np.arange(n_half, dtype=np.int64)
    m = (idx[:, None] * idx[None, :]) % n
    ang = 2.0 * np.pi * m.astype(np.float64) / n
    cm = np.cos(ang)
    sm = -np.sin(ang)
    sm[0, :] = 1.0 - 2.0 * (idx % 2)
    return cm.astype(np.float32), sm.astype(np.float32)


def _tables(n_tok):
    n_half = n_tok // 2
    cm, sm = _dft_tables(n_half)
    mats = tuple(jnp.asarray(t).astype(BF16) for t in (cm, sm, np.ascontiguousarray(sm.T)))
    ang = np.pi * np.arange(n_half, dtype=np.float64) / n_tok
    tw = [np.broadcast_to(v[:, None], (n_half, HY_CW)).astype(np.float32) for v in (np.cos(ang), -np.sin(ang))]
    return mats + (jnp.asarray(tw[0]), jnp.asarray(tw[1]))


def _put_cols(ref, x):
    for j in range(ref.shape[0]):
        ref[j] = x[:, LANES * j:LANES * (j + 1)]


def _get_cols(ref):
    return jnp.concatenate([ref[j] for j in range(ref.shape[0])], axis=1)


def _get_parity(ref, parity):
    n_half = ref.shape[1] // 2
    return jnp.concatenate([ref[j, pl.ds(parity, n_half, stride=2), :] for j in range(ref.shape[0])], axis=1)


def _put_parity(ref, parity, x):
    n_half = ref.shape[1] // 2
    for j in range(ref.shape[0]):
        ref[j, pl.ds(parity, n_half, stride=2), :] = x[:, LANES * j:LANES * (j + 1)]


def _set_row0(x, v):
    first = lax.broadcasted_iota(jnp.int32, (8, x.shape[1]), 0) == 0
    return jnp.concatenate([jnp.where(first, v, x[:8]), x[8:]], axis=0)


def _rfft_packed(x_ref, cm, sm, tw_re, tw_im):
    xe = _get_parity(x_ref, 0).astype(BF16)
    xo = _get_parity(x_ref, 1).astype(BF16)
    e_re, e_im = _dot(cm, xe), _dot(sm, xe)
    o_re, o_im = _dot(cm, xo), _dot(sm, xo)
    t_re = tw_re * o_re - tw_im * o_im
    t_im = tw_re * o_im + tw_im * o_re
    a_im = _set_row0(e_im + t_im, e_im[0:1])
    b_im = _set_row0(t_im - e_im, -o_im[0:1])
    return e_re + t_re, a_im, e_re - t_re, b_im


def _irfft_packed(y_ref, ya_re, ya_im, yb_re, yb_im, cm, st, tw_re, tw_im):
    p_e = ya_re + yb_re
    q_e = _set_row0(ya_im - yb_im, ya_im[0:1])
    _put_parity(y_ref, 0, _dot(cm, p_e.astype(BF16)) + _dot(st, q_e.astype(BF16)))
    ra_re = ya_re * tw_re + ya_im * tw_im
    ra_im = ya_im * tw_re - ya_re * tw_im
    rb_re = yb_im * tw_im - yb_re * tw_re
    rb_im = -(yb_re * tw_im + yb_im * tw_re)
    p_o = ra_re + rb_re
    q_o = _set_row0(ra_im - rb_im, -yb_im[0:1])
    _put_parity(y_ref, 1, _dot(cm, p_o.astype(BF16)) + _dot(st, q_o.astype(BF16)))


def _filt_kernel(n_tok, z_ref, t_ref, w1_ref, b1_ref, w2_ref, b2_ref, fr_ref, w3f_ref, w3b_ref,
                 dl_ref, cm_ref, sm_ref, twr_ref, twi_ref, kar_ref, kai_ref, kbr_ref, kbi_ref, p_ref, q_ref):
    fr = fr_ref[...]
    h = jnp.sin(fr * (_mm3(z_ref[...], w1_ref[...]) + b1_ref[...]))
    h = jnp.sin(fr * (_mm3(h, w2_ref[...]) + b2_ref[...]))
    decay = jnp.exp(-t_ref[...] * dl_ref[...])
    row = lax.broadcasted_iota(jnp.int32, decay.shape, 0)
    hf = _mm3(h, w3f_ref[...]) * decay
    hb = jnp.where(row == 0, 0.0, _mm3(h, w3b_ref[...]) * decay)
    _put_cols(p_ref, hf + hb)
    _put_cols(q_ref, hf - hb)
    cm, sm, tw_re, tw_im = cm_ref[...], sm_ref[...], twr_ref[...], twi_ref[...]
    row0 = lax.broadcasted_iota(jnp.int32, tw_re.shape, 0) == 0
    pa_re, pa_im, pb_re, _ = _rfft_packed(p_ref, cm, sm, tw_re, tw_im)
    _, qa_im, _, qb_im = _rfft_packed(q_ref, cm, sm, tw_re, tw_im)
    inv_n = 1.0 / (2 * n_tok)
    w_re = jnp.where(row0, inv_n, 2.0 * inv_n)
    kar_ref[...] = w_re * pa_re
    kbr_ref[...] = w_re * pb_re
    kai_ref[...] = (2.0 * inv_n) * _set_row0(qa_im, pa_im[0:1])
    kbi_ref[...] = (2.0 * inv_n) * qb_im


def _hyena_filters(n_tok, tabs, hy_f_w1, hy_f_b1, hy_f_w2, hy_f_b2, hy_f_w3, hy_freq):
    cm, sm, _, tw_re, tw_im = tabs
    n_half = n_tok // 2
    t = jnp.linspace(0.0, 1.0, n_tok, dtype=F32)[:, None]
    wv = 2.0 * math.pi * jnp.arange(n_tok, dtype=F32) / n_tok
    fb = jnp.linspace(1e-4, HY_BANDS - 1, HY_BANDS, dtype=F32)
    ang = wv[:, None] * fb[None, :]
    z = jnp.concatenate([t, jnp.cos(ang), -jnp.sin(ang)], axis=-1)
    z = jnp.pad(z, ((0, 0), (0, LANES - HY_EMB)))
    w1 = jnp.pad(hy_f_w1, ((0, LANES - HY_EMB), (0, 0)))
    deltas = jnp.abs(jnp.linspace(HY_MIN_DECAY, HY_MAX_DECAY, D_HY, dtype=F32))[None, :]
    ncb = D_HY // HY_CW
    full = lambda j: (0, 0)
    out_sd = jax.ShapeDtypeStruct((n_half, 2 * D_HY), F32)
    mat = pl.BlockSpec((n_half, n_half), full, pipeline_mode=pl.Buffered(1))
    twb = pl.BlockSpec((n_half, HY_CW), full, pipeline_mode=pl.Buffered(1))
    return pl.pallas_call(
        functools.partial(_filt_kernel, n_tok),
        grid=(2 * ncb,),
        in_specs=[pl.BlockSpec((n_tok, LANES), full),
                  pl.BlockSpec((n_tok, 1), full),
                  pl.BlockSpec((LANES, HY_HID), full),
                  pl.BlockSpec((1, HY_HID), full),
                  pl.BlockSpec((HY_HID, HY_HID), full),
                  pl.BlockSpec((1, HY_HID), full),
                  pl.BlockSpec((1, HY_HID), full),
                  pl.BlockSpec((HY_HID, HY_CW), lambda j: (0, 2 * ncb * (j // ncb) + j % ncb)),
                  pl.BlockSpec((HY_HID, HY_CW), lambda j: (0, 2 * ncb * (j // ncb) + ncb + j % ncb)),
                  pl.BlockSpec((1, HY_CW), lambda j: (0, j % ncb)),
                  mat, mat, twb, twb],
        out_specs=[pl.BlockSpec((n_half, HY_CW), lambda j: (0, j))] * 4,
        out_shape=[out_sd] * 4,
        scratch_shapes=[pltpu.VMEM((HY_CW // LANES, n_tok, LANES), F32)] * 2,
        compiler_params=_cparams(("arbitrary",)),
        name=f"filt{n_tok}",
    )(z, t, w1, hy_f_b1.reshape(1, -1), hy_f_w2, hy_f_b2.reshape(1, -1), hy_freq.reshape(1, -1),
      hy_f_w3, hy_f_w3, deltas, cm, sm, tw_re, tw_im)


def _hyena_kernel(pv_ref, p1_ref, p2_ref, cwv_ref, cw1_ref, cw2_ref, cbv_ref, cb1_ref, cb2_ref,
                  fbias_ref, cm_ref, sm_ref, st_ref, twr_ref, twi_ref,
                  kar0_ref, kai0_ref, kbr0_ref, kbi0_ref, kar1_ref, kai1_ref, kbr1_ref, kbi1_ref,
                  o_ref, u_ref, y_ref):
    n_tok = pv_ref.shape[1]
    row = lax.broadcasted_iota(jnp.int32, (n_tok, pv_ref.shape[2]), 0)

    def short_conv(p_ref, cw_ref, cb_ref):
        p = p_ref[0]
        prev = jnp.where(row == 0, 0.0, pltpu.roll(p, 1, axis=0))
        nxt = jnp.where(row == n_tok - 1, 0.0, pltpu.roll(p, n_tok - 1, axis=0))
        return cb_ref[...] + prev * cw_ref[0:1, :] + p * cw_ref[1:2, :] + nxt * cw_ref[2:3, :]

    cm, sm, st, tw_re, tw_im = cm_ref[...], sm_ref[...], st_ref[...], twr_ref[...], twi_ref[...]

    def fftconv(u, kar_ref, kai_ref, kbr_ref, kbi_ref, skip):
        _put_cols(u_ref, u)
        ua_re, ua_im, ub_re, ub_im = _rfft_packed(u_ref, cm, sm, tw_re, tw_im)
        ka_re, ka_im, kb_re, kb_im = kar_ref[...], kai_ref[...], kbr_ref[...], kbi_ref[...]
        zero_row = jnp.zeros_like(ka_im[0:1])
        kaz = _set_row0(ka_im, zero_row)
        kbz = _set_row0(kb_im, zero_row)
        ya_re = ua_re * ka_re - ua_im * kaz
        yb_re = ub_re * kb_re - ub_im * kbz
        h_re = ua_im[0:1] * ka_im[0:1] - ub_im[0:1] * kb_im[0:1]
        h_im = ua_im[0:1] * kb_im[0:1] + ub_im[0:1] * ka_im[0:1]
        ya_im = _set_row0(ua_re * ka_im + ua_im * ka_re, h_re)
        yb_im = _set_row0(ub_re * kb_im + ub_im * kb_re, h_im)
        _irfft_packed(y_ref, ya_re, ya_im, yb_re, yb_im, cm, st, tw_re, tw_im)
        return _get_cols(y_ref) + u * skip

    v = short_conv(pv_ref, cwv_ref, cbv_ref)
    x1 = short_conv(p1_ref, cw1_ref, cb1_ref)
    z = x1 * fftconv(v, kar0_ref, kai0_ref, kbr0_ref, kbi0_ref, fbias_ref[0:1, :])
    x2 = short_conv(p2_ref, cw2_ref, cb2_ref)
    o_ref[0] = x2 * fftconv(z, kar1_ref, kai1_ref, kbr1_ref, kbi1_ref, fbias_ref[1:2, :])


def _hyena(proj_hy, tabs, filt, hy_conv_w, hy_conv_b, hy_fbias):
    bsz, n_tok, _ = proj_hy.shape
    n_half = n_tok // 2
    ncb = D_HY // HY_CW
    cm, sm, st, tw_re, tw_im = tabs
    cb = hy_conv_b.reshape(1, -1)
    const = lambda shape: pl.BlockSpec(shape, lambda b, c: (0, 0), pipeline_mode=pl.Buffered(1))
    mat = const((n_half, n_half))
    twb = const((n_half, HY_CW))

    def pspec(k):
        return pl.BlockSpec((1, n_tok, HY_CW), lambda b, c: (b, 0, k * ncb + c))

    def cwspec(k):
        return pl.BlockSpec((3, HY_CW), lambda b, c: (0, k * ncb + c))

    def cbspec(k):
        return pl.BlockSpec((1, HY_CW), lambda b, c: (0, k * ncb + c))

    def fspec(o):
        mode = pl.Buffered(1) if ncb == 1 else None
        return pl.BlockSpec((n_half, HY_CW), lambda b, c: (0, o * ncb + c), pipeline_mode=mode)

    return pl.pallas_call(
        _hyena_kernel,
        grid=(bsz, ncb),
        in_specs=[pspec(0), pspec(1), pspec(2), cwspec(0), cwspec(1), cwspec(2),
                  cbspec(0), cbspec(1), cbspec(2),
                  pl.BlockSpec((2, HY_CW), lambda b, c: (0, c)),
                  mat, mat, mat, twb, twb] + [fspec(0)] * 4 + [fspec(1)] * 4,
        out_specs=pl.BlockSpec((1, n_tok, HY_CW), lambda b, c: (b, 0, c)),
        out_shape=jax.ShapeDtypeStruct((bsz, n_tok, D_HY), F32),
        scratch_shapes=[pltpu.VMEM((HY_CW // LANES, n_tok, LANES), F32)] * 2,
        compiler_params=_cparams(("arbitrary", "arbitrary")),
        name=f"hyena{n_tok}",
    )(proj_hy, proj_hy, proj_hy, hy_conv_w, hy_conv_w, hy_conv_w, cb, cb, cb, hy_fbias,
      cm, sm, st, tw_re, tw_im, *filt, *filt)


def _s5ops_kernel(*refs):
    for g in range(S5OPS_GB):
        _s5ops_group(g, *refs)


def _s5ops_group(g, are_ref, aim_ref, ldt_ref, btr_ref, bti_ref, cre_ref, cim_ref, d_ref,
                 mt_ref, erh_ref, erl_ref, eih_ref, eil_ref, gr_ref, gi_ref, atr_ref, ati_ref,
                 er_ref, ei_ref):
    a_re, a_im = are_ref[g], aim_ref[g]
    dt = jnp.exp(ldt_ref[g])
    mag = jnp.exp(a_re * dt)
    ab_re = mag * jnp.cos(a_im * dt)
    ab_im = mag * jnp.sin(a_im * dt)
    n_re, n_im = ab_re - 1.0, ab_im
    den = a_re * a_re + a_im * a_im
    q_re = (n_re * a_re + n_im * a_im) / den
    q_im = (n_im * a_re - n_re * a_im) / den
    bt_re, bt_im = btr_ref[g], bti_ref[g]
    bb_re = q_re * bt_re - q_im * bt_im
    bb_im = q_re * bt_im + q_im * bt_re
    c_re, c_im = cre_ref[g, 0:S5_CH, :], cim_ref[g, 0:S5_CH, :]
    pw = [(jnp.ones_like(ab_re), jnp.zeros_like(ab_re))]
    for _ in range(S5_CHUNK):
        pr, pi = pw[-1]
        pw.append((pr * ab_re - pi * ab_im, pr * ab_im + pi * ab_re))
    lane = lax.broadcasted_iota(jnp.int32, ab_re.shape, 1)
    fwd = lane < S5_STATE
    for s in range(S5_CHUNK):
        e_re = jnp.where(fwd, pw[S5_CHUNK - 1 - s][0], pw[s][0])
        e_im = jnp.where(fwd, pw[S5_CHUNK - 1 - s][1], pw[s][1])
        er_ref[g, pl.ds(S5_CH * s, S5_CH), :] = e_re * bb_re - e_im * bb_im
        ei_ref[g, pl.ds(S5_CH * s, S5_CH), :] = e_re * bb_im + e_im * bb_re
        g_re = jnp.where(fwd, pw[s + 1][0], pw[S5_CHUNK - s][0])
        g_im = jnp.where(fwd, pw[s + 1][1], pw[S5_CHUNK - s][1])
        gr_ref[g, pl.ds(S5_CH * s, S5_CH), :] = (c_re * g_re - c_im * g_im).astype(BF16)
        gi_ref[g, pl.ds(S5_CH * s, S5_CH), :] = (-(c_re * g_im + c_im * g_re)).astype(BF16)
    atr_ref[g] = pw[S5_CHUNK][0]
    ati_ref[g] = pw[S5_CHUNK][1]
    er, ei = er_ref[g], ei_ref[g]
    erh_ref[g], erl_ref[g] = _split(er)
    eih_ref[g], eil_ref[g] = _split(ei)
    lane2 = lax.broadcasted_iota(jnp.int32, er.shape, 1)
    row2 = lax.broadcasted_iota(jnp.int32, er.shape, 0)
    f2 = lane2 < S5_STATE
    zero = jnp.zeros_like(er)

    cp_re, cp_im = cre_ref[g], cim_ref[g]
    kf = _mm3_t(jnp.where(f2, er, zero), cp_re) - _mm3_t(jnp.where(f2, ei, zero), cp_im)
    kb = _mm3_t(jnp.where(f2, zero, er), cp_re) - _mm3_t(jnp.where(f2, zero, ei), cp_im)
    d_row = d_ref[g]
    steps_per_vreg = LANES // S5_CH
    for half in range(S5_CHUNK // steps_per_vreg):
        acc = zero
        for tt in range(steps_per_vreg):
            t = half * steps_per_vreg + tt
            nf = S5_CH * (S5_CHUNK - 1 - t)
            nb = S5_CH * t
            col_f = jnp.concatenate([kf[nf:], zero[:nf]], axis=0) if nf else kf
            col_b = jnp.concatenate([zero[:nb], kb[:S5_ROW - nb]], axis=0) if nb else kb
            diag = jnp.where((row2 // S5_CH == t) & (row2 % S5_CH == lane2), d_row, 0.0)
            col = col_f + col_b + diag
            r = pltpu.roll(col, S5_CH * tt, axis=1) if tt else col
            acc = jnp.where((lane2 >= S5_CH * tt) & (lane2 < S5_CH * (tt + 1)), r, acc)
        mt_ref[g, :, LANES * half:LANES * (half + 1)] = acc.astype(BF16)


def _s5_operators(s5_a_re, s5_a_im, s5_log_dt, s5_b_re, s5_b_im, s5_c_re, s5_c_im, s5_d):
    g, p, h = S5_GROUPS, S5_STATE, S5_CH
    cat = lambda x: jnp.concatenate([x[0], x[1]], axis=-1)
    a_re = cat(s5_a_re).reshape(g, 1, 2 * p)
    a_im = cat(s5_a_im).reshape(g, 1, 2 * p)
    ldt = cat(jnp.broadcast_to(s5_log_dt[:, :, None], (2, g, p))).reshape(g, 1, 2 * p)
    bt_re = cat(jnp.swapaxes(s5_b_re, -1, -2))
    bt_im = cat(jnp.swapaxes(s5_b_im, -1, -2))
    cpad = lambda c: jnp.pad(jnp.concatenate([c, c], axis=-1), ((0, 0), (0, LANES - h), (0, 0)))
    c_re, c_im = cpad(s5_c_re), cpad(s5_c_im)
    d_row = jnp.pad(s5_d.reshape(g, 1, h), ((0, 0), (0, 0), (0, LANES - h)))
    vec = pl.BlockSpec((S5OPS_GB, 1, 2 * p), lambda i: (i, 0, 0))
    hp = pl.BlockSpec((S5OPS_GB, h, 2 * p), lambda i: (i, 0, 0))
    sq = pl.BlockSpec((S5OPS_GB, LANES, 2 * p), lambda i: (i, 0, 0))
    big = pl.BlockSpec((S5OPS_GB, S5_ROW, 2 * p), lambda i: (i, 0, 0))
    mts = pl.BlockSpec((S5OPS_GB, S5_ROW, S5_ROW), lambda i: (i, 0, 0))
    big_sd = jax.ShapeDtypeStruct((g, S5_ROW, 2 * p), BF16)
    vec_sd = jax.ShapeDtypeStruct((g, 1, 2 * p), F32)
    return pl.pallas_call(
        _s5ops_kernel,
        grid=(g // S5OPS_GB,),
        in_specs=[vec, vec, vec, hp, hp, sq, sq, vec],
        out_specs=[mts, big, big, big, big, big, big, vec, vec],
        out_shape=[jax.ShapeDtypeStruct((g, S5_ROW, S5_ROW), BF16)] + [big_sd] * 6 + [vec_sd, vec_sd],
        scratch_shapes=[pltpu.VMEM((S5OPS_GB, S5_ROW, 2 * p), F32)] * 2,
        compiler_params=_cparams(("arbitrary",)),
        name="s5ops",
    )(a_re, a_im, ldt, bt_re, bt_im, c_re, c_im, d_row)


def _block_transpose(xs):
    n = len(xs)
    lane = lax.broadcasted_iota(jnp.int32, xs[0].shape, 1)
    xs = list(xs)
    d = n // 2
    while d:
        keep = ((lane // S5_CH) & d) == 0
        for i in range(n):
            if i & d:
                continue
            lo, hi = xs[i], xs[i + d]
            xs[i] = jnp.where(keep, lo, pltpu.roll(hi, S5_CH * d, axis=1))
            xs[i + d] = jnp.where(keep, pltpu.roll(lo, LANES - S5_CH * d, axis=1), hi)
        d //= 2
    return xs


def _s5_kernel(bsz, n_chunks, u_ref, mt_ref, erh_ref, erl_ref, eih_ref, eil_ref, gr_ref, gi_ref,
               atr_ref, ati_ref, h0r_ref, h0i_ref, y_ref, fr_ref, fi_ref,
               ua_ref, ub_ref, ya_ref, yb_ref, sr_ref, si_ref, xfr_ref, xfi_ref, xbr_ref, xbi_ref):
    nc = n_chunks
    spv = LANES // S5_CH
    rsub = min(nc, 32)

    def to_chunks(b, carry):
        for half, dst in ((0, ua_ref), (1, ub_ref)):
            for r0 in range(0, nc, rsub):
                xs = [u_ref[b, pl.ds(S5_CHUNK * r0 + half * spv + tt, rsub, stride=S5_CHUNK), :]
                      for tt in range(spv)]
                for k, blk in enumerate(_block_transpose(xs)):
                    dst[k, pl.ds(r0 * bsz + b, rsub, stride=bsz), :] = blk
        return carry

    lax.fori_loop(0, bsz, to_chunks, 0, unroll=2)

    lane = lax.broadcasted_iota(jnp.int32, (bsz, 2 * S5_STATE), 1)
    fwd = lane < S5_STATE
    lane_all = lax.broadcasted_iota(jnp.int32, (bsz * nc, 2 * S5_STATE), 1)
    fwd_all = lane_all < S5_STATE

    def group(k, slot):
        u = jnp.concatenate([ua_ref[k], ub_ref[k]], axis=1)
        uh, ul = _split(u)
        sr_ref[slot] = _dot(uh, erh_ref[k]) + _dot(ul, erh_ref[k]) + _dot(uh, erl_ref[k])
        si_ref[slot] = _dot(uh, eih_ref[k]) + _dot(ul, eih_ref[k]) + _dot(uh, eil_ref[k])
        at_re, at_im = atr_ref[k], ati_ref[k]
        y_intra = _dot(uh, mt_ref[k])

        def step(i, xc):
            x_re, x_im = xc
            rf = pl.ds(pl.multiple_of(i * bsz, bsz), bsz)
            rb = pl.ds(pl.multiple_of((nc - 1 - i) * bsz, bsz), bsz)
            xfr_ref[slot, rf, :] = x_re
            xfi_ref[slot, rf, :] = x_im
            xbr_ref[slot, rb, :] = x_re
            xbi_ref[slot, rb, :] = x_im
            s_re = jnp.where(fwd, sr_ref[slot, rf, :], sr_ref[slot, rb, :])
            s_im = jnp.where(fwd, si_ref[slot, rf, :], si_ref[slot, rb, :])
            return (at_re * x_re - at_im * x_im + s_re, at_re * x_im + at_im * x_re + s_im)

        x_re, x_im = lax.fori_loop(0, nc, step, (h0r_ref[k], h0i_ref[k]), unroll=True)
        fr_ref[k] = x_re
        fi_ref[k] = x_im
        xp_re = jnp.where(fwd_all, xfr_ref[slot], xbr_ref[slot]).astype(BF16)
        xp_im = jnp.where(fwd_all, xfi_ref[slot], xbi_ref[slot]).astype(BF16)
        y = y_intra + _dot_t(xp_re, gr_ref[k]) + _dot_t(xp_im, gi_ref[k])
        ya_ref[k] = y[:, :LANES]
        yb_ref[k] = y[:, LANES:]

    def group_pair(j, carry):
        group(2 * j, 0)
        group(2 * j + 1, 1)
        return carry

    lax.fori_loop(0, S5_GB // 2, group_pair, 0)

    def to_tokens(b, carry):
        for half, src in ((0, ya_ref), (1, yb_ref)):
            for r0 in range(0, nc, rsub):
                ys = [src[k, pl.ds(r0 * bsz + b, rsub, stride=bsz), :] for k in range(S5_GB)]
                for tt, blk in enumerate(_block_transpose(ys)):
                    y_ref[b, pl.ds(S5_CHUNK * r0 + half * spv + tt, rsub, stride=S5_CHUNK), :] = blk
        return carry

    lax.fori_loop(0, bsz, to_tokens, 0, unroll=2)


def _s5(u, ops, h0_re, h0_im):
    bsz, n_tok, _ = u.shape
    g, p = S5_GROUPS, S5_STATE
    nc = n_tok // S5_CHUNK
    rows = nc * bsz
    tok = pl.BlockSpec((bsz, n_tok, LANES), lambda j: (0, 0, j))
    gspec = lambda shape: pl.BlockSpec((S5_GB,) + shape, lambda j: (j, 0, 0))
    op = gspec((S5_ROW, 2 * p))
    return pl.pallas_call(
        functools.partial(_s5_kernel, bsz, nc),
        grid=(g // S5_GB,),
        in_specs=[tok, gspec((S5_ROW, S5_ROW)), op, op, op, op, op, op,
                  gspec((1, 2 * p)), gspec((1, 2 * p)), gspec((bsz, 2 * p)), gspec((bsz, 2 * p))],
        out_specs=[tok, gspec((bsz, 2 * p)), gspec((bsz, 2 * p))],
        out_shape=[jax.ShapeDtypeStruct((bsz, n_tok, D_S5), F32),
                   jax.ShapeDtypeStruct((g, bsz, 2 * p), F32),
                   jax.ShapeDtypeStruct((g, bsz, 2 * p), F32)],
        scratch_shapes=([pltpu.VMEM((S5_GB, rows, LANES), F32)] * 4
                        + [pltpu.VMEM((2, rows, 2 * p), F32)] * 6),
        compiler_params=_cparams(("arbitrary",)),
        name=f"s5_{n_tok}",
    )(u, *ops, h0_re, h0_im)


def _in_kernel(has_pos, *refs):
    if has_pos:
        x_ref, pos_ref, mod_ref, w_ref, hy_ref, s5_ref = refs
        x = x_ref[0] + pos_ref[...]
    else:
        x_ref, mod_ref, w_ref, hy_ref, s5_ref = refs
        x = x_ref[0]
    sh1 = mod_ref[0, :, 0:D_MODEL]
    sc1 = mod_ref[0, :, D_MODEL:2 * D_MODEL]
    h = _norm(x) * (1.0 + sc1) + sh1
    proj = _dot(h.astype(BF16), w_ref[...])
    hy_ref[0] = proj[:, :3 * D_HY]
    s5_ref[0] = proj[:, 3 * D_HY:]


def _in_proj(x3, pos, mod3, w_in_bf, tm):
    nb, lt, _ = x3.shape
    has_pos = pos is not None
    per_batch = mod3.shape[0] > 1
    midx = (lambda b, i: (b, 0, 0)) if per_batch else (lambda b, i: (0, 0, 0))
    in_specs = [pl.BlockSpec((1, tm, D_MODEL), lambda b, i: (b, i, 0))]
    args = [x3]
    if has_pos:
        in_specs.append(pl.BlockSpec((tm, D_MODEL), lambda b, i: (i, 0)))
        args.append(pos)
    in_specs += [pl.BlockSpec((1, 1, 6 * D_MODEL), midx),
                 pl.BlockSpec((D_MODEL, 3 * D_HY + D_S5), lambda b, i: (0, 0))]
    args += [mod3, w_in_bf]
    return pl.pallas_call(
        functools.partial(_in_kernel, has_pos),
        grid=(nb, lt // tm),
        in_specs=in_specs,
        out_specs=[pl.BlockSpec((1, tm, 3 * D_HY), lambda b, i: (b, i, 0)),
                   pl.BlockSpec((1, tm, D_S5), lambda b, i: (b, i, 0))],
        out_shape=[jax.ShapeDtypeStruct((nb, lt, 3 * D_HY), F32),
                   jax.ShapeDtypeStruct((nb, lt, D_S5), F32)],
        compiler_params=_cparams(("arbitrary", "arbitrary")),
        name=f"in_proj{nb}",
    )(*args)


def _route(logits):
    lane = lax.broadcasted_iota(jnp.int32, logits.shape, 1)
    lane_f = lane.astype(F32)
    neg = -jnp.inf
    big = float(LANES)
    m1 = (lane >= N_EXPERTS) & (lane < N_EXPERTS + N_EGROUPS)
    l1 = jnp.where(m1, logits, neg)
    top1 = jnp.max(l1, axis=-1, keepdims=True)
    grp = jnp.min(jnp.where(l1 == top1, lane_f, big), axis=-1, keepdims=True) - float(N_EXPERTS)
    den = jnp.sum(jnp.where(m1, jnp.exp(logits - top1), 0.0), axis=-1, keepdims=True)
    p_grp = 1.0 / den
    lo = grp * float(N_EPG)
    m2 = (lane_f >= lo) & (lane_f < lo + float(N_EPG))
    l2 = jnp.where(m2, logits, neg)
    v1 = jnp.max(l2, axis=-1, keepdims=True)
    i1 = jnp.min(jnp.where(l2 == v1, lane_f, big), axis=-1, keepdims=True)
    l2b = jnp.where(lane_f == i1, neg, l2)
    v2 = jnp.max(l2b, axis=-1, keepdims=True)
    i2 = jnp.min(jnp.where(l2b == v2, lane_f, big), axis=-1, keepdims=True)
    e = jnp.exp(v2 - v1)
    w1 = 1.0 / (1.0 + e)
    w2 = e / (1.0 + e)
    gates = jnp.where(lane_f == i1, w1 * p_grp, 0.0) + jnp.where(lane_f == i2, w2 * p_grp, 0.0)
    return jnp.where(lane_f == grp + float(N_EXPERTS), 1.0, gates)


def _out_kernel(n_ctx_blocks, xc_ref, xl_ref, pos_ref, yhyc_ref, yhyl_ref, ys5c_ref, ys5l_ref, mod_ref,
                wglu_ref, bglu_ref, ong_ref, wout_ref, ln1g_ref, ln1b_ref, wrh_ref, wrl_ref, br_ref,
                x1_ref, h2_ref, gate_ref, cnt_ref):
    is_ctx = pl.program_id(0) < n_ctx_blocks
    x = jnp.where(is_ctx, xc_ref[...], xl_ref[...] + pos_ref[...])
    y = jnp.where(is_ctx, ys5c_ref[...], ys5l_ref[...])
    y_hy = jnp.where(is_ctx, yhyc_ref[...], yhyl_ref[...])
    s5 = jax.nn.gelu(y) * jax.nn.sigmoid(_dot(y.astype(BF16), wglu_ref[...]) + bglu_ref[...])
    m_hy = _rms(y_hy) * ong_ref[:, 0:D_HY]
    m_s5 = _rms(s5) * ong_ref[:, D_HY:]
    o = (_dot(m_hy.astype(BF16), wout_ref[0:D_HY, :]) + _dot(m_s5.astype(BF16), wout_ref[D_HY:, :]))
    g1 = mod_ref[0, :, 2 * D_MODEL:3 * D_MODEL]
    sh2 = mod_ref[0, :, 3 * D_MODEL:4 * D_MODEL]
    sc2 = mod_ref[0, :, 4 * D_MODEL:5 * D_MODEL]
    x1 = _norm(ALPHA * x + g1 * o) * ln1g_ref[...] + ln1b_ref[...]
    x1_ref[...] = x1
    h2 = _norm(x1) * (1.0 + sc2) + sh2
    h2_ref[...] = h2.astype(BF16)
    hh, hl = _split(h2)
    logits = (_dot(hh, wrh_ref[...]) + _dot(hl, wrh_ref[...]) + _dot(hh, wrl_ref[...]) + br_ref[...])
    gates = _route(logits)
    gate_ref[...] = gates
    cnt_ref[0] = jnp.sum(gates, axis=0, keepdims=True)


def _out_proj(xc, xl, pos, yhy_c, yhy_l, ys5_c, ys5_l, mod, wglu_bf, bglu, ong, wout_bf, ln1g, ln1b,
              wr_hi, wr_lo, br, tm):
    n_ctx, n_lat = xc.shape[0], xl.shape[0]
    l_lat = pos.shape[0]
    ncb, nlb, npb = n_ctx // tm, n_lat // tm, l_lat // tm
    ctx = lambda w: pl.BlockSpec((tm, w), lambda i: (jnp.minimum(i, ncb - 1), 0))
    lat = lambda w: pl.BlockSpec((tm, w), lambda i: (jnp.maximum(i - ncb, 0), 0))
    full = lambda shape: pl.BlockSpec(shape, lambda i: (0,) * len(shape))
    out = lambda w: pl.BlockSpec((tm, w), lambda i: (i, 0))
    mod_idx = lambda i: (jnp.where(i < ncb, 0, 1 + jnp.maximum(i - ncb, 0) // npb), 0, 0)
    n_all = n_ctx + n_lat
    return pl.pallas_call(
        functools.partial(_out_kernel, ncb),
        grid=(ncb + nlb,),
        in_specs=[ctx(D_MODEL), lat(D_MODEL),
                  pl.BlockSpec((tm, D_MODEL), lambda i: (jnp.maximum(i - ncb, 0) % npb, 0)),
                  ctx(D_HY), lat(D_HY), ctx(D_S5), lat(D_S5),
                  pl.BlockSpec((1, 1, 6 * D_MODEL), mod_idx),
                  full((D_S5, D_S5)), full((1, D_S5)), full((1, D_MODEL)), full((D_MODEL, D_MODEL)),
                  full((1, D_MODEL)), full((1, D_MODEL)), full((D_MODEL, LANES)), full((D_MODEL, LANES)),
                  full((1, LANES))],
        out_specs=[out(D_MODEL), out(D_MODEL), out(LANES), pl.BlockSpec((1, 1, LANES), lambda i: (i, 0, 0))],
        out_shape=[jax.ShapeDtypeStruct((n_all, D_MODEL), F32),
                   jax.ShapeDtypeStruct((n_all, D_MODEL), BF16),
                   jax.ShapeDtypeStruct((n_all, LANES), F32),
                   jax.ShapeDtypeStruct((n_all // tm, 1, LANES), F32)],
        compiler_params=_cparams(("arbitrary",)),
        name="out_proj",
    )(xc, xl, pos, yhy_c, yhy_l, ys5_c, ys5_l, mod.reshape(mod.shape[0], 1, 6 * D_MODEL),
      wglu_bf, bglu, ong, wout_bf, ln1g, ln1b, wr_hi, wr_lo, br)


def _perm_t(gates, loc_ref, s):
    n = gates.shape[0]
    lane = lax.broadcasted_iota(jnp.int32, gates.shape, 1)
    oh = jnp.where((lane >= N_EXPERTS) & (lane < N_EXPERTS + N_EGROUPS), gates, 0.0)
    r = lax.broadcasted_iota(jnp.int32, (n, n), 0)
    c = lax.broadcasted_iota(jnp.int32, (n, n), 1)
    earlier = jnp.where(c < r, 1.0, 0.0).astype(BF16)
    cum = _dot(earlier, oh.astype(BF16))
    rank = jnp.sum(cum * oh, axis=-1, keepdims=True)
    lane1 = lax.broadcasted_iota(jnp.int32, (1, LANES), 1)
    locv = jnp.zeros((1, LANES), F32)
    for grp in range(N_EGROUPS):
        locv = jnp.where(lane1 == N_EXPERTS + grp, loc_ref[N_EGROUPS * s + grp].astype(F32), locv)
    dest = rank + jnp.sum(oh * locv, axis=-1, keepdims=True)
    slot = lax.broadcasted_iota(jnp.int32, (n, MOE_SLOTS), 1).astype(F32)
    return jnp.where(slot == dest, 1.0, 0.0)


def _segment_copies(s, loc_ref, len_ref, off_ref, make):
    for grp in range(N_EGROUPS):
        loc = loc_ref[N_EGROUPS * s + grp]
        off = off_ref[N_EGROUPS * s + grp]
        n_units = len_ref[N_EGROUPS * s + grp] // MOE_UNIT

        def body(i, carry):
            make(pl.multiple_of(loc + MOE_UNIT * i, MOE_UNIT), pl.multiple_of(off + MOE_UNIT * i, MOE_UNIT))
            return carry

        lax.fori_loop(0, n_units, body, 0)


def _moe_sort_kernel(loc_ref, len_ref, off_ref, h_ref, gate_ref, xs_in, gs_in, xs_hbm, gs_hbm,
                     xs_v, gs_v, sem):
    del xs_in, gs_in
    s = pl.program_id(0)
    slot = s % 2
    gates = gate_ref[...]
    p = _perm_t(gates, loc_ref, s).T.astype(BF16)
    xs_v[slot] = _dot(p, h_ref[...]).astype(BF16)
    g_hi = gates.astype(BF16)
    r1 = gates - g_hi.astype(F32)
    g_mid = r1.astype(BF16)
    g_lo = (r1 - g_mid.astype(F32)).astype(BF16)
    gs_v[slot] = _dot(p, g_hi) + _dot(p, g_mid) + _dot(p, g_lo)

    def copies(buf):
        def x_copy(lr, gr):
            return pltpu.make_async_copy(xs_v.at[buf, pl.ds(lr, MOE_UNIT), :],
                                         xs_hbm.at[pl.ds(gr, MOE_UNIT), :], sem.at[0, buf])

        def g_copy(lr, gr):
            return pltpu.make_async_copy(gs_v.at[buf, pl.ds(lr, MOE_UNIT), :],
                                         gs_hbm.at[pl.ds(gr, MOE_UNIT), :], sem.at[1, buf])

        def start(lr, gr):
            x_copy(lr, gr).start()
            g_copy(lr, gr).start()

        def wait(lr, gr):
            x_copy(lr, gr).wait()
            g_copy(lr, gr).wait()

        return start, wait

    _segment_copies(s, loc_ref, len_ref, off_ref, copies(slot)[0])

    @pl.when(s > 0)
    def _():
        _segment_copies(s - 1, loc_ref, len_ref, off_ref, copies(1 - slot)[1])

    @pl.when(s == pl.num_programs(0) - 1)
    def _():
        _segment_copies(s, loc_ref, len_ref, off_ref, copies(slot)[1])


def _moe_expert_kernel(bg_ref, nb_ref, xs_ref, gs_ref, wg_ref, wu_ref, wd_ref, o_ref):
    i = pl.program_id(0)

    @pl.when(i < nb_ref[0])
    def _():
        grp = bg_ref[i]
        x = xs_ref[...]
        gates = gs_ref[...]
        lane = lax.broadcasted_iota(jnp.int32, gates.shape, 1)
        acc = jnp.zeros(o_ref.shape, F32)
        for e in range(N_EPG):
            a = _dot(x, wg_ref[e])
            u = _dot(x, wu_ref[e])
            ge = jnp.sum(jnp.where(lane == N_EPG * grp + e, gates, 0.0), axis=-1, keepdims=True)
            hid = jax.nn.silu(a) * u * ge
            acc = acc + _dot(hid.astype(BF16), wd_ref[e])
        o_ref[...] = acc.astype(BF16)

    @pl.when(i >= nb_ref[0])
    def _():
        o_ref[...] = jnp.zeros_like(o_ref)


def _moe_combine_kernel(loc_ref, len_ref, off_ref, gate_ref, x1_ref, mod_ref, ln2g_ref, ln2b_ref, o_hbm,
                        ctx_ref, lat_ref, o_v, sem, *, n_ctx_tiles):
    s = pl.program_id(0)
    slot = s % 2

    def copies(buf):
        def o_copy(lr, gr):
            return pltpu.make_async_copy(o_hbm.at[pl.ds(gr, MOE_UNIT), :],
                                         o_v.at[buf, pl.ds(lr, MOE_UNIT), :], sem.at[buf])

        return (lambda lr, gr: o_copy(lr, gr).start()), (lambda lr, gr: o_copy(lr, gr).wait())

    @pl.when(s == 0)
    def _():
        o_v[...] = jnp.zeros_like(o_v)
        _segment_copies(s, loc_ref, len_ref, off_ref, copies(slot)[0])

    @pl.when(s + 1 < pl.num_programs(0))
    def _():
        _segment_copies(s + 1, loc_ref, len_ref, off_ref, copies(1 - slot)[0])

    pt = _perm_t(gate_ref[...], loc_ref, s).astype(BF16)
    _segment_copies(s, loc_ref, len_ref, off_ref, copies(slot)[1])
    f = _dot(pt, o_v[slot])
    g2 = mod_ref[0, :, 5 * D_MODEL:6 * D_MODEL]
    x2 = _norm(ALPHA * x1_ref[...] + g2 * f) * ln2g_ref[...] + ln2b_ref[...]

    @pl.when(s < n_ctx_tiles)
    def _():
        ctx_ref[...] = x2

    @pl.when(s >= n_ctx_tiles)
    def _():
        lat_ref[...] = x2


def _moe_plan(tile_counts, n_blocks):
    cnt = tile_counts[:, 0, N_EXPERTS:N_EXPERTS + N_EGROUPS].astype(jnp.int32)
    len16 = ((cnt + MOE_UNIT - 1) // MOE_UNIT) * MOE_UNIT
    loc = jnp.cumsum(len16, axis=1) - len16
    rows_g = jnp.sum(len16, axis=0)
    reg_g = ((rows_g + MOE_TM - 1) // MOE_TM) * MOE_TM
    reg_start = jnp.cumsum(reg_g) - reg_g
    off = reg_start[None, :] + jnp.cumsum(len16, axis=0) - len16
    blk_end = jnp.cumsum(reg_g // MOE_TM)
    bi = jnp.arange(n_blocks, dtype=jnp.int32)
    blk_group = jnp.minimum(jnp.sum((bi[:, None] >= blk_end[None, :]).astype(jnp.int32), axis=1),
                            N_EGROUPS - 1)
    flat = lambda a: a.reshape(-1).astype(jnp.int32)
    return flat(loc), flat(len16), flat(off), blk_group.astype(jnp.int32), blk_end[-1:].astype(jnp.int32)


def _moe(h2_all, gates_all, tile_counts, x1_all, mod, wg_bf, wu_bf, wd_bf, ln2g, ln2b, n_ctx,
         tokens_per_mod_row):
    n_tok = h2_all.shape[0]
    n_tiles = n_tok // MOE_ST
    n_ctx_tiles = n_ctx // MOE_ST
    max_rows = n_tok + n_tiles * N_EGROUPS * (MOE_UNIT - 1) + N_EGROUPS * (MOE_TM - 1)
    n_blocks = -(-max_rows // MOE_TM)
    n_rows = n_blocks * MOE_TM
    loc, len16, off, blk_group, n_used = _moe_plan(tile_counts, n_blocks)

    tile = lambda w: pl.BlockSpec((MOE_ST, w), lambda s, *_: (s, 0))
    anyspec = pl.BlockSpec(memory_space=pl.ANY)
    xs, gs = pl.pallas_call(
        _moe_sort_kernel,
        grid_spec=pltpu.PrefetchScalarGridSpec(
            num_scalar_prefetch=3, grid=(n_tiles,),
            in_specs=[tile(D_MODEL), tile(LANES), anyspec, anyspec],
            out_specs=[anyspec, anyspec],
            scratch_shapes=[pltpu.VMEM((2, MOE_SLOTS, D_MODEL), BF16), pltpu.VMEM((2, MOE_SLOTS, LANES), F32),
                            pltpu.SemaphoreType.DMA((2, 2))]),
        out_shape=[jax.ShapeDtypeStruct((n_rows, D_MODEL), BF16),
                   jax.ShapeDtypeStruct((n_rows, LANES), F32)],
        input_output_aliases={5: 0, 6: 1},
        compiler_params=_cparams(("arbitrary",)),
        name="moe_sort",
    )(loc, len16, off, h2_all, gates_all, jnp.zeros((n_rows, D_MODEL), BF16), jnp.zeros((n_rows, LANES), F32))

    blk = lambda w: pl.BlockSpec((MOE_TM, w), lambda i, bg, nb: (jnp.minimum(i, nb[0] - 1), 0))
    wspec = lambda a, b: pl.BlockSpec((N_EPG, a, b), lambda i, bg, nb: (bg[i], 0, 0))
    o_sorted = pl.pallas_call(
        _moe_expert_kernel,
        grid_spec=pltpu.PrefetchScalarGridSpec(
            num_scalar_prefetch=2, grid=(n_blocks,),
            in_specs=[blk(D_MODEL), blk(LANES), wspec(D_MODEL, D_EXPERT), wspec(D_MODEL, D_EXPERT),
                      wspec(D_EXPERT, D_MODEL)],
            out_specs=pl.BlockSpec((MOE_TM, D_MODEL), lambda i, bg, nb: (i, 0))),
        out_shape=jax.ShapeDtypeStruct((n_rows, D_MODEL), BF16),
        compiler_params=_cparams(("arbitrary",)),
        name="moe_experts",
    )(blk_group, n_used, xs, gs, wg_bf, wu_bf, wd_bf)

    lat_per_row = tokens_per_mod_row // MOE_ST

    def mod_idx(s, *_):
        return (jnp.where(s < n_ctx_tiles, 0, 1 + (s - n_ctx_tiles) // lat_per_row), 0, 0)

    vec = pl.BlockSpec((1, D_MODEL), lambda s, *_: (0, 0))
    return pl.pallas_call(
        functools.partial(_moe_combine_kernel, n_ctx_tiles=n_ctx_tiles),
        grid_spec=pltpu.PrefetchScalarGridSpec(
            num_scalar_prefetch=3, grid=(n_tiles,),
            in_specs=[tile(LANES), tile(D_MODEL), pl.BlockSpec((1, 1, 6 * D_MODEL), mod_idx), vec, vec,
                      anyspec],
            out_specs=[pl.BlockSpec((MOE_ST, D_MODEL), lambda s, *_: (jnp.minimum(s, n_ctx_tiles - 1), 0)),
                       pl.BlockSpec((MOE_ST, D_MODEL), lambda s, *_: (jnp.maximum(s - n_ctx_tiles, 0), 0))],
            scratch_shapes=[pltpu.VMEM((2, MOE_SLOTS, D_MODEL), BF16), pltpu.SemaphoreType.DMA((2,))]),
        out_shape=[jax.ShapeDtypeStruct((n_ctx, D_MODEL), F32),
                   jax.ShapeDtypeStruct((n_tok - n_ctx, D_MODEL), F32)],
        compiler_params=_cparams(("arbitrary",)),
        name="moe_combine",
    )(loc, len16, off, gates_all, x1_all, mod.reshape(mod.shape[0], 1, 6 * D_MODEL), ln2g, ln2b, o_sorted)


def _grid_pos_embed(n_tokens):
    rows = n_tokens // GRID_W
    row = np.repeat(np.arange(rows, dtype=np.float64), GRID_W)
    col = np.tile(np.arange(GRID_W, dtype=np.float64), rows)
    quarter = D_MODEL // 4
    omega = 1.0 / (POS_BASE ** (np.arange(quarter, dtype=np.float64) / quarter))
    er = row[:, None] * omega
    ec = col[:, None] * omega
    return jnp.asarray(np.concatenate([np.sin(er), np.cos(er), np.sin(ec), np.cos(ec)], axis=-1), F32)


def _mixers(x, pos, mod3, h0_re, h0_im, tabs, filt, s5ops, wts, tm):
    bsz, n_tok, _ = x.shape
    shared = mod3.shape[0] == 1
    x3 = x.reshape(1, bsz * n_tok, D_MODEL) if shared else x
    proj_hy, u_s5 = _in_proj(x3, pos, mod3, wts['w_in'], tm)
    y_hy = _hyena(proj_hy.reshape(bsz, n_tok, 3 * D_HY), tabs, filt,
                  wts['hy_conv_w'], wts['hy_conv_b'], wts['hy_fbias'])
    y_s5, f_re, f_im = _s5(u_s5.reshape(bsz, n_tok, D_S5), s5ops, h0_re, h0_im)
    return y_hy.reshape(bsz * n_tok, D_HY), y_s5.reshape(bsz * n_tok, D_S5), f_re, f_im


def kernel(x_prompt, x_sample, state_s5_re, state_s5_im, c, c_ctx, w_ada, b_ada, w_in, hy_conv_w, hy_conv_b, hy_f_w1, hy_f_b1, hy_f_w2, hy_f_b2, hy_f_w3, hy_freq, hy_fbias, s5_a_re, s5_a_im, s5_log_dt, s5_b_re, s5_b_im, s5_c_re, s5_c_im, s5_d, s5_w_glu, s5_b_glu, out_norm_g, w_out, ln1_g, ln1_b, moe_w_r1, moe_b_r1, moe_w_r2, moe_b_r2, moe_w_gate, moe_w_up, moe_w_down, ln2_g, ln2_b):
    b_ctx, l_ctx, _ = x_prompt.shape
    b_lat, l_lat, _ = x_sample.shape
    g, p = S5_GROUPS, S5_STATE
    assert w_ada.shape[0] == 1, "single-layer trunk"
    l = 0

    nrow = 16
    cond = jnp.concatenate([c_ctx[None, :], c, jnp.zeros((nrow - 1 - b_lat, D_MODEL), F32)], axis=0)
    mod = _ada(cond, w_ada[l], b_ada[l])
    mod_ctx = mod[0:1].reshape(1, 1, 6 * D_MODEL)
    mod_lat = mod[1:1 + b_lat].reshape(b_lat, 1, 6 * D_MODEL)

    wr = jnp.concatenate([moe_w_r2[l].transpose(1, 0, 2).reshape(D_MODEL, N_EXPERTS), moe_w_r1[l]], axis=1)
    wr = jnp.pad(wr, ((0, 0), (0, LANES - wr.shape[1])))
    br = jnp.concatenate([moe_b_r2[l].reshape(-1), moe_b_r1[l]])
    br = jnp.pad(br, (0, LANES - br.shape[0])).reshape(1, LANES)
    wr_hi, wr_lo = _split(wr)

    wts = {
        'w_in': w_in[l].astype(BF16), 'hy_conv_w': hy_conv_w[l], 'hy_conv_b': hy_conv_b[l],
        'hy_fbias': hy_fbias[l], 'w_glu': s5_w_glu[l].astype(BF16), 'b_glu': s5_b_glu[l].reshape(1, -1),
        'out_norm_g': out_norm_g[l].reshape(1, -1), 'w_out': w_out[l].astype(BF16),
        'ln1_g': ln1_g[l].reshape(1, -1), 'ln1_b': ln1_b[l].reshape(1, -1),
        'wr_hi': wr_hi, 'wr_lo': wr_lo, 'br': br,
        'w_gate': moe_w_gate[l].astype(BF16), 'w_up': moe_w_up[l].astype(BF16),
        'w_down': moe_w_down[l].astype(BF16),
        'ln2_g': ln2_g[l].reshape(1, -1), 'ln2_b': ln2_b[l].reshape(1, -1),
    }

    s5ops = _s5_operators(s5_a_re[l], s5_a_im[l], s5_log_dt[l], s5_b_re[l], s5_b_im[l],
                          s5_c_re[l], s5_c_im[l], s5_d[l])
    tabs_ctx = _tables(l_ctx)
    tabs_lat = _tables(l_lat)
    filt_args = (hy_f_w1[l], hy_f_b1[l], hy_f_w2[l], hy_f_b2[l], hy_f_w3[l], hy_freq[l])
    filt_ctx = _hyena_filters(l_ctx, tabs_ctx, *filt_args)
    filt_lat = _hyena_filters(l_lat, tabs_lat, *filt_args)

    zero = jnp.zeros((g, b_ctx, 2 * p), F32)
    yhy_c, ys5_c, f_re, f_im = _mixers(x_prompt, None, mod_ctx, zero, zero, tabs_ctx, filt_ctx, s5ops, wts, 512)
    unpack = lambda f: f.reshape(g, b_ctx, 2, p).transpose(1, 2, 0, 3)[:, None]
    new_re, new_im = unpack(f_re), unpack(f_im)

    pack = lambda s: s[:, l].transpose(2, 0, 1, 3).reshape(g, b_lat, 2 * p)
    pos = _grid_pos_embed(l_lat)
    yhy_l, ys5_l, _, _ = _mixers(x_sample, pos, mod_lat, pack(state_s5_re), pack(state_s5_im),
                                 tabs_lat, filt_lat, s5ops, wts, 512)

    n_ctx = b_ctx * l_ctx
    x1_all, h2_all, gates_all, tile_counts = _out_proj(
        x_prompt.reshape(n_ctx, D_MODEL), x_sample.reshape(b_lat * l_lat, D_MODEL), pos,
        yhy_c, yhy_l, ys5_c, ys5_l, mod, wts['w_glu'], wts['b_glu'], wts['out_norm_g'], wts['w_out'],
        wts['ln1_g'], wts['ln1_b'], wts['wr_hi'], wts['wr_lo'], wts['br'], MOE_ST)
    y_ctx, y_lat = _moe(h2_all, gates_all, tile_counts, x1_all, mod,
                        wts['w_gate'], wts['w_up'], wts['w_down'], wts['ln2_g'], wts['ln2_b'],
                        n_ctx, l_lat)
    return (y_ctx.reshape(x_prompt.shape), y_lat.reshape(x_sample.shape), new_re, new_im)
```

```python
import functools
import math

import numpy as np
import jax
import jax.numpy as jnp
from jax import lax
from jax.experimental import pallas as pl
from jax.experimental.pallas import tpu as pltpu

F32 = jnp.float32
BF16 = jnp.bfloat16

D_MODEL = 1024
DEPTH = 1
GRID_W = 64
POS_BASE = 10000.0
D_HY = 512
D_S5 = 512
S5_CH = 16
S5_GROUPS = 32
S5_STATE = 64
S5_CHUNK = 16
S5_ROW = S5_CHUNK * S5_CH
HY_BANDS = 16
HY_EMB = 1 + 2 * HY_BANDS
HY_HID = 64
HY_MIN_DECAY = math.log(1e-2) / 1.5
HY_MAX_DECAY = math.log(1e-2) / 0.3
N_EGROUPS = 4
N_EPG = 4
N_EXPERTS = 16
D_EXPERT = 512
LN_EPS = 1e-5
ALPHA = (2.0 * DEPTH) ** 0.25
LANES = 128
S5_GB = LANES // S5_CH
S5OPS_GB = 4
HY_CW = 512
MOE_ST = 512
MOE_SLOTS = 640
MOE_UNIT = 16
MOE_TM = 512
VMEM_LIMIT = 60000 * 1024


def _cparams(sem):
    return pltpu.CompilerParams(dimension_semantics=sem, vmem_limit_bytes=VMEM_LIMIT)


def _split(x):
    hi = x.astype(BF16)
    lo = (x - hi.astype(F32)).astype(BF16)
    return hi, lo


def _dot(a, b):
    return jnp.dot(a, b, preferred_element_type=F32)


def _dot_t(a, b):
    return lax.dot_general(a, b, (((1,), (1,)), ((), ())), preferred_element_type=F32)


def _mm3(a, b):
    ah, al = _split(a)
    bh, bl = _split(b)
    return _dot(ah, bh) + _dot(al, bh) + _dot(ah, bl)


def _mm3_t(a, b):
    ah, al = _split(a)
    bh, bl = _split(b)
    return _dot_t(ah, bh) + _dot_t(al, bh) + _dot_t(ah, bl)


def _norm(x):
    xc = x - jnp.mean(x, axis=-1, keepdims=True)
    return xc * lax.rsqrt(jnp.mean(xc * xc, axis=-1, keepdims=True) + LN_EPS)


def _rms(y):
    return y * lax.rsqrt(jnp.mean(y * y, axis=-1, keepdims=True) + LN_EPS)


def _ada_kernel(cond_ref, w_ref, b_ref, o_ref):
    c = jax.nn.silu(cond_ref[...])
    o_ref[...] = _mm3(c, w_ref[...]) + b_ref[...]


def _ada(cond, w_ada, b_ada):
    nb = cond.shape[0]
    n = w_ada.shape[1]
    tn = 1024
    return pl.pallas_call(
        _ada_kernel,
        grid=(n // tn,),
        in_specs=[pl.BlockSpec((nb, D_MODEL), lambda j: (0, 0)),
                  pl.BlockSpec((D_MODEL, tn), lambda j: (0, j)),
                  pl.BlockSpec((1, tn), lambda j: (0, j))],
        out_specs=pl.BlockSpec((nb, tn), lambda j: (0, j)),
        out_shape=jax.ShapeDtypeStruct((nb, n), F32),
        compiler_params=_cparams(("arbitrary",)),
        name="ada",
    )(cond, w_ada, b_ada.reshape(1, n))


def _dft_tables(n_half):
    n = 2 * n_half
    idx = np.arange(n_half, dtype=np.int64)
    m = (idx[:, None] * idx[None, :]) % n
    ang = 2.0 * np.pi * m.astype(np.float64) / n
    cm = np.cos(ang)
    sm = -np.sin(ang)
    sm[0, :] = 1.0 - 2.0 * (idx % 2)
    return cm.astype(np.float32), sm.astype(np.float32)


def _tables(n_tok):
    n_half = n_tok // 2
    cm, sm = _dft_tables(n_half)
    mats = tuple(jnp.asarray(t).astype(BF16) for t in (cm, sm, np.ascontiguousarray(sm.T)))
    ang = np.pi * np.arange(n_half, dtype=np.float64) / n_tok
    tw = [np.broadcast_to(v[:, None], (n_half, HY_CW)).astype(np.float32) for v in (np.cos(ang), -np.sin(ang))]
    return mats + (jnp.asarray(tw[0]), jnp.asarray(tw[1]))


def _put_cols(ref, x):
    for j in range(ref.shape[0]):
        ref[j] = x[:, LANES * j:LANES * (j + 1)]


def _get_cols(ref):
    return jnp.concatenate([ref[j] for j in range(ref.shape[0])], axis=1)


def _get_parity(ref, parity):
    n_half = ref.shape[1] // 2
    return jnp.concatenate([ref[j, pl.ds(parity, n_half, stride=2), :] for j in range(ref.shape[0])], axis=1)


def _put_parity(ref, parity, x):
    n_half = ref.shape[1] // 2
    for j in range(ref.shape[0]):
        ref[j, pl.ds(parity, n_half, stride=2), :] = x[:, LANES * j:LANES * (j + 1)]


def _set_row0(x, v):
    first = lax.broadcasted_iota(jnp.int32, (8, x.shape[1]), 0) == 0
    return jnp.concatenate([jnp.where(first, v, x[:8]), x[8:]], axis=0)


def _rfft_packed(x_ref, cm, sm, tw_re, tw_im):
    xe = _get_parity(x_ref, 0).astype(BF16)
    xo = _get_parity(x_ref, 1).astype(BF16)
    e_re, e_im = _dot(cm, xe), _dot(sm, xe)
    o_re, o_im = _dot(cm, xo), _dot(sm, xo)
    t_re = tw_re * o_re - tw_im * o_im
    t_im = tw_re * o_im + tw_im * o_re
    a_im = _set_row0(e_im + t_im, e_im[0:1])
    b_im = _set_row0(t_im - e_im, -o_im[0:1])
    return e_re + t_re, a_im, e_re - t_re, b_im


def _irfft_packed(y_ref, ya_re, ya_im, yb_re, yb_im, cm, st, tw_re, tw_im):
    p_e = ya_re + yb_re
    q_e = _set_row0(ya_im - yb_im, ya_im[0:1])
    _put_parity(y_ref, 0, _dot(cm, p_e.astype(BF16)) + _dot(st, q_e.astype(BF16)))
    ra_re = ya_re * tw_re + ya_im * tw_im
    ra_im = ya_im * tw_re - ya_re * tw_im
    rb_re = yb_im * tw_im - yb_re * tw_re
    rb_im = -(yb_re * tw_im + yb_im * tw_re)
    p_o = ra_re + rb_re
    q_o = _set_row0(ra_im - rb_im, -yb_im[0:1])
    _put_parity(y_ref, 1, _dot(cm, p_o.astype(BF16)) + _dot(st, q_o.astype(BF16)))


def _filt_kernel(n_tok, z_ref, t_ref, w1_ref, b1_ref, w2_ref, b2_ref, fr_ref, w3f_ref, w3b_ref,
                 dl_ref, cm_ref, sm_ref, twr_ref, twi_ref, kar_ref, kai_ref, kbr_ref, kbi_ref, p_ref, q_ref):
    fr = fr_ref[...]
    h = jnp.sin(fr * (_mm3(z_ref[...], w1_ref[...]) + b1_ref[...]))
    h = jnp.sin(fr * (_mm3(h, w2_ref[...]) + b2_ref[...]))
    decay = jnp.exp(-t_ref[...] * dl_ref[...])
    row = lax.broadcasted_iota(jnp.int32, decay.shape, 0)
    hf = _mm3(h, w3f_ref[...]) * decay
    hb = jnp.where(row == 0, 0.0, _mm3(h, w3b_ref[...]) * decay)
    _put_cols(p_ref, hf + hb)
    _put_cols(q_ref, hf - hb)
    cm, sm, tw_re, tw_im = cm_ref[...], sm_ref[...], twr_ref[...], twi_ref[...]
    row0 = lax.broadcasted_iota(jnp.int32, tw_re.shape, 0) == 0
    pa_re, pa_im, pb_re, _ = _rfft_packed(p_ref, cm, sm, tw_re, tw_im)
    _, qa_im, _, qb_im = _rfft_packed(q_ref, cm, sm, tw_re, tw_im)
    inv_n = 1.0 / (2 * n_tok)
    w_re = jnp.where(row0, inv_n, 2.0 * inv_n)
    kar_ref[...] = w_re * pa_re
    kbr_ref[...] = w_re * pb_re
    kai_ref[...] = (2.0 * inv_n) * _set_row0(qa_im, pa_im[0:1])
    kbi_ref[...] = (2.0 * inv_n) * qb_im


def _hyena_filters(n_tok, tabs, hy_f_w1, hy_f_b1, hy_f_w2, hy_f_b2, hy_f_w3, hy_freq):
    cm, sm, _, tw_re, tw_im = tabs
    n_half = n_tok // 2
    t = jnp.linspace(0.0, 1.0, n_tok, dtype=F32)[:, None]
    wv = 2.0 * math.pi * jnp.arange(n_tok, dtype=F32) / n_tok
    fb = jnp.linspace(1e-4, HY_BANDS - 1, HY_BANDS, dtype=F32)
    ang = wv[:, None] * fb[None, :]
    z = jnp.concatenate([t, jnp.cos(ang), -jnp.sin(ang)], axis=-1)
    z = jnp.pad(z, ((0, 0), (0, LANES - HY_EMB)))
    w1 = jnp.pad(hy_f_w1, ((0, LANES - HY_EMB), (0, 0)))
    deltas = jnp.abs(jnp.linspace(HY_MIN_DECAY, HY_MAX_DECAY, D_HY, dtype=F32))[None, :]
    ncb = D_HY // HY_CW
    full = lambda j: (0, 0)
    out_sd = jax.ShapeDtypeStruct((n_half, 2 * D_HY), F32)
    mat = pl.BlockSpec((n_half, n_half), full, pipeline_mode=pl.Buffered(1))
    twb = pl.BlockSpec((n_half, HY_CW), full, pipeline_mode=pl.Buffered(1))
    return pl.pallas_call(
        functools.partial(_filt_kernel, n_tok),
        grid=(2 * ncb,),
        in_specs=[pl.BlockSpec((n_tok, LANES), full),
                  pl.BlockSpec((n_tok, 1), full),
                  pl.BlockSpec((LANES, HY_HID), full),
                  pl.BlockSpec((1, HY_HID), full),
                  pl.BlockSpec((HY_HID, HY_HID), full),
                  pl.BlockSpec((1, HY_HID), full),
                  pl.BlockSpec((1, HY_HID), full),
                  pl.BlockSpec((HY_HID, HY_CW), lambda j: (0, 2 * ncb * (j // ncb) + j % ncb)),
                  pl.BlockSpec((HY_HID, HY_CW), lambda j: (0, 2 * ncb * (j // ncb) + ncb + j % ncb)),
                  pl.BlockSpec((1, HY_CW), lambda j: (0, j % ncb)),
                  mat, mat, twb, twb],
        out_specs=[pl.BlockSpec((n_half, HY_CW), lambda j: (0, j))] * 4,
        out_shape=[out_sd] * 4,
        scratch_shapes=[pltpu.VMEM((HY_CW // LANES, n_tok, LANES), F32)] * 2,
        compiler_params=_cparams(("arbitrary",)),
        name=f"filt{n_tok}",
    )(z, t, w1, hy_f_b1.reshape(1, -1), hy_f_w2, hy_f_b2.reshape(1, -1), hy_freq.reshape(1, -1),
      hy_f_w3, hy_f_w3, deltas, cm, sm, tw_re, tw_im)


def _hyena_kernel(pv_ref, p1_ref, p2_ref, cwv_ref, cw1_ref, cw2_ref, cbv_ref, cb1_ref, cb2_ref,
                  fbias_ref, cm_ref, sm_ref, st_ref, twr_ref, twi_ref,
                  kar0_ref, kai0_ref, kbr0_ref, kbi0_ref, kar1_ref, kai1_ref, kbr1_ref, kbi1_ref,
                  o_ref, u_ref, y_ref):
    n_tok = pv_ref.shape[1]
    row = lax.broadcasted_iota(jnp.int32, (n_tok, pv_ref.shape[2]), 0)

    def short_conv(p_ref, cw_ref, cb_ref):
        p = p_ref[0]
        prev = jnp.where(row == 0, 0.0, pltpu.roll(p, 1, axis=0))
        nxt = jnp.where(row == n_tok - 1, 0.0, pltpu.roll(p, n_tok - 1, axis=0))
        return cb_ref[...] + prev * cw_ref[0:1, :] + p * cw_ref[1:2, :] + nxt * cw_ref[2:3, :]

    cm, sm, st, tw_re, tw_im = cm_ref[...], sm_ref[...], st_ref[...], twr_ref[...], twi_ref[...]

    def fftconv(u, kar_ref, kai_ref, kbr_ref, kbi_ref, skip):
        _put_cols(u_ref, u)
        ua_re, ua_im, ub_re, ub_im = _rfft_packed(u_ref, cm, sm, tw_re, tw_im)
        ka_re, ka_im, kb_re, kb_im = kar_ref[...], kai_ref[...], kbr_ref[...], kbi_ref[...]
        zero_row = jnp.zeros_like(ka_im[0:1])
        kaz = _set_row0(ka_im, zero_row)
        kbz = _set_row0(kb_im, zero_row)
        ya_re = ua_re * ka_re - ua_im * kaz
        yb_re = ub_re * kb_re - ub_im * kbz
        h_re = ua_im[0:1] * ka_im[0:1] - ub_im[0:1] * kb_im[0:1]
        h_im = ua_im[0:1] * kb_im[0:1] + ub_im[0:1] * ka_im[0:1]
        ya_im = _set_row0(ua_re * ka_im + ua_im * ka_re, h_re)
        yb_im = _set_row0(ub_re * kb_im + ub_im * kb_re, h_im)
        _irfft_packed(y_ref, ya_re, ya_im, yb_re, yb_im, cm, st, tw_re, tw_im)
        return _get_cols(y_ref) + u * skip

    v = short_conv(pv_ref, cwv_ref, cbv_ref)
    x1 = short_conv(p1_ref, cw1_ref, cb1_ref)
    z = x1 * fftconv(v, kar0_ref, kai0_ref, kbr0_ref, kbi0_ref, fbias_ref[0:1, :])
    x2 = short_conv(p2_ref, cw2_ref, cb2_ref)
    o_ref[0] = x2 * fftconv(z, kar1_ref, kai1_ref, kbr1_ref, kbi1_ref, fbias_ref[1:2, :])


def _hyena(proj_hy, tabs, filt, hy_conv_w, hy_conv_b, hy_fbias):
    bsz, n_tok, _ = proj_hy.shape
    n_half = n_tok // 2
    ncb = D_HY // HY_CW
    cm, sm, st, tw_re, tw_im = tabs
    cb = hy_conv_b.reshape(1, -1)
    const = lambda shape: pl.BlockSpec(shape, lambda b, c: (0, 0), pipeline_mode=pl.Buffered(1))
    mat = const((n_half, n_half))
    twb = const((n_half, HY_CW))

    def pspec(k):
        return pl.BlockSpec((1, n_tok, HY_CW), lambda b, c: (b, 0, k * ncb + c))

    def cwspec(k):
        return pl.BlockSpec((3, HY_CW), lambda b, c: (0, k * ncb + c))

    def cbspec(k):
        return pl.BlockSpec((1, HY_CW), lambda b, c: (0, k * ncb + c))

    def fspec(o):
        mode = pl.Buffered(1) if ncb == 1 else None
        return pl.BlockSpec((n_half, HY_CW), lambda b, c: (0, o * ncb + c), pipeline_mode=mode)

    return pl.pallas_call(
        _hyena_kernel,
        grid=(bsz, ncb),
        in_specs=[pspec(0), pspec(1), pspec(2), cwspec(0), cwspec(1), cwspec(2),
                  cbspec(0), cbspec(1), cbspec(2),
                  pl.BlockSpec((2, HY_CW), lambda b, c: (0, c)),
                  mat, mat, mat, twb, twb] + [fspec(0)] * 4 + [fspec(1)] * 4,
        out_specs=pl.BlockSpec((1, n_tok, HY_CW), lambda b, c: (b, 0, c)),
        out_shape=jax.ShapeDtypeStruct((bsz, n_tok, D_HY), F32),
        scratch_shapes=[pltpu.VMEM((HY_CW // LANES, n_tok, LANES), F32)] * 2,
        compiler_params=_cparams(("arbitrary", "arbitrary")),
        name=f"hyena{n_tok}",
    )(proj_hy, proj_hy, proj_hy, hy_conv_w, hy_conv_w, hy_conv_w, cb, cb, cb, hy_fbias,
      cm, sm, st, tw_re, tw_im, *filt, *filt)


def _s5ops_kernel(*refs):
    for g in range(S5OPS_GB):
        _s5ops_group(g, *refs)


def _s5ops_group(g, are_ref, aim_ref, ldt_ref, btr_ref, bti_ref, cre_ref, cim_ref, d_ref,
                 mt_ref, erh_ref, erl_ref, eih_ref, eil_ref, gr_ref, gi_ref, atr_ref, ati_ref,
                 er_ref, ei_ref):
    a_re, a_im = are_ref[g], aim_ref[g]
    dt = jnp.exp(ldt_ref[g])
    mag = jnp.exp(a_re * dt)
    ab_re = mag * jnp.cos(a_im * dt)
    ab_im = mag * jnp.sin(a_im * dt)
    n_re, n_im = ab_re - 1.0, ab_im
    den = a_re * a_re + a_im * a_im
    q_re = (n_re * a_re + n_im * a_im) / den
    q_im = (n_im * a_re - n_re * a_im) / den
    bt_re, bt_im = btr_ref[g], bti_ref[g]
    bb_re = q_re * bt_re - q_im * bt_im
    bb_im = q_re * bt_im + q_im * bt_re
    c_re, c_im = cre_ref[g, 0:S5_CH, :], cim_ref[g, 0:S5_CH, :]
    pw = [(jnp.ones_like(ab_re), jnp.zeros_like(ab_re))]
    for _ in range(S5_CHUNK):
        pr, pi = pw[-1]
        pw.append((pr * ab_re - pi * ab_im, pr * ab_im + pi * ab_re))
    lane = lax.broadcasted_iota(jnp.int32, ab_re.shape, 1)
    fwd = lane < S5_STATE
    for s in range(S5_CHUNK):
        e_re = jnp.where(fwd, pw[S5_CHUNK - 1 - s][0], pw[s][0])
        e_im = jnp.where(fwd, pw[S5_CHUNK - 1 - s][1], pw[s][1])
        er_ref[g, pl.ds(S5_CH * s, S5_CH), :] = e_re * bb_re - e_im * bb_im
        ei_ref[g, pl.ds(S5_CH * s, S5_CH), :] = e_re * bb_im + e_im * bb_re
        g_re = jnp.where(fwd, pw[s + 1][0], pw[S5_CHUNK - s][0])
        g_im = jnp.where(fwd, pw[s + 1][1], pw[S5_CHUNK - s][1])
        gr_ref[g, pl.ds(S5_CH * s, S5_CH), :] = (c_re * g_re - c_im * g_im).astype(BF16)
        gi_ref[g, pl.ds(S5_CH * s, S5_CH), :] = (-(c_re * g_im + c_im * g_re)).astype(BF16)
    atr_ref[g] = pw[S5_CHUNK][0]
    ati_ref[g] = pw[S5_CHUNK][1]
    er, ei = er_ref[g], ei_ref[g]
    erh_ref[g], erl_ref[g] = _split(er)
    eih_ref[g], eil_ref[g] = _split(ei)
    lane2 = lax.broadcasted_iota(jnp.int32, er.shape, 1)
    row2 = lax.broadcasted_iota(jnp.int32, er.shape, 0)
    f2 = lane2 < S5_STATE
    zero = jnp.zeros_like(er)

    cp_re, cp_im = cre_ref[g], cim_ref[g]
    kf = _mm3_t(jnp.where(f2, er, zero), cp_re) - _mm3_t(jnp.where(f2, ei, zero), cp_im)
    kb = _mm3_t(jnp.where(f2, zero, er), cp_re) - _mm3_t(jnp.where(f2, zero, ei), cp_im)
    d_row = d_ref[g]
    steps_per_vreg = LANES // S5_CH
    for half in range(S5_CHUNK // steps_per_vreg):
        acc = zero
        for tt in range(steps_per_vreg):
            t = half * steps_per_vreg + tt
            nf = S5_CH * (S5_CHUNK - 1 - t)
            nb = S5_CH * t
            col_f = jnp.concatenate([kf[nf:], zero[:nf]], axis=0) if nf else kf
            col_b = jnp.concatenate([zero[:nb], kb[:S5_ROW - nb]], axis=0) if nb else kb
            diag = jnp.where((row2 // S5_CH == t) & (row2 % S5_CH == lane2), d_row, 0.0)
            col = col_f + col_b + diag
            r = pltpu.roll(col, S5_CH * tt, axis=1) if tt else col
            acc = jnp.where((lane2 >= S5_CH * tt) & (lane2 < S5_CH * (tt + 1)), r, acc)
        mt_ref[g, :, LANES * half:LANES * (half + 1)] = acc.astype(BF16)


def _s5_operators(s5_a_re, s5_a_im, s5_log_dt, s5_b_re, s5_b_im, s5_c_re, s5_c_im, s5_d):
    g, p, h = S5_GROUPS, S5_STATE, S5_CH
    cat = lambda x: jnp.concatenate([x[0], x[1]], axis=-1)
    a_re = cat(s5_a_re).reshape(g, 1, 2 * p)
    a_im = cat(s5_a_im).reshape(g, 1, 2 * p)
    ldt = cat(jnp.broadcast_to(s5_log_dt[:, :, None], (2, g, p))).reshape(g, 1, 2 * p)
    bt_re = cat(jnp.swapaxes(s5_b_re, -1, -2))
    bt_im = cat(jnp.swapaxes(s5_b_im, -1, -2))
    cpad = lambda c: jnp.pad(jnp.concatenate([c, c], axis=-1), ((0, 0), (0, LANES - h), (0, 0)))
    c_re, c_im = cpad(s5_c_re), cpad(s5_c_im)
    d_row = jnp.pad(s5_d.reshape(g, 1, h), ((0, 0), (0, 0), (0, LANES - h)))
    vec = pl.BlockSpec((S5OPS_GB, 1, 2 * p), lambda i: (i, 0, 0))
    hp = pl.BlockSpec((S5OPS_GB, h, 2 * p), lambda i: (i, 0, 0))
    sq = pl.BlockSpec((S5OPS_GB, LANES, 2 * p), lambda i: (i, 0, 0))
    big = pl.BlockSpec((S5OPS_GB, S5_ROW, 2 * p), lambda i: (i, 0, 0))
    mts = pl.BlockSpec((S5OPS_GB, S5_ROW, S5_ROW), lambda i: (i, 0, 0))
    big_sd = jax.ShapeDtypeStruct((g, S5_ROW, 2 * p), BF16)
    vec_sd = jax.ShapeDtypeStruct((g, 1, 2 * p), F32)
    return pl.pallas_call(
        _s5ops_kernel,
        grid=(g // S5OPS_GB,),
        in_specs=[vec, vec, vec, hp, hp, sq, sq, vec],
        out_specs=[mts, big, big, big, big, big, big, vec, vec],
        out_shape=[jax.ShapeDtypeStruct((g, S5_ROW, S5_ROW), BF16)] + [big_sd] * 6 + [vec_sd, vec_sd],
        scratch_shapes=[pltpu.VMEM((S5OPS_GB, S5_ROW, 2 * p), F32)] * 2,
        compiler_params=_cparams(("arbitrary",)),
        name="s5ops",
    )(a_re, a_im, ldt, bt_re, bt_im, c_re, c_im, d_row)


def _block_transpose(xs):
    n = len(xs)
    lane = lax.broadcasted_iota(jnp.int32, xs[0].shape, 1)
    xs = list(xs)
    d = n // 2
    while d:
        keep = ((lane // S5_CH) & d) == 0
        for i in range(n):
            if i & d:
                continue
            lo, hi = xs[i], xs[i + d]
            xs[i] = jnp.where(keep, lo, pltpu.roll(hi, S5_CH * d, axis=1))
            xs[i + d] = jnp.where(keep, pltpu.roll(lo, LANES - S5_CH * d, axis=1), hi)
        d //= 2
    return xs


def _s5_kernel(bsz, n_chunks, u_ref, mt_ref, erh_ref, erl_ref, eih_ref, eil_ref, gr_ref, gi_ref,
               atr_ref, ati_ref, h0r_ref, h0i_ref, y_ref, fr_ref, fi_ref,
               ua_ref, ub_ref, ya_ref, yb_ref, sr_ref, si_ref, xfr_ref, xfi_ref, xbr_ref, xbi_ref):
    nc = n_chunks
    spv = LANES // S5_CH
    rsub = min(nc, 32)

    def to_chunks(b, carry):
        for half, dst in ((0, ua_ref), (1, ub_ref)):
            for r0 in range(0, nc, rsub):
                xs = [u_ref[b, pl.ds(S5_CHUNK * r0 + half * spv + tt, rsub, stride=S5_CHUNK), :]
                      for tt in range(spv)]
                for k, blk in enumerate(_block_transpose(xs)):
                    dst[k, pl.ds(r0 * bsz + b, rsub, stride=bsz), :] = blk
        return carry

    lax.fori_loop(0, bsz, to_chunks, 0, unroll=2)

    lane = lax.broadcasted_iota(jnp.int32, (bsz, 2 * S5_STATE), 1)
    fwd = lane < S5_STATE
    lane_all = lax.broadcasted_iota(jnp.int32, (bsz * nc, 2 * S5_STATE), 1)
    fwd_all = lane_all < S5_STATE

    def group(k, slot):
        u = jnp.concatenate([ua_ref[k], ub_ref[k]], axis=1)
        uh, ul = _split(u)
        sr_ref[slot] = _dot(uh, erh_ref[k]) + _dot(ul, erh_ref[k]) + _dot(uh, erl_ref[k])
        si_ref[slot] = _dot(uh, eih_ref[k]) + _dot(ul, eih_ref[k]) + _dot(uh, eil_ref[k])
        at_re, at_im = atr_ref[k], ati_ref[k]
        y_intra = _dot(uh, mt_ref[k])

        def step(i, xc):
            x_re, x_im = xc
            rf = pl.ds(pl.multiple_of(i * bsz, bsz), bsz)
            rb = pl.ds(pl.multiple_of((nc - 1 - i) * bsz, bsz), bsz)
            xfr_ref[slot, rf, :] = x_re
            xfi_ref[slot, rf, :] = x_im
            xbr_ref[slot, rb, :] = x_re
            xbi_ref[slot, rb, :] = x_im
            s_re = jnp.where(fwd, sr_ref[slot, rf, :], sr_ref[slot, rb, :])
            s_im = jnp.where(fwd, si_ref[slot, rf, :], si_ref[slot, rb, :])
            return (at_re * x_re - at_im * x_im + s_re, at_re * x_im + at_im * x_re + s_im)

        x_re, x_im = lax.fori_loop(0, nc, step, (h0r_ref[k], h0i_ref[k]), unroll=True)
        fr_ref[k] = x_re
        fi_ref[k] = x_im
        xp_re = jnp.where(fwd_all, xfr_ref[slot], xbr_ref[slot]).astype(BF16)
        xp_im = jnp.where(fwd_all, xfi_ref[slot], xbi_ref[slot]).astype(BF16)
        y = y_intra + _dot_t(xp_re, gr_ref[k]) + _dot_t(xp_im, gi_ref[k])
        ya_ref[k] = y[:, :LANES]
        yb_ref[k] = y[:, LANES:]

    def group_pair(j, carry):
        group(2 * j, 0)
        group(2 * j + 1, 1)
        return carry

    lax.fori_loop(0, S5_GB // 2, group_pair, 0)

    def to_tokens(b, carry):
        for half, src in ((0, ya_ref), (1, yb_ref)):
            for r0 in range(0, nc, rsub):
                ys = [src[k, pl.ds(r0 * bsz + b, rsub, stride=bsz), :] for k in range(S5_GB)]
                for tt, blk in enumerate(_block_transpose(ys)):
                    y_ref[b, pl.ds(S5_CHUNK * r0 + half * spv + tt, rsub, stride=S5_CHUNK), :] = blk
        return carry

    lax.fori_loop(0, bsz, to_tokens, 0, unroll=2)


def _s5(u, ops, h0_re, h0_im):
    bsz, n_tok, _ = u.shape
    g, p = S5_GROUPS, S5_STATE
    nc = n_tok // S5_CHUNK
    rows = nc * bsz
    tok = pl.BlockSpec((bsz, n_tok, LANES), lambda j: (0, 0, j))
    gspec = lambda shape: pl.BlockSpec((S5_GB,) + shape, lambda j: (j, 0, 0))
    op = gspec((S5_ROW, 2 * p))
    return pl.pallas_call(
        functools.partial(_s5_kernel, bsz, nc),
        grid=(g // S5_GB,),
        in_specs=[tok, gspec((S5_ROW, S5_ROW)), op, op, op, op, op, op,
                  gspec((1, 2 * p)), gspec((1, 2 * p)), gspec((bsz, 2 * p)), gspec((bsz, 2 * p))],
        out_specs=[tok, gspec((bsz, 2 * p)), gspec((bsz, 2 * p))],
        out_shape=[jax.ShapeDtypeStruct((bsz, n_tok, D_S5), F32),
                   jax.ShapeDtypeStruct((g, bsz, 2 * p), F32),
                   jax.ShapeDtypeStruct((g, bsz, 2 * p), F32)],
        scratch_shapes=([pltpu.VMEM((S5_GB, rows, LANES), F32)] * 4
                        + [pltpu.VMEM((2, rows, 2 * p), F32)] * 6),
        compiler_params=_cparams(("arbitrary",)),
        name=f"s5_{n_tok}",
    )(u, *ops, h0_re, h0_im)


def _in_kernel(has_pos, *refs):
    if has_pos:
        x_ref, pos_ref, mod_ref, w_ref, hy_ref, s5_ref = refs
        x = x_ref[0] + pos_ref[...]
    else:
        x_ref, mod_ref, w_ref, hy_ref, s5_ref = refs
        x = x_ref[0]
    sh1 = mod_ref[0, :, 0:D_MODEL]
    sc1 = mod_ref[0, :, D_MODEL:2 * D_MODEL]
    h = _norm(x) * (1.0 + sc1) + sh1
    proj = _dot(h.astype(BF16), w_ref[...])
    hy_ref[0] = proj[:, :3 * D_HY]
    s5_ref[0] = proj[:, 3 * D_HY:]


def _in_proj(x3, pos, mod3, w_in_bf, tm):
    nb, lt, _ = x3.shape
    has_pos = pos is not None
    per_batch = mod3.shape[0] > 1
    midx = (lambda b, i: (b, 0, 0)) if per_batch else (lambda b, i: (0, 0, 0))
    in_specs = [pl.BlockSpec((1, tm, D_MODEL), lambda b, i: (b, i, 0))]
    args = [x3]
    if has_pos:
        in_specs.append(pl.BlockSpec((tm, D_MODEL), lambda b, i: (i, 0)))
        args.append(pos)
    in_specs += [pl.BlockSpec((1, 1, 6 * D_MODEL), midx),
                 pl.BlockSpec((D_MODEL, 3 * D_HY + D_S5), lambda b, i: (0, 0))]
    args += [mod3, w_in_bf]
    return pl.pallas_call(
        functools.partial(_in_kernel, has_pos),
        grid=(nb, lt // tm),
        in_specs=in_specs,
        out_specs=[pl.BlockSpec((1, tm, 3 * D_HY), lambda b, i: (b, i, 0)),
                   pl.BlockSpec((1, tm, D_S5), lambda b, i: (b, i, 0))],
        out_shape=[jax.ShapeDtypeStruct((nb, lt, 3 * D_HY), F32),
                   jax.ShapeDtypeStruct((nb, lt, D_S5), F32)],
        compiler_params=_cparams(("arbitrary", "arbitrary")),
        name=f"in_proj{nb}",
    )(*args)


def _route(logits):
    lane = lax.broadcasted_iota(jnp.int32, logits.shape, 1)
    lane_f = lane.astype(F32)
    neg = -jnp.inf
    big = float(LANES)
    m1 = (lane >= N_EXPERTS) & (lane < N_EXPERTS + N_EGROUPS)
    l1 = jnp.where(m1, logits, neg)
    top1 = jnp.max(l1, axis=-1, keepdims=True)
    grp = jnp.min(jnp.where(l1 == top1, lane_f, big), axis=-1, keepdims=True) - float(N_EXPERTS)
    den = jnp.sum(jnp.where(m1, jnp.exp(logits - top1), 0.0), axis=-1, keepdims=True)
    p_grp = 1.0 / den
    lo = grp * float(N_EPG)
    m2 = (lane_f >= lo) & (lane_f < lo + float(N_EPG))
    l2 = jnp.where(m2, logits, neg)
    v1 = jnp.max(l2, axis=-1, keepdims=True)
    i1 = jnp.min(jnp.where(l2 == v1, lane_f, big), axis=-1, keepdims=True)
    l2b = jnp.where(lane_f == i1, neg, l2)
    v2 = jnp.max(l2b, axis=-1, keepdims=True)
    i2 = jnp.min(jnp.where(l2b == v2, lane_f, big), axis=-1, keepdims=True)
    e = jnp.exp(v2 - v1)
    w1 = 1.0 / (1.0 + e)
    w2 = e / (1.0 + e)
    gates = jnp.where(lane_f == i1, w1 * p_grp, 0.0) + jnp.where(lane_f == i2, w2 * p_grp, 0.0)
    return jnp.where(lane_f == grp + float(N_EXPERTS), 1.0, gates)


def _out_kernel(n_ctx_blocks, xc_ref, xl_ref, pos_ref, yhyc_ref, yhyl_ref, ys5c_ref, ys5l_ref, mod_ref,
                wglu_ref, bglu_ref, ong_ref, wout_ref, ln1g_ref, ln1b_ref, wrh_ref, wrl_ref, br_ref,
                x1_ref, h2_ref, gate_ref, cnt_ref):
    is_ctx = pl.program_id(0) < n_ctx_blocks
    x = jnp.where(is_ctx, xc_ref[...], xl_ref[...] + pos_ref[...])
    y = jnp.where(is_ctx, ys5c_ref[...], ys5l_ref[...])
    y_hy = jnp.where(is_ctx, yhyc_ref[...], yhyl_ref[...])
    s5 = jax.nn.gelu(y) * jax.nn.sigmoid(_dot(y.astype(BF16), wglu_ref[...]) + bglu_ref[...])
    m_hy = _rms(y_hy) * ong_ref[:, 0:D_HY]
    m_s5 = _rms(s5) * ong_ref[:, D_HY:]
    o = (_dot(m_hy.astype(BF16), wout_ref[0:D_HY, :]) + _dot(m_s5.astype(BF16), wout_ref[D_HY:, :]))
    g1 = mod_ref[0, :, 2 * D_MODEL:3 * D_MODEL]
    sh2 = mod_ref[0, :, 3 * D_MODEL:4 * D_MODEL]
    sc2 = mod_ref[0, :, 4 * D_MODEL:5 * D_MODEL]
    x1 = _norm(ALPHA * x + g1 * o) * ln1g_ref[...] + ln1b_ref[...]
    x1_ref[...] = x1
    h2 = _norm(x1) * (1.0 + sc2) + sh2
    h2_ref[...] = h2.astype(BF16)
    hh, hl = _split(h2)
    logits = (_dot(hh, wrh_ref[...]) + _dot(hl, wrh_ref[...]) + _dot(hh, wrl_ref[...]) + br_ref[...])
    gates = _route(logits)
    gate_ref[...] = gates
    cnt_ref[0] = jnp.sum(gates, axis=0, keepdims=True)


def _out_proj(xc, xl, pos, yhy_c, yhy_l, ys5_c, ys5_l, mod, wglu_bf, bglu, ong, wout_bf, ln1g, ln1b,
              wr_hi, wr_lo, br, tm):
    n_ctx, n_lat = xc.shape[0], xl.shape[0]
    l_lat = pos.shape[0]
    ncb, nlb, npb = n_ctx // tm, n_lat // tm, l_lat // tm
    ctx = lambda w: pl.BlockSpec((tm, w), lambda i: (jnp.minimum(i, ncb - 1), 0))
    lat = lambda w: pl.BlockSpec((tm, w), lambda i: (jnp.maximum(i - ncb, 0), 0))
    full = lambda shape: pl.BlockSpec(shape, lambda i: (0,) * len(shape))
    out = lambda w: pl.BlockSpec((tm, w), lambda i: (i, 0))
    mod_idx = lambda i: (jnp.where(i < ncb, 0, 1 + jnp.maximum(i - ncb, 0) // npb), 0, 0)
    n_all = n_ctx + n_lat
    return pl.pallas_call(
        functools.partial(_out_kernel, ncb),
        grid=(ncb + nlb,),
        in_specs=[ctx(D_MODEL), lat(D_MODEL),
                  pl.BlockSpec((tm, D_MODEL), lambda i: (jnp.maximum(i - ncb, 0) % npb, 0)),
                  ctx(D_HY), lat(D_HY), ctx(D_S5), lat(D_S5),
                  pl.BlockSpec((1, 1, 6 * D_MODEL), mod_idx),
                  full((D_S5, D_S5)), full((1, D_S5)), full((1, D_MODEL)), full((D_MODEL, D_MODEL)),
                  full((1, D_MODEL)), full((1, D_MODEL)), full((D_MODEL, LANES)), full((D_MODEL, LANES)),
                  full((1, LANES))],
        out_specs=[out(D_MODEL), out(D_MODEL), out(LANES), pl.BlockSpec((1, 1, LANES), lambda i: (i, 0, 0))],
        out_shape=[jax.ShapeDtypeStruct((n_all, D_MODEL), F32),
                   jax.ShapeDtypeStruct((n_all, D_MODEL), BF16),
                   jax.ShapeDtypeStruct((n_all, LANES), F32),
                   jax.ShapeDtypeStruct((n_all // tm, 1, LANES), F32)],
        compiler_params=_cparams(("arbitrary",)),
        name="out_proj",
    )(xc, xl, pos, yhy_c, yhy_l, ys5_c, ys5_l, mod.reshape(mod.shape[0], 1, 6 * D_MODEL),
      wglu_bf, bglu, ong, wout_bf, ln1g, ln1b, wr_hi, wr_lo, br)


def _perm_t(gates, loc_ref, s):
    n = gates.shape[0]
    lane = lax.broadcasted_iota(jnp.int32, gates.shape, 1)
    oh = jnp.where((lane >= N_EXPERTS) & (lane < N_EXPERTS + N_EGROUPS), gates, 0.0)
    r = lax.broadcasted_iota(jnp.int32, (n, n), 0)
    c = lax.broadcasted_iota(jnp.int32, (n, n), 1)
    earlier = jnp.where(c < r, 1.0, 0.0).astype(BF16)
    cum = _dot(earlier, oh.astype(BF16))
    rank = jnp.sum(cum * oh, axis=-1, keepdims=True)
    lane1 = lax.broadcasted_iota(jnp.int32, (1, LANES), 1)
    locv = jnp.zeros((1, LANES), F32)
    for grp in range(N_EGROUPS):
        locv = jnp.where(lane1 == N_EXPERTS + grp, loc_ref[N_EGROUPS * s + grp].astype(F32), locv)
    dest = rank + jnp.sum(oh * locv, axis=-1, keepdims=True)
    slot = lax.broadcasted_iota(jnp.int32, (n, MOE_SLOTS), 1).astype(F32)
    return jnp.where(slot == dest, 1.0, 0.0)


def _segment_copies(s, loc_ref, len_ref, off_ref, make):
    for grp in range(N_EGROUPS):
        loc = loc_ref[N_EGROUPS * s + grp]
        off = off_ref[N_EGROUPS * s + grp]
        n_units = len_ref[N_EGROUPS * s + grp] // MOE_UNIT

        def body(i, carry):
            make(pl.multiple_of(loc + MOE_UNIT * i, MOE_UNIT), pl.multiple_of(off + MOE_UNIT * i, MOE_UNIT))
            return carry

        lax.fori_loop(0, n_units, body, 0)


def _moe_sort_kernel(loc_ref, len_ref, off_ref, h_ref, gate_ref, xs_in, gs_in, xs_hbm, gs_hbm,
                     xs_v, gs_v, sem):
    del xs_in, gs_in
    s = pl.program_id(0)
    slot = s % 2
    gates = gate_ref[...]
    p = _perm_t(gates, loc_ref, s).T.astype(BF16)
    xs_v[slot] = _dot(p, h_ref[...]).astype(BF16)
    g_hi = gates.astype(BF16)
    r1 = gates - g_hi.astype(F32)
    g_mid = r1.astype(BF16)
    g_lo = (r1 - g_mid.astype(F32)).astype(BF16)
    gs_v[slot] = _dot(p, g_hi) + _dot(p, g_mid) + _dot(p, g_lo)

    def copies(buf):
        def x_copy(lr, gr):
            return pltpu.make_async_copy(xs_v.at[buf, pl.ds(lr, MOE_UNIT), :],
                                         xs_hbm.at[pl.ds(gr, MOE_UNIT), :], sem.at[0, buf])

        def g_copy(lr, gr):
            return pltpu.make_async_copy(gs_v.at[buf, pl.ds(lr, MOE_UNIT), :],
                                         gs_hbm.at[pl.ds(gr, MOE_UNIT), :], sem.at[1, buf])

        def start(lr, gr):
            x_copy(lr, gr).start()
            g_copy(lr, gr).start()

        def wait(lr, gr):
            x_copy(lr, gr).wait()
            g_copy(lr, gr).wait()

        return start, wait

    _segment_copies(s, loc_ref, len_ref, off_ref, copies(slot)[0])

    @pl.when(s > 0)
    def _():
        _segment_copies(s - 1, loc_ref, len_ref, off_ref, copies(1 - slot)[1])

    @pl.when(s == pl.num_programs(0) - 1)
    def _():
        _segment_copies(s, loc_ref, len_ref, off_ref, copies(slot)[1])


def _moe_expert_kernel(bg_ref, nb_ref, xs_ref, gs_ref, wg_ref, wu_ref, wd_ref, o_ref):
    i = pl.program_id(0)

    @pl.when(i < nb_ref[0])
    def _():
        grp = bg_ref[i]
        x = xs_ref[...]
        gates = gs_ref[...]
        lane = lax.broadcasted_iota(jnp.int32, gates.shape, 1)
        acc = jnp.zeros(o_ref.shape, F32)
        for e in range(N_EPG):
            a = _dot(x, wg_ref[e].astype(BF16))
            u = _dot(x, wu_ref[e].astype(BF16))
            ge = jnp.sum(jnp.where(lane == N_EPG * grp + e, gates, 0.0), axis=-1, keepdims=True)
            hid = jax.nn.silu(a) * u * ge
            acc = acc + _dot(hid.astype(BF16), wd_ref[e].astype(BF16))
        o_ref[...] = acc.astype(BF16)

    @pl.when(i >= nb_ref[0])
    def _():
        o_ref[...] = jnp.zeros_like(o_ref)


def _moe_combine_kernel(loc_ref, len_ref, off_ref, gate_ref, x1_ref, mod_ref, ln2g_ref, ln2b_ref, o_hbm,
                        ctx_ref, lat_ref, o_v, sem, *, n_ctx_tiles):
    s = pl.program_id(0)
    slot = s % 2

    def copies(buf):
        def o_copy(lr, gr):
            return pltpu.make_async_copy(o_hbm.at[pl.ds(gr, MOE_UNIT), :],
                                         o_v.at[buf, pl.ds(lr, MOE_UNIT), :], sem.at[buf])

        return (lambda lr, gr: o_copy(lr, gr).start()), (lambda lr, gr: o_copy(lr, gr).wait())

    @pl.when(s == 0)
    def _():
        o_v[...] = jnp.zeros_like(o_v)
        _segment_copies(s, loc_ref, len_ref, off_ref, copies(slot)[0])

    @pl.when(s + 1 < pl.num_programs(0))
    def _():
        _segment_copies(s + 1, loc_ref, len_ref, off_ref, copies(1 - slot)[0])

    pt = _perm_t(gate_ref[...], loc_ref, s).astype(BF16)
    _segment_copies(s, loc_ref, len_ref, off_ref, copies(slot)[1])
    f = _dot(pt, o_v[slot])
    g2 = mod_ref[0, :, 5 * D_MODEL:6 * D_MODEL]
    x2 = _norm(ALPHA * x1_ref[...] + g2 * f) * ln2g_ref[...] + ln2b_ref[...]

    @pl.when(s < n_ctx_tiles)
    def _():
        ctx_ref[...] = x2

    @pl.when(s >= n_ctx_tiles)
    def _():
        lat_ref[...] = x2


def _moe_plan(tile_counts, n_blocks):
    cnt = tile_counts[:, 0, N_EXPERTS:N_EXPERTS + N_EGROUPS].astype(jnp.int32)
    len16 = ((cnt + MOE_UNIT - 1) // MOE_UNIT) * MOE_UNIT
    loc = jnp.cumsum(len16, axis=1) - len16
    rows_g = jnp.sum(len16, axis=0)
    reg_g = ((rows_g + MOE_TM - 1) // MOE_TM) * MOE_TM
    reg_start = jnp.cumsum(reg_g) - reg_g
    off = reg_start[None, :] + jnp.cumsum(len16, axis=0) - len16
    blk_end = jnp.cumsum(reg_g // MOE_TM)
    bi = jnp.arange(n_blocks, dtype=jnp.int32)
    blk_group = jnp.minimum(jnp.sum((bi[:, None] >= blk_end[None, :]).astype(jnp.int32), axis=1),
                            N_EGROUPS - 1)
    flat = lambda a: a.reshape(-1).astype(jnp.int32)
    return flat(loc), flat(len16), flat(off), blk_group.astype(jnp.int32), blk_end[-1:].astype(jnp.int32)


def _moe(h2_all, gates_all, tile_counts, x1_all, mod, w_gate, w_up, w_down, ln2g, ln2b, n_ctx,
         tokens_per_mod_row):
    n_tok = h2_all.shape[0]
    n_tiles = n_tok // MOE_ST
    n_ctx_tiles = n_ctx // MOE_ST
    max_rows = n_tok + n_tiles * N_EGROUPS * (MOE_UNIT - 1) + N_EGROUPS * (MOE_TM - 1)
    n_blocks = -(-max_rows // MOE_TM)
    n_rows = n_blocks * MOE_TM
    loc, len16, off, blk_group, n_used = _moe_plan(tile_counts, n_blocks)

    tile = lambda w: pl.BlockSpec((MOE_ST, w), lambda s, *_: (s, 0))
    anyspec = pl.BlockSpec(memory_space=pl.ANY)
    xs, gs = pl.pallas_call(
        _moe_sort_kernel,
        grid_spec=pltpu.PrefetchScalarGridSpec(
            num_scalar_prefetch=3, grid=(n_tiles,),
            in_specs=[tile(D_MODEL), tile(LANES), anyspec, anyspec],
            out_specs=[anyspec, anyspec],
            scratch_shapes=[pltpu.VMEM((2, MOE_SLOTS, D_MODEL), BF16), pltpu.VMEM((2, MOE_SLOTS, LANES), F32),
                            pltpu.SemaphoreType.DMA((2, 2))]),
        out_shape=[jax.ShapeDtypeStruct((n_rows, D_MODEL), BF16),
                   jax.ShapeDtypeStruct((n_rows, LANES), F32)],
        input_output_aliases={5: 0, 6: 1},
        compiler_params=_cparams(("arbitrary",)),
        name="moe_sort",
    )(loc, len16, off, h2_all, gates_all, jnp.zeros((n_rows, D_MODEL), BF16), jnp.zeros((n_rows, LANES), F32))

    blk = lambda w: pl.BlockSpec((MOE_TM, w), lambda i, bg, nb: (jnp.minimum(i, nb[0] - 1), 0))
    wspec = lambda a, b: pl.BlockSpec((N_EPG, a, b), lambda i, bg, nb: (bg[i], 0, 0),
                                      pipeline_mode=pl.Buffered(1))
    o_sorted = pl.pallas_call(
        _moe_expert_kernel,
        grid_spec=pltpu.PrefetchScalarGridSpec(
            num_scalar_prefetch=2, grid=(n_blocks,),
            in_specs=[blk(D_MODEL), blk(LANES), wspec(D_MODEL, D_EXPERT), wspec(D_MODEL, D_EXPERT),
                      wspec(D_EXPERT, D_MODEL)],
            out_specs=pl.BlockSpec((MOE_TM, D_MODEL), lambda i, bg, nb: (i, 0))),
        out_shape=jax.ShapeDtypeStruct((n_rows, D_MODEL), BF16),
        compiler_params=_cparams(("arbitrary",)),
        name="moe_experts",
    )(blk_group, n_used, xs, gs, w_gate, w_up, w_down)

    lat_per_row = tokens_per_mod_row // MOE_ST

    def mod_idx(s, *_):
        return (jnp.where(s < n_ctx_tiles, 0, 1 + (s - n_ctx_tiles) // lat_per_row), 0, 0)

    vec = pl.BlockSpec((1, D_MODEL), lambda s, *_: (0, 0))
    return pl.pallas_call(
        functools.partial(_moe_combine_kernel, n_ctx_tiles=n_ctx_tiles),
        grid_spec=pltpu.PrefetchScalarGridSpec(
            num_scalar_prefetch=3, grid=(n_tiles,),
            in_specs=[tile(LANES), tile(D_MODEL), pl.BlockSpec((1, 1, 6 * D_MODEL), mod_idx), vec, vec,
                      anyspec],
            out_specs=[pl.BlockSpec((MOE_ST, D_MODEL), lambda s, *_: (jnp.minimum(s, n_ctx_tiles - 1), 0)),
                       pl.BlockSpec((MOE_ST, D_MODEL), lambda s, *_: (jnp.maximum(s - n_ctx_tiles, 0), 0))],
            scratch_shapes=[pltpu.VMEM((2, MOE_SLOTS, D_MODEL), BF16), pltpu.SemaphoreType.DMA((2,))]),
        out_shape=[jax.ShapeDtypeStruct((n_ctx, D_MODEL), F32),
                   jax.ShapeDtypeStruct((n_tok - n_ctx, D_MODEL), F32)],
        compiler_params=_cparams(("arbitrary",)),
        name="moe_combine",
    )(loc, len16, off, gates_all, x1_all, mod.reshape(mod.shape[0], 1, 6 * D_MODEL), ln2g, ln2b, o_sorted)


def _grid_pos_embed(n_tokens):
    rows = n_tokens // GRID_W
    row = np.repeat(np.arange(rows, dtype=np.float64), GRID_W)
    col = np.tile(np.arange(GRID_W, dtype=np.float64), rows)
    quarter = D_MODEL // 4
    omega = 1.0 / (POS_BASE ** (np.arange(quarter, dtype=np.float64) / quarter))
    er = row[:, None] * omega
    ec = col[:, None] * omega
    return jnp.asarray(np.concatenate([np.sin(er), np.cos(er), np.sin(ec), np.cos(ec)], axis=-1), F32)


def _mixers(x, pos, mod3, h0_re, h0_im, tabs, filt, s5ops, wts, tm):
    bsz, n_tok, _ = x.shape
    shared = mod3.shape[0] == 1
    x3 = x.reshape(1, bsz * n_tok, D_MODEL) if shared else x
    proj_hy, u_s5 = _in_proj(x3, pos, mod3, wts['w_in'], tm)
    y_hy = _hyena(proj_hy.reshape(bsz, n_tok, 3 * D_HY), tabs, filt,
                  wts['hy_conv_w'], wts['hy_conv_b'], wts['hy_fbias'])
    y_s5, f_re, f_im = _s5(u_s5.reshape(bsz, n_tok, D_S5), s5ops, h0_re, h0_im)
    return y_hy.reshape(bsz * n_tok, D_HY), y_s5.reshape(bsz * n_tok, D_S5), f_re, f_im


def kernel(x_prompt, x_sample, state_s5_re, state_s5_im, c, c_ctx, w_ada, b_ada, w_in, hy_conv_w, hy_conv_b, hy_f_w1, hy_f_b1, hy_f_w2, hy_f_b2, hy_f_w3, hy_freq, hy_fbias, s5_a_re, s5_a_im, s5_log_dt, s5_b_re, s5_b_im, s5_c_re, s5_c_im, s5_d, s5_w_glu, s5_b_glu, out_norm_g, w_out, ln1_g, ln1_b, moe_w_r1, moe_b_r1, moe_w_r2, moe_b_r2, moe_w_gate, moe_w_up, moe_w_down, ln2_g, ln2_b):
    b_ctx, l_ctx, _ = x_prompt.shape
    b_lat, l_lat, _ = x_sample.shape
    g, p = S5_GROUPS, S5_STATE
    assert w_ada.shape[0] == 1, "single-layer trunk"
    l = 0

    nrow = 16
    cond = jnp.concatenate([c_ctx[None, :], c, jnp.zeros((nrow - 1 - b_lat, D_MODEL), F32)], axis=0)
    mod = _ada(cond, w_ada[l], b_ada[l])
    mod_ctx = mod[0:1].reshape(1, 1, 6 * D_MODEL)
    mod_lat = mod[1:1 + b_lat].reshape(b_lat, 1, 6 * D_MODEL)

    wr = jnp.concatenate([moe_w_r2[l].transpose(1, 0, 2).reshape(D_MODEL, N_EXPERTS), moe_w_r1[l]], axis=1)
    wr = jnp.pad(wr, ((0, 0), (0, LANES - wr.shape[1])))
    br = jnp.concatenate([moe_b_r2[l].reshape(-1), moe_b_r1[l]])
    br = jnp.pad(br, (0, LANES - br.shape[0])).reshape(1, LANES)
    wr_hi, wr_lo = _split(wr)

    wts = {
        'w_in': w_in[l].astype(BF16), 'hy_conv_w': hy_conv_w[l], 'hy_conv_b': hy_conv_b[l],
        'hy_fbias': hy_fbias[l], 'w_glu': s5_w_glu[l].astype(BF16), 'b_glu': s5_b_glu[l].reshape(1, -1),
        'out_norm_g': out_norm_g[l].reshape(1, -1), 'w_out': w_out[l].astype(BF16),
        'ln1_g': ln1_g[l].reshape(1, -1), 'ln1_b': ln1_b[l].reshape(1, -1),
        'wr_hi': wr_hi, 'wr_lo': wr_lo, 'br': br,
        'w_gate': moe_w_gate[l], 'w_up': moe_w_up[l], 'w_down': moe_w_down[l],
        'ln2_g': ln2_g[l].reshape(1, -1), 'ln2_b': ln2_b[l].reshape(1, -1),
    }

    s5ops = _s5_operators(s5_a_re[l], s5_a_im[l], s5_log_dt[l], s5_b_re[l], s5_b_im[l],
                          s5_c_re[l], s5_c_im[l], s5_d[l])
    tabs_ctx = _tables(l_ctx)
    tabs_lat = _tables(l_lat)
    filt_args = (hy_f_w1[l], hy_f_b1[l], hy_f_w2[l], hy_f_b2[l], hy_f_w3[l], hy_freq[l])
    filt_ctx = _hyena_filters(l_ctx, tabs_ctx, *filt_args)
    filt_lat = _hyena_filters(l_lat, tabs_lat, *filt_args)

    zero = jnp.zeros((g, b_ctx, 2 * p), F32)
    yhy_c, ys5_c, f_re, f_im = _mixers(x_prompt, None, mod_ctx, zero, zero, tabs_ctx, filt_ctx, s5ops, wts, 512)
    unpack = lambda f: f.reshape(g, b_ctx, 2, p).transpose(1, 2, 0, 3)[:, None]
    new_re, new_im = unpack(f_re), unpack(f_im)

    pack = lambda s: s[:, l].transpose(2, 0, 1, 3).reshape(g, b_lat, 2 * p)
    pos = _grid_pos_embed(l_lat)
    yhy_l, ys5_l, _, _ = _mixers(x_sample, pos, mod_lat, pack(state_s5_re), pack(state_s5_im),
                                 tabs_lat, filt_lat, s5ops, wts, 512)

    n_ctx = b_ctx * l_ctx
    x1_all, h2_all, gates_all, tile_counts = _out_proj(
        x_prompt.reshape(n_ctx, D_MODEL), x_sample.reshape(b_lat * l_lat, D_MODEL), pos,
        yhy_c, yhy_l, ys5_c, ys5_l, mod, wts['w_glu'], wts['b_glu'], wts['out_norm_g'], wts['w_out'],
        wts['ln1_g'], wts['ln1_b'], wts['wr_hi'], wts['wr_lo'], wts['br'], MOE_ST)
    y_ctx, y_lat = _moe(h2_all, gates_all, tile_counts, x1_all, mod,
                        wts['w_gate'], wts['w_up'], wts['w_down'], wts['ln2_g'], wts['ln2_b'],
                        n_ctx, l_lat)
    return (y_ctx.reshape(x_prompt.shape), y_lat.reshape(x_sample.shape), new_re, new_im)
```

```python
import functools
import math

import numpy as np
import jax
import jax.numpy as jnp
from jax import lax
from jax.experimental import pallas as pl
from jax.experimental.pallas import tpu as pltpu

F32 = jnp.float32
BF16 = jnp.bfloat16

D_MODEL = 1024
DEPTH = 1
GRID_W = 64
POS_BASE = 10000.0
D_HY = 512
D_S5 = 512
S5_CH = 16
S5_GROUPS = 32
S5_STATE = 64
S5_CHUNK = 16
S5_ROW = S5_CHUNK * S5_CH
HY_BANDS = 16
HY_EMB = 1 + 2 * HY_BANDS
HY_HID = 64
HY_MIN_DECAY = math.log(1e-2) / 1.5
HY_MAX_DECAY = math.log(1e-2) / 0.3
N_EGROUPS = 4
N_EPG = 4
N_EXPERTS = 16
D_EXPERT = 512
LN_EPS = 1e-5
ALPHA = (2.0 * DEPTH) ** 0.25
LANES = 128
S5_GB = LANES // S5_CH
S5OPS_GB = 4
HY_CW = 512
MOE_ST = 512
MOE_SLOTS = 640
MOE_UNIT = 16
MOE_TM = 512
VMEM_LIMIT = 60000 * 1024


def _cparams(sem):
    return pltpu.CompilerParams(dimension_semantics=sem, vmem_limit_bytes=VMEM_LIMIT)


def _split(x):
    hi = x.astype(BF16)
    lo = (x - hi.astype(F32)).astype(BF16)
    return hi, lo


def _dot(a, b):
    return jnp.dot(a, b, preferred_element_type=F32)


def _dot_t(a, b):
    return lax.dot_general(a, b, (((1,), (1,)), ((), ())), preferred_element_type=F32)


def _mm3(a, b):
    ah, al = _split(a)
    bh, bl = _split(b)
    return _dot(ah, bh) + _dot(al, bh) + _dot(ah, bl)


def _mm3_t(a, b):
    ah, al = _split(a)
    bh, bl = _split(b)
    return _dot_t(ah, bh) + _dot_t(al, bh) + _dot_t(ah, bl)


def _norm(x):
    xc = x - jnp.mean(x, axis=-1, keepdims=True)
    return xc * lax.rsqrt(jnp.mean(xc * xc, axis=-1, keepdims=True) + LN_EPS)


def _rms(y):
    return y * lax.rsqrt(jnp.mean(y * y, axis=-1, keepdims=True) + LN_EPS)


def _ada_kernel(cond_ref, w_ref, b_ref, o_ref):
    c = jax.nn.silu(cond_ref[...])
    o_ref[...] = _mm3(c, w_ref[...]) + b_ref[...]


def _ada(cond, w_ada, b_ada):
    nb = cond.shape[0]
    n = w_ada.shape[1]
    tn = 1024
    return pl.pallas_call(
        _ada_kernel,
        grid=(n // tn,),
        in_specs=[pl.BlockSpec((nb, D_MODEL), lambda j: (0, 0)),
                  pl.BlockSpec((D_MODEL, tn), lambda j: (0, j)),
                  pl.BlockSpec((1, tn), lambda j: (0, j))],
        out_specs=pl.BlockSpec((nb, tn), lambda j: (0, j)),
        out_shape=jax.ShapeDtypeStruct((nb, n), F32),
        compiler_params=_cparams(("arbitrary",)),
        name="ada",
    )(cond, w_ada, b_ada.reshape(1, n))


def _dft_tables(n_half):
    n = 2 * n_half
    idx = np.arange(n_half, dtype=np.int64)
    m = (idx[:, None] * idx[None, :]) % n
    ang = 2.0 * np.pi * m.astype(np.float64) / n
    cm = np.cos(ang)
    sm = -np.sin(ang)
    sm[0, :] = 1.0 - 2.0 * (idx % 2)
    return cm.astype(np.float32), sm.astype(np.float32)


def _tables(n_tok):
    n_half = n_tok // 2
    cm, sm = _dft_tables(n_half)
    mats = tuple(jnp.asarray(t).astype(BF16) for t in (cm, sm, np.ascontiguousarray(sm.T)))
    ang = np.pi * np.arange(n_half, dtype=np.float64) / n_tok
    tw = [np.broadcast_to(v[:, None], (n_half, HY_CW)).astype(np.float32) for v in (np.cos(ang), -np.sin(ang))]
    return mats + (jnp.asarray(tw[0]), jnp.asarray(tw[1]))


def _put_cols(ref, x):
    for j in range(ref.shape[0]):
        ref[j] = x[:, LANES * j:LANES * (j + 1)]


def _get_cols(ref):
    return jnp.concatenate([ref[j] for j in range(ref.shape[0])], axis=1)


def _get_parity(ref, parity):
    n_half = ref.shape[1] // 2
    return jnp.concatenate([ref[j, pl.ds(parity, n_half, stride=2), :] for j in range(ref.shape[0])], axis=1)


def _put_parity(ref, parity, x):
    n_half = ref.shape[1] // 2
    for j in range(ref.shape[0]):
        ref[j, pl.ds(parity, n_half, stride=2), :] = x[:, LANES * j:LANES * (j + 1)]


def _set_row0(x, v):
    first = lax.broadcasted_iota(jnp.int32, (8, x.shape[1]), 0) == 0
    return jnp.concatenate([jnp.where(first, v, x[:8]), x[8:]], axis=0)


def _rfft_packed(x_ref, cm, sm, tw_re, tw_im):
    xe = _get_parity(x_ref, 0).astype(BF16)
    xo = _get_parity(x_ref, 1).astype(BF16)
    e_re, e_im = _dot(cm, xe), _dot(sm, xe)
    o_re, o_im = _dot(cm, xo), _dot(sm, xo)
    t_re = tw_re * o_re - tw_im * o_im
    t_im = tw_re * o_im + tw_im * o_re
    a_im = _set_row0(e_im + t_im, e_im[0:1])
    b_im = _set_row0(t_im - e_im, -o_im[0:1])
    return e_re + t_re, a_im, e_re - t_re, b_im


def _irfft_packed(y_ref, ya_re, ya_im, yb_re, yb_im, cm, st, tw_re, tw_im):
    p_e = ya_re + yb_re
    q_e = _set_row0(ya_im - yb_im, ya_im[0:1])
    _put_parity(y_ref, 0, _dot(cm, p_e.astype(BF16)) + _dot(st, q_e.astype(BF16)))
    ra_re = ya_re * tw_re + ya_im * tw_im
    ra_im = ya_im * tw_re - ya_re * tw_im
    rb_re = yb_im * tw_im - yb_re * tw_re
    rb_im = -(yb_re * tw_im + yb_im * tw_re)
    p_o = ra_re + rb_re
    q_o = _set_row0(ra_im - rb_im, -yb_im[0:1])
    _put_parity(y_ref, 1, _dot(cm, p_o.astype(BF16)) + _dot(st, q_o.astype(BF16)))


def _filt_kernel(n_tok, z_ref, t_ref, w1_ref, b1_ref, w2_ref, b2_ref, fr_ref, w3f_ref, w3b_ref,
                 dl_ref, cm_ref, sm_ref, twr_ref, twi_ref, kar_ref, kai_ref, kbr_ref, kbi_ref, p_ref, q_ref):
    fr = fr_ref[...]
    h = jnp.sin(fr * (_mm3(z_ref[...], w1_ref[...]) + b1_ref[...]))
    h = jnp.sin(fr * (_mm3(h, w2_ref[...]) + b2_ref[...]))
    decay = jnp.exp(-t_ref[...] * dl_ref[...])
    row = lax.broadcasted_iota(jnp.int32, decay.shape, 0)
    hf = _mm3(h, w3f_ref[...]) * decay
    hb = jnp.where(row == 0, 0.0, _mm3(h, w3b_ref[...]) * decay)
    _put_cols(p_ref, hf + hb)
    _put_cols(q_ref, hf - hb)
    cm, sm, tw_re, tw_im = cm_ref[...], sm_ref[...], twr_ref[...], twi_ref[...]
    row0 = lax.broadcasted_iota(jnp.int32, tw_re.shape, 0) == 0
    pa_re, pa_im, pb_re, _ = _rfft_packed(p_ref, cm, sm, tw_re, tw_im)
    _, qa_im, _, qb_im = _rfft_packed(q_ref, cm, sm, tw_re, tw_im)
    inv_n = 1.0 / (2 * n_tok)
    w_re = jnp.where(row0, inv_n, 2.0 * inv_n)
    kar_ref[...] = w_re * pa_re
    kbr_ref[...] = w_re * pb_re
    kai_ref[...] = (2.0 * inv_n) * _set_row0(qa_im, pa_im[0:1])
    kbi_ref[...] = (2.0 * inv_n) * qb_im


def _hyena_filters(n_tok, tabs, hy_f_w1, hy_f_b1, hy_f_w2, hy_f_b2, hy_f_w3, hy_freq):
    cm, sm, _, tw_re, tw_im = tabs
    n_half = n_tok // 2
    t = jnp.linspace(0.0, 1.0, n_tok, dtype=F32)[:, None]
    wv = 2.0 * math.pi * jnp.arange(n_tok, dtype=F32) / n_tok
    fb = jnp.linspace(1e-4, HY_BANDS - 1, HY_BANDS, dtype=F32)
    ang = wv[:, None] * fb[None, :]
    z = jnp.concatenate([t, jnp.cos(ang), -jnp.sin(ang)], axis=-1)
    z = jnp.pad(z, ((0, 0), (0, LANES - HY_EMB)))
    w1 = jnp.pad(hy_f_w1, ((0, LANES - HY_EMB), (0, 0)))
    deltas = jnp.abs(jnp.linspace(HY_MIN_DECAY, HY_MAX_DECAY, D_HY, dtype=F32))[None, :]
    ncb = D_HY // HY_CW
    full = lambda j: (0, 0)
    out_sd = jax.ShapeDtypeStruct((n_half, 2 * D_HY), F32)
    mat = pl.BlockSpec((n_half, n_half), full, pipeline_mode=pl.Buffered(1))
    twb = pl.BlockSpec((n_half, HY_CW), full, pipeline_mode=pl.Buffered(1))
    return pl.pallas_call(
        functools.partial(_filt_kernel, n_tok),
        grid=(2 * ncb,),
        in_specs=[pl.BlockSpec((n_tok, LANES), full),
                  pl.BlockSpec((n_tok, 1), full),
                  pl.BlockSpec((LANES, HY_HID), full),
                  pl.BlockSpec((1, HY_HID), full),
                  pl.BlockSpec((HY_HID, HY_HID), full),
                  pl.BlockSpec((1, HY_HID), full),
                  pl.BlockSpec((1, HY_HID), full),
                  pl.BlockSpec((HY_HID, HY_CW), lambda j: (0, 2 * ncb * (j // ncb) + j % ncb)),
                  pl.BlockSpec((HY_HID, HY_CW), lambda j: (0, 2 * ncb * (j // ncb) + ncb + j % ncb)),
                  pl.BlockSpec((1, HY_CW), lambda j: (0, j % ncb)),
                  mat, mat, twb, twb],
        out_specs=[pl.BlockSpec((n_half, HY_CW), lambda j: (0, j))] * 4,
        out_shape=[out_sd] * 4,
        scratch_shapes=[pltpu.VMEM((HY_CW // LANES, n_tok, LANES), F32)] * 2,
        compiler_params=_cparams(("arbitrary",)),
        name=f"filt{n_tok}",
    )(z, t, w1, hy_f_b1.reshape(1, -1), hy_f_w2, hy_f_b2.reshape(1, -1), hy_freq.reshape(1, -1),
      hy_f_w3, hy_f_w3, deltas, cm, sm, tw_re, tw_im)


def _hyena_kernel(pv_ref, p1_ref, p2_ref, cwv_ref, cw1_ref, cw2_ref, cbv_ref, cb1_ref, cb2_ref,
                  fbias_ref, cm_ref, sm_ref, st_ref, twr_ref, twi_ref,
                  kar0_ref, kai0_ref, kbr0_ref, kbi0_ref, kar1_ref, kai1_ref, kbr1_ref, kbi1_ref,
                  o_ref, u_ref, y_ref):
    n_tok = pv_ref.shape[1]
    row = lax.broadcasted_iota(jnp.int32, (n_tok, pv_ref.shape[2]), 0)

    def short_conv(p_ref, cw_ref, cb_ref):
        p = p_ref[0]
        prev = jnp.where(row == 0, 0.0, pltpu.roll(p, 1, axis=0))
        nxt = jnp.where(row == n_tok - 1, 0.0, pltpu.roll(p, n_tok - 1, axis=0))
        return cb_ref[...] + prev * cw_ref[0:1, :] + p * cw_ref[1:2, :] + nxt * cw_ref[2:3, :]

    cm, sm, st, tw_re, tw_im = cm_ref[...], sm_ref[...], st_ref[...], twr_ref[...], twi_ref[...]

    def fftconv(u, kar_ref, kai_ref, kbr_ref, kbi_ref, skip):
        _put_cols(u_ref, u)
        ua_re, ua_im, ub_re, ub_im = _rfft_packed(u_ref, cm, sm, tw_re, tw_im)
        ka_re, ka_im, kb_re, kb_im = kar_ref[...], kai_ref[...], kbr_ref[...], kbi_ref[...]
        zero_row = jnp.zeros_like(ka_im[0:1])
        kaz = _set_row0(ka_im, zero_row)
        kbz = _set_row0(kb_im, zero_row)
        ya_re = ua_re * ka_re - ua_im * kaz
        yb_re = ub_re * kb_re - ub_im * kbz
        h_re = ua_im[0:1] * ka_im[0:1] - ub_im[0:1] * kb_im[0:1]
        h_im = ua_im[0:1] * kb_im[0:1] + ub_im[0:1] * ka_im[0:1]
        ya_im = _set_row0(ua_re * ka_im + ua_im * ka_re, h_re)
        yb_im = _set_row0(ub_re * kb_im + ub_im * kb_re, h_im)
        _irfft_packed(y_ref, ya_re, ya_im, yb_re, yb_im, cm, st, tw_re, tw_im)
        return _get_cols(y_ref) + u * skip

    v = short_conv(pv_ref, cwv_ref, cbv_ref)
    x1 = short_conv(p1_ref, cw1_ref, cb1_ref)
    z = x1 * fftconv(v, kar0_ref, kai0_ref, kbr0_ref, kbi0_ref, fbias_ref[0:1, :])
    x2 = short_conv(p2_ref, cw2_ref, cb2_ref)
    o_ref[0] = x2 * fftconv(z, kar1_ref, kai1_ref, kbr1_ref, kbi1_ref, fbias_ref[1:2, :])


def _hyena(proj_hy, tabs, filt, hy_conv_w, hy_conv_b, hy_fbias):
    bsz, n_tok, _ = proj_hy.shape
    n_half = n_tok // 2
    ncb = D_HY // HY_CW
    cm, sm, st, tw_re, tw_im = tabs
    cb = hy_conv_b.reshape(1, -1)
    const = lambda shape: pl.BlockSpec(shape, lambda b, c: (0, 0), pipeline_mode=pl.Buffered(1))
    mat = const((n_half, n_half))
    twb = const((n_half, HY_CW))

    def pspec(k):
        return pl.BlockSpec((1, n_tok, HY_CW), lambda b, c: (b, 0, k * ncb + c))

    def cwspec(k):
        return pl.BlockSpec((3, HY_CW), lambda b, c: (0, k * ncb + c))

    def cbspec(k):
        return pl.BlockSpec((1, HY_CW), lambda b, c: (0, k * ncb + c))

    def fspec(o):
        mode = pl.Buffered(1) if ncb == 1 else None
        return pl.BlockSpec((n_half, HY_CW), lambda b, c: (0, o * ncb + c), pipeline_mode=mode)

    return pl.pallas_call(
        _hyena_kernel,
        grid=(bsz, ncb),
        in_specs=[pspec(0), pspec(1), pspec(2), cwspec(0), cwspec(1), cwspec(2),
                  cbspec(0), cbspec(1), cbspec(2),
                  pl.BlockSpec((2, HY_CW), lambda b, c: (0, c)),
                  mat, mat, mat, twb, twb] + [fspec(0)] * 4 + [fspec(1)] * 4,
        out_specs=pl.BlockSpec((1, n_tok, HY_CW), lambda b, c: (b, 0, c)),
        out_shape=jax.ShapeDtypeStruct((bsz, n_tok, D_HY), F32),
        scratch_shapes=[pltpu.VMEM((HY_CW // LANES, n_tok, LANES), F32)] * 2,
        compiler_params=_cparams(("arbitrary", "arbitrary")),
        name=f"hyena{n_tok}",
    )(proj_hy, proj_hy, proj_hy, hy_conv_w, hy_conv_w, hy_conv_w, cb, cb, cb, hy_fbias,
      cm, sm, st, tw_re, tw_im, *filt, *filt)


def _s5ops_kernel(*refs):
    for g in range(S5OPS_GB):
        _s5ops_group(g, *refs)


def _s5ops_group(g, are_ref, aim_ref, ldt_ref, btr_ref, bti_ref, cre_ref, cim_ref, d_ref,
                 mt_ref, erh_ref, erl_ref, eih_ref, eil_ref, gr_ref, gi_ref, atr_ref, ati_ref,
                 er_ref, ei_ref):
    a_re, a_im = are_ref[g], aim_ref[g]
    dt = jnp.exp(ldt_ref[g])
    mag = jnp.exp(a_re * dt)
    ab_re = mag * jnp.cos(a_im * dt)
    ab_im = mag * jnp.sin(a_im * dt)
    n_re, n_im = ab_re - 1.0, ab_im
    den = a_re * a_re + a_im * a_im
    q_re = (n_re * a_re + n_im * a_im) / den
    q_im = (n_im * a_re - n_re * a_im) / den
    bt_re, bt_im = btr_ref[g], bti_ref[g]
    bb_re = q_re * bt_re - q_im * bt_im
    bb_im = q_re * bt_im + q_im * bt_re
    c_re, c_im = cre_ref[g, 0:S5_CH, :], cim_ref[g, 0:S5_CH, :]
    pw = [(jnp.ones_like(ab_re), jnp.zeros_like(ab_re))]
    for _ in range(S5_CHUNK):
        pr, pi = pw[-1]
        pw.append((pr * ab_re - pi * ab_im, pr * ab_im + pi * ab_re))
    lane = lax.broadcasted_iota(jnp.int32, ab_re.shape, 1)
    fwd = lane < S5_STATE
    for s in range(S5_CHUNK):
        e_re = jnp.where(fwd, pw[S5_CHUNK - 1 - s][0], pw[s][0])
        e_im = jnp.where(fwd, pw[S5_CHUNK - 1 - s][1], pw[s][1])
        er_ref[g, pl.ds(S5_CH * s, S5_CH), :] = e_re * bb_re - e_im * bb_im
        ei_ref[g, pl.ds(S5_CH * s, S5_CH), :] = e_re * bb_im + e_im * bb_re
        g_re = jnp.where(fwd, pw[s + 1][0], pw[S5_CHUNK - s][0])
        g_im = jnp.where(fwd, pw[s + 1][1], pw[S5_CHUNK - s][1])
        gr_ref[g, pl.ds(S5_CH * s, S5_CH), :] = (c_re * g_re - c_im * g_im).astype(BF16)
        gi_ref[g, pl.ds(S5_CH * s, S5_CH), :] = (-(c_re * g_im + c_im * g_re)).astype(BF16)
    atr_ref[g] = pw[S5_CHUNK][0]
    ati_ref[g] = pw[S5_CHUNK][1]
    er, ei = er_ref[g], ei_ref[g]
    erh_ref[g], erl_ref[g] = _split(er)
    eih_ref[g], eil_ref[g] = _split(ei)
    lane2 = lax.broadcasted_iota(jnp.int32, er.shape, 1)
    row2 = lax.broadcasted_iota(jnp.int32, er.shape, 0)
    f2 = lane2 < S5_STATE
    zero = jnp.zeros_like(er)

    cp_re, cp_im = cre_ref[g], cim_ref[g]
    kf = _mm3_t(jnp.where(f2, er, zero), cp_re) - _mm3_t(jnp.where(f2, ei, zero), cp_im)
    kb = _mm3_t(jnp.where(f2, zero, er), cp_re) - _mm3_t(jnp.where(f2, zero, ei), cp_im)
    d_row = d_ref[g]
    steps_per_vreg = LANES // S5_CH
    for half in range(S5_CHUNK // steps_per_vreg):
        acc = zero
        for tt in range(steps_per_vreg):
            t = half * steps_per_vreg + tt
            nf = S5_CH * (S5_CHUNK - 1 - t)
            nb = S5_CH * t
            col_f = jnp.concatenate([kf[nf:], zero[:nf]], axis=0) if nf else kf
            col_b = jnp.concatenate([zero[:nb], kb[:S5_ROW - nb]], axis=0) if nb else kb
            diag = jnp.where((row2 // S5_CH == t) & (row2 % S5_CH == lane2), d_row, 0.0)
            col = col_f + col_b + diag
            r = pltpu.roll(col, S5_CH * tt, axis=1) if tt else col
            acc = jnp.where((lane2 >= S5_CH * tt) & (lane2 < S5_CH * (tt + 1)), r, acc)
        mt_ref[g, :, LANES * half:LANES * (half + 1)] = acc.astype(BF16)


def _s5_operators(s5_a_re, s5_a_im, s5_log_dt, s5_b_re, s5_b_im, s5_c_re, s5_c_im, s5_d):
    g, p, h = S5_GROUPS, S5_STATE, S5_CH
    cat = lambda x: jnp.concatenate([x[0], x[1]], axis=-1)
    a_re = cat(s5_a_re).reshape(g, 1, 2 * p)
    a_im = cat(s5_a_im).reshape(g, 1, 2 * p)
    ldt = cat(jnp.broadcast_to(s5_log_dt[:, :, None], (2, g, p))).reshape(g, 1, 2 * p)
    bt_re = cat(jnp.swapaxes(s5_b_re, -1, -2))
    bt_im = cat(jnp.swapaxes(s5_b_im, -1, -2))
    cpad = lambda c: jnp.pad(jnp.concatenate([c, c], axis=-1), ((0, 0), (0, LANES - h), (0, 0)))
    c_re, c_im = cpad(s5_c_re), cpad(s5_c_im)
    d_row = jnp.pad(s5_d.reshape(g, 1, h), ((0, 0), (0, 0), (0, LANES - h)))
    vec = pl.BlockSpec((S5OPS_GB, 1, 2 * p), lambda i: (i, 0, 0))
    hp = pl.BlockSpec((S5OPS_GB, h, 2 * p), lambda i: (i, 0, 0))
    sq = pl.BlockSpec((S5OPS_GB, LANES, 2 * p), lambda i: (i, 0, 0))
    big = pl.BlockSpec((S5OPS_GB, S5_ROW, 2 * p), lambda i: (i, 0, 0))
    mts = pl.BlockSpec((S5OPS_GB, S5_ROW, S5_ROW), lambda i: (i, 0, 0))
    big_sd = jax.ShapeDtypeStruct((g, S5_ROW, 2 * p), BF16)
    vec_sd = jax.ShapeDtypeStruct((g, 1, 2 * p), F32)
    return pl.pallas_call(
        _s5ops_kernel,
        grid=(g // S5OPS_GB,),
        in_specs=[vec, vec, vec, hp, hp, sq, sq, vec],
        out_specs=[mts, big, big, big, big, big, big, vec, vec],
        out_shape=[jax.ShapeDtypeStruct((g, S5_ROW, S5_ROW), BF16)] + [big_sd] * 6 + [vec_sd, vec_sd],
        scratch_shapes=[pltpu.VMEM((S5OPS_GB, S5_ROW, 2 * p), F32)] * 2,
        compiler_params=_cparams(("arbitrary",)),
        name="s5ops",
    )(a_re, a_im, ldt, bt_re, bt_im, c_re, c_im, d_row)


def _block_transpose(xs):
    n = len(xs)
    lane = lax.broadcasted_iota(jnp.int32, xs[0].shape, 1)
    xs = list(xs)
    d = n // 2
    while d:
        keep = ((lane // S5_CH) & d) == 0
        for i in range(n):
            if i & d:
                continue
            lo, hi = xs[i], xs[i + d]
            xs[i] = jnp.where(keep, lo, pltpu.roll(hi, S5_CH * d, axis=1))
            xs[i + d] = jnp.where(keep, pltpu.roll(lo, LANES - S5_CH * d, axis=1), hi)
        d //= 2
    return xs


def _s5_kernel(bsz, n_chunks, u_ref, mt_ref, erh_ref, erl_ref, eih_ref, eil_ref, gr_ref, gi_ref,
               atr_ref, ati_ref, h0r_ref, h0i_ref, y_ref, fr_ref, fi_ref,
               ua_ref, ub_ref, ya_ref, yb_ref, sr_ref, si_ref, xfr_ref, xfi_ref, xbr_ref, xbi_ref):
    nc = n_chunks
    spv = LANES // S5_CH
    rsub = min(nc, 32)

    def to_chunks(b, carry):
        for half, dst in ((0, ua_ref), (1, ub_ref)):
            for r0 in range(0, nc, rsub):
                xs = [u_ref[b, pl.ds(S5_CHUNK * r0 + half * spv + tt, rsub, stride=S5_CHUNK), :]
                      for tt in range(spv)]
                for k, blk in enumerate(_block_transpose(xs)):
                    dst[k, pl.ds(r0 * bsz + b, rsub, stride=bsz), :] = blk
        return carry

    lax.fori_loop(0, bsz, to_chunks, 0, unroll=2)

    lane = lax.broadcasted_iota(jnp.int32, (bsz, 2 * S5_STATE), 1)
    fwd = lane < S5_STATE
    lane_all = lax.broadcasted_iota(jnp.int32, (bsz * nc, 2 * S5_STATE), 1)
    fwd_all = lane_all < S5_STATE

    def group(k, slot):
        u = jnp.concatenate([ua_ref[k], ub_ref[k]], axis=1)
        uh, ul = _split(u)
        sr_ref[slot] = _dot(uh, erh_ref[k]) + _dot(ul, erh_ref[k]) + _dot(uh, erl_ref[k])
        si_ref[slot] = _dot(uh, eih_ref[k]) + _dot(ul, eih_ref[k]) + _dot(uh, eil_ref[k])
        at_re, at_im = atr_ref[k], ati_ref[k]
        y_intra = _dot(uh, mt_ref[k])

        def step(i, xc):
            x_re, x_im = xc
            rf = pl.ds(pl.multiple_of(i * bsz, bsz), bsz)
            rb = pl.ds(pl.multiple_of((nc - 1 - i) * bsz, bsz), bsz)
            xfr_ref[slot, rf, :] = x_re
            xfi_ref[slot, rf, :] = x_im
            xbr_ref[slot, rb, :] = x_re
            xbi_ref[slot, rb, :] = x_im
            s_re = jnp.where(fwd, sr_ref[slot, rf, :], sr_ref[slot, rb, :])
            s_im = jnp.where(fwd, si_ref[slot, rf, :], si_ref[slot, rb, :])
            return (at_re * x_re - at_im * x_im + s_re, at_re * x_im + at_im * x_re + s_im)

        x_re, x_im = lax.fori_loop(0, nc, step, (h0r_ref[k], h0i_ref[k]), unroll=True)
        fr_ref[k] = x_re
        fi_ref[k] = x_im
        xp_re = jnp.where(fwd_all, xfr_ref[slot], xbr_ref[slot]).astype(BF16)
        xp_im = jnp.where(fwd_all, xfi_ref[slot], xbi_ref[slot]).astype(BF16)
        y = y_intra + _dot_t(xp_re, gr_ref[k]) + _dot_t(xp_im, gi_ref[k])
        ya_ref[k] = y[:, :LANES]
        yb_ref[k] = y[:, LANES:]

    def group_pair(j, carry):
        group(2 * j, 0)
        group(2 * j + 1, 1)
        return carry

    lax.fori_loop(0, S5_GB // 2, group_pair, 0)

    def to_tokens(b, carry):
        for half, src in ((0, ya_ref), (1, yb_ref)):
            for r0 in range(0, nc, rsub):
                ys = [src[k, pl.ds(r0 * bsz + b, rsub, stride=bsz), :] for k in range(S5_GB)]
                for tt, blk in enumerate(_block_transpose(ys)):
                    y_ref[b, pl.ds(S5_CHUNK * r0 + half * spv + tt, rsub, stride=S5_CHUNK), :] = blk
        return carry

    lax.fori_loop(0, bsz, to_tokens, 0, unroll=2)


def _s5(u, ops, h0_re, h0_im):
    bsz, n_tok, _ = u.shape
    g, p = S5_GROUPS, S5_STATE
    nc = n_tok // S5_CHUNK
    rows = nc * bsz
    tok = pl.BlockSpec((bsz, n_tok, LANES), lambda j: (0, 0, j))
    gspec = lambda shape: pl.BlockSpec((S5_GB,) + shape, lambda j: (j, 0, 0))
    op = gspec((S5_ROW, 2 * p))
    return pl.pallas_call(
        functools.partial(_s5_kernel, bsz, nc),
        grid=(g // S5_GB,),
        in_specs=[tok, gspec((S5_ROW, S5_ROW)), op, op, op, op, op, op,
                  gspec((1, 2 * p)), gspec((1, 2 * p)), gspec((bsz, 2 * p)), gspec((bsz, 2 * p))],
        out_specs=[tok, gspec((bsz, 2 * p)), gspec((bsz, 2 * p))],
        out_shape=[jax.ShapeDtypeStruct((bsz, n_tok, D_S5), F32),
                   jax.ShapeDtypeStruct((g, bsz, 2 * p), F32),
                   jax.ShapeDtypeStruct((g, bsz, 2 * p), F32)],
        scratch_shapes=([pltpu.VMEM((S5_GB, rows, LANES), F32)] * 4
                        + [pltpu.VMEM((2, rows, 2 * p), F32)] * 6),
        compiler_params=_cparams(("arbitrary",)),
        name=f"s5_{n_tok}",
    )(u, *ops, h0_re, h0_im)


def _in_kernel(has_pos, *refs):
    if has_pos:
        x_ref, pos_ref, mod_ref, w_ref, hy_ref, s5_ref = refs
        x = x_ref[0] + pos_ref[...]
    else:
        x_ref, mod_ref, w_ref, hy_ref, s5_ref = refs
        x = x_ref[0]
    sh1 = mod_ref[0, :, 0:D_MODEL]
    sc1 = mod_ref[0, :, D_MODEL:2 * D_MODEL]
    h = _norm(x) * (1.0 + sc1) + sh1
    proj = _dot(h.astype(BF16), w_ref[...])
    hy_ref[0] = proj[:, :3 * D_HY]
    s5_ref[0] = proj[:, 3 * D_HY:]


def _in_proj(x3, pos, mod3, w_in_bf, tm):
    nb, lt, _ = x3.shape
    has_pos = pos is not None
    per_batch = mod3.shape[0] > 1
    midx = (lambda b, i: (b, 0, 0)) if per_batch else (lambda b, i: (0, 0, 0))
    in_specs = [pl.BlockSpec((1, tm, D_MODEL), lambda b, i: (b, i, 0))]
    args = [x3]
    if has_pos:
        in_specs.append(pl.BlockSpec((tm, D_MODEL), lambda b, i: (i, 0)))
        args.append(pos)
    in_specs += [pl.BlockSpec((1, 1, 6 * D_MODEL), midx),
                 pl.BlockSpec((D_MODEL, 3 * D_HY + D_S5), lambda b, i: (0, 0))]
    args += [mod3, w_in_bf]
    return pl.pallas_call(
        functools.partial(_in_kernel, has_pos),
        grid=(nb, lt // tm),
        in_specs=in_specs,
        out_specs=[pl.BlockSpec((1, tm, 3 * D_HY), lambda b, i: (b, i, 0)),
                   pl.BlockSpec((1, tm, D_S5), lambda b, i: (b, i, 0))],
        out_shape=[jax.ShapeDtypeStruct((nb, lt, 3 * D_HY), F32),
                   jax.ShapeDtypeStruct((nb, lt, D_S5), F32)],
        compiler_params=_cparams(("arbitrary", "arbitrary")),
        name=f"in_proj{nb}",
    )(*args)


def _route(logits):
    lane = lax.broadcasted_iota(jnp.int32, logits.shape, 1)
    lane_f = lane.astype(F32)
    neg = -jnp.inf
    big = float(LANES)
    m1 = (lane >= N_EXPERTS) & (lane < N_EXPERTS + N_EGROUPS)
    l1 = jnp.where(m1, logits, neg)
    top1 = jnp.max(l1, axis=-1, keepdims=True)
    grp = jnp.min(jnp.where(l1 == top1, lane_f, big), axis=-1, keepdims=True) - float(N_EXPERTS)
    den = jnp.sum(jnp.where(m1, jnp.exp(logits - top1), 0.0), axis=-1, keepdims=True)
    p_grp = 1.0 / den
    lo = grp * float(N_EPG)
    m2 = (lane_f >= lo) & (lane_f < lo + float(N_EPG))
    l2 = jnp.where(m2, logits, neg)
    v1 = jnp.max(l2, axis=-1, keepdims=True)
    i1 = jnp.min(jnp.where(l2 == v1, lane_f, big), axis=-1, keepdims=True)
    l2b = jnp.where(lane_f == i1, neg, l2)
    v2 = jnp.max(l2b, axis=-1, keepdims=True)
    i2 = jnp.min(jnp.where(l2b == v2, lane_f, big), axis=-1, keepdims=True)
    e = jnp.exp(v2 - v1)
    w1 = 1.0 / (1.0 + e)
    w2 = e / (1.0 + e)
    gates = jnp.where(lane_f == i1, w1 * p_grp, 0.0) + jnp.where(lane_f == i2, w2 * p_grp, 0.0)
    return jnp.where(lane_f == grp + float(N_EXPERTS), 1.0, gates)


def _out_kernel(n_ctx_blocks, xc_ref, xl_ref, pos_ref, yhyc_ref, yhyl_ref, ys5c_ref, ys5l_ref, mod_ref,
                wglu_ref, bglu_ref, ong_ref, wout_ref, ln1g_ref, ln1b_ref, wrh_ref, wrl_ref, br_ref,
                x1_ref, h2_ref, gate_ref, cnt_ref):
    is_ctx = pl.program_id(0) < n_ctx_blocks
    x = jnp.where(is_ctx, xc_ref[...], xl_ref[...] + pos_ref[...])
    y = jnp.where(is_ctx, ys5c_ref[...], ys5l_ref[...])
    y_hy = jnp.where(is_ctx, yhyc_ref[...], yhyl_ref[...])
    s5 = jax.nn.gelu(y) * jax.nn.sigmoid(_dot(y.astype(BF16), wglu_ref[...]) + bglu_ref[...])
    m_hy = _rms(y_hy) * ong_ref[:, 0:D_HY]
    m_s5 = _rms(s5) * ong_ref[:, D_HY:]
    o = (_dot(m_hy.astype(BF16), wout_ref[0:D_HY, :]) + _dot(m_s5.astype(BF16), wout_ref[D_HY:, :]))
    g1 = mod_ref[0, :, 2 * D_MODEL:3 * D_MODEL]
    sh2 = mod_ref[0, :, 3 * D_MODEL:4 * D_MODEL]
    sc2 = mod_ref[0, :, 4 * D_MODEL:5 * D_MODEL]
    x1 = _norm(ALPHA * x + g1 * o) * ln1g_ref[...] + ln1b_ref[...]
    x1_ref[...] = x1
    h2 = _norm(x1) * (1.0 + sc2) + sh2
    h2_ref[...] = h2.astype(BF16)
    hh, hl = _split(h2)
    logits = (_dot(hh, wrh_ref[...]) + _dot(hl, wrh_ref[...]) + _dot(hh, wrl_ref[...]) + br_ref[...])
    gates = _route(logits)
    gate_ref[...] = gates
    cnt_ref[0] = jnp.sum(gates, axis=0, keepdims=True)


def _out_proj(xc, xl, pos, yhy_c, yhy_l, ys5_c, ys5_l, mod, wglu_bf, bglu, ong, wout_bf, ln1g, ln1b,
              wr_hi, wr_lo, br, tm):
    n_ctx, n_lat = xc.shape[0], xl.shape[0]
    l_lat = pos.shape[0]
    ncb, nlb, npb = n_ctx // tm, n_lat // tm, l_lat // tm
    ctx = lambda w: pl.BlockSpec((tm, w), lambda i: (jnp.minimum(i, ncb - 1), 0))
    lat = lambda w: pl.BlockSpec((tm, w), lambda i: (jnp.maximum(i - ncb, 0), 0))
    full = lambda shape: pl.BlockSpec(shape, lambda i: (0,) * len(shape))
    out = lambda w: pl.BlockSpec((tm, w), lambda i: (i, 0))
    mod_idx = lambda i: (jnp.where(i < ncb, 0, 1 + jnp.maximum(i - ncb, 0) // npb), 0, 0)
    n_all = n_ctx + n_lat
    return pl.pallas_call(
        functools.partial(_out_kernel, ncb),
        grid=(ncb + nlb,),
        in_specs=[ctx(D_MODEL), lat(D_MODEL),
                  pl.BlockSpec((tm, D_MODEL), lambda i: (jnp.maximum(i - ncb, 0) % npb, 0)),
                  ctx(D_HY), lat(D_HY), ctx(D_S5), lat(D_S5),
                  pl.BlockSpec((1, 1, 6 * D_MODEL), mod_idx),
                  full((D_S5, D_S5)), full((1, D_S5)), full((1, D_MODEL)), full((D_MODEL, D_MODEL)),
                  full((1, D_MODEL)), full((1, D_MODEL)), full((D_MODEL, LANES)), full((D_MODEL, LANES)),
                  full((1, LANES))],
        out_specs=[out(D_MODEL), out(D_MODEL), out(LANES), pl.BlockSpec((1, 1, LANES), lambda i: (i, 0, 0))],
        out_shape=[jax.ShapeDtypeStruct((n_all, D_MODEL), F32),
                   jax.ShapeDtypeStruct((n_all, D_MODEL), BF16),
                   jax.ShapeDtypeStruct((n_all, LANES), F32),
                   jax.ShapeDtypeStruct((n_all // tm, 1, LANES), F32)],
        compiler_params=_cparams(("arbitrary",)),
        name="out_proj",
    )(xc, xl, pos, yhy_c, yhy_l, ys5_c, ys5_l, mod.reshape(mod.shape[0], 1, 6 * D_MODEL),
      wglu_bf, bglu, ong, wout_bf, ln1g, ln1b, wr_hi, wr_lo, br)


def _perm_t(gates, loc_ref, s):
    n = gates.shape[0]
    lane = lax.broadcasted_iota(jnp.int32, gates.shape, 1)
    oh = jnp.where((lane >= N_EXPERTS) & (lane < N_EXPERTS + N_EGROUPS), gates, 0.0)
    r = lax.broadcasted_iota(jnp.int32, (n, n), 0)
    c = lax.broadcasted_iota(jnp.int32, (n, n), 1)
    earlier = jnp.where(c < r, 1.0, 0.0).astype(BF16)
    cum = _dot(earlier, oh.astype(BF16))
    rank = jnp.sum(cum * oh, axis=-1, keepdims=True)
    lane1 = lax.broadcasted_iota(jnp.int32, (1, LANES), 1)
    locv = jnp.zeros((1, LANES), F32)
    for grp in range(N_EGROUPS):
        locv = jnp.where(lane1 == N_EXPERTS + grp, loc_ref[N_EGROUPS * s + grp].astype(F32), locv)
    dest = rank + jnp.sum(oh * locv, axis=-1, keepdims=True)
    slot = lax.broadcasted_iota(jnp.int32, (n, MOE_SLOTS), 1).astype(F32)
    return jnp.where(slot == dest, 1.0, 0.0)


def _segment_copies(s, loc_ref, len_ref, off_ref, make):
    for grp in range(N_EGROUPS):
        loc = loc_ref[N_EGROUPS * s + grp]
        off = off_ref[N_EGROUPS * s + grp]
        n_units = len_ref[N_EGROUPS * s + grp] // MOE_UNIT

        def body(i, carry):
            make(pl.multiple_of(loc + MOE_UNIT * i, MOE_UNIT), pl.multiple_of(off + MOE_UNIT * i, MOE_UNIT))
            return carry

        lax.fori_loop(0, n_units, body, 0)


def _moe_sort_kernel(loc_ref, len_ref, off_ref, h_ref, gate_ref, xs_in, gs_in, xs_hbm, gs_hbm,
                     xs_v, gs_v, sem):
    del xs_in, gs_in
    s = pl.program_id(0)
    slot = s % 2
    gates = gate_ref[...]
    p = _perm_t(gates, loc_ref, s).T.astype(BF16)
    xs_v[slot] = _dot(p, h_ref[...]).astype(BF16)
    g_hi = gates.astype(BF16)
    r1 = gates - g_hi.astype(F32)
    g_mid = r1.astype(BF16)
    g_lo = (r1 - g_mid.astype(F32)).astype(BF16)
    gs_v[slot] = _dot(p, g_hi) + _dot(p, g_mid) + _dot(p, g_lo)

    def copies(buf):
        def x_copy(lr, gr):
            return pltpu.make_async_copy(xs_v.at[buf, pl.ds(lr, MOE_UNIT), :],
                                         xs_hbm.at[pl.ds(gr, MOE_UNIT), :], sem.at[0, buf])

        def g_copy(lr, gr):
            return pltpu.make_async_copy(gs_v.at[buf, pl.ds(lr, MOE_UNIT), :],
                                         gs_hbm.at[pl.ds(gr, MOE_UNIT), :], sem.at[1, buf])

        def start(lr, gr):
            x_copy(lr, gr).start()
            g_copy(lr, gr).start()

        def wait(lr, gr):
            x_copy(lr, gr).wait()
            g_copy(lr, gr).wait()

        return start, wait

    _segment_copies(s, loc_ref, len_ref, off_ref, copies(slot)[0])

    @pl.when(s > 0)
    def _():
        _segment_copies(s - 1, loc_ref, len_ref, off_ref, copies(1 - slot)[1])

    @pl.when(s == pl.num_programs(0) - 1)
    def _():
        _segment_copies(s, loc_ref, len_ref, off_ref, copies(slot)[1])


def _moe_expert_kernel(bg_ref, nb_ref, xs_ref, gs_ref, wg_ref, wu_ref, wd_ref, o_ref):
    i = pl.program_id(0)

    @pl.when(i < nb_ref[0])
    def _():
        grp = bg_ref[i]
        x = xs_ref[...]
        gates = gs_ref[...]
        lane = lax.broadcasted_iota(jnp.int32, gates.shape, 1)
        acc = jnp.zeros(o_ref.shape, F32)
        for e in range(N_EPG):
            a = _dot(x, wg_ref[e].astype(BF16))
            u = _dot(x, wu_ref[e].astype(BF16))
            ge = jnp.sum(jnp.where(lane == N_EPG * grp + e, gates, 0.0), axis=-1, keepdims=True)
            hid = jax.nn.silu(a) * u * ge
            acc = acc + _dot(hid.astype(BF16), wd_ref[e].astype(BF16))
        o_ref[...] = acc.astype(BF16)

    @pl.when(i >= nb_ref[0])
    def _():
        o_ref[...] = jnp.zeros_like(o_ref)


def _moe_combine_kernel(loc_ref, len_ref, off_ref, gate_ref, x1_ref, mod_ref, ln2g_ref, ln2b_ref, o_hbm,
                        ctx_ref, lat_ref, o_v, sem, *, n_ctx_tiles):
    s = pl.program_id(0)
    slot = s % 2

    def copies(buf):
        def o_copy(lr, gr):
            return pltpu.make_async_copy(o_hbm.at[pl.ds(gr, MOE_UNIT), :],
                                         o_v.at[buf, pl.ds(lr, MOE_UNIT), :], sem.at[buf])

        return (lambda lr, gr: o_copy(lr, gr).start()), (lambda lr, gr: o_copy(lr, gr).wait())

    @pl.when(s == 0)
    def _():
        o_v[...] = jnp.zeros_like(o_v)
        _segment_copies(s, loc_ref, len_ref, off_ref, copies(slot)[0])

    @pl.when(s + 1 < pl.num_programs(0))
    def _():
        _segment_copies(s + 1, loc_ref, len_ref, off_ref, copies(1 - slot)[0])

    pt = _perm_t(gate_ref[...], loc_ref, s).astype(BF16)
    _segment_copies(s, loc_ref, len_ref, off_ref, copies(slot)[1])
    f = _dot(pt, o_v[slot])
    g2 = mod_ref[0, :, 5 * D_MODEL:6 * D_MODEL]
    x2 = _norm(ALPHA * x1_ref[...] + g2 * f) * ln2g_ref[...] + ln2b_ref[...]

    @pl.when(s < n_ctx_tiles)
    def _():
        ctx_ref[...] = x2

    @pl.when(s >= n_ctx_tiles)
    def _():
        lat_ref[...] = x2


def _moe_plan(tile_counts, n_blocks):
    cnt = tile_counts[:, 0, N_EXPERTS:N_EXPERTS + N_EGROUPS].astype(jnp.int32)
    len16 = ((cnt + MOE_UNIT - 1) // MOE_UNIT) * MOE_UNIT
    loc = jnp.cumsum(len16, axis=1) - len16
    rows_g = jnp.sum(len16, axis=0)
    reg_g = ((rows_g + MOE_TM - 1) // MOE_TM) * MOE_TM
    reg_start = jnp.cumsum(reg_g) - reg_g
    off = reg_start[None, :] + jnp.cumsum(len16, axis=0) - len16
    blk_end = jnp.cumsum(reg_g // MOE_TM)
    bi = jnp.arange(n_blocks, dtype=jnp.int32)
    blk_group = jnp.minimum(jnp.sum((bi[:, None] >= blk_end[None, :]).astype(jnp.int32), axis=1),
                            N_EGROUPS - 1)
    flat = lambda a: a.reshape(-1).astype(jnp.int32)
    return flat(loc), flat(len16), flat(off), blk_group.astype(jnp.int32), blk_end[-1:].astype(jnp.int32)


def _moe(h2_all, gates_all, tile_counts, x1_all, mod, w_gate, w_up, w_down, ln2g, ln2b, n_ctx,
         tokens_per_mod_row):
    n_tok = h2_all.shape[0]
    n_tiles = n_tok // MOE_ST
    n_ctx_tiles = n_ctx // MOE_ST
    max_rows = n_tok + n_tiles * N_EGROUPS * (MOE_UNIT - 1) + N_EGROUPS * (MOE_TM - 1)
    n_blocks = -(-max_rows // MOE_TM)
    n_rows = n_blocks * MOE_TM
    loc, len16, off, blk_group, n_used = _moe_plan(tile_counts, n_blocks)

    tile = lambda w: pl.BlockSpec((MOE_ST, w), lambda s, *_: (s, 0))
    anyspec = pl.BlockSpec(memory_space=pl.ANY)
    xs, gs = pl.pallas_call(
        _moe_sort_kernel,
        grid_spec=pltpu.PrefetchScalarGridSpec(
            num_scalar_prefetch=3, grid=(n_tiles,),
            in_specs=[tile(D_MODEL), tile(LANES), anyspec, anyspec],
            out_specs=[anyspec, anyspec],
            scratch_shapes=[pltpu.VMEM((2, MOE_SLOTS, D_MODEL), BF16), pltpu.VMEM((2, MOE_SLOTS, LANES), F32),
                            pltpu.SemaphoreType.DMA((2, 2))]),
        out_shape=[jax.ShapeDtypeStruct((n_rows, D_MODEL), BF16),
                   jax.ShapeDtypeStruct((n_rows, LANES), F32)],
        input_output_aliases={5: 0, 6: 1},
        compiler_params=_cparams(("arbitrary",)),
        name="moe_sort",
    )(loc, len16, off, h2_all, gates_all, jnp.zeros((n_rows, D_MODEL), BF16), jnp.zeros((n_rows, LANES), F32))

    blk = lambda w: pl.BlockSpec((MOE_TM, w), lambda i, bg, nb: (jnp.minimum(i, nb[0] - 1), 0))
    wspec = lambda a, b, mode: pl.BlockSpec((N_EPG, a, b), lambda i, bg, nb: (bg[i], 0, 0),
                                            pipeline_mode=mode)
    o_sorted = pl.pallas_call(
        _moe_expert_kernel,
        grid_spec=pltpu.PrefetchScalarGridSpec(
            num_scalar_prefetch=2, grid=(n_blocks,),
            in_specs=[blk(D_MODEL), blk(LANES), wspec(D_MODEL, D_EXPERT, pl.Buffered(1)),
                      wspec(D_MODEL, D_EXPERT, None), wspec(D_EXPERT, D_MODEL, None)],
            out_specs=pl.BlockSpec((MOE_TM, D_MODEL), lambda i, bg, nb: (i, 0))),
        out_shape=jax.ShapeDtypeStruct((n_rows, D_MODEL), BF16),
        compiler_params=_cparams(("arbitrary",)),
        name="moe_experts",
    )(blk_group, n_used, xs, gs, w_gate, w_up, w_down)

    lat_per_row = tokens_per_mod_row // MOE_ST

    def mod_idx(s, *_):
        return (jnp.where(s < n_ctx_tiles, 0, 1 + (s - n_ctx_tiles) // lat_per_row), 0, 0)

    vec = pl.BlockSpec((1, D_MODEL), lambda s, *_: (0, 0))
    return pl.pallas_call(
        functools.partial(_moe_combine_kernel, n_ctx_tiles=n_ctx_tiles),
        grid_spec=pltpu.PrefetchScalarGridSpec(
            num_scalar_prefetch=3, grid=(n_tiles,),
            in_specs=[tile(LANES), tile(D_MODEL), pl.BlockSpec((1, 1, 6 * D_MODEL), mod_idx), vec, vec,
                      anyspec],
            out_specs=[pl.BlockSpec((MOE_ST, D_MODEL), lambda s, *_: (jnp.minimum(s, n_ctx_tiles - 1), 0)),
                       pl.BlockSpec((MOE_ST, D_MODEL), lambda s, *_: (jnp.maximum(s - n_ctx_tiles, 0), 0))],
            scratch_shapes=[pltpu.VMEM((2, MOE_SLOTS, D_MODEL), BF16), pltpu.SemaphoreType.DMA((2,))]),
        out_shape=[jax.ShapeDtypeStruct((n_ctx, D_MODEL), F32),
                   jax.ShapeDtypeStruct((n_tok - n_ctx, D_MODEL), F32)],
        compiler_params=_cparams(("arbitrary",)),
        name="moe_combine",
    )(loc, len16, off, gates_all, x1_all, mod.reshape(mod.shape[0], 1, 6 * D_MODEL), ln2g, ln2b, o_sorted)


def _grid_pos_embed(n_tokens):
    rows = n_tokens // GRID_W
    row = np.repeat(np.arange(rows, dtype=np.float64), GRID_W)
    col = np.tile(np.arange(GRID_W, dtype=np.float64), rows)
    quarter = D_MODEL // 4
    omega = 1.0 / (POS_BASE ** (np.arange(quarter, dtype=np.float64) / quarter))
    er = row[:, None] * omega
    ec = col[:, None] * omega
    return jnp.asarray(np.concatenate([np.sin(er), np.cos(er), np.sin(ec), np.cos(ec)], axis=-1), F32)


def _mixers(x, pos, mod3, h0_re, h0_im, tabs, filt, s5ops, wts, tm):
    bsz, n_tok, _ = x.shape
    shared = mod3.shape[0] == 1
    x3 = x.reshape(1, bsz * n_tok, D_MODEL) if shared else x
    proj_hy, u_s5 = _in_proj(x3, pos, mod3, wts['w_in'], tm)
    y_hy = _hyena(proj_hy.reshape(bsz, n_tok, 3 * D_HY), tabs, filt,
                  wts['hy_conv_w'], wts['hy_conv_b'], wts['hy_fbias'])
    y_s5, f_re, f_im = _s5(u_s5.reshape(bsz, n_tok, D_S5), s5ops, h0_re, h0_im)
    return y_hy.reshape(bsz * n_tok, D_HY), y_s5.reshape(bsz * n_tok, D_S5), f_re, f_im


def kernel(x_prompt, x_sample, state_s5_re, state_s5_im, c, c_ctx, w_ada, b_ada, w_in, hy_conv_w, hy_conv_b, hy_f_w1, hy_f_b1, hy_f_w2, hy_f_b2, hy_f_w3, hy_freq, hy_fbias, s5_a_re, s5_a_im, s5_log_dt, s5_b_re, s5_b_im, s5_c_re, s5_c_im, s5_d, s5_w_glu, s5_b_glu, out_norm_g, w_out, ln1_g, ln1_b, moe_w_r1, moe_b_r1, moe_w_r2, moe_b_r2, moe_w_gate, moe_w_up, moe_w_down, ln2_g, ln2_b):
    b_ctx, l_ctx, _ = x_prompt.shape
    b_lat, l_lat, _ = x_sample.shape
    g, p = S5_GROUPS, S5_STATE
    assert w_ada.shape[0] == 1, "single-layer trunk"
    l = 0

    nrow = 16
    cond = jnp.concatenate([c_ctx[None, :], c, jnp.zeros((nrow - 1 - b_lat, D_MODEL), F32)], axis=0)
    mod = _ada(cond, w_ada[l], b_ada[l])
    mod_ctx = mod[0:1].reshape(1, 1, 6 * D_MODEL)
    mod_lat = mod[1:1 + b_lat].reshape(b_lat, 1, 6 * D_MODEL)

    wr = jnp.concatenate([moe_w_r2[l].transpose(1, 0, 2).reshape(D_MODEL, N_EXPERTS), moe_w_r1[l]], axis=1)
    wr = jnp.pad(wr, ((0, 0), (0, LANES - wr.shape[1])))
    br = jnp.concatenate([moe_b_r2[l].reshape(-1), moe_b_r1[l]])
    br = jnp.pad(br, (0, LANES - br.shape[0])).reshape(1, LANES)
    wr_hi, wr_lo = _split(wr)

    wts = {
        'w_in': w_in[l].astype(BF16), 'hy_conv_w': hy_conv_w[l], 'hy_conv_b': hy_conv_b[l],
        'hy_fbias': hy_fbias[l], 'w_glu': s5_w_glu[l].astype(BF16), 'b_glu': s5_b_glu[l].reshape(1, -1),
        'out_norm_g': out_norm_g[l].reshape(1, -1), 'w_out': w_out[l].astype(BF16),
        'ln1_g': ln1_g[l].reshape(1, -1), 'ln1_b': ln1_b[l].reshape(1, -1),
        'wr_hi': wr_hi, 'wr_lo': wr_lo, 'br': br,
        'w_gate': moe_w_gate[l], 'w_up': moe_w_up[l], 'w_down': moe_w_down[l],
        'ln2_g': ln2_g[l].reshape(1, -1), 'ln2_b': ln2_b[l].reshape(1, -1),
    }

    s5ops = _s5_operators(s5_a_re[l], s5_a_im[l], s5_log_dt[l], s5_b_re[l], s5_b_im[l],
                          s5_c_re[l], s5_c_im[l], s5_d[l])
    tabs_ctx = _tables(l_ctx)
    tabs_lat = _tables(l_lat)
    filt_args = (hy_f_w1[l], hy_f_b1[l], hy_f_w2[l], hy_f_b2[l], hy_f_w3[l], hy_freq[l])
    filt_ctx = _hyena_filters(l_ctx, tabs_ctx, *filt_args)
    filt_lat = _hyena_filters(l_lat, tabs_lat, *filt_args)

    zero = jnp.zeros((g, b_ctx, 2 * p), F32)
    yhy_c, ys5_c, f_re, f_im = _mixers(x_prompt, None, mod_ctx, zero, zero, tabs_ctx, filt_ctx, s5ops, wts, 512)
    unpack = lambda f: f.reshape(g, b_ctx, 2, p).transpose(1, 2, 0, 3)[:, None]
    new_re, new_im = unpack(f_re), unpack(f_im)

    pack = lambda s: s[:, l].transpose(2, 0, 1, 3).reshape(g, b_lat, 2 * p)
    pos = _grid_pos_embed(l_lat)
    yhy_l, ys5_l, _, _ = _mixers(x_sample, pos, mod_lat, pack(state_s5_re), pack(state_s5_im),
                                 tabs_lat, filt_lat, s5ops, wts, 512)

    n_ctx = b_ctx * l_ctx
    x1_all, h2_all, gates_all, tile_counts = _out_proj(
        x_prompt.reshape(n_ctx, D_MODEL), x_sample.reshape(b_lat * l_lat, D_MODEL), pos,
        yhy_c, yhy_l, ys5_c, ys5_l, mod, wts['w_glu'], wts['b_glu'], wts['out_norm_g'], wts['w_out'],
        wts['ln1_g'], wts['ln1_b'], wts['wr_hi'], wts['wr_lo'], wts['br'], MOE_ST)
    y_ctx, y_lat = _moe(h2_all, gates_all, tile_counts, x1_all, mod,
                        wts['w_gate'], wts['w_up'], wts['w_down'], wts['ln2_g'], wts['ln2_b'],
                        n_ctx, l_lat)
    return (y_ctx.reshape(x_prompt.shape), y_lat.reshape(x_sample.shape), new_re, new_im)
```

```python
import functools
import math

import numpy as np
import jax
import jax.numpy as jnp
from jax import lax
from jax.experimental import pallas as pl
from jax.experimental.pallas import tpu as pltpu

F32 = jnp.float32
BF16 = jnp.bfloat16

D_MODEL = 1024
DEPTH = 1
GRID_W = 64
POS_BASE = 10000.0
D_HY = 512
D_S5 = 512
S5_CH = 16
S5_GROUPS = 32
S5_STATE = 64
S5_CHUNK = 16
S5_ROW = S5_CHUNK * S5_CH
HY_BANDS = 16
HY_EMB = 1 + 2 * HY_BANDS
HY_HID = 64
HY_MIN_DECAY = math.log(1e-2) / 1.5
HY_MAX_DECAY = math.log(1e-2) / 0.3
N_EGROUPS = 4
N_EPG = 4
N_EXPERTS = 16
D_EXPERT = 512
LN_EPS = 1e-5
ALPHA = (2.0 * DEPTH) ** 0.25
LANES = 128
S5_GB = LANES // S5_CH
S5OPS_GB = 4
HY_CW = 512
MOE_ST = 512
MOE_SLOTS = 640
MOE_UNIT = 16
MOE_TM = 512
VMEM_LIMIT = 60000 * 1024


def _cparams(sem):
    return pltpu.CompilerParams(dimension_semantics=sem, vmem_limit_bytes=VMEM_LIMIT)


def _split(x):
    hi = x.astype(BF16)
    lo = (x - hi.astype(F32)).astype(BF16)
    return hi, lo


def _dot(a, b):
    return jnp.dot(a, b, preferred_element_type=F32)


def _dot_t(a, b):
    return lax.dot_general(a, b, (((1,), (1,)), ((), ())), preferred_element_type=F32)


def _mm3(a, b):
    ah, al = _split(a)
    bh, bl = _split(b)
    return _dot(ah, bh) + _dot(al, bh) + _dot(ah, bl)


def _mm3_t(a, b):
    ah, al = _split(a)
    bh, bl = _split(b)
    return _dot_t(ah, bh) + _dot_t(al, bh) + _dot_t(ah, bl)


def _norm(x):
    xc = x - jnp.mean(x, axis=-1, keepdims=True)
    return xc * lax.rsqrt(jnp.mean(xc * xc, axis=-1, keepdims=True) + LN_EPS)


def _rms(y):
    return y * lax.rsqrt(jnp.mean(y * y, axis=-1, keepdims=True) + LN_EPS)


def _ada_kernel(cond_ref, w_ref, b_ref, o_ref):
    c = jax.nn.silu(cond_ref[...])
    o_ref[...] = _mm3(c, w_ref[...]) + b_ref[...]


def _ada(cond, w_ada, b_ada):
    nb = cond.shape[0]
    n = w_ada.shape[1]
    tn = 1024
    return pl.pallas_call(
        _ada_kernel,
        grid=(n // tn,),
        in_specs=[pl.BlockSpec((nb, D_MODEL), lambda j: (0, 0)),
                  pl.BlockSpec((D_MODEL, tn), lambda j: (0, j)),
                  pl.BlockSpec((1, tn), lambda j: (0, j))],
        out_specs=pl.BlockSpec((nb, tn), lambda j: (0, j)),
        out_shape=jax.ShapeDtypeStruct((nb, n), F32),
        compiler_params=_cparams(("arbitrary",)),
        name="ada",
    )(cond, w_ada, b_ada.reshape(1, n))


def _dft_tables(n_half):
    n = 2 * n_half
    idx = np.arange(n_half, dtype=np.int64)
    m = (idx[:, None] * idx[None, :]) % n
    ang = 2.0 * np.pi * m.astype(np.float64) / n
    cm = np.cos(ang)
    sm = -np.sin(ang)
    sm[0, :] = 1.0 - 2.0 * (idx % 2)
    return cm.astype(np.float32), sm.astype(np.float32)


def _tables(n_tok):
    n_half = n_tok // 2
    cm, sm = _dft_tables(n_half)
    mats = tuple(jnp.asarray(t).astype(BF16) for t in (cm, sm, np.ascontiguousarray(sm.T)))
    ang = np.pi * np.arange(n_half, dtype=np.float64) / n_tok
    tw = [np.broadcast_to(v[:, None], (n_half, HY_CW)).astype(np.float32) for v in (np.cos(ang), -np.sin(ang))]
    return mats + (jnp.asarray(tw[0]), jnp.asarray(tw[1]))


def _put_cols(ref, x):
    for j in range(ref.shape[0]):
        ref[j] = x[:, LANES * j:LANES * (j + 1)]


def _get_cols(ref):
    return jnp.concatenate([ref[j] for j in range(ref.shape[0])], axis=1)


def _get_parity(ref, parity):
    n_half = ref.shape[1] // 2
    return jnp.concatenate([ref[j, pl.ds(parity, n_half, stride=2), :] for j in range(ref.shape[0])], axis=1)


def _put_parity(ref, parity, x):
    n_half = ref.shape[1] // 2
    for j in range(ref.shape[0]):
        ref[j, pl.ds(parity, n_half, stride=2), :] = x[:, LANES * j:LANES * (j + 1)]


def _set_row0(x, v):
    first = lax.broadcasted_iota(jnp.int32, (8, x.shape[1]), 0) == 0
    return jnp.concatenate([jnp.where(first, v, x[:8]), x[8:]], axis=0)


def _rfft_packed(x_ref, cm, sm, tw_re, tw_im):
    xe = _get_parity(x_ref, 0).astype(BF16)
    xo = _get_parity(x_ref, 1).astype(BF16)
    e_re, e_im = _dot(cm, xe), _dot(sm, xe)
    o_re, o_im = _dot(cm, xo), _dot(sm, xo)
    t_re = tw_re * o_re - tw_im * o_im
    t_im = tw_re * o_im + tw_im * o_re
    a_im = _set_row0(e_im + t_im, e_im[0:1])
    b_im = _set_row0(t_im - e_im, -o_im[0:1])
    return e_re + t_re, a_im, e_re - t_re, b_im


def _irfft_packed(y_ref, ya_re, ya_im, yb_re, yb_im, cm, st, tw_re, tw_im):
    p_e = ya_re + yb_re
    q_e = _set_row0(ya_im - yb_im, ya_im[0:1])
    _put_parity(y_ref, 0, _dot(cm, p_e.astype(BF16)) + _dot(st, q_e.astype(BF16)))
    ra_re = ya_re * tw_re + ya_im * tw_im
    ra_im = ya_im * tw_re - ya_re * tw_im
    rb_re = yb_im * tw_im - yb_re * tw_re
    rb_im = -(yb_re * tw_im + yb_im * tw_re)
    p_o = ra_re + rb_re
    q_o = _set_row0(ra_im - rb_im, -yb_im[0:1])
    _put_parity(y_ref, 1, _dot(cm, p_o.astype(BF16)) + _dot(st, q_o.astype(BF16)))


def _filt_kernel(n_tok, z_ref, t_ref, w1_ref, b1_ref, w2_ref, b2_ref, fr_ref, w3f_ref, w3b_ref,
                 dl_ref, cm_ref, sm_ref, twr_ref, twi_ref, kar_ref, kai_ref, kbr_ref, kbi_ref, p_ref, q_ref):
    fr = fr_ref[...]
    h = jnp.sin(fr * (_mm3(z_ref[...], w1_ref[...]) + b1_ref[...]))
    h = jnp.sin(fr * (_mm3(h, w2_ref[...]) + b2_ref[...]))
    decay = jnp.exp(-t_ref[...] * dl_ref[...])
    row = lax.broadcasted_iota(jnp.int32, decay.shape, 0)
    hf = _mm3(h, w3f_ref[...]) * decay
    hb = jnp.where(row == 0, 0.0, _mm3(h, w3b_ref[...]) * decay)
    _put_cols(p_ref, hf + hb)
    _put_cols(q_ref, hf - hb)
    cm, sm, tw_re, tw_im = cm_ref[...], sm_ref[...], twr_ref[...], twi_ref[...]
    row0 = lax.broadcasted_iota(jnp.int32, tw_re.shape, 0) == 0
    pa_re, pa_im, pb_re, _ = _rfft_packed(p_ref, cm, sm, tw_re, tw_im)
    _, qa_im, _, qb_im = _rfft_packed(q_ref, cm, sm, tw_re, tw_im)
    inv_n = 1.0 / (2 * n_tok)
    w_re = jnp.where(row0, inv_n, 2.0 * inv_n)
    kar_ref[...] = w_re * pa_re
    kbr_ref[...] = w_re * pb_re
    kai_ref[...] = (2.0 * inv_n) * _set_row0(qa_im, pa_im[0:1])
    kbi_ref[...] = (2.0 * inv_n) * qb_im


def _hyena_filters(n_tok, tabs, hy_f_w1, hy_f_b1, hy_f_w2, hy_f_b2, hy_f_w3, hy_freq):
    cm, sm, _, tw_re, tw_im = tabs
    n_half = n_tok // 2
    t = jnp.linspace(0.0, 1.0, n_tok, dtype=F32)[:, None]
    wv = 2.0 * math.pi * jnp.arange(n_tok, dtype=F32) / n_tok
    fb = jnp.linspace(1e-4, HY_BANDS - 1, HY_BANDS, dtype=F32)
    ang = wv[:, None] * fb[None, :]
    z = jnp.concatenate([t, jnp.cos(ang), -jnp.sin(ang)], axis=-1)
    z = jnp.pad(z, ((0, 0), (0, LANES - HY_EMB)))
    w1 = jnp.pad(hy_f_w1, ((0, LANES - HY_EMB), (0, 0)))
    deltas = jnp.abs(jnp.linspace(HY_MIN_DECAY, HY_MAX_DECAY, D_HY, dtype=F32))[None, :]
    ncb = D_HY // HY_CW
    full = lambda j: (0, 0)
    out_sd = jax.ShapeDtypeStruct((n_half, 2 * D_HY), F32)
    mat = pl.BlockSpec((n_half, n_half), full, pipeline_mode=pl.Buffered(1))
    twb = pl.BlockSpec((n_half, HY_CW), full, pipeline_mode=pl.Buffered(1))
    return pl.pallas_call(
        functools.partial(_filt_kernel, n_tok),
        grid=(2 * ncb,),
        in_specs=[pl.BlockSpec((n_tok, LANES), full),
                  pl.BlockSpec((n_tok, 1), full),
                  pl.BlockSpec((LANES, HY_HID), full),
                  pl.BlockSpec((1, HY_HID), full),
                  pl.BlockSpec((HY_HID, HY_HID), full),
                  pl.BlockSpec((1, HY_HID), full),
                  pl.BlockSpec((1, HY_HID), full),
                  pl.BlockSpec((HY_HID, HY_CW), lambda j: (0, 2 * ncb * (j // ncb) + j % ncb)),
                  pl.BlockSpec((HY_HID, HY_CW), lambda j: (0, 2 * ncb * (j // ncb) + ncb + j % ncb)),
                  pl.BlockSpec((1, HY_CW), lambda j: (0, j % ncb)),
                  mat, mat, twb, twb],
        out_specs=[pl.BlockSpec((n_half, HY_CW), lambda j: (0, j))] * 4,
        out_shape=[out_sd] * 4,
        scratch_shapes=[pltpu.VMEM((HY_CW // LANES, n_tok, LANES), F32)] * 2,
        compiler_params=_cparams(("arbitrary",)),
        name=f"filt{n_tok}",
    )(z, t, w1, hy_f_b1.reshape(1, -1), hy_f_w2, hy_f_b2.reshape(1, -1), hy_freq.reshape(1, -1),
      hy_f_w3, hy_f_w3, deltas, cm, sm, tw_re, tw_im)


def _hyena_kernel(pv_ref, p1_ref, p2_ref, cwv_ref, cw1_ref, cw2_ref, cbv_ref, cb1_ref, cb2_ref,
                  fbias_ref, cm_ref, sm_ref, st_ref, twr_ref, twi_ref,
                  kar0_ref, kai0_ref, kbr0_ref, kbi0_ref, kar1_ref, kai1_ref, kbr1_ref, kbi1_ref,
                  o_ref, u_ref, y_ref):
    n_tok = pv_ref.shape[1]
    row = lax.broadcasted_iota(jnp.int32, (n_tok, pv_ref.shape[2]), 0)

    def short_conv(p_ref, cw_ref, cb_ref):
        p = p_ref[0]
        prev = jnp.where(row == 0, 0.0, pltpu.roll(p, 1, axis=0))
        nxt = jnp.where(row == n_tok - 1, 0.0, pltpu.roll(p, n_tok - 1, axis=0))
        return cb_ref[...] + prev * cw_ref[0:1, :] + p * cw_ref[1:2, :] + nxt * cw_ref[2:3, :]

    cm, sm, st, tw_re, tw_im = cm_ref[...], sm_ref[...], st_ref[...], twr_ref[...], twi_ref[...]

    def fftconv(u, kar_ref, kai_ref, kbr_ref, kbi_ref, skip):
        _put_cols(u_ref, u)
        ua_re, ua_im, ub_re, ub_im = _rfft_packed(u_ref, cm, sm, tw_re, tw_im)
        ka_re, ka_im, kb_re, kb_im = kar_ref[...], kai_ref[...], kbr_ref[...], kbi_ref[...]
        zero_row = jnp.zeros_like(ka_im[0:1])
        kaz = _set_row0(ka_im, zero_row)
        kbz = _set_row0(kb_im, zero_row)
        ya_re = ua_re * ka_re - ua_im * kaz
        yb_re = ub_re * kb_re - ub_im * kbz
        h_re = ua_im[0:1] * ka_im[0:1] - ub_im[0:1] * kb_im[0:1]
        h_im = ua_im[0:1] * kb_im[0:1] + ub_im[0:1] * ka_im[0:1]
        ya_im = _set_row0(ua_re * ka_im + ua_im * ka_re, h_re)
        yb_im = _set_row0(ub_re * kb_im + ub_im * kb_re, h_im)
        _irfft_packed(y_ref, ya_re, ya_im, yb_re, yb_im, cm, st, tw_re, tw_im)
        return _get_cols(y_ref) + u * skip

    v = short_conv(pv_ref, cwv_ref, cbv_ref)
    x1 = short_conv(p1_ref, cw1_ref, cb1_ref)
    z = x1 * fftconv(v, kar0_ref, kai0_ref, kbr0_ref, kbi0_ref, fbias_ref[0:1, :])
    x2 = short_conv(p2_ref, cw2_ref, cb2_ref)
    o_ref[0] = x2 * fftconv(z, kar1_ref, kai1_ref, kbr1_ref, kbi1_ref, fbias_ref[1:2, :])


def _hyena(proj_hy, tabs, filt, hy_conv_w, hy_conv_b, hy_fbias):
    bsz, n_tok, _ = proj_hy.shape
    n_half = n_tok // 2
    ncb = D_HY // HY_CW
    cm, sm, st, tw_re, tw_im = tabs
    cb = hy_conv_b.reshape(1, -1)
    const = lambda shape: pl.BlockSpec(shape, lambda b, c: (0, 0), pipeline_mode=pl.Buffered(1))
    mat = const((n_half, n_half))
    twb = const((n_half, HY_CW))

    def pspec(k):
        return pl.BlockSpec((1, n_tok, HY_CW), lambda b, c: (b, 0, k * ncb + c))

    def cwspec(k):
        return pl.BlockSpec((3, HY_CW), lambda b, c: (0, k * ncb + c))

    def cbspec(k):
        return pl.BlockSpec((1, HY_CW), lambda b, c: (0, k * ncb + c))

    def fspec(o):
        mode = pl.Buffered(1) if ncb == 1 else None
        return pl.BlockSpec((n_half, HY_CW), lambda b, c: (0, o * ncb + c), pipeline_mode=mode)

    return pl.pallas_call(
        _hyena_kernel,
        grid=(bsz, ncb),
        in_specs=[pspec(0), pspec(1), pspec(2), cwspec(0), cwspec(1), cwspec(2),
                  cbspec(0), cbspec(1), cbspec(2),
                  pl.BlockSpec((2, HY_CW), lambda b, c: (0, c)),
                  mat, mat, mat, twb, twb] + [fspec(0)] * 4 + [fspec(1)] * 4,
        out_specs=pl.BlockSpec((1, n_tok, HY_CW), lambda b, c: (b, 0, c)),
        out_shape=jax.ShapeDtypeStruct((bsz, n_tok, D_HY), F32),
        scratch_shapes=[pltpu.VMEM((HY_CW // LANES, n_tok, LANES), F32)] * 2,
        compiler_params=_cparams(("arbitrary", "arbitrary")),
        name=f"hyena{n_tok}",
    )(proj_hy, proj_hy, proj_hy, hy_conv_w, hy_conv_w, hy_conv_w, cb, cb, cb, hy_fbias,
      cm, sm, st, tw_re, tw_im, *filt, *filt)


def _s5ops_kernel(*refs):
    for g in range(S5OPS_GB):
        _s5ops_group(g, *refs)


def _s5ops_group(g, are_ref, aim_ref, ldt_ref, btr_ref, bti_ref, cre_ref, cim_ref, d_ref,
                 mt_ref, erh_ref, erl_ref, eih_ref, eil_ref, gr_ref, gi_ref, atr_ref, ati_ref,
                 er_ref, ei_ref):
    a_re, a_im = are_ref[g], aim_ref[g]
    dt = jnp.exp(ldt_ref[g])
    mag = jnp.exp(a_re * dt)
    ab_re = mag * jnp.cos(a_im * dt)
    ab_im = mag * jnp.sin(a_im * dt)
    n_re, n_im = ab_re - 1.0, ab_im
    den = a_re * a_re + a_im * a_im
    q_re = (n_re * a_re + n_im * a_im) / den
    q_im = (n_im * a_re - n_re * a_im) / den
    bt_re, bt_im = btr_ref[g], bti_ref[g]
    bb_re = q_re * bt_re - q_im * bt_im
    bb_im = q_re * bt_im + q_im * bt_re
    c_re, c_im = cre_ref[g, 0:S5_CH, :], cim_ref[g, 0:S5_CH, :]
    pw = [(jnp.ones_like(ab_re), jnp.zeros_like(ab_re))]
    for _ in range(S5_CHUNK):
        pr, pi = pw[-1]
        pw.append((pr * ab_re - pi * ab_im, pr * ab_im + pi * ab_re))
    lane = lax.broadcasted_iota(jnp.int32, ab_re.shape, 1)
    fwd = lane < S5_STATE
    for s in range(S5_CHUNK):
        e_re = jnp.where(fwd, pw[S5_CHUNK - 1 - s][0], pw[s][0])
        e_im = jnp.where(fwd, pw[S5_CHUNK - 1 - s][1], pw[s][1])
        er_ref[g, pl.ds(S5_CH * s, S5_CH), :] = e_re * bb_re - e_im * bb_im
        ei_ref[g, pl.ds(S5_CH * s, S5_CH), :] = e_re * bb_im + e_im * bb_re
        g_re = jnp.where(fwd, pw[s + 1][0], pw[S5_CHUNK - s][0])
        g_im = jnp.where(fwd, pw[s + 1][1], pw[S5_CHUNK - s][1])
        gr_ref[g, pl.ds(S5_CH * s, S5_CH), :] = (c_re * g_re - c_im * g_im).astype(BF16)
        gi_ref[g, pl.ds(S5_CH * s, S5_CH), :] = (-(c_re * g_im + c_im * g_re)).astype(BF16)
    atr_ref[g] = pw[S5_CHUNK][0]
    ati_ref[g] = pw[S5_CHUNK][1]
    er, ei = er_ref[g], ei_ref[g]
    erh_ref[g], erl_ref[g] = _split(er)
    eih_ref[g], eil_ref[g] = _split(ei)
    lane2 = lax.broadcasted_iota(jnp.int32, er.shape, 1)
    row2 = lax.broadcasted_iota(jnp.int32, er.shape, 0)
    f2 = lane2 < S5_STATE
    zero = jnp.zeros_like(er)

    cp_re, cp_im = cre_ref[g], cim_ref[g]
    kf = _mm3_t(jnp.where(f2, er, zero), cp_re) - _mm3_t(jnp.where(f2, ei, zero), cp_im)
    kb = _mm3_t(jnp.where(f2, zero, er), cp_re) - _mm3_t(jnp.where(f2, zero, ei), cp_im)
    d_row = d_ref[g]
    steps_per_vreg = LANES // S5_CH
    for half in range(S5_CHUNK // steps_per_vreg):
        acc = zero
        for tt in range(steps_per_vreg):
            t = half * steps_per_vreg + tt
            nf = S5_CH * (S5_CHUNK - 1 - t)
            nb = S5_CH * t
            col_f = jnp.concatenate([kf[nf:], zero[:nf]], axis=0) if nf else kf
            col_b = jnp.concatenate([zero[:nb], kb[:S5_ROW - nb]], axis=0) if nb else kb
            diag = jnp.where((row2 // S5_CH == t) & (row2 % S5_CH == lane2), d_row, 0.0)
            col = col_f + col_b + diag
            r = pltpu.roll(col, S5_CH * tt, axis=1) if tt else col
            acc = jnp.where((lane2 >= S5_CH * tt) & (lane2 < S5_CH * (tt + 1)), r, acc)
        mt_ref[g, :, LANES * half:LANES * (half + 1)] = acc.astype(BF16)


def _s5_operators(s5_a_re, s5_a_im, s5_log_dt, s5_b_re, s5_b_im, s5_c_re, s5_c_im, s5_d):
    g, p, h = S5_GROUPS, S5_STATE, S5_CH
    cat = lambda x: jnp.concatenate([x[0], x[1]], axis=-1)
    a_re = cat(s5_a_re).reshape(g, 1, 2 * p)
    a_im = cat(s5_a_im).reshape(g, 1, 2 * p)
    ldt = cat(jnp.broadcast_to(s5_log_dt[:, :, None], (2, g, p))).reshape(g, 1, 2 * p)
    bt_re = cat(jnp.swapaxes(s5_b_re, -1, -2))
    bt_im = cat(jnp.swapaxes(s5_b_im, -1, -2))
    cpad = lambda c: jnp.pad(jnp.concatenate([c, c], axis=-1), ((0, 0), (0, LANES - h), (0, 0)))
    c_re, c_im = cpad(s5_c_re), cpad(s5_c_im)
    d_row = jnp.pad(s5_d.reshape(g, 1, h), ((0, 0), (0, 0), (0, LANES - h)))
    vec = pl.BlockSpec((S5OPS_GB, 1, 2 * p), lambda i: (i, 0, 0))
    hp = pl.BlockSpec((S5OPS_GB, h, 2 * p), lambda i: (i, 0, 0))
    sq = pl.BlockSpec((S5OPS_GB, LANES, 2 * p), lambda i: (i, 0, 0))
    big = pl.BlockSpec((S5OPS_GB, S5_ROW, 2 * p), lambda i: (i, 0, 0))
    mts = pl.BlockSpec((S5OPS_GB, S5_ROW, S5_ROW), lambda i: (i, 0, 0))
    big_sd = jax.ShapeDtypeStruct((g, S5_ROW, 2 * p), BF16)
    vec_sd = jax.ShapeDtypeStruct((g, 1, 2 * p), F32)
    return pl.pallas_call(
        _s5ops_kernel,
        grid=(g // S5OPS_GB,),
        in_specs=[vec, vec, vec, hp, hp, sq, sq, vec],
        out_specs=[mts, big, big, big, big, big, big, vec, vec],
        out_shape=[jax.ShapeDtypeStruct((g, S5_ROW, S5_ROW), BF16)] + [big_sd] * 6 + [vec_sd, vec_sd],
        scratch_shapes=[pltpu.VMEM((S5OPS_GB, S5_ROW, 2 * p), F32)] * 2,
        compiler_params=_cparams(("arbitrary",)),
        name="s5ops",
    )(a_re, a_im, ldt, bt_re, bt_im, c_re, c_im, d_row)


def _block_transpose(xs):
    n = len(xs)
    lane = lax.broadcasted_iota(jnp.int32, xs[0].shape, 1)
    xs = list(xs)
    d = n // 2
    while d:
        keep = ((lane // S5_CH) & d) == 0
        for i in range(n):
            if i & d:
                continue
            lo, hi = xs[i], xs[i + d]
            xs[i] = jnp.where(keep, lo, pltpu.roll(hi, S5_CH * d, axis=1))
            xs[i + d] = jnp.where(keep, pltpu.roll(lo, LANES - S5_CH * d, axis=1), hi)
        d //= 2
    return xs


def _s5_kernel(bsz, n_chunks, u_ref, mt_ref, erh_ref, erl_ref, eih_ref, eil_ref, gr_ref, gi_ref,
               atr_ref, ati_ref, h0r_ref, h0i_ref, y_ref, fr_ref, fi_ref,
               ua_ref, ub_ref, ya_ref, yb_ref, sr_ref, si_ref, xfr_ref, xfi_ref, xbr_ref, xbi_ref):
    nc = n_chunks
    spv = LANES // S5_CH
    rsub = min(nc, 32)

    def to_chunks(b, carry):
        for half, dst in ((0, ua_ref), (1, ub_ref)):
            for r0 in range(0, nc, rsub):
                xs = [u_ref[b, pl.ds(S5_CHUNK * r0 + half * spv + tt, rsub, stride=S5_CHUNK), :]
                      for tt in range(spv)]
                for k, blk in enumerate(_block_transpose(xs)):
                    dst[k, pl.ds(r0 * bsz + b, rsub, stride=bsz), :] = blk
        return carry

    lax.fori_loop(0, bsz, to_chunks, 0, unroll=2)

    lane = lax.broadcasted_iota(jnp.int32, (bsz, 2 * S5_STATE), 1)
    fwd = lane < S5_STATE
    lane_all = lax.broadcasted_iota(jnp.int32, (bsz * nc, 2 * S5_STATE), 1)
    fwd_all = lane_all < S5_STATE

    def group(k, slot):
        u = jnp.concatenate([ua_ref[k], ub_ref[k]], axis=1)
        uh, ul = _split(u)
        sr_ref[slot] = _dot(uh, erh_ref[k]) + _dot(ul, erh_ref[k]) + _dot(uh, erl_ref[k])
        si_ref[slot] = _dot(uh, eih_ref[k]) + _dot(ul, eih_ref[k]) + _dot(uh, eil_ref[k])
        at_re, at_im = atr_ref[k], ati_ref[k]
        y_intra = _dot(uh, mt_ref[k])

        def step(i, xc):
            x_re, x_im = xc
            rf = pl.ds(pl.multiple_of(i * bsz, bsz), bsz)
            rb = pl.ds(pl.multiple_of((nc - 1 - i) * bsz, bsz), bsz)
            xfr_ref[slot, rf, :] = x_re
            xfi_ref[slot, rf, :] = x_im
            xbr_ref[slot, rb, :] = x_re
            xbi_ref[slot, rb, :] = x_im
            s_re = jnp.where(fwd, sr_ref[slot, rf, :], sr_ref[slot, rb, :])
            s_im = jnp.where(fwd, si_ref[slot, rf, :], si_ref[slot, rb, :])
            return (at_re * x_re - at_im * x_im + s_re, at_re * x_im + at_im * x_re + s_im)

        x_re, x_im = lax.fori_loop(0, nc, step, (h0r_ref[k], h0i_ref[k]), unroll=True)
        fr_ref[k] = x_re
        fi_ref[k] = x_im
        xp_re = jnp.where(fwd_all, xfr_ref[slot], xbr_ref[slot]).astype(BF16)
        xp_im = jnp.where(fwd_all, xfi_ref[slot], xbi_ref[slot]).astype(BF16)
        y = y_intra + _dot_t(xp_re, gr_ref[k]) + _dot_t(xp_im, gi_ref[k])
        ya_ref[k] = y[:, :LANES]
        yb_ref[k] = y[:, LANES:]

    def group_pair(j, carry):
        group(2 * j, 0)
        group(2 * j + 1, 1)
        return carry

    lax.fori_loop(0, S5_GB // 2, group_pair, 0)

    def to_tokens(b, carry):
        for half, src in ((0, ya_ref), (1, yb_ref)):
            for r0 in range(0, nc, rsub):
                ys = [src[k, pl.ds(r0 * bsz + b, rsub, stride=bsz), :] for k in range(S5_GB)]
                for tt, blk in enumerate(_block_transpose(ys)):
                    y_ref[b, pl.ds(S5_CHUNK * r0 + half * spv + tt, rsub, stride=S5_CHUNK), :] = blk
        return carry

    lax.fori_loop(0, bsz, to_tokens, 0, unroll=2)


def _s5(u, ops, h0_re, h0_im):
    bsz, n_tok, _ = u.shape
    g, p = S5_GROUPS, S5_STATE
    nc = n_tok // S5_CHUNK
    rows = nc * bsz
    tok = pl.BlockSpec((bsz, n_tok, LANES), lambda j: (0, 0, j))
    gspec = lambda shape: pl.BlockSpec((S5_GB,) + shape, lambda j: (j, 0, 0))
    op = gspec((S5_ROW, 2 * p))
    return pl.pallas_call(
        functools.partial(_s5_kernel, bsz, nc),
        grid=(g // S5_GB,),
        in_specs=[tok, gspec((S5_ROW, S5_ROW)), op, op, op, op, op, op,
                  gspec((1, 2 * p)), gspec((1, 2 * p)), gspec((bsz, 2 * p)), gspec((bsz, 2 * p))],
        out_specs=[tok, gspec((bsz, 2 * p)), gspec((bsz, 2 * p))],
        out_shape=[jax.ShapeDtypeStruct((bsz, n_tok, D_S5), F32),
                   jax.ShapeDtypeStruct((g, bsz, 2 * p), F32),
                   jax.ShapeDtypeStruct((g, bsz, 2 * p), F32)],
        scratch_shapes=([pltpu.VMEM((S5_GB, rows, LANES), F32)] * 4
                        + [pltpu.VMEM((2, rows, 2 * p), F32)] * 6),
        compiler_params=_cparams(("arbitrary",)),
        name=f"s5_{n_tok}",
    )(u, *ops, h0_re, h0_im)


def _in_kernel(has_pos, *refs):
    if has_pos:
        x_ref, pos_ref, mod_ref, w_ref, hy_ref, s5_ref = refs
        x = x_ref[0] + pos_ref[...]
    else:
        x_ref, mod_ref, w_ref, hy_ref, s5_ref = refs
        x = x_ref[0]
    sh1 = mod_ref[0, :, 0:D_MODEL]
    sc1 = mod_ref[0, :, D_MODEL:2 * D_MODEL]
    h = _norm(x) * (1.0 + sc1) + sh1
    proj = _dot(h.astype(BF16), w_ref[...])
    hy_ref[0] = proj[:, :3 * D_HY]
    s5_ref[0] = proj[:, 3 * D_HY:]


def _in_proj(x3, pos, mod3, w_in_bf, tm):
    nb, lt, _ = x3.shape
    has_pos = pos is not None
    per_batch = mod3.shape[0] > 1
    midx = (lambda b, i: (b, 0, 0)) if per_batch else (lambda b, i: (0, 0, 0))
    in_specs = [pl.BlockSpec((1, tm, D_MODEL), lambda b, i: (b, i, 0))]
    args = [x3]
    if has_pos:
        in_specs.append(pl.BlockSpec((tm, D_MODEL), lambda b, i: (i, 0)))
        args.append(pos)
    in_specs += [pl.BlockSpec((1, 1, 6 * D_MODEL), midx),
                 pl.BlockSpec((D_MODEL, 3 * D_HY + D_S5), lambda b, i: (0, 0))]
    args += [mod3, w_in_bf]
    return pl.pallas_call(
        functools.partial(_in_kernel, has_pos),
        grid=(nb, lt // tm),
        in_specs=in_specs,
        out_specs=[pl.BlockSpec((1, tm, 3 * D_HY), lambda b, i: (b, i, 0)),
                   pl.BlockSpec((1, tm, D_S5), lambda b, i: (b, i, 0))],
        out_shape=[jax.ShapeDtypeStruct((nb, lt, 3 * D_HY), F32),
                   jax.ShapeDtypeStruct((nb, lt, D_S5), F32)],
        compiler_params=_cparams(("arbitrary", "arbitrary")),
        name=f"in_proj{nb}",
    )(*args)


def _route(logits):
    lane = lax.broadcasted_iota(jnp.int32, logits.shape, 1)
    lane_f = lane.astype(F32)
    neg = -jnp.inf
    big = float(LANES)
    m1 = (lane >= N_EXPERTS) & (lane < N_EXPERTS + N_EGROUPS)
    l1 = jnp.where(m1, logits, neg)
    top1 = jnp.max(l1, axis=-1, keepdims=True)
    grp = jnp.min(jnp.where(l1 == top1, lane_f, big), axis=-1, keepdims=True) - float(N_EXPERTS)
    den = jnp.sum(jnp.where(m1, jnp.exp(logits - top1), 0.0), axis=-1, keepdims=True)
    p_grp = 1.0 / den
    lo = grp * float(N_EPG)
    m2 = (lane_f >= lo) & (lane_f < lo + float(N_EPG))
    l2 = jnp.where(m2, logits, neg)
    v1 = jnp.max(l2, axis=-1, keepdims=True)
    i1 = jnp.min(jnp.where(l2 == v1, lane_f, big), axis=-1, keepdims=True)
    l2b = jnp.where(lane_f == i1, neg, l2)
    v2 = jnp.max(l2b, axis=-1, keepdims=True)
    i2 = jnp.min(jnp.where(l2b == v2, lane_f, big), axis=-1, keepdims=True)
    e = jnp.exp(v2 - v1)
    w1 = 1.0 / (1.0 + e)
    w2 = e / (1.0 + e)
    gates = jnp.where(lane_f == i1, w1 * p_grp, 0.0) + jnp.where(lane_f == i2, w2 * p_grp, 0.0)
    return jnp.where(lane_f == grp + float(N_EXPERTS), 1.0, gates)


def _out_kernel(n_ctx_blocks, xc_ref, xl_ref, pos_ref, yhyc_ref, yhyl_ref, ys5c_ref, ys5l_ref, mod_ref,
                wglu_ref, bglu_ref, ong_ref, wout_ref, ln1g_ref, ln1b_ref, wrh_ref, wrl_ref, br_ref,
                x1_ref, h2_ref, gate_ref, cnt_ref):
    is_ctx = pl.program_id(0) < n_ctx_blocks
    x = jnp.where(is_ctx, xc_ref[...], xl_ref[...] + pos_ref[...])
    y = jnp.where(is_ctx, ys5c_ref[...], ys5l_ref[...])
    y_hy = jnp.where(is_ctx, yhyc_ref[...], yhyl_ref[...])
    s5 = jax.nn.gelu(y) * jax.nn.sigmoid(_dot(y.astype(BF16), wglu_ref[...]) + bglu_ref[...])
    m_hy = _rms(y_hy) * ong_ref[:, 0:D_HY]
    m_s5 = _rms(s5) * ong_ref[:, D_HY:]
    o = (_dot(m_hy.astype(BF16), wout_ref[0:D_HY, :]) + _dot(m_s5.astype(BF16), wout_ref[D_HY:, :]))
    g1 = mod_ref[0, :, 2 * D_MODEL:3 * D_MODEL]
    sh2 = mod_ref[0, :, 3 * D_MODEL:4 * D_MODEL]
    sc2 = mod_ref[0, :, 4 * D_MODEL:5 * D_MODEL]
    x1 = _norm(ALPHA * x + g1 * o) * ln1g_ref[...] + ln1b_ref[...]
    x1_ref[...] = x1
    h2 = _norm(x1) * (1.0 + sc2) + sh2
    h2_ref[...] = h2.astype(BF16)
    hh, hl = _split(h2)
    logits = (_dot(hh, wrh_ref[...]) + _dot(hl, wrh_ref[...]) + _dot(hh, wrl_ref[...]) + br_ref[...])
    gates = _route(logits)
    gate_ref[...] = gates
    cnt_ref[0] = jnp.sum(gates, axis=0, keepdims=True)


def _out_proj(xc, xl, pos, yhy_c, yhy_l, ys5_c, ys5_l, mod, wglu_bf, bglu, ong, wout_bf, ln1g, ln1b,
              wr_hi, wr_lo, br, tm):
    n_ctx, n_lat = xc.shape[0], xl.shape[0]
    l_lat = pos.shape[0]
    ncb, nlb, npb = n_ctx // tm, n_lat // tm, l_lat // tm
    ctx = lambda w: pl.BlockSpec((tm, w), lambda i: (jnp.minimum(i, ncb - 1), 0))
    lat = lambda w: pl.BlockSpec((tm, w), lambda i: (jnp.maximum(i - ncb, 0), 0))
    full = lambda shape: pl.BlockSpec(shape, lambda i: (0,) * len(shape))
    out = lambda w: pl.BlockSpec((tm, w), lambda i: (i, 0))
    mod_idx = lambda i: (jnp.where(i < ncb, 0, 1 + jnp.maximum(i - ncb, 0) // npb), 0, 0)
    n_all = n_ctx + n_lat
    return pl.pallas_call(
        functools.partial(_out_kernel, ncb),
        grid=(ncb + nlb,),
        in_specs=[ctx(D_MODEL), lat(D_MODEL),
                  pl.BlockSpec((tm, D_MODEL), lambda i: (jnp.maximum(i - ncb, 0) % npb, 0)),
                  ctx(D_HY), lat(D_HY), ctx(D_S5), lat(D_S5),
                  pl.BlockSpec((1, 1, 6 * D_MODEL), mod_idx),
                  full((D_S5, D_S5)), full((1, D_S5)), full((1, D_MODEL)), full((D_MODEL, D_MODEL)),
                  full((1, D_MODEL)), full((1, D_MODEL)), full((D_MODEL, LANES)), full((D_MODEL, LANES)),
                  full((1, LANES))],
        out_specs=[out(D_MODEL), out(D_MODEL), out(LANES), pl.BlockSpec((1, 1, LANES), lambda i: (i, 0, 0))],
        out_shape=[jax.ShapeDtypeStruct((n_all, D_MODEL), F32),
                   jax.ShapeDtypeStruct((n_all, D_MODEL), BF16),
                   jax.ShapeDtypeStruct((n_all, LANES), F32),
                   jax.ShapeDtypeStruct((n_all // tm, 1, LANES), F32)],
        compiler_params=_cparams(("arbitrary",)),
        name="out_proj",
    )(xc, xl, pos, yhy_c, yhy_l, ys5_c, ys5_l, mod.reshape(mod.shape[0], 1, 6 * D_MODEL),
      wglu_bf, bglu, ong, wout_bf, ln1g, ln1b, wr_hi, wr_lo, br)


def _perm_t(gates, loc_ref, s):
    n = gates.shape[0]
    lane = lax.broadcasted_iota(jnp.int32, gates.shape, 1)
    oh = jnp.where((lane >= N_EXPERTS) & (lane < N_EXPERTS + N_EGROUPS), gates, 0.0)
    r = lax.broadcasted_iota(jnp.int32, (n, n), 0)
    c = lax.broadcasted_iota(jnp.int32, (n, n), 1)
    earlier = jnp.where(c < r, 1.0, 0.0).astype(BF16)
    cum = _dot(earlier, oh.astype(BF16))
    rank = jnp.sum(cum * oh, axis=-1, keepdims=True)
    lane1 = lax.broadcasted_iota(jnp.int32, (1, LANES), 1)
    locv = jnp.zeros((1, LANES), F32)
    for grp in range(N_EGROUPS):
        locv = jnp.where(lane1 == N_EXPERTS + grp, loc_ref[N_EGROUPS * s + grp].astype(F32), locv)
    dest = rank + jnp.sum(oh * locv, axis=-1, keepdims=True)
    slot = lax.broadcasted_iota(jnp.int32, (n, MOE_SLOTS), 1).astype(F32)
    return jnp.where(slot == dest, 1.0, 0.0)


def _segment_copies(s, loc_ref, len_ref, off_ref, make):
    for grp in range(N_EGROUPS):
        loc = loc_ref[N_EGROUPS * s + grp]
        off = off_ref[N_EGROUPS * s + grp]
        n_units = len_ref[N_EGROUPS * s + grp] // MOE_UNIT

        def body(i, carry):
            make(pl.multiple_of(loc + MOE_UNIT * i, MOE_UNIT), pl.multiple_of(off + MOE_UNIT * i, MOE_UNIT))
            return carry

        lax.fori_loop(0, n_units, body, 0)


def _moe_sort_kernel(loc_ref, len_ref, off_ref, h_ref, gate_ref, xs_in, gs_in, xs_hbm, gs_hbm,
                     xs_v, gs_v, sem):
    del xs_in, gs_in
    s = pl.program_id(0)
    slot = s % 2
    gates = gate_ref[...]
    p = _perm_t(gates, loc_ref, s).T.astype(BF16)
    xs_v[slot] = _dot(p, h_ref[...]).astype(BF16)
    g_hi = gates.astype(BF16)
    r1 = gates - g_hi.astype(F32)
    g_mid = r1.astype(BF16)
    g_lo = (r1 - g_mid.astype(F32)).astype(BF16)
    gs_v[slot] = _dot(p, g_hi) + _dot(p, g_mid) + _dot(p, g_lo)

    def copies(buf):
        def x_copy(lr, gr):
            return pltpu.make_async_copy(xs_v.at[buf, pl.ds(lr, MOE_UNIT), :],
                                         xs_hbm.at[pl.ds(gr, MOE_UNIT), :], sem.at[0, buf])

        def g_copy(lr, gr):
            return pltpu.make_async_copy(gs_v.at[buf, pl.ds(lr, MOE_UNIT), :],
                                         gs_hbm.at[pl.ds(gr, MOE_UNIT), :], sem.at[1, buf])

        def start(lr, gr):
            x_copy(lr, gr).start()
            g_copy(lr, gr).start()

        def wait(lr, gr):
            x_copy(lr, gr).wait()
            g_copy(lr, gr).wait()

        return start, wait

    _segment_copies(s, loc_ref, len_ref, off_ref, copies(slot)[0])

    @pl.when(s > 0)
    def _():
        _segment_copies(s - 1, loc_ref, len_ref, off_ref, copies(1 - slot)[1])

    @pl.when(s == pl.num_programs(0) - 1)
    def _():
        _segment_copies(s, loc_ref, len_ref, off_ref, copies(slot)[1])


def _moe_expert_kernel(bg_ref, nb_ref, xs_ref, gs_ref, wg_ref, wu_ref, wd_ref, o_ref):
    i = pl.program_id(0)

    @pl.when(i < nb_ref[0])
    def _():
        grp = bg_ref[i]
        x = xs_ref[...]
        gates = gs_ref[...]
        lane = lax.broadcasted_iota(jnp.int32, gates.shape, 1)
        acc = jnp.zeros(o_ref.shape, F32)
        for e in range(N_EPG):
            a = _dot(x, wg_ref[e].astype(BF16))
            u = _dot(x, wu_ref[e].astype(BF16))
            ge = jnp.sum(jnp.where(lane == N_EPG * grp + e, gates, 0.0), axis=-1, keepdims=True)
            hid = jax.nn.silu(a) * u * ge
            acc = acc + _dot(hid.astype(BF16), wd_ref[e].astype(BF16))
        o_ref[...] = acc.astype(BF16)

    @pl.when(i >= nb_ref[0])
    def _():
        o_ref[...] = jnp.zeros_like(o_ref)


def _moe_combine_kernel(loc_ref, len_ref, off_ref, gate_ref, x1_ref, mod_ref, ln2g_ref, ln2b_ref, o_hbm,
                        ctx_ref, lat_ref, o_v, sem, *, n_ctx_tiles):
    s = pl.program_id(0)
    slot = s % 2

    def copies(buf):
        def o_copy(lr, gr):
            return pltpu.make_async_copy(o_hbm.at[pl.ds(gr, MOE_UNIT), :],
                                         o_v.at[buf, pl.ds(lr, MOE_UNIT), :], sem.at[buf])

        return (lambda lr, gr: o_copy(lr, gr).start()), (lambda lr, gr: o_copy(lr, gr).wait())

    @pl.when(s == 0)
    def _():
        o_v[...] = jnp.zeros_like(o_v)
        _segment_copies(s, loc_ref, len_ref, off_ref, copies(slot)[0])

    @pl.when(s + 1 < pl.num_programs(0))
    def _():
        _segment_copies(s + 1, loc_ref, len_ref, off_ref, copies(1 - slot)[0])

    pt = _perm_t(gate_ref[...], loc_ref, s).astype(BF16)
    _segment_copies(s, loc_ref, len_ref, off_ref, copies(slot)[1])
    f = _dot(pt, o_v[slot])
    g2 = mod_ref[0, :, 5 * D_MODEL:6 * D_MODEL]
    x2 = _norm(ALPHA * x1_ref[...] + g2 * f) * ln2g_ref[...] + ln2b_ref[...]

    @pl.when(s < n_ctx_tiles)
    def _():
        ctx_ref[...] = x2

    @pl.when(s >= n_ctx_tiles)
    def _():
        lat_ref[...] = x2


def _moe_plan(tile_counts, n_blocks):
    cnt = tile_counts[:, 0, N_EXPERTS:N_EXPERTS + N_EGROUPS].astype(jnp.int32)
    len16 = ((cnt + MOE_UNIT - 1) // MOE_UNIT) * MOE_UNIT
    loc = jnp.cumsum(len16, axis=1) - len16
    rows_g = jnp.sum(len16, axis=0)
    reg_g = ((rows_g + MOE_TM - 1) // MOE_TM) * MOE_TM
    reg_start = jnp.cumsum(reg_g) - reg_g
    off = reg_start[None, :] + jnp.cumsum(len16, axis=0) - len16
    blk_end = jnp.cumsum(reg_g // MOE_TM)
    bi = jnp.arange(n_blocks, dtype=jnp.int32)
    blk_group = jnp.minimum(jnp.sum((bi[:, None] >= blk_end[None, :]).astype(jnp.int32), axis=1),
                            N_EGROUPS - 1)
    flat = lambda a: a.reshape(-1).astype(jnp.int32)
    return flat(loc), flat(len16), flat(off), blk_group.astype(jnp.int32), blk_end[-1:].astype(jnp.int32)


def _moe(h2_all, gates_all, tile_counts, x1_all, mod, w_gate, w_up, w_down, ln2g, ln2b, n_ctx,
         tokens_per_mod_row):
    n_tok = h2_all.shape[0]
    n_tiles = n_tok // MOE_ST
    n_ctx_tiles = n_ctx // MOE_ST
    max_rows = n_tok + n_tiles * N_EGROUPS * (MOE_UNIT - 1) + N_EGROUPS * (MOE_TM - 1)
    n_blocks = -(-max_rows // MOE_TM)
    n_rows = n_blocks * MOE_TM
    loc, len16, off, blk_group, n_used = _moe_plan(tile_counts, n_blocks)

    tile = lambda w: pl.BlockSpec((MOE_ST, w), lambda s, *_: (s, 0))
    anyspec = pl.BlockSpec(memory_space=pl.ANY)
    xs, gs = pl.pallas_call(
        _moe_sort_kernel,
        grid_spec=pltpu.PrefetchScalarGridSpec(
            num_scalar_prefetch=3, grid=(n_tiles,),
            in_specs=[tile(D_MODEL), tile(LANES), anyspec, anyspec],
            out_specs=[anyspec, anyspec],
            scratch_shapes=[pltpu.VMEM((2, MOE_SLOTS, D_MODEL), BF16), pltpu.VMEM((2, MOE_SLOTS, LANES), F32),
                            pltpu.SemaphoreType.DMA((2, 2))]),
        out_shape=[jax.ShapeDtypeStruct((n_rows, D_MODEL), BF16),
                   jax.ShapeDtypeStruct((n_rows, LANES), F32)],
        input_output_aliases={5: 0, 6: 1},
        compiler_params=_cparams(("arbitrary",)),
        name="moe_sort",
    )(loc, len16, off, h2_all, gates_all, jnp.zeros((n_rows, D_MODEL), BF16), jnp.zeros((n_rows, LANES), F32))

    blk = lambda w: pl.BlockSpec((MOE_TM, w), lambda i, bg, nb: (jnp.minimum(i, nb[0] - 1), 0))
    wspec = lambda a, b, mode: pl.BlockSpec((N_EPG, a, b), lambda i, bg, nb: (bg[i], 0, 0),
                                            pipeline_mode=mode)
    o_sorted = pl.pallas_call(
        _moe_expert_kernel,
        grid_spec=pltpu.PrefetchScalarGridSpec(
            num_scalar_prefetch=2, grid=(n_blocks,),
            in_specs=[blk(D_MODEL), blk(LANES), wspec(D_MODEL, D_EXPERT, None),
                      wspec(D_MODEL, D_EXPERT, None), wspec(D_EXPERT, D_MODEL, None)],
            out_specs=pl.BlockSpec((MOE_TM, D_MODEL), lambda i, bg, nb: (i, 0))),
        out_shape=jax.ShapeDtypeStruct((n_rows, D_MODEL), BF16),
        compiler_params=_cparams(("arbitrary",)),
        name="moe_experts",
    )(blk_group, n_used, xs, gs, w_gate, w_up, w_down)

    lat_per_row = tokens_per_mod_row // MOE_ST

    def mod_idx(s, *_):
        return (jnp.where(s < n_ctx_tiles, 0, 1 + (s - n_ctx_tiles) // lat_per_row), 0, 0)

    vec = pl.BlockSpec((1, D_MODEL), lambda s, *_: (0, 0))
    return pl.pallas_call(
        functools.partial(_moe_combine_kernel, n_ctx_tiles=n_ctx_tiles),
        grid_spec=pltpu.PrefetchScalarGridSpec(
            num_scalar_prefetch=3, grid=(n_tiles,),
            in_specs=[tile(LANES), tile(D_MODEL), pl.BlockSpec((1, 1, 6 * D_MODEL), mod_idx), vec, vec,
                      anyspec],
            out_specs=[pl.BlockSpec((MOE_ST, D_MODEL), lambda s, *_: (jnp.minimum(s, n_ctx_tiles - 1), 0)),
                       pl.BlockSpec((MOE_ST, D_MODEL), lambda s, *_: (jnp.maximum(s - n_ctx_tiles, 0), 0))],
            scratch_shapes=[pltpu.VMEM((2, MOE_SLOTS, D_MODEL), BF16), pltpu.SemaphoreType.DMA((2,))]),
        out_shape=[jax.ShapeDtypeStruct((n_ctx, D_MODEL), F32),
                   jax.ShapeDtypeStruct((n_tok - n_ctx, D_MODEL), F32)],
        compiler_params=_cparams(("arbitrary",)),
        name="moe_combine",
    )(loc, len16, off, gates_all, x1_all, mod.reshape(mod.shape[0], 1, 6 * D_MODEL), ln2g, ln2b, o_sorted)


def _grid_pos_embed(n_tokens):
    rows = n_tokens // GRID_W
    row = np.repeat(np.arange(rows, dtype=np.float64), GRID_W)
    col = np.tile(np.arange(GRID_W, dtype=np.float64), rows)
    quarter = D_MODEL // 4
    omega = 1.0 / (POS_BASE ** (np.arange(quarter, dtype=np.float64) / quarter))
    er = row[:, None] * omega
    ec = col[:, None] * omega
    return jnp.asarray(np.concatenate([np.sin(er), np.cos(er), np.sin(ec), np.cos(ec)], axis=-1), F32)


def _mixers(x, pos, mod3, h0_re, h0_im, tabs, filt, s5ops, wts, tm):
    bsz, n_tok, _ = x.shape
    shared = mod3.shape[0] == 1
    x3 = x.reshape(1, bsz * n_tok, D_MODEL) if shared else x
    proj_hy, u_s5 = _in_proj(x3, pos, mod3, wts['w_in'], tm)
    y_hy = _hyena(proj_hy.reshape(bsz, n_tok, 3 * D_HY), tabs, filt,
                  wts['hy_conv_w'], wts['hy_conv_b'], wts['hy_fbias'])
    y_s5, f_re, f_im = _s5(u_s5.reshape(bsz, n_tok, D_S5), s5ops, h0_re, h0_im)
    return y_hy.reshape(bsz * n_tok, D_HY), y_s5.reshape(bsz * n_tok, D_S5), f_re, f_im


def kernel(x_prompt, x_sample, state_s5_re, state_s5_im, c, c_ctx, w_ada, b_ada, w_in, hy_conv_w, hy_conv_b, hy_f_w1, hy_f_b1, hy_f_w2, hy_f_b2, hy_f_w3, hy_freq, hy_fbias, s5_a_re, s5_a_im, s5_log_dt, s5_b_re, s5_b_im, s5_c_re, s5_c_im, s5_d, s5_w_glu, s5_b_glu, out_norm_g, w_out, ln1_g, ln1_b, moe_w_r1, moe_b_r1, moe_w_r2, moe_b_r2, moe_w_gate, moe_w_up, moe_w_down, ln2_g, ln2_b):
    b_ctx, l_ctx, _ = x_prompt.shape
    b_lat, l_lat, _ = x_sample.shape
    g, p = S5_GROUPS, S5_STATE
    assert w_ada.shape[0] == 1, "single-layer trunk"
    l = 0

    nrow = 16
    cond = jnp.concatenate([c_ctx[None, :], c, jnp.zeros((nrow - 1 - b_lat, D_MODEL), F32)], axis=0)
    mod = _ada(cond, w_ada[l], b_ada[l])
    mod_ctx = mod[0:1].reshape(1, 1, 6 * D_MODEL)
    mod_lat = mod[1:1 + b_lat].reshape(b_lat, 1, 6 * D_MODEL)

    wr = jnp.concatenate([moe_w_r2[l].transpose(1, 0, 2).reshape(D_MODEL, N_EXPERTS), moe_w_r1[l]], axis=1)
    wr = jnp.pad(wr, ((0, 0), (0, LANES - wr.shape[1])))
    br = jnp.concatenate([moe_b_r2[l].reshape(-1), moe_b_r1[l]])
    br = jnp.pad(br, (0, LANES - br.shape[0])).reshape(1, LANES)
    wr_hi, wr_lo = _split(wr)

    wts = {
        'w_in': w_in[l].astype(BF16), 'hy_conv_w': hy_conv_w[l], 'hy_conv_b': hy_conv_b[l],
        'hy_fbias': hy_fbias[l], 'w_glu': s5_w_glu[l].astype(BF16), 'b_glu': s5_b_glu[l].reshape(1, -1),
        'out_norm_g': out_norm_g[l].reshape(1, -1), 'w_out': w_out[l].astype(BF16),
        'ln1_g': ln1_g[l].reshape(1, -1), 'ln1_b': ln1_b[l].reshape(1, -1),
        'wr_hi': wr_hi, 'wr_lo': wr_lo, 'br': br,
        'w_gate': moe_w_gate[l], 'w_up': moe_w_up[l], 'w_down': moe_w_down[l],
        'ln2_g': ln2_g[l].reshape(1, -1), 'ln2_b': ln2_b[l].reshape(1, -1),
    }

    s5ops = _s5_operators(s5_a_re[l], s5_a_im[l], s5_log_dt[l], s5_b_re[l], s5_b_im[l],
                          s5_c_re[l], s5_c_im[l], s5_d[l])
    tabs_ctx = _tables(l_ctx)
    tabs_lat = _tables(l_lat)
    filt_args = (hy_f_w1[l], hy_f_b1[l], hy_f_w2[l], hy_f_b2[l], hy_f_w3[l], hy_freq[l])
    filt_ctx = _hyena_filters(l_ctx, tabs_ctx, *filt_args)
    filt_lat = _hyena_filters(l_lat, tabs_lat, *filt_args)

    zero = jnp.zeros((g, b_ctx, 2 * p), F32)
    yhy_c, ys5_c, f_re, f_im = _mixers(x_prompt, None, mod_ctx, zero, zero, tabs_ctx, filt_ctx, s5ops, wts, 512)
    unpack = lambda f: f.reshape(g, b_ctx, 2, p).transpose(1, 2, 0, 3)[:, None]
    new_re, new_im = unpack(f_re), unpack(f_im)

    pack = lambda s: s[:, l].transpose(2, 0, 1, 3).reshape(g, b_lat, 2 * p)
    pos = _grid_pos_embed(l_lat)
    yhy_l, ys5_l, _, _ = _mixers(x_sample, pos, mod_lat, pack(state_s5_re), pack(state_s5_im),
                                 tabs_lat, filt_lat, s5ops, wts, 512)

    n_ctx = b_ctx * l_ctx
    x1_all, h2_all, gates_all, tile_counts = _out_proj(
        x_prompt.reshape(n_ctx, D_MODEL), x_sample.reshape(b_lat * l_lat, D_MODEL), pos,
        yhy_c, yhy_l, ys5_c, ys5_l, mod, wts['w_glu'], wts['b_glu'], wts['out_norm_g'], wts['w_out'],
        wts['ln1_g'], wts['ln1_b'], wts['wr_hi'], wts['wr_lo'], wts['br'], MOE_ST)
    y_ctx, y_lat = _moe(h2_all, gates_all, tile_counts, x1_all, mod,
                        wts['w_gate'], wts['w_up'], wts['w_down'], wts['ln2_g'], wts['ln2_b'],
                        n_ctx, l_lat)
    return (y_ctx.reshape(x_prompt.shape), y_lat.reshape(x_sample.shape), new_re, new_im)
```

```python
import functools
import math

import numpy as np
import jax
import jax.numpy as jnp
from jax import lax
from jax.experimental import pallas as pl
from jax.experimental.pallas import tpu as pltpu

F32 = jnp.float32
BF16 = jnp.bfloat16

D_MODEL = 1024
DEPTH = 1
GRID_W = 64
POS_BASE = 10000.0
D_HY = 512
D_S5 = 512
S5_CH = 16
S5_GROUPS = 32
S5_STATE = 64
S5_CHUNK = 16
S5_ROW = S5_CHUNK * S5_CH
HY_BANDS = 16
HY_EMB = 1 + 2 * HY_BANDS
HY_HID = 64
HY_MIN_DECAY = math.log(1e-2) / 1.5
HY_MAX_DECAY = math.log(1e-2) / 0.3
N_EGROUPS = 4
N_EPG = 4
N_EXPERTS = 16
D_EXPERT = 512
LN_EPS = 1e-5
ALPHA = (2.0 * DEPTH) ** 0.25
LANES = 128
S5_GB = LANES // S5_CH
S5OPS_GB = 4
HY_CW = 512
MOE_ST = 512
MOE_SLOTS = 640
MOE_UNIT = 16
MOE_TM = 512
VMEM_LIMIT = 60000 * 1024


def _cparams(sem):
    return pltpu.CompilerParams(dimension_semantics=sem, vmem_limit_bytes=VMEM_LIMIT)


def _split(x):
    hi = x.astype(BF16)
    lo = (x - hi.astype(F32)).astype(BF16)
    return hi, lo


def _dot(a, b):
    return jnp.dot(a, b, preferred_element_type=F32)


def _dot_t(a, b):
    return lax.dot_general(a, b, (((1,), (1,)), ((), ())), preferred_element_type=F32)


def _mm3(a, b):
    ah, al = _split(a)
    bh, bl = _split(b)
    return _dot(ah, bh) + _dot(al, bh) + _dot(ah, bl)


def _mm3_t(a, b):
    ah, al = _split(a)
    bh, bl = _split(b)
    return _dot_t(ah, bh) + _dot_t(al, bh) + _dot_t(ah, bl)


def _norm(x):
    xc = x - jnp.mean(x, axis=-1, keepdims=True)
    return xc * lax.rsqrt(jnp.mean(xc * xc, axis=-1, keepdims=True) + LN_EPS)


def _rms(y):
    return y * lax.rsqrt(jnp.mean(y * y, axis=-1, keepdims=True) + LN_EPS)


def _ada_kernel(cond_ref, w_ref, b_ref, o_ref):
    c = jax.nn.silu(cond_ref[...])
    o_ref[...] = _mm3(c, w_ref[...]) + b_ref[...]


def _ada(cond, w_ada, b_ada):
    nb = cond.shape[0]
    n = w_ada.shape[1]
    tn = 1024
    return pl.pallas_call(
        _ada_kernel,
        grid=(n // tn,),
        in_specs=[pl.BlockSpec((nb, D_MODEL), lambda j: (0, 0)),
                  pl.BlockSpec((D_MODEL, tn), lambda j: (0, j)),
                  pl.BlockSpec((1, tn), lambda j: (0, j))],
        out_specs=pl.BlockSpec((nb, tn), lambda j: (0, j)),
        out_shape=jax.ShapeDtypeStruct((nb, n), F32),
        compiler_params=_cparams(("arbitrary",)),
        name="ada",
    )(cond, w_ada, b_ada.reshape(1, n))


def _dft_tables(n_half):
    n = 2 * n_half
    idx = np.arange(n_half, dtype=np.int64)
    m = (idx[:, None] * idx[None, :]) % n
    ang = 2.0 * np.pi * m.astype(np.float64) / n
    cm = np.cos(ang)
    sm = -np.sin(ang)
    sm[0, :] = 1.0 - 2.0 * (idx % 2)
    return cm.astype(np.float32), sm.astype(np.float32)


def _tables(n_tok):
    n_half = n_tok // 2
    cm, sm = _dft_tables(n_half)
    mats = tuple(jnp.asarray(t).astype(BF16) for t in (cm, sm, np.ascontiguousarray(sm.T)))
    ang = np.pi * np.arange(n_half, dtype=np.float64) / n_tok
    tw = [np.broadcast_to(v[:, None], (n_half, HY_CW)).astype(np.float32) for v in (np.cos(ang), -np.sin(ang))]
    return mats + (jnp.asarray(tw[0]), jnp.asarray(tw[1]))


def _put_cols(ref, x):
    for j in range(ref.shape[0]):
        ref[j] = x[:, LANES * j:LANES * (j + 1)]


def _get_cols(ref):
    return jnp.concatenate([ref[j] for j in range(ref.shape[0])], axis=1)


def _get_parity(ref, parity):
    n_half = ref.shape[1] // 2
    return jnp.concatenate([ref[j, pl.ds(parity, n_half, stride=2), :] for j in range(ref.shape[0])], axis=1)


def _put_parity(ref, parity, x):
    n_half = ref.shape[1] // 2
    for j in range(ref.shape[0]):
        ref[j, pl.ds(parity, n_half, stride=2), :] = x[:, LANES * j:LANES * (j + 1)]


def _set_row0(x, v):
    first = lax.broadcasted_iota(jnp.int32, (8, x.shape[1]), 0) == 0
    return jnp.concatenate([jnp.where(first, v, x[:8]), x[8:]], axis=0)


def _rfft_packed(x_ref, cm, sm, tw_re, tw_im):
    xe = _get_parity(x_ref, 0).astype(BF16)
    xo = _get_parity(x_ref, 1).astype(BF16)
    e_re, e_im = _dot(cm, xe), _dot(sm, xe)
    o_re, o_im = _dot(cm, xo), _dot(sm, xo)
    t_re = tw_re * o_re - tw_im * o_im
    t_im = tw_re * o_im + tw_im * o_re
    a_im = _set_row0(e_im + t_im, e_im[0:1])
    b_im = _set_row0(t_im - e_im, -o_im[0:1])
    return e_re + t_re, a_im, e_re - t_re, b_im


def _irfft_packed(y_ref, ya_re, ya_im, yb_re, yb_im, cm, st, tw_re, tw_im):
    p_e = ya_re + yb_re
    q_e = _set_row0(ya_im - yb_im, ya_im[0:1])
    _put_parity(y_ref, 0, _dot(cm, p_e.astype(BF16)) + _dot(st, q_e.astype(BF16)))
    ra_re = ya_re * tw_re + ya_im * tw_im
    ra_im = ya_im * tw_re - ya_re * tw_im
    rb_re = yb_im * tw_im - yb_re * tw_re
    rb_im = -(yb_re * tw_im + yb_im * tw_re)
    p_o = ra_re + rb_re
    q_o = _set_row0(ra_im - rb_im, -yb_im[0:1])
    _put_parity(y_ref, 1, _dot(cm, p_o.astype(BF16)) + _dot(st, q_o.astype(BF16)))


def _filt_kernel(n_tok, z_ref, t_ref, w1_ref, b1_ref, w2_ref, b2_ref, fr_ref, w3f_ref, w3b_ref,
                 dl_ref, cm_ref, sm_ref, twr_ref, twi_ref, kar_ref, kai_ref, kbr_ref, kbi_ref, p_ref, q_ref):
    fr = fr_ref[...]
    h = jnp.sin(fr * (_mm3(z_ref[...], w1_ref[...]) + b1_ref[...]))
    h = jnp.sin(fr * (_mm3(h, w2_ref[...]) + b2_ref[...]))
    decay = jnp.exp(-t_ref[...] * dl_ref[...])
    row = lax.broadcasted_iota(jnp.int32, decay.shape, 0)
    hf = _mm3(h, w3f_ref[...]) * decay
    hb = jnp.where(row == 0, 0.0, _mm3(h, w3b_ref[...]) * decay)
    _put_cols(p_ref, hf + hb)
    _put_cols(q_ref, hf - hb)
    cm, sm, tw_re, tw_im = cm_ref[...], sm_ref[...], twr_ref[...], twi_ref[...]
    row0 = lax.broadcasted_iota(jnp.int32, tw_re.shape, 0) == 0
    pa_re, pa_im, pb_re, _ = _rfft_packed(p_ref, cm, sm, tw_re, tw_im)
    _, qa_im, _, qb_im = _rfft_packed(q_ref, cm, sm, tw_re, tw_im)
    inv_n = 1.0 / (2 * n_tok)
    w_re = jnp.where(row0, inv_n, 2.0 * inv_n)
    kar_ref[...] = w_re * pa_re
    kbr_ref[...] = w_re * pb_re
    kai_ref[...] = (2.0 * inv_n) * _set_row0(qa_im, pa_im[0:1])
    kbi_ref[...] = (2.0 * inv_n) * qb_im


def _hyena_filters(n_tok, tabs, hy_f_w1, hy_f_b1, hy_f_w2, hy_f_b2, hy_f_w3, hy_freq):
    cm, sm, _, tw_re, tw_im = tabs
    n_half = n_tok // 2
    t = jnp.linspace(0.0, 1.0, n_tok, dtype=F32)[:, None]
    wv = 2.0 * math.pi * jnp.arange(n_tok, dtype=F32) / n_tok
    fb = jnp.linspace(1e-4, HY_BANDS - 1, HY_BANDS, dtype=F32)
    ang = wv[:, None] * fb[None, :]
    z = jnp.concatenate([t, jnp.cos(ang), -jnp.sin(ang)], axis=-1)
    z = jnp.pad(z, ((0, 0), (0, LANES - HY_EMB)))
    w1 = jnp.pad(hy_f_w1, ((0, LANES - HY_EMB), (0, 0)))
    deltas = jnp.abs(jnp.linspace(HY_MIN_DECAY, HY_MAX_DECAY, D_HY, dtype=F32))[None, :]
    ncb = D_HY // HY_CW
    full = lambda j: (0, 0)
    out_sd = jax.ShapeDtypeStruct((n_half, 2 * D_HY), F32)
    mat = pl.BlockSpec((n_half, n_half), full, pipeline_mode=pl.Buffered(1))
    twb = pl.BlockSpec((n_half, HY_CW), full, pipeline_mode=pl.Buffered(1))
    return pl.pallas_call(
        functools.partial(_filt_kernel, n_tok),
        grid=(2 * ncb,),
        in_specs=[pl.BlockSpec((n_tok, LANES), full),
                  pl.BlockSpec((n_tok, 1), full),
                  pl.BlockSpec((LANES, HY_HID), full),
                  pl.BlockSpec((1, HY_HID), full),
                  pl.BlockSpec((HY_HID, HY_HID), full),
                  pl.BlockSpec((1, HY_HID), full),
                  pl.BlockSpec((1, HY_HID), full),
                  pl.BlockSpec((HY_HID, HY_CW), lambda j: (0, 2 * ncb * (j // ncb) + j % ncb)),
                  pl.BlockSpec((HY_HID, HY_CW), lambda j: (0, 2 * ncb * (j // ncb) + ncb + j % ncb)),
                  pl.BlockSpec((1, HY_CW), lambda j: (0, j % ncb)),
                  mat, mat, twb, twb],
        out_specs=[pl.BlockSpec((n_half, HY_CW), lambda j: (0, j))] * 4,
        out_shape=[out_sd] * 4,
        scratch_shapes=[pltpu.VMEM((HY_CW // LANES, n_tok, LANES), F32)] * 2,
        compiler_params=_cparams(("arbitrary",)),
        name=f"filt{n_tok}",
    )(z, t, w1, hy_f_b1.reshape(1, -1), hy_f_w2, hy_f_b2.reshape(1, -1), hy_freq.reshape(1, -1),
      hy_f_w3, hy_f_w3, deltas, cm, sm, tw_re, tw_im)


def _hyena_kernel(pv_ref, p1_ref, p2_ref, cwv_ref, cw1_ref, cw2_ref, cbv_ref, cb1_ref, cb2_ref,
                  fbias_ref, cm_ref, sm_ref, st_ref, twr_ref, twi_ref,
                  kar0_ref, kai0_ref, kbr0_ref, kbi0_ref, kar1_ref, kai1_ref, kbr1_ref, kbi1_ref,
                  o_ref, u_ref, y_ref):
    n_tok = pv_ref.shape[1]
    row = lax.broadcasted_iota(jnp.int32, (n_tok, pv_ref.shape[2]), 0)

    def short_conv(p_ref, cw_ref, cb_ref):
        p = p_ref[0]
        prev = jnp.where(row == 0, 0.0, pltpu.roll(p, 1, axis=0))
        nxt = jnp.where(row == n_tok - 1, 0.0, pltpu.roll(p, n_tok - 1, axis=0))
        return cb_ref[...] + prev * cw_ref[0:1, :] + p * cw_ref[1:2, :] + nxt * cw_ref[2:3, :]

    cm, sm, st, tw_re, tw_im = cm_ref[...], sm_ref[...], st_ref[...], twr_ref[...], twi_ref[...]

    def fftconv(u, kar_ref, kai_ref, kbr_ref, kbi_ref, skip):
        _put_cols(u_ref, u)
        ua_re, ua_im, ub_re, ub_im = _rfft_packed(u_ref, cm, sm, tw_re, tw_im)
        ka_re, ka_im, kb_re, kb_im = kar_ref[...], kai_ref[...], kbr_ref[...], kbi_ref[...]
        zero_row = jnp.zeros_like(ka_im[0:1])
        kaz = _set_row0(ka_im, zero_row)
        kbz = _set_row0(kb_im, zero_row)
        ya_re = ua_re * ka_re - ua_im * kaz
        yb_re = ub_re * kb_re - ub_im * kbz
        h_re = ua_im[0:1] * ka_im[0:1] - ub_im[0:1] * kb_im[0:1]
        h_im = ua_im[0:1] * kb_im[0:1] + ub_im[0:1] * ka_im[0:1]
        ya_im = _set_row0(ua_re * ka_im + ua_im * ka_re, h_re)
        yb_im = _set_row0(ub_re * kb_im + ub_im * kb_re, h_im)
        _irfft_packed(y_ref, ya_re, ya_im, yb_re, yb_im, cm, st, tw_re, tw_im)
        return _get_cols(y_ref) + u * skip

    v = short_conv(pv_ref, cwv_ref, cbv_ref)
    x1 = short_conv(p1_ref, cw1_ref, cb1_ref)
    z = x1 * fftconv(v, kar0_ref, kai0_ref, kbr0_ref, kbi0_ref, fbias_ref[0:1, :])
    x2 = short_conv(p2_ref, cw2_ref, cb2_ref)
    o_ref[0] = x2 * fftconv(z, kar1_ref, kai1_ref, kbr1_ref, kbi1_ref, fbias_ref[1:2, :])


def _hyena(proj_hy, tabs, filt, hy_conv_w, hy_conv_b, hy_fbias):
    bsz, n_tok, _ = proj_hy.shape
    n_half = n_tok // 2
    ncb = D_HY // HY_CW
    cm, sm, st, tw_re, tw_im = tabs
    cb = hy_conv_b.reshape(1, -1)
    const = lambda shape: pl.BlockSpec(shape, lambda b, c: (0, 0), pipeline_mode=pl.Buffered(1))
    mat = const((n_half, n_half))
    twb = const((n_half, HY_CW))

    def pspec(k):
        return pl.BlockSpec((1, n_tok, HY_CW), lambda b, c: (b, 0, k * ncb + c))

    def cwspec(k):
        return pl.BlockSpec((3, HY_CW), lambda b, c: (0, k * ncb + c))

    def cbspec(k):
        return pl.BlockSpec((1, HY_CW), lambda b, c: (0, k * ncb + c))

    def fspec(o):
        mode = pl.Buffered(1) if ncb == 1 else None
        return pl.BlockSpec((n_half, HY_CW), lambda b, c: (0, o * ncb + c), pipeline_mode=mode)

    return pl.pallas_call(
        _hyena_kernel,
        grid=(bsz, ncb),
        in_specs=[pspec(0), pspec(1), pspec(2), cwspec(0), cwspec(1), cwspec(2),
                  cbspec(0), cbspec(1), cbspec(2),
                  pl.BlockSpec((2, HY_CW), lambda b, c: (0, c)),
                  mat, mat, mat, twb, twb] + [fspec(0)] * 4 + [fspec(1)] * 4,
        out_specs=pl.BlockSpec((1, n_tok, HY_CW), lambda b, c: (b, 0, c)),
        out_shape=jax.ShapeDtypeStruct((bsz, n_tok, D_HY), F32),
        scratch_shapes=[pltpu.VMEM((HY_CW // LANES, n_tok, LANES), F32)] * 2,
        compiler_params=_cparams(("arbitrary", "arbitrary")),
        name=f"hyena{n_tok}",
    )(proj_hy, proj_hy, proj_hy, hy_conv_w, hy_conv_w, hy_conv_w, cb, cb, cb, hy_fbias,
      cm, sm, st, tw_re, tw_im, *filt, *filt)


def _s5ops_kernel(*refs):
    for g in range(S5OPS_GB):
        _s5ops_group(g, *refs)


def _s5ops_group(g, are_ref, aim_ref, ldt_ref, btr_ref, bti_ref, cre_ref, cim_ref, d_ref,
                 mt_ref, erh_ref, erl_ref, eih_ref, eil_ref, gr_ref, gi_ref, atr_ref, ati_ref,
                 er_ref, ei_ref):
    a_re, a_im = are_ref[g], aim_ref[g]
    dt = jnp.exp(ldt_ref[g])
    mag = jnp.exp(a_re * dt)
    ab_re = mag * jnp.cos(a_im * dt)
    ab_im = mag * jnp.sin(a_im * dt)
    n_re, n_im = ab_re - 1.0, ab_im
    den = a_re * a_re + a_im * a_im
    q_re = (n_re * a_re + n_im * a_im) / den
    q_im = (n_im * a_re - n_re * a_im) / den
    bt_re, bt_im = btr_ref[g], bti_ref[g]
    bb_re = q_re * bt_re - q_im * bt_im
    bb_im = q_re * bt_im + q_im * bt_re
    c_re, c_im = cre_ref[g, 0:S5_CH, :], cim_ref[g, 0:S5_CH, :]
    pw = [(jnp.ones_like(ab_re), jnp.zeros_like(ab_re))]
    for _ in range(S5_CHUNK):
        pr, pi = pw[-1]
        pw.append((pr * ab_re - pi * ab_im, pr * ab_im + pi * ab_re))
    lane = lax.broadcasted_iota(jnp.int32, ab_re.shape, 1)
    fwd = lane < S5_STATE
    for s in range(S5_CHUNK):
        e_re = jnp.where(fwd, pw[S5_CHUNK - 1 - s][0], pw[s][0])
        e_im = jnp.where(fwd, pw[S5_CHUNK - 1 - s][1], pw[s][1])
        er_ref[g, pl.ds(S5_CH * s, S5_CH), :] = e_re * bb_re - e_im * bb_im
        ei_ref[g, pl.ds(S5_CH * s, S5_CH), :] = e_re * bb_im + e_im * bb_re
        g_re = jnp.where(fwd, pw[s + 1][0], pw[S5_CHUNK - s][0])
        g_im = jnp.where(fwd, pw[s + 1][1], pw[S5_CHUNK - s][1])
        gr_ref[g, pl.ds(S5_CH * s, S5_CH), :] = (c_re * g_re - c_im * g_im).astype(BF16)
        gi_ref[g, pl.ds(S5_CH * s, S5_CH), :] = (-(c_re * g_im + c_im * g_re)).astype(BF16)
    atr_ref[g] = pw[S5_CHUNK][0]
    ati_ref[g] = pw[S5_CHUNK][1]
    er, ei = er_ref[g], ei_ref[g]
    erh_ref[g], erl_ref[g] = _split(er)
    eih_ref[g], eil_ref[g] = _split(ei)
    lane2 = lax.broadcasted_iota(jnp.int32, er.shape, 1)
    row2 = lax.broadcasted_iota(jnp.int32, er.shape, 0)
    f2 = lane2 < S5_STATE
    zero = jnp.zeros_like(er)

    cp_re, cp_im = cre_ref[g], cim_ref[g]
    kf = _mm3_t(jnp.where(f2, er, zero), cp_re) - _mm3_t(jnp.where(f2, ei, zero), cp_im)
    kb = _mm3_t(jnp.where(f2, zero, er), cp_re) - _mm3_t(jnp.where(f2, zero, ei), cp_im)
    d_row = d_ref[g]
    steps_per_vreg = LANES // S5_CH
    for half in range(S5_CHUNK // steps_per_vreg):
        acc = zero
        for tt in range(steps_per_vreg):
            t = half * steps_per_vreg + tt
            nf = S5_CH * (S5_CHUNK - 1 - t)
            nb = S5_CH * t
            col_f = jnp.concatenate([kf[nf:], zero[:nf]], axis=0) if nf else kf
            col_b = jnp.concatenate([zero[:nb], kb[:S5_ROW - nb]], axis=0) if nb else kb
            diag = jnp.where((row2 // S5_CH == t) & (row2 % S5_CH == lane2), d_row, 0.0)
            col = col_f + col_b + diag
            r = pltpu.roll(col, S5_CH * tt, axis=1) if tt else col
            acc = jnp.where((lane2 >= S5_CH * tt) & (lane2 < S5_CH * (tt + 1)), r, acc)
        mt_ref[g, :, LANES * half:LANES * (half + 1)] = acc.astype(BF16)


def _s5_operators(s5_a_re, s5_a_im, s5_log_dt, s5_b_re, s5_b_im, s5_c_re, s5_c_im, s5_d):
    g, p, h = S5_GROUPS, S5_STATE, S5_CH
    cat = lambda x: jnp.concatenate([x[0], x[1]], axis=-1)
    a_re = cat(s5_a_re).reshape(g, 1, 2 * p)
    a_im = cat(s5_a_im).reshape(g, 1, 2 * p)
    ldt = cat(jnp.broadcast_to(s5_log_dt[:, :, None], (2, g, p))).reshape(g, 1, 2 * p)
    bt_re = cat(jnp.swapaxes(s5_b_re, -1, -2))
    bt_im = cat(jnp.swapaxes(s5_b_im, -1, -2))
    cpad = lambda c: jnp.pad(jnp.concatenate([c, c], axis=-1), ((0, 0), (0, LANES - h), (0, 0)))
    c_re, c_im = cpad(s5_c_re), cpad(s5_c_im)
    d_row = jnp.pad(s5_d.reshape(g, 1, h), ((0, 0), (0, 0), (0, LANES - h)))
    vec = pl.BlockSpec((S5OPS_GB, 1, 2 * p), lambda i: (i, 0, 0))
    hp = pl.BlockSpec((S5OPS_GB, h, 2 * p), lambda i: (i, 0, 0))
    sq = pl.BlockSpec((S5OPS_GB, LANES, 2 * p), lambda i: (i, 0, 0))
    big = pl.BlockSpec((S5OPS_GB, S5_ROW, 2 * p), lambda i: (i, 0, 0))
    mts = pl.BlockSpec((S5OPS_GB, S5_ROW, S5_ROW), lambda i: (i, 0, 0))
    big_sd = jax.ShapeDtypeStruct((g, S5_ROW, 2 * p), BF16)
    vec_sd = jax.ShapeDtypeStruct((g, 1, 2 * p), F32)
    return pl.pallas_call(
        _s5ops_kernel,
        grid=(g // S5OPS_GB,),
        in_specs=[vec, vec, vec, hp, hp, sq, sq, vec],
        out_specs=[mts, big, big, big, big, big, big, vec, vec],
        out_shape=[jax.ShapeDtypeStruct((g, S5_ROW, S5_ROW), BF16)] + [big_sd] * 6 + [vec_sd, vec_sd],
        scratch_shapes=[pltpu.VMEM((S5OPS_GB, S5_ROW, 2 * p), F32)] * 2,
        compiler_params=_cparams(("arbitrary",)),
        name="s5ops",
    )(a_re, a_im, ldt, bt_re, bt_im, c_re, c_im, d_row)


def _block_transpose(xs):
    n = len(xs)
    lane = lax.broadcasted_iota(jnp.int32, xs[0].shape, 1)
    xs = list(xs)
    d = n // 2
    while d:
        keep = ((lane // S5_CH) & d) == 0
        for i in range(n):
            if i & d:
                continue
            lo, hi = xs[i], xs[i + d]
            xs[i] = jnp.where(keep, lo, pltpu.roll(hi, S5_CH * d, axis=1))
            xs[i + d] = jnp.where(keep, pltpu.roll(lo, LANES - S5_CH * d, axis=1), hi)
        d //= 2
    return xs


def _s5_kernel(bsz, n_chunks, u_ref, mt_ref, erh_ref, erl_ref, eih_ref, eil_ref, gr_ref, gi_ref,
               atr_ref, ati_ref, h0r_ref, h0i_ref, y_ref, fr_ref, fi_ref,
               ua_ref, ub_ref, ya_ref, yb_ref, sr_ref, si_ref, xfr_ref, xfi_ref, xbr_ref, xbi_ref):
    nc = n_chunks
    spv = LANES // S5_CH
    rsub = min(nc, 32)

    def to_chunks(b, carry):
        for half, dst in ((0, ua_ref), (1, ub_ref)):
            for r0 in range(0, nc, rsub):
                xs = [u_ref[b, pl.ds(S5_CHUNK * r0 + half * spv + tt, rsub, stride=S5_CHUNK), :]
                      for tt in range(spv)]
                for k, blk in enumerate(_block_transpose(xs)):
                    dst[k, pl.ds(r0 * bsz + b, rsub, stride=bsz), :] = blk
        return carry

    lax.fori_loop(0, bsz, to_chunks, 0, unroll=2)

    lane = lax.broadcasted_iota(jnp.int32, (bsz, 2 * S5_STATE), 1)
    fwd = lane < S5_STATE
    lane_all = lax.broadcasted_iota(jnp.int32, (bsz * nc, 2 * S5_STATE), 1)
    fwd_all = lane_all < S5_STATE

    def group(k, slot):
        u = jnp.concatenate([ua_ref[k], ub_ref[k]], axis=1)
        uh, ul = _split(u)
        sr_ref[slot] = _dot(uh, erh_ref[k]) + _dot(ul, erh_ref[k]) + _dot(uh, erl_ref[k])
        si_ref[slot] = _dot(uh, eih_ref[k]) + _dot(ul, eih_ref[k]) + _dot(uh, eil_ref[k])
        at_re, at_im = atr_ref[k], ati_ref[k]
        y_intra = _dot(uh, mt_ref[k])

        def step(i, xc):
            x_re, x_im = xc
            rf = pl.ds(pl.multiple_of(i * bsz, bsz), bsz)
            rb = pl.ds(pl.multiple_of((nc - 1 - i) * bsz, bsz), bsz)
            xfr_ref[slot, rf, :] = x_re
            xfi_ref[slot, rf, :] = x_im
            xbr_ref[slot, rb, :] = x_re
            xbi_ref[slot, rb, :] = x_im
            s_re = jnp.where(fwd, sr_ref[slot, rf, :], sr_ref[slot, rb, :])
            s_im = jnp.where(fwd, si_ref[slot, rf, :], si_ref[slot, rb, :])
            return (at_re * x_re - at_im * x_im + s_re, at_re * x_im + at_im * x_re + s_im)

        x_re, x_im = lax.fori_loop(0, nc, step, (h0r_ref[k], h0i_ref[k]), unroll=True)
        fr_ref[k] = x_re
        fi_ref[k] = x_im
        xp_re = jnp.where(fwd_all, xfr_ref[slot], xbr_ref[slot]).astype(BF16)
        xp_im = jnp.where(fwd_all, xfi_ref[slot], xbi_ref[slot]).astype(BF16)
        y = y_intra + _dot_t(xp_re, gr_ref[k]) + _dot_t(xp_im, gi_ref[k])
        ya_ref[k] = y[:, :LANES]
        yb_ref[k] = y[:, LANES:]

    def group_pair(j, carry):
        group(2 * j, 0)
        group(2 * j + 1, 1)
        return carry

    lax.fori_loop(0, S5_GB // 2, group_pair, 0)

    def to_tokens(b, carry):
        for half, src in ((0, ya_ref), (1, yb_ref)):
            for r0 in range(0, nc, rsub):
                ys = [src[k, pl.ds(r0 * bsz + b, rsub, stride=bsz), :] for k in range(S5_GB)]
                for tt, blk in enumerate(_block_transpose(ys)):
                    y_ref[b, pl.ds(S5_CHUNK * r0 + half * spv + tt, rsub, stride=S5_CHUNK), :] = blk
        return carry

    lax.fori_loop(0, bsz, to_tokens, 0, unroll=2)


def _s5(u, ops, h0_re, h0_im):
    bsz, n_tok, _ = u.shape
    g, p = S5_GROUPS, S5_STATE
    nc = n_tok // S5_CHUNK
    rows = nc * bsz
    tok = pl.BlockSpec((bsz, n_tok, LANES), lambda j: (0, 0, j))
    gspec = lambda shape: pl.BlockSpec((S5_GB,) + shape, lambda j: (j, 0, 0))
    op = gspec((S5_ROW, 2 * p))
    return pl.pallas_call(
        functools.partial(_s5_kernel, bsz, nc),
        grid=(g // S5_GB,),
        in_specs=[tok, gspec((S5_ROW, S5_ROW)), op, op, op, op, op, op,
                  gspec((1, 2 * p)), gspec((1, 2 * p)), gspec((bsz, 2 * p)), gspec((bsz, 2 * p))],
        out_specs=[tok, gspec((bsz, 2 * p)), gspec((bsz, 2 * p))],
        out_shape=[jax.ShapeDtypeStruct((bsz, n_tok, D_S5), F32),
                   jax.ShapeDtypeStruct((g, bsz, 2 * p), F32),
                   jax.ShapeDtypeStruct((g, bsz, 2 * p), F32)],
        scratch_shapes=([pltpu.VMEM((S5_GB, rows, LANES), F32)] * 4
                        + [pltpu.VMEM((2, rows, 2 * p), F32)] * 6),
        compiler_params=_cparams(("arbitrary",)),
        name=f"s5_{n_tok}",
    )(u, *ops, h0_re, h0_im)


def _in_kernel(has_pos, *refs):
    if has_pos:
        x_ref, pos_ref, mod_ref, w_ref, hy_ref, s5_ref = refs
        x = x_ref[0] + pos_ref[...]
    else:
        x_ref, mod_ref, w_ref, hy_ref, s5_ref = refs
        x = x_ref[0]
    sh1 = mod_ref[0, :, 0:D_MODEL]
    sc1 = mod_ref[0, :, D_MODEL:2 * D_MODEL]
    h = _norm(x) * (1.0 + sc1) + sh1
    proj = _dot(h.astype(BF16), w_ref[...])
    hy_ref[0] = proj[:, :3 * D_HY]
    s5_ref[0] = proj[:, 3 * D_HY:]


def _in_proj(x3, pos, mod3, w_in_bf, tm):
    nb, lt, _ = x3.shape
    has_pos = pos is not None
    per_batch = mod3.shape[0] > 1
    midx = (lambda b, i: (b, 0, 0)) if per_batch else (lambda b, i: (0, 0, 0))
    in_specs = [pl.BlockSpec((1, tm, D_MODEL), lambda b, i: (b, i, 0))]
    args = [x3]
    if has_pos:
        in_specs.append(pl.BlockSpec((tm, D_MODEL), lambda b, i: (i, 0)))
        args.append(pos)
    in_specs += [pl.BlockSpec((1, 1, 6 * D_MODEL), midx),
                 pl.BlockSpec((D_MODEL, 3 * D_HY + D_S5), lambda b, i: (0, 0))]
    args += [mod3, w_in_bf]
    return pl.pallas_call(
        functools.partial(_in_kernel, has_pos),
        grid=(nb, lt // tm),
        in_specs=in_specs,
        out_specs=[pl.BlockSpec((1, tm, 3 * D_HY), lambda b, i: (b, i, 0)),
                   pl.BlockSpec((1, tm, D_S5), lambda b, i: (b, i, 0))],
        out_shape=[jax.ShapeDtypeStruct((nb, lt, 3 * D_HY), F32),
                   jax.ShapeDtypeStruct((nb, lt, D_S5), F32)],
        compiler_params=_cparams(("arbitrary", "arbitrary")),
        name=f"in_proj{nb}",
    )(*args)


def _route(logits):
    lane = lax.broadcasted_iota(jnp.int32, logits.shape, 1)
    lane_f = lane.astype(F32)
    neg = -jnp.inf
    big = float(LANES)
    m1 = (lane >= N_EXPERTS) & (lane < N_EXPERTS + N_EGROUPS)
    l1 = jnp.where(m1, logits, neg)
    top1 = jnp.max(l1, axis=-1, keepdims=True)
    grp = jnp.min(jnp.where(l1 == top1, lane_f, big), axis=-1, keepdims=True) - float(N_EXPERTS)
    den = jnp.sum(jnp.where(m1, jnp.exp(logits - top1), 0.0), axis=-1, keepdims=True)
    p_grp = 1.0 / den
    lo = grp * float(N_EPG)
    m2 = (lane_f >= lo) & (lane_f < lo + float(N_EPG))
    l2 = jnp.where(m2, logits, neg)
    v1 = jnp.max(l2, axis=-1, keepdims=True)
    i1 = jnp.min(jnp.where(l2 == v1, lane_f, big), axis=-1, keepdims=True)
    l2b = jnp.where(lane_f == i1, neg, l2)
    v2 = jnp.max(l2b, axis=-1, keepdims=True)
    i2 = jnp.min(jnp.where(l2b == v2, lane_f, big), axis=-1, keepdims=True)
    e = jnp.exp(v2 - v1)
    w1 = 1.0 / (1.0 + e)
    w2 = e / (1.0 + e)
    gates = jnp.where(lane_f == i1, w1 * p_grp, 0.0) + jnp.where(lane_f == i2, w2 * p_grp, 0.0)
    return jnp.where(lane_f == grp + float(N_EXPERTS), 1.0, gates)


def _out_kernel(n_ctx_blocks, xc_ref, xl_ref, pos_ref, yhyc_ref, yhyl_ref, ys5c_ref, ys5l_ref, mod_ref,
                wglu_ref, bglu_ref, ong_ref, wout_ref, ln1g_ref, ln1b_ref, wrh_ref, wrl_ref, br_ref,
                x1_ref, h2_ref, gate_ref, cnt_ref):
    is_ctx = pl.program_id(0) < n_ctx_blocks
    x = jnp.where(is_ctx, xc_ref[...], xl_ref[...] + pos_ref[...])
    y = jnp.where(is_ctx, ys5c_ref[...], ys5l_ref[...])
    y_hy = jnp.where(is_ctx, yhyc_ref[...], yhyl_ref[...])
    s5 = jax.nn.gelu(y) * jax.nn.sigmoid(_dot(y.astype(BF16), wglu_ref[...]) + bglu_ref[...])
    m_hy = _rms(y_hy) * ong_ref[:, 0:D_HY]
    m_s5 = _rms(s5) * ong_ref[:, D_HY:]
    o = (_dot(m_hy.astype(BF16), wout_ref[0:D_HY, :]) + _dot(m_s5.astype(BF16), wout_ref[D_HY:, :]))
    g1 = mod_ref[0, :, 2 * D_MODEL:3 * D_MODEL]
    sh2 = mod_ref[0, :, 3 * D_MODEL:4 * D_MODEL]
    sc2 = mod_ref[0, :, 4 * D_MODEL:5 * D_MODEL]
    x1 = _norm(ALPHA * x + g1 * o) * ln1g_ref[...] + ln1b_ref[...]
    x1_ref[...] = x1
    h2 = _norm(x1) * (1.0 + sc2) + sh2
    h2_ref[...] = h2.astype(BF16)
    hh, hl = _split(h2)
    logits = (_dot(hh, wrh_ref[...]) + _dot(hl, wrh_ref[...]) + _dot(hh, wrl_ref[...]) + br_ref[...])
    gates = _route(logits)
    gate_ref[...] = gates
    cnt_ref[0] = jnp.sum(gates, axis=0, keepdims=True)


def _out_proj(xc, xl, pos, yhy_c, yhy_l, ys5_c, ys5_l, mod, wglu_bf, bglu, ong, wout_bf, ln1g, ln1b,
              wr_hi, wr_lo, br, tm):
    n_ctx, n_lat = xc.shape[0], xl.shape[0]
    l_lat = pos.shape[0]
    ncb, nlb, npb = n_ctx // tm, n_lat // tm, l_lat // tm
    ctx = lambda w: pl.BlockSpec((tm, w), lambda i: (jnp.minimum(i, ncb - 1), 0))
    lat = lambda w: pl.BlockSpec((tm, w), lambda i: (jnp.maximum(i - ncb, 0), 0))
    full = lambda shape: pl.BlockSpec(shape, lambda i: (0,) * len(shape))
    out = lambda w: pl.BlockSpec((tm, w), lambda i: (i, 0))
    mod_idx = lambda i: (jnp.where(i < ncb, 0, 1 + jnp.maximum(i - ncb, 0) // npb), 0, 0)
    n_all = n_ctx + n_lat
    return pl.pallas_call(
        functools.partial(_out_kernel, ncb),
        grid=(ncb + nlb,),
        in_specs=[ctx(D_MODEL), lat(D_MODEL),
                  pl.BlockSpec((tm, D_MODEL), lambda i: (jnp.maximum(i - ncb, 0) % npb, 0)),
                  ctx(D_HY), lat(D_HY), ctx(D_S5), lat(D_S5),
                  pl.BlockSpec((1, 1, 6 * D_MODEL), mod_idx),
                  full((D_S5, D_S5)), full((1, D_S5)), full((1, D_MODEL)), full((D_MODEL, D_MODEL)),
                  full((1, D_MODEL)), full((1, D_MODEL)), full((D_MODEL, LANES)), full((D_MODEL, LANES)),
                  full((1, LANES))],
        out_specs=[out(D_MODEL), out(D_MODEL), out(LANES), pl.BlockSpec((1, 1, LANES), lambda i: (i, 0, 0))],
        out_shape=[jax.ShapeDtypeStruct((n_all, D_MODEL), F32),
                   jax.ShapeDtypeStruct((n_all, D_MODEL), BF16),
                   jax.ShapeDtypeStruct((n_all, LANES), F32),
                   jax.ShapeDtypeStruct((n_all // tm, 1, LANES), F32)],
        compiler_params=_cparams(("arbitrary",)),
        name="out_proj",
    )(xc, xl, pos, yhy_c, yhy_l, ys5_c, ys5_l, mod.reshape(mod.shape[0], 1, 6 * D_MODEL),
      wglu_bf, bglu, ong, wout_bf, ln1g, ln1b, wr_hi, wr_lo, br)


def _perm_t(gates, loc_ref, s):
    n = gates.shape[0]
    lane = lax.broadcasted_iota(jnp.int32, gates.shape, 1)
    oh = jnp.where((lane >= N_EXPERTS) & (lane < N_EXPERTS + N_EGROUPS), gates, 0.0)
    r = lax.broadcasted_iota(jnp.int32, (n, n), 0)
    c = lax.broadcasted_iota(jnp.int32, (n, n), 1)
    earlier = jnp.where(c < r, 1.0, 0.0).astype(BF16)
    cum = _dot(earlier, oh.astype(BF16))
    rank = jnp.sum(cum * oh, axis=-1, keepdims=True)
    lane1 = lax.broadcasted_iota(jnp.int32, (1, LANES), 1)
    locv = jnp.zeros((1, LANES), F32)
    for grp in range(N_EGROUPS):
        locv = jnp.where(lane1 == N_EXPERTS + grp, loc_ref[N_EGROUPS * s + grp].astype(F32), locv)
    dest = rank + jnp.sum(oh * locv, axis=-1, keepdims=True)
    slot = lax.broadcasted_iota(jnp.int32, (n, MOE_SLOTS), 1).astype(F32)
    return jnp.where(slot == dest, 1.0, 0.0)


def _segment_copies(s, loc_ref, len_ref, off_ref, make):
    for grp in range(N_EGROUPS):
        loc = loc_ref[N_EGROUPS * s + grp]
        off = off_ref[N_EGROUPS * s + grp]
        n_units = len_ref[N_EGROUPS * s + grp] // MOE_UNIT

        def body(i, carry):
            make(pl.multiple_of(loc + MOE_UNIT * i, MOE_UNIT), pl.multiple_of(off + MOE_UNIT * i, MOE_UNIT))
            return carry

        lax.fori_loop(0, n_units, body, 0)


def _pad_copies(pad_ref, n_blocks, zx_v, zg_v, xs_hbm, gs_hbm, sem, op):
    def unit(row, rows):
        getattr(pltpu.make_async_copy(zx_v.at[pl.ds(0, rows), :], xs_hbm.at[pl.ds(row, rows), :], sem.at[0]), op)()
        getattr(pltpu.make_async_copy(zg_v.at[pl.ds(0, rows), :], gs_hbm.at[pl.ds(row, rows), :], sem.at[1]), op)()

    for grp in range(N_EGROUPS):
        start = pad_ref[grp]

        def body(i, carry):
            unit(pl.multiple_of(start + MOE_UNIT * i, MOE_UNIT), MOE_UNIT)
            return carry

        lax.fori_loop(0, pad_ref[N_EGROUPS + grp], body, 0)

    def tail(b, carry):
        unit(pl.multiple_of(b * MOE_TM, MOE_TM), MOE_TM)
        return carry

    lax.fori_loop(pad_ref[2 * N_EGROUPS], n_blocks, tail, 0)


def _moe_sort_kernel(n_blocks, loc_ref, len_ref, off_ref, pad_ref, h_ref, gate_ref, xs_hbm, gs_hbm,
                     xs_v, gs_v, zx_v, zg_v, sem, zsem):
    s = pl.program_id(0)
    slot = s % 2

    @pl.when(s == 0)
    def _():
        zx_v[...] = jnp.zeros_like(zx_v)
        zg_v[...] = jnp.zeros_like(zg_v)
        _pad_copies(pad_ref, n_blocks, zx_v, zg_v, xs_hbm, gs_hbm, zsem, 'start')

    gates = gate_ref[...]
    p = _perm_t(gates, loc_ref, s).T.astype(BF16)
    xs_v[slot] = _dot(p, h_ref[...]).astype(BF16)
    g_hi = gates.astype(BF16)
    r1 = gates - g_hi.astype(F32)
    g_mid = r1.astype(BF16)
    g_lo = (r1 - g_mid.astype(F32)).astype(BF16)
    gs_v[slot] = _dot(p, g_hi) + _dot(p, g_mid) + _dot(p, g_lo)

    def copies(buf):
        def x_copy(lr, gr):
            return pltpu.make_async_copy(xs_v.at[buf, pl.ds(lr, MOE_UNIT), :],
                                         xs_hbm.at[pl.ds(gr, MOE_UNIT), :], sem.at[0, buf])

        def g_copy(lr, gr):
            return pltpu.make_async_copy(gs_v.at[buf, pl.ds(lr, MOE_UNIT), :],
                                         gs_hbm.at[pl.ds(gr, MOE_UNIT), :], sem.at[1, buf])

        def start(lr, gr):
            x_copy(lr, gr).start()
            g_copy(lr, gr).start()

        def wait(lr, gr):
            x_copy(lr, gr).wait()
            g_copy(lr, gr).wait()

        return start, wait

    _segment_copies(s, loc_ref, len_ref, off_ref, copies(slot)[0])

    @pl.when(s > 0)
    def _():
        _segment_copies(s - 1, loc_ref, len_ref, off_ref, copies(1 - slot)[1])

    @pl.when(s == pl.num_programs(0) - 1)
    def _():
        _segment_copies(s, loc_ref, len_ref, off_ref, copies(slot)[1])
        _pad_copies(pad_ref, n_blocks, zx_v, zg_v, xs_hbm, gs_hbm, zsem, 'wait')


def _moe_expert_kernel(bg_ref, nb_ref, xs_ref, gs_ref, wg_ref, wu_ref, wd_ref, o_ref):
    i = pl.program_id(0)

    @pl.when(i < nb_ref[0])
    def _():
        grp = bg_ref[i]
        x = xs_ref[...]
        gates = gs_ref[...]
        lane = lax.broadcasted_iota(jnp.int32, gates.shape, 1)
        acc = jnp.zeros(o_ref.shape, F32)
        for e in range(N_EPG):
            a = _dot(x, wg_ref[e].astype(BF16))
            u = _dot(x, wu_ref[e].astype(BF16))
            ge = jnp.sum(jnp.where(lane == N_EPG * grp + e, gates, 0.0), axis=-1, keepdims=True)
            hid = jax.nn.silu(a) * u * ge
            acc = acc + _dot(hid.astype(BF16), wd_ref[e].astype(BF16))
        o_ref[...] = acc.astype(BF16)

    @pl.when(i >= nb_ref[0])
    def _():
        o_ref[...] = jnp.zeros_like(o_ref)


def _moe_combine_kernel(loc_ref, len_ref, off_ref, gate_ref, x1_ref, mod_ref, ln2g_ref, ln2b_ref, o_hbm,
                        ctx_ref, lat_ref, o_v, sem, *, n_ctx_tiles):
    s = pl.program_id(0)
    slot = s % 2

    def copies(buf):
        def o_copy(lr, gr):
            return pltpu.make_async_copy(o_hbm.at[pl.ds(gr, MOE_UNIT), :],
                                         o_v.at[buf, pl.ds(lr, MOE_UNIT), :], sem.at[buf])

        return (lambda lr, gr: o_copy(lr, gr).start()), (lambda lr, gr: o_copy(lr, gr).wait())

    @pl.when(s == 0)
    def _():
        o_v[...] = jnp.zeros_like(o_v)
        _segment_copies(s, loc_ref, len_ref, off_ref, copies(slot)[0])

    @pl.when(s + 1 < pl.num_programs(0))
    def _():
        _segment_copies(s + 1, loc_ref, len_ref, off_ref, copies(1 - slot)[0])

    pt = _perm_t(gate_ref[...], loc_ref, s).astype(BF16)
    _segment_copies(s, loc_ref, len_ref, off_ref, copies(slot)[1])
    f = _dot(pt, o_v[slot])
    g2 = mod_ref[0, :, 5 * D_MODEL:6 * D_MODEL]
    x2 = _norm(ALPHA * x1_ref[...] + g2 * f) * ln2g_ref[...] + ln2b_ref[...]

    @pl.when(s < n_ctx_tiles)
    def _():
        ctx_ref[...] = x2

    @pl.when(s >= n_ctx_tiles)
    def _():
        lat_ref[...] = x2


def _moe_plan(tile_counts, n_blocks):
    cnt = tile_counts[:, 0, N_EXPERTS:N_EXPERTS + N_EGROUPS].astype(jnp.int32)
    len16 = ((cnt + MOE_UNIT - 1) // MOE_UNIT) * MOE_UNIT
    loc = jnp.cumsum(len16, axis=1) - len16
    rows_g = jnp.sum(len16, axis=0)
    reg_g = ((rows_g + MOE_TM - 1) // MOE_TM) * MOE_TM
    reg_start = jnp.cumsum(reg_g) - reg_g
    off = reg_start[None, :] + jnp.cumsum(len16, axis=0) - len16
    blk_end = jnp.cumsum(reg_g // MOE_TM)
    bi = jnp.arange(n_blocks, dtype=jnp.int32)
    blk_group = jnp.minimum(jnp.sum((bi[:, None] >= blk_end[None, :]).astype(jnp.int32), axis=1),
                            N_EGROUPS - 1)
    flat = lambda a: a.reshape(-1).astype(jnp.int32)
    pads = jnp.concatenate([reg_start + rows_g, (reg_g - rows_g) // MOE_UNIT, blk_end[-1:]])
    return (flat(loc), flat(len16), flat(off), flat(pads), blk_group.astype(jnp.int32),
            blk_end[-1:].astype(jnp.int32))


def _moe(h2_all, gates_all, tile_counts, x1_all, mod, w_gate, w_up, w_down, ln2g, ln2b, n_ctx,
         tokens_per_mod_row):
    n_tok = h2_all.shape[0]
    n_tiles = n_tok // MOE_ST
    n_ctx_tiles = n_ctx // MOE_ST
    max_rows = n_tok + n_tiles * N_EGROUPS * (MOE_UNIT - 1) + N_EGROUPS * (MOE_TM - 1)
    n_blocks = -(-max_rows // MOE_TM)
    n_rows = n_blocks * MOE_TM
    loc, len16, off, pads, blk_group, n_used = _moe_plan(tile_counts, n_blocks)

    tile = lambda w: pl.BlockSpec((MOE_ST, w), lambda s, *_: (s, 0))
    anyspec = pl.BlockSpec(memory_space=pl.ANY)
    xs, gs = pl.pallas_call(
        functools.partial(_moe_sort_kernel, n_blocks),
        grid_spec=pltpu.PrefetchScalarGridSpec(
            num_scalar_prefetch=4, grid=(n_tiles,),
            in_specs=[tile(D_MODEL), tile(LANES)],
            out_specs=[anyspec, anyspec],
            scratch_shapes=[pltpu.VMEM((2, MOE_SLOTS, D_MODEL), BF16), pltpu.VMEM((2, MOE_SLOTS, LANES), F32),
                            pltpu.VMEM((MOE_TM, D_MODEL), BF16), pltpu.VMEM((MOE_TM, LANES), F32),
                            pltpu.SemaphoreType.DMA((2, 2)), pltpu.SemaphoreType.DMA((2,))]),
        out_shape=[jax.ShapeDtypeStruct((n_rows, D_MODEL), BF16),
                   jax.ShapeDtypeStruct((n_rows, LANES), F32)],
        compiler_params=_cparams(("arbitrary",)),
        name="moe_sort",
    )(loc, len16, off, pads, h2_all, gates_all)

    blk = lambda w: pl.BlockSpec((MOE_TM, w), lambda i, bg, nb: (jnp.minimum(i, nb[0] - 1), 0))
    wspec = lambda a, b, mode: pl.BlockSpec((N_EPG, a, b), lambda i, bg, nb: (bg[i], 0, 0),
                                            pipeline_mode=mode)
    o_sorted = pl.pallas_call(
        _moe_expert_kernel,
        grid_spec=pltpu.PrefetchScalarGridSpec(
            num_scalar_prefetch=2, grid=(n_blocks,),
            in_specs=[blk(D_MODEL), blk(LANES), wspec(D_MODEL, D_EXPERT, None),
                      wspec(D_MODEL, D_EXPERT, None), wspec(D_EXPERT, D_MODEL, None)],
            out_specs=pl.BlockSpec((MOE_TM, D_MODEL), lambda i, bg, nb: (i, 0))),
        out_shape=jax.ShapeDtypeStruct((n_rows, D_MODEL), BF16),
        compiler_params=_cparams(("arbitrary",)),
        name="moe_experts",
    )(blk_group, n_used, xs, gs, w_gate, w_up, w_down)

    lat_per_row = tokens_per_mod_row // MOE_ST

    def mod_idx(s, *_):
        return (jnp.where(s < n_ctx_tiles, 0, 1 + (s - n_ctx_tiles) // lat_per_row), 0, 0)

    vec = pl.BlockSpec((1, D_MODEL), lambda s, *_: (0, 0))
    return pl.pallas_call(
        functools.partial(_moe_combine_kernel, n_ctx_tiles=n_ctx_tiles),
        grid_spec=pltpu.PrefetchScalarGridSpec(
            num_scalar_prefetch=3, grid=(n_tiles,),
            in_specs=[tile(LANES), tile(D_MODEL), pl.BlockSpec((1, 1, 6 * D_MODEL), mod_idx), vec, vec,
                      anyspec],
            out_specs=[pl.BlockSpec((MOE_ST, D_MODEL), lambda s, *_: (jnp.minimum(s, n_ctx_tiles - 1), 0)),
                       pl.BlockSpec((MOE_ST, D_MODEL), lambda s, *_: (jnp.maximum(s - n_ctx_tiles, 0), 0))],
            scratch_shapes=[pltpu.VMEM((2, MOE_SLOTS, D_MODEL), BF16), pltpu.SemaphoreType.DMA((2,))]),
        out_shape=[jax.ShapeDtypeStruct((n_ctx, D_MODEL), F32),
                   jax.ShapeDtypeStruct((n_tok - n_ctx, D_MODEL), F32)],
        compiler_params=_cparams(("arbitrary",)),
        name="moe_combine",
    )(loc, len16, off, gates_all, x1_all, mod.reshape(mod.shape[0], 1, 6 * D_MODEL), ln2g, ln2b, o_sorted)


def _grid_pos_embed(n_tokens):
    rows = n_tokens // GRID_W
    row = np.repeat(np.arange(rows, dtype=np.float64), GRID_W)
    col = np.tile(np.arange(GRID_W, dtype=np.float64), rows)
    quarter = D_MODEL // 4
    omega = 1.0 / (POS_BASE ** (np.arange(quarter, dtype=np.float64) / quarter))
    er = row[:, None] * omega
    ec = col[:, None] * omega
    return jnp.asarray(np.concatenate([np.sin(er), np.cos(er), np.sin(ec), np.cos(ec)], axis=-1), F32)


def _mixers(x, pos, mod3, h0_re, h0_im, tabs, filt, s5ops, wts, tm):
    bsz, n_tok, _ = x.shape
    shared = mod3.shape[0] == 1
    x3 = x.reshape(1, bsz * n_tok, D_MODEL) if shared else x
    proj_hy, u_s5 = _in_proj(x3, pos, mod3, wts['w_in'], tm)
    y_hy = _hyena(proj_hy.reshape(bsz, n_tok, 3 * D_HY), tabs, filt,
                  wts['hy_conv_w'], wts['hy_conv_b'], wts['hy_fbias'])
    y_s5, f_re, f_im = _s5(u_s5.reshape(bsz, n_tok, D_S5), s5ops, h0_re, h0_im)
    return y_hy.reshape(bsz * n_tok, D_HY), y_s5.reshape(bsz * n_tok, D_S5), f_re, f_im


def kernel(x_prompt, x_sample, state_s5_re, state_s5_im, c, c_ctx, w_ada, b_ada, w_in, hy_conv_w, hy_conv_b, hy_f_w1, hy_f_b1, hy_f_w2, hy_f_b2, hy_f_w3, hy_freq, hy_fbias, s5_a_re, s5_a_im, s5_log_dt, s5_b_re, s5_b_im, s5_c_re, s5_c_im, s5_d, s5_w_glu, s5_b_glu, out_norm_g, w_out, ln1_g, ln1_b, moe_w_r1, moe_b_r1, moe_w_r2, moe_b_r2, moe_w_gate, moe_w_up, moe_w_down, ln2_g, ln2_b):
    b_ctx, l_ctx, _ = x_prompt.shape
    b_lat, l_lat, _ = x_sample.shape
    g, p = S5_GROUPS, S5_STATE
    assert w_ada.shape[0] == 1, "single-layer trunk"
    l = 0

    nrow = 16
    cond = jnp.concatenate([c_ctx[None, :], c, jnp.zeros((nrow - 1 - b_lat, D_MODEL), F32)], axis=0)
    mod = _ada(cond, w_ada[l], b_ada[l])
    mod_ctx = mod[0:1].reshape(1, 1, 6 * D_MODEL)
    mod_lat = mod[1:1 + b_lat].reshape(b_lat, 1, 6 * D_MODEL)

    wr = jnp.concatenate([moe_w_r2[l].transpose(1, 0, 2).reshape(D_MODEL, N_EXPERTS), moe_w_r1[l]], axis=1)
    wr = jnp.pad(wr, ((0, 0), (0, LANES - wr.shape[1])))
    br = jnp.concatenate([moe_b_r2[l].reshape(-1), moe_b_r1[l]])
    br = jnp.pad(br, (0, LANES - br.shape[0])).reshape(1, LANES)
    wr_hi, wr_lo = _split(wr)

    wts = {
        'w_in': w_in[l].astype(BF16), 'hy_conv_w': hy_conv_w[l], 'hy_conv_b': hy_conv_b[l],
        'hy_fbias': hy_fbias[l], 'w_glu': s5_w_glu[l].astype(BF16), 'b_glu': s5_b_glu[l].reshape(1, -1),
        'out_norm_g': out_norm_g[l].reshape(1, -1), 'w_out': w_out[l].astype(BF16),
        'ln1_g': ln1_g[l].reshape(1, -1), 'ln1_b': ln1_b[l].reshape(1, -1),
        'wr_hi': wr_hi, 'wr_lo': wr_lo, 'br': br,
        'w_gate': moe_w_gate[l], 'w_up': moe_w_up[l], 'w_down': moe_w_down[l],
        'ln2_g': ln2_g[l].reshape(1, -1), 'ln2_b': ln2_b[l].reshape(1, -1),
    }

    s5ops = _s5_operators(s5_a_re[l], s5_a_im[l], s5_log_dt[l], s5_b_re[l], s5_b_im[l],
                          s5_c_re[l], s5_c_im[l], s5_d[l])
    tabs_ctx = _tables(l_ctx)
    tabs_lat = _tables(l_lat)
    filt_args = (hy_f_w1[l], hy_f_b1[l], hy_f_w2[l], hy_f_b2[l], hy_f_w3[l], hy_freq[l])
    filt_ctx = _hyena_filters(l_ctx, tabs_ctx, *filt_args)
    filt_lat = _hyena_filters(l_lat, tabs_lat, *filt_args)

    zero = jnp.zeros((g, b_ctx, 2 * p), F32)
    yhy_c, ys5_c, f_re, f_im = _mixers(x_prompt, None, mod_ctx, zero, zero, tabs_ctx, filt_ctx, s5ops, wts, 512)
    unpack = lambda f: f.reshape(g, b_ctx, 2, p).transpose(1, 2, 0, 3)[:, None]
    new_re, new_im = unpack(f_re), unpack(f_im)

    pack = lambda s: s[:, l].transpose(2, 0, 1, 3).reshape(g, b_lat, 2 * p)
    pos = _grid_pos_embed(l_lat)
    yhy_l, ys5_l, _, _ = _mixers(x_sample, pos, mod_lat, pack(state_s5_re), pack(state_s5_im),
                                 tabs_lat, filt_lat, s5ops, wts, 512)

    n_ctx = b_ctx * l_ctx
    x1_all, h2_all, gates_all, tile_counts = _out_proj(
        x_prompt.reshape(n_ctx, D_MODEL), x_sample.reshape(b_lat * l_lat, D_MODEL), pos,
        yhy_c, yhy_l, ys5_c, ys5_l, mod, wts['w_glu'], wts['b_glu'], wts['out_norm_g'], wts['w_out'],
        wts['ln1_g'], wts['ln1_b'], wts['wr_hi'], wts['wr_lo'], wts['br'], MOE_ST)
    y_ctx, y_lat = _moe(h2_all, gates_all, tile_counts, x1_all, mod,
                        wts['w_gate'], wts['w_up'], wts['w_down'], wts['ln2_g'], wts['ln2_b'],
                        n_ctx, l_lat)
    return (y_ctx.reshape(x_prompt.shape), y_lat.reshape(x_sample.shape), new_re, new_im)
```

```python
import functools
import math

import numpy as np
import jax
import jax.numpy as jnp
from jax import lax
from jax.experimental import pallas as pl
from jax.experimental.pallas import tpu as pltpu

F32 = jnp.float32
BF16 = jnp.bfloat16

D_MODEL = 1024
DEPTH = 1
GRID_W = 64
POS_BASE = 10000.0
D_HY = 512
D_S5 = 512
S5_CH = 16
S5_GROUPS = 32
S5_STATE = 64
S5_CHUNK = 16
S5_ROW = S5_CHUNK * S5_CH
HY_BANDS = 16
HY_EMB = 1 + 2 * HY_BANDS
HY_HID = 64
HY_MIN_DECAY = math.log(1e-2) / 1.5
HY_MAX_DECAY = math.log(1e-2) / 0.3
N_EGROUPS = 4
N_EPG = 4
N_EXPERTS = 16
D_EXPERT = 512
LN_EPS = 1e-5
ALPHA = (2.0 * DEPTH) ** 0.25
LANES = 128
S5_GB = LANES // S5_CH
S5OPS_GB = 4
HY_CW = 512
MOE_ST = 512
MOE_SLOTS = 640
MOE_UNIT = 16
MOE_TM = 512
VMEM_LIMIT = 60000 * 1024


def _cparams(sem):
    return pltpu.CompilerParams(dimension_semantics=sem, vmem_limit_bytes=VMEM_LIMIT)


def _split(x):
    hi = x.astype(BF16)
    lo = (x - hi.astype(F32)).astype(BF16)
    return hi, lo


def _dot(a, b):
    return jnp.dot(a, b, preferred_element_type=F32)


def _dot_t(a, b):
    return lax.dot_general(a, b, (((1,), (1,)), ((), ())), preferred_element_type=F32)


def _mm3(a, b):
    ah, al = _split(a)
    bh, bl = _split(b)
    return _dot(ah, bh) + _dot(al, bh) + _dot(ah, bl)


def _mm3_t(a, b):
    ah, al = _split(a)
    bh, bl = _split(b)
    return _dot_t(ah, bh) + _dot_t(al, bh) + _dot_t(ah, bl)


def _norm(x):
    xc = x - jnp.mean(x, axis=-1, keepdims=True)
    return xc * lax.rsqrt(jnp.mean(xc * xc, axis=-1, keepdims=True) + LN_EPS)


def _rms(y):
    return y * lax.rsqrt(jnp.mean(y * y, axis=-1, keepdims=True) + LN_EPS)


def _ada_kernel(cond_ref, w_ref, b_ref, o_ref):
    c = jax.nn.silu(cond_ref[...])
    o_ref[...] = _mm3(c, w_ref[...]) + b_ref[...]


def _ada(cond, w_ada, b_ada):
    nb = cond.shape[0]
    n = w_ada.shape[1]
    tn = 1024
    return pl.pallas_call(
        _ada_kernel,
        grid=(n // tn,),
        in_specs=[pl.BlockSpec((nb, D_MODEL), lambda j: (0, 0)),
                  pl.BlockSpec((D_MODEL, tn), lambda j: (0, j)),
                  pl.BlockSpec((1, tn), lambda j: (0, j))],
        out_specs=pl.BlockSpec((nb, tn), lambda j: (0, j)),
        out_shape=jax.ShapeDtypeStruct((nb, n), F32),
        compiler_params=_cparams(("arbitrary",)),
        name="ada",
    )(cond, w_ada, b_ada.reshape(1, n))


def _dft_tables(n_half):
    n = 2 * n_half
    idx = np.arange(n_half, dtype=np.int64)
    m = (idx[:, None] * idx[None, :]) % n
    ang = 2.0 * np.pi * m.astype(np.float64) / n
    cm = np.cos(ang)
    sm = -np.sin(ang)
    sm[0, :] = 1.0 - 2.0 * (idx % 2)
    return cm.astype(np.float32), sm.astype(np.float32)


def _tables(n_tok):
    n_half = n_tok // 2
    cm, sm = _dft_tables(n_half)
    mats = tuple(jnp.asarray(t).astype(BF16) for t in (cm, sm, np.ascontiguousarray(sm.T)))
    ang = np.pi * np.arange(n_half, dtype=np.float64) / n_tok
    tw = [np.broadcast_to(v[:, None], (n_half, HY_CW)).astype(np.float32) for v in (np.cos(ang), -np.sin(ang))]
    return mats + (jnp.asarray(tw[0]), jnp.asarray(tw[1]))


def _put_cols(ref, x):
    for j in range(ref.shape[0]):
        ref[j] = x[:, LANES * j:LANES * (j + 1)]


def _get_cols(ref):
    return jnp.concatenate([ref[j] for j in range(ref.shape[0])], axis=1)


def _get_parity(ref, parity):
    n_half = ref.shape[1] // 2
    return jnp.concatenate([ref[j, pl.ds(parity, n_half, stride=2), :] for j in range(ref.shape[0])], axis=1)


def _put_parity(ref, parity, x):
    n_half = ref.shape[1] // 2
    for j in range(ref.shape[0]):
        ref[j, pl.ds(parity, n_half, stride=2), :] = x[:, LANES * j:LANES * (j + 1)]


def _set_row0(x, v):
    first = lax.broadcasted_iota(jnp.int32, (8, x.shape[1]), 0) == 0
    return jnp.concatenate([jnp.where(first, v, x[:8]), x[8:]], axis=0)


def _rfft_packed(x_ref, cm, sm, tw_re, tw_im):
    xe = _get_parity(x_ref, 0).astype(BF16)
    xo = _get_parity(x_ref, 1).astype(BF16)
    e_re, e_im = _dot(cm, xe), _dot(sm, xe)
    o_re, o_im = _dot(cm, xo), _dot(sm, xo)
    t_re = tw_re * o_re - tw_im * o_im
    t_im = tw_re * o_im + tw_im * o_re
    a_im = _set_row0(e_im + t_im, e_im[0:1])
    b_im = _set_row0(t_im - e_im, -o_im[0:1])
    return e_re + t_re, a_im, e_re - t_re, b_im


def _irfft_packed(y_ref, ya_re, ya_im, yb_re, yb_im, cm, st, tw_re, tw_im):
    p_e = ya_re + yb_re
    q_e = _set_row0(ya_im - yb_im, ya_im[0:1])
    _put_parity(y_ref, 0, _dot(cm, p_e.astype(BF16)) + _dot(st, q_e.astype(BF16)))
    ra_re = ya_re * tw_re + ya_im * tw_im
    ra_im = ya_im * tw_re - ya_re * tw_im
    rb_re = yb_im * tw_im - yb_re * tw_re
    rb_im = -(yb_re * tw_im + yb_im * tw_re)
    p_o = ra_re + rb_re
    q_o = _set_row0(ra_im - rb_im, -yb_im[0:1])
    _put_parity(y_ref, 1, _dot(cm, p_o.astype(BF16)) + _dot(st, q_o.astype(BF16)))


def _filt_kernel(n_tok, z_ref, t_ref, w1_ref, b1_ref, w2_ref, b2_ref, fr_ref, w3f_ref, w3b_ref,
                 dl_ref, cm_ref, sm_ref, twr_ref, twi_ref, kar_ref, kai_ref, kbr_ref, kbi_ref, p_ref, q_ref):
    fr = fr_ref[...]
    h = jnp.sin(fr * (_mm3(z_ref[...], w1_ref[...]) + b1_ref[...]))
    h = jnp.sin(fr * (_mm3(h, w2_ref[...]) + b2_ref[...]))
    decay = jnp.exp(-t_ref[...] * dl_ref[...])
    row = lax.broadcasted_iota(jnp.int32, decay.shape, 0)
    hf = _mm3(h, w3f_ref[...]) * decay
    hb = jnp.where(row == 0, 0.0, _mm3(h, w3b_ref[...]) * decay)
    _put_cols(p_ref, hf + hb)
    _put_cols(q_ref, hf - hb)
    cm, sm, tw_re, tw_im = cm_ref[...], sm_ref[...], twr_ref[...], twi_ref[...]
    row0 = lax.broadcasted_iota(jnp.int32, tw_re.shape, 0) == 0
    pa_re, pa_im, pb_re, _ = _rfft_packed(p_ref, cm, sm, tw_re, tw_im)
    _, qa_im, _, qb_im = _rfft_packed(q_ref, cm, sm, tw_re, tw_im)
    inv_n = 1.0 / (2 * n_tok)
    w_re = jnp.where(row0, inv_n, 2.0 * inv_n)
    kar_ref[...] = w_re * pa_re
    kbr_ref[...] = w_re * pb_re
    kai_ref[...] = (2.0 * inv_n) * _set_row0(qa_im, pa_im[0:1])
    kbi_ref[...] = (2.0 * inv_n) * qb_im


def _hyena_filters(n_tok, tabs, hy_f_w1, hy_f_b1, hy_f_w2, hy_f_b2, hy_f_w3, hy_freq):
    cm, sm, _, tw_re, tw_im = tabs
    n_half = n_tok // 2
    t = jnp.linspace(0.0, 1.0, n_tok, dtype=F32)[:, None]
    wv = 2.0 * math.pi * jnp.arange(n_tok, dtype=F32) / n_tok
    fb = jnp.linspace(1e-4, HY_BANDS - 1, HY_BANDS, dtype=F32)
    ang = wv[:, None] * fb[None, :]
    z = jnp.concatenate([t, jnp.cos(ang), -jnp.sin(ang)], axis=-1)
    z = jnp.pad(z, ((0, 0), (0, LANES - HY_EMB)))
    w1 = jnp.pad(hy_f_w1, ((0, LANES - HY_EMB), (0, 0)))
    deltas = jnp.abs(jnp.linspace(HY_MIN_DECAY, HY_MAX_DECAY, D_HY, dtype=F32))[None, :]
    ncb = D_HY // HY_CW
    full = lambda j: (0, 0)
    out_sd = jax.ShapeDtypeStruct((n_half, 2 * D_HY), F32)
    mat = pl.BlockSpec((n_half, n_half), full, pipeline_mode=pl.Buffered(1))
    twb = pl.BlockSpec((n_half, HY_CW), full, pipeline_mode=pl.Buffered(1))
    return pl.pallas_call(
        functools.partial(_filt_kernel, n_tok),
        grid=(2 * ncb,),
        in_specs=[pl.BlockSpec((n_tok, LANES), full),
                  pl.BlockSpec((n_tok, 1), full),
                  pl.BlockSpec((LANES, HY_HID), full),
                  pl.BlockSpec((1, HY_HID), full),
                  pl.BlockSpec((HY_HID, HY_HID), full),
                  pl.BlockSpec((1, HY_HID), full),
                  pl.BlockSpec((1, HY_HID), full),
                  pl.BlockSpec((HY_HID, HY_CW), lambda j: (0, 2 * ncb * (j // ncb) + j % ncb)),
                  pl.BlockSpec((HY_HID, HY_CW), lambda j: (0, 2 * ncb * (j // ncb) + ncb + j % ncb)),
                  pl.BlockSpec((1, HY_CW), lambda j: (0, j % ncb)),
                  mat, mat, twb, twb],
        out_specs=[pl.BlockSpec((n_half, HY_CW), lambda j: (0, j))] * 4,
        out_shape=[out_sd] * 4,
        scratch_shapes=[pltpu.VMEM((HY_CW // LANES, n_tok, LANES), F32)] * 2,
        compiler_params=_cparams(("arbitrary",)),
        name=f"filt{n_tok}",
    )(z, t, w1, hy_f_b1.reshape(1, -1), hy_f_w2, hy_f_b2.reshape(1, -1), hy_freq.reshape(1, -1),
      hy_f_w3, hy_f_w3, deltas, cm, sm, tw_re, tw_im)


def _hyena_kernel(pv_ref, p1_ref, p2_ref, cwv_ref, cw1_ref, cw2_ref, cbv_ref, cb1_ref, cb2_ref,
                  fbias_ref, cm_ref, sm_ref, st_ref, twr_ref, twi_ref,
                  kar0_ref, kai0_ref, kbr0_ref, kbi0_ref, kar1_ref, kai1_ref, kbr1_ref, kbi1_ref,
                  o_ref, u_ref, y_ref):
    n_tok = pv_ref.shape[1]
    row = lax.broadcasted_iota(jnp.int32, (n_tok, pv_ref.shape[2]), 0)

    def short_conv(p_ref, cw_ref, cb_ref):
        p = p_ref[0]
        prev = jnp.where(row == 0, 0.0, pltpu.roll(p, 1, axis=0))
        nxt = jnp.where(row == n_tok - 1, 0.0, pltpu.roll(p, n_tok - 1, axis=0))
        return cb_ref[...] + prev * cw_ref[0:1, :] + p * cw_ref[1:2, :] + nxt * cw_ref[2:3, :]

    cm, sm, st, tw_re, tw_im = cm_ref[...], sm_ref[...], st_ref[...], twr_ref[...], twi_ref[...]

    def fftconv(u, kar_ref, kai_ref, kbr_ref, kbi_ref, skip):
        _put_cols(u_ref, u)
        ua_re, ua_im, ub_re, ub_im = _rfft_packed(u_ref, cm, sm, tw_re, tw_im)
        ka_re, ka_im, kb_re, kb_im = kar_ref[...], kai_ref[...], kbr_ref[...], kbi_ref[...]
        zero_row = jnp.zeros_like(ka_im[0:1])
        kaz = _set_row0(ka_im, zero_row)
        kbz = _set_row0(kb_im, zero_row)
        ya_re = ua_re * ka_re - ua_im * kaz
        yb_re = ub_re * kb_re - ub_im * kbz
        h_re = ua_im[0:1] * ka_im[0:1] - ub_im[0:1] * kb_im[0:1]
        h_im = ua_im[0:1] * kb_im[0:1] + ub_im[0:1] * ka_im[0:1]
        ya_im = _set_row0(ua_re * ka_im + ua_im * ka_re, h_re)
        yb_im = _set_row0(ub_re * kb_im + ub_im * kb_re, h_im)
        _irfft_packed(y_ref, ya_re, ya_im, yb_re, yb_im, cm, st, tw_re, tw_im)
        return _get_cols(y_ref) + u * skip

    v = short_conv(pv_ref, cwv_ref, cbv_ref)
    x1 = short_conv(p1_ref, cw1_ref, cb1_ref)
    z = x1 * fftconv(v, kar0_ref, kai0_ref, kbr0_ref, kbi0_ref, fbias_ref[0:1, :])
    x2 = short_conv(p2_ref, cw2_ref, cb2_ref)
    o_ref[0] = x2 * fftconv(z, kar1_ref, kai1_ref, kbr1_ref, kbi1_ref, fbias_ref[1:2, :])


def _hyena(proj_hy, tabs, filt, hy_conv_w, hy_conv_b, hy_fbias):
    bsz, n_tok, _ = proj_hy.shape
    n_half = n_tok // 2
    ncb = D_HY // HY_CW
    cm, sm, st, tw_re, tw_im = tabs
    cb = hy_conv_b.reshape(1, -1)
    const = lambda shape: pl.BlockSpec(shape, lambda b, c: (0, 0), pipeline_mode=pl.Buffered(1))
    mat = const((n_half, n_half))
    twb = const((n_half, HY_CW))

    def pspec(k):
        return pl.BlockSpec((1, n_tok, HY_CW), lambda b, c: (b, 0, k * ncb + c))

    def cwspec(k):
        return pl.BlockSpec((3, HY_CW), lambda b, c: (0, k * ncb + c))

    def cbspec(k):
        return pl.BlockSpec((1, HY_CW), lambda b, c: (0, k * ncb + c))

    def fspec(o):
        mode = pl.Buffered(1) if ncb == 1 else None
        return pl.BlockSpec((n_half, HY_CW), lambda b, c: (0, o * ncb + c), pipeline_mode=mode)

    return pl.pallas_call(
        _hyena_kernel,
        grid=(bsz, ncb),
        in_specs=[pspec(0), pspec(1), pspec(2), cwspec(0), cwspec(1), cwspec(2),
                  cbspec(0), cbspec(1), cbspec(2),
                  pl.BlockSpec((2, HY_CW), lambda b, c: (0, c)),
                  mat, mat, mat, twb, twb] + [fspec(0)] * 4 + [fspec(1)] * 4,
        out_specs=pl.BlockSpec((1, n_tok, HY_CW), lambda b, c: (b, 0, c)),
        out_shape=jax.ShapeDtypeStruct((bsz, n_tok, D_HY), F32),
        scratch_shapes=[pltpu.VMEM((HY_CW // LANES, n_tok, LANES), F32)] * 2,
        compiler_params=_cparams(("arbitrary", "arbitrary")),
        name=f"hyena{n_tok}",
    )(proj_hy, proj_hy, proj_hy, hy_conv_w, hy_conv_w, hy_conv_w, cb, cb, cb, hy_fbias,
      cm, sm, st, tw_re, tw_im, *filt, *filt)


def _s5ops_kernel(*refs):
    for g in range(S5OPS_GB):
        _s5ops_group(g, *refs)


def _s5ops_group(g, are_ref, aim_ref, ldt_ref, btr_ref, bti_ref, cre_ref, cim_ref, d_ref,
                 mt_ref, erh_ref, erl_ref, eih_ref, eil_ref, gr_ref, gi_ref, atr_ref, ati_ref,
                 er_ref, ei_ref):
    a_re, a_im = are_ref[g], aim_ref[g]
    dt = jnp.exp(ldt_ref[g])
    mag = jnp.exp(a_re * dt)
    ab_re = mag * jnp.cos(a_im * dt)
    ab_im = mag * jnp.sin(a_im * dt)
    n_re, n_im = ab_re - 1.0, ab_im
    den = a_re * a_re + a_im * a_im
    q_re = (n_re * a_re + n_im * a_im) / den
    q_im = (n_im * a_re - n_re * a_im) / den
    bt_re, bt_im = btr_ref[g], bti_ref[g]
    bb_re = q_re * bt_re - q_im * bt_im
    bb_im = q_re * bt_im + q_im * bt_re
    c_re, c_im = cre_ref[g, 0:S5_CH, :], cim_ref[g, 0:S5_CH, :]
    pw = [(jnp.ones_like(ab_re), jnp.zeros_like(ab_re))]
    for _ in range(S5_CHUNK):
        pr, pi = pw[-1]
        pw.append((pr * ab_re - pi * ab_im, pr * ab_im + pi * ab_re))
    lane = lax.broadcasted_iota(jnp.int32, ab_re.shape, 1)
    fwd = lane < S5_STATE
    for s in range(S5_CHUNK):
        e_re = jnp.where(fwd, pw[S5_CHUNK - 1 - s][0], pw[s][0])
        e_im = jnp.where(fwd, pw[S5_CHUNK - 1 - s][1], pw[s][1])
        er_ref[g, pl.ds(S5_CH * s, S5_CH), :] = e_re * bb_re - e_im * bb_im
        ei_ref[g, pl.ds(S5_CH * s, S5_CH), :] = e_re * bb_im + e_im * bb_re
        g_re = jnp.where(fwd, pw[s + 1][0], pw[S5_CHUNK - s][0])
        g_im = jnp.where(fwd, pw[s + 1][1], pw[S5_CHUNK - s][1])
        gr_ref[g, pl.ds(S5_CH * s, S5_CH), :] = (c_re * g_re - c_im * g_im).astype(BF16)
        gi_ref[g, pl.ds(S5_CH * s, S5_CH), :] = (-(c_re * g_im + c_im * g_re)).astype(BF16)
    atr_ref[g] = pw[S5_CHUNK][0]
    ati_ref[g] = pw[S5_CHUNK][1]
    er, ei = er_ref[g], ei_ref[g]
    erh_ref[g], erl_ref[g] = _split(er)
    eih_ref[g], eil_ref[g] = _split(ei)
    lane2 = lax.broadcasted_iota(jnp.int32, er.shape, 1)
    row2 = lax.broadcasted_iota(jnp.int32, er.shape, 0)
    f2 = lane2 < S5_STATE
    zero = jnp.zeros_like(er)

    cp_re, cp_im = cre_ref[g], cim_ref[g]
    kf = _mm3_t(jnp.where(f2, er, zero), cp_re) - _mm3_t(jnp.where(f2, ei, zero), cp_im)
    kb = _mm3_t(jnp.where(f2, zero, er), cp_re) - _mm3_t(jnp.where(f2, zero, ei), cp_im)
    d_row = d_ref[g]
    steps_per_vreg = LANES // S5_CH
    for half in range(S5_CHUNK // steps_per_vreg):
        acc = zero
        for tt in range(steps_per_vreg):
            t = half * steps_per_vreg + tt
            nf = S5_CH * (S5_CHUNK - 1 - t)
            nb = S5_CH * t
            col_f = jnp.concatenate([kf[nf:], zero[:nf]], axis=0) if nf else kf
            col_b = jnp.concatenate([zero[:nb], kb[:S5_ROW - nb]], axis=0) if nb else kb
            diag = jnp.where((row2 // S5_CH == t) & (row2 % S5_CH == lane2), d_row, 0.0)
            col = col_f + col_b + diag
            r = pltpu.roll(col, S5_CH * tt, axis=1) if tt else col
            acc = jnp.where((lane2 >= S5_CH * tt) & (lane2 < S5_CH * (tt + 1)), r, acc)
        mt_ref[g, :, LANES * half:LANES * (half + 1)] = acc.astype(BF16)


def _s5_operators(s5_a_re, s5_a_im, s5_log_dt, s5_b_re, s5_b_im, s5_c_re, s5_c_im, s5_d):
    g, p, h = S5_GROUPS, S5_STATE, S5_CH
    cat = lambda x: jnp.concatenate([x[0], x[1]], axis=-1)
    a_re = cat(s5_a_re).reshape(g, 1, 2 * p)
    a_im = cat(s5_a_im).reshape(g, 1, 2 * p)
    ldt = cat(jnp.broadcast_to(s5_log_dt[:, :, None], (2, g, p))).reshape(g, 1, 2 * p)
    bt_re = cat(jnp.swapaxes(s5_b_re, -1, -2))
    bt_im = cat(jnp.swapaxes(s5_b_im, -1, -2))
    cpad = lambda c: jnp.pad(jnp.concatenate([c, c], axis=-1), ((0, 0), (0, LANES - h), (0, 0)))
    c_re, c_im = cpad(s5_c_re), cpad(s5_c_im)
    d_row = jnp.pad(s5_d.reshape(g, 1, h), ((0, 0), (0, 0), (0, LANES - h)))
    vec = pl.BlockSpec((S5OPS_GB, 1, 2 * p), lambda i: (i, 0, 0))
    hp = pl.BlockSpec((S5OPS_GB, h, 2 * p), lambda i: (i, 0, 0))
    sq = pl.BlockSpec((S5OPS_GB, LANES, 2 * p), lambda i: (i, 0, 0))
    big = pl.BlockSpec((S5OPS_GB, S5_ROW, 2 * p), lambda i: (i, 0, 0))
    mts = pl.BlockSpec((S5OPS_GB, S5_ROW, S5_ROW), lambda i: (i, 0, 0))
    big_sd = jax.ShapeDtypeStruct((g, S5_ROW, 2 * p), BF16)
    vec_sd = jax.ShapeDtypeStruct((g, 1, 2 * p), F32)
    return pl.pallas_call(
        _s5ops_kernel,
        grid=(g // S5OPS_GB,),
        in_specs=[vec, vec, vec, hp, hp, sq, sq, vec],
        out_specs=[mts, big, big, big, big, big, big, vec, vec],
        out_shape=[jax.ShapeDtypeStruct((g, S5_ROW, S5_ROW), BF16)] + [big_sd] * 6 + [vec_sd, vec_sd],
        scratch_shapes=[pltpu.VMEM((S5OPS_GB, S5_ROW, 2 * p), F32)] * 2,
        compiler_params=_cparams(("arbitrary",)),
        name="s5ops",
    )(a_re, a_im, ldt, bt_re, bt_im, c_re, c_im, d_row)


def _block_transpose(xs):
    n = len(xs)
    lane = lax.broadcasted_iota(jnp.int32, xs[0].shape, 1)
    xs = list(xs)
    d = n // 2
    while d:
        keep = ((lane // S5_CH) & d) == 0
        for i in range(n):
            if i & d:
                continue
            lo, hi = xs[i], xs[i + d]
            xs[i] = jnp.where(keep, lo, pltpu.roll(hi, S5_CH * d, axis=1))
            xs[i + d] = jnp.where(keep, pltpu.roll(lo, LANES - S5_CH * d, axis=1), hi)
        d //= 2
    return xs


def _s5_kernel(bsz, n_chunks, u_ref, mt_ref, erh_ref, erl_ref, eih_ref, eil_ref, gr_ref, gi_ref,
               atr_ref, ati_ref, h0r_ref, h0i_ref, y_ref, fr_ref, fi_ref,
               ua_ref, ub_ref, ya_ref, yb_ref, sr_ref, si_ref, xfr_ref, xfi_ref, xbr_ref, xbi_ref):
    nc = n_chunks
    spv = LANES // S5_CH
    rsub = min(nc, 32)

    def to_chunks(b, carry):
        for half, dst in ((0, ua_ref), (1, ub_ref)):
            for r0 in range(0, nc, rsub):
                xs = [u_ref[b, pl.ds(S5_CHUNK * r0 + half * spv + tt, rsub, stride=S5_CHUNK), :]
                      for tt in range(spv)]
                for k, blk in enumerate(_block_transpose(xs)):
                    dst[k, pl.ds(r0 * bsz + b, rsub, stride=bsz), :] = blk
        return carry

    lax.fori_loop(0, bsz, to_chunks, 0, unroll=2)

    lane = lax.broadcasted_iota(jnp.int32, (bsz, 2 * S5_STATE), 1)
    fwd = lane < S5_STATE
    lane_all = lax.broadcasted_iota(jnp.int32, (bsz * nc, 2 * S5_STATE), 1)
    fwd_all = lane_all < S5_STATE

    def group(k, slot):
        u = jnp.concatenate([ua_ref[k], ub_ref[k]], axis=1)
        uh, ul = _split(u)
        sr_ref[slot] = _dot(uh, erh_ref[k]) + _dot(ul, erh_ref[k]) + _dot(uh, erl_ref[k])
        si_ref[slot] = _dot(uh, eih_ref[k]) + _dot(ul, eih_ref[k]) + _dot(uh, eil_ref[k])
        at_re, at_im = atr_ref[k], ati_ref[k]
        y_intra = _dot(uh, mt_ref[k])

        def step(i, xc):
            x_re, x_im = xc
            rf = pl.ds(pl.multiple_of(i * bsz, bsz), bsz)
            rb = pl.ds(pl.multiple_of((nc - 1 - i) * bsz, bsz), bsz)
            xfr_ref[slot, rf, :] = x_re
            xfi_ref[slot, rf, :] = x_im
            xbr_ref[slot, rb, :] = x_re
            xbi_ref[slot, rb, :] = x_im
            s_re = jnp.where(fwd, sr_ref[slot, rf, :], sr_ref[slot, rb, :])
            s_im = jnp.where(fwd, si_ref[slot, rf, :], si_ref[slot, rb, :])
            return (at_re * x_re - at_im * x_im + s_re, at_re * x_im + at_im * x_re + s_im)

        x_re, x_im = lax.fori_loop(0, nc, step, (h0r_ref[k], h0i_ref[k]), unroll=True)
        fr_ref[k] = x_re
        fi_ref[k] = x_im
        xp_re = jnp.where(fwd_all, xfr_ref[slot], xbr_ref[slot]).astype(BF16)
        xp_im = jnp.where(fwd_all, xfi_ref[slot], xbi_ref[slot]).astype(BF16)
        y = y_intra + _dot_t(xp_re, gr_ref[k]) + _dot_t(xp_im, gi_ref[k])
        ya_ref[k] = y[:, :LANES]
        yb_ref[k] = y[:, LANES:]

    def group_pair(j, carry):
        group(2 * j, 0)
        group(2 * j + 1, 1)
        return carry

    lax.fori_loop(0, S5_GB // 2, group_pair, 0)

    def to_tokens(b, carry):
        for half, src in ((0, ya_ref), (1, yb_ref)):
            for r0 in range(0, nc, rsub):
                ys = [src[k, pl.ds(r0 * bsz + b, rsub, stride=bsz), :] for k in range(S5_GB)]
                for tt, blk in enumerate(_block_transpose(ys)):
                    y_ref[b, pl.ds(S5_CHUNK * r0 + half * spv + tt, rsub, stride=S5_CHUNK), :] = blk
        return carry

    lax.fori_loop(0, bsz, to_tokens, 0, unroll=2)


def _s5(u, ops, h0_re, h0_im):
    bsz, n_tok, _ = u.shape
    g, p = S5_GROUPS, S5_STATE
    nc = n_tok // S5_CHUNK
    rows = nc * bsz
    tok = pl.BlockSpec((bsz, n_tok, LANES), lambda j: (0, 0, j))
    gspec = lambda shape: pl.BlockSpec((S5_GB,) + shape, lambda j: (j, 0, 0))
    op = gspec((S5_ROW, 2 * p))
    return pl.pallas_call(
        functools.partial(_s5_kernel, bsz, nc),
        grid=(g // S5_GB,),
        in_specs=[tok, gspec((S5_ROW, S5_ROW)), op, op, op, op, op, op,
                  gspec((1, 2 * p)), gspec((1, 2 * p)), gspec((bsz, 2 * p)), gspec((bsz, 2 * p))],
        out_specs=[tok, gspec((bsz, 2 * p)), gspec((bsz, 2 * p))],
        out_shape=[jax.ShapeDtypeStruct((bsz, n_tok, D_S5), F32),
                   jax.ShapeDtypeStruct((g, bsz, 2 * p), F32),
                   jax.ShapeDtypeStruct((g, bsz, 2 * p), F32)],
        scratch_shapes=([pltpu.VMEM((S5_GB, rows, LANES), F32)] * 4
                        + [pltpu.VMEM((2, rows, 2 * p), F32)] * 6),
        compiler_params=_cparams(("arbitrary",)),
        name=f"s5_{n_tok}",
    )(u, *ops, h0_re, h0_im)


def _in_kernel(has_pos, *refs):
    if has_pos:
        x_ref, pos_ref, mod_ref, w_ref, hy_ref, s5_ref = refs
        x = x_ref[0] + pos_ref[...]
    else:
        x_ref, mod_ref, w_ref, hy_ref, s5_ref = refs
        x = x_ref[0]
    sh1 = mod_ref[0, :, 0:D_MODEL]
    sc1 = mod_ref[0, :, D_MODEL:2 * D_MODEL]
    h = _norm(x) * (1.0 + sc1) + sh1
    proj = _dot(h.astype(BF16), w_ref[...].astype(BF16))
    hy_ref[0] = proj[:, :3 * D_HY]
    s5_ref[0] = proj[:, 3 * D_HY:]


def _in_proj(x3, pos, mod3, w_in, tm):
    nb, lt, _ = x3.shape
    has_pos = pos is not None
    per_batch = mod3.shape[0] > 1
    midx = (lambda b, i: (b, 0, 0)) if per_batch else (lambda b, i: (0, 0, 0))
    in_specs = [pl.BlockSpec((1, tm, D_MODEL), lambda b, i: (b, i, 0))]
    args = [x3]
    if has_pos:
        in_specs.append(pl.BlockSpec((tm, D_MODEL), lambda b, i: (i, 0)))
        args.append(pos)
    in_specs += [pl.BlockSpec((1, 1, 6 * D_MODEL), midx),
                 pl.BlockSpec((D_MODEL, 3 * D_HY + D_S5), lambda b, i: (0, 0), pipeline_mode=pl.Buffered(1))]
    args += [mod3, w_in]
    return pl.pallas_call(
        functools.partial(_in_kernel, has_pos),
        grid=(nb, lt // tm),
        in_specs=in_specs,
        out_specs=[pl.BlockSpec((1, tm, 3 * D_HY), lambda b, i: (b, i, 0)),
                   pl.BlockSpec((1, tm, D_S5), lambda b, i: (b, i, 0))],
        out_shape=[jax.ShapeDtypeStruct((nb, lt, 3 * D_HY), F32),
                   jax.ShapeDtypeStruct((nb, lt, D_S5), F32)],
        compiler_params=_cparams(("arbitrary", "arbitrary")),
        name=f"in_proj{nb}",
    )(*args)


def _route(logits):
    lane = lax.broadcasted_iota(jnp.int32, logits.shape, 1)
    lane_f = lane.astype(F32)
    neg = -jnp.inf
    big = float(LANES)
    m1 = (lane >= N_EXPERTS) & (lane < N_EXPERTS + N_EGROUPS)
    l1 = jnp.where(m1, logits, neg)
    top1 = jnp.max(l1, axis=-1, keepdims=True)
    grp = jnp.min(jnp.where(l1 == top1, lane_f, big), axis=-1, keepdims=True) - float(N_EXPERTS)
    den = jnp.sum(jnp.where(m1, jnp.exp(logits - top1), 0.0), axis=-1, keepdims=True)
    p_grp = 1.0 / den
    lo = grp * float(N_EPG)
    m2 = (lane_f >= lo) & (lane_f < lo + float(N_EPG))
    l2 = jnp.where(m2, logits, neg)
    v1 = jnp.max(l2, axis=-1, keepdims=True)
    i1 = jnp.min(jnp.where(l2 == v1, lane_f, big), axis=-1, keepdims=True)
    l2b = jnp.where(lane_f == i1, neg, l2)
    v2 = jnp.max(l2b, axis=-1, keepdims=True)
    i2 = jnp.min(jnp.where(l2b == v2, lane_f, big), axis=-1, keepdims=True)
    e = jnp.exp(v2 - v1)
    w1 = 1.0 / (1.0 + e)
    w2 = e / (1.0 + e)
    gates = jnp.where(lane_f == i1, w1 * p_grp, 0.0) + jnp.where(lane_f == i2, w2 * p_grp, 0.0)
    return jnp.where(lane_f == grp + float(N_EXPERTS), 1.0, gates)


def _out_kernel(n_ctx_blocks, xc_ref, xl_ref, pos_ref, yhyc_ref, yhyl_ref, ys5c_ref, ys5l_ref, mod_ref,
                wglu_ref, bglu_ref, ong_ref, wout_ref, ln1g_ref, ln1b_ref, wrh_ref, wrl_ref, br_ref,
                x1_ref, h2_ref, gate_ref, cnt_ref):
    is_ctx = pl.program_id(0) < n_ctx_blocks
    x = jnp.where(is_ctx, xc_ref[...], xl_ref[...] + pos_ref[...])
    y = jnp.where(is_ctx, ys5c_ref[...], ys5l_ref[...])
    y_hy = jnp.where(is_ctx, yhyc_ref[...], yhyl_ref[...])
    s5 = jax.nn.gelu(y) * jax.nn.sigmoid(_dot(y.astype(BF16), wglu_ref[...]) + bglu_ref[...])
    m_hy = _rms(y_hy) * ong_ref[:, 0:D_HY]
    m_s5 = _rms(s5) * ong_ref[:, D_HY:]
    o = (_dot(m_hy.astype(BF16), wout_ref[0:D_HY, :]) + _dot(m_s5.astype(BF16), wout_ref[D_HY:, :]))
    g1 = mod_ref[0, :, 2 * D_MODEL:3 * D_MODEL]
    sh2 = mod_ref[0, :, 3 * D_MODEL:4 * D_MODEL]
    sc2 = mod_ref[0, :, 4 * D_MODEL:5 * D_MODEL]
    x1 = _norm(ALPHA * x + g1 * o) * ln1g_ref[...] + ln1b_ref[...]
    x1_ref[...] = x1
    h2 = _norm(x1) * (1.0 + sc2) + sh2
    h2_ref[...] = h2.astype(BF16)
    hh, hl = _split(h2)
    logits = (_dot(hh, wrh_ref[...]) + _dot(hl, wrh_ref[...]) + _dot(hh, wrl_ref[...]) + br_ref[...])
    gates = _route(logits)
    gate_ref[...] = gates
    cnt_ref[0] = jnp.sum(gates, axis=0, keepdims=True)


def _out_proj(xc, xl, pos, yhy_c, yhy_l, ys5_c, ys5_l, mod, wglu_bf, bglu, ong, wout_bf, ln1g, ln1b,
              wr_hi, wr_lo, br, tm):
    n_ctx, n_lat = xc.shape[0], xl.shape[0]
    l_lat = pos.shape[0]
    ncb, nlb, npb = n_ctx // tm, n_lat // tm, l_lat // tm
    ctx = lambda w: pl.BlockSpec((tm, w), lambda i: (jnp.minimum(i, ncb - 1), 0))
    lat = lambda w: pl.BlockSpec((tm, w), lambda i: (jnp.maximum(i - ncb, 0), 0))
    full = lambda shape: pl.BlockSpec(shape, lambda i: (0,) * len(shape))
    out = lambda w: pl.BlockSpec((tm, w), lambda i: (i, 0))
    mod_idx = lambda i: (jnp.where(i < ncb, 0, 1 + jnp.maximum(i - ncb, 0) // npb), 0, 0)
    n_all = n_ctx + n_lat
    return pl.pallas_call(
        functools.partial(_out_kernel, ncb),
        grid=(ncb + nlb,),
        in_specs=[ctx(D_MODEL), lat(D_MODEL),
                  pl.BlockSpec((tm, D_MODEL), lambda i: (jnp.maximum(i - ncb, 0) % npb, 0)),
                  ctx(D_HY), lat(D_HY), ctx(D_S5), lat(D_S5),
                  pl.BlockSpec((1, 1, 6 * D_MODEL), mod_idx),
                  full((D_S5, D_S5)), full((1, D_S5)), full((1, D_MODEL)), full((D_MODEL, D_MODEL)),
                  full((1, D_MODEL)), full((1, D_MODEL)), full((D_MODEL, LANES)), full((D_MODEL, LANES)),
                  full((1, LANES))],
        out_specs=[out(D_MODEL), out(D_MODEL), out(LANES), pl.BlockSpec((1, 1, LANES), lambda i: (i, 0, 0))],
        out_shape=[jax.ShapeDtypeStruct((n_all, D_MODEL), F32),
                   jax.ShapeDtypeStruct((n_all, D_MODEL), BF16),
                   jax.ShapeDtypeStruct((n_all, LANES), F32),
                   jax.ShapeDtypeStruct((n_all // tm, 1, LANES), F32)],
        compiler_params=_cparams(("arbitrary",)),
        name="out_proj",
    )(xc, xl, pos, yhy_c, yhy_l, ys5_c, ys5_l, mod.reshape(mod.shape[0], 1, 6 * D_MODEL),
      wglu_bf, bglu, ong, wout_bf, ln1g, ln1b, wr_hi, wr_lo, br)


def _perm_t(gates, loc_ref, s):
    n = gates.shape[0]
    lane = lax.broadcasted_iota(jnp.int32, gates.shape, 1)
    oh = jnp.where((lane >= N_EXPERTS) & (lane < N_EXPERTS + N_EGROUPS), gates, 0.0)
    r = lax.broadcasted_iota(jnp.int32, (n, n), 0)
    c = lax.broadcasted_iota(jnp.int32, (n, n), 1)
    earlier = jnp.where(c < r, 1.0, 0.0).astype(BF16)
    cum = _dot(earlier, oh.astype(BF16))
    rank = jnp.sum(cum * oh, axis=-1, keepdims=True)
    lane1 = lax.broadcasted_iota(jnp.int32, (1, LANES), 1)
    locv = jnp.zeros((1, LANES), F32)
    for grp in range(N_EGROUPS):
        locv = jnp.where(lane1 == N_EXPERTS + grp, loc_ref[N_EGROUPS * s + grp].astype(F32), locv)
    dest = rank + jnp.sum(oh * locv, axis=-1, keepdims=True)
    slot = lax.broadcasted_iota(jnp.int32, (n, MOE_SLOTS), 1).astype(F32)
    return jnp.where(slot == dest, 1.0, 0.0)


def _segment_copies(s, loc_ref, len_ref, off_ref, make):
    for grp in range(N_EGROUPS):
        loc = loc_ref[N_EGROUPS * s + grp]
        off = off_ref[N_EGROUPS * s + grp]
        n_units = len_ref[N_EGROUPS * s + grp] // MOE_UNIT

        def body(i, carry):
            make(pl.multiple_of(loc + MOE_UNIT * i, MOE_UNIT), pl.multiple_of(off + MOE_UNIT * i, MOE_UNIT))
            return carry

        lax.fori_loop(0, n_units, body, 0)


def _pad_copies(pad_ref, n_blocks, zx_v, zg_v, xs_hbm, gs_hbm, sem, op):
    def unit(row, rows):
        getattr(pltpu.make_async_copy(zx_v.at[pl.ds(0, rows), :], xs_hbm.at[pl.ds(row, rows), :], sem.at[0]), op)()
        getattr(pltpu.make_async_copy(zg_v.at[pl.ds(0, rows), :], gs_hbm.at[pl.ds(row, rows), :], sem.at[1]), op)()

    for grp in range(N_EGROUPS):
        start = pad_ref[grp]

        def body(i, carry):
            unit(pl.multiple_of(start + MOE_UNIT * i, MOE_UNIT), MOE_UNIT)
            return carry

        lax.fori_loop(0, pad_ref[N_EGROUPS + grp], body, 0)

    def tail(b, carry):
        unit(pl.multiple_of(b * MOE_TM, MOE_TM), MOE_TM)
        return carry

    lax.fori_loop(pad_ref[2 * N_EGROUPS], n_blocks, tail, 0)


def _moe_sort_kernel(n_blocks, loc_ref, len_ref, off_ref, pad_ref, h_ref, gate_ref, xs_hbm, gs_hbm,
                     xs_v, gs_v, zx_v, zg_v, sem, zsem):
    s = pl.program_id(0)
    slot = s % 2

    @pl.when(s == 0)
    def _():
        zx_v[...] = jnp.zeros_like(zx_v)
        zg_v[...] = jnp.zeros_like(zg_v)
        _pad_copies(pad_ref, n_blocks, zx_v, zg_v, xs_hbm, gs_hbm, zsem, 'start')

    gates = gate_ref[...]
    p = _perm_t(gates, loc_ref, s).T.astype(BF16)
    xs_v[slot] = _dot(p, h_ref[...]).astype(BF16)
    g_hi = gates.astype(BF16)
    r1 = gates - g_hi.astype(F32)
    g_mid = r1.astype(BF16)
    g_lo = (r1 - g_mid.astype(F32)).astype(BF16)
    gs_v[slot] = _dot(p, g_hi) + _dot(p, g_mid) + _dot(p, g_lo)

    def copies(buf):
        def x_copy(lr, gr):
            return pltpu.make_async_copy(xs_v.at[buf, pl.ds(lr, MOE_UNIT), :],
                                         xs_hbm.at[pl.ds(gr, MOE_UNIT), :], sem.at[0, buf])

        def g_copy(lr, gr):
            return pltpu.make_async_copy(gs_v.at[buf, pl.ds(lr, MOE_UNIT), :],
                                         gs_hbm.at[pl.ds(gr, MOE_UNIT), :], sem.at[1, buf])

        def start(lr, gr):
            x_copy(lr, gr).start()
            g_copy(lr, gr).start()

        def wait(lr, gr):
            x_copy(lr, gr).wait()
            g_copy(lr, gr).wait()

        return start, wait

    _segment_copies(s, loc_ref, len_ref, off_ref, copies(slot)[0])

    @pl.when(s > 0)
    def _():
        _segment_copies(s - 1, loc_ref, len_ref, off_ref, copies(1 - slot)[1])

    @pl.when(s == pl.num_programs(0) - 1)
    def _():
        _segment_copies(s, loc_ref, len_ref, off_ref, copies(slot)[1])
        _pad_copies(pad_ref, n_blocks, zx_v, zg_v, xs_hbm, gs_hbm, zsem, 'wait')


def _moe_expert_kernel(bg_ref, nb_ref, xs_ref, gs_ref, wg_ref, wu_ref, wd_ref, o_ref):
    i = pl.program_id(0)

    @pl.when(i < nb_ref[0])
    def _():
        grp = bg_ref[i]
        x = xs_ref[...]
        gates = gs_ref[...]
        lane = lax.broadcasted_iota(jnp.int32, gates.shape, 1)
        acc = jnp.zeros(o_ref.shape, F32)
        for e in range(N_EPG):
            a = _dot(x, wg_ref[e].astype(BF16))
            u = _dot(x, wu_ref[e].astype(BF16))
            ge = jnp.sum(jnp.where(lane == N_EPG * grp + e, gates, 0.0), axis=-1, keepdims=True)
            hid = jax.nn.silu(a) * u * ge
            acc = acc + _dot(hid.astype(BF16), wd_ref[e].astype(BF16))
        o_ref[...] = acc.astype(BF16)

    @pl.when(i >= nb_ref[0])
    def _():
        o_ref[...] = jnp.zeros_like(o_ref)


def _moe_combine_kernel(loc_ref, len_ref, off_ref, gate_ref, x1_ref, mod_ref, ln2g_ref, ln2b_ref, o_hbm,
                        ctx_ref, lat_ref, o_v, sem, *, n_ctx_tiles):
    s = pl.program_id(0)
    slot = s % 2

    def copies(buf):
        def o_copy(lr, gr):
            return pltpu.make_async_copy(o_hbm.at[pl.ds(gr, MOE_UNIT), :],
                                         o_v.at[buf, pl.ds(lr, MOE_UNIT), :], sem.at[buf])

        return (lambda lr, gr: o_copy(lr, gr).start()), (lambda lr, gr: o_copy(lr, gr).wait())

    @pl.when(s == 0)
    def _():
        o_v[...] = jnp.zeros_like(o_v)
        _segment_copies(s, loc_ref, len_ref, off_ref, copies(slot)[0])

    @pl.when(s + 1 < pl.num_programs(0))
    def _():
        _segment_copies(s + 1, loc_ref, len_ref, off_ref, copies(1 - slot)[0])

    pt = _perm_t(gate_ref[...], loc_ref, s).astype(BF16)
    _segment_copies(s, loc_ref, len_ref, off_ref, copies(slot)[1])
    f = _dot(pt, o_v[slot])
    g2 = mod_ref[0, :, 5 * D_MODEL:6 * D_MODEL]
    x2 = _norm(ALPHA * x1_ref[...] + g2 * f) * ln2g_ref[...] + ln2b_ref[...]

    @pl.when(s < n_ctx_tiles)
    def _():
        ctx_ref[...] = x2

    @pl.when(s >= n_ctx_tiles)
    def _():
        lat_ref[...] = x2


def _moe_plan(tile_counts, n_blocks):
    cnt = tile_counts[:, 0, N_EXPERTS:N_EXPERTS + N_EGROUPS].astype(jnp.int32)
    len16 = ((cnt + MOE_UNIT - 1) // MOE_UNIT) * MOE_UNIT
    loc = jnp.cumsum(len16, axis=1) - len16
    rows_g = jnp.sum(len16, axis=0)
    reg_g = ((rows_g + MOE_TM - 1) // MOE_TM) * MOE_TM
    reg_start = jnp.cumsum(reg_g) - reg_g
    off = reg_start[None, :] + jnp.cumsum(len16, axis=0) - len16
    blk_end = jnp.cumsum(reg_g // MOE_TM)
    bi = jnp.arange(n_blocks, dtype=jnp.int32)
    blk_group = jnp.minimum(jnp.sum((bi[:, None] >= blk_end[None, :]).astype(jnp.int32), axis=1),
                            N_EGROUPS - 1)
    flat = lambda a: a.reshape(-1).astype(jnp.int32)
    pads = jnp.concatenate([reg_start + rows_g, (reg_g - rows_g) // MOE_UNIT, blk_end[-1:]])
    return (flat(loc), flat(len16), flat(off), flat(pads), blk_group.astype(jnp.int32),
            blk_end[-1:].astype(jnp.int32))


def _moe(h2_all, gates_all, tile_counts, x1_all, mod, w_gate, w_up, w_down, ln2g, ln2b, n_ctx,
         tokens_per_mod_row):
    n_tok = h2_all.shape[0]
    n_tiles = n_tok // MOE_ST
    n_ctx_tiles = n_ctx // MOE_ST
    max_rows = n_tok + n_tiles * N_EGROUPS * (MOE_UNIT - 1) + N_EGROUPS * (MOE_TM - 1)
    n_blocks = -(-max_rows // MOE_TM)
    n_rows = n_blocks * MOE_TM
    loc, len16, off, pads, blk_group, n_used = _moe_plan(tile_counts, n_blocks)

    tile = lambda w: pl.BlockSpec((MOE_ST, w), lambda s, *_: (s, 0))
    anyspec = pl.BlockSpec(memory_space=pl.ANY)
    xs, gs = pl.pallas_call(
        functools.partial(_moe_sort_kernel, n_blocks),
        grid_spec=pltpu.PrefetchScalarGridSpec(
            num_scalar_prefetch=4, grid=(n_tiles,),
            in_specs=[tile(D_MODEL), tile(LANES)],
            out_specs=[anyspec, anyspec],
            scratch_shapes=[pltpu.VMEM((2, MOE_SLOTS, D_MODEL), BF16), pltpu.VMEM((2, MOE_SLOTS, LANES), F32),
                            pltpu.VMEM((MOE_TM, D_MODEL), BF16), pltpu.VMEM((MOE_TM, LANES), F32),
                            pltpu.SemaphoreType.DMA((2, 2)), pltpu.SemaphoreType.DMA((2,))]),
        out_shape=[jax.ShapeDtypeStruct((n_rows, D_MODEL), BF16),
                   jax.ShapeDtypeStruct((n_rows, LANES), F32)],
        compiler_params=_cparams(("arbitrary",)),
        name="moe_sort",
    )(loc, len16, off, pads, h2_all, gates_all)

    blk = lambda w: pl.BlockSpec((MOE_TM, w), lambda i, bg, nb: (jnp.minimum(i, nb[0] - 1), 0))
    wspec = lambda a, b, mode: pl.BlockSpec((N_EPG, a, b), lambda i, bg, nb: (bg[i], 0, 0),
                                            pipeline_mode=mode)
    o_sorted = pl.pallas_call(
        _moe_expert_kernel,
        grid_spec=pltpu.PrefetchScalarGridSpec(
            num_scalar_prefetch=2, grid=(n_blocks,),
            in_specs=[blk(D_MODEL), blk(LANES), wspec(D_MODEL, D_EXPERT, None),
                      wspec(D_MODEL, D_EXPERT, None), wspec(D_EXPERT, D_MODEL, None)],
            out_specs=pl.BlockSpec((MOE_TM, D_MODEL), lambda i, bg, nb: (i, 0))),
        out_shape=jax.ShapeDtypeStruct((n_rows, D_MODEL), BF16),
        compiler_params=_cparams(("arbitrary",)),
        name="moe_experts",
    )(blk_group, n_used, xs, gs, w_gate, w_up, w_down)

    lat_per_row = tokens_per_mod_row // MOE_ST

    def mod_idx(s, *_):
        return (jnp.where(s < n_ctx_tiles, 0, 1 + (s - n_ctx_tiles) // lat_per_row), 0, 0)

    vec = pl.BlockSpec((1, D_MODEL), lambda s, *_: (0, 0))
    return pl.pallas_call(
        functools.partial(_moe_combine_kernel, n_ctx_tiles=n_ctx_tiles),
        grid_spec=pltpu.PrefetchScalarGridSpec(
            num_scalar_prefetch=3, grid=(n_tiles,),
            in_specs=[tile(LANES), tile(D_MODEL), pl.BlockSpec((1, 1, 6 * D_MODEL), mod_idx), vec, vec,
                      anyspec],
            out_specs=[pl.BlockSpec((MOE_ST, D_MODEL), lambda s, *_: (jnp.minimum(s, n_ctx_tiles - 1), 0)),
                       pl.BlockSpec((MOE_ST, D_MODEL), lambda s, *_: (jnp.maximum(s - n_ctx_tiles, 0), 0))],
            scratch_shapes=[pltpu.VMEM((2, MOE_SLOTS, D_MODEL), BF16), pltpu.SemaphoreType.DMA((2,))]),
        out_shape=[jax.ShapeDtypeStruct((n_ctx, D_MODEL), F32),
                   jax.ShapeDtypeStruct((n_tok - n_ctx, D_MODEL), F32)],
        compiler_params=_cparams(("arbitrary",)),
        name="moe_combine",
    )(loc, len16, off, gates_all, x1_all, mod.reshape(mod.shape[0], 1, 6 * D_MODEL), ln2g, ln2b, o_sorted)


def _grid_pos_embed(n_tokens):
    rows = n_tokens // GRID_W
    row = np.repeat(np.arange(rows, dtype=np.float64), GRID_W)
    col = np.tile(np.arange(GRID_W, dtype=np.float64), rows)
    quarter = D_MODEL // 4
    omega = 1.0 / (POS_BASE ** (np.arange(quarter, dtype=np.float64) / quarter))
    er = row[:, None] * omega
    ec = col[:, None] * omega
    return jnp.asarray(np.concatenate([np.sin(er), np.cos(er), np.sin(ec), np.cos(ec)], axis=-1), F32)


def _mixers(x, pos, mod3, h0_re, h0_im, tabs, filt, s5ops, wts, tm):
    bsz, n_tok, _ = x.shape
    shared = mod3.shape[0] == 1
    x3 = x.reshape(1, bsz * n_tok, D_MODEL) if shared else x
    proj_hy, u_s5 = _in_proj(x3, pos, mod3, wts['w_in'], tm)
    y_hy = _hyena(proj_hy.reshape(bsz, n_tok, 3 * D_HY), tabs, filt,
                  wts['hy_conv_w'], wts['hy_conv_b'], wts['hy_fbias'])
    y_s5, f_re, f_im = _s5(u_s5.reshape(bsz, n_tok, D_S5), s5ops, h0_re, h0_im)
    return y_hy.reshape(bsz * n_tok, D_HY), y_s5.reshape(bsz * n_tok, D_S5), f_re, f_im


def kernel(x_prompt, x_sample, state_s5_re, state_s5_im, c, c_ctx, w_ada, b_ada, w_in, hy_conv_w, hy_conv_b, hy_f_w1, hy_f_b1, hy_f_w2, hy_f_b2, hy_f_w3, hy_freq, hy_fbias, s5_a_re, s5_a_im, s5_log_dt, s5_b_re, s5_b_im, s5_c_re, s5_c_im, s5_d, s5_w_glu, s5_b_glu, out_norm_g, w_out, ln1_g, ln1_b, moe_w_r1, moe_b_r1, moe_w_r2, moe_b_r2, moe_w_gate, moe_w_up, moe_w_down, ln2_g, ln2_b):
    b_ctx, l_ctx, _ = x_prompt.shape
    b_lat, l_lat, _ = x_sample.shape
    g, p = S5_GROUPS, S5_STATE
    assert w_ada.shape[0] == 1, "single-layer trunk"
    l = 0

    nrow = 16
    cond = jnp.concatenate([c_ctx[None, :], c, jnp.zeros((nrow - 1 - b_lat, D_MODEL), F32)], axis=0)
    mod = _ada(cond, w_ada[l], b_ada[l])
    mod_ctx = mod[0:1].reshape(1, 1, 6 * D_MODEL)
    mod_lat = mod[1:1 + b_lat].reshape(b_lat, 1, 6 * D_MODEL)

    wr = jnp.concatenate([moe_w_r2[l].transpose(1, 0, 2).reshape(D_MODEL, N_EXPERTS), moe_w_r1[l]], axis=1)
    wr = jnp.pad(wr, ((0, 0), (0, LANES - wr.shape[1])))
    br = jnp.concatenate([moe_b_r2[l].reshape(-1), moe_b_r1[l]])
    br = jnp.pad(br, (0, LANES - br.shape[0])).reshape(1, LANES)
    wr_hi, wr_lo = _split(wr)

    wts = {
        'w_in': w_in[l], 'hy_conv_w': hy_conv_w[l], 'hy_conv_b': hy_conv_b[l],
        'hy_fbias': hy_fbias[l], 'w_glu': s5_w_glu[l].astype(BF16), 'b_glu': s5_b_glu[l].reshape(1, -1),
        'out_norm_g': out_norm_g[l].reshape(1, -1), 'w_out': w_out[l].astype(BF16),
        'ln1_g': ln1_g[l].reshape(1, -1), 'ln1_b': ln1_b[l].reshape(1, -1),
        'wr_hi': wr_hi, 'wr_lo': wr_lo, 'br': br,
        'w_gate': moe_w_gate[l], 'w_up': moe_w_up[l], 'w_down': moe_w_down[l],
        'ln2_g': ln2_g[l].reshape(1, -1), 'ln2_b': ln2_b[l].reshape(1, -1),
    }

    s5ops = _s5_operators(s5_a_re[l], s5_a_im[l], s5_log_dt[l], s5_b_re[l], s5_b_im[l],
                          s5_c_re[l], s5_c_im[l], s5_d[l])
    tabs_ctx = _tables(l_ctx)
    tabs_lat = _tables(l_lat)
    filt_args = (hy_f_w1[l], hy_f_b1[l], hy_f_w2[l], hy_f_b2[l], hy_f_w3[l], hy_freq[l])
    filt_ctx = _hyena_filters(l_ctx, tabs_ctx, *filt_args)
    filt_lat = _hyena_filters(l_lat, tabs_lat, *filt_args)

    zero = jnp.zeros((g, b_ctx, 2 * p), F32)
    yhy_c, ys5_c, f_re, f_im = _mixers(x_prompt, None, mod_ctx, zero, zero, tabs_ctx, filt_ctx, s5ops, wts, 1024)
    unpack = lambda f: f.reshape(g, b_ctx, 2, p).transpose(1, 2, 0, 3)[:, None]
    new_re, new_im = unpack(f_re), unpack(f_im)

    pack = lambda s: s[:, l].transpose(2, 0, 1, 3).reshape(g, b_lat, 2 * p)
    pos = _grid_pos_embed(l_lat)
    yhy_l, ys5_l, _, _ = _mixers(x_sample, pos, mod_lat, pack(state_s5_re), pack(state_s5_im),
                                 tabs_lat, filt_lat, s5ops, wts, 1024)

    n_ctx = b_ctx * l_ctx
    x1_all, h2_all, gates_all, tile_counts = _out_proj(
        x_prompt.reshape(n_ctx, D_MODEL), x_sample.reshape(b_lat * l_lat, D_MODEL), pos,
        yhy_c, yhy_l, ys5_c, ys5_l, mod, wts['w_glu'], wts['b_glu'], wts['out_norm_g'], wts['w_out'],
        wts['ln1_g'], wts['ln1_b'], wts['wr_hi'], wts['wr_lo'], wts['br'], MOE_ST)
    y_ctx, y_lat = _moe(h2_all, gates_all, tile_counts, x1_all, mod,
                        wts['w_gate'], wts['w_up'], wts['w_down'], wts['ln2_g'], wts['ln2_b'],
                        n_ctx, l_lat)
    return (y_ctx.reshape(x_prompt.shape), y_lat.reshape(x_sample.shape), new_re, new_im)
```

```python
import functools
import math

import numpy as np
import jax
import jax.numpy as jnp
from jax import lax
from jax.experimental import pallas as pl
from jax.experimental.pallas import tpu as pltpu

F32 = jnp.float32
BF16 = jnp.bfloat16

D_MODEL = 1024
DEPTH = 1
GRID_W = 64
POS_BASE = 10000.0
D_HY = 512
D_S5 = 512
S5_CH = 16
S5_GROUPS = 32
S5_STATE = 64
S5_CHUNK = 16
S5_ROW = S5_CHUNK * S5_CH
HY_BANDS = 16
HY_EMB = 1 + 2 * HY_BANDS
HY_HID = 64
HY_MIN_DECAY = math.log(1e-2) / 1.5
HY_MAX_DECAY = math.log(1e-2) / 0.3
N_EGROUPS = 4
N_EPG = 4
N_EXPERTS = 16
D_EXPERT = 512
LN_EPS = 1e-5
ALPHA = (2.0 * DEPTH) ** 0.25
LANES = 128
S5_GB = LANES // S5_CH
S5OPS_GB = 4
HY_CW = 512
MOE_ST = 512
MOE_SLOTS = 640
MOE_UNIT = 16
MOE_TM = 512
VMEM_LIMIT = 60000 * 1024


def _cparams(sem):
    return pltpu.CompilerParams(dimension_semantics=sem, vmem_limit_bytes=VMEM_LIMIT)


def _split(x):
    hi = x.astype(BF16)
    lo = (x - hi.astype(F32)).astype(BF16)
    return hi, lo


def _dot(a, b):
    return jnp.dot(a, b, preferred_element_type=F32)


def _dot_t(a, b):
    return lax.dot_general(a, b, (((1,), (1,)), ((), ())), preferred_element_type=F32)


def _mm3(a, b):
    ah, al = _split(a)
    bh, bl = _split(b)
    return _dot(ah, bh) + _dot(al, bh) + _dot(ah, bl)


def _mm3_t(a, b):
    ah, al = _split(a)
    bh, bl = _split(b)
    return _dot_t(ah, bh) + _dot_t(al, bh) + _dot_t(ah, bl)


def _norm(x):
    xc = x - jnp.mean(x, axis=-1, keepdims=True)
    return xc * lax.rsqrt(jnp.mean(xc * xc, axis=-1, keepdims=True) + LN_EPS)


def _rms(y):
    return y * lax.rsqrt(jnp.mean(y * y, axis=-1, keepdims=True) + LN_EPS)


def _ada_kernel(cond_ref, w_ref, b_ref, o_ref):
    c = jax.nn.silu(cond_ref[...])
    o_ref[...] = _mm3(c, w_ref[...]) + b_ref[...]


def _ada(cond, w_ada, b_ada):
    nb = cond.shape[0]
    n = w_ada.shape[1]
    tn = 1024
    return pl.pallas_call(
        _ada_kernel,
        grid=(n // tn,),
        in_specs=[pl.BlockSpec((nb, D_MODEL), lambda j: (0, 0)),
                  pl.BlockSpec((D_MODEL, tn), lambda j: (0, j)),
                  pl.BlockSpec((1, tn), lambda j: (0, j))],
        out_specs=pl.BlockSpec((nb, tn), lambda j: (0, j)),
        out_shape=jax.ShapeDtypeStruct((nb, n), F32),
        compiler_params=_cparams(("arbitrary",)),
        name="ada",
    )(cond, w_ada, b_ada.reshape(1, n))


RADIX2_MIN_HALF = 256


def _dft_tables(n_half):
    n = 2 * n_half
    idx = np.arange(n_half, dtype=np.int64)
    m = (idx[:, None] * idx[None, :]) % n
    ang = 2.0 * np.pi * m.astype(np.float64) / n
    cm = np.cos(ang)
    sm = -np.sin(ang)
    sm[0, :] = 1.0 - 2.0 * (idx % 2)
    return cm, sm


def _dense_tables(n_tok):
    n_half = n_tok // 2
    f = np.concatenate([np.arange(n_half), n_tok - np.arange(n_half)]).astype(np.int64)
    sidx = np.arange(n_tok, dtype=np.int64)
    ang = 2.0 * np.pi * ((f[:, None] * sidx[None, :]) % (2 * n_tok)).astype(np.float64) / (2 * n_tok)
    cd = np.cos(ang)
    sd = -np.sin(ang)
    half = 2.0 * np.pi * ((n_half * sidx) % (2 * n_tok)).astype(np.float64) / (2 * n_tok)
    sd[0, :] = np.cos(half)
    sd[n_half, :] = -np.sin(half)
    return cd, sd


def _tables(n_tok):
    n_half = n_tok // 2
    bf = lambda t: jnp.asarray(t.astype(np.float32)).astype(BF16)
    if n_half < RADIX2_MIN_HALF:
        cd, sd = _dense_tables(n_tok)
        return (bf(cd), bf(sd), bf(np.ascontiguousarray(cd.T)), bf(np.ascontiguousarray(sd.T)))
    cm, sm = _dft_tables(n_half)
    ang = np.pi * np.arange(n_half, dtype=np.float64) / n_tok
    tw = [jnp.asarray(np.broadcast_to(v[:, None], (n_half, HY_CW)).astype(np.float32))
          for v in (np.cos(ang), -np.sin(ang))]
    return (bf(cm), bf(sm), bf(cm), bf(np.ascontiguousarray(sm.T)), tw[0], tw[1])


def _put_cols(ref, x):
    for j in range(ref.shape[0]):
        ref[j] = x[:, LANES * j:LANES * (j + 1)]


def _get_cols(ref):
    return jnp.concatenate([ref[j] for j in range(ref.shape[0])], axis=1)


def _get_parity(ref, parity):
    n_half = ref.shape[1] // 2
    return jnp.concatenate([ref[j, pl.ds(parity, n_half, stride=2), :] for j in range(ref.shape[0])], axis=1)


def _put_parity(ref, parity, x):
    n_half = ref.shape[1] // 2
    for j in range(ref.shape[0]):
        ref[j, pl.ds(parity, n_half, stride=2), :] = x[:, LANES * j:LANES * (j + 1)]


def _set_row0(x, v):
    first = lax.broadcasted_iota(jnp.int32, (8, x.shape[1]), 0) == 0
    return jnp.concatenate([jnp.where(first, v, x[:8]), x[8:]], axis=0)


def _rfft_packed(x_ref, tabs):
    n_half = x_ref.shape[1] // 2
    if len(tabs) == 4:
        x = _get_cols(x_ref).astype(BF16)
        re, im = _dot(tabs[0], x), _dot(tabs[1], x)
        return re[:n_half], im[:n_half], re[n_half:], im[n_half:]
    cm, sm, _, _, tw_re, tw_im = tabs
    xe = _get_parity(x_ref, 0).astype(BF16)
    xo = _get_parity(x_ref, 1).astype(BF16)
    e_re, e_im = _dot(cm, xe), _dot(sm, xe)
    o_re, o_im = _dot(cm, xo), _dot(sm, xo)
    t_re = tw_re * o_re - tw_im * o_im
    t_im = tw_re * o_im + tw_im * o_re
    a_im = _set_row0(e_im + t_im, e_im[0:1])
    b_im = _set_row0(t_im - e_im, -o_im[0:1])
    return e_re + t_re, a_im, e_re - t_re, b_im


def _irfft_packed(y_ref, ya_re, ya_im, yb_re, yb_im, tabs):
    if len(tabs) == 4:
        y_re = jnp.concatenate([ya_re, yb_re], axis=0).astype(BF16)
        y_im = jnp.concatenate([ya_im, yb_im], axis=0).astype(BF16)
        _put_cols(y_ref, _dot(tabs[2], y_re) + _dot(tabs[3], y_im))
        return
    _, _, cm, st, tw_re, tw_im = tabs
    p_e = ya_re + yb_re
    q_e = _set_row0(ya_im - yb_im, ya_im[0:1])
    _put_parity(y_ref, 0, _dot(cm, p_e.astype(BF16)) + _dot(st, q_e.astype(BF16)))
    ra_re = ya_re * tw_re + ya_im * tw_im
    ra_im = ya_im * tw_re - ya_re * tw_im
    rb_re = yb_im * tw_im - yb_re * tw_re
    rb_im = -(yb_re * tw_im + yb_im * tw_re)
    p_o = ra_re + rb_re
    q_o = _set_row0(ra_im - rb_im, -yb_im[0:1])
    _put_parity(y_ref, 1, _dot(cm, p_o.astype(BF16)) + _dot(st, q_o.astype(BF16)))


def _filt_kernel(n_tok, n_tab, z_ref, t_ref, w1_ref, b1_ref, w2_ref, b2_ref, fr_ref, w3f_ref, w3b_ref,
                 dl_ref, *rest):
    tabs = tuple(r[...] for r in rest[:n_tab])
    kar_ref, kai_ref, kbr_ref, kbi_ref, p_ref, q_ref = rest[n_tab:]
    fr = fr_ref[...]
    h = jnp.sin(fr * (_mm3(z_ref[...], w1_ref[...]) + b1_ref[...]))
    h = jnp.sin(fr * (_mm3(h, w2_ref[...]) + b2_ref[...]))
    decay = jnp.exp(-t_ref[...] * dl_ref[...])
    row = lax.broadcasted_iota(jnp.int32, decay.shape, 0)
    hf = _mm3(h, w3f_ref[...]) * decay
    hb = jnp.where(row == 0, 0.0, _mm3(h, w3b_ref[...]) * decay)
    _put_cols(p_ref, hf + hb)
    _put_cols(q_ref, hf - hb)
    pa_re, pa_im, pb_re, _ = _rfft_packed(p_ref, tabs)
    _, qa_im, _, qb_im = _rfft_packed(q_ref, tabs)
    row0 = lax.broadcasted_iota(jnp.int32, pa_re.shape, 0) == 0
    inv_n = 1.0 / (2 * n_tok)
    w_re = jnp.where(row0, inv_n, 2.0 * inv_n)
    kar_ref[...] = w_re * pa_re
    kbr_ref[...] = w_re * pb_re
    kai_ref[...] = (2.0 * inv_n) * _set_row0(qa_im, pa_im[0:1])
    kbi_ref[...] = (2.0 * inv_n) * qb_im


def _hyena_filters(n_tok, tabs, hy_f_w1, hy_f_b1, hy_f_w2, hy_f_b2, hy_f_w3, hy_freq):
    n_half = n_tok // 2
    tt = np.linspace(0.0, 1.0, n_tok)[:, None]
    ang = (2.0 * np.pi * np.arange(n_tok) / n_tok)[:, None] * np.linspace(1e-4, HY_BANDS - 1, HY_BANDS)[None, :]
    z = np.concatenate([tt, np.cos(ang), -np.sin(ang), np.zeros((n_tok, LANES - HY_EMB))], axis=-1)
    z, t = jnp.asarray(z, F32), jnp.asarray(tt, F32)
    deltas = jnp.asarray(np.abs(np.linspace(HY_MIN_DECAY, HY_MAX_DECAY, D_HY))[None, :], F32)
    w1 = jnp.pad(hy_f_w1, ((0, LANES - HY_EMB), (0, 0)))
    ncb = D_HY // HY_CW
    full = lambda j: (0, 0)
    out_sd = jax.ShapeDtypeStruct((n_half, 2 * D_HY), F32)
    tab_specs = [pl.BlockSpec(t_.shape, full, pipeline_mode=pl.Buffered(1)) for t_ in tabs]
    return pl.pallas_call(
        functools.partial(_filt_kernel, n_tok, len(tabs)),
        grid=(2 * ncb,),
        in_specs=[pl.BlockSpec((n_tok, LANES), full),
                  pl.BlockSpec((n_tok, 1), full),
                  pl.BlockSpec((LANES, HY_HID), full),
                  pl.BlockSpec((1, HY_HID), full),
                  pl.BlockSpec((HY_HID, HY_HID), full),
                  pl.BlockSpec((1, HY_HID), full),
                  pl.BlockSpec((1, HY_HID), full),
                  pl.BlockSpec((HY_HID, HY_CW), lambda j: (0, 2 * ncb * (j // ncb) + j % ncb)),
                  pl.BlockSpec((HY_HID, HY_CW), lambda j: (0, 2 * ncb * (j // ncb) + ncb + j % ncb)),
                  pl.BlockSpec((1, HY_CW), lambda j: (0, j % ncb))] + tab_specs,
        out_specs=[pl.BlockSpec((n_half, HY_CW), lambda j: (0, j))] * 4,
        out_shape=[out_sd] * 4,
        scratch_shapes=[pltpu.VMEM((HY_CW // LANES, n_tok, LANES), F32)] * 2,
        compiler_params=_cparams(("arbitrary",)),
        name=f"filt{n_tok}",
    )(z, t, w1, hy_f_b1.reshape(1, -1), hy_f_w2, hy_f_b2.reshape(1, -1), hy_freq.reshape(1, -1),
      hy_f_w3, hy_f_w3, deltas, *tabs)


def _hyena_kernel(n_tab, pv_ref, p1_ref, p2_ref, cwv_ref, cw1_ref, cw2_ref, cbv_ref, cb1_ref, cb2_ref,
                  fbias_ref, *rest):
    tabs = tuple(r[...] for r in rest[:n_tab])
    (kar0_ref, kai0_ref, kbr0_ref, kbi0_ref, kar1_ref, kai1_ref, kbr1_ref, kbi1_ref,
     o_ref, u_ref, y_ref) = rest[n_tab:]
    n_tok = pv_ref.shape[1]
    row = lax.broadcasted_iota(jnp.int32, (n_tok, pv_ref.shape[2]), 0)

    def short_conv(p_ref, cw_ref, cb_ref):
        p = p_ref[0]
        prev = jnp.where(row == 0, 0.0, pltpu.roll(p, 1, axis=0))
        nxt = jnp.where(row == n_tok - 1, 0.0, pltpu.roll(p, n_tok - 1, axis=0))
        return cb_ref[...] + prev * cw_ref[0:1, :] + p * cw_ref[1:2, :] + nxt * cw_ref[2:3, :]

    def fftconv(u, kar_ref, kai_ref, kbr_ref, kbi_ref, skip):
        _put_cols(u_ref, u)
        ua_re, ua_im, ub_re, ub_im = _rfft_packed(u_ref, tabs)
        ka_re, ka_im, kb_re, kb_im = kar_ref[...], kai_ref[...], kbr_ref[...], kbi_ref[...]
        zero_row = jnp.zeros_like(ka_im[0:1])
        kaz = _set_row0(ka_im, zero_row)
        kbz = _set_row0(kb_im, zero_row)
        ya_re = ua_re * ka_re - ua_im * kaz
        yb_re = ub_re * kb_re - ub_im * kbz
        h_re = ua_im[0:1] * ka_im[0:1] - ub_im[0:1] * kb_im[0:1]
        h_im = ua_im[0:1] * kb_im[0:1] + ub_im[0:1] * ka_im[0:1]
        ya_im = _set_row0(ua_re * ka_im + ua_im * ka_re, h_re)
        yb_im = _set_row0(ub_re * kb_im + ub_im * kb_re, h_im)
        _irfft_packed(y_ref, ya_re, ya_im, yb_re, yb_im, tabs)
        return _get_cols(y_ref) + u * skip

    v = short_conv(pv_ref, cwv_ref, cbv_ref)
    x1 = short_conv(p1_ref, cw1_ref, cb1_ref)
    z = x1 * fftconv(v, kar0_ref, kai0_ref, kbr0_ref, kbi0_ref, fbias_ref[0:1, :])
    x2 = short_conv(p2_ref, cw2_ref, cb2_ref)
    o_ref[0] = x2 * fftconv(z, kar1_ref, kai1_ref, kbr1_ref, kbi1_ref, fbias_ref[1:2, :])


def _hyena(proj_hy, tabs, filt, hy_conv_w, hy_conv_b, hy_fbias):
    bsz, n_tok, _ = proj_hy.shape
    n_half = n_tok // 2
    ncb = D_HY // HY_CW
    cb = hy_conv_b.reshape(1, -1)
    tab_specs = [pl.BlockSpec(t.shape, lambda b, c: (0, 0), pipeline_mode=pl.Buffered(1)) for t in tabs]

    def pspec(k):
        return pl.BlockSpec((1, n_tok, HY_CW), lambda b, c: (b, 0, k * ncb + c))

    def cwspec(k):
        return pl.BlockSpec((3, HY_CW), lambda b, c: (0, k * ncb + c))

    def cbspec(k):
        return pl.BlockSpec((1, HY_CW), lambda b, c: (0, k * ncb + c))

    def fspec(o):
        mode = pl.Buffered(1) if ncb == 1 else None
        return pl.BlockSpec((n_half, HY_CW), lambda b, c: (0, o * ncb + c), pipeline_mode=mode)

    return pl.pallas_call(
        functools.partial(_hyena_kernel, len(tabs)),
        grid=(bsz, ncb),
        in_specs=[pspec(0), pspec(1), pspec(2), cwspec(0), cwspec(1), cwspec(2),
                  cbspec(0), cbspec(1), cbspec(2),
                  pl.BlockSpec((2, HY_CW), lambda b, c: (0, c))] + tab_specs + [fspec(0)] * 4 + [fspec(1)] * 4,
        out_specs=pl.BlockSpec((1, n_tok, HY_CW), lambda b, c: (b, 0, c)),
        out_shape=jax.ShapeDtypeStruct((bsz, n_tok, D_HY), F32),
        scratch_shapes=[pltpu.VMEM((HY_CW // LANES, n_tok, LANES), F32)] * 2,
        compiler_params=_cparams(("arbitrary", "arbitrary")),
        name=f"hyena{n_tok}",
    )(proj_hy, proj_hy, proj_hy, hy_conv_w, hy_conv_w, hy_conv_w, cb, cb, cb, hy_fbias,
      *tabs, *filt, *filt)


def _s5ops_kernel(*refs):
    for g in range(S5OPS_GB):
        _s5ops_group(g, *refs)


def _s5ops_group(g, are_ref, aim_ref, ldt_ref, btr_ref, bti_ref, cre_ref, cim_ref, d_ref,
                 mt_ref, erh_ref, erl_ref, eih_ref, eil_ref, gr_ref, gi_ref, atr_ref, ati_ref,
                 er_ref, ei_ref):
    a_re, a_im = are_ref[g], aim_ref[g]
    dt = jnp.exp(ldt_ref[g])
    mag = jnp.exp(a_re * dt)
    ab_re = mag * jnp.cos(a_im * dt)
    ab_im = mag * jnp.sin(a_im * dt)
    n_re, n_im = ab_re - 1.0, ab_im
    den = a_re * a_re + a_im * a_im
    q_re = (n_re * a_re + n_im * a_im) / den
    q_im = (n_im * a_re - n_re * a_im) / den
    bt_re, bt_im = btr_ref[g], bti_ref[g]
    bb_re = q_re * bt_re - q_im * bt_im
    bb_im = q_re * bt_im + q_im * bt_re
    c_re, c_im = cre_ref[g, 0:S5_CH, :], cim_ref[g, 0:S5_CH, :]
    pw = [(jnp.ones_like(ab_re), jnp.zeros_like(ab_re))]
    for _ in range(S5_CHUNK):
        pr, pi = pw[-1]
        pw.append((pr * ab_re - pi * ab_im, pr * ab_im + pi * ab_re))
    lane = lax.broadcasted_iota(jnp.int32, ab_re.shape, 1)
    fwd = lane < S5_STATE
    for s in range(S5_CHUNK):
        e_re = jnp.where(fwd, pw[S5_CHUNK - 1 - s][0], pw[s][0])
        e_im = jnp.where(fwd, pw[S5_CHUNK - 1 - s][1], pw[s][1])
        er_ref[g, pl.ds(S5_CH * s, S5_CH), :] = e_re * bb_re - e_im * bb_im
        ei_ref[g, pl.ds(S5_CH * s, S5_CH), :] = e_re * bb_im + e_im * bb_re
        g_re = jnp.where(fwd, pw[s + 1][0], pw[S5_CHUNK - s][0])
        g_im = jnp.where(fwd, pw[s + 1][1], pw[S5_CHUNK - s][1])
        gr_ref[g, pl.ds(S5_CH * s, S5_CH), :] = (c_re * g_re - c_im * g_im).astype(BF16)
        gi_ref[g, pl.ds(S5_CH * s, S5_CH), :] = (-(c_re * g_im + c_im * g_re)).astype(BF16)
    atr_ref[g] = pw[S5_CHUNK][0]
    ati_ref[g] = pw[S5_CHUNK][1]
    er, ei = er_ref[g], ei_ref[g]
    erh_ref[g], erl_ref[g] = _split(er)
    eih_ref[g], eil_ref[g] = _split(ei)
    lane2 = lax.broadcasted_iota(jnp.int32, er.shape, 1)
    row2 = lax.broadcasted_iota(jnp.int32, er.shape, 0)
    f2 = lane2 < S5_STATE
    zero = jnp.zeros_like(er)

    cp_re, cp_im = cre_ref[g], cim_ref[g]
    kf = _mm3_t(jnp.where(f2, er, zero), cp_re) - _mm3_t(jnp.where(f2, ei, zero), cp_im)
    kb = _mm3_t(jnp.where(f2, zero, er), cp_re) - _mm3_t(jnp.where(f2, zero, ei), cp_im)
    d_row = d_ref[g]
    steps_per_vreg = LANES // S5_CH
    for half in range(S5_CHUNK // steps_per_vreg):
        acc = zero
        for tt in range(steps_per_vreg):
            t = half * steps_per_vreg + tt
            nf = S5_CH * (S5_CHUNK - 1 - t)
            nb = S5_CH * t
            col_f = jnp.concatenate([kf[nf:], zero[:nf]], axis=0) if nf else kf
            col_b = jnp.concatenate([zero[:nb], kb[:S5_ROW - nb]], axis=0) if nb else kb
            diag = jnp.where((row2 // S5_CH == t) & (row2 % S5_CH == lane2), d_row, 0.0)
            col = col_f + col_b + diag
            r = pltpu.roll(col, S5_CH * tt, axis=1) if tt else col
            acc = jnp.where((lane2 >= S5_CH * tt) & (lane2 < S5_CH * (tt + 1)), r, acc)
        mt_ref[g, :, LANES * half:LANES * (half + 1)] = acc.astype(BF16)


def _s5_operators(s5_a_re, s5_a_im, s5_log_dt, s5_b_re, s5_b_im, s5_c_re, s5_c_im, s5_d):
    g, p, h = S5_GROUPS, S5_STATE, S5_CH
    cat = lambda x: jnp.concatenate([x[0], x[1]], axis=-1)
    a_re = cat(s5_a_re).reshape(g, 1, 2 * p)
    a_im = cat(s5_a_im).reshape(g, 1, 2 * p)
    ldt = cat(jnp.broadcast_to(s5_log_dt[:, :, None], (2, g, p))).reshape(g, 1, 2 * p)
    bt_re = cat(jnp.swapaxes(s5_b_re, -1, -2))
    bt_im = cat(jnp.swapaxes(s5_b_im, -1, -2))
    cpad = lambda c: jnp.pad(jnp.concatenate([c, c], axis=-1), ((0, 0), (0, LANES - h), (0, 0)))
    c_re, c_im = cpad(s5_c_re), cpad(s5_c_im)
    d_row = jnp.pad(s5_d.reshape(g, 1, h), ((0, 0), (0, 0), (0, LANES - h)))
    vec = pl.BlockSpec((S5OPS_GB, 1, 2 * p), lambda i: (i, 0, 0))
    hp = pl.BlockSpec((S5OPS_GB, h, 2 * p), lambda i: (i, 0, 0))
    sq = pl.BlockSpec((S5OPS_GB, LANES, 2 * p), lambda i: (i, 0, 0))
    big = pl.BlockSpec((S5OPS_GB, S5_ROW, 2 * p), lambda i: (i, 0, 0))
    mts = pl.BlockSpec((S5OPS_GB, S5_ROW, S5_ROW), lambda i: (i, 0, 0))
    big_sd = jax.ShapeDtypeStruct((g, S5_ROW, 2 * p), BF16)
    vec_sd = jax.ShapeDtypeStruct((g, 1, 2 * p), F32)
    return pl.pallas_call(
        _s5ops_kernel,
        grid=(g // S5OPS_GB,),
        in_specs=[vec, vec, vec, hp, hp, sq, sq, vec],
        out_specs=[mts, big, big, big, big, big, big, vec, vec],
        out_shape=[jax.ShapeDtypeStruct((g, S5_ROW, S5_ROW), BF16)] + [big_sd] * 6 + [vec_sd, vec_sd],
        scratch_shapes=[pltpu.VMEM((S5OPS_GB, S5_ROW, 2 * p), F32)] * 2,
        compiler_params=_cparams(("arbitrary",)),
        name="s5ops",
    )(a_re, a_im, ldt, bt_re, bt_im, c_re, c_im, d_row)


def _block_transpose(xs):
    n = len(xs)
    lane = lax.broadcasted_iota(jnp.int32, xs[0].shape, 1)
    xs = list(xs)
    d = n // 2
    while d:
        keep = ((lane // S5_CH) & d) == 0
        for i in range(n):
            if i & d:
                continue
            lo, hi = xs[i], xs[i + d]
            xs[i] = jnp.where(keep, lo, pltpu.roll(hi, S5_CH * d, axis=1))
            xs[i + d] = jnp.where(keep, pltpu.roll(lo, LANES - S5_CH * d, axis=1), hi)
        d //= 2
    return xs


def _s5_kernel(bsz, n_chunks, u_ref, mt_ref, erh_ref, erl_ref, eih_ref, eil_ref, gr_ref, gi_ref,
               atr_ref, ati_ref, h0r_ref, h0i_ref, y_ref, fr_ref, fi_ref,
               ua_ref, ub_ref, ya_ref, yb_ref, sr_ref, si_ref, xfr_ref, xfi_ref, xbr_ref, xbi_ref):
    nc = n_chunks
    spv = LANES // S5_CH
    rsub = min(nc, 32)

    def to_chunks(b, carry):
        for half, dst in ((0, ua_ref), (1, ub_ref)):
            for r0 in range(0, nc, rsub):
                xs = [u_ref[b, pl.ds(S5_CHUNK * r0 + half * spv + tt, rsub, stride=S5_CHUNK), :]
                      for tt in range(spv)]
                for k, blk in enumerate(_block_transpose(xs)):
                    dst[k, pl.ds(r0 * bsz + b, rsub, stride=bsz), :] = blk
        return carry

    lax.fori_loop(0, bsz, to_chunks, 0, unroll=2)

    lane = lax.broadcasted_iota(jnp.int32, (bsz, 2 * S5_STATE), 1)
    fwd = lane < S5_STATE
    lane_all = lax.broadcasted_iota(jnp.int32, (bsz * nc, 2 * S5_STATE), 1)
    fwd_all = lane_all < S5_STATE

    def group(k, slot):
        u = jnp.concatenate([ua_ref[k], ub_ref[k]], axis=1)
        uh, ul = _split(u)
        sr_ref[slot] = _dot(uh, erh_ref[k]) + _dot(ul, erh_ref[k]) + _dot(uh, erl_ref[k])
        si_ref[slot] = _dot(uh, eih_ref[k]) + _dot(ul, eih_ref[k]) + _dot(uh, eil_ref[k])
        at_re, at_im = atr_ref[k], ati_ref[k]
        y_intra = _dot(uh, mt_ref[k])

        def step(i, xc):
            x_re, x_im = xc
            rf = pl.ds(pl.multiple_of(i * bsz, bsz), bsz)
            rb = pl.ds(pl.multiple_of((nc - 1 - i) * bsz, bsz), bsz)
            xfr_ref[slot, rf, :] = x_re
            xfi_ref[slot, rf, :] = x_im
            xbr_ref[slot, rb, :] = x_re
            xbi_ref[slot, rb, :] = x_im
            s_re = jnp.where(fwd, sr_ref[slot, rf, :], sr_ref[slot, rb, :])
            s_im = jnp.where(fwd, si_ref[slot, rf, :], si_ref[slot, rb, :])
            return (at_re * x_re - at_im * x_im + s_re, at_re * x_im + at_im * x_re + s_im)

        x_re, x_im = lax.fori_loop(0, nc, step, (h0r_ref[k], h0i_ref[k]), unroll=True)
        fr_ref[k] = x_re
        fi_ref[k] = x_im
        xp_re = jnp.where(fwd_all, xfr_ref[slot], xbr_ref[slot]).astype(BF16)
        xp_im = jnp.where(fwd_all, xfi_ref[slot], xbi_ref[slot]).astype(BF16)
        y = y_intra + _dot_t(xp_re, gr_ref[k]) + _dot_t(xp_im, gi_ref[k])
        ya_ref[k] = y[:, :LANES]
        yb_ref[k] = y[:, LANES:]

    def group_pair(j, carry):
        group(2 * j, 0)
        group(2 * j + 1, 1)
        return carry

    lax.fori_loop(0, S5_GB // 2, group_pair, 0)

    def to_tokens(b, carry):
        for half, src in ((0, ya_ref), (1, yb_ref)):
            for r0 in range(0, nc, rsub):
                ys = [src[k, pl.ds(r0 * bsz + b, rsub, stride=bsz), :] for k in range(S5_GB)]
                for tt, blk in enumerate(_block_transpose(ys)):
                    y_ref[b, pl.ds(S5_CHUNK * r0 + half * spv + tt, rsub, stride=S5_CHUNK), :] = blk
        return carry

    lax.fori_loop(0, bsz, to_tokens, 0, unroll=2)


def _s5(u, ops, h0_re, h0_im):
    bsz, n_tok, _ = u.shape
    g, p = S5_GROUPS, S5_STATE
    nc = n_tok // S5_CHUNK
    rows = nc * bsz
    tok = pl.BlockSpec((bsz, n_tok, LANES), lambda j: (0, 0, j))
    gspec = lambda shape: pl.BlockSpec((S5_GB,) + shape, lambda j: (j, 0, 0))
    op = gspec((S5_ROW, 2 * p))
    return pl.pallas_call(
        functools.partial(_s5_kernel, bsz, nc),
        grid=(g // S5_GB,),
        in_specs=[tok, gspec((S5_ROW, S5_ROW)), op, op, op, op, op, op,
                  gspec((1, 2 * p)), gspec((1, 2 * p)), gspec((bsz, 2 * p)), gspec((bsz, 2 * p))],
        out_specs=[tok, gspec((bsz, 2 * p)), gspec((bsz, 2 * p))],
        out_shape=[jax.ShapeDtypeStruct((bsz, n_tok, D_S5), F32),
                   jax.ShapeDtypeStruct((g, bsz, 2 * p), F32),
                   jax.ShapeDtypeStruct((g, bsz, 2 * p), F32)],
        scratch_shapes=([pltpu.VMEM((S5_GB, rows, LANES), F32)] * 4
                        + [pltpu.VMEM((2, rows, 2 * p), F32)] * 6),
        compiler_params=_cparams(("arbitrary",)),
        name=f"s5_{n_tok}",
    )(u, *ops, h0_re, h0_im)


def _in_kernel(has_pos, *refs):
    if has_pos:
        x_ref, pos_ref, mod_ref, w_ref, hy_ref, s5_ref = refs
        x = x_ref[0] + pos_ref[...]
    else:
        x_ref, mod_ref, w_ref, hy_ref, s5_ref = refs
        x = x_ref[0]
    sh1 = mod_ref[0, :, 0:D_MODEL]
    sc1 = mod_ref[0, :, D_MODEL:2 * D_MODEL]
    h = _norm(x) * (1.0 + sc1) + sh1
    proj = _dot(h.astype(BF16), w_ref[...].astype(BF16))
    hy_ref[0] = proj[:, :3 * D_HY]
    s5_ref[0] = proj[:, 3 * D_HY:]


def _in_proj(x3, pos, mod3, w_in, tm):
    nb, lt, _ = x3.shape
    has_pos = pos is not None
    per_batch = mod3.shape[0] > 1
    midx = (lambda b, i: (b, 0, 0)) if per_batch else (lambda b, i: (0, 0, 0))
    in_specs = [pl.BlockSpec((1, tm, D_MODEL), lambda b, i: (b, i, 0))]
    args = [x3]
    if has_pos:
        in_specs.append(pl.BlockSpec((tm, D_MODEL), lambda b, i: (i, 0)))
        args.append(pos)
    in_specs += [pl.BlockSpec((1, 1, 6 * D_MODEL), midx),
                 pl.BlockSpec((D_MODEL, 3 * D_HY + D_S5), lambda b, i: (0, 0), pipeline_mode=pl.Buffered(1))]
    args += [mod3, w_in]
    return pl.pallas_call(
        functools.partial(_in_kernel, has_pos),
        grid=(nb, lt // tm),
        in_specs=in_specs,
        out_specs=[pl.BlockSpec((1, tm, 3 * D_HY), lambda b, i: (b, i, 0)),
                   pl.BlockSpec((1, tm, D_S5), lambda b, i: (b, i, 0))],
        out_shape=[jax.ShapeDtypeStruct((nb, lt, 3 * D_HY), F32),
                   jax.ShapeDtypeStruct((nb, lt, D_S5), F32)],
        compiler_params=_cparams(("arbitrary", "arbitrary")),
        name=f"in_proj{nb}",
    )(*args)


def _route(logits):
    lane = lax.broadcasted_iota(jnp.int32, logits.shape, 1)
    lane_f = lane.astype(F32)
    neg = -jnp.inf
    big = float(LANES)
    m1 = (lane >= N_EXPERTS) & (lane < N_EXPERTS + N_EGROUPS)
    l1 = jnp.where(m1, logits, neg)
    top1 = jnp.max(l1, axis=-1, keepdims=True)
    grp = jnp.min(jnp.where(l1 == top1, lane_f, big), axis=-1, keepdims=True) - float(N_EXPERTS)
    den = jnp.sum(jnp.where(m1, jnp.exp(logits - top1), 0.0), axis=-1, keepdims=True)
    p_grp = 1.0 / den
    lo = grp * float(N_EPG)
    m2 = (lane_f >= lo) & (lane_f < lo + float(N_EPG))
    l2 = jnp.where(m2, logits, neg)
    v1 = jnp.max(l2, axis=-1, keepdims=True)
    i1 = jnp.min(jnp.where(l2 == v1, lane_f, big), axis=-1, keepdims=True)
    l2b = jnp.where(lane_f == i1, neg, l2)
    v2 = jnp.max(l2b, axis=-1, keepdims=True)
    i2 = jnp.min(jnp.where(l2b == v2, lane_f, big), axis=-1, keepdims=True)
    e = jnp.exp(v2 - v1)
    w1 = 1.0 / (1.0 + e)
    w2 = e / (1.0 + e)
    gates = jnp.where(lane_f == i1, w1 * p_grp, 0.0) + jnp.where(lane_f == i2, w2 * p_grp, 0.0)
    return jnp.where(lane_f == grp + float(N_EXPERTS), 1.0, gates)


def _out_kernel(n_ctx_blocks, xc_ref, xl_ref, pos_ref, yhyc_ref, yhyl_ref, ys5c_ref, ys5l_ref, mod_ref,
                wglu_ref, bglu_ref, ong_ref, wout_ref, ln1g_ref, ln1b_ref, wrh_ref, wrl_ref, br_ref,
                x1_ref, h2_ref, gate_ref, cnt_ref):
    is_ctx = pl.program_id(0) < n_ctx_blocks
    x = jnp.where(is_ctx, xc_ref[...], xl_ref[...] + pos_ref[...])
    y = jnp.where(is_ctx, ys5c_ref[...], ys5l_ref[...])
    y_hy = jnp.where(is_ctx, yhyc_ref[...], yhyl_ref[...])
    s5 = jax.nn.gelu(y) * jax.nn.sigmoid(_dot(y.astype(BF16), wglu_ref[...]) + bglu_ref[...])
    m_hy = _rms(y_hy) * ong_ref[:, 0:D_HY]
    m_s5 = _rms(s5) * ong_ref[:, D_HY:]
    o = (_dot(m_hy.astype(BF16), wout_ref[0:D_HY, :]) + _dot(m_s5.astype(BF16), wout_ref[D_HY:, :]))
    g1 = mod_ref[0, :, 2 * D_MODEL:3 * D_MODEL]
    sh2 = mod_ref[0, :, 3 * D_MODEL:4 * D_MODEL]
    sc2 = mod_ref[0, :, 4 * D_MODEL:5 * D_MODEL]
    x1 = _norm(ALPHA * x + g1 * o) * ln1g_ref[...] + ln1b_ref[...]
    x1_ref[...] = x1
    h2 = _norm(x1) * (1.0 + sc2) + sh2
    h2_ref[...] = h2.astype(BF16)
    hh, hl = _split(h2)
    logits = (_dot(hh, wrh_ref[...]) + _dot(hl, wrh_ref[...]) + _dot(hh, wrl_ref[...]) + br_ref[...])
    gates = _route(logits)
    gate_ref[...] = gates
    cnt_ref[0] = jnp.sum(gates, axis=0, keepdims=True)


def _out_proj(xc, xl, pos, yhy_c, yhy_l, ys5_c, ys5_l, mod, wglu_bf, bglu, ong, wout_bf, ln1g, ln1b,
              wr_hi, wr_lo, br, tm):
    n_ctx, n_lat = xc.shape[0], xl.shape[0]
    l_lat = pos.shape[0]
    ncb, nlb, npb = n_ctx // tm, n_lat // tm, l_lat // tm
    ctx = lambda w: pl.BlockSpec((tm, w), lambda i: (jnp.minimum(i, ncb - 1), 0))
    lat = lambda w: pl.BlockSpec((tm, w), lambda i: (jnp.maximum(i - ncb, 0), 0))
    full = lambda shape: pl.BlockSpec(shape, lambda i: (0,) * len(shape))
    out = lambda w: pl.BlockSpec((tm, w), lambda i: (i, 0))
    mod_idx = lambda i: (jnp.where(i < ncb, 0, 1 + jnp.maximum(i - ncb, 0) // npb), 0, 0)
    n_all = n_ctx + n_lat
    return pl.pallas_call(
        functools.partial(_out_kernel, ncb),
        grid=(ncb + nlb,),
        in_specs=[ctx(D_MODEL), lat(D_MODEL),
                  pl.BlockSpec((tm, D_MODEL), lambda i: (jnp.maximum(i - ncb, 0) % npb, 0)),
                  ctx(D_HY), lat(D_HY), ctx(D_S5), lat(D_S5),
                  pl.BlockSpec((1, 1, 6 * D_MODEL), mod_idx),
                  full((D_S5, D_S5)), full((1, D_S5)), full((1, D_MODEL)), full((D_MODEL, D_MODEL)),
                  full((1, D_MODEL)), full((1, D_MODEL)), full((D_MODEL, LANES)), full((D_MODEL, LANES)),
                  full((1, LANES))],
        out_specs=[out(D_MODEL), out(D_MODEL), out(LANES), pl.BlockSpec((1, 1, LANES), lambda i: (i, 0, 0))],
        out_shape=[jax.ShapeDtypeStruct((n_all, D_MODEL), F32),
                   jax.ShapeDtypeStruct((n_all, D_MODEL), BF16),
                   jax.ShapeDtypeStruct((n_all, LANES), F32),
                   jax.ShapeDtypeStruct((n_all // tm, 1, LANES), F32)],
        compiler_params=_cparams(("arbitrary",)),
        name="out_proj",
    )(xc, xl, pos, yhy_c, yhy_l, ys5_c, ys5_l, mod.reshape(mod.shape[0], 1, 6 * D_MODEL),
      wglu_bf, bglu, ong, wout_bf, ln1g, ln1b, wr_hi, wr_lo, br)


def _perm_t(gates, loc_ref, s):
    n = gates.shape[0]
    lane = lax.broadcasted_iota(jnp.int32, gates.shape, 1)
    oh = jnp.where((lane >= N_EXPERTS) & (lane < N_EXPERTS + N_EGROUPS), gates, 0.0)
    r = lax.broadcasted_iota(jnp.int32, (n, n), 0)
    c = lax.broadcasted_iota(jnp.int32, (n, n), 1)
    earlier = jnp.where(c < r, 1.0, 0.0).astype(BF16)
    cum = _dot(earlier, oh.astype(BF16))
    rank = jnp.sum(cum * oh, axis=-1, keepdims=True)
    lane1 = lax.broadcasted_iota(jnp.int32, (1, LANES), 1)
    locv = jnp.zeros((1, LANES), F32)
    for grp in range(N_EGROUPS):
        locv = jnp.where(lane1 == N_EXPERTS + grp, loc_ref[N_EGROUPS * s + grp].astype(F32), locv)
    dest = rank + jnp.sum(oh * locv, axis=-1, keepdims=True)
    slot = lax.broadcasted_iota(jnp.int32, (n, MOE_SLOTS), 1).astype(F32)
    return jnp.where(slot == dest, 1.0, 0.0)


def _segment_copies(s, loc_ref, len_ref, off_ref, make):
    for grp in range(N_EGROUPS):
        loc = loc_ref[N_EGROUPS * s + grp]
        off = off_ref[N_EGROUPS * s + grp]
        n_units = len_ref[N_EGROUPS * s + grp] // MOE_UNIT

        def body(i, carry):
            make(pl.multiple_of(loc + MOE_UNIT * i, MOE_UNIT), pl.multiple_of(off + MOE_UNIT * i, MOE_UNIT))
            return carry

        lax.fori_loop(0, n_units, body, 0)


def _pad_copies(pad_ref, n_blocks, zx_v, zg_v, xs_hbm, gs_hbm, sem, op):
    def unit(row, rows):
        getattr(pltpu.make_async_copy(zx_v.at[pl.ds(0, rows), :], xs_hbm.at[pl.ds(row, rows), :], sem.at[0]), op)()
        getattr(pltpu.make_async_copy(zg_v.at[pl.ds(0, rows), :], gs_hbm.at[pl.ds(row, rows), :], sem.at[1]), op)()

    for grp in range(N_EGROUPS):
        start = pad_ref[grp]

        def body(i, carry):
            unit(pl.multiple_of(start + MOE_UNIT * i, MOE_UNIT), MOE_UNIT)
            return carry

        lax.fori_loop(0, pad_ref[N_EGROUPS + grp], body, 0)

    def tail(b, carry):
        unit(pl.multiple_of(b * MOE_TM, MOE_TM), MOE_TM)
        return carry

    lax.fori_loop(pad_ref[2 * N_EGROUPS], n_blocks, tail, 0)


def _moe_sort_kernel(n_blocks, loc_ref, len_ref, off_ref, pad_ref, h_ref, gate_ref, xs_hbm, gs_hbm,
                     xs_v, gs_v, zx_v, zg_v, sem, zsem):
    s = pl.program_id(0)
    slot = s % 2

    @pl.when(s == 0)
    def _():
        zx_v[...] = jnp.zeros_like(zx_v)
        zg_v[...] = jnp.zeros_like(zg_v)
        _pad_copies(pad_ref, n_blocks, zx_v, zg_v, xs_hbm, gs_hbm, zsem, 'start')

    gates = gate_ref[...]
    p = _perm_t(gates, loc_ref, s).T.astype(BF16)
    xs_v[slot] = _dot(p, h_ref[...]).astype(BF16)
    g_hi = gates.astype(BF16)
    r1 = gates - g_hi.astype(F32)
    g_mid = r1.astype(BF16)
    g_lo = (r1 - g_mid.astype(F32)).astype(BF16)
    gs_v[slot] = _dot(p, g_hi) + _dot(p, g_mid) + _dot(p, g_lo)

    def copies(buf):
        def x_copy(lr, gr):
            return pltpu.make_async_copy(xs_v.at[buf, pl.ds(lr, MOE_UNIT), :],
                                         xs_hbm.at[pl.ds(gr, MOE_UNIT), :], sem.at[0, buf])

        def g_copy(lr, gr):
            return pltpu.make_async_copy(gs_v.at[buf, pl.ds(lr, MOE_UNIT), :],
                                         gs_hbm.at[pl.ds(gr, MOE_UNIT), :], sem.at[1, buf])

        def start(lr, gr):
            x_copy(lr, gr).start()
            g_copy(lr, gr).start()

        def wait(lr, gr):
            x_copy(lr, gr).wait()
            g_copy(lr, gr).wait()

        return start, wait

    _segment_copies(s, loc_ref, len_ref, off_ref, copies(slot)[0])

    @pl.when(s > 0)
    def _():
        _segment_copies(s - 1, loc_ref, len_ref, off_ref, copies(1 - slot)[1])

    @pl.when(s == pl.num_programs(0) - 1)
    def _():
        _segment_copies(s, loc_ref, len_ref, off_ref, copies(slot)[1])
        _pad_copies(pad_ref, n_blocks, zx_v, zg_v, xs_hbm, gs_hbm, zsem, 'wait')


def _moe_expert_kernel(bg_ref, nb_ref, xs_ref, gs_ref, wg_ref, wu_ref, wd_ref, o_ref):
    i = pl.program_id(0)

    @pl.when(i < nb_ref[0])
    def _():
        grp = bg_ref[i]
        x = xs_ref[...]
        gates = gs_ref[...]
        lane = lax.broadcasted_iota(jnp.int32, gates.shape, 1)
        acc = jnp.zeros(o_ref.shape, F32)
        for e in range(N_EPG):
            a = _dot(x, wg_ref[e].astype(BF16))
            u = _dot(x, wu_ref[e].astype(BF16))
            ge = jnp.sum(jnp.where(lane == N_EPG * grp + e, gates, 0.0), axis=-1, keepdims=True)
            hid = jax.nn.silu(a) * u * ge
            acc = acc + _dot(hid.astype(BF16), wd_ref[e].astype(BF16))
        o_ref[...] = acc.astype(BF16)

    @pl.when(i >= nb_ref[0])
    def _():
        o_ref[...] = jnp.zeros_like(o_ref)


def _moe_combine_kernel(loc_ref, len_ref, off_ref, gate_ref, x1_ref, mod_ref, ln2g_ref, ln2b_ref, o_hbm,
                        ctx_ref, lat_ref, o_v, sem, *, n_ctx_tiles):
    s = pl.program_id(0)
    slot = s % 2

    def copies(buf):
        def o_copy(lr, gr):
            return pltpu.make_async_copy(o_hbm.at[pl.ds(gr, MOE_UNIT), :],
                                         o_v.at[buf, pl.ds(lr, MOE_UNIT), :], sem.at[buf])

        return (lambda lr, gr: o_copy(lr, gr).start()), (lambda lr, gr: o_copy(lr, gr).wait())

    @pl.when(s == 0)
    def _():
        o_v[...] = jnp.zeros_like(o_v)
        _segment_copies(s, loc_ref, len_ref, off_ref, copies(slot)[0])

    @pl.when(s + 1 < pl.num_programs(0))
    def _():
        _segment_copies(s + 1, loc_ref, len_ref, off_ref, copies(1 - slot)[0])

    pt = _perm_t(gate_ref[...], loc_ref, s).astype(BF16)
    _segment_copies(s, loc_ref, len_ref, off_ref, copies(slot)[1])
    f = _dot(pt, o_v[slot])
    g2 = mod_ref[0, :, 5 * D_MODEL:6 * D_MODEL]
    x2 = _norm(ALPHA * x1_ref[...] + g2 * f) * ln2g_ref[...] + ln2b_ref[...]

    @pl.when(s < n_ctx_tiles)
    def _():
        ctx_ref[...] = x2

    @pl.when(s >= n_ctx_tiles)
    def _():
        lat_ref[...] = x2


def _moe_plan(tile_counts, n_blocks):
    cnt = tile_counts[:, 0, N_EXPERTS:N_EXPERTS + N_EGROUPS].astype(jnp.int32)
    len16 = ((cnt + MOE_UNIT - 1) // MOE_UNIT) * MOE_UNIT
    loc = jnp.cumsum(len16, axis=1) - len16
    rows_g = jnp.sum(len16, axis=0)
    reg_g = ((rows_g + MOE_TM - 1) // MOE_TM) * MOE_TM
    reg_start = jnp.cumsum(reg_g) - reg_g
    off = reg_start[None, :] + jnp.cumsum(len16, axis=0) - len16
    blk_end = jnp.cumsum(reg_g // MOE_TM)
    bi = jnp.arange(n_blocks, dtype=jnp.int32)
    blk_group = jnp.minimum(jnp.sum((bi[:, None] >= blk_end[None, :]).astype(jnp.int32), axis=1),
                            N_EGROUPS - 1)
    flat = lambda a: a.reshape(-1).astype(jnp.int32)
    pads = jnp.concatenate([reg_start + rows_g, (reg_g - rows_g) // MOE_UNIT, blk_end[-1:]])
    return (flat(loc), flat(len16), flat(off), flat(pads), blk_group.astype(jnp.int32),
            blk_end[-1:].astype(jnp.int32))


def _moe(h2_all, gates_all, tile_counts, x1_all, mod, w_gate, w_up, w_down, ln2g, ln2b, n_ctx,
         tokens_per_mod_row):
    n_tok = h2_all.shape[0]
    n_tiles = n_tok // MOE_ST
    n_ctx_tiles = n_ctx // MOE_ST
    max_rows = n_tok + n_tiles * N_EGROUPS * (MOE_UNIT - 1) + N_EGROUPS * (MOE_TM - 1)
    n_blocks = -(-max_rows // MOE_TM)
    n_rows = n_blocks * MOE_TM
    loc, len16, off, pads, blk_group, n_used = _moe_plan(tile_counts, n_blocks)

    tile = lambda w: pl.BlockSpec((MOE_ST, w), lambda s, *_: (s, 0))
    anyspec = pl.BlockSpec(memory_space=pl.ANY)
    xs, gs = pl.pallas_call(
        functools.partial(_moe_sort_kernel, n_blocks),
        grid_spec=pltpu.PrefetchScalarGridSpec(
            num_scalar_prefetch=4, grid=(n_tiles,),
            in_specs=[tile(D_MODEL), tile(LANES)],
            out_specs=[anyspec, anyspec],
            scratch_shapes=[pltpu.VMEM((2, MOE_SLOTS, D_MODEL), BF16), pltpu.VMEM((2, MOE_SLOTS, LANES), F32),
                            pltpu.VMEM((MOE_TM, D_MODEL), BF16), pltpu.VMEM((MOE_TM, LANES), F32),
                            pltpu.SemaphoreType.DMA((2, 2)), pltpu.SemaphoreType.DMA((2,))]),
        out_shape=[jax.ShapeDtypeStruct((n_rows, D_MODEL), BF16),
                   jax.ShapeDtypeStruct((n_rows, LANES), F32)],
        compiler_params=_cparams(("arbitrary",)),
        name="moe_sort",
    )(loc, len16, off, pads, h2_all, gates_all)

    blk = lambda w: pl.BlockSpec((MOE_TM, w), lambda i, bg, nb: (jnp.minimum(i, nb[0] - 1), 0))
    wspec = lambda a, b, mode: pl.BlockSpec((N_EPG, a, b), lambda i, bg, nb: (bg[i], 0, 0),
                                            pipeline_mode=mode)
    o_sorted = pl.pallas_call(
        _moe_expert_kernel,
        grid_spec=pltpu.PrefetchScalarGridSpec(
            num_scalar_prefetch=2, grid=(n_blocks,),
            in_specs=[blk(D_MODEL), blk(LANES), wspec(D_MODEL, D_EXPERT, None),
                      wspec(D_MODEL, D_EXPERT, None), wspec(D_EXPERT, D_MODEL, None)],
            out_specs=pl.BlockSpec((MOE_TM, D_MODEL), lambda i, bg, nb: (i, 0))),
        out_shape=jax.ShapeDtypeStruct((n_rows, D_MODEL), BF16),
        compiler_params=_cparams(("arbitrary",)),
        name="moe_experts",
    )(blk_group, n_used, xs, gs, w_gate, w_up, w_down)

    lat_per_row = tokens_per_mod_row // MOE_ST

    def mod_idx(s, *_):
        return (jnp.where(s < n_ctx_tiles, 0, 1 + (s - n_ctx_tiles) // lat_per_row), 0, 0)

    vec = pl.BlockSpec((1, D_MODEL), lambda s, *_: (0, 0))
    return pl.pallas_call(
        functools.partial(_moe_combine_kernel, n_ctx_tiles=n_ctx_tiles),
        grid_spec=pltpu.PrefetchScalarGridSpec(
            num_scalar_prefetch=3, grid=(n_tiles,),
            in_specs=[tile(LANES), tile(D_MODEL), pl.BlockSpec((1, 1, 6 * D_MODEL), mod_idx), vec, vec,
                      anyspec],
            out_specs=[pl.BlockSpec((MOE_ST, D_MODEL), lambda s, *_: (jnp.minimum(s, n_ctx_tiles - 1), 0)),
                       pl.BlockSpec((MOE_ST, D_MODEL), lambda s, *_: (jnp.maximum(s - n_ctx_tiles, 0), 0))],
            scratch_shapes=[pltpu.VMEM((2, MOE_SLOTS, D_MODEL), BF16), pltpu.SemaphoreType.DMA((2,))]),
        out_shape=[jax.ShapeDtypeStruct((n_ctx, D_MODEL), F32),
                   jax.ShapeDtypeStruct((n_tok - n_ctx, D_MODEL), F32)],
        compiler_params=_cparams(("arbitrary",)),
        name="moe_combine",
    )(loc, len16, off, gates_all, x1_all, mod.reshape(mod.shape[0], 1, 6 * D_MODEL), ln2g, ln2b, o_sorted)


def _grid_pos_embed(n_tokens):
    rows = n_tokens // GRID_W
    row = np.repeat(np.arange(rows, dtype=np.float64), GRID_W)
    col = np.tile(np.arange(GRID_W, dtype=np.float64), rows)
    quarter = D_MODEL // 4
    omega = 1.0 / (POS_BASE ** (np.arange(quarter, dtype=np.float64) / quarter))
    er = row[:, None] * omega
    ec = col[:, None] * omega
    return jnp.asarray(np.concatenate([np.sin(er), np.cos(er), np.sin(ec), np.cos(ec)], axis=-1), F32)


def _mixers(x, pos, mod3, h0_re, h0_im, tabs, filt, s5ops, wts, tm):
    bsz, n_tok, _ = x.shape
    shared = mod3.shape[0] == 1
    x3 = x.reshape(1, bsz * n_tok, D_MODEL) if shared else x
    proj_hy, u_s5 = _in_proj(x3, pos, mod3, wts['w_in'], tm)
    y_hy = _hyena(proj_hy.reshape(bsz, n_tok, 3 * D_HY), tabs, filt,
                  wts['hy_conv_w'], wts['hy_conv_b'], wts['hy_fbias'])
    y_s5, f_re, f_im = _s5(u_s5.reshape(bsz, n_tok, D_S5), s5ops, h0_re, h0_im)
    return y_hy.reshape(bsz * n_tok, D_HY), y_s5.reshape(bsz * n_tok, D_S5), f_re, f_im


def kernel(x_prompt, x_sample, state_s5_re, state_s5_im, c, c_ctx, w_ada, b_ada, w_in, hy_conv_w, hy_conv_b, hy_f_w1, hy_f_b1, hy_f_w2, hy_f_b2, hy_f_w3, hy_freq, hy_fbias, s5_a_re, s5_a_im, s5_log_dt, s5_b_re, s5_b_im, s5_c_re, s5_c_im, s5_d, s5_w_glu, s5_b_glu, out_norm_g, w_out, ln1_g, ln1_b, moe_w_r1, moe_b_r1, moe_w_r2, moe_b_r2, moe_w_gate, moe_w_up, moe_w_down, ln2_g, ln2_b):
    b_ctx, l_ctx, _ = x_prompt.shape
    b_lat, l_lat, _ = x_sample.shape
    g, p = S5_GROUPS, S5_STATE
    assert w_ada.shape[0] == 1, "single-layer trunk"
    l = 0

    nrow = 16
    cond = jnp.concatenate([c_ctx[None, :], c, jnp.zeros((nrow - 1 - b_lat, D_MODEL), F32)], axis=0)
    mod = _ada(cond, w_ada[l], b_ada[l])
    mod_ctx = mod[0:1].reshape(1, 1, 6 * D_MODEL)
    mod_lat = mod[1:1 + b_lat].reshape(b_lat, 1, 6 * D_MODEL)

    wr = jnp.concatenate([moe_w_r2[l].transpose(1, 0, 2).reshape(D_MODEL, N_EXPERTS), moe_w_r1[l]], axis=1)
    wr = jnp.pad(wr, ((0, 0), (0, LANES - wr.shape[1])))
    br = jnp.concatenate([moe_b_r2[l].reshape(-1), moe_b_r1[l]])
    br = jnp.pad(br, (0, LANES - br.shape[0])).reshape(1, LANES)
    wr_hi, wr_lo = _split(wr)

    wts = {
        'w_in': w_in[l], 'hy_conv_w': hy_conv_w[l], 'hy_conv_b': hy_conv_b[l],
        'hy_fbias': hy_fbias[l], 'w_glu': s5_w_glu[l].astype(BF16), 'b_glu': s5_b_glu[l].reshape(1, -1),
        'out_norm_g': out_norm_g[l].reshape(1, -1), 'w_out': w_out[l].astype(BF16),
        'ln1_g': ln1_g[l].reshape(1, -1), 'ln1_b': ln1_b[l].reshape(1, -1),
        'wr_hi': wr_hi, 'wr_lo': wr_lo, 'br': br,
        'w_gate': moe_w_gate[l], 'w_up': moe_w_up[l], 'w_down': moe_w_down[l],
        'ln2_g': ln2_g[l].reshape(1, -1), 'ln2_b': ln2_b[l].reshape(1, -1),
    }

    s5ops = _s5_operators(s5_a_re[l], s5_a_im[l], s5_log_dt[l], s5_b_re[l], s5_b_im[l],
                          s5_c_re[l], s5_c_im[l], s5_d[l])
    tabs_ctx = _tables(l_ctx)
    tabs_lat = _tables(l_lat)
    filt_args = (hy_f_w1[l], hy_f_b1[l], hy_f_w2[l], hy_f_b2[l], hy_f_w3[l], hy_freq[l])
    filt_ctx = _hyena_filters(l_ctx, tabs_ctx, *filt_args)
    filt_lat = _hyena_filters(l_lat, tabs_lat, *filt_args)

    zero = jnp.zeros((g, b_ctx, 2 * p), F32)
    yhy_c, ys5_c, f_re, f_im = _mixers(x_prompt, None, mod_ctx, zero, zero, tabs_ctx, filt_ctx, s5ops, wts, 1024)
    unpack = lambda f: f.reshape(g, b_ctx, 2, p).transpose(1, 2, 0, 3)[:, None]
    new_re, new_im = unpack(f_re), unpack(f_im)

    pack = lambda s: s[:, l].transpose(2, 0, 1, 3).reshape(g, b_lat, 2 * p)
    pos = _grid_pos_embed(l_lat)
    yhy_l, ys5_l, _, _ = _mixers(x_sample, pos, mod_lat, pack(state_s5_re), pack(state_s5_im),
                                 tabs_lat, filt_lat, s5ops, wts, 1024)

    n_ctx = b_ctx * l_ctx
    x1_all, h2_all, gates_all, tile_counts = _out_proj(
        x_prompt.reshape(n_ctx, D_MODEL), x_sample.reshape(b_lat * l_lat, D_MODEL), pos,
        yhy_c, yhy_l, ys5_c, ys5_l, mod, wts['w_glu'], wts['b_glu'], wts['out_norm_g'], wts['w_out'],
        wts['ln1_g'], wts['ln1_b'], wts['wr_hi'], wts['wr_lo'], wts['br'], MOE_ST)
    y_ctx, y_lat = _moe(h2_all, gates_all, tile_counts, x1_all, mod,
                        wts['w_gate'], wts['w_up'], wts['w_down'], wts['ln2_g'], wts['ln2_b'],
                        n_ctx, l_lat)
    return (y_ctx.reshape(x_prompt.shape), y_lat.reshape(x_sample.shape), new_re, new_im)
```

```python
import functools
import math

import numpy as np
import jax
import jax.numpy as jnp
from jax import lax
from jax.experimental import pallas as pl
from jax.experimental.pallas import tpu as pltpu

F32 = jnp.float32
BF16 = jnp.bfloat16

D_MODEL = 1024
DEPTH = 1
GRID_W = 64
POS_BASE = 10000.0
D_HY = 512
D_S5 = 512
S5_CH = 16
S5_GROUPS = 32
S5_STATE = 64
S5_CHUNK = 16
S5_ROW = S5_CHUNK * S5_CH
HY_BANDS = 16
HY_EMB = 1 + 2 * HY_BANDS
HY_HID = 64
HY_MIN_DECAY = math.log(1e-2) / 1.5
HY_MAX_DECAY = math.log(1e-2) / 0.3
N_EGROUPS = 4
N_EPG = 4
N_EXPERTS = 16
D_EXPERT = 512
LN_EPS = 1e-5
ALPHA = (2.0 * DEPTH) ** 0.25
LANES = 128
S5_GB = LANES // S5_CH
S5OPS_GB = 4
HY_CW = 512
MOE_ST = 512
MOE_SLOTS = 640
MOE_UNIT = 16
MOE_BIG = 64
MOE_TM = 512
VMEM_LIMIT = 60000 * 1024


def _cparams(sem):
    return pltpu.CompilerParams(dimension_semantics=sem, vmem_limit_bytes=VMEM_LIMIT)


def _split(x):
    hi = x.astype(BF16)
    lo = (x - hi.astype(F32)).astype(BF16)
    return hi, lo


def _dot(a, b):
    return jnp.dot(a, b, preferred_element_type=F32)


def _dot_t(a, b):
    return lax.dot_general(a, b, (((1,), (1,)), ((), ())), preferred_element_type=F32)


def _mm3(a, b):
    ah, al = _split(a)
    bh, bl = _split(b)
    return _dot(ah, bh) + _dot(al, bh) + _dot(ah, bl)


def _mm3_t(a, b):
    ah, al = _split(a)
    bh, bl = _split(b)
    return _dot_t(ah, bh) + _dot_t(al, bh) + _dot_t(ah, bl)


def _norm(x):
    xc = x - jnp.mean(x, axis=-1, keepdims=True)
    return xc * lax.rsqrt(jnp.mean(xc * xc, axis=-1, keepdims=True) + LN_EPS)


def _rms(y):
    return y * lax.rsqrt(jnp.mean(y * y, axis=-1, keepdims=True) + LN_EPS)


def _ada_kernel(cond_ref, w_ref, b_ref, o_ref):
    c = jax.nn.silu(cond_ref[...])
    o_ref[...] = _mm3(c, w_ref[...]) + b_ref[...]


def _ada(cond, w_ada, b_ada):
    nb = cond.shape[0]
    n = w_ada.shape[1]
    tn = 1024
    return pl.pallas_call(
        _ada_kernel,
        grid=(n // tn,),
        in_specs=[pl.BlockSpec((nb, D_MODEL), lambda j: (0, 0)),
                  pl.BlockSpec((D_MODEL, tn), lambda j: (0, j)),
                  pl.BlockSpec((1, tn), lambda j: (0, j))],
        out_specs=pl.BlockSpec((nb, tn), lambda j: (0, j)),
        out_shape=jax.ShapeDtypeStruct((nb, n), F32),
        compiler_params=_cparams(("arbitrary",)),
        name="ada",
    )(cond, w_ada, b_ada.reshape(1, n))


RADIX2_MIN_HALF = 256


def _dft_tables(n_half):
    n = 2 * n_half
    idx = np.arange(n_half, dtype=np.int64)
    m = (idx[:, None] * idx[None, :]) % n
    ang = 2.0 * np.pi * m.astype(np.float64) / n
    cm = np.cos(ang)
    sm = -np.sin(ang)
    sm[0, :] = 1.0 - 2.0 * (idx % 2)
    return cm, sm


def _dense_tables(n_tok):
    n_half = n_tok // 2
    f = np.concatenate([np.arange(n_half), n_tok - np.arange(n_half)]).astype(np.int64)
    sidx = np.arange(n_tok, dtype=np.int64)
    ang = 2.0 * np.pi * ((f[:, None] * sidx[None, :]) % (2 * n_tok)).astype(np.float64) / (2 * n_tok)
    cd = np.cos(ang)
    sd = -np.sin(ang)
    half = 2.0 * np.pi * ((n_half * sidx) % (2 * n_tok)).astype(np.float64) / (2 * n_tok)
    sd[0, :] = np.cos(half)
    sd[n_half, :] = -np.sin(half)
    return cd, sd


def _tables(n_tok):
    n_half = n_tok // 2
    bf = lambda t: jnp.asarray(t.astype(np.float32)).astype(BF16)
    if n_half < RADIX2_MIN_HALF:
        cd, sd = _dense_tables(n_tok)
        return (bf(cd), bf(sd), bf(np.ascontiguousarray(cd.T)), bf(np.ascontiguousarray(sd.T)))
    cm, sm = _dft_tables(n_half)
    ang = np.pi * np.arange(n_half, dtype=np.float64) / n_tok
    tw = [jnp.asarray(np.broadcast_to(v[:, None], (n_half, HY_CW)).astype(np.float32))
          for v in (np.cos(ang), -np.sin(ang))]
    return (bf(cm), bf(sm), bf(cm), bf(np.ascontiguousarray(sm.T)), tw[0], tw[1])


def _put_cols(ref, x):
    for j in range(ref.shape[0]):
        ref[j] = x[:, LANES * j:LANES * (j + 1)]


def _get_cols(ref):
    return jnp.concatenate([ref[j] for j in range(ref.shape[0])], axis=1)


def _get_parity(ref, parity):
    n_half = ref.shape[1] // 2
    return jnp.concatenate([ref[j, pl.ds(parity, n_half, stride=2), :] for j in range(ref.shape[0])], axis=1)


def _put_parity(ref, parity, x):
    n_half = ref.shape[1] // 2
    for j in range(ref.shape[0]):
        ref[j, pl.ds(parity, n_half, stride=2), :] = x[:, LANES * j:LANES * (j + 1)]


def _set_row0(x, v):
    first = lax.broadcasted_iota(jnp.int32, (8, x.shape[1]), 0) == 0
    return jnp.concatenate([jnp.where(first, v, x[:8]), x[8:]], axis=0)


def _rfft_packed(x_ref, tabs):
    n_half = x_ref.shape[1] // 2
    if len(tabs) == 4:
        x = _get_cols(x_ref).astype(BF16)
        re, im = _dot(tabs[0], x), _dot(tabs[1], x)
        return re[:n_half], im[:n_half], re[n_half:], im[n_half:]
    cm, sm, _, _, tw_re, tw_im = tabs
    xe = _get_parity(x_ref, 0).astype(BF16)
    xo = _get_parity(x_ref, 1).astype(BF16)
    e_re, e_im = _dot(cm, xe), _dot(sm, xe)
    o_re, o_im = _dot(cm, xo), _dot(sm, xo)
    t_re = tw_re * o_re - tw_im * o_im
    t_im = tw_re * o_im + tw_im * o_re
    a_im = _set_row0(e_im + t_im, e_im[0:1])
    b_im = _set_row0(t_im - e_im, -o_im[0:1])
    return e_re + t_re, a_im, e_re - t_re, b_im


def _irfft_packed(y_ref, ya_re, ya_im, yb_re, yb_im, tabs):
    if len(tabs) == 4:
        y_re = jnp.concatenate([ya_re, yb_re], axis=0).astype(BF16)
        y_im = jnp.concatenate([ya_im, yb_im], axis=0).astype(BF16)
        _put_cols(y_ref, _dot(tabs[2], y_re) + _dot(tabs[3], y_im))
        return
    _, _, cm, st, tw_re, tw_im = tabs
    p_e = ya_re + yb_re
    q_e = _set_row0(ya_im - yb_im, ya_im[0:1])
    _put_parity(y_ref, 0, _dot(cm, p_e.astype(BF16)) + _dot(st, q_e.astype(BF16)))
    ra_re = ya_re * tw_re + ya_im * tw_im
    ra_im = ya_im * tw_re - ya_re * tw_im
    rb_re = yb_im * tw_im - yb_re * tw_re
    rb_im = -(yb_re * tw_im + yb_im * tw_re)
    p_o = ra_re + rb_re
    q_o = _set_row0(ra_im - rb_im, -yb_im[0:1])
    _put_parity(y_ref, 1, _dot(cm, p_o.astype(BF16)) + _dot(st, q_o.astype(BF16)))


def _filt_kernel(n_tok, n_tab, z_ref, t_ref, w1_ref, b1_ref, w2_ref, b2_ref, fr_ref, w3f_ref, w3b_ref,
                 dl_ref, *rest):
    tabs = tuple(r[...] for r in rest[:n_tab])
    kar_ref, kai_ref, kbr_ref, kbi_ref, p_ref, q_ref = rest[n_tab:]
    fr = fr_ref[...]
    h = jnp.sin(fr * (_mm3(z_ref[...], w1_ref[...]) + b1_ref[...]))
    h = jnp.sin(fr * (_mm3(h, w2_ref[...]) + b2_ref[...]))
    decay = jnp.exp(-t_ref[...] * dl_ref[...])
    row = lax.broadcasted_iota(jnp.int32, decay.shape, 0)
    hf = _mm3(h, w3f_ref[...]) * decay
    hb = jnp.where(row == 0, 0.0, _mm3(h, w3b_ref[...]) * decay)
    _put_cols(p_ref, hf + hb)
    _put_cols(q_ref, hf - hb)
    pa_re, pa_im, pb_re, _ = _rfft_packed(p_ref, tabs)
    _, qa_im, _, qb_im = _rfft_packed(q_ref, tabs)
    row0 = lax.broadcasted_iota(jnp.int32, pa_re.shape, 0) == 0
    inv_n = 1.0 / (2 * n_tok)
    w_re = jnp.where(row0, inv_n, 2.0 * inv_n)
    kar_ref[...] = w_re * pa_re
    kbr_ref[...] = w_re * pb_re
    kai_ref[...] = (2.0 * inv_n) * _set_row0(qa_im, pa_im[0:1])
    kbi_ref[...] = (2.0 * inv_n) * qb_im


def _hyena_filters(n_tok, tabs, hy_f_w1, hy_f_b1, hy_f_w2, hy_f_b2, hy_f_w3, hy_freq):
    n_half = n_tok // 2
    tt = np.linspace(0.0, 1.0, n_tok)[:, None]
    ang = (2.0 * np.pi * np.arange(n_tok) / n_tok)[:, None] * np.linspace(1e-4, HY_BANDS - 1, HY_BANDS)[None, :]
    z = np.concatenate([tt, np.cos(ang), -np.sin(ang), np.zeros((n_tok, LANES - HY_EMB))], axis=-1)
    z, t = jnp.asarray(z, F32), jnp.asarray(tt, F32)
    deltas = jnp.asarray(np.abs(np.linspace(HY_MIN_DECAY, HY_MAX_DECAY, D_HY))[None, :], F32)
    w1 = jnp.pad(hy_f_w1, ((0, LANES - HY_EMB), (0, 0)))
    ncb = D_HY // HY_CW
    full = lambda j: (0, 0)
    out_sd = jax.ShapeDtypeStruct((n_half, 2 * D_HY), F32)
    tab_specs = [pl.BlockSpec(t_.shape, full, pipeline_mode=pl.Buffered(1)) for t_ in tabs]
    return pl.pallas_call(
        functools.partial(_filt_kernel, n_tok, len(tabs)),
        grid=(2 * ncb,),
        in_specs=[pl.BlockSpec((n_tok, LANES), full),
                  pl.BlockSpec((n_tok, 1), full),
                  pl.BlockSpec((LANES, HY_HID), full),
                  pl.BlockSpec((1, HY_HID), full),
                  pl.BlockSpec((HY_HID, HY_HID), full),
                  pl.BlockSpec((1, HY_HID), full),
                  pl.BlockSpec((1, HY_HID), full),
                  pl.BlockSpec((HY_HID, HY_CW), lambda j: (0, 2 * ncb * (j // ncb) + j % ncb)),
                  pl.BlockSpec((HY_HID, HY_CW), lambda j: (0, 2 * ncb * (j // ncb) + ncb + j % ncb)),
                  pl.BlockSpec((1, HY_CW), lambda j: (0, j % ncb))] + tab_specs,
        out_specs=[pl.BlockSpec((n_half, HY_CW), lambda j: (0, j))] * 4,
        out_shape=[out_sd] * 4,
        scratch_shapes=[pltpu.VMEM((HY_CW // LANES, n_tok, LANES), F32)] * 2,
        compiler_params=_cparams(("arbitrary",)),
        name=f"filt{n_tok}",
    )(z, t, w1, hy_f_b1.reshape(1, -1), hy_f_w2, hy_f_b2.reshape(1, -1), hy_freq.reshape(1, -1),
      hy_f_w3, hy_f_w3, deltas, *tabs)


def _hyena_kernel(n_tab, pv_ref, p1_ref, p2_ref, cwv_ref, cw1_ref, cw2_ref, cbv_ref, cb1_ref, cb2_ref,
                  fbias_ref, *rest):
    tabs = tuple(r[...] for r in rest[:n_tab])
    (kar0_ref, kai0_ref, kbr0_ref, kbi0_ref, kar1_ref, kai1_ref, kbr1_ref, kbi1_ref,
     o_ref, u_ref, y_ref) = rest[n_tab:]
    n_tok = pv_ref.shape[1]
    row = lax.broadcasted_iota(jnp.int32, (n_tok, pv_ref.shape[2]), 0)

    def short_conv(p_ref, cw_ref, cb_ref):
        p = p_ref[0]
        prev = jnp.where(row == 0, 0.0, pltpu.roll(p, 1, axis=0))
        nxt = jnp.where(row == n_tok - 1, 0.0, pltpu.roll(p, n_tok - 1, axis=0))
        return cb_ref[...] + prev * cw_ref[0:1, :] + p * cw_ref[1:2, :] + nxt * cw_ref[2:3, :]

    def fftconv(u, kar_ref, kai_ref, kbr_ref, kbi_ref, skip):
        _put_cols(u_ref, u)
        ua_re, ua_im, ub_re, ub_im = _rfft_packed(u_ref, tabs)
        ka_re, ka_im, kb_re, kb_im = kar_ref[...], kai_ref[...], kbr_ref[...], kbi_ref[...]
        zero_row = jnp.zeros_like(ka_im[0:1])
        kaz = _set_row0(ka_im, zero_row)
        kbz = _set_row0(kb_im, zero_row)
        ya_re = ua_re * ka_re - ua_im * kaz
        yb_re = ub_re * kb_re - ub_im * kbz
        h_re = ua_im[0:1] * ka_im[0:1] - ub_im[0:1] * kb_im[0:1]
        h_im = ua_im[0:1] * kb_im[0:1] + ub_im[0:1] * ka_im[0:1]
        ya_im = _set_row0(ua_re * ka_im + ua_im * ka_re, h_re)
        yb_im = _set_row0(ub_re * kb_im + ub_im * kb_re, h_im)
        _irfft_packed(y_ref, ya_re, ya_im, yb_re, yb_im, tabs)
        return _get_cols(y_ref) + u * skip

    v = short_conv(pv_ref, cwv_ref, cbv_ref)
    x1 = short_conv(p1_ref, cw1_ref, cb1_ref)
    z = x1 * fftconv(v, kar0_ref, kai0_ref, kbr0_ref, kbi0_ref, fbias_ref[0:1, :])
    x2 = short_conv(p2_ref, cw2_ref, cb2_ref)
    o_ref[0] = x2 * fftconv(z, kar1_ref, kai1_ref, kbr1_ref, kbi1_ref, fbias_ref[1:2, :])


def _hyena(proj_hy, tabs, filt, hy_conv_w, hy_conv_b, hy_fbias):
    bsz, n_tok, _ = proj_hy.shape
    n_half = n_tok // 2
    ncb = D_HY // HY_CW
    cb = hy_conv_b.reshape(1, -1)
    tab_specs = [pl.BlockSpec(t.shape, lambda b, c: (0, 0), pipeline_mode=pl.Buffered(1)) for t in tabs]

    def pspec(k):
        return pl.BlockSpec((1, n_tok, HY_CW), lambda b, c: (b, 0, k * ncb + c))

    def cwspec(k):
        return pl.BlockSpec((3, HY_CW), lambda b, c: (0, k * ncb + c))

    def cbspec(k):
        return pl.BlockSpec((1, HY_CW), lambda b, c: (0, k * ncb + c))

    def fspec(o):
        mode = pl.Buffered(1) if ncb == 1 else None
        return pl.BlockSpec((n_half, HY_CW), lambda b, c: (0, o * ncb + c), pipeline_mode=mode)

    return pl.pallas_call(
        functools.partial(_hyena_kernel, len(tabs)),
        grid=(bsz, ncb),
        in_specs=[pspec(0), pspec(1), pspec(2), cwspec(0), cwspec(1), cwspec(2),
                  cbspec(0), cbspec(1), cbspec(2),
                  pl.BlockSpec((2, HY_CW), lambda b, c: (0, c))] + tab_specs + [fspec(0)] * 4 + [fspec(1)] * 4,
        out_specs=pl.BlockSpec((1, n_tok, HY_CW), lambda b, c: (b, 0, c)),
        out_shape=jax.ShapeDtypeStruct((bsz, n_tok, D_HY), F32),
        scratch_shapes=[pltpu.VMEM((HY_CW // LANES, n_tok, LANES), F32)] * 2,
        compiler_params=_cparams(("arbitrary", "arbitrary")),
        name=f"hyena{n_tok}",
    )(proj_hy, proj_hy, proj_hy, hy_conv_w, hy_conv_w, hy_conv_w, cb, cb, cb, hy_fbias,
      *tabs, *filt, *filt)


def _s5ops_kernel(*refs):
    for g in range(S5OPS_GB):
        _s5ops_group(g, *refs)


def _s5ops_group(g, are_ref, aim_ref, ldt_ref, btr_ref, bti_ref, cre_ref, cim_ref, d_ref,
                 mt_ref, erh_ref, erl_ref, eih_ref, eil_ref, gr_ref, gi_ref, atr_ref, ati_ref,
                 er_ref, ei_ref):
    a_re, a_im = are_ref[g], aim_ref[g]
    dt = jnp.exp(ldt_ref[g])
    mag = jnp.exp(a_re * dt)
    ab_re = mag * jnp.cos(a_im * dt)
    ab_im = mag * jnp.sin(a_im * dt)
    n_re, n_im = ab_re - 1.0, ab_im
    den = a_re * a_re + a_im * a_im
    q_re = (n_re * a_re + n_im * a_im) / den
    q_im = (n_im * a_re - n_re * a_im) / den
    bt_re, bt_im = btr_ref[g], bti_ref[g]
    bb_re = q_re * bt_re - q_im * bt_im
    bb_im = q_re * bt_im + q_im * bt_re
    c_re, c_im = cre_ref[g, 0:S5_CH, :], cim_ref[g, 0:S5_CH, :]
    pw = [(jnp.ones_like(ab_re), jnp.zeros_like(ab_re))]
    for _ in range(S5_CHUNK):
        pr, pi = pw[-1]
        pw.append((pr * ab_re - pi * ab_im, pr * ab_im + pi * ab_re))
    lane = lax.broadcasted_iota(jnp.int32, ab_re.shape, 1)
    fwd = lane < S5_STATE
    for s in range(S5_CHUNK):
        e_re = jnp.where(fwd, pw[S5_CHUNK - 1 - s][0], pw[s][0])
        e_im = jnp.where(fwd, pw[S5_CHUNK - 1 - s][1], pw[s][1])
        er_ref[g, pl.ds(S5_CH * s, S5_CH), :] = e_re * bb_re - e_im * bb_im
        ei_ref[g, pl.ds(S5_CH * s, S5_CH), :] = e_re * bb_im + e_im * bb_re
        g_re = jnp.where(fwd, pw[s + 1][0], pw[S5_CHUNK - s][0])
        g_im = jnp.where(fwd, pw[s + 1][1], pw[S5_CHUNK - s][1])
        gr_ref[g, pl.ds(S5_CH * s, S5_CH), :] = (c_re * g_re - c_im * g_im).astype(BF16)
        gi_ref[g, pl.ds(S5_CH * s, S5_CH), :] = (-(c_re * g_im + c_im * g_re)).astype(BF16)
    atr_ref[g] = pw[S5_CHUNK][0]
    ati_ref[g] = pw[S5_CHUNK][1]
    er, ei = er_ref[g], ei_ref[g]
    erh_ref[g], erl_ref[g] = _split(er)
    eih_ref[g], eil_ref[g] = _split(ei)
    lane2 = lax.broadcasted_iota(jnp.int32, er.shape, 1)
    row2 = lax.broadcasted_iota(jnp.int32, er.shape, 0)
    f2 = lane2 < S5_STATE
    zero = jnp.zeros_like(er)

    cp_re, cp_im = cre_ref[g], cim_ref[g]
    kf = _mm3_t(jnp.where(f2, er, zero), cp_re) - _mm3_t(jnp.where(f2, ei, zero), cp_im)
    kb = _mm3_t(jnp.where(f2, zero, er), cp_re) - _mm3_t(jnp.where(f2, zero, ei), cp_im)
    d_row = d_ref[g]
    steps_per_vreg = LANES // S5_CH
    for half in range(S5_CHUNK // steps_per_vreg):
        acc = zero
        for tt in range(steps_per_vreg):
            t = half * steps_per_vreg + tt
            nf = S5_CH * (S5_CHUNK - 1 - t)
            nb = S5_CH * t
            col_f = jnp.concatenate([kf[nf:], zero[:nf]], axis=0) if nf else kf
            col_b = jnp.concatenate([zero[:nb], kb[:S5_ROW - nb]], axis=0) if nb else kb
            diag = jnp.where((row2 // S5_CH == t) & (row2 % S5_CH == lane2), d_row, 0.0)
            col = col_f + col_b + diag
            r = pltpu.roll(col, S5_CH * tt, axis=1) if tt else col
            acc = jnp.where((lane2 >= S5_CH * tt) & (lane2 < S5_CH * (tt + 1)), r, acc)
        mt_ref[g, :, LANES * half:LANES * (half + 1)] = acc.astype(BF16)


def _s5_operators(s5_a_re, s5_a_im, s5_log_dt, s5_b_re, s5_b_im, s5_c_re, s5_c_im, s5_d):
    g, p, h = S5_GROUPS, S5_STATE, S5_CH
    cat = lambda x: jnp.concatenate([x[0], x[1]], axis=-1)
    a_re = cat(s5_a_re).reshape(g, 1, 2 * p)
    a_im = cat(s5_a_im).reshape(g, 1, 2 * p)
    ldt = cat(jnp.broadcast_to(s5_log_dt[:, :, None], (2, g, p))).reshape(g, 1, 2 * p)
    bt_re = cat(jnp.swapaxes(s5_b_re, -1, -2))
    bt_im = cat(jnp.swapaxes(s5_b_im, -1, -2))
    cpad = lambda c: jnp.pad(jnp.concatenate([c, c], axis=-1), ((0, 0), (0, LANES - h), (0, 0)))
    c_re, c_im = cpad(s5_c_re), cpad(s5_c_im)
    d_row = jnp.pad(s5_d.reshape(g, 1, h), ((0, 0), (0, 0), (0, LANES - h)))
    vec = pl.BlockSpec((S5OPS_GB, 1, 2 * p), lambda i: (i, 0, 0))
    hp = pl.BlockSpec((S5OPS_GB, h, 2 * p), lambda i: (i, 0, 0))
    sq = pl.BlockSpec((S5OPS_GB, LANES, 2 * p), lambda i: (i, 0, 0))
    big = pl.BlockSpec((S5OPS_GB, S5_ROW, 2 * p), lambda i: (i, 0, 0))
    mts = pl.BlockSpec((S5OPS_GB, S5_ROW, S5_ROW), lambda i: (i, 0, 0))
    big_sd = jax.ShapeDtypeStruct((g, S5_ROW, 2 * p), BF16)
    vec_sd = jax.ShapeDtypeStruct((g, 1, 2 * p), F32)
    return pl.pallas_call(
        _s5ops_kernel,
        grid=(g // S5OPS_GB,),
        in_specs=[vec, vec, vec, hp, hp, sq, sq, vec],
        out_specs=[mts, big, big, big, big, big, big, vec, vec],
        out_shape=[jax.ShapeDtypeStruct((g, S5_ROW, S5_ROW), BF16)] + [big_sd] * 6 + [vec_sd, vec_sd],
        scratch_shapes=[pltpu.VMEM((S5OPS_GB, S5_ROW, 2 * p), F32)] * 2,
        compiler_params=_cparams(("arbitrary",)),
        name="s5ops",
    )(a_re, a_im, ldt, bt_re, bt_im, c_re, c_im, d_row)


def _block_transpose(xs):
    n = len(xs)
    lane = lax.broadcasted_iota(jnp.int32, xs[0].shape, 1)
    xs = list(xs)
    d = n // 2
    while d:
        keep = ((lane // S5_CH) & d) == 0
        for i in range(n):
            if i & d:
                continue
            lo, hi = xs[i], xs[i + d]
            xs[i] = jnp.where(keep, lo, pltpu.roll(hi, S5_CH * d, axis=1))
            xs[i + d] = jnp.where(keep, pltpu.roll(lo, LANES - S5_CH * d, axis=1), hi)
        d //= 2
    return xs


def _s5_kernel(bsz, n_chunks, u_ref, mt_ref, erh_ref, erl_ref, eih_ref, eil_ref, gr_ref, gi_ref,
               atr_ref, ati_ref, h0r_ref, h0i_ref, y_ref, fr_ref, fi_ref,
               ua_ref, ub_ref, ya_ref, yb_ref, sr_ref, si_ref, xfr_ref, xfi_ref, xbr_ref, xbi_ref):
    nc = n_chunks
    spv = LANES // S5_CH
    rsub = min(nc, 32)

    def to_chunks(b, carry):
        for half, dst in ((0, ua_ref), (1, ub_ref)):
            for r0 in range(0, nc, rsub):
                xs = [u_ref[b, pl.ds(S5_CHUNK * r0 + half * spv + tt, rsub, stride=S5_CHUNK), :]
                      for tt in range(spv)]
                for k, blk in enumerate(_block_transpose(xs)):
                    dst[k, pl.ds(r0 * bsz + b, rsub, stride=bsz), :] = blk
        return carry

    lax.fori_loop(0, bsz, to_chunks, 0, unroll=2)

    lane = lax.broadcasted_iota(jnp.int32, (bsz, 2 * S5_STATE), 1)
    fwd = lane < S5_STATE
    lane_all = lax.broadcasted_iota(jnp.int32, (bsz * nc, 2 * S5_STATE), 1)
    fwd_all = lane_all < S5_STATE

    def group(k, slot):
        u = jnp.concatenate([ua_ref[k], ub_ref[k]], axis=1)
        uh, ul = _split(u)
        sr_ref[slot] = _dot(uh, erh_ref[k]) + _dot(ul, erh_ref[k]) + _dot(uh, erl_ref[k])
        si_ref[slot] = _dot(uh, eih_ref[k]) + _dot(ul, eih_ref[k]) + _dot(uh, eil_ref[k])
        at_re, at_im = atr_ref[k], ati_ref[k]
        y_intra = _dot(uh, mt_ref[k])

        def step(i, xc):
            x_re, x_im = xc
            rf = pl.ds(pl.multiple_of(i * bsz, bsz), bsz)
            rb = pl.ds(pl.multiple_of((nc - 1 - i) * bsz, bsz), bsz)
            xfr_ref[slot, rf, :] = x_re
            xfi_ref[slot, rf, :] = x_im
            xbr_ref[slot, rb, :] = x_re
            xbi_ref[slot, rb, :] = x_im
            s_re = jnp.where(fwd, sr_ref[slot, rf, :], sr_ref[slot, rb, :])
            s_im = jnp.where(fwd, si_ref[slot, rf, :], si_ref[slot, rb, :])
            return (at_re * x_re - at_im * x_im + s_re, at_re * x_im + at_im * x_re + s_im)

        x_re, x_im = lax.fori_loop(0, nc, step, (h0r_ref[k], h0i_ref[k]), unroll=True)
        fr_ref[k] = x_re
        fi_ref[k] = x_im
        xp_re = jnp.where(fwd_all, xfr_ref[slot], xbr_ref[slot]).astype(BF16)
        xp_im = jnp.where(fwd_all, xfi_ref[slot], xbi_ref[slot]).astype(BF16)
        y = y_intra + _dot_t(xp_re, gr_ref[k]) + _dot_t(xp_im, gi_ref[k])
        ya_ref[k] = y[:, :LANES]
        yb_ref[k] = y[:, LANES:]

    def group_pair(j, carry):
        group(2 * j, 0)
        group(2 * j + 1, 1)
        return carry

    lax.fori_loop(0, S5_GB // 2, group_pair, 0)

    def to_tokens(b, carry):
        for half, src in ((0, ya_ref), (1, yb_ref)):
            for r0 in range(0, nc, rsub):
                ys = [src[k, pl.ds(r0 * bsz + b, rsub, stride=bsz), :] for k in range(S5_GB)]
                for tt, blk in enumerate(_block_transpose(ys)):
                    y_ref[b, pl.ds(S5_CHUNK * r0 + half * spv + tt, rsub, stride=S5_CHUNK), :] = blk
        return carry

    lax.fori_loop(0, bsz, to_tokens, 0, unroll=2)


def _s5(u, ops, h0_re, h0_im):
    bsz, n_tok, _ = u.shape
    g, p = S5_GROUPS, S5_STATE
    nc = n_tok // S5_CHUNK
    rows = nc * bsz
    tok = pl.BlockSpec((bsz, n_tok, LANES), lambda j: (0, 0, j))
    gspec = lambda shape: pl.BlockSpec((S5_GB,) + shape, lambda j: (j, 0, 0))
    op = gspec((S5_ROW, 2 * p))
    return pl.pallas_call(
        functools.partial(_s5_kernel, bsz, nc),
        grid=(g // S5_GB,),
        in_specs=[tok, gspec((S5_ROW, S5_ROW)), op, op, op, op, op, op,
                  gspec((1, 2 * p)), gspec((1, 2 * p)), gspec((bsz, 2 * p)), gspec((bsz, 2 * p))],
        out_specs=[tok, gspec((bsz, 2 * p)), gspec((bsz, 2 * p))],
        out_shape=[jax.ShapeDtypeStruct((bsz, n_tok, D_S5), F32),
                   jax.ShapeDtypeStruct((g, bsz, 2 * p), F32),
                   jax.ShapeDtypeStruct((g, bsz, 2 * p), F32)],
        scratch_shapes=([pltpu.VMEM((S5_GB, rows, LANES), F32)] * 4
                        + [pltpu.VMEM((2, rows, 2 * p), F32)] * 6),
        compiler_params=_cparams(("arbitrary",)),
        name=f"s5_{n_tok}",
    )(u, *ops, h0_re, h0_im)


def _in_kernel(has_pos, *refs):
    if has_pos:
        x_ref, pos_ref, mod_ref, w_ref, hy_ref, s5_ref = refs
        x = x_ref[0] + pos_ref[...]
    else:
        x_ref, mod_ref, w_ref, hy_ref, s5_ref = refs
        x = x_ref[0]
    sh1 = mod_ref[0, :, 0:D_MODEL]
    sc1 = mod_ref[0, :, D_MODEL:2 * D_MODEL]
    h = _norm(x) * (1.0 + sc1) + sh1
    proj = _dot(h.astype(BF16), w_ref[...].astype(BF16))
    hy_ref[0] = proj[:, :3 * D_HY]
    s5_ref[0] = proj[:, 3 * D_HY:]


def _in_proj(x3, pos, mod3, w_in, tm):
    nb, lt, _ = x3.shape
    has_pos = pos is not None
    per_batch = mod3.shape[0] > 1
    midx = (lambda b, i: (b, 0, 0)) if per_batch else (lambda b, i: (0, 0, 0))
    in_specs = [pl.BlockSpec((1, tm, D_MODEL), lambda b, i: (b, i, 0))]
    args = [x3]
    if has_pos:
        in_specs.append(pl.BlockSpec((tm, D_MODEL), lambda b, i: (i, 0)))
        args.append(pos)
    in_specs += [pl.BlockSpec((1, 1, 6 * D_MODEL), midx),
                 pl.BlockSpec((D_MODEL, 3 * D_HY + D_S5), lambda b, i: (0, 0), pipeline_mode=pl.Buffered(1))]
    args += [mod3, w_in]
    return pl.pallas_call(
        functools.partial(_in_kernel, has_pos),
        grid=(nb, lt // tm),
        in_specs=in_specs,
        out_specs=[pl.BlockSpec((1, tm, 3 * D_HY), lambda b, i: (b, i, 0)),
                   pl.BlockSpec((1, tm, D_S5), lambda b, i: (b, i, 0))],
        out_shape=[jax.ShapeDtypeStruct((nb, lt, 3 * D_HY), F32),
                   jax.ShapeDtypeStruct((nb, lt, D_S5), F32)],
        compiler_params=_cparams(("arbitrary", "arbitrary")),
        name=f"in_proj{nb}",
    )(*args)


def _route(logits):
    lane = lax.broadcasted_iota(jnp.int32, logits.shape, 1)
    lane_f = lane.astype(F32)
    neg = -jnp.inf
    big = float(LANES)
    m1 = (lane >= N_EXPERTS) & (lane < N_EXPERTS + N_EGROUPS)
    l1 = jnp.where(m1, logits, neg)
    top1 = jnp.max(l1, axis=-1, keepdims=True)
    grp = jnp.min(jnp.where(l1 == top1, lane_f, big), axis=-1, keepdims=True) - float(N_EXPERTS)
    den = jnp.sum(jnp.where(m1, jnp.exp(logits - top1), 0.0), axis=-1, keepdims=True)
    p_grp = 1.0 / den
    lo = grp * float(N_EPG)
    m2 = (lane_f >= lo) & (lane_f < lo + float(N_EPG))
    l2 = jnp.where(m2, logits, neg)
    v1 = jnp.max(l2, axis=-1, keepdims=True)
    i1 = jnp.min(jnp.where(l2 == v1, lane_f, big), axis=-1, keepdims=True)
    l2b = jnp.where(lane_f == i1, neg, l2)
    v2 = jnp.max(l2b, axis=-1, keepdims=True)
    i2 = jnp.min(jnp.where(l2b == v2, lane_f, big), axis=-1, keepdims=True)
    e = jnp.exp(v2 - v1)
    w1 = 1.0 / (1.0 + e)
    w2 = e / (1.0 + e)
    gates = jnp.where(lane_f == i1, w1 * p_grp, 0.0) + jnp.where(lane_f == i2, w2 * p_grp, 0.0)
    return jnp.where(lane_f == grp + float(N_EXPERTS), 1.0, gates)


def _out_kernel(n_ctx_blocks, xc_ref, xl_ref, pos_ref, yhyc_ref, yhyl_ref, ys5c_ref, ys5l_ref, mod_ref,
                wglu_ref, bglu_ref, ong_ref, wout_ref, ln1g_ref, ln1b_ref, wrh_ref, wrl_ref, br_ref,
                x1_ref, h2_ref, gate_ref, cnt_ref):
    is_ctx = pl.program_id(0) < n_ctx_blocks
    x = jnp.where(is_ctx, xc_ref[...], xl_ref[...] + pos_ref[...])
    y = jnp.where(is_ctx, ys5c_ref[...], ys5l_ref[...])
    y_hy = jnp.where(is_ctx, yhyc_ref[...], yhyl_ref[...])
    s5 = jax.nn.gelu(y) * jax.nn.sigmoid(_dot(y.astype(BF16), wglu_ref[...]) + bglu_ref[...])
    m_hy = _rms(y_hy) * ong_ref[:, 0:D_HY]
    m_s5 = _rms(s5) * ong_ref[:, D_HY:]
    o = (_dot(m_hy.astype(BF16), wout_ref[0:D_HY, :]) + _dot(m_s5.astype(BF16), wout_ref[D_HY:, :]))
    g1 = mod_ref[0, :, 2 * D_MODEL:3 * D_MODEL]
    sh2 = mod_ref[0, :, 3 * D_MODEL:4 * D_MODEL]
    sc2 = mod_ref[0, :, 4 * D_MODEL:5 * D_MODEL]
    x1 = _norm(ALPHA * x + g1 * o) * ln1g_ref[...] + ln1b_ref[...]
    x1_ref[...] = x1
    h2 = _norm(x1) * (1.0 + sc2) + sh2
    h2_ref[...] = h2.astype(BF16)
    hh, hl = _split(h2)
    logits = (_dot(hh, wrh_ref[...]) + _dot(hl, wrh_ref[...]) + _dot(hh, wrl_ref[...]) + br_ref[...])
    gates = _route(logits)
    gate_ref[...] = gates
    cnt_ref[0] = jnp.sum(gates, axis=0, keepdims=True)


def _out_proj(xc, xl, pos, yhy_c, yhy_l, ys5_c, ys5_l, mod, wglu_bf, bglu, ong, wout_bf, ln1g, ln1b,
              wr_hi, wr_lo, br, tm):
    n_ctx, n_lat = xc.shape[0], xl.shape[0]
    l_lat = pos.shape[0]
    ncb, nlb, npb = n_ctx // tm, n_lat // tm, l_lat // tm
    ctx = lambda w: pl.BlockSpec((tm, w), lambda i: (jnp.minimum(i, ncb - 1), 0))
    lat = lambda w: pl.BlockSpec((tm, w), lambda i: (jnp.maximum(i - ncb, 0), 0))
    full = lambda shape: pl.BlockSpec(shape, lambda i: (0,) * len(shape))
    out = lambda w: pl.BlockSpec((tm, w), lambda i: (i, 0))
    mod_idx = lambda i: (jnp.where(i < ncb, 0, 1 + jnp.maximum(i - ncb, 0) // npb), 0, 0)
    n_all = n_ctx + n_lat
    return pl.pallas_call(
        functools.partial(_out_kernel, ncb),
        grid=(ncb + nlb,),
        in_specs=[ctx(D_MODEL), lat(D_MODEL),
                  pl.BlockSpec((tm, D_MODEL), lambda i: (jnp.maximum(i - ncb, 0) % npb, 0)),
                  ctx(D_HY), lat(D_HY), ctx(D_S5), lat(D_S5),
                  pl.BlockSpec((1, 1, 6 * D_MODEL), mod_idx),
                  full((D_S5, D_S5)), full((1, D_S5)), full((1, D_MODEL)), full((D_MODEL, D_MODEL)),
                  full((1, D_MODEL)), full((1, D_MODEL)), full((D_MODEL, LANES)), full((D_MODEL, LANES)),
                  full((1, LANES))],
        out_specs=[out(D_MODEL), out(D_MODEL), out(LANES), pl.BlockSpec((1, 1, LANES), lambda i: (i, 0, 0))],
        out_shape=[jax.ShapeDtypeStruct((n_all, D_MODEL), F32),
                   jax.ShapeDtypeStruct((n_all, D_MODEL), BF16),
                   jax.ShapeDtypeStruct((n_all, LANES), F32),
                   jax.ShapeDtypeStruct((n_all // tm, 1, LANES), F32)],
        compiler_params=_cparams(("arbitrary",)),
        name="out_proj",
    )(xc, xl, pos, yhy_c, yhy_l, ys5_c, ys5_l, mod.reshape(mod.shape[0], 1, 6 * D_MODEL),
      wglu_bf, bglu, ong, wout_bf, ln1g, ln1b, wr_hi, wr_lo, br)


def _perm_t(gates, loc_ref, s):
    n = gates.shape[0]
    lane = lax.broadcasted_iota(jnp.int32, gates.shape, 1)
    oh = jnp.where((lane >= N_EXPERTS) & (lane < N_EXPERTS + N_EGROUPS), gates, 0.0)
    r = lax.broadcasted_iota(jnp.int32, (n, n), 0)
    c = lax.broadcasted_iota(jnp.int32, (n, n), 1)
    earlier = jnp.where(c < r, 1.0, 0.0).astype(BF16)
    cum = _dot(earlier, oh.astype(BF16))
    rank = jnp.sum(cum * oh, axis=-1, keepdims=True)
    lane1 = lax.broadcasted_iota(jnp.int32, (1, LANES), 1)
    locv = jnp.zeros((1, LANES), F32)
    for grp in range(N_EGROUPS):
        locv = jnp.where(lane1 == N_EXPERTS + grp, loc_ref[N_EGROUPS * s + grp].astype(F32), locv)
    dest = rank + jnp.sum(oh * locv, axis=-1, keepdims=True)
    slot = lax.broadcasted_iota(jnp.int32, (n, MOE_SLOTS), 1).astype(F32)
    return jnp.where(slot == dest, 1.0, 0.0)


def _segment_copies(s, loc_ref, len_ref, off_ref, make):
    for grp in range(N_EGROUPS):
        loc = loc_ref[N_EGROUPS * s + grp]
        off = off_ref[N_EGROUPS * s + grp]
        length = len_ref[N_EGROUPS * s + grp]
        n_big = length // MOE_BIG
        done = n_big * MOE_BIG

        def big(i, carry):
            make(pl.multiple_of(loc + MOE_BIG * i, MOE_UNIT), pl.multiple_of(off + MOE_BIG * i, MOE_UNIT), MOE_BIG)
            return carry

        def unit(i, carry):
            make(pl.multiple_of(loc + done + MOE_UNIT * i, MOE_UNIT),
                 pl.multiple_of(off + done + MOE_UNIT * i, MOE_UNIT), MOE_UNIT)
            return carry

        lax.fori_loop(0, n_big, big, 0)
        lax.fori_loop(0, (length - done) // MOE_UNIT, unit, 0)


def _pad_copies(pad_ref, n_blocks, zx_v, zg_v, xs_hbm, gs_hbm, sem, op):
    def unit(row, rows):
        getattr(pltpu.make_async_copy(zx_v.at[pl.ds(0, rows), :], xs_hbm.at[pl.ds(row, rows), :], sem.at[0]), op)()
        getattr(pltpu.make_async_copy(zg_v.at[pl.ds(0, rows), :], gs_hbm.at[pl.ds(row, rows), :], sem.at[1]), op)()

    for grp in range(N_EGROUPS):
        start = pad_ref[grp]

        def body(i, carry):
            unit(pl.multiple_of(start + MOE_UNIT * i, MOE_UNIT), MOE_UNIT)
            return carry

        lax.fori_loop(0, pad_ref[N_EGROUPS + grp], body, 0)

    def tail(b, carry):
        unit(pl.multiple_of(b * MOE_TM, MOE_TM), MOE_TM)
        return carry

    lax.fori_loop(pad_ref[2 * N_EGROUPS], n_blocks, tail, 0)


def _moe_sort_kernel(n_blocks, loc_ref, len_ref, off_ref, pad_ref, h_ref, gate_ref, xs_hbm, gs_hbm,
                     xs_v, gs_v, zx_v, zg_v, sem, zsem):
    s = pl.program_id(0)
    slot = s % 2

    @pl.when(s == 0)
    def _():
        zx_v[...] = jnp.zeros_like(zx_v)
        zg_v[...] = jnp.zeros_like(zg_v)
        _pad_copies(pad_ref, n_blocks, zx_v, zg_v, xs_hbm, gs_hbm, zsem, 'start')

    gates = gate_ref[...]
    p = _perm_t(gates, loc_ref, s).T.astype(BF16)
    xs_v[slot] = _dot(p, h_ref[...]).astype(BF16)
    g_hi = gates.astype(BF16)
    r1 = gates - g_hi.astype(F32)
    g_mid = r1.astype(BF16)
    g_lo = (r1 - g_mid.astype(F32)).astype(BF16)
    gs_v[slot] = _dot(p, g_hi) + _dot(p, g_mid) + _dot(p, g_lo)

    def copies(buf):
        def x_copy(lr, gr, rows):
            return pltpu.make_async_copy(xs_v.at[buf, pl.ds(lr, rows), :],
                                         xs_hbm.at[pl.ds(gr, rows), :], sem.at[0, buf])

        def g_copy(lr, gr, rows):
            return pltpu.make_async_copy(gs_v.at[buf, pl.ds(lr, rows), :],
                                         gs_hbm.at[pl.ds(gr, rows), :], sem.at[1, buf])

        def start(lr, gr, rows):
            x_copy(lr, gr, rows).start()
            g_copy(lr, gr, rows).start()

        def wait(lr, gr, rows):
            x_copy(lr, gr, rows).wait()
            g_copy(lr, gr, rows).wait()

        return start, wait

    _segment_copies(s, loc_ref, len_ref, off_ref, copies(slot)[0])

    @pl.when(s > 0)
    def _():
        _segment_copies(s - 1, loc_ref, len_ref, off_ref, copies(1 - slot)[1])

    @pl.when(s == pl.num_programs(0) - 1)
    def _():
        _segment_copies(s, loc_ref, len_ref, off_ref, copies(slot)[1])
        _pad_copies(pad_ref, n_blocks, zx_v, zg_v, xs_hbm, gs_hbm, zsem, 'wait')


def _moe_expert_kernel(bg_ref, nb_ref, xs_ref, gs_ref, wg_ref, wu_ref, wd_ref, o_ref):
    i = pl.program_id(0)

    @pl.when(i < nb_ref[0])
    def _():
        grp = bg_ref[i]
        x = xs_ref[...]
        gates = gs_ref[...]
        lane = lax.broadcasted_iota(jnp.int32, gates.shape, 1)
        acc = jnp.zeros(o_ref.shape, F32)
        for e in range(N_EPG):
            a = _dot(x, wg_ref[e].astype(BF16))
            u = _dot(x, wu_ref[e].astype(BF16))
            ge = jnp.sum(jnp.where(lane == N_EPG * grp + e, gates, 0.0), axis=-1, keepdims=True)
            hid = jax.nn.silu(a) * u * ge
            acc = acc + _dot(hid.astype(BF16), wd_ref[e].astype(BF16))
        o_ref[...] = acc.astype(BF16)

    @pl.when(i >= nb_ref[0])
    def _():
        o_ref[...] = jnp.zeros_like(o_ref)


def _moe_combine_kernel(loc_ref, len_ref, off_ref, gate_ref, x1_ref, mod_ref, ln2g_ref, ln2b_ref, o_hbm,
                        ctx_ref, lat_ref, o_v, sem, *, n_ctx_tiles):
    s = pl.program_id(0)
    slot = s % 2

    def copies(buf):
        def o_copy(lr, gr, rows):
            return pltpu.make_async_copy(o_hbm.at[pl.ds(gr, rows), :],
                                         o_v.at[buf, pl.ds(lr, rows), :], sem.at[buf])

        return (lambda lr, gr, rows: o_copy(lr, gr, rows).start()), (lambda lr, gr, rows: o_copy(lr, gr, rows).wait())

    @pl.when(s == 0)
    def _():
        o_v[...] = jnp.zeros_like(o_v)
        _segment_copies(s, loc_ref, len_ref, off_ref, copies(slot)[0])

    @pl.when(s + 1 < pl.num_programs(0))
    def _():
        _segment_copies(s + 1, loc_ref, len_ref, off_ref, copies(1 - slot)[0])

    pt = _perm_t(gate_ref[...], loc_ref, s).astype(BF16)
    _segment_copies(s, loc_ref, len_ref, off_ref, copies(slot)[1])
    f = _dot(pt, o_v[slot])
    g2 = mod_ref[0, :, 5 * D_MODEL:6 * D_MODEL]
    x2 = _norm(ALPHA * x1_ref[...] + g2 * f) * ln2g_ref[...] + ln2b_ref[...]

    @pl.when(s < n_ctx_tiles)
    def _():
        ctx_ref[...] = x2

    @pl.when(s >= n_ctx_tiles)
    def _():
        lat_ref[...] = x2


def _moe_plan(tile_counts, n_blocks):
    cnt = tile_counts[:, 0, N_EXPERTS:N_EXPERTS + N_EGROUPS].astype(jnp.int32)
    len16 = ((cnt + MOE_UNIT - 1) // MOE_UNIT) * MOE_UNIT
    loc = jnp.cumsum(len16, axis=1) - len16
    rows_g = jnp.sum(len16, axis=0)
    reg_g = ((rows_g + MOE_TM - 1) // MOE_TM) * MOE_TM
    reg_start = jnp.cumsum(reg_g) - reg_g
    off = reg_start[None, :] + jnp.cumsum(len16, axis=0) - len16
    blk_end = jnp.cumsum(reg_g // MOE_TM)
    bi = jnp.arange(n_blocks, dtype=jnp.int32)
    blk_group = jnp.minimum(jnp.sum((bi[:, None] >= blk_end[None, :]).astype(jnp.int32), axis=1),
                            N_EGROUPS - 1)
    flat = lambda a: a.reshape(-1).astype(jnp.int32)
    pads = jnp.concatenate([reg_start + rows_g, (reg_g - rows_g) // MOE_UNIT, blk_end[-1:]])
    return (flat(loc), flat(len16), flat(off), flat(pads), blk_group.astype(jnp.int32),
            blk_end[-1:].astype(jnp.int32))


def _moe(h2_all, gates_all, tile_counts, x1_all, mod, w_gate, w_up, w_down, ln2g, ln2b, n_ctx,
         tokens_per_mod_row):
    n_tok = h2_all.shape[0]
    n_tiles = n_tok // MOE_ST
    n_ctx_tiles = n_ctx // MOE_ST
    max_rows = n_tok + n_tiles * N_EGROUPS * (MOE_UNIT - 1) + N_EGROUPS * (MOE_TM - 1)
    n_blocks = -(-max_rows // MOE_TM)
    n_rows = n_blocks * MOE_TM
    loc, len16, off, pads, blk_group, n_used = _moe_plan(tile_counts, n_blocks)

    tile = lambda w: pl.BlockSpec((MOE_ST, w), lambda s, *_: (s, 0))
    anyspec = pl.BlockSpec(memory_space=pl.ANY)
    xs, gs = pl.pallas_call(
        functools.partial(_moe_sort_kernel, n_blocks),
        grid_spec=pltpu.PrefetchScalarGridSpec(
            num_scalar_prefetch=4, grid=(n_tiles,),
            in_specs=[tile(D_MODEL), tile(LANES)],
            out_specs=[anyspec, anyspec],
            scratch_shapes=[pltpu.VMEM((2, MOE_SLOTS, D_MODEL), BF16), pltpu.VMEM((2, MOE_SLOTS, LANES), F32),
                            pltpu.VMEM((MOE_TM, D_MODEL), BF16), pltpu.VMEM((MOE_TM, LANES), F32),
                            pltpu.SemaphoreType.DMA((2, 2)), pltpu.SemaphoreType.DMA((2,))]),
        out_shape=[jax.ShapeDtypeStruct((n_rows, D_MODEL), BF16),
                   jax.ShapeDtypeStruct((n_rows, LANES), F32)],
        compiler_params=_cparams(("arbitrary",)),
        name="moe_sort",
    )(loc, len16, off, pads, h2_all, gates_all)

    blk = lambda w: pl.BlockSpec((MOE_TM, w), lambda i, bg, nb: (jnp.minimum(i, nb[0] - 1), 0))
    wspec = lambda a, b, mode: pl.BlockSpec((N_EPG, a, b), lambda i, bg, nb: (bg[i], 0, 0),
                                            pipeline_mode=mode)
    o_sorted = pl.pallas_call(
        _moe_expert_kernel,
        grid_spec=pltpu.PrefetchScalarGridSpec(
            num_scalar_prefetch=2, grid=(n_blocks,),
            in_specs=[blk(D_MODEL), blk(LANES), wspec(D_MODEL, D_EXPERT, None),
                      wspec(D_MODEL, D_EXPERT, None), wspec(D_EXPERT, D_MODEL, None)],
            out_specs=pl.BlockSpec((MOE_TM, D_MODEL), lambda i, bg, nb: (i, 0))),
        out_shape=jax.ShapeDtypeStruct((n_rows, D_MODEL), BF16),
        compiler_params=_cparams(("arbitrary",)),
        name="moe_experts",
    )(blk_group, n_used, xs, gs, w_gate, w_up, w_down)

    lat_per_row = tokens_per_mod_row // MOE_ST

    def mod_idx(s, *_):
        return (jnp.where(s < n_ctx_tiles, 0, 1 + (s - n_ctx_tiles) // lat_per_row), 0, 0)

    vec = pl.BlockSpec((1, D_MODEL), lambda s, *_: (0, 0))
    return pl.pallas_call(
        functools.partial(_moe_combine_kernel, n_ctx_tiles=n_ctx_tiles),
        grid_spec=pltpu.PrefetchScalarGridSpec(
            num_scalar_prefetch=3, grid=(n_tiles,),
            in_specs=[tile(LANES), tile(D_MODEL), pl.BlockSpec((1, 1, 6 * D_MODEL), mod_idx), vec, vec,
                      anyspec],
            out_specs=[pl.BlockSpec((MOE_ST, D_MODEL), lambda s, *_: (jnp.minimum(s, n_ctx_tiles - 1), 0)),
                       pl.BlockSpec((MOE_ST, D_MODEL), lambda s, *_: (jnp.maximum(s - n_ctx_tiles, 0), 0))],
            scratch_shapes=[pltpu.VMEM((2, MOE_SLOTS, D_MODEL), BF16), pltpu.SemaphoreType.DMA((2,))]),
        out_shape=[jax.ShapeDtypeStruct((n_ctx, D_MODEL), F32),
                   jax.ShapeDtypeStruct((n_tok - n_ctx, D_MODEL), F32)],
        compiler_params=_cparams(("arbitrary",)),
        name="moe_combine",
    )(loc, len16, off, gates_all, x1_all, mod.reshape(mod.shape[0], 1, 6 * D_MODEL), ln2g, ln2b, o_sorted)


def _grid_pos_embed(n_tokens):
    rows = n_tokens // GRID_W
    row = np.repeat(np.arange(rows, dtype=np.float64), GRID_W)
    col = np.tile(np.arange(GRID_W, dtype=np.float64), rows)
    quarter = D_MODEL // 4
    omega = 1.0 / (POS_BASE ** (np.arange(quarter, dtype=np.float64) / quarter))
    er = row[:, None] * omega
    ec = col[:, None] * omega
    return jnp.asarray(np.concatenate([np.sin(er), np.cos(er), np.sin(ec), np.cos(ec)], axis=-1), F32)


def _mixers(x, pos, mod3, h0_re, h0_im, tabs, filt, s5ops, wts, tm):
    bsz, n_tok, _ = x.shape
    shared = mod3.shape[0] == 1
    x3 = x.reshape(1, bsz * n_tok, D_MODEL) if shared else x
    proj_hy, u_s5 = _in_proj(x3, pos, mod3, wts['w_in'], tm)
    y_hy = _hyena(proj_hy.reshape(bsz, n_tok, 3 * D_HY), tabs, filt,
                  wts['hy_conv_w'], wts['hy_conv_b'], wts['hy_fbias'])
    y_s5, f_re, f_im = _s5(u_s5.reshape(bsz, n_tok, D_S5), s5ops, h0_re, h0_im)
    return y_hy.reshape(bsz * n_tok, D_HY), y_s5.reshape(bsz * n_tok, D_S5), f_re, f_im


def kernel(x_prompt, x_sample, state_s5_re, state_s5_im, c, c_ctx, w_ada, b_ada, w_in, hy_conv_w, hy_conv_b, hy_f_w1, hy_f_b1, hy_f_w2, hy_f_b2, hy_f_w3, hy_freq, hy_fbias, s5_a_re, s5_a_im, s5_log_dt, s5_b_re, s5_b_im, s5_c_re, s5_c_im, s5_d, s5_w_glu, s5_b_glu, out_norm_g, w_out, ln1_g, ln1_b, moe_w_r1, moe_b_r1, moe_w_r2, moe_b_r2, moe_w_gate, moe_w_up, moe_w_down, ln2_g, ln2_b):
    b_ctx, l_ctx, _ = x_prompt.shape
    b_lat, l_lat, _ = x_sample.shape
    g, p = S5_GROUPS, S5_STATE
    assert w_ada.shape[0] == 1, "single-layer trunk"
    l = 0

    nrow = 16
    cond = jnp.concatenate([c_ctx[None, :], c, jnp.zeros((nrow - 1 - b_lat, D_MODEL), F32)], axis=0)
    mod = _ada(cond, w_ada[l], b_ada[l])
    mod_ctx = mod[0:1].reshape(1, 1, 6 * D_MODEL)
    mod_lat = mod[1:1 + b_lat].reshape(b_lat, 1, 6 * D_MODEL)

    wr = jnp.concatenate([moe_w_r2[l].transpose(1, 0, 2).reshape(D_MODEL, N_EXPERTS), moe_w_r1[l]], axis=1)
    wr = jnp.pad(wr, ((0, 0), (0, LANES - wr.shape[1])))
    br = jnp.concatenate([moe_b_r2[l].reshape(-1), moe_b_r1[l]])
    br = jnp.pad(br, (0, LANES - br.shape[0])).reshape(1, LANES)
    wr_hi, wr_lo = _split(wr)

    wts = {
        'w_in': w_in[l], 'hy_conv_w': hy_conv_w[l], 'hy_conv_b': hy_conv_b[l],
        'hy_fbias': hy_fbias[l], 'w_glu': s5_w_glu[l].astype(BF16), 'b_glu': s5_b_glu[l].reshape(1, -1),
        'out_norm_g': out_norm_g[l].reshape(1, -1), 'w_out': w_out[l].astype(BF16),
        'ln1_g': ln1_g[l].reshape(1, -1), 'ln1_b': ln1_b[l].reshape(1, -1),
        'wr_hi': wr_hi, 'wr_lo': wr_lo, 'br': br,
        'w_gate': moe_w_gate[l], 'w_up': moe_w_up[l], 'w_down': moe_w_down[l],
        'ln2_g': ln2_g[l].reshape(1, -1), 'ln2_b': ln2_b[l].reshape(1, -1),
    }

    s5ops = _s5_operators(s5_a_re[l], s5_a_im[l], s5_log_dt[l], s5_b_re[l], s5_b_im[l],
                          s5_c_re[l], s5_c_im[l], s5_d[l])
    tabs_ctx = _tables(l_ctx)
    tabs_lat = _tables(l_lat)
    filt_args = (hy_f_w1[l], hy_f_b1[l], hy_f_w2[l], hy_f_b2[l], hy_f_w3[l], hy_freq[l])
    filt_ctx = _hyena_filters(l_ctx, tabs_ctx, *filt_args)
    filt_lat = _hyena_filters(l_lat, tabs_lat, *filt_args)

    zero = jnp.zeros((g, b_ctx, 2 * p), F32)
    yhy_c, ys5_c, f_re, f_im = _mixers(x_prompt, None, mod_ctx, zero, zero, tabs_ctx, filt_ctx, s5ops, wts, 1024)
    unpack = lambda f: f.reshape(g, b_ctx, 2, p).transpose(1, 2, 0, 3)[:, None]
    new_re, new_im = unpack(f_re), unpack(f_im)

    pack = lambda s: s[:, l].transpose(2, 0, 1, 3).reshape(g, b_lat, 2 * p)
    pos = _grid_pos_embed(l_lat)
    yhy_l, ys5_l, _, _ = _mixers(x_sample, pos, mod_lat, pack(state_s5_re), pack(state_s5_im),
                                 tabs_lat, filt_lat, s5ops, wts, 1024)

    n_ctx = b_ctx * l_ctx
    x1_all, h2_all, gates_all, tile_counts = _out_proj(
        x_prompt.reshape(n_ctx, D_MODEL), x_sample.reshape(b_lat * l_lat, D_MODEL), pos,
        yhy_c, yhy_l, ys5_c, ys5_l, mod, wts['w_glu'], wts['b_glu'], wts['out_norm_g'], wts['w_out'],
        wts['ln1_g'], wts['ln1_b'], wts['wr_hi'], wts['wr_lo'], wts['br'], MOE_ST)
    y_ctx, y_lat = _moe(h2_all, gates_all, tile_counts, x1_all, mod,
                        wts['w_gate'], wts['w_up'], wts['w_down'], wts['ln2_g'], wts['ln2_b'],
                        n_ctx, l_lat)
    return (y_ctx.reshape(x_prompt.shape), y_lat.reshape(x_sample.shape), new_re, new_im)
```

```python
import functools
import math

import numpy as np
import jax
import jax.numpy as jnp
from jax import lax
from jax.experimental import pallas as pl
from jax.experimental.pallas import tpu as pltpu

F32 = jnp.float32
BF16 = jnp.bfloat16

D_MODEL = 1024
DEPTH = 1
GRID_W = 64
POS_BASE = 10000.0
D_HY = 512
D_S5 = 512
S5_CH = 16
S5_GROUPS = 32
S5_STATE = 64
S5_CHUNK = 16
S5_ROW = S5_CHUNK * S5_CH
HY_BANDS = 16
HY_EMB = 1 + 2 * HY_BANDS
HY_HID = 64
HY_MIN_DECAY = math.log(1e-2) / 1.5
HY_MAX_DECAY = math.log(1e-2) / 0.3
N_EGROUPS = 4
N_EPG = 4
N_EXPERTS = 16
D_EXPERT = 512
LN_EPS = 1e-5
ALPHA = (2.0 * DEPTH) ** 0.25
LANES = 128
S5_GB = LANES // S5_CH
S5OPS_GB = 4
HY_CW = 512
MOE_ST = 512
MOE_SLOTS = 640
MOE_UNIT = 16
MOE_BIG = 64
MOE_TM = 512
VMEM_LIMIT = 60000 * 1024


def _cparams(sem):
    return pltpu.CompilerParams(dimension_semantics=sem, vmem_limit_bytes=VMEM_LIMIT)


def _split(x):
    hi = x.astype(BF16)
    lo = (x - hi.astype(F32)).astype(BF16)
    return hi, lo


def _dot(a, b):
    return jnp.dot(a, b, preferred_element_type=F32)


def _dot_t(a, b):
    return lax.dot_general(a, b, (((1,), (1,)), ((), ())), preferred_element_type=F32)


def _mm3(a, b):
    ah, al = _split(a)
    bh, bl = _split(b)
    return _dot(ah, bh) + _dot(al, bh) + _dot(ah, bl)


def _mm3_t(a, b):
    ah, al = _split(a)
    bh, bl = _split(b)
    return _dot_t(ah, bh) + _dot_t(al, bh) + _dot_t(ah, bl)


def _norm(x):
    xc = x - jnp.mean(x, axis=-1, keepdims=True)
    return xc * lax.rsqrt(jnp.mean(xc * xc, axis=-1, keepdims=True) + LN_EPS)


def _rms(y):
    return y * lax.rsqrt(jnp.mean(y * y, axis=-1, keepdims=True) + LN_EPS)


def _ada_kernel(cond_ref, w_ref, b_ref, o_ref):
    c = jax.nn.silu(cond_ref[...])
    o_ref[...] = _mm3(c, w_ref[...]) + b_ref[...]


def _ada(cond, w_ada, b_ada):
    nb = cond.shape[0]
    n = w_ada.shape[1]
    tn = 1024
    return pl.pallas_call(
        _ada_kernel,
        grid=(n // tn,),
        in_specs=[pl.BlockSpec((nb, D_MODEL), lambda j: (0, 0)),
                  pl.BlockSpec((D_MODEL, tn), lambda j: (0, j)),
                  pl.BlockSpec((1, tn), lambda j: (0, j))],
        out_specs=pl.BlockSpec((nb, tn), lambda j: (0, j)),
        out_shape=jax.ShapeDtypeStruct((nb, n), F32),
        compiler_params=_cparams(("arbitrary",)),
        name="ada",
    )(cond, w_ada, b_ada.reshape(1, n))


RADIX2_MIN_HALF = 256


def _dft_tables(n_half):
    n = 2 * n_half
    idx = np.arange(n_half, dtype=np.int64)
    m = (idx[:, None] * idx[None, :]) % n
    ang = 2.0 * np.pi * m.astype(np.float64) / n
    cm = np.cos(ang)
    sm = -np.sin(ang)
    sm[0, :] = 1.0 - 2.0 * (idx % 2)
    return cm, sm


def _dense_tables(n_tok):
    n_half = n_tok // 2
    f = np.concatenate([np.arange(n_half), n_tok - np.arange(n_half)]).astype(np.int64)
    sidx = np.arange(n_tok, dtype=np.int64)
    ang = 2.0 * np.pi * ((f[:, None] * sidx[None, :]) % (2 * n_tok)).astype(np.float64) / (2 * n_tok)
    cd = np.cos(ang)
    sd = -np.sin(ang)
    half = 2.0 * np.pi * ((n_half * sidx) % (2 * n_tok)).astype(np.float64) / (2 * n_tok)
    sd[0, :] = np.cos(half)
    sd[n_half, :] = -np.sin(half)
    return cd, sd


def _tables(n_tok):
    n_half = n_tok // 2
    bf = lambda t: jnp.asarray(t.astype(np.float32)).astype(BF16)
    if n_half < RADIX2_MIN_HALF:
        cd, sd = _dense_tables(n_tok)
        return (bf(cd), bf(sd), bf(np.ascontiguousarray(cd.T)), bf(np.ascontiguousarray(sd.T)))
    cm, sm = _dft_tables(n_half)
    ang = np.pi * np.arange(n_half, dtype=np.float64) / n_tok
    tw = [jnp.asarray(np.broadcast_to(v[:, None], (n_half, HY_CW)).astype(np.float32))
          for v in (np.cos(ang), -np.sin(ang))]
    return (bf(cm), bf(sm), bf(cm), bf(np.ascontiguousarray(sm.T)), tw[0], tw[1])


def _put_cols(ref, x):
    for j in range(ref.shape[0]):
        ref[j] = x[:, LANES * j:LANES * (j + 1)]


def _get_cols(ref):
    return jnp.concatenate([ref[j] for j in range(ref.shape[0])], axis=1)


def _get_parity(ref, parity):
    n_half = ref.shape[1] // 2
    return jnp.concatenate([ref[j, pl.ds(parity, n_half, stride=2), :] for j in range(ref.shape[0])], axis=1)


def _put_parity(ref, parity, x):
    n_half = ref.shape[1] // 2
    for j in range(ref.shape[0]):
        ref[j, pl.ds(parity, n_half, stride=2), :] = x[:, LANES * j:LANES * (j + 1)]


def _set_row0(x, v):
    first = lax.broadcasted_iota(jnp.int32, (8, x.shape[1]), 0) == 0
    return jnp.concatenate([jnp.where(first, v, x[:8]), x[8:]], axis=0)


def _rfft_packed(x_ref, tabs):
    n_half = x_ref.shape[1] // 2
    if len(tabs) == 4:
        x = _get_cols(x_ref).astype(BF16)
        re, im = _dot(tabs[0], x), _dot(tabs[1], x)
        return re[:n_half], im[:n_half], re[n_half:], im[n_half:]
    cm, sm, _, _, tw_re, tw_im = tabs
    xe = _get_parity(x_ref, 0).astype(BF16)
    xo = _get_parity(x_ref, 1).astype(BF16)
    e_re, e_im = _dot(cm, xe), _dot(sm, xe)
    o_re, o_im = _dot(cm, xo), _dot(sm, xo)
    t_re = tw_re * o_re - tw_im * o_im
    t_im = tw_re * o_im + tw_im * o_re
    a_im = _set_row0(e_im + t_im, e_im[0:1])
    b_im = _set_row0(t_im - e_im, -o_im[0:1])
    return e_re + t_re, a_im, e_re - t_re, b_im


def _irfft_packed(y_ref, ya_re, ya_im, yb_re, yb_im, tabs):
    if len(tabs) == 4:
        y_re = jnp.concatenate([ya_re, yb_re], axis=0).astype(BF16)
        y_im = jnp.concatenate([ya_im, yb_im], axis=0).astype(BF16)
        _put_cols(y_ref, _dot(tabs[2], y_re) + _dot(tabs[3], y_im))
        return
    _, _, cm, st, tw_re, tw_im = tabs
    p_e = ya_re + yb_re
    q_e = _set_row0(ya_im - yb_im, ya_im[0:1])
    _put_parity(y_ref, 0, _dot(cm, p_e.astype(BF16)) + _dot(st, q_e.astype(BF16)))
    ra_re = ya_re * tw_re + ya_im * tw_im
    ra_im = ya_im * tw_re - ya_re * tw_im
    rb_re = yb_im * tw_im - yb_re * tw_re
    rb_im = -(yb_re * tw_im + yb_im * tw_re)
    p_o = ra_re + rb_re
    q_o = _set_row0(ra_im - rb_im, -yb_im[0:1])
    _put_parity(y_ref, 1, _dot(cm, p_o.astype(BF16)) + _dot(st, q_o.astype(BF16)))


def _filt_kernel(n_tok, n_tab, z_ref, t_ref, w1_ref, b1_ref, w2_ref, b2_ref, fr_ref, w3f_ref, w3b_ref,
                 dl_ref, *rest):
    tabs = tuple(r[...] for r in rest[:n_tab])
    kar_ref, kai_ref, kbr_ref, kbi_ref, p_ref, q_ref = rest[n_tab:]
    fr = fr_ref[...]
    h = jnp.sin(fr * (_mm3(z_ref[...], w1_ref[...]) + b1_ref[...]))
    h = jnp.sin(fr * (_mm3(h, w2_ref[...]) + b2_ref[...]))
    decay = jnp.exp(-t_ref[...] * dl_ref[...])
    row = lax.broadcasted_iota(jnp.int32, decay.shape, 0)
    hf = _mm3(h, w3f_ref[...]) * decay
    hb = jnp.where(row == 0, 0.0, _mm3(h, w3b_ref[...]) * decay)
    _put_cols(p_ref, hf + hb)
    _put_cols(q_ref, hf - hb)
    pa_re, pa_im, pb_re, _ = _rfft_packed(p_ref, tabs)
    _, qa_im, _, qb_im = _rfft_packed(q_ref, tabs)
    row0 = lax.broadcasted_iota(jnp.int32, pa_re.shape, 0) == 0
    inv_n = 1.0 / (2 * n_tok)
    w_re = jnp.where(row0, inv_n, 2.0 * inv_n)
    kar_ref[...] = w_re * pa_re
    kbr_ref[...] = w_re * pb_re
    kai_ref[...] = (2.0 * inv_n) * _set_row0(qa_im, pa_im[0:1])
    kbi_ref[...] = (2.0 * inv_n) * qb_im


def _hyena_filters(n_tok, tabs, hy_f_w1, hy_f_b1, hy_f_w2, hy_f_b2, hy_f_w3, hy_freq):
    n_half = n_tok // 2
    tt = np.linspace(0.0, 1.0, n_tok)[:, None]
    ang = (2.0 * np.pi * np.arange(n_tok) / n_tok)[:, None] * np.linspace(1e-4, HY_BANDS - 1, HY_BANDS)[None, :]
    z = np.concatenate([tt, np.cos(ang), -np.sin(ang), np.zeros((n_tok, LANES - HY_EMB))], axis=-1)
    z, t = jnp.asarray(z, F32), jnp.asarray(tt, F32)
    deltas = jnp.asarray(np.abs(np.linspace(HY_MIN_DECAY, HY_MAX_DECAY, D_HY))[None, :], F32)
    w1 = jnp.pad(hy_f_w1, ((0, LANES - HY_EMB), (0, 0)))
    ncb = D_HY // HY_CW
    full = lambda j: (0, 0)
    out_sd = jax.ShapeDtypeStruct((n_half, 2 * D_HY), F32)
    tab_specs = [pl.BlockSpec(t_.shape, full, pipeline_mode=pl.Buffered(1)) for t_ in tabs]
    return pl.pallas_call(
        functools.partial(_filt_kernel, n_tok, len(tabs)),
        grid=(2 * ncb,),
        in_specs=[pl.BlockSpec((n_tok, LANES), full),
                  pl.BlockSpec((n_tok, 1), full),
                  pl.BlockSpec((LANES, HY_HID), full),
                  pl.BlockSpec((1, HY_HID), full),
                  pl.BlockSpec((HY_HID, HY_HID), full),
                  pl.BlockSpec((1, HY_HID), full),
                  pl.BlockSpec((1, HY_HID), full),
                  pl.BlockSpec((HY_HID, HY_CW), lambda j: (0, 2 * ncb * (j // ncb) + j % ncb)),
                  pl.BlockSpec((HY_HID, HY_CW), lambda j: (0, 2 * ncb * (j // ncb) + ncb + j % ncb)),
                  pl.BlockSpec((1, HY_CW), lambda j: (0, j % ncb))] + tab_specs,
        out_specs=[pl.BlockSpec((n_half, HY_CW), lambda j: (0, j))] * 4,
        out_shape=[out_sd] * 4,
        scratch_shapes=[pltpu.VMEM((HY_CW // LANES, n_tok, LANES), F32)] * 2,
        compiler_params=_cparams(("arbitrary",)),
        name=f"filt{n_tok}",
    )(z, t, w1, hy_f_b1.reshape(1, -1), hy_f_w2, hy_f_b2.reshape(1, -1), hy_freq.reshape(1, -1),
      hy_f_w3, hy_f_w3, deltas, *tabs)


def _hyena_kernel(n_tab, pv_ref, p1_ref, p2_ref, cwv_ref, cw1_ref, cw2_ref, cbv_ref, cb1_ref, cb2_ref,
                  fbias_ref, *rest):
    tabs = tuple(r[...] for r in rest[:n_tab])
    (kar0_ref, kai0_ref, kbr0_ref, kbi0_ref, kar1_ref, kai1_ref, kbr1_ref, kbi1_ref,
     o_ref, u_ref, y_ref) = rest[n_tab:]
    n_tok = pv_ref.shape[1]
    row = lax.broadcasted_iota(jnp.int32, (n_tok, pv_ref.shape[2]), 0)

    def short_conv(p_ref, cw_ref, cb_ref):
        p = p_ref[0]
        prev = jnp.where(row == 0, 0.0, pltpu.roll(p, 1, axis=0))
        nxt = jnp.where(row == n_tok - 1, 0.0, pltpu.roll(p, n_tok - 1, axis=0))
        return cb_ref[...] + prev * cw_ref[0:1, :] + p * cw_ref[1:2, :] + nxt * cw_ref[2:3, :]

    def fftconv(u, kar_ref, kai_ref, kbr_ref, kbi_ref, skip):
        _put_cols(u_ref, u)
        ua_re, ua_im, ub_re, ub_im = _rfft_packed(u_ref, tabs)
        ka_re, ka_im, kb_re, kb_im = kar_ref[...], kai_ref[...], kbr_ref[...], kbi_ref[...]
        zero_row = jnp.zeros_like(ka_im[0:1])
        kaz = _set_row0(ka_im, zero_row)
        kbz = _set_row0(kb_im, zero_row)
        ya_re = ua_re * ka_re - ua_im * kaz
        yb_re = ub_re * kb_re - ub_im * kbz
        h_re = ua_im[0:1] * ka_im[0:1] - ub_im[0:1] * kb_im[0:1]
        h_im = ua_im[0:1] * kb_im[0:1] + ub_im[0:1] * ka_im[0:1]
        ya_im = _set_row0(ua_re * ka_im + ua_im * ka_re, h_re)
        yb_im = _set_row0(ub_re * kb_im + ub_im * kb_re, h_im)
        _irfft_packed(y_ref, ya_re, ya_im, yb_re, yb_im, tabs)
        return _get_cols(y_ref) + u * skip

    v = short_conv(pv_ref, cwv_ref, cbv_ref)
    x1 = short_conv(p1_ref, cw1_ref, cb1_ref)
    z = x1 * fftconv(v, kar0_ref, kai0_ref, kbr0_ref, kbi0_ref, fbias_ref[0:1, :])
    x2 = short_conv(p2_ref, cw2_ref, cb2_ref)
    o_ref[0] = x2 * fftconv(z, kar1_ref, kai1_ref, kbr1_ref, kbi1_ref, fbias_ref[1:2, :])


def _hyena(proj_hy, tabs, filt, hy_conv_w, hy_conv_b, hy_fbias):
    bsz, n_tok, _ = proj_hy.shape
    n_half = n_tok // 2
    ncb = D_HY // HY_CW
    cb = hy_conv_b.reshape(1, -1)
    tab_specs = [pl.BlockSpec(t.shape, lambda b, c: (0, 0), pipeline_mode=pl.Buffered(1)) for t in tabs]

    def pspec(k):
        return pl.BlockSpec((1, n_tok, HY_CW), lambda b, c: (b, 0, k * ncb + c))

    def cwspec(k):
        return pl.BlockSpec((3, HY_CW), lambda b, c: (0, k * ncb + c))

    def cbspec(k):
        return pl.BlockSpec((1, HY_CW), lambda b, c: (0, k * ncb + c))

    def fspec(o):
        mode = pl.Buffered(1) if ncb == 1 else None
        return pl.BlockSpec((n_half, HY_CW), lambda b, c: (0, o * ncb + c), pipeline_mode=mode)

    return pl.pallas_call(
        functools.partial(_hyena_kernel, len(tabs)),
        grid=(bsz, ncb),
        in_specs=[pspec(0), pspec(1), pspec(2), cwspec(0), cwspec(1), cwspec(2),
                  cbspec(0), cbspec(1), cbspec(2),
                  pl.BlockSpec((2, HY_CW), lambda b, c: (0, c))] + tab_specs + [fspec(0)] * 4 + [fspec(1)] * 4,
        out_specs=pl.BlockSpec((1, n_tok, HY_CW), lambda b, c: (b, 0, c)),
        out_shape=jax.ShapeDtypeStruct((bsz, n_tok, D_HY), F32),
        scratch_shapes=[pltpu.VMEM((HY_CW // LANES, n_tok, LANES), F32)] * 2,
        compiler_params=_cparams(("arbitrary", "arbitrary")),
        name=f"hyena{n_tok}",
    )(proj_hy, proj_hy, proj_hy, hy_conv_w, hy_conv_w, hy_conv_w, cb, cb, cb, hy_fbias,
      *tabs, *filt, *filt)


def _s5ops_kernel(*refs):
    for g in range(S5OPS_GB):
        _s5ops_group(g, *refs)


def _s5ops_group(g, are_ref, aim_ref, ldt_ref, btr_ref, bti_ref, cre_ref, cim_ref, d_ref,
                 opa_ref, opb_ref, gr_ref, gi_ref, atr_ref, ati_ref, er_ref, ei_ref):
    a_re, a_im = are_ref[g], aim_ref[g]
    dt = jnp.exp(ldt_ref[g])
    mag = jnp.exp(a_re * dt)
    ab_re = mag * jnp.cos(a_im * dt)
    ab_im = mag * jnp.sin(a_im * dt)
    n_re, n_im = ab_re - 1.0, ab_im
    den = a_re * a_re + a_im * a_im
    q_re = (n_re * a_re + n_im * a_im) / den
    q_im = (n_im * a_re - n_re * a_im) / den
    bt_re, bt_im = btr_ref[g], bti_ref[g]
    bb_re = q_re * bt_re - q_im * bt_im
    bb_im = q_re * bt_im + q_im * bt_re
    c_re, c_im = cre_ref[g, 0:S5_CH, :], cim_ref[g, 0:S5_CH, :]
    pw = [(jnp.ones_like(ab_re), jnp.zeros_like(ab_re))]
    for _ in range(S5_CHUNK):
        pr, pi = pw[-1]
        pw.append((pr * ab_re - pi * ab_im, pr * ab_im + pi * ab_re))
    lane = lax.broadcasted_iota(jnp.int32, ab_re.shape, 1)
    fwd = lane < S5_STATE
    for s in range(S5_CHUNK):
        e_re = jnp.where(fwd, pw[S5_CHUNK - 1 - s][0], pw[s][0])
        e_im = jnp.where(fwd, pw[S5_CHUNK - 1 - s][1], pw[s][1])
        er_ref[g, pl.ds(S5_CH * s, S5_CH), :] = e_re * bb_re - e_im * bb_im
        ei_ref[g, pl.ds(S5_CH * s, S5_CH), :] = e_re * bb_im + e_im * bb_re
        g_re = jnp.where(fwd, pw[s + 1][0], pw[S5_CHUNK - s][0])
        g_im = jnp.where(fwd, pw[s + 1][1], pw[S5_CHUNK - s][1])
        gr_ref[g, pl.ds(S5_CH * s, S5_CH), :] = (c_re * g_re - c_im * g_im).astype(BF16)
        gi_ref[g, pl.ds(S5_CH * s, S5_CH), :] = (-(c_re * g_im + c_im * g_re)).astype(BF16)
    atr_ref[g] = pw[S5_CHUNK][0]
    ati_ref[g] = pw[S5_CHUNK][1]
    er, ei = er_ref[g], ei_ref[g]
    opa_ref[g, :, 2 * LANES:3 * LANES], opb_ref[g, :, 0:LANES] = _split(er)
    opa_ref[g, :, 3 * LANES:4 * LANES], opb_ref[g, :, LANES:2 * LANES] = _split(ei)
    lane2 = lax.broadcasted_iota(jnp.int32, er.shape, 1)
    row2 = lax.broadcasted_iota(jnp.int32, er.shape, 0)
    f2 = lane2 < S5_STATE
    zero = jnp.zeros_like(er)

    cp_re, cp_im = cre_ref[g], cim_ref[g]
    kf = _mm3_t(jnp.where(f2, er, zero), cp_re) - _mm3_t(jnp.where(f2, ei, zero), cp_im)
    kb = _mm3_t(jnp.where(f2, zero, er), cp_re) - _mm3_t(jnp.where(f2, zero, ei), cp_im)
    d_row = d_ref[g]
    steps_per_vreg = LANES // S5_CH
    for half in range(S5_CHUNK // steps_per_vreg):
        acc = zero
        for tt in range(steps_per_vreg):
            t = half * steps_per_vreg + tt
            nf = S5_CH * (S5_CHUNK - 1 - t)
            nb = S5_CH * t
            col_f = jnp.concatenate([kf[nf:], zero[:nf]], axis=0) if nf else kf
            col_b = jnp.concatenate([zero[:nb], kb[:S5_ROW - nb]], axis=0) if nb else kb
            diag = jnp.where((row2 // S5_CH == t) & (row2 % S5_CH == lane2), d_row, 0.0)
            col = col_f + col_b + diag
            r = pltpu.roll(col, S5_CH * tt, axis=1) if tt else col
            acc = jnp.where((lane2 >= S5_CH * tt) & (lane2 < S5_CH * (tt + 1)), r, acc)
        opa_ref[g, :, LANES * half:LANES * (half + 1)] = acc.astype(BF16)


def _s5_operators(s5_a_re, s5_a_im, s5_log_dt, s5_b_re, s5_b_im, s5_c_re, s5_c_im, s5_d):
    g, p, h = S5_GROUPS, S5_STATE, S5_CH
    cat = lambda x: jnp.concatenate([x[0], x[1]], axis=-1)
    a_re = cat(s5_a_re).reshape(g, 1, 2 * p)
    a_im = cat(s5_a_im).reshape(g, 1, 2 * p)
    ldt = cat(jnp.broadcast_to(s5_log_dt[:, :, None], (2, g, p))).reshape(g, 1, 2 * p)
    bt_re = cat(jnp.swapaxes(s5_b_re, -1, -2))
    bt_im = cat(jnp.swapaxes(s5_b_im, -1, -2))
    cpad = lambda c: jnp.pad(jnp.concatenate([c, c], axis=-1), ((0, 0), (0, LANES - h), (0, 0)))
    c_re, c_im = cpad(s5_c_re), cpad(s5_c_im)
    d_row = jnp.pad(s5_d.reshape(g, 1, h), ((0, 0), (0, 0), (0, LANES - h)))
    vec = pl.BlockSpec((S5OPS_GB, 1, 2 * p), lambda i: (i, 0, 0))
    hp = pl.BlockSpec((S5OPS_GB, h, 2 * p), lambda i: (i, 0, 0))
    sq = pl.BlockSpec((S5OPS_GB, LANES, 2 * p), lambda i: (i, 0, 0))
    big = pl.BlockSpec((S5OPS_GB, S5_ROW, 2 * p), lambda i: (i, 0, 0))
    wide = lambda n: pl.BlockSpec((S5OPS_GB, S5_ROW, n), lambda i: (i, 0, 0))
    big_sd = jax.ShapeDtypeStruct((g, S5_ROW, 2 * p), BF16)
    vec_sd = jax.ShapeDtypeStruct((g, 1, 2 * p), F32)
    return pl.pallas_call(
        _s5ops_kernel,
        grid=(g // S5OPS_GB,),
        in_specs=[vec, vec, vec, hp, hp, sq, sq, vec],
        out_specs=[wide(2 * S5_ROW), wide(S5_ROW), big, big, vec, vec],
        out_shape=[jax.ShapeDtypeStruct((g, S5_ROW, 2 * S5_ROW), BF16),
                   jax.ShapeDtypeStruct((g, S5_ROW, S5_ROW), BF16), big_sd, big_sd, vec_sd, vec_sd],
        scratch_shapes=[pltpu.VMEM((S5OPS_GB, S5_ROW, 2 * p), F32)] * 2,
        compiler_params=_cparams(("arbitrary",)),
        name="s5ops",
    )(a_re, a_im, ldt, bt_re, bt_im, c_re, c_im, d_row)


def _block_transpose(xs):
    n = len(xs)
    lane = lax.broadcasted_iota(jnp.int32, xs[0].shape, 1)
    xs = list(xs)
    d = n // 2
    while d:
        keep = ((lane // S5_CH) & d) == 0
        for i in range(n):
            if i & d:
                continue
            lo, hi = xs[i], xs[i + d]
            xs[i] = jnp.where(keep, lo, pltpu.roll(hi, S5_CH * d, axis=1))
            xs[i + d] = jnp.where(keep, pltpu.roll(lo, LANES - S5_CH * d, axis=1), hi)
        d //= 2
    return xs


def _s5_kernel(bsz, n_chunks, u_ref, opa_ref, opb_ref, gr_ref, gi_ref,
               atr_ref, ati_ref, h0r_ref, h0i_ref, y_ref, fr_ref, fi_ref,
               ua_ref, ub_ref, ya_ref, yb_ref, sr_ref, si_ref, xfr_ref, xfi_ref, xbr_ref, xbi_ref):
    nc = n_chunks
    spv = LANES // S5_CH
    rsub = min(nc, 32)

    def to_chunks(b, carry):
        for half, dst in ((0, ua_ref), (1, ub_ref)):
            for r0 in range(0, nc, rsub):
                xs = [u_ref[b, pl.ds(S5_CHUNK * r0 + half * spv + tt, rsub, stride=S5_CHUNK), :]
                      for tt in range(spv)]
                for k, blk in enumerate(_block_transpose(xs)):
                    dst[k, pl.ds(r0 * bsz + b, rsub, stride=bsz), :] = blk
        return carry

    lax.fori_loop(0, bsz, to_chunks, 0, unroll=2)

    lane = lax.broadcasted_iota(jnp.int32, (bsz, 2 * S5_STATE), 1)
    fwd = lane < S5_STATE
    lane_all = lax.broadcasted_iota(jnp.int32, (bsz * nc, 2 * S5_STATE), 1)
    fwd_all = lane_all < S5_STATE

    def group(k, slot):
        u = jnp.concatenate([ua_ref[k], ub_ref[k]], axis=1)
        uh, ul = _split(u)
        wide = _dot(uh, opa_ref[k])
        inj = (wide[:, S5_ROW:] + _dot(ul, opa_ref[k, :, S5_ROW:2 * S5_ROW]) + _dot(uh, opb_ref[k]))
        sr_ref[slot] = inj[:, :LANES]
        si_ref[slot] = inj[:, LANES:]
        at_re, at_im = atr_ref[k], ati_ref[k]
        y_intra = wide[:, :S5_ROW]

        def step(i, xc):
            x_re, x_im = xc
            rf = pl.ds(pl.multiple_of(i * bsz, bsz), bsz)
            rb = pl.ds(pl.multiple_of((nc - 1 - i) * bsz, bsz), bsz)
            xfr_ref[slot, rf, :] = x_re
            xfi_ref[slot, rf, :] = x_im
            xbr_ref[slot, rb, :] = x_re
            xbi_ref[slot, rb, :] = x_im
            s_re = jnp.where(fwd, sr_ref[slot, rf, :], sr_ref[slot, rb, :])
            s_im = jnp.where(fwd, si_ref[slot, rf, :], si_ref[slot, rb, :])
            return (at_re * x_re - at_im * x_im + s_re, at_re * x_im + at_im * x_re + s_im)

        x_re, x_im = lax.fori_loop(0, nc, step, (h0r_ref[k], h0i_ref[k]), unroll=True)
        fr_ref[k] = x_re
        fi_ref[k] = x_im
        xp_re = jnp.where(fwd_all, xfr_ref[slot], xbr_ref[slot]).astype(BF16)
        xp_im = jnp.where(fwd_all, xfi_ref[slot], xbi_ref[slot]).astype(BF16)
        y = y_intra + _dot_t(xp_re, gr_ref[k]) + _dot_t(xp_im, gi_ref[k])
        ya_ref[k] = y[:, :LANES]
        yb_ref[k] = y[:, LANES:]

    def group_pair(j, carry):
        group(2 * j, 0)
        group(2 * j + 1, 1)
        return carry

    lax.fori_loop(0, S5_GB // 2, group_pair, 0)

    def to_tokens(b, carry):
        for half, src in ((0, ya_ref), (1, yb_ref)):
            for r0 in range(0, nc, rsub):
                ys = [src[k, pl.ds(r0 * bsz + b, rsub, stride=bsz), :] for k in range(S5_GB)]
                for tt, blk in enumerate(_block_transpose(ys)):
                    y_ref[b, pl.ds(S5_CHUNK * r0 + half * spv + tt, rsub, stride=S5_CHUNK), :] = blk
        return carry

    lax.fori_loop(0, bsz, to_tokens, 0, unroll=2)


def _s5(u, ops, h0_re, h0_im):
    bsz, n_tok, _ = u.shape
    g, p = S5_GROUPS, S5_STATE
    nc = n_tok // S5_CHUNK
    rows = nc * bsz
    tok = pl.BlockSpec((bsz, n_tok, LANES), lambda j: (0, 0, j))
    gspec = lambda shape: pl.BlockSpec((S5_GB,) + shape, lambda j: (j, 0, 0))
    op = gspec((S5_ROW, 2 * p))
    return pl.pallas_call(
        functools.partial(_s5_kernel, bsz, nc),
        grid=(g // S5_GB,),
        in_specs=[tok, gspec((S5_ROW, 2 * S5_ROW)), gspec((S5_ROW, S5_ROW)), op, op,
                  gspec((1, 2 * p)), gspec((1, 2 * p)), gspec((bsz, 2 * p)), gspec((bsz, 2 * p))],
        out_specs=[tok, gspec((bsz, 2 * p)), gspec((bsz, 2 * p))],
        out_shape=[jax.ShapeDtypeStruct((bsz, n_tok, D_S5), F32),
                   jax.ShapeDtypeStruct((g, bsz, 2 * p), F32),
                   jax.ShapeDtypeStruct((g, bsz, 2 * p), F32)],
        scratch_shapes=([pltpu.VMEM((S5_GB, rows, LANES), F32)] * 4
                        + [pltpu.VMEM((2, rows, 2 * p), F32)] * 6),
        compiler_params=_cparams(("arbitrary",)),
        name=f"s5_{n_tok}",
    )(u, *ops, h0_re, h0_im)


def _in_kernel(has_pos, *refs):
    if has_pos:
        x_ref, pos_ref, mod_ref, w_ref, hy_ref, s5_ref = refs
        x = x_ref[0] + pos_ref[...]
    else:
        x_ref, mod_ref, w_ref, hy_ref, s5_ref = refs
        x = x_ref[0]
    sh1 = mod_ref[0, :, 0:D_MODEL]
    sc1 = mod_ref[0, :, D_MODEL:2 * D_MODEL]
    h = _norm(x) * (1.0 + sc1) + sh1
    proj = _dot(h.astype(BF16), w_ref[...].astype(BF16))
    hy_ref[0] = proj[:, :3 * D_HY]
    s5_ref[0] = proj[:, 3 * D_HY:]


def _in_proj(x3, pos, mod3, w_in, tm):
    nb, lt, _ = x3.shape
    has_pos = pos is not None
    per_batch = mod3.shape[0] > 1
    midx = (lambda b, i: (b, 0, 0)) if per_batch else (lambda b, i: (0, 0, 0))
    in_specs = [pl.BlockSpec((1, tm, D_MODEL), lambda b, i: (b, i, 0))]
    args = [x3]
    if has_pos:
        in_specs.append(pl.BlockSpec((tm, D_MODEL), lambda b, i: (i, 0)))
        args.append(pos)
    in_specs += [pl.BlockSpec((1, 1, 6 * D_MODEL), midx),
                 pl.BlockSpec((D_MODEL, 3 * D_HY + D_S5), lambda b, i: (0, 0), pipeline_mode=pl.Buffered(1))]
    args += [mod3, w_in]
    return pl.pallas_call(
        functools.partial(_in_kernel, has_pos),
        grid=(nb, lt // tm),
        in_specs=in_specs,
        out_specs=[pl.BlockSpec((1, tm, 3 * D_HY), lambda b, i: (b, i, 0)),
                   pl.BlockSpec((1, tm, D_S5), lambda b, i: (b, i, 0))],
        out_shape=[jax.ShapeDtypeStruct((nb, lt, 3 * D_HY), F32),
                   jax.ShapeDtypeStruct((nb, lt, D_S5), F32)],
        compiler_params=_cparams(("arbitrary", "arbitrary")),
        name=f"in_proj{nb}",
    )(*args)


def _route(logits):
    lane = lax.broadcasted_iota(jnp.int32, logits.shape, 1)
    lane_f = lane.astype(F32)
    neg = -jnp.inf
    big = float(LANES)
    m1 = (lane >= N_EXPERTS) & (lane < N_EXPERTS + N_EGROUPS)
    l1 = jnp.where(m1, logits, neg)
    top1 = jnp.max(l1, axis=-1, keepdims=True)
    grp = jnp.min(jnp.where(l1 == top1, lane_f, big), axis=-1, keepdims=True) - float(N_EXPERTS)
    den = jnp.sum(jnp.where(m1, jnp.exp(logits - top1), 0.0), axis=-1, keepdims=True)
    p_grp = 1.0 / den
    lo = grp * float(N_EPG)
    m2 = (lane_f >= lo) & (lane_f < lo + float(N_EPG))
    l2 = jnp.where(m2, logits, neg)
    v1 = jnp.max(l2, axis=-1, keepdims=True)
    i1 = jnp.min(jnp.where(l2 == v1, lane_f, big), axis=-1, keepdims=True)
    l2b = jnp.where(lane_f == i1, neg, l2)
    v2 = jnp.max(l2b, axis=-1, keepdims=True)
    i2 = jnp.min(jnp.where(l2b == v2, lane_f, big), axis=-1, keepdims=True)
    e = jnp.exp(v2 - v1)
    w1 = 1.0 / (1.0 + e)
    w2 = e / (1.0 + e)
    gates = jnp.where(lane_f == i1, w1 * p_grp, 0.0) + jnp.where(lane_f == i2, w2 * p_grp, 0.0)
    return jnp.where(lane_f == grp + float(N_EXPERTS), 1.0, gates)


def _out_kernel(n_ctx_blocks, xc_ref, xl_ref, pos_ref, yhyc_ref, yhyl_ref, ys5c_ref, ys5l_ref, mod_ref,
                wglu_ref, bglu_ref, ong_ref, wout_ref, ln1g_ref, ln1b_ref, wrh_ref, wrl_ref, br_ref,
                x1_ref, h2_ref, gate_ref, cnt_ref):
    is_ctx = pl.program_id(0) < n_ctx_blocks
    x = jnp.where(is_ctx, xc_ref[...], xl_ref[...] + pos_ref[...])
    y = jnp.where(is_ctx, ys5c_ref[...], ys5l_ref[...])
    y_hy = jnp.where(is_ctx, yhyc_ref[...], yhyl_ref[...])
    s5 = jax.nn.gelu(y) * jax.nn.sigmoid(_dot(y.astype(BF16), wglu_ref[...]) + bglu_ref[...])
    m_hy = _rms(y_hy) * ong_ref[:, 0:D_HY]
    m_s5 = _rms(s5) * ong_ref[:, D_HY:]
    o = (_dot(m_hy.astype(BF16), wout_ref[0:D_HY, :]) + _dot(m_s5.astype(BF16), wout_ref[D_HY:, :]))
    g1 = mod_ref[0, :, 2 * D_MODEL:3 * D_MODEL]
    sh2 = mod_ref[0, :, 3 * D_MODEL:4 * D_MODEL]
    sc2 = mod_ref[0, :, 4 * D_MODEL:5 * D_MODEL]
    x1 = _norm(ALPHA * x + g1 * o) * ln1g_ref[...] + ln1b_ref[...]
    x1_ref[...] = x1
    h2 = _norm(x1) * (1.0 + sc2) + sh2
    h2_ref[...] = h2.astype(BF16)
    hh, hl = _split(h2)
    logits = (_dot(hh, wrh_ref[...]) + _dot(hl, wrh_ref[...]) + _dot(hh, wrl_ref[...]) + br_ref[...])
    gates = _route(logits)
    gate_ref[...] = gates
    cnt_ref[0] = jnp.sum(gates, axis=0, keepdims=True)


def _out_proj(xc, xl, pos, yhy_c, yhy_l, ys5_c, ys5_l, mod, wglu_bf, bglu, ong, wout_bf, ln1g, ln1b,
              wr_hi, wr_lo, br, tm):
    n_ctx, n_lat = xc.shape[0], xl.shape[0]
    l_lat = pos.shape[0]
    ncb, nlb, npb = n_ctx // tm, n_lat // tm, l_lat // tm
    ctx = lambda w: pl.BlockSpec((tm, w), lambda i: (jnp.minimum(i, ncb - 1), 0))
    lat = lambda w: pl.BlockSpec((tm, w), lambda i: (jnp.maximum(i - ncb, 0), 0))
    full = lambda shape: pl.BlockSpec(shape, lambda i: (0,) * len(shape))
    out = lambda w: pl.BlockSpec((tm, w), lambda i: (i, 0))
    mod_idx = lambda i: (jnp.where(i < ncb, 0, 1 + jnp.maximum(i - ncb, 0) // npb), 0, 0)
    n_all = n_ctx + n_lat
    return pl.pallas_call(
        functools.partial(_out_kernel, ncb),
        grid=(ncb + nlb,),
        in_specs=[ctx(D_MODEL), lat(D_MODEL),
                  pl.BlockSpec((tm, D_MODEL), lambda i: (jnp.maximum(i - ncb, 0) % npb, 0)),
                  ctx(D_HY), lat(D_HY), ctx(D_S5), lat(D_S5),
                  pl.BlockSpec((1, 1, 6 * D_MODEL), mod_idx),
                  full((D_S5, D_S5)), full((1, D_S5)), full((1, D_MODEL)), full((D_MODEL, D_MODEL)),
                  full((1, D_MODEL)), full((1, D_MODEL)), full((D_MODEL, LANES)), full((D_MODEL, LANES)),
                  full((1, LANES))],
        out_specs=[out(D_MODEL), out(D_MODEL), out(LANES), pl.BlockSpec((1, 1, LANES), lambda i: (i, 0, 0))],
        out_shape=[jax.ShapeDtypeStruct((n_all, D_MODEL), F32),
                   jax.ShapeDtypeStruct((n_all, D_MODEL), BF16),
                   jax.ShapeDtypeStruct((n_all, LANES), F32),
                   jax.ShapeDtypeStruct((n_all // tm, 1, LANES), F32)],
        compiler_params=_cparams(("arbitrary",)),
        name="out_proj",
    )(xc, xl, pos, yhy_c, yhy_l, ys5_c, ys5_l, mod.reshape(mod.shape[0], 1, 6 * D_MODEL),
      wglu_bf, bglu, ong, wout_bf, ln1g, ln1b, wr_hi, wr_lo, br)


def _perm_t(gates, loc_ref, s):
    n = gates.shape[0]
    lane = lax.broadcasted_iota(jnp.int32, gates.shape, 1)
    oh = jnp.where((lane >= N_EXPERTS) & (lane < N_EXPERTS + N_EGROUPS), gates, 0.0)
    r = lax.broadcasted_iota(jnp.int32, (n, n), 0)
    c = lax.broadcasted_iota(jnp.int32, (n, n), 1)
    earlier = jnp.where(c < r, 1.0, 0.0).astype(BF16)
    cum = _dot(earlier, oh.astype(BF16))
    rank = jnp.sum(cum * oh, axis=-1, keepdims=True)
    lane1 = lax.broadcasted_iota(jnp.int32, (1, LANES), 1)
    locv = jnp.zeros((1, LANES), F32)
    for grp in range(N_EGROUPS):
        locv = jnp.where(lane1 == N_EXPERTS + grp, loc_ref[N_EGROUPS * s + grp].astype(F32), locv)
    dest = rank + jnp.sum(oh * locv, axis=-1, keepdims=True)
    slot = lax.broadcasted_iota(jnp.int32, (n, MOE_SLOTS), 1).astype(F32)
    return jnp.where(slot == dest, 1.0, 0.0)


def _segment_copies(s, loc_ref, len_ref, off_ref, make):
    for grp in range(N_EGROUPS):
        loc = loc_ref[N_EGROUPS * s + grp]
        off = off_ref[N_EGROUPS * s + grp]
        length = len_ref[N_EGROUPS * s + grp]
        n_big = length // MOE_BIG
        done = n_big * MOE_BIG

        def big(i, carry):
            make(pl.multiple_of(loc + MOE_BIG * i, MOE_UNIT), pl.multiple_of(off + MOE_BIG * i, MOE_UNIT), MOE_BIG)
            return carry

        def unit(i, carry):
            make(pl.multiple_of(loc + done + MOE_UNIT * i, MOE_UNIT),
                 pl.multiple_of(off + done + MOE_UNIT * i, MOE_UNIT), MOE_UNIT)
            return carry

        lax.fori_loop(0, n_big, big, 0)
        lax.fori_loop(0, (length - done) // MOE_UNIT, unit, 0)


def _pad_copies(pad_ref, n_blocks, zx_v, zg_v, xs_hbm, gs_hbm, sem, op):
    def unit(row, rows):
        getattr(pltpu.make_async_copy(zx_v.at[pl.ds(0, rows), :], xs_hbm.at[pl.ds(row, rows), :], sem.at[0]), op)()
        getattr(pltpu.make_async_copy(zg_v.at[pl.ds(0, rows), :], gs_hbm.at[pl.ds(row, rows), :], sem.at[1]), op)()

    for grp in range(N_EGROUPS):
        start = pad_ref[grp]

        def body(i, carry):
            unit(pl.multiple_of(start + MOE_UNIT * i, MOE_UNIT), MOE_UNIT)
            return carry

        lax.fori_loop(0, pad_ref[N_EGROUPS + grp], body, 0)

    def tail(b, carry):
        unit(pl.multiple_of(b * MOE_TM, MOE_TM), MOE_TM)
        return carry

    lax.fori_loop(pad_ref[2 * N_EGROUPS], n_blocks, tail, 0)


def _moe_sort_kernel(n_blocks, loc_ref, len_ref, off_ref, pad_ref, h_ref, gate_ref, xs_hbm, gs_hbm,
                     xs_v, gs_v, zx_v, zg_v, sem, zsem):
    s = pl.program_id(0)
    slot = s % 2

    @pl.when(s == 0)
    def _():
        zx_v[...] = jnp.zeros_like(zx_v)
        zg_v[...] = jnp.zeros_like(zg_v)
        _pad_copies(pad_ref, n_blocks, zx_v, zg_v, xs_hbm, gs_hbm, zsem, 'start')

    gates = gate_ref[...]
    p = _perm_t(gates, loc_ref, s).T.astype(BF16)
    xs_v[slot] = _dot(p, h_ref[...]).astype(BF16)
    g_hi = gates.astype(BF16)
    r1 = gates - g_hi.astype(F32)
    g_mid = r1.astype(BF16)
    g_lo = (r1 - g_mid.astype(F32)).astype(BF16)
    gs_v[slot] = _dot(p, g_hi) + _dot(p, g_mid) + _dot(p, g_lo)

    def copies(buf):
        def x_copy(lr, gr, rows):
            return pltpu.make_async_copy(xs_v.at[buf, pl.ds(lr, rows), :],
                                         xs_hbm.at[pl.ds(gr, rows), :], sem.at[0, buf])

        def g_copy(lr, gr, rows):
            return pltpu.make_async_copy(gs_v.at[buf, pl.ds(lr, rows), :],
                                         gs_hbm.at[pl.ds(gr, rows), :], sem.at[1, buf])

        def start(lr, gr, rows):
            x_copy(lr, gr, rows).start()
            g_copy(lr, gr, rows).start()

        def wait(lr, gr, rows):
            x_copy(lr, gr, rows).wait()
            g_copy(lr, gr, rows).wait()

        return start, wait

    _segment_copies(s, loc_ref, len_ref, off_ref, copies(slot)[0])

    @pl.when(s > 0)
    def _():
        _segment_copies(s - 1, loc_ref, len_ref, off_ref, copies(1 - slot)[1])

    @pl.when(s == pl.num_programs(0) - 1)
    def _():
        _segment_copies(s, loc_ref, len_ref, off_ref, copies(slot)[1])
        _pad_copies(pad_ref, n_blocks, zx_v, zg_v, xs_hbm, gs_hbm, zsem, 'wait')


def _moe_expert_kernel(bg_ref, nb_ref, xs_ref, gs_ref, wg_ref, wu_ref, wd_ref, o_ref):
    i = pl.program_id(0)

    @pl.when(i < nb_ref[0])
    def _():
        grp = bg_ref[i]
        x = xs_ref[...]
        gates = gs_ref[...]
        lane = lax.broadcasted_iota(jnp.int32, gates.shape, 1)
        acc = jnp.zeros(o_ref.shape, F32)
        for e in range(N_EPG):
            a = _dot(x, wg_ref[e].astype(BF16))
            u = _dot(x, wu_ref[e].astype(BF16))
            ge = jnp.sum(jnp.where(lane == N_EPG * grp + e, gates, 0.0), axis=-1, keepdims=True)
            hid = jax.nn.silu(a) * u * ge
            acc = acc + _dot(hid.astype(BF16), wd_ref[e].astype(BF16))
        o_ref[...] = acc.astype(BF16)

    @pl.when(i >= nb_ref[0])
    def _():
        o_ref[...] = jnp.zeros_like(o_ref)


def _moe_combine_kernel(loc_ref, len_ref, off_ref, gate_ref, x1_ref, mod_ref, ln2g_ref, ln2b_ref, o_hbm,
                        ctx_ref, lat_ref, o_v, sem, *, n_ctx_tiles):
    s = pl.program_id(0)
    slot = s % 2

    def copies(buf):
        def o_copy(lr, gr, rows):
            return pltpu.make_async_copy(o_hbm.at[pl.ds(gr, rows), :],
                                         o_v.at[buf, pl.ds(lr, rows), :], sem.at[buf])

        return (lambda lr, gr, rows: o_copy(lr, gr, rows).start()), (lambda lr, gr, rows: o_copy(lr, gr, rows).wait())

    @pl.when(s == 0)
    def _():
        o_v[...] = jnp.zeros_like(o_v)
        _segment_copies(s, loc_ref, len_ref, off_ref, copies(slot)[0])

    @pl.when(s + 1 < pl.num_programs(0))
    def _():
        _segment_copies(s + 1, loc_ref, len_ref, off_ref, copies(1 - slot)[0])

    pt = _perm_t(gate_ref[...], loc_ref, s).astype(BF16)
    _segment_copies(s, loc_ref, len_ref, off_ref, copies(slot)[1])
    f = _dot(pt, o_v[slot])
    g2 = mod_ref[0, :, 5 * D_MODEL:6 * D_MODEL]
    x2 = _norm(ALPHA * x1_ref[...] + g2 * f) * ln2g_ref[...] + ln2b_ref[...]

    @pl.when(s < n_ctx_tiles)
    def _():
        ctx_ref[...] = x2

    @pl.when(s >= n_ctx_tiles)
    def _():
        lat_ref[...] = x2


def _moe_plan(tile_counts, n_blocks):
    cnt = tile_counts[:, 0, N_EXPERTS:N_EXPERTS + N_EGROUPS].astype(jnp.int32)
    len16 = ((cnt + MOE_UNIT - 1) // MOE_UNIT) * MOE_UNIT
    loc = jnp.cumsum(len16, axis=1) - len16
    rows_g = jnp.sum(len16, axis=0)
    reg_g = ((rows_g + MOE_TM - 1) // MOE_TM) * MOE_TM
    reg_start = jnp.cumsum(reg_g) - reg_g
    off = reg_start[None, :] + jnp.cumsum(len16, axis=0) - len16
    blk_end = jnp.cumsum(reg_g // MOE_TM)
    bi = jnp.arange(n_blocks, dtype=jnp.int32)
    blk_group = jnp.minimum(jnp.sum((bi[:, None] >= blk_end[None, :]).astype(jnp.int32), axis=1),
                            N_EGROUPS - 1)
    flat = lambda a: a.reshape(-1).astype(jnp.int32)
    pads = jnp.concatenate([reg_start + rows_g, (reg_g - rows_g) // MOE_UNIT, blk_end[-1:]])
    return (flat(loc), flat(len16), flat(off), flat(pads), blk_group.astype(jnp.int32),
            blk_end[-1:].astype(jnp.int32))


def _moe(h2_all, gates_all, tile_counts, x1_all, mod, w_gate, w_up, w_down, ln2g, ln2b, n_ctx,
         tokens_per_mod_row):
    n_tok = h2_all.shape[0]
    n_tiles = n_tok // MOE_ST
    n_ctx_tiles = n_ctx // MOE_ST
    max_rows = n_tok + n_tiles * N_EGROUPS * (MOE_UNIT - 1) + N_EGROUPS * (MOE_TM - 1)
    n_blocks = -(-max_rows // MOE_TM)
    n_rows = n_blocks * MOE_TM
    loc, len16, off, pads, blk_group, n_used = _moe_plan(tile_counts, n_blocks)

    tile = lambda w: pl.BlockSpec((MOE_ST, w), lambda s, *_: (s, 0))
    anyspec = pl.BlockSpec(memory_space=pl.ANY)
    xs, gs = pl.pallas_call(
        functools.partial(_moe_sort_kernel, n_blocks),
        grid_spec=pltpu.PrefetchScalarGridSpec(
            num_scalar_prefetch=4, grid=(n_tiles,),
            in_specs=[tile(D_MODEL), tile(LANES)],
            out_specs=[anyspec, anyspec],
            scratch_shapes=[pltpu.VMEM((2, MOE_SLOTS, D_MODEL), BF16), pltpu.VMEM((2, MOE_SLOTS, LANES), F32),
                            pltpu.VMEM((MOE_TM, D_MODEL), BF16), pltpu.VMEM((MOE_TM, LANES), F32),
                            pltpu.SemaphoreType.DMA((2, 2)), pltpu.SemaphoreType.DMA((2,))]),
        out_shape=[jax.ShapeDtypeStruct((n_rows, D_MODEL), BF16),
                   jax.ShapeDtypeStruct((n_rows, LANES), F32)],
        compiler_params=_cparams(("arbitrary",)),
        name="moe_sort",
    )(loc, len16, off, pads, h2_all, gates_all)

    blk = lambda w: pl.BlockSpec((MOE_TM, w), lambda i, bg, nb: (jnp.minimum(i, nb[0] - 1), 0))
    wspec = lambda a, b, mode: pl.BlockSpec((N_EPG, a, b), lambda i, bg, nb: (bg[i], 0, 0),
                                            pipeline_mode=mode)
    o_sorted = pl.pallas_call(
        _moe_expert_kernel,
        grid_spec=pltpu.PrefetchScalarGridSpec(
            num_scalar_prefetch=2, grid=(n_blocks,),
            in_specs=[blk(D_MODEL), blk(LANES), wspec(D_MODEL, D_EXPERT, None),
                      wspec(D_MODEL, D_EXPERT, None), wspec(D_EXPERT, D_MODEL, None)],
            out_specs=pl.BlockSpec((MOE_TM, D_MODEL), lambda i, bg, nb: (i, 0))),
        out_shape=jax.ShapeDtypeStruct((n_rows, D_MODEL), BF16),
        compiler_params=_cparams(("arbitrary",)),
        name="moe_experts",
    )(blk_group, n_used, xs, gs, w_gate, w_up, w_down)

    lat_per_row = tokens_per_mod_row // MOE_ST

    def mod_idx(s, *_):
        return (jnp.where(s < n_ctx_tiles, 0, 1 + (s - n_ctx_tiles) // lat_per_row), 0, 0)

    vec = pl.BlockSpec((1, D_MODEL), lambda s, *_: (0, 0))
    return pl.pallas_call(
        functools.partial(_moe_combine_kernel, n_ctx_tiles=n_ctx_tiles),
        grid_spec=pltpu.PrefetchScalarGridSpec(
            num_scalar_prefetch=3, grid=(n_tiles,),
            in_specs=[tile(LANES), tile(D_MODEL), pl.BlockSpec((1, 1, 6 * D_MODEL), mod_idx), vec, vec,
                      anyspec],
            out_specs=[pl.BlockSpec((MOE_ST, D_MODEL), lambda s, *_: (jnp.minimum(s, n_ctx_tiles - 1), 0)),
                       pl.BlockSpec((MOE_ST, D_MODEL), lambda s, *_: (jnp.maximum(s - n_ctx_tiles, 0), 0))],
            scratch_shapes=[pltpu.VMEM((2, MOE_SLOTS, D_MODEL), BF16), pltpu.SemaphoreType.DMA((2,))]),
        out_shape=[jax.ShapeDtypeStruct((n_ctx, D_MODEL), F32),
                   jax.ShapeDtypeStruct((n_tok - n_ctx, D_MODEL), F32)],
        compiler_params=_cparams(("arbitrary",)),
        name="moe_combine",
    )(loc, len16, off, gates_all, x1_all, mod.reshape(mod.shape[0], 1, 6 * D_MODEL), ln2g, ln2b, o_sorted)


def _grid_pos_embed(n_tokens):
    rows = n_tokens // GRID_W
    row = np.repeat(np.arange(rows, dtype=np.float64), GRID_W)
    col = np.tile(np.arange(GRID_W, dtype=np.float64), rows)
    quarter = D_MODEL // 4
    omega = 1.0 / (POS_BASE ** (np.arange(quarter, dtype=np.float64) / quarter))
    er = row[:, None] * omega
    ec = col[:, None] * omega
    return jnp.asarray(np.concatenate([np.sin(er), np.cos(er), np.sin(ec), np.cos(ec)], axis=-1), F32)


def _mixers(x, pos, mod3, h0_re, h0_im, tabs, filt, s5ops, wts, tm):
    bsz, n_tok, _ = x.shape
    shared = mod3.shape[0] == 1
    x3 = x.reshape(1, bsz * n_tok, D_MODEL) if shared else x
    proj_hy, u_s5 = _in_proj(x3, pos, mod3, wts['w_in'], tm)
    y_hy = _hyena(proj_hy.reshape(bsz, n_tok, 3 * D_HY), tabs, filt,
                  wts['hy_conv_w'], wts['hy_conv_b'], wts['hy_fbias'])
    y_s5, f_re, f_im = _s5(u_s5.reshape(bsz, n_tok, D_S5), s5ops, h0_re, h0_im)
    return y_hy.reshape(bsz * n_tok, D_HY), y_s5.reshape(bsz * n_tok, D_S5), f_re, f_im


def kernel(x_prompt, x_sample, state_s5_re, state_s5_im, c, c_ctx, w_ada, b_ada, w_in, hy_conv_w, hy_conv_b, hy_f_w1, hy_f_b1, hy_f_w2, hy_f_b2, hy_f_w3, hy_freq, hy_fbias, s5_a_re, s5_a_im, s5_log_dt, s5_b_re, s5_b_im, s5_c_re, s5_c_im, s5_d, s5_w_glu, s5_b_glu, out_norm_g, w_out, ln1_g, ln1_b, moe_w_r1, moe_b_r1, moe_w_r2, moe_b_r2, moe_w_gate, moe_w_up, moe_w_down, ln2_g, ln2_b):
    b_ctx, l_ctx, _ = x_prompt.shape
    b_lat, l_lat, _ = x_sample.shape
    g, p = S5_GROUPS, S5_STATE
    assert w_ada.shape[0] == 1, "single-layer trunk"
    l = 0

    nrow = 16
    cond = jnp.concatenate([c_ctx[None, :], c, jnp.zeros((nrow - 1 - b_lat, D_MODEL), F32)], axis=0)
    mod = _ada(cond, w_ada[l], b_ada[l])
    mod_ctx = mod[0:1].reshape(1, 1, 6 * D_MODEL)
    mod_lat = mod[1:1 + b_lat].reshape(b_lat, 1, 6 * D_MODEL)

    wr = jnp.concatenate([moe_w_r2[l].transpose(1, 0, 2).reshape(D_MODEL, N_EXPERTS), moe_w_r1[l]], axis=1)
    wr = jnp.pad(wr, ((0, 0), (0, LANES - wr.shape[1])))
    br = jnp.concatenate([moe_b_r2[l].reshape(-1), moe_b_r1[l]])
    br = jnp.pad(br, (0, LANES - br.shape[0])).reshape(1, LANES)
    wr_hi, wr_lo = _split(wr)

    wts = {
        'w_in': w_in[l], 'hy_conv_w': hy_conv_w[l], 'hy_conv_b': hy_conv_b[l],
        'hy_fbias': hy_fbias[l], 'w_glu': s5_w_glu[l].astype(BF16), 'b_glu': s5_b_glu[l].reshape(1, -1),
        'out_norm_g': out_norm_g[l].reshape(1, -1), 'w_out': w_out[l].astype(BF16),
        'ln1_g': ln1_g[l].reshape(1, -1), 'ln1_b': ln1_b[l].reshape(1, -1),
        'wr_hi': wr_hi, 'wr_lo': wr_lo, 'br': br,
        'w_gate': moe_w_gate[l], 'w_up': moe_w_up[l], 'w_down': moe_w_down[l],
        'ln2_g': ln2_g[l].reshape(1, -1), 'ln2_b': ln2_b[l].reshape(1, -1),
    }

    s5ops = _s5_operators(s5_a_re[l], s5_a_im[l], s5_log_dt[l], s5_b_re[l], s5_b_im[l],
                          s5_c_re[l], s5_c_im[l], s5_d[l])
    tabs_ctx = _tables(l_ctx)
    tabs_lat = _tables(l_lat)
    filt_args = (hy_f_w1[l], hy_f_b1[l], hy_f_w2[l], hy_f_b2[l], hy_f_w3[l], hy_freq[l])
    filt_ctx = _hyena_filters(l_ctx, tabs_ctx, *filt_args)
    filt_lat = _hyena_filters(l_lat, tabs_lat, *filt_args)

    zero = jnp.zeros((g, b_ctx, 2 * p), F32)
    yhy_c, ys5_c, f_re, f_im = _mixers(x_prompt, None, mod_ctx, zero, zero, tabs_ctx, filt_ctx, s5ops, wts, 1024)
    unpack = lambda f: f.reshape(g, b_ctx, 2, p).transpose(1, 2, 0, 3)[:, None]
    new_re, new_im = unpack(f_re), unpack(f_im)

    pack = lambda s: s[:, l].transpose(2, 0, 1, 3).reshape(g, b_lat, 2 * p)
    pos = _grid_pos_embed(l_lat)
    yhy_l, ys5_l, _, _ = _mixers(x_sample, pos, mod_lat, pack(state_s5_re), pack(state_s5_im),
                                 tabs_lat, filt_lat, s5ops, wts, 1024)

    n_ctx = b_ctx * l_ctx
    x1_all, h2_all, gates_all, tile_counts = _out_proj(
        x_prompt.reshape(n_ctx, D_MODEL), x_sample.reshape(b_lat * l_lat, D_MODEL), pos,
        yhy_c, yhy_l, ys5_c, ys5_l, mod, wts['w_glu'], wts['b_glu'], wts['out_norm_g'], wts['w_out'],
        wts['ln1_g'], wts['ln1_b'], wts['wr_hi'], wts['wr_lo'], wts['br'], MOE_ST)
    y_ctx, y_lat = _moe(h2_all, gates_all, tile_counts, x1_all, mod,
                        wts['w_gate'], wts['w_up'], wts['w_down'], wts['ln2_g'], wts['ln2_b'],
                        n_ctx, l_lat)
    return (y_ctx.reshape(x_prompt.shape), y_lat.reshape(x_sample.shape), new_re, new_im)
```

```python
import functools
import math

import numpy as np
import jax
import jax.numpy as jnp
from jax import lax
from jax.experimental import pallas as pl
from jax.experimental.pallas import tpu as pltpu

F32 = jnp.float32
BF16 = jnp.bfloat16

D_MODEL = 1024
DEPTH = 1
GRID_W = 64
POS_BASE = 10000.0
D_HY = 512
D_S5 = 512
S5_CH = 16
S5_GROUPS = 32
S5_STATE = 64
S5_CHUNK = 16
S5_ROW = S5_CHUNK * S5_CH
HY_BANDS = 16
HY_EMB = 1 + 2 * HY_BANDS
HY_HID = 64
HY_MIN_DECAY = math.log(1e-2) / 1.5
HY_MAX_DECAY = math.log(1e-2) / 0.3
N_EGROUPS = 4
N_EPG = 4
N_EXPERTS = 16
D_EXPERT = 512
LN_EPS = 1e-5
ALPHA = (2.0 * DEPTH) ** 0.25
LANES = 128
S5_GB = LANES // S5_CH
S5OPS_GB = 4
HY_CW = 512
MOE_ST = 512
MOE_SLOTS = 640
MOE_UNIT = 16
MOE_BIG = 64
MOE_TM = 512
VMEM_LIMIT = 60000 * 1024


def _cparams(sem):
    return pltpu.CompilerParams(dimension_semantics=sem, vmem_limit_bytes=VMEM_LIMIT)


def _split(x):
    hi = x.astype(BF16)
    lo = (x - hi.astype(F32)).astype(BF16)
    return hi, lo


def _dot(a, b):
    return jnp.dot(a, b, preferred_element_type=F32)


def _dot_t(a, b):
    return lax.dot_general(a, b, (((1,), (1,)), ((), ())), preferred_element_type=F32)


def _mm3(a, b):
    ah, al = _split(a)
    bh, bl = _split(b)
    return _dot(ah, bh) + _dot(al, bh) + _dot(ah, bl)


def _mm3_t(a, b):
    ah, al = _split(a)
    bh, bl = _split(b)
    return _dot_t(ah, bh) + _dot_t(al, bh) + _dot_t(ah, bl)


def _norm(x):
    xc = x - jnp.mean(x, axis=-1, keepdims=True)
    return xc * lax.rsqrt(jnp.mean(xc * xc, axis=-1, keepdims=True) + LN_EPS)


def _rms(y):
    return y * lax.rsqrt(jnp.mean(y * y, axis=-1, keepdims=True) + LN_EPS)


def _ada_kernel(cond_ref, w_ref, b_ref, o_ref):
    c = jax.nn.silu(cond_ref[...])
    o_ref[...] = _mm3(c, w_ref[...]) + b_ref[...]


def _ada(cond, w_ada, b_ada):
    nb = cond.shape[0]
    n = w_ada.shape[1]
    tn = 1024
    return pl.pallas_call(
        _ada_kernel,
        grid=(n // tn,),
        in_specs=[pl.BlockSpec((nb, D_MODEL), lambda j: (0, 0)),
                  pl.BlockSpec((D_MODEL, tn), lambda j: (0, j)),
                  pl.BlockSpec((1, tn), lambda j: (0, j))],
        out_specs=pl.BlockSpec((nb, tn), lambda j: (0, j)),
        out_shape=jax.ShapeDtypeStruct((nb, n), F32),
        compiler_params=_cparams(("arbitrary",)),
        name="ada",
    )(cond, w_ada, b_ada.reshape(1, n))


RADIX2_MIN_HALF = 256


def _dft_tables(n_half):
    n = 2 * n_half
    idx = np.arange(n_half, dtype=np.int64)
    m = (idx[:, None] * idx[None, :]) % n
    ang = 2.0 * np.pi * m.astype(np.float64) / n
    cm = np.cos(ang)
    sm = -np.sin(ang)
    sm[0, :] = 1.0 - 2.0 * (idx % 2)
    return cm, sm


def _dense_tables(n_tok):
    n_half = n_tok // 2
    f = np.concatenate([np.arange(n_half), n_tok - np.arange(n_half)]).astype(np.int64)
    sidx = np.arange(n_tok, dtype=np.int64)
    ang = 2.0 * np.pi * ((f[:, None] * sidx[None, :]) % (2 * n_tok)).astype(np.float64) / (2 * n_tok)
    cd = np.cos(ang)
    sd = -np.sin(ang)
    half = 2.0 * np.pi * ((n_half * sidx) % (2 * n_tok)).astype(np.float64) / (2 * n_tok)
    sd[0, :] = np.cos(half)
    sd[n_half, :] = -np.sin(half)
    return cd, sd


def _tables(n_tok):
    n_half = n_tok // 2
    bf = lambda t: jnp.asarray(t.astype(np.float32)).astype(BF16)
    c, s = _dense_tables(n_tok) if n_half < RADIX2_MIN_HALF else _dft_tables(n_half)
    stacks = (bf(np.concatenate([c, s], axis=0)), bf(np.concatenate([c.T, s.T], axis=1)))
    if n_half < RADIX2_MIN_HALF:
        return stacks
    ang = np.pi * np.arange(n_half, dtype=np.float64) / n_tok
    tw = [jnp.asarray(np.broadcast_to(v[:, None], (n_half, HY_CW)).astype(np.float32))
          for v in (np.cos(ang), -np.sin(ang))]
    return stacks + (tw[0], tw[1])


def _put_cols(ref, x):
    for j in range(ref.shape[0]):
        ref[j] = x[:, LANES * j:LANES * (j + 1)]


def _get_cols(ref):
    return jnp.concatenate([ref[j] for j in range(ref.shape[0])], axis=1)


def _get_parity(ref, parity):
    n_half = ref.shape[1] // 2
    return jnp.concatenate([ref[j, pl.ds(parity, n_half, stride=2), :] for j in range(ref.shape[0])], axis=1)


def _put_parity(ref, parity, x):
    n_half = ref.shape[1] // 2
    for j in range(ref.shape[0]):
        ref[j, pl.ds(parity, n_half, stride=2), :] = x[:, LANES * j:LANES * (j + 1)]


def _set_row0(x, v):
    first = lax.broadcasted_iota(jnp.int32, (8, x.shape[1]), 0) == 0
    return jnp.concatenate([jnp.where(first, v, x[:8]), x[8:]], axis=0)


def _rfft_packed(x_ref, tabs):
    n_tok = x_ref.shape[1]
    n_half = n_tok // 2
    if len(tabs) == 2:
        r = _dot(tabs[0], _get_cols(x_ref).astype(BF16))
        return r[:n_half], r[n_tok:n_tok + n_half], r[n_half:n_tok], r[n_tok + n_half:]
    fwd, _, tw_re, tw_im = tabs
    e = _dot(fwd, _get_parity(x_ref, 0).astype(BF16))
    o = _dot(fwd, _get_parity(x_ref, 1).astype(BF16))
    e_re, e_im, o_re, o_im = e[:n_half], e[n_half:], o[:n_half], o[n_half:]
    t_re = tw_re * o_re - tw_im * o_im
    t_im = tw_re * o_im + tw_im * o_re
    a_im = _set_row0(e_im + t_im, e_im[0:1])
    b_im = _set_row0(t_im - e_im, -o_im[0:1])
    return e_re + t_re, a_im, e_re - t_re, b_im


def _irfft_packed(y_ref, ya_re, ya_im, yb_re, yb_im, tabs):
    if len(tabs) == 2:
        y = jnp.concatenate([ya_re, yb_re, ya_im, yb_im], axis=0).astype(BF16)
        _put_cols(y_ref, _dot(tabs[1], y))
        return
    _, inv, tw_re, tw_im = tabs
    p_e = ya_re + yb_re
    q_e = _set_row0(ya_im - yb_im, ya_im[0:1])
    _put_parity(y_ref, 0, _dot(inv, jnp.concatenate([p_e, q_e], axis=0).astype(BF16)))
    ra_re = ya_re * tw_re + ya_im * tw_im
    ra_im = ya_im * tw_re - ya_re * tw_im
    rb_re = yb_im * tw_im - yb_re * tw_re
    rb_im = -(yb_re * tw_im + yb_im * tw_re)
    p_o = ra_re + rb_re
    q_o = _set_row0(ra_im - rb_im, -yb_im[0:1])
    _put_parity(y_ref, 1, _dot(inv, jnp.concatenate([p_o, q_o], axis=0).astype(BF16)))


def _filt_kernel(n_tok, n_tab, z_ref, t_ref, w1_ref, b1_ref, w2_ref, b2_ref, fr_ref, w3f_ref, w3b_ref,
                 dl_ref, *rest):
    tabs = tuple(r[...] for r in rest[:n_tab])
    kar_ref, kai_ref, kbr_ref, kbi_ref, p_ref, q_ref = rest[n_tab:]
    fr = fr_ref[...]
    h = jnp.sin(fr * (_mm3(z_ref[...], w1_ref[...]) + b1_ref[...]))
    h = jnp.sin(fr * (_mm3(h, w2_ref[...]) + b2_ref[...]))
    decay = jnp.exp(-t_ref[...] * dl_ref[...])
    row = lax.broadcasted_iota(jnp.int32, decay.shape, 0)
    hf = _mm3(h, w3f_ref[...]) * decay
    hb = jnp.where(row == 0, 0.0, _mm3(h, w3b_ref[...]) * decay)
    _put_cols(p_ref, hf + hb)
    _put_cols(q_ref, hf - hb)
    pa_re, pa_im, pb_re, _ = _rfft_packed(p_ref, tabs)
    _, qa_im, _, qb_im = _rfft_packed(q_ref, tabs)
    row0 = lax.broadcasted_iota(jnp.int32, pa_re.shape, 0) == 0
    inv_n = 1.0 / (2 * n_tok)
    w_re = jnp.where(row0, inv_n, 2.0 * inv_n)
    kar_ref[...] = w_re * pa_re
    kbr_ref[...] = w_re * pb_re
    kai_ref[...] = (2.0 * inv_n) * _set_row0(qa_im, pa_im[0:1])
    kbi_ref[...] = (2.0 * inv_n) * qb_im


def _hyena_filters(n_tok, tabs, hy_f_w1, hy_f_b1, hy_f_w2, hy_f_b2, hy_f_w3, hy_freq):
    n_half = n_tok // 2
    tt = np.linspace(0.0, 1.0, n_tok)[:, None]
    ang = (2.0 * np.pi * np.arange(n_tok) / n_tok)[:, None] * np.linspace(1e-4, HY_BANDS - 1, HY_BANDS)[None, :]
    z = np.concatenate([tt, np.cos(ang), -np.sin(ang), np.zeros((n_tok, LANES - HY_EMB))], axis=-1)
    z, t = jnp.asarray(z, F32), jnp.asarray(tt, F32)
    deltas = jnp.asarray(np.abs(np.linspace(HY_MIN_DECAY, HY_MAX_DECAY, D_HY))[None, :], F32)
    w1 = jnp.pad(hy_f_w1, ((0, LANES - HY_EMB), (0, 0)))
    ncb = D_HY // HY_CW
    full = lambda j: (0, 0)
    out_sd = jax.ShapeDtypeStruct((n_half, 2 * D_HY), F32)
    tab_specs = [pl.BlockSpec(t_.shape, full, pipeline_mode=pl.Buffered(1)) for t_ in tabs]
    return pl.pallas_call(
        functools.partial(_filt_kernel, n_tok, len(tabs)),
        grid=(2 * ncb,),
        in_specs=[pl.BlockSpec((n_tok, LANES), full),
                  pl.BlockSpec((n_tok, 1), full),
                  pl.BlockSpec((LANES, HY_HID), full),
                  pl.BlockSpec((1, HY_HID), full),
                  pl.BlockSpec((HY_HID, HY_HID), full),
                  pl.BlockSpec((1, HY_HID), full),
                  pl.BlockSpec((1, HY_HID), full),
                  pl.BlockSpec((HY_HID, HY_CW), lambda j: (0, 2 * ncb * (j // ncb) + j % ncb)),
                  pl.BlockSpec((HY_HID, HY_CW), lambda j: (0, 2 * ncb * (j // ncb) + ncb + j % ncb)),
                  pl.BlockSpec((1, HY_CW), lambda j: (0, j % ncb))] + tab_specs,
        out_specs=[pl.BlockSpec((n_half, HY_CW), lambda j: (0, j))] * 4,
        out_shape=[out_sd] * 4,
        scratch_shapes=[pltpu.VMEM((HY_CW // LANES, n_tok, LANES), F32)] * 2,
        compiler_params=_cparams(("arbitrary",)),
        name=f"filt{n_tok}",
    )(z, t, w1, hy_f_b1.reshape(1, -1), hy_f_w2, hy_f_b2.reshape(1, -1), hy_freq.reshape(1, -1),
      hy_f_w3, hy_f_w3, deltas, *tabs)


def _hyena_kernel(n_tab, pv_ref, p1_ref, p2_ref, cwv_ref, cw1_ref, cw2_ref, cbv_ref, cb1_ref, cb2_ref,
                  fbias_ref, *rest):
    tabs = tuple(r[...] for r in rest[:n_tab])
    (kar0_ref, kai0_ref, kbr0_ref, kbi0_ref, kar1_ref, kai1_ref, kbr1_ref, kbi1_ref,
     o_ref, u_ref, y_ref) = rest[n_tab:]
    n_tok = pv_ref.shape[1]
    row = lax.broadcasted_iota(jnp.int32, (n_tok, pv_ref.shape[2]), 0)

    def short_conv(p_ref, cw_ref, cb_ref):
        p = p_ref[0]
        prev = jnp.where(row == 0, 0.0, pltpu.roll(p, 1, axis=0))
        nxt = jnp.where(row == n_tok - 1, 0.0, pltpu.roll(p, n_tok - 1, axis=0))
        return cb_ref[...] + prev * cw_ref[0:1, :] + p * cw_ref[1:2, :] + nxt * cw_ref[2:3, :]

    def fftconv(u, kar_ref, kai_ref, kbr_ref, kbi_ref, skip):
        _put_cols(u_ref, u)
        ua_re, ua_im, ub_re, ub_im = _rfft_packed(u_ref, tabs)
        ka_re, ka_im, kb_re, kb_im = kar_ref[...], kai_ref[...], kbr_ref[...], kbi_ref[...]
        zero_row = jnp.zeros_like(ka_im[0:1])
        kaz = _set_row0(ka_im, zero_row)
        kbz = _set_row0(kb_im, zero_row)
        ya_re = ua_re * ka_re - ua_im * kaz
        yb_re = ub_re * kb_re - ub_im * kbz
        h_re = ua_im[0:1] * ka_im[0:1] - ub_im[0:1] * kb_im[0:1]
        h_im = ua_im[0:1] * kb_im[0:1] + ub_im[0:1] * ka_im[0:1]
        ya_im = _set_row0(ua_re * ka_im + ua_im * ka_re, h_re)
        yb_im = _set_row0(ub_re * kb_im + ub_im * kb_re, h_im)
        _irfft_packed(y_ref, ya_re, ya_im, yb_re, yb_im, tabs)
        return _get_cols(y_ref) + u * skip

    v = short_conv(pv_ref, cwv_ref, cbv_ref)
    x1 = short_conv(p1_ref, cw1_ref, cb1_ref)
    z = x1 * fftconv(v, kar0_ref, kai0_ref, kbr0_ref, kbi0_ref, fbias_ref[0:1, :])
    x2 = short_conv(p2_ref, cw2_ref, cb2_ref)
    o_ref[0] = x2 * fftconv(z, kar1_ref, kai1_ref, kbr1_ref, kbi1_ref, fbias_ref[1:2, :])


def _hyena(proj_hy, tabs, filt, hy_conv_w, hy_conv_b, hy_fbias):
    bsz, n_tok, _ = proj_hy.shape
    n_half = n_tok // 2
    ncb = D_HY // HY_CW
    cb = hy_conv_b.reshape(1, -1)
    tab_specs = [pl.BlockSpec(t.shape, lambda b, c: (0, 0), pipeline_mode=pl.Buffered(1)) for t in tabs]

    def pspec(k):
        return pl.BlockSpec((1, n_tok, HY_CW), lambda b, c: (b, 0, k * ncb + c))

    def cwspec(k):
        return pl.BlockSpec((3, HY_CW), lambda b, c: (0, k * ncb + c))

    def cbspec(k):
        return pl.BlockSpec((1, HY_CW), lambda b, c: (0, k * ncb + c))

    def fspec(o):
        mode = pl.Buffered(1) if ncb == 1 else None
        return pl.BlockSpec((n_half, HY_CW), lambda b, c: (0, o * ncb + c), pipeline_mode=mode)

    return pl.pallas_call(
        functools.partial(_hyena_kernel, len(tabs)),
        grid=(bsz, ncb),
        in_specs=[pspec(0), pspec(1), pspec(2), cwspec(0), cwspec(1), cwspec(2),
                  cbspec(0), cbspec(1), cbspec(2),
                  pl.BlockSpec((2, HY_CW), lambda b, c: (0, c))] + tab_specs + [fspec(0)] * 4 + [fspec(1)] * 4,
        out_specs=pl.BlockSpec((1, n_tok, HY_CW), lambda b, c: (b, 0, c)),
        out_shape=jax.ShapeDtypeStruct((bsz, n_tok, D_HY), F32),
        scratch_shapes=[pltpu.VMEM((HY_CW // LANES, n_tok, LANES), F32)] * 2,
        compiler_params=_cparams(("arbitrary", "arbitrary")),
        name=f"hyena{n_tok}",
    )(proj_hy, proj_hy, proj_hy, hy_conv_w, hy_conv_w, hy_conv_w, cb, cb, cb, hy_fbias,
      *tabs, *filt, *filt)


def _s5ops_kernel(*refs):
    for g in range(S5OPS_GB):
        _s5ops_group(g, *refs)


def _s5ops_group(g, are_ref, aim_ref, ldt_ref, btr_ref, bti_ref, cre_ref, cim_ref, d_ref,
                 opa_ref, opb_ref, g_ref, atr_ref, ati_ref, er_ref, ei_ref):
    a_re, a_im = are_ref[g], aim_ref[g]
    dt = jnp.exp(ldt_ref[g])
    mag = jnp.exp(a_re * dt)
    ab_re = mag * jnp.cos(a_im * dt)
    ab_im = mag * jnp.sin(a_im * dt)
    n_re, n_im = ab_re - 1.0, ab_im
    den = a_re * a_re + a_im * a_im
    q_re = (n_re * a_re + n_im * a_im) / den
    q_im = (n_im * a_re - n_re * a_im) / den
    bt_re, bt_im = btr_ref[g], bti_ref[g]
    bb_re = q_re * bt_re - q_im * bt_im
    bb_im = q_re * bt_im + q_im * bt_re
    c_re, c_im = cre_ref[g, 0:S5_CH, :], cim_ref[g, 0:S5_CH, :]
    pw = [(jnp.ones_like(ab_re), jnp.zeros_like(ab_re))]
    for _ in range(S5_CHUNK):
        pr, pi = pw[-1]
        pw.append((pr * ab_re - pi * ab_im, pr * ab_im + pi * ab_re))
    lane = lax.broadcasted_iota(jnp.int32, ab_re.shape, 1)
    fwd = lane < S5_STATE
    for s in range(S5_CHUNK):
        e_re = jnp.where(fwd, pw[S5_CHUNK - 1 - s][0], pw[s][0])
        e_im = jnp.where(fwd, pw[S5_CHUNK - 1 - s][1], pw[s][1])
        er_ref[g, pl.ds(S5_CH * s, S5_CH), :] = e_re * bb_re - e_im * bb_im
        ei_ref[g, pl.ds(S5_CH * s, S5_CH), :] = e_re * bb_im + e_im * bb_re
        g_re = jnp.where(fwd, pw[s + 1][0], pw[S5_CHUNK - s][0])
        g_im = jnp.where(fwd, pw[s + 1][1], pw[S5_CHUNK - s][1])
        g_ref[g, pl.ds(S5_CH * s, S5_CH), 0:LANES] = (c_re * g_re - c_im * g_im).astype(BF16)
        g_ref[g, pl.ds(S5_CH * s, S5_CH), LANES:2 * LANES] = (-(c_re * g_im + c_im * g_re)).astype(BF16)
    atr_ref[g] = pw[S5_CHUNK][0]
    ati_ref[g] = pw[S5_CHUNK][1]
    er, ei = er_ref[g], ei_ref[g]
    opa_ref[g, :, 2 * LANES:3 * LANES], opb_ref[g, :, 0:LANES] = _split(er)
    opa_ref[g, :, 3 * LANES:4 * LANES], opb_ref[g, :, LANES:2 * LANES] = _split(ei)
    lane2 = lax.broadcasted_iota(jnp.int32, er.shape, 1)
    row2 = lax.broadcasted_iota(jnp.int32, er.shape, 0)
    f2 = lane2 < S5_STATE
    zero = jnp.zeros_like(er)

    cp_re, cp_im = cre_ref[g], cim_ref[g]
    kf = _mm3_t(jnp.where(f2, er, zero), cp_re) - _mm3_t(jnp.where(f2, ei, zero), cp_im)
    kb = _mm3_t(jnp.where(f2, zero, er), cp_re) - _mm3_t(jnp.where(f2, zero, ei), cp_im)
    d_row = d_ref[g]
    steps_per_vreg = LANES // S5_CH
    for half in range(S5_CHUNK // steps_per_vreg):
        acc = zero
        for tt in range(steps_per_vreg):
            t = half * steps_per_vreg + tt
            nf = S5_CH * (S5_CHUNK - 1 - t)
            nb = S5_CH * t
            col_f = jnp.concatenate([kf[nf:], zero[:nf]], axis=0) if nf else kf
            col_b = jnp.concatenate([zero[:nb], kb[:S5_ROW - nb]], axis=0) if nb else kb
            diag = jnp.where((row2 // S5_CH == t) & (row2 % S5_CH == lane2), d_row, 0.0)
            col = col_f + col_b + diag
            r = pltpu.roll(col, S5_CH * tt, axis=1) if tt else col
            acc = jnp.where((lane2 >= S5_CH * tt) & (lane2 < S5_CH * (tt + 1)), r, acc)
        opa_ref[g, :, LANES * half:LANES * (half + 1)] = acc.astype(BF16)


def _s5_operators(s5_a_re, s5_a_im, s5_log_dt, s5_b_re, s5_b_im, s5_c_re, s5_c_im, s5_d):
    g, p, h = S5_GROUPS, S5_STATE, S5_CH
    cat = lambda x: jnp.concatenate([x[0], x[1]], axis=-1)
    a_re = cat(s5_a_re).reshape(g, 1, 2 * p)
    a_im = cat(s5_a_im).reshape(g, 1, 2 * p)
    ldt = cat(jnp.broadcast_to(s5_log_dt[:, :, None], (2, g, p))).reshape(g, 1, 2 * p)
    bt_re = cat(jnp.swapaxes(s5_b_re, -1, -2))
    bt_im = cat(jnp.swapaxes(s5_b_im, -1, -2))
    cpad = lambda c: jnp.pad(jnp.concatenate([c, c], axis=-1), ((0, 0), (0, LANES - h), (0, 0)))
    c_re, c_im = cpad(s5_c_re), cpad(s5_c_im)
    d_row = jnp.pad(s5_d.reshape(g, 1, h), ((0, 0), (0, 0), (0, LANES - h)))
    vec = pl.BlockSpec((S5OPS_GB, 1, 2 * p), lambda i: (i, 0, 0))
    hp = pl.BlockSpec((S5OPS_GB, h, 2 * p), lambda i: (i, 0, 0))
    sq = pl.BlockSpec((S5OPS_GB, LANES, 2 * p), lambda i: (i, 0, 0))
    wide = lambda n: pl.BlockSpec((S5OPS_GB, S5_ROW, n), lambda i: (i, 0, 0))
    vec_sd = jax.ShapeDtypeStruct((g, 1, 2 * p), F32)
    return pl.pallas_call(
        _s5ops_kernel,
        grid=(g // S5OPS_GB,),
        in_specs=[vec, vec, vec, hp, hp, sq, sq, vec],
        out_specs=[wide(2 * S5_ROW), wide(S5_ROW), wide(S5_ROW), vec, vec],
        out_shape=[jax.ShapeDtypeStruct((g, S5_ROW, 2 * S5_ROW), BF16),
                   jax.ShapeDtypeStruct((g, S5_ROW, S5_ROW), BF16),
                   jax.ShapeDtypeStruct((g, S5_ROW, S5_ROW), BF16), vec_sd, vec_sd],
        scratch_shapes=[pltpu.VMEM((S5OPS_GB, S5_ROW, 2 * p), F32)] * 2,
        compiler_params=_cparams(("arbitrary",)),
        name="s5ops",
    )(a_re, a_im, ldt, bt_re, bt_im, c_re, c_im, d_row)


def _block_transpose(xs):
    n = len(xs)
    lane = lax.broadcasted_iota(jnp.int32, xs[0].shape, 1)
    xs = list(xs)
    d = n // 2
    while d:
        keep = ((lane // S5_CH) & d) == 0
        for i in range(n):
            if i & d:
                continue
            lo, hi = xs[i], xs[i + d]
            xs[i] = jnp.where(keep, lo, pltpu.roll(hi, S5_CH * d, axis=1))
            xs[i + d] = jnp.where(keep, pltpu.roll(lo, LANES - S5_CH * d, axis=1), hi)
        d //= 2
    return xs


def _s5_kernel(bsz, n_chunks, u_ref, opa_ref, opb_ref, g_ref,
               atr_ref, ati_ref, h0r_ref, h0i_ref, y_ref, fr_ref, fi_ref,
               ua_ref, ub_ref, ya_ref, yb_ref, sr_ref, si_ref, xfr_ref, xfi_ref, xbr_ref, xbi_ref):
    nc = n_chunks
    spv = LANES // S5_CH
    rsub = min(nc, 32)

    def to_chunks(b, carry):
        for half, dst in ((0, ua_ref), (1, ub_ref)):
            for r0 in range(0, nc, rsub):
                xs = [u_ref[b, pl.ds(S5_CHUNK * r0 + half * spv + tt, rsub, stride=S5_CHUNK), :]
                      for tt in range(spv)]
                for k, blk in enumerate(_block_transpose(xs)):
                    dst[k, pl.ds(r0 * bsz + b, rsub, stride=bsz), :] = blk
        return carry

    lax.fori_loop(0, bsz, to_chunks, 0, unroll=2)

    lane = lax.broadcasted_iota(jnp.int32, (bsz, 2 * S5_STATE), 1)
    fwd = lane < S5_STATE
    lane_all = lax.broadcasted_iota(jnp.int32, (bsz * nc, 2 * S5_STATE), 1)
    fwd_all = lane_all < S5_STATE

    def group(k, slot):
        u = jnp.concatenate([ua_ref[k], ub_ref[k]], axis=1)
        uh, ul = _split(u)
        wide = _dot(uh, opa_ref[k])
        inj = (wide[:, S5_ROW:] + _dot(ul, opa_ref[k, :, S5_ROW:2 * S5_ROW]) + _dot(uh, opb_ref[k]))
        sr_ref[slot] = inj[:, :LANES]
        si_ref[slot] = inj[:, LANES:]
        at_re, at_im = atr_ref[k], ati_ref[k]
        y_intra = wide[:, :S5_ROW]

        def step(i, xc):
            x_re, x_im = xc
            rf = pl.ds(pl.multiple_of(i * bsz, bsz), bsz)
            rb = pl.ds(pl.multiple_of((nc - 1 - i) * bsz, bsz), bsz)
            xfr_ref[slot, rf, :] = x_re
            xfi_ref[slot, rf, :] = x_im
            xbr_ref[slot, rb, :] = x_re
            xbi_ref[slot, rb, :] = x_im
            s_re = jnp.where(fwd, sr_ref[slot, rf, :], sr_ref[slot, rb, :])
            s_im = jnp.where(fwd, si_ref[slot, rf, :], si_ref[slot, rb, :])
            return (at_re * x_re - at_im * x_im + s_re, at_re * x_im + at_im * x_re + s_im)

        x_re, x_im = lax.fori_loop(0, nc, step, (h0r_ref[k], h0i_ref[k]), unroll=True)
        fr_ref[k] = x_re
        fi_ref[k] = x_im
        xp = jnp.concatenate([jnp.where(fwd_all, xfr_ref[slot], xbr_ref[slot]),
                              jnp.where(fwd_all, xfi_ref[slot], xbi_ref[slot])], axis=1).astype(BF16)
        y = y_intra + _dot_t(xp, g_ref[k])
        ya_ref[k] = y[:, :LANES]
        yb_ref[k] = y[:, LANES:]

    def group_pair(j, carry):
        group(2 * j, 0)
        group(2 * j + 1, 1)
        return carry

    lax.fori_loop(0, S5_GB // 2, group_pair, 0)

    def to_tokens(b, carry):
        for half, src in ((0, ya_ref), (1, yb_ref)):
            for r0 in range(0, nc, rsub):
                ys = [src[k, pl.ds(r0 * bsz + b, rsub, stride=bsz), :] for k in range(S5_GB)]
                for tt, blk in enumerate(_block_transpose(ys)):
                    y_ref[b, pl.ds(S5_CHUNK * r0 + half * spv + tt, rsub, stride=S5_CHUNK), :] = blk
        return carry

    lax.fori_loop(0, bsz, to_tokens, 0, unroll=2)


def _s5(u, ops, h0_re, h0_im):
    bsz, n_tok, _ = u.shape
    g, p = S5_GROUPS, S5_STATE
    nc = n_tok // S5_CHUNK
    rows = nc * bsz
    tok = pl.BlockSpec((bsz, n_tok, LANES), lambda j: (0, 0, j))
    gspec = lambda shape: pl.BlockSpec((S5_GB,) + shape, lambda j: (j, 0, 0))
    return pl.pallas_call(
        functools.partial(_s5_kernel, bsz, nc),
        grid=(g // S5_GB,),
        in_specs=[tok, gspec((S5_ROW, 2 * S5_ROW)), gspec((S5_ROW, S5_ROW)), gspec((S5_ROW, S5_ROW)),
                  gspec((1, 2 * p)), gspec((1, 2 * p)), gspec((bsz, 2 * p)), gspec((bsz, 2 * p))],
        out_specs=[tok, gspec((bsz, 2 * p)), gspec((bsz, 2 * p))],
        out_shape=[jax.ShapeDtypeStruct((bsz, n_tok, D_S5), F32),
                   jax.ShapeDtypeStruct((g, bsz, 2 * p), F32),
                   jax.ShapeDtypeStruct((g, bsz, 2 * p), F32)],
        scratch_shapes=([pltpu.VMEM((S5_GB, rows, LANES), F32)] * 4
                        + [pltpu.VMEM((2, rows, 2 * p), F32)] * 6),
        compiler_params=_cparams(("arbitrary",)),
        name=f"s5_{n_tok}",
    )(u, *ops, h0_re, h0_im)


def _in_kernel(has_pos, *refs):
    if has_pos:
        x_ref, pos_ref, mod_ref, w_ref, hy_ref, s5_ref = refs
        x = x_ref[0] + pos_ref[...]
    else:
        x_ref, mod_ref, w_ref, hy_ref, s5_ref = refs
        x = x_ref[0]
    sh1 = mod_ref[0, :, 0:D_MODEL]
    sc1 = mod_ref[0, :, D_MODEL:2 * D_MODEL]
    h = _norm(x) * (1.0 + sc1) + sh1
    proj = _dot(h.astype(BF16), w_ref[...].astype(BF16))
    hy_ref[0] = proj[:, :3 * D_HY]
    s5_ref[0] = proj[:, 3 * D_HY:]


def _in_proj(x3, pos, mod3, w_in, tm):
    nb, lt, _ = x3.shape
    has_pos = pos is not None
    per_batch = mod3.shape[0] > 1
    midx = (lambda b, i: (b, 0, 0)) if per_batch else (lambda b, i: (0, 0, 0))
    in_specs = [pl.BlockSpec((1, tm, D_MODEL), lambda b, i: (b, i, 0))]
    args = [x3]
    if has_pos:
        in_specs.append(pl.BlockSpec((tm, D_MODEL), lambda b, i: (i, 0)))
        args.append(pos)
    in_specs += [pl.BlockSpec((1, 1, 6 * D_MODEL), midx),
                 pl.BlockSpec((D_MODEL, 3 * D_HY + D_S5), lambda b, i: (0, 0), pipeline_mode=pl.Buffered(1))]
    args += [mod3, w_in]
    return pl.pallas_call(
        functools.partial(_in_kernel, has_pos),
        grid=(nb, lt // tm),
        in_specs=in_specs,
        out_specs=[pl.BlockSpec((1, tm, 3 * D_HY), lambda b, i: (b, i, 0)),
                   pl.BlockSpec((1, tm, D_S5), lambda b, i: (b, i, 0))],
        out_shape=[jax.ShapeDtypeStruct((nb, lt, 3 * D_HY), F32),
                   jax.ShapeDtypeStruct((nb, lt, D_S5), F32)],
        compiler_params=_cparams(("arbitrary", "arbitrary")),
        name=f"in_proj{nb}",
    )(*args)


def _route(logits):
    lane = lax.broadcasted_iota(jnp.int32, logits.shape, 1)
    lane_f = lane.astype(F32)
    neg = -jnp.inf
    big = float(LANES)
    m1 = (lane >= N_EXPERTS) & (lane < N_EXPERTS + N_EGROUPS)
    l1 = jnp.where(m1, logits, neg)
    top1 = jnp.max(l1, axis=-1, keepdims=True)
    grp = jnp.min(jnp.where(l1 == top1, lane_f, big), axis=-1, keepdims=True) - float(N_EXPERTS)
    den = jnp.sum(jnp.where(m1, jnp.exp(logits - top1), 0.0), axis=-1, keepdims=True)
    p_grp = 1.0 / den
    lo = grp * float(N_EPG)
    m2 = (lane_f >= lo) & (lane_f < lo + float(N_EPG))
    l2 = jnp.where(m2, logits, neg)
    v1 = jnp.max(l2, axis=-1, keepdims=True)
    i1 = jnp.min(jnp.where(l2 == v1, lane_f, big), axis=-1, keepdims=True)
    l2b = jnp.where(lane_f == i1, neg, l2)
    v2 = jnp.max(l2b, axis=-1, keepdims=True)
    i2 = jnp.min(jnp.where(l2b == v2, lane_f, big), axis=-1, keepdims=True)
    e = jnp.exp(v2 - v1)
    w1 = 1.0 / (1.0 + e)
    w2 = e / (1.0 + e)
    gates = jnp.where(lane_f == i1, w1 * p_grp, 0.0) + jnp.where(lane_f == i2, w2 * p_grp, 0.0)
    return jnp.where(lane_f == grp + float(N_EXPERTS), 1.0, gates)


def _out_kernel(n_ctx_blocks, xc_ref, xl_ref, pos_ref, yhyc_ref, yhyl_ref, ys5c_ref, ys5l_ref, mod_ref,
                wglu_ref, bglu_ref, ong_ref, wout_ref, ln1g_ref, ln1b_ref, wrh_ref, wrl_ref, br_ref,
                x1_ref, h2_ref, gate_ref, cnt_ref):
    is_ctx = pl.program_id(0) < n_ctx_blocks
    x = jnp.where(is_ctx, xc_ref[...], xl_ref[...] + pos_ref[...])
    y = jnp.where(is_ctx, ys5c_ref[...], ys5l_ref[...])
    y_hy = jnp.where(is_ctx, yhyc_ref[...], yhyl_ref[...])
    s5 = jax.nn.gelu(y) * jax.nn.sigmoid(_dot(y.astype(BF16), wglu_ref[...]) + bglu_ref[...])
    m_hy = _rms(y_hy) * ong_ref[:, 0:D_HY]
    m_s5 = _rms(s5) * ong_ref[:, D_HY:]
    o = (_dot(m_hy.astype(BF16), wout_ref[0:D_HY, :]) + _dot(m_s5.astype(BF16), wout_ref[D_HY:, :]))
    g1 = mod_ref[0, :, 2 * D_MODEL:3 * D_MODEL]
    sh2 = mod_ref[0, :, 3 * D_MODEL:4 * D_MODEL]
    sc2 = mod_ref[0, :, 4 * D_MODEL:5 * D_MODEL]
    x1 = _norm(ALPHA * x + g1 * o) * ln1g_ref[...] + ln1b_ref[...]
    x1_ref[...] = x1
    h2 = _norm(x1) * (1.0 + sc2) + sh2
    h2_ref[...] = h2.astype(BF16)
    hh, hl = _split(h2)
    logits = (_dot(hh, wrh_ref[...]) + _dot(hl, wrh_ref[...]) + _dot(hh, wrl_ref[...]) + br_ref[...])
    gates = _route(logits)
    gate_ref[...] = gates
    cnt_ref[0] = jnp.sum(gates, axis=0, keepdims=True)


def _out_proj(xc, xl, pos, yhy_c, yhy_l, ys5_c, ys5_l, mod, wglu_bf, bglu, ong, wout_bf, ln1g, ln1b,
              wr_hi, wr_lo, br, tm):
    n_ctx, n_lat = xc.shape[0], xl.shape[0]
    l_lat = pos.shape[0]
    ncb, nlb, npb = n_ctx // tm, n_lat // tm, l_lat // tm
    ctx = lambda w: pl.BlockSpec((tm, w), lambda i: (jnp.minimum(i, ncb - 1), 0))
    lat = lambda w: pl.BlockSpec((tm, w), lambda i: (jnp.maximum(i - ncb, 0), 0))
    full = lambda shape: pl.BlockSpec(shape, lambda i: (0,) * len(shape))
    out = lambda w: pl.BlockSpec((tm, w), lambda i: (i, 0))
    mod_idx = lambda i: (jnp.where(i < ncb, 0, 1 + jnp.maximum(i - ncb, 0) // npb), 0, 0)
    n_all = n_ctx + n_lat
    return pl.pallas_call(
        functools.partial(_out_kernel, ncb),
        grid=(ncb + nlb,),
        in_specs=[ctx(D_MODEL), lat(D_MODEL),
                  pl.BlockSpec((tm, D_MODEL), lambda i: (jnp.maximum(i - ncb, 0) % npb, 0)),
                  ctx(D_HY), lat(D_HY), ctx(D_S5), lat(D_S5),
                  pl.BlockSpec((1, 1, 6 * D_MODEL), mod_idx),
                  full((D_S5, D_S5)), full((1, D_S5)), full((1, D_MODEL)), full((D_MODEL, D_MODEL)),
                  full((1, D_MODEL)), full((1, D_MODEL)), full((D_MODEL, LANES)), full((D_MODEL, LANES)),
                  full((1, LANES))],
        out_specs=[out(D_MODEL), out(D_MODEL), out(LANES), pl.BlockSpec((1, 1, LANES), lambda i: (i, 0, 0))],
        out_shape=[jax.ShapeDtypeStruct((n_all, D_MODEL), F32),
                   jax.ShapeDtypeStruct((n_all, D_MODEL), BF16),
                   jax.ShapeDtypeStruct((n_all, LANES), F32),
                   jax.ShapeDtypeStruct((n_all // tm, 1, LANES), F32)],
        compiler_params=_cparams(("arbitrary",)),
        name="out_proj",
    )(xc, xl, pos, yhy_c, yhy_l, ys5_c, ys5_l, mod.reshape(mod.shape[0], 1, 6 * D_MODEL),
      wglu_bf, bglu, ong, wout_bf, ln1g, ln1b, wr_hi, wr_lo, br)


def _perm_t(gates, loc_ref, s):
    n = gates.shape[0]
    lane = lax.broadcasted_iota(jnp.int32, gates.shape, 1)
    oh = jnp.where((lane >= N_EXPERTS) & (lane < N_EXPERTS + N_EGROUPS), gates, 0.0)
    r = lax.broadcasted_iota(jnp.int32, (n, n), 0)
    c = lax.broadcasted_iota(jnp.int32, (n, n), 1)
    earlier = jnp.where(c < r, 1.0, 0.0).astype(BF16)
    cum = _dot(earlier, oh.astype(BF16))
    rank = jnp.sum(cum * oh, axis=-1, keepdims=True)
    lane1 = lax.broadcasted_iota(jnp.int32, (1, LANES), 1)
    locv = jnp.zeros((1, LANES), F32)
    for grp in range(N_EGROUPS):
        locv = jnp.where(lane1 == N_EXPERTS + grp, loc_ref[N_EGROUPS * s + grp].astype(F32), locv)
    dest = rank + jnp.sum(oh * locv, axis=-1, keepdims=True)
    slot = lax.broadcasted_iota(jnp.int32, (n, MOE_SLOTS), 1).astype(F32)
    return jnp.where(slot == dest, 1.0, 0.0)


def _segment_copies(s, loc_ref, len_ref, off_ref, make):
    for grp in range(N_EGROUPS):
        loc = loc_ref[N_EGROUPS * s + grp]
        off = off_ref[N_EGROUPS * s + grp]
        length = len_ref[N_EGROUPS * s + grp]
        n_big = length // MOE_BIG
        done = n_big * MOE_BIG

        def big(i, carry):
            make(pl.multiple_of(loc + MOE_BIG * i, MOE_UNIT), pl.multiple_of(off + MOE_BIG * i, MOE_UNIT), MOE_BIG)
            return carry

        def unit(i, carry):
            make(pl.multiple_of(loc + done + MOE_UNIT * i, MOE_UNIT),
                 pl.multiple_of(off + done + MOE_UNIT * i, MOE_UNIT), MOE_UNIT)
            return carry

        lax.fori_loop(0, n_big, big, 0)
        lax.fori_loop(0, (length - done) // MOE_UNIT, unit, 0)


def _pad_copies(pad_ref, n_blocks, zx_v, zg_v, xs_hbm, gs_hbm, sem, op):
    def unit(row, rows):
        getattr(pltpu.make_async_copy(zx_v.at[pl.ds(0, rows), :], xs_hbm.at[pl.ds(row, rows), :], sem.at[0]), op)()
        getattr(pltpu.make_async_copy(zg_v.at[pl.ds(0, rows), :], gs_hbm.at[pl.ds(row, rows), :], sem.at[1]), op)()

    for grp in range(N_EGROUPS):
        start = pad_ref[grp]

        def body(i, carry):
            unit(pl.multiple_of(start + MOE_UNIT * i, MOE_UNIT), MOE_UNIT)
            return carry

        lax.fori_loop(0, pad_ref[N_EGROUPS + grp], body, 0)

    def tail(b, carry):
        unit(pl.multiple_of(b * MOE_TM, MOE_TM), MOE_TM)
        return carry

    lax.fori_loop(pad_ref[2 * N_EGROUPS], n_blocks, tail, 0)


def _moe_sort_kernel(n_blocks, loc_ref, len_ref, off_ref, pad_ref, h_ref, gate_ref, xs_hbm, gs_hbm,
                     xs_v, gs_v, zx_v, zg_v, sem, zsem):
    s = pl.program_id(0)
    slot = s % 2

    @pl.when(s == 0)
    def _():
        zx_v[...] = jnp.zeros_like(zx_v)
        zg_v[...] = jnp.zeros_like(zg_v)
        _pad_copies(pad_ref, n_blocks, zx_v, zg_v, xs_hbm, gs_hbm, zsem, 'start')

    gates = gate_ref[...]
    p = _perm_t(gates, loc_ref, s).T.astype(BF16)
    xs_v[slot] = _dot(p, h_ref[...]).astype(BF16)
    g_hi = gates.astype(BF16)
    r1 = gates - g_hi.astype(F32)
    g_mid = r1.astype(BF16)
    g_lo = (r1 - g_mid.astype(F32)).astype(BF16)
    parts = _dot(p, jnp.concatenate([g_hi, g_mid, g_lo], axis=1))
    gs_v[slot] = parts[:, :LANES] + parts[:, LANES:2 * LANES] + parts[:, 2 * LANES:]

    def copies(buf):
        def x_copy(lr, gr, rows):
            return pltpu.make_async_copy(xs_v.at[buf, pl.ds(lr, rows), :],
                                         xs_hbm.at[pl.ds(gr, rows), :], sem.at[0, buf])

        def g_copy(lr, gr, rows):
            return pltpu.make_async_copy(gs_v.at[buf, pl.ds(lr, rows), :],
                                         gs_hbm.at[pl.ds(gr, rows), :], sem.at[1, buf])

        def start(lr, gr, rows):
            x_copy(lr, gr, rows).start()
            g_copy(lr, gr, rows).start()

        def wait(lr, gr, rows):
            x_copy(lr, gr, rows).wait()
            g_copy(lr, gr, rows).wait()

        return start, wait

    _segment_copies(s, loc_ref, len_ref, off_ref, copies(slot)[0])

    @pl.when(s > 0)
    def _():
        _segment_copies(s - 1, loc_ref, len_ref, off_ref, copies(1 - slot)[1])

    @pl.when(s == pl.num_programs(0) - 1)
    def _():
        _segment_copies(s, loc_ref, len_ref, off_ref, copies(slot)[1])
        _pad_copies(pad_ref, n_blocks, zx_v, zg_v, xs_hbm, gs_hbm, zsem, 'wait')


def _moe_expert_kernel(bg_ref, nb_ref, xs_ref, gs_ref, wg_ref, wu_ref, wd_ref, o_ref):
    i = pl.program_id(0)

    @pl.when(i < nb_ref[0])
    def _():
        grp = bg_ref[i]
        x = xs_ref[...]
        gates = gs_ref[...]
        lane = lax.broadcasted_iota(jnp.int32, gates.shape, 1)
        acc = jnp.zeros(o_ref.shape, F32)
        for e in range(N_EPG):
            a = _dot(x, wg_ref[e].astype(BF16))
            u = _dot(x, wu_ref[e].astype(BF16))
            ge = jnp.sum(jnp.where(lane == N_EPG * grp + e, gates, 0.0), axis=-1, keepdims=True)
            hid = jax.nn.silu(a) * u * ge
            acc = acc + _dot(hid.astype(BF16), wd_ref[e].astype(BF16))
        o_ref[...] = acc.astype(BF16)

    @pl.when(i >= nb_ref[0])
    def _():
        o_ref[...] = jnp.zeros_like(o_ref)


def _moe_combine_kernel(loc_ref, len_ref, off_ref, gate_ref, x1_ref, mod_ref, ln2g_ref, ln2b_ref, o_hbm,
                        ctx_ref, lat_ref, o_v, sem, *, n_ctx_tiles):
    s = pl.program_id(0)
    slot = s % 2

    def copies(buf):
        def o_copy(lr, gr, rows):
            return pltpu.make_async_copy(o_hbm.at[pl.ds(gr, rows), :],
                                         o_v.at[buf, pl.ds(lr, rows), :], sem.at[buf])

        return (lambda lr, gr, rows: o_copy(lr, gr, rows).start()), (lambda lr, gr, rows: o_copy(lr, gr, rows).wait())

    @pl.when(s == 0)
    def _():
        o_v[...] = jnp.zeros_like(o_v)
        _segment_copies(s, loc_ref, len_ref, off_ref, copies(slot)[0])

    @pl.when(s + 1 < pl.num_programs(0))
    def _():
        _segment_copies(s + 1, loc_ref, len_ref, off_ref, copies(1 - slot)[0])

    pt = _perm_t(gate_ref[...], loc_ref, s).astype(BF16)
    _segment_copies(s, loc_ref, len_ref, off_ref, copies(slot)[1])
    f = _dot(pt, o_v[slot])
    g2 = mod_ref[0, :, 5 * D_MODEL:6 * D_MODEL]
    x2 = _norm(ALPHA * x1_ref[...] + g2 * f) * ln2g_ref[...] + ln2b_ref[...]

    @pl.when(s < n_ctx_tiles)
    def _():
        ctx_ref[...] = x2

    @pl.when(s >= n_ctx_tiles)
    def _():
        lat_ref[...] = x2


def _moe_plan(tile_counts, n_blocks):
    cnt = tile_counts[:, 0, N_EXPERTS:N_EXPERTS + N_EGROUPS].astype(jnp.int32)
    len16 = ((cnt + MOE_UNIT - 1) // MOE_UNIT) * MOE_UNIT
    loc = jnp.cumsum(len16, axis=1) - len16
    rows_g = jnp.sum(len16, axis=0)
    reg_g = ((rows_g + MOE_TM - 1) // MOE_TM) * MOE_TM
    reg_start = jnp.cumsum(reg_g) - reg_g
    off = reg_start[None, :] + jnp.cumsum(len16, axis=0) - len16
    blk_end = jnp.cumsum(reg_g // MOE_TM)
    bi = jnp.arange(n_blocks, dtype=jnp.int32)
    blk_group = jnp.minimum(jnp.sum((bi[:, None] >= blk_end[None, :]).astype(jnp.int32), axis=1),
                            N_EGROUPS - 1)
    flat = lambda a: a.reshape(-1).astype(jnp.int32)
    pads = jnp.concatenate([reg_start + rows_g, (reg_g - rows_g) // MOE_UNIT, blk_end[-1:]])
    return (flat(loc), flat(len16), flat(off), flat(pads), blk_group.astype(jnp.int32),
            blk_end[-1:].astype(jnp.int32))


def _moe(h2_all, gates_all, tile_counts, x1_all, mod, w_gate, w_up, w_down, ln2g, ln2b, n_ctx,
         tokens_per_mod_row):
    n_tok = h2_all.shape[0]
    n_tiles = n_tok // MOE_ST
    n_ctx_tiles = n_ctx // MOE_ST
    max_rows = n_tok + n_tiles * N_EGROUPS * (MOE_UNIT - 1) + N_EGROUPS * (MOE_TM - 1)
    n_blocks = -(-max_rows // MOE_TM)
    n_rows = n_blocks * MOE_TM
    loc, len16, off, pads, blk_group, n_used = _moe_plan(tile_counts, n_blocks)

    tile = lambda w: pl.BlockSpec((MOE_ST, w), lambda s, *_: (s, 0))
    anyspec = pl.BlockSpec(memory_space=pl.ANY)
    xs, gs = pl.pallas_call(
        functools.partial(_moe_sort_kernel, n_blocks),
        grid_spec=pltpu.PrefetchScalarGridSpec(
            num_scalar_prefetch=4, grid=(n_tiles,),
            in_specs=[tile(D_MODEL), tile(LANES)],
            out_specs=[anyspec, anyspec],
            scratch_shapes=[pltpu.VMEM((2, MOE_SLOTS, D_MODEL), BF16), pltpu.VMEM((2, MOE_SLOTS, LANES), F32),
                            pltpu.VMEM((MOE_TM, D_MODEL), BF16), pltpu.VMEM((MOE_TM, LANES), F32),
                            pltpu.SemaphoreType.DMA((2, 2)), pltpu.SemaphoreType.DMA((2,))]),
        out_shape=[jax.ShapeDtypeStruct((n_rows, D_MODEL), BF16),
                   jax.ShapeDtypeStruct((n_rows, LANES), F32)],
        compiler_params=_cparams(("arbitrary",)),
        name="moe_sort",
    )(loc, len16, off, pads, h2_all, gates_all)

    blk = lambda w: pl.BlockSpec((MOE_TM, w), lambda i, bg, nb: (jnp.minimum(i, nb[0] - 1), 0))
    wspec = lambda a, b, mode: pl.BlockSpec((N_EPG, a, b), lambda i, bg, nb: (bg[i], 0, 0),
                                            pipeline_mode=mode)
    o_sorted = pl.pallas_call(
        _moe_expert_kernel,
        grid_spec=pltpu.PrefetchScalarGridSpec(
            num_scalar_prefetch=2, grid=(n_blocks,),
            in_specs=[blk(D_MODEL), blk(LANES), wspec(D_MODEL, D_EXPERT, None),
                      wspec(D_MODEL, D_EXPERT, None), wspec(D_EXPERT, D_MODEL, None)],
            out_specs=pl.BlockSpec((MOE_TM, D_MODEL), lambda i, bg, nb: (i, 0))),
        out_shape=jax.ShapeDtypeStruct((n_rows, D_MODEL), BF16),
        compiler_params=_cparams(("arbitrary",)),
        name="moe_experts",
    )(blk_group, n_used, xs, gs, w_gate, w_up, w_down)

    lat_per_row = tokens_per_mod_row // MOE_ST

    def mod_idx(s, *_):
        return (jnp.where(s < n_ctx_tiles, 0, 1 + (s - n_ctx_tiles) // lat_per_row), 0, 0)

    vec = pl.BlockSpec((1, D_MODEL), lambda s, *_: (0, 0))
    return pl.pallas_call(
        functools.partial(_moe_combine_kernel, n_ctx_tiles=n_ctx_tiles),
        grid_spec=pltpu.PrefetchScalarGridSpec(
            num_scalar_prefetch=3, grid=(n_tiles,),
            in_specs=[tile(LANES), tile(D_MODEL), pl.BlockSpec((1, 1, 6 * D_MODEL), mod_idx), vec, vec,
                      anyspec],
            out_specs=[pl.BlockSpec((MOE_ST, D_MODEL), lambda s, *_: (jnp.minimum(s, n_ctx_tiles - 1), 0)),
                       pl.BlockSpec((MOE_ST, D_MODEL), lambda s, *_: (jnp.maximum(s - n_ctx_tiles, 0), 0))],
            scratch_shapes=[pltpu.VMEM((2, MOE_SLOTS, D_MODEL), BF16), pltpu.SemaphoreType.DMA((2,))]),
        out_shape=[jax.ShapeDtypeStruct((n_ctx, D_MODEL), F32),
                   jax.ShapeDtypeStruct((n_tok - n_ctx, D_MODEL), F32)],
        compiler_params=_cparams(("arbitrary",)),
        name="moe_combine",
    )(loc, len16, off, gates_all, x1_all, mod.reshape(mod.shape[0], 1, 6 * D_MODEL), ln2g, ln2b, o_sorted)


def _grid_pos_embed(n_tokens):
    rows = n_tokens // GRID_W
    row = np.repeat(np.arange(rows, dtype=np.float64), GRID_W)
    col = np.tile(np.arange(GRID_W, dtype=np.float64), rows)
    quarter = D_MODEL // 4
    omega = 1.0 / (POS_BASE ** (np.arange(quarter, dtype=np.float64) / quarter))
    er = row[:, None] * omega
    ec = col[:, None] * omega
    return jnp.asarray(np.concatenate([np.sin(er), np.cos(er), np.sin(ec), np.cos(ec)], axis=-1), F32)


def _mixers(x, pos, mod3, h0_re, h0_im, tabs, filt, s5ops, wts, tm):
    bsz, n_tok, _ = x.shape
    shared = mod3.shape[0] == 1
    x3 = x.reshape(1, bsz * n_tok, D_MODEL) if shared else x
    proj_hy, u_s5 = _in_proj(x3, pos, mod3, wts['w_in'], tm)
    y_hy = _hyena(proj_hy.reshape(bsz, n_tok, 3 * D_HY), tabs, filt,
                  wts['hy_conv_w'], wts['hy_conv_b'], wts['hy_fbias'])
    y_s5, f_re, f_im = _s5(u_s5.reshape(bsz, n_tok, D_S5), s5ops, h0_re, h0_im)
    return y_hy.reshape(bsz * n_tok, D_HY), y_s5.reshape(bsz * n_tok, D_S5), f_re, f_im


def kernel(x_prompt, x_sample, state_s5_re, state_s5_im, c, c_ctx, w_ada, b_ada, w_in, hy_conv_w, hy_conv_b, hy_f_w1, hy_f_b1, hy_f_w2, hy_f_b2, hy_f_w3, hy_freq, hy_fbias, s5_a_re, s5_a_im, s5_log_dt, s5_b_re, s5_b_im, s5_c_re, s5_c_im, s5_d, s5_w_glu, s5_b_glu, out_norm_g, w_out, ln1_g, ln1_b, moe_w_r1, moe_b_r1, moe_w_r2, moe_b_r2, moe_w_gate, moe_w_up, moe_w_down, ln2_g, ln2_b):
    b_ctx, l_ctx, _ = x_prompt.shape
    b_lat, l_lat, _ = x_sample.shape
    g, p = S5_GROUPS, S5_STATE
    assert w_ada.shape[0] == 1, "single-layer trunk"
    l = 0

    nrow = 16
    cond = jnp.concatenate([c_ctx[None, :], c, jnp.zeros((nrow - 1 - b_lat, D_MODEL), F32)], axis=0)
    mod = _ada(cond, w_ada[l], b_ada[l])
    mod_ctx = mod[0:1].reshape(1, 1, 6 * D_MODEL)
    mod_lat = mod[1:1 + b_lat].reshape(b_lat, 1, 6 * D_MODEL)

    wr = jnp.concatenate([moe_w_r2[l].transpose(1, 0, 2).reshape(D_MODEL, N_EXPERTS), moe_w_r1[l]], axis=1)
    wr = jnp.pad(wr, ((0, 0), (0, LANES - wr.shape[1])))
    br = jnp.concatenate([moe_b_r2[l].reshape(-1), moe_b_r1[l]])
    br = jnp.pad(br, (0, LANES - br.shape[0])).reshape(1, LANES)
    wr_hi, wr_lo = _split(wr)

    wts = {
        'w_in': w_in[l], 'hy_conv_w': hy_conv_w[l], 'hy_conv_b': hy_conv_b[l],
        'hy_fbias': hy_fbias[l], 'w_glu': s5_w_glu[l].astype(BF16), 'b_glu': s5_b_glu[l].reshape(1, -1),
        'out_norm_g': out_norm_g[l].reshape(1, -1), 'w_out': w_out[l].astype(BF16),
        'ln1_g': ln1_g[l].reshape(1, -1), 'ln1_b': ln1_b[l].reshape(1, -1),
        'wr_hi': wr_hi, 'wr_lo': wr_lo, 'br': br,
        'w_gate': moe_w_gate[l], 'w_up': moe_w_up[l], 'w_down': moe_w_down[l],
        'ln2_g': ln2_g[l].reshape(1, -1), 'ln2_b': ln2_b[l].reshape(1, -1),
    }

    s5ops = _s5_operators(s5_a_re[l], s5_a_im[l], s5_log_dt[l], s5_b_re[l], s5_b_im[l],
                          s5_c_re[l], s5_c_im[l], s5_d[l])
    tabs_ctx = _tables(l_ctx)
    tabs_lat = _tables(l_lat)
    filt_args = (hy_f_w1[l], hy_f_b1[l], hy_f_w2[l], hy_f_b2[l], hy_f_w3[l], hy_freq[l])
    filt_ctx = _hyena_filters(l_ctx, tabs_ctx, *filt_args)
    filt_lat = _hyena_filters(l_lat, tabs_lat, *filt_args)

    zero = jnp.zeros((g, b_ctx, 2 * p), F32)
    yhy_c, ys5_c, f_re, f_im = _mixers(x_prompt, None, mod_ctx, zero, zero, tabs_ctx, filt_ctx, s5ops, wts, 1024)
    unpack = lambda f: f.reshape(g, b_ctx, 2, p).transpose(1, 2, 0, 3)[:, None]
    new_re, new_im = unpack(f_re), unpack(f_im)

    pack = lambda s: s[:, l].transpose(2, 0, 1, 3).reshape(g, b_lat, 2 * p)
    pos = _grid_pos_embed(l_lat)
    yhy_l, ys5_l, _, _ = _mixers(x_sample, pos, mod_lat, pack(state_s5_re), pack(state_s5_im),
                                 tabs_lat, filt_lat, s5ops, wts, 1024)

    n_ctx = b_ctx * l_ctx
    x1_all, h2_all, gates_all, tile_counts = _out_proj(
        x_prompt.reshape(n_ctx, D_MODEL), x_sample.reshape(b_lat * l_lat, D_MODEL), pos,
        yhy_c, yhy_l, ys5_c, ys5_l, mod, wts['w_glu'], wts['b_glu'], wts['out_norm_g'], wts['w_out'],
        wts['ln1_g'], wts['ln1_b'], wts['wr_hi'], wts['wr_lo'], wts['br'], MOE_ST)
    y_ctx, y_lat = _moe(h2_all, gates_all, tile_counts, x1_all, mod,
                        wts['w_gate'], wts['w_up'], wts['w_down'], wts['ln2_g'], wts['ln2_b'],
                        n_ctx, l_lat)
    return (y_ctx.reshape(x_prompt.shape), y_lat.reshape(x_sample.shape), new_re, new_im)
```

```python
import functools
import math

import numpy as np
import jax
import jax.numpy as jnp
from jax import lax
from jax.experimental import pallas as pl
from jax.experimental.pallas import tpu as pltpu

F32 = jnp.float32
BF16 = jnp.bfloat16

D_MODEL = 1024
DEPTH = 1
GRID_W = 64
POS_BASE = 10000.0
D_HY = 512
D_S5 = 512
S5_CH = 16
S5_GROUPS = 32
S5_STATE = 64
S5_CHUNK = 16
S5_ROW = S5_CHUNK * S5_CH
HY_BANDS = 16
HY_EMB = 1 + 2 * HY_BANDS
HY_HID = 64
HY_MIN_DECAY = math.log(1e-2) / 1.5
HY_MAX_DECAY = math.log(1e-2) / 0.3
N_EGROUPS = 4
N_EPG = 4
N_EXPERTS = 16
D_EXPERT = 512
LN_EPS = 1e-5
ALPHA = (2.0 * DEPTH) ** 0.25
LANES = 128
S5_GB = LANES // S5_CH
S5OPS_GB = 4
HY_CW = 512
MOE_ST = 256
MOE_SLOTS = 384
OUT_TM = 512
MOE_UNIT = 16
MOE_BIG = 64
MOE_TM = 512
VMEM_LIMIT = 60000 * 1024


def _cparams(sem):
    return pltpu.CompilerParams(dimension_semantics=sem, vmem_limit_bytes=VMEM_LIMIT)


def _split(x):
    hi = x.astype(BF16)
    lo = (x - hi.astype(F32)).astype(BF16)
    return hi, lo


def _dot(a, b):
    return jnp.dot(a, b, preferred_element_type=F32)


def _dot_t(a, b):
    return lax.dot_general(a, b, (((1,), (1,)), ((), ())), preferred_element_type=F32)


def _mm3(a, b):
    ah, al = _split(a)
    bh, bl = _split(b)
    return _dot(ah, bh) + _dot(al, bh) + _dot(ah, bl)


def _mm3_t(a, b):
    ah, al = _split(a)
    bh, bl = _split(b)
    return _dot_t(ah, bh) + _dot_t(al, bh) + _dot_t(ah, bl)


def _norm(x):
    xc = x - jnp.mean(x, axis=-1, keepdims=True)
    return xc * lax.rsqrt(jnp.mean(xc * xc, axis=-1, keepdims=True) + LN_EPS)


def _rms(y):
    return y * lax.rsqrt(jnp.mean(y * y, axis=-1, keepdims=True) + LN_EPS)


def _ada_kernel(cond_ref, w_ref, b_ref, o_ref):
    c = jax.nn.silu(cond_ref[...])
    o_ref[...] = _mm3(c, w_ref[...]) + b_ref[...]


def _ada(cond, w_ada, b_ada):
    nb = cond.shape[0]
    n = w_ada.shape[1]
    tn = 1024
    return pl.pallas_call(
        _ada_kernel,
        grid=(n // tn,),
        in_specs=[pl.BlockSpec((nb, D_MODEL), lambda j: (0, 0)),
                  pl.BlockSpec((D_MODEL, tn), lambda j: (0, j)),
                  pl.BlockSpec((1, tn), lambda j: (0, j))],
        out_specs=pl.BlockSpec((nb, tn), lambda j: (0, j)),
        out_shape=jax.ShapeDtypeStruct((nb, n), F32),
        compiler_params=_cparams(("arbitrary",)),
        name="ada",
    )(cond, w_ada, b_ada.reshape(1, n))


RADIX2_MIN_HALF = 256


def _dft_tables(n_half):
    n = 2 * n_half
    idx = np.arange(n_half, dtype=np.int64)
    m = (idx[:, None] * idx[None, :]) % n
    ang = 2.0 * np.pi * m.astype(np.float64) / n
    cm = np.cos(ang)
    sm = -np.sin(ang)
    sm[0, :] = 1.0 - 2.0 * (idx % 2)
    return cm, sm


def _dense_tables(n_tok):
    n_half = n_tok // 2
    f = np.concatenate([np.arange(n_half), n_tok - np.arange(n_half)]).astype(np.int64)
    sidx = np.arange(n_tok, dtype=np.int64)
    ang = 2.0 * np.pi * ((f[:, None] * sidx[None, :]) % (2 * n_tok)).astype(np.float64) / (2 * n_tok)
    cd = np.cos(ang)
    sd = -np.sin(ang)
    half = 2.0 * np.pi * ((n_half * sidx) % (2 * n_tok)).astype(np.float64) / (2 * n_tok)
    sd[0, :] = np.cos(half)
    sd[n_half, :] = -np.sin(half)
    return cd, sd


def _tables(n_tok):
    n_half = n_tok // 2
    bf = lambda t: jnp.asarray(t.astype(np.float32)).astype(BF16)
    c, s = _dense_tables(n_tok) if n_half < RADIX2_MIN_HALF else _dft_tables(n_half)
    stacks = (bf(np.concatenate([c, s], axis=0)), bf(np.concatenate([c.T, s.T], axis=1)))
    if n_half < RADIX2_MIN_HALF:
        return stacks
    ang = np.pi * np.arange(n_half, dtype=np.float64) / n_tok
    tw = [jnp.asarray(np.broadcast_to(v[:, None], (n_half, HY_CW)).astype(np.float32))
          for v in (np.cos(ang), -np.sin(ang))]
    return stacks + (tw[0], tw[1])


def _put_cols(ref, x):
    for j in range(ref.shape[0]):
        ref[j] = x[:, LANES * j:LANES * (j + 1)]


def _get_cols(ref):
    return jnp.concatenate([ref[j] for j in range(ref.shape[0])], axis=1)


def _get_parity(ref, parity):
    n_half = ref.shape[1] // 2
    return jnp.concatenate([ref[j, pl.ds(parity, n_half, stride=2), :] for j in range(ref.shape[0])], axis=1)


def _put_parity(ref, parity, x):
    n_half = ref.shape[1] // 2
    for j in range(ref.shape[0]):
        ref[j, pl.ds(parity, n_half, stride=2), :] = x[:, LANES * j:LANES * (j + 1)]


def _set_row0(x, v):
    first = lax.broadcasted_iota(jnp.int32, (8, x.shape[1]), 0) == 0
    return jnp.concatenate([jnp.where(first, v, x[:8]), x[8:]], axis=0)


def _rfft_packed(x_ref, tabs):
    n_tok = x_ref.shape[1]
    n_half = n_tok // 2
    if len(tabs) == 2:
        r = _dot(tabs[0], _get_cols(x_ref).astype(BF16))
        return r[:n_half], r[n_tok:n_tok + n_half], r[n_half:n_tok], r[n_tok + n_half:]
    fwd, _, tw_re, tw_im = tabs
    e = _dot(fwd, _get_parity(x_ref, 0).astype(BF16))
    o = _dot(fwd, _get_parity(x_ref, 1).astype(BF16))
    e_re, e_im, o_re, o_im = e[:n_half], e[n_half:], o[:n_half], o[n_half:]
    t_re = tw_re * o_re - tw_im * o_im
    t_im = tw_re * o_im + tw_im * o_re
    a_im = _set_row0(e_im + t_im, e_im[0:1])
    b_im = _set_row0(t_im - e_im, -o_im[0:1])
    return e_re + t_re, a_im, e_re - t_re, b_im


def _irfft_packed(y_ref, ya_re, ya_im, yb_re, yb_im, tabs):
    if len(tabs) == 2:
        y = jnp.concatenate([ya_re, yb_re, ya_im, yb_im], axis=0).astype(BF16)
        _put_cols(y_ref, _dot(tabs[1], y))
        return
    _, inv, tw_re, tw_im = tabs
    p_e = ya_re + yb_re
    q_e = _set_row0(ya_im - yb_im, ya_im[0:1])
    _put_parity(y_ref, 0, _dot(inv, jnp.concatenate([p_e, q_e], axis=0).astype(BF16)))
    ra_re = ya_re * tw_re + ya_im * tw_im
    ra_im = ya_im * tw_re - ya_re * tw_im
    rb_re = yb_im * tw_im - yb_re * tw_re
    rb_im = -(yb_re * tw_im + yb_im * tw_re)
    p_o = ra_re + rb_re
    q_o = _set_row0(ra_im - rb_im, -yb_im[0:1])
    _put_parity(y_ref, 1, _dot(inv, jnp.concatenate([p_o, q_o], axis=0).astype(BF16)))


def _filt_kernel(n_tok, n_tab, z_ref, t_ref, w1_ref, b1_ref, w2_ref, b2_ref, fr_ref, w3f_ref, w3b_ref,
                 dl_ref, *rest):
    tabs = tuple(r[...] for r in rest[:n_tab])
    kar_ref, kai_ref, kbr_ref, kbi_ref, p_ref, q_ref = rest[n_tab:]
    fr = fr_ref[...]
    h = jnp.sin(fr * (_mm3(z_ref[...], w1_ref[...]) + b1_ref[...]))
    h = jnp.sin(fr * (_mm3(h, w2_ref[...]) + b2_ref[...]))
    decay = jnp.exp(-t_ref[...] * dl_ref[...])
    row = lax.broadcasted_iota(jnp.int32, decay.shape, 0)
    hf = _mm3(h, w3f_ref[...]) * decay
    hb = jnp.where(row == 0, 0.0, _mm3(h, w3b_ref[...]) * decay)
    _put_cols(p_ref, hf + hb)
    _put_cols(q_ref, hf - hb)
    pa_re, pa_im, pb_re, _ = _rfft_packed(p_ref, tabs)
    _, qa_im, _, qb_im = _rfft_packed(q_ref, tabs)
    row0 = lax.broadcasted_iota(jnp.int32, pa_re.shape, 0) == 0
    inv_n = 1.0 / (2 * n_tok)
    w_re = jnp.where(row0, inv_n, 2.0 * inv_n)
    kar_ref[...] = w_re * pa_re
    kbr_ref[...] = w_re * pb_re
    kai_ref[...] = (2.0 * inv_n) * _set_row0(qa_im, pa_im[0:1])
    kbi_ref[...] = (2.0 * inv_n) * qb_im


def _hyena_filters(n_tok, tabs, hy_f_w1, hy_f_b1, hy_f_w2, hy_f_b2, hy_f_w3, hy_freq):
    n_half = n_tok // 2
    tt = np.linspace(0.0, 1.0, n_tok)[:, None]
    ang = (2.0 * np.pi * np.arange(n_tok) / n_tok)[:, None] * np.linspace(1e-4, HY_BANDS - 1, HY_BANDS)[None, :]
    z = np.concatenate([tt, np.cos(ang), -np.sin(ang), np.zeros((n_tok, LANES - HY_EMB))], axis=-1)
    z, t = jnp.asarray(z, F32), jnp.asarray(tt, F32)
    deltas = jnp.asarray(np.abs(np.linspace(HY_MIN_DECAY, HY_MAX_DECAY, D_HY))[None, :], F32)
    w1 = jnp.pad(hy_f_w1, ((0, LANES - HY_EMB), (0, 0)))
    ncb = D_HY // HY_CW
    full = lambda j: (0, 0)
    out_sd = jax.ShapeDtypeStruct((n_half, 2 * D_HY), F32)
    tab_specs = [pl.BlockSpec(t_.shape, full, pipeline_mode=pl.Buffered(1)) for t_ in tabs]
    return pl.pallas_call(
        functools.partial(_filt_kernel, n_tok, len(tabs)),
        grid=(2 * ncb,),
        in_specs=[pl.BlockSpec((n_tok, LANES), full),
                  pl.BlockSpec((n_tok, 1), full),
                  pl.BlockSpec((LANES, HY_HID), full),
                  pl.BlockSpec((1, HY_HID), full),
                  pl.BlockSpec((HY_HID, HY_HID), full),
                  pl.BlockSpec((1, HY_HID), full),
                  pl.BlockSpec((1, HY_HID), full),
                  pl.BlockSpec((HY_HID, HY_CW), lambda j: (0, 2 * ncb * (j // ncb) + j % ncb)),
                  pl.BlockSpec((HY_HID, HY_CW), lambda j: (0, 2 * ncb * (j // ncb) + ncb + j % ncb)),
                  pl.BlockSpec((1, HY_CW), lambda j: (0, j % ncb))] + tab_specs,
        out_specs=[pl.BlockSpec((n_half, HY_CW), lambda j: (0, j))] * 4,
        out_shape=[out_sd] * 4,
        scratch_shapes=[pltpu.VMEM((HY_CW // LANES, n_tok, LANES), F32)] * 2,
        compiler_params=_cparams(("arbitrary",)),
        name=f"filt{n_tok}",
    )(z, t, w1, hy_f_b1.reshape(1, -1), hy_f_w2, hy_f_b2.reshape(1, -1), hy_freq.reshape(1, -1),
      hy_f_w3, hy_f_w3, deltas, *tabs)


def _hyena_kernel(n_tab, pv_ref, p1_ref, p2_ref, cwv_ref, cw1_ref, cw2_ref, cbv_ref, cb1_ref, cb2_ref,
                  fbias_ref, *rest):
    tabs = tuple(r[...] for r in rest[:n_tab])
    (kar0_ref, kai0_ref, kbr0_ref, kbi0_ref, kar1_ref, kai1_ref, kbr1_ref, kbi1_ref,
     o_ref, u_ref, y_ref) = rest[n_tab:]
    n_tok = pv_ref.shape[1]
    row = lax.broadcasted_iota(jnp.int32, (n_tok, pv_ref.shape[2]), 0)

    def short_conv(p_ref, cw_ref, cb_ref):
        p = p_ref[0]
        prev = jnp.where(row == 0, 0.0, pltpu.roll(p, 1, axis=0))
        nxt = jnp.where(row == n_tok - 1, 0.0, pltpu.roll(p, n_tok - 1, axis=0))
        return cb_ref[...] + prev * cw_ref[0:1, :] + p * cw_ref[1:2, :] + nxt * cw_ref[2:3, :]

    def fftconv(u, kar_ref, kai_ref, kbr_ref, kbi_ref, skip):
        _put_cols(u_ref, u)
        ua_re, ua_im, ub_re, ub_im = _rfft_packed(u_ref, tabs)
        ka_re, ka_im, kb_re, kb_im = kar_ref[...], kai_ref[...], kbr_ref[...], kbi_ref[...]
        zero_row = jnp.zeros_like(ka_im[0:1])
        kaz = _set_row0(ka_im, zero_row)
        kbz = _set_row0(kb_im, zero_row)
        ya_re = ua_re * ka_re - ua_im * kaz
        yb_re = ub_re * kb_re - ub_im * kbz
        h_re = ua_im[0:1] * ka_im[0:1] - ub_im[0:1] * kb_im[0:1]
        h_im = ua_im[0:1] * kb_im[0:1] + ub_im[0:1] * ka_im[0:1]
        ya_im = _set_row0(ua_re * ka_im + ua_im * ka_re, h_re)
        yb_im = _set_row0(ub_re * kb_im + ub_im * kb_re, h_im)
        _irfft_packed(y_ref, ya_re, ya_im, yb_re, yb_im, tabs)
        return _get_cols(y_ref) + u * skip

    v = short_conv(pv_ref, cwv_ref, cbv_ref)
    x1 = short_conv(p1_ref, cw1_ref, cb1_ref)
    z = x1 * fftconv(v, kar0_ref, kai0_ref, kbr0_ref, kbi0_ref, fbias_ref[0:1, :])
    x2 = short_conv(p2_ref, cw2_ref, cb2_ref)
    o_ref[0] = x2 * fftconv(z, kar1_ref, kai1_ref, kbr1_ref, kbi1_ref, fbias_ref[1:2, :])


def _hyena(proj_hy, tabs, filt, hy_conv_w, hy_conv_b, hy_fbias):
    bsz, n_tok, _ = proj_hy.shape
    n_half = n_tok // 2
    ncb = D_HY // HY_CW
    cb = hy_conv_b.reshape(1, -1)
    tab_specs = [pl.BlockSpec(t.shape, lambda b, c: (0, 0), pipeline_mode=pl.Buffered(1)) for t in tabs]

    def pspec(k):
        return pl.BlockSpec((1, n_tok, HY_CW), lambda b, c: (b, 0, k * ncb + c))

    def cwspec(k):
        return pl.BlockSpec((3, HY_CW), lambda b, c: (0, k * ncb + c))

    def cbspec(k):
        return pl.BlockSpec((1, HY_CW), lambda b, c: (0, k * ncb + c))

    def fspec(o):
        mode = pl.Buffered(1) if ncb == 1 else None
        return pl.BlockSpec((n_half, HY_CW), lambda b, c: (0, o * ncb + c), pipeline_mode=mode)

    return pl.pallas_call(
        functools.partial(_hyena_kernel, len(tabs)),
        grid=(bsz, ncb),
        in_specs=[pspec(0), pspec(1), pspec(2), cwspec(0), cwspec(1), cwspec(2),
                  cbspec(0), cbspec(1), cbspec(2),
                  pl.BlockSpec((2, HY_CW), lambda b, c: (0, c))] + tab_specs + [fspec(0)] * 4 + [fspec(1)] * 4,
        out_specs=pl.BlockSpec((1, n_tok, HY_CW), lambda b, c: (b, 0, c)),
        out_shape=jax.ShapeDtypeStruct((bsz, n_tok, D_HY), F32),
        scratch_shapes=[pltpu.VMEM((HY_CW // LANES, n_tok, LANES), F32)] * 2,
        compiler_params=_cparams(("arbitrary", "arbitrary")),
        name=f"hyena{n_tok}",
    )(proj_hy, proj_hy, proj_hy, hy_conv_w, hy_conv_w, hy_conv_w, cb, cb, cb, hy_fbias,
      *tabs, *filt, *filt)


def _s5ops_kernel(*refs):
    for g in range(S5OPS_GB):
        _s5ops_group(g, *refs)


def _s5ops_group(g, are_ref, aim_ref, ldt_ref, btr_ref, bti_ref, cre_ref, cim_ref, d_ref,
                 opa_ref, opb_ref, g_ref, atr_ref, ati_ref, er_ref, ei_ref):
    a_re, a_im = are_ref[g], aim_ref[g]
    dt = jnp.exp(ldt_ref[g])
    mag = jnp.exp(a_re * dt)
    ab_re = mag * jnp.cos(a_im * dt)
    ab_im = mag * jnp.sin(a_im * dt)
    n_re, n_im = ab_re - 1.0, ab_im
    den = a_re * a_re + a_im * a_im
    q_re = (n_re * a_re + n_im * a_im) / den
    q_im = (n_im * a_re - n_re * a_im) / den
    bt_re, bt_im = btr_ref[g], bti_ref[g]
    bb_re = q_re * bt_re - q_im * bt_im
    bb_im = q_re * bt_im + q_im * bt_re
    c_re, c_im = cre_ref[g, 0:S5_CH, :], cim_ref[g, 0:S5_CH, :]
    pw = [(jnp.ones_like(ab_re), jnp.zeros_like(ab_re))]
    for _ in range(S5_CHUNK):
        pr, pi = pw[-1]
        pw.append((pr * ab_re - pi * ab_im, pr * ab_im + pi * ab_re))
    lane = lax.broadcasted_iota(jnp.int32, ab_re.shape, 1)
    fwd = lane < S5_STATE
    for s in range(S5_CHUNK):
        e_re = jnp.where(fwd, pw[S5_CHUNK - 1 - s][0], pw[s][0])
        e_im = jnp.where(fwd, pw[S5_CHUNK - 1 - s][1], pw[s][1])
        er_ref[g, pl.ds(S5_CH * s, S5_CH), :] = e_re * bb_re - e_im * bb_im
        ei_ref[g, pl.ds(S5_CH * s, S5_CH), :] = e_re * bb_im + e_im * bb_re
        g_re = jnp.where(fwd, pw[s + 1][0], pw[S5_CHUNK - s][0])
        g_im = jnp.where(fwd, pw[s + 1][1], pw[S5_CHUNK - s][1])
        g_ref[g, pl.ds(S5_CH * s, S5_CH), 0:LANES] = (c_re * g_re - c_im * g_im).astype(BF16)
        g_ref[g, pl.ds(S5_CH * s, S5_CH), LANES:2 * LANES] = (-(c_re * g_im + c_im * g_re)).astype(BF16)
    atr_ref[g] = pw[S5_CHUNK][0]
    ati_ref[g] = pw[S5_CHUNK][1]
    er, ei = er_ref[g], ei_ref[g]
    opa_ref[g, :, 2 * LANES:3 * LANES], opb_ref[g, :, 0:LANES] = _split(er)
    opa_ref[g, :, 3 * LANES:4 * LANES], opb_ref[g, :, LANES:2 * LANES] = _split(ei)
    lane2 = lax.broadcasted_iota(jnp.int32, er.shape, 1)
    row2 = lax.broadcasted_iota(jnp.int32, er.shape, 0)
    f2 = lane2 < S5_STATE
    zero = jnp.zeros_like(er)

    cp_re, cp_im = cre_ref[g], cim_ref[g]
    kf = _mm3_t(jnp.where(f2, er, zero), cp_re) - _mm3_t(jnp.where(f2, ei, zero), cp_im)
    kb = _mm3_t(jnp.where(f2, zero, er), cp_re) - _mm3_t(jnp.where(f2, zero, ei), cp_im)
    d_row = d_ref[g]
    steps_per_vreg = LANES // S5_CH
    for half in range(S5_CHUNK // steps_per_vreg):
        acc = zero
        for tt in range(steps_per_vreg):
            t = half * steps_per_vreg + tt
            nf = S5_CH * (S5_CHUNK - 1 - t)
            nb = S5_CH * t
            col_f = jnp.concatenate([kf[nf:], zero[:nf]], axis=0) if nf else kf
            col_b = jnp.concatenate([zero[:nb], kb[:S5_ROW - nb]], axis=0) if nb else kb
            diag = jnp.where((row2 // S5_CH == t) & (row2 % S5_CH == lane2), d_row, 0.0)
            col = col_f + col_b + diag
            r = pltpu.roll(col, S5_CH * tt, axis=1) if tt else col
            acc = jnp.where((lane2 >= S5_CH * tt) & (lane2 < S5_CH * (tt + 1)), r, acc)
        opa_ref[g, :, LANES * half:LANES * (half + 1)] = acc.astype(BF16)


def _s5_operators(s5_a_re, s5_a_im, s5_log_dt, s5_b_re, s5_b_im, s5_c_re, s5_c_im, s5_d):
    g, p, h = S5_GROUPS, S5_STATE, S5_CH
    cat = lambda x: jnp.concatenate([x[0], x[1]], axis=-1)
    a_re = cat(s5_a_re).reshape(g, 1, 2 * p)
    a_im = cat(s5_a_im).reshape(g, 1, 2 * p)
    ldt = cat(jnp.broadcast_to(s5_log_dt[:, :, None], (2, g, p))).reshape(g, 1, 2 * p)
    bt_re = cat(jnp.swapaxes(s5_b_re, -1, -2))
    bt_im = cat(jnp.swapaxes(s5_b_im, -1, -2))
    cpad = lambda c: jnp.pad(jnp.concatenate([c, c], axis=-1), ((0, 0), (0, LANES - h), (0, 0)))
    c_re, c_im = cpad(s5_c_re), cpad(s5_c_im)
    d_row = jnp.pad(s5_d.reshape(g, 1, h), ((0, 0), (0, 0), (0, LANES - h)))
    vec = pl.BlockSpec((S5OPS_GB, 1, 2 * p), lambda i: (i, 0, 0))
    hp = pl.BlockSpec((S5OPS_GB, h, 2 * p), lambda i: (i, 0, 0))
    sq = pl.BlockSpec((S5OPS_GB, LANES, 2 * p), lambda i: (i, 0, 0))
    wide = lambda n: pl.BlockSpec((S5OPS_GB, S5_ROW, n), lambda i: (i, 0, 0))
    vec_sd = jax.ShapeDtypeStruct((g, 1, 2 * p), F32)
    return pl.pallas_call(
        _s5ops_kernel,
        grid=(g // S5OPS_GB,),
        in_specs=[vec, vec, vec, hp, hp, sq, sq, vec],
        out_specs=[wide(2 * S5_ROW), wide(S5_ROW), wide(S5_ROW), vec, vec],
        out_shape=[jax.ShapeDtypeStruct((g, S5_ROW, 2 * S5_ROW), BF16),
                   jax.ShapeDtypeStruct((g, S5_ROW, S5_ROW), BF16),
                   jax.ShapeDtypeStruct((g, S5_ROW, S5_ROW), BF16), vec_sd, vec_sd],
        scratch_shapes=[pltpu.VMEM((S5OPS_GB, S5_ROW, 2 * p), F32)] * 2,
        compiler_params=_cparams(("arbitrary",)),
        name="s5ops",
    )(a_re, a_im, ldt, bt_re, bt_im, c_re, c_im, d_row)


def _block_transpose(xs):
    n = len(xs)
    lane = lax.broadcasted_iota(jnp.int32, xs[0].shape, 1)
    xs = list(xs)
    d = n // 2
    while d:
        keep = ((lane // S5_CH) & d) == 0
        for i in range(n):
            if i & d:
                continue
            lo, hi = xs[i], xs[i + d]
            xs[i] = jnp.where(keep, lo, pltpu.roll(hi, S5_CH * d, axis=1))
            xs[i + d] = jnp.where(keep, pltpu.roll(lo, LANES - S5_CH * d, axis=1), hi)
        d //= 2
    return xs


def _s5_kernel(bsz, n_chunks, u_ref, opa_ref, opb_ref, g_ref,
               atr_ref, ati_ref, h0r_ref, h0i_ref, y_ref, fr_ref, fi_ref,
               ua_ref, ub_ref, ya_ref, yb_ref, sr_ref, si_ref, xfr_ref, xfi_ref, xbr_ref, xbi_ref):
    nc = n_chunks
    spv = LANES // S5_CH
    rsub = min(nc, 32)

    def to_chunks(b, carry):
        for half, dst in ((0, ua_ref), (1, ub_ref)):
            for r0 in range(0, nc, rsub):
                xs = [u_ref[b, pl.ds(S5_CHUNK * r0 + half * spv + tt, rsub, stride=S5_CHUNK), :]
                      for tt in range(spv)]
                for k, blk in enumerate(_block_transpose(xs)):
                    dst[k, pl.ds(r0 * bsz + b, rsub, stride=bsz), :] = blk
        return carry

    lax.fori_loop(0, bsz, to_chunks, 0, unroll=2)

    lane = lax.broadcasted_iota(jnp.int32, (bsz, 2 * S5_STATE), 1)
    fwd = lane < S5_STATE
    lane_all = lax.broadcasted_iota(jnp.int32, (bsz * nc, 2 * S5_STATE), 1)
    fwd_all = lane_all < S5_STATE

    def group(k, slot):
        u = jnp.concatenate([ua_ref[k], ub_ref[k]], axis=1)
        uh, ul = _split(u)
        wide = _dot(uh, opa_ref[k])
        inj = (wide[:, S5_ROW:] + _dot(ul, opa_ref[k, :, S5_ROW:2 * S5_ROW]) + _dot(uh, opb_ref[k]))
        sr_ref[slot] = inj[:, :LANES]
        si_ref[slot] = inj[:, LANES:]
        at_re, at_im = atr_ref[k], ati_ref[k]
        y_intra = wide[:, :S5_ROW]

        def step(i, xc):
            x_re, x_im = xc
            rf = pl.ds(pl.multiple_of(i * bsz, bsz), bsz)
            rb = pl.ds(pl.multiple_of((nc - 1 - i) * bsz, bsz), bsz)
            xfr_ref[slot, rf, :] = x_re
            xfi_ref[slot, rf, :] = x_im
            xbr_ref[slot, rb, :] = x_re
            xbi_ref[slot, rb, :] = x_im
            s_re = jnp.where(fwd, sr_ref[slot, rf, :], sr_ref[slot, rb, :])
            s_im = jnp.where(fwd, si_ref[slot, rf, :], si_ref[slot, rb, :])
            return (at_re * x_re - at_im * x_im + s_re, at_re * x_im + at_im * x_re + s_im)

        x_re, x_im = lax.fori_loop(0, nc, step, (h0r_ref[k], h0i_ref[k]), unroll=True)
        fr_ref[k] = x_re
        fi_ref[k] = x_im
        xp = jnp.concatenate([jnp.where(fwd_all, xfr_ref[slot], xbr_ref[slot]),
                              jnp.where(fwd_all, xfi_ref[slot], xbi_ref[slot])], axis=1).astype(BF16)
        y = y_intra + _dot_t(xp, g_ref[k])
        ya_ref[k] = y[:, :LANES]
        yb_ref[k] = y[:, LANES:]

    def group_pair(j, carry):
        group(2 * j, 0)
        group(2 * j + 1, 1)
        return carry

    lax.fori_loop(0, S5_GB // 2, group_pair, 0)

    def to_tokens(b, carry):
        for half, src in ((0, ya_ref), (1, yb_ref)):
            for r0 in range(0, nc, rsub):
                ys = [src[k, pl.ds(r0 * bsz + b, rsub, stride=bsz), :] for k in range(S5_GB)]
                for tt, blk in enumerate(_block_transpose(ys)):
                    y_ref[b, pl.ds(S5_CHUNK * r0 + half * spv + tt, rsub, stride=S5_CHUNK), :] = blk
        return carry

    lax.fori_loop(0, bsz, to_tokens, 0, unroll=2)


def _s5(u, ops, h0_re, h0_im):
    bsz, n_tok, _ = u.shape
    g, p = S5_GROUPS, S5_STATE
    nc = n_tok // S5_CHUNK
    rows = nc * bsz
    tok = pl.BlockSpec((bsz, n_tok, LANES), lambda j: (0, 0, j))
    gspec = lambda shape: pl.BlockSpec((S5_GB,) + shape, lambda j: (j, 0, 0))
    return pl.pallas_call(
        functools.partial(_s5_kernel, bsz, nc),
        grid=(g // S5_GB,),
        in_specs=[tok, gspec((S5_ROW, 2 * S5_ROW)), gspec((S5_ROW, S5_ROW)), gspec((S5_ROW, S5_ROW)),
                  gspec((1, 2 * p)), gspec((1, 2 * p)), gspec((bsz, 2 * p)), gspec((bsz, 2 * p))],
        out_specs=[tok, gspec((bsz, 2 * p)), gspec((bsz, 2 * p))],
        out_shape=[jax.ShapeDtypeStruct((bsz, n_tok, D_S5), F32),
                   jax.ShapeDtypeStruct((g, bsz, 2 * p), F32),
                   jax.ShapeDtypeStruct((g, bsz, 2 * p), F32)],
        scratch_shapes=([pltpu.VMEM((S5_GB, rows, LANES), F32)] * 4
                        + [pltpu.VMEM((2, rows, 2 * p), F32)] * 6),
        compiler_params=_cparams(("arbitrary",)),
        name=f"s5_{n_tok}",
    )(u, *ops, h0_re, h0_im)


def _in_kernel(has_pos, *refs):
    if has_pos:
        x_ref, pos_ref, mod_ref, w_ref, hy_ref, s5_ref = refs
        x = x_ref[0] + pos_ref[...]
    else:
        x_ref, mod_ref, w_ref, hy_ref, s5_ref = refs
        x = x_ref[0]
    sh1 = mod_ref[0, :, 0:D_MODEL]
    sc1 = mod_ref[0, :, D_MODEL:2 * D_MODEL]
    h = _norm(x) * (1.0 + sc1) + sh1
    proj = _dot(h.astype(BF16), w_ref[...].astype(BF16))
    hy_ref[0] = proj[:, :3 * D_HY]
    s5_ref[0] = proj[:, 3 * D_HY:]


def _in_proj(x3, pos, mod3, w_in, tm):
    nb, lt, _ = x3.shape
    has_pos = pos is not None
    per_batch = mod3.shape[0] > 1
    midx = (lambda b, i: (b, 0, 0)) if per_batch else (lambda b, i: (0, 0, 0))
    in_specs = [pl.BlockSpec((1, tm, D_MODEL), lambda b, i: (b, i, 0))]
    args = [x3]
    if has_pos:
        in_specs.append(pl.BlockSpec((tm, D_MODEL), lambda b, i: (i, 0)))
        args.append(pos)
    in_specs += [pl.BlockSpec((1, 1, 6 * D_MODEL), midx),
                 pl.BlockSpec((D_MODEL, 3 * D_HY + D_S5), lambda b, i: (0, 0), pipeline_mode=pl.Buffered(1))]
    args += [mod3, w_in]
    return pl.pallas_call(
        functools.partial(_in_kernel, has_pos),
        grid=(nb, lt // tm),
        in_specs=in_specs,
        out_specs=[pl.BlockSpec((1, tm, 3 * D_HY), lambda b, i: (b, i, 0)),
                   pl.BlockSpec((1, tm, D_S5), lambda b, i: (b, i, 0))],
        out_shape=[jax.ShapeDtypeStruct((nb, lt, 3 * D_HY), F32),
                   jax.ShapeDtypeStruct((nb, lt, D_S5), F32)],
        compiler_params=_cparams(("arbitrary", "arbitrary")),
        name=f"in_proj{nb}",
    )(*args)


def _route(logits):
    lane = lax.broadcasted_iota(jnp.int32, logits.shape, 1)
    lane_f = lane.astype(F32)
    neg = -jnp.inf
    big = float(LANES)
    m1 = (lane >= N_EXPERTS) & (lane < N_EXPERTS + N_EGROUPS)
    l1 = jnp.where(m1, logits, neg)
    top1 = jnp.max(l1, axis=-1, keepdims=True)
    grp = jnp.min(jnp.where(l1 == top1, lane_f, big), axis=-1, keepdims=True) - float(N_EXPERTS)
    den = jnp.sum(jnp.where(m1, jnp.exp(logits - top1), 0.0), axis=-1, keepdims=True)
    p_grp = 1.0 / den
    lo = grp * float(N_EPG)
    m2 = (lane_f >= lo) & (lane_f < lo + float(N_EPG))
    l2 = jnp.where(m2, logits, neg)
    v1 = jnp.max(l2, axis=-1, keepdims=True)
    i1 = jnp.min(jnp.where(l2 == v1, lane_f, big), axis=-1, keepdims=True)
    l2b = jnp.where(lane_f == i1, neg, l2)
    v2 = jnp.max(l2b, axis=-1, keepdims=True)
    i2 = jnp.min(jnp.where(l2b == v2, lane_f, big), axis=-1, keepdims=True)
    e = jnp.exp(v2 - v1)
    w1 = 1.0 / (1.0 + e)
    w2 = e / (1.0 + e)
    gates = jnp.where(lane_f == i1, w1 * p_grp, 0.0) + jnp.where(lane_f == i2, w2 * p_grp, 0.0)
    return jnp.where(lane_f == grp + float(N_EXPERTS), 1.0, gates)


def _out_kernel(n_ctx_blocks, xc_ref, xl_ref, pos_ref, yhyc_ref, yhyl_ref, ys5c_ref, ys5l_ref, mod_ref,
                wglu_ref, bglu_ref, ong_ref, wout_ref, ln1g_ref, ln1b_ref, wrh_ref, wrl_ref, br_ref,
                x1_ref, h2_ref, gate_ref, cnt_ref):
    is_ctx = pl.program_id(0) < n_ctx_blocks
    x = jnp.where(is_ctx, xc_ref[...], xl_ref[...] + pos_ref[...])
    y = jnp.where(is_ctx, ys5c_ref[...], ys5l_ref[...])
    y_hy = jnp.where(is_ctx, yhyc_ref[...], yhyl_ref[...])
    s5 = jax.nn.gelu(y) * jax.nn.sigmoid(_dot(y.astype(BF16), wglu_ref[...]) + bglu_ref[...])
    m_hy = _rms(y_hy) * ong_ref[:, 0:D_HY]
    m_s5 = _rms(s5) * ong_ref[:, D_HY:]
    o = (_dot(m_hy.astype(BF16), wout_ref[0:D_HY, :]) + _dot(m_s5.astype(BF16), wout_ref[D_HY:, :]))
    g1 = mod_ref[0, :, 2 * D_MODEL:3 * D_MODEL]
    sh2 = mod_ref[0, :, 3 * D_MODEL:4 * D_MODEL]
    sc2 = mod_ref[0, :, 4 * D_MODEL:5 * D_MODEL]
    x1 = _norm(ALPHA * x + g1 * o) * ln1g_ref[...] + ln1b_ref[...]
    x1_ref[...] = x1
    h2 = _norm(x1) * (1.0 + sc2) + sh2
    h2_ref[...] = h2.astype(BF16)
    hh, hl = _split(h2)
    logits = (_dot(hh, wrh_ref[...]) + _dot(hl, wrh_ref[...]) + _dot(hh, wrl_ref[...]) + br_ref[...])
    gates = _route(logits)
    gate_ref[...] = gates
    for j in range(cnt_ref.shape[1]):
        cnt_ref[0, j:j + 1, :] = jnp.sum(gates[MOE_ST * j:MOE_ST * (j + 1)], axis=0, keepdims=True)


def _out_proj(xc, xl, pos, yhy_c, yhy_l, ys5_c, ys5_l, mod, wglu_bf, bglu, ong, wout_bf, ln1g, ln1b,
              wr_hi, wr_lo, br, tm):
    n_ctx, n_lat = xc.shape[0], xl.shape[0]
    l_lat = pos.shape[0]
    ncb, nlb, npb = n_ctx // tm, n_lat // tm, l_lat // tm
    ctx = lambda w: pl.BlockSpec((tm, w), lambda i: (jnp.minimum(i, ncb - 1), 0))
    lat = lambda w: pl.BlockSpec((tm, w), lambda i: (jnp.maximum(i - ncb, 0), 0))
    full = lambda shape: pl.BlockSpec(shape, lambda i: (0,) * len(shape))
    out = lambda w: pl.BlockSpec((tm, w), lambda i: (i, 0))
    mod_idx = lambda i: (jnp.where(i < ncb, 0, 1 + jnp.maximum(i - ncb, 0) // npb), 0, 0)
    n_all = n_ctx + n_lat
    return pl.pallas_call(
        functools.partial(_out_kernel, ncb),
        grid=(ncb + nlb,),
        in_specs=[ctx(D_MODEL), lat(D_MODEL),
                  pl.BlockSpec((tm, D_MODEL), lambda i: (jnp.maximum(i - ncb, 0) % npb, 0)),
                  ctx(D_HY), lat(D_HY), ctx(D_S5), lat(D_S5),
                  pl.BlockSpec((1, 1, 6 * D_MODEL), mod_idx),
                  full((D_S5, D_S5)), full((1, D_S5)), full((1, D_MODEL)), full((D_MODEL, D_MODEL)),
                  full((1, D_MODEL)), full((1, D_MODEL)), full((D_MODEL, LANES)), full((D_MODEL, LANES)),
                  full((1, LANES))],
        out_specs=[out(D_MODEL), out(D_MODEL), out(LANES),
                   pl.BlockSpec((1, tm // MOE_ST, LANES), lambda i: (i, 0, 0))],
        out_shape=[jax.ShapeDtypeStruct((n_all, D_MODEL), F32),
                   jax.ShapeDtypeStruct((n_all, D_MODEL), BF16),
                   jax.ShapeDtypeStruct((n_all, LANES), F32),
                   jax.ShapeDtypeStruct((n_all // tm, tm // MOE_ST, LANES), F32)],
        compiler_params=_cparams(("arbitrary",)),
        name="out_proj",
    )(xc, xl, pos, yhy_c, yhy_l, ys5_c, ys5_l, mod.reshape(mod.shape[0], 1, 6 * D_MODEL),
      wglu_bf, bglu, ong, wout_bf, ln1g, ln1b, wr_hi, wr_lo, br)


def _perm_t(gates, loc_ref, s):
    n = gates.shape[0]
    lane = lax.broadcasted_iota(jnp.int32, gates.shape, 1)
    oh = jnp.where((lane >= N_EXPERTS) & (lane < N_EXPERTS + N_EGROUPS), gates, 0.0)
    r = lax.broadcasted_iota(jnp.int32, (n, n), 0)
    c = lax.broadcasted_iota(jnp.int32, (n, n), 1)
    earlier = jnp.where(c < r, 1.0, 0.0).astype(BF16)
    cum = _dot(earlier, oh.astype(BF16))
    rank = jnp.sum(cum * oh, axis=-1, keepdims=True)
    lane1 = lax.broadcasted_iota(jnp.int32, (1, LANES), 1)
    locv = jnp.zeros((1, LANES), F32)
    for grp in range(N_EGROUPS):
        locv = jnp.where(lane1 == N_EXPERTS + grp, loc_ref[N_EGROUPS * s + grp].astype(F32), locv)
    dest = rank + jnp.sum(oh * locv, axis=-1, keepdims=True)
    slot = lax.broadcasted_iota(jnp.int32, (n, MOE_SLOTS), 1).astype(F32)
    return jnp.where(slot == dest, 1.0, 0.0)


def _segment_copies(s, loc_ref, len_ref, off_ref, make):
    for grp in range(N_EGROUPS):
        loc = loc_ref[N_EGROUPS * s + grp]
        off = off_ref[N_EGROUPS * s + grp]
        length = len_ref[N_EGROUPS * s + grp]
        n_big = length // MOE_BIG
        done = n_big * MOE_BIG

        def big(i, carry):
            make(pl.multiple_of(loc + MOE_BIG * i, MOE_UNIT), pl.multiple_of(off + MOE_BIG * i, MOE_UNIT), MOE_BIG)
            return carry

        def unit(i, carry):
            make(pl.multiple_of(loc + done + MOE_UNIT * i, MOE_UNIT),
                 pl.multiple_of(off + done + MOE_UNIT * i, MOE_UNIT), MOE_UNIT)
            return carry

        lax.fori_loop(0, n_big, big, 0)
        lax.fori_loop(0, (length - done) // MOE_UNIT, unit, 0)


def _pad_copies(pad_ref, n_blocks, zx_v, zg_v, xs_hbm, gs_hbm, sem, op):
    def unit(row, rows):
        getattr(pltpu.make_async_copy(zx_v.at[pl.ds(0, rows), :], xs_hbm.at[pl.ds(row, rows), :], sem.at[0]), op)()
        getattr(pltpu.make_async_copy(zg_v.at[pl.ds(0, rows), :], gs_hbm.at[pl.ds(row, rows), :], sem.at[1]), op)()

    for grp in range(N_EGROUPS):
        start = pad_ref[grp]

        def body(i, carry):
            unit(pl.multiple_of(start + MOE_UNIT * i, MOE_UNIT), MOE_UNIT)
            return carry

        lax.fori_loop(0, pad_ref[N_EGROUPS + grp], body, 0)

    def tail(b, carry):
        unit(pl.multiple_of(b * MOE_TM, MOE_TM), MOE_TM)
        return carry

    lax.fori_loop(pad_ref[2 * N_EGROUPS], n_blocks, tail, 0)


def _moe_sort_kernel(n_blocks, loc_ref, len_ref, off_ref, pad_ref, h_ref, gate_ref, xs_hbm, gs_hbm,
                     xs_v, gs_v, zx_v, zg_v, sem, zsem):
    s = pl.program_id(0)
    slot = s % 2

    @pl.when(s == 0)
    def _():
        zx_v[...] = jnp.zeros_like(zx_v)
        zg_v[...] = jnp.zeros_like(zg_v)
        _pad_copies(pad_ref, n_blocks, zx_v, zg_v, xs_hbm, gs_hbm, zsem, 'start')

    gates = gate_ref[...]
    p = _perm_t(gates, loc_ref, s).T.astype(BF16)
    xs_v[slot] = _dot(p, h_ref[...]).astype(BF16)
    g_hi = gates.astype(BF16)
    r1 = gates - g_hi.astype(F32)
    g_mid = r1.astype(BF16)
    g_lo = (r1 - g_mid.astype(F32)).astype(BF16)
    parts = _dot(p, jnp.concatenate([g_hi, g_mid, g_lo], axis=1))
    gs_v[slot] = parts[:, :LANES] + parts[:, LANES:2 * LANES] + parts[:, 2 * LANES:]

    def copies(buf):
        def x_copy(lr, gr, rows):
            return pltpu.make_async_copy(xs_v.at[buf, pl.ds(lr, rows), :],
                                         xs_hbm.at[pl.ds(gr, rows), :], sem.at[0, buf])

        def g_copy(lr, gr, rows):
            return pltpu.make_async_copy(gs_v.at[buf, pl.ds(lr, rows), :],
                                         gs_hbm.at[pl.ds(gr, rows), :], sem.at[1, buf])

        def start(lr, gr, rows):
            x_copy(lr, gr, rows).start()
            g_copy(lr, gr, rows).start()

        def wait(lr, gr, rows):
            x_copy(lr, gr, rows).wait()
            g_copy(lr, gr, rows).wait()

        return start, wait

    _segment_copies(s, loc_ref, len_ref, off_ref, copies(slot)[0])

    @pl.when(s > 0)
    def _():
        _segment_copies(s - 1, loc_ref, len_ref, off_ref, copies(1 - slot)[1])

    @pl.when(s == pl.num_programs(0) - 1)
    def _():
        _segment_copies(s, loc_ref, len_ref, off_ref, copies(slot)[1])
        _pad_copies(pad_ref, n_blocks, zx_v, zg_v, xs_hbm, gs_hbm, zsem, 'wait')


def _moe_expert_kernel(bg_ref, nb_ref, xs_ref, gs_ref, wg_ref, wu_ref, wd_ref, o_ref):
    i = pl.program_id(0)

    @pl.when(i < nb_ref[0])
    def _():
        grp = bg_ref[i]
        x = xs_ref[...]
        gates = gs_ref[...]
        lane = lax.broadcasted_iota(jnp.int32, gates.shape, 1)
        acc = jnp.zeros(o_ref.shape, F32)
        for e in range(N_EPG):
            a = _dot(x, wg_ref[e].astype(BF16))
            u = _dot(x, wu_ref[e].astype(BF16))
            ge = jnp.sum(jnp.where(lane == N_EPG * grp + e, gates, 0.0), axis=-1, keepdims=True)
            hid = jax.nn.silu(a) * u * ge
            acc = acc + _dot(hid.astype(BF16), wd_ref[e].astype(BF16))
        o_ref[...] = acc.astype(BF16)

    @pl.when(i >= nb_ref[0])
    def _():
        o_ref[...] = jnp.zeros_like(o_ref)


def _moe_combine_kernel(loc_ref, len_ref, off_ref, gate_ref, x1_ref, mod_ref, ln2g_ref, ln2b_ref, o_hbm,
                        ctx_ref, lat_ref, o_v, sem, *, n_ctx_tiles):
    s = pl.program_id(0)
    slot = s % 2

    def copies(buf):
        def o_copy(lr, gr, rows):
            return pltpu.make_async_copy(o_hbm.at[pl.ds(gr, rows), :],
                                         o_v.at[buf, pl.ds(lr, rows), :], sem.at[buf])

        return (lambda lr, gr, rows: o_copy(lr, gr, rows).start()), (lambda lr, gr, rows: o_copy(lr, gr, rows).wait())

    @pl.when(s == 0)
    def _():
        o_v[...] = jnp.zeros_like(o_v)
        _segment_copies(s, loc_ref, len_ref, off_ref, copies(slot)[0])

    @pl.when(s + 1 < pl.num_programs(0))
    def _():
        _segment_copies(s + 1, loc_ref, len_ref, off_ref, copies(1 - slot)[0])

    pt = _perm_t(gate_ref[...], loc_ref, s).astype(BF16)
    _segment_copies(s, loc_ref, len_ref, off_ref, copies(slot)[1])
    f = _dot(pt, o_v[slot])
    g2 = mod_ref[0, :, 5 * D_MODEL:6 * D_MODEL]
    x2 = _norm(ALPHA * x1_ref[...] + g2 * f) * ln2g_ref[...] + ln2b_ref[...]

    @pl.when(s < n_ctx_tiles)
    def _():
        ctx_ref[...] = x2

    @pl.when(s >= n_ctx_tiles)
    def _():
        lat_ref[...] = x2


def _moe_plan(tile_counts, n_blocks):
    cnt = tile_counts.reshape(-1, LANES)[:, N_EXPERTS:N_EXPERTS + N_EGROUPS].astype(jnp.int32)
    len16 = ((cnt + MOE_UNIT - 1) // MOE_UNIT) * MOE_UNIT
    loc = jnp.cumsum(len16, axis=1) - len16
    rows_g = jnp.sum(len16, axis=0)
    reg_g = ((rows_g + MOE_TM - 1) // MOE_TM) * MOE_TM
    reg_start = jnp.cumsum(reg_g) - reg_g
    off = reg_start[None, :] + jnp.cumsum(len16, axis=0) - len16
    blk_end = jnp.cumsum(reg_g // MOE_TM)
    bi = jnp.arange(n_blocks, dtype=jnp.int32)
    blk_group = jnp.minimum(jnp.sum((bi[:, None] >= blk_end[None, :]).astype(jnp.int32), axis=1),
                            N_EGROUPS - 1)
    flat = lambda a: a.reshape(-1).astype(jnp.int32)
    pads = jnp.concatenate([reg_start + rows_g, (reg_g - rows_g) // MOE_UNIT, blk_end[-1:]])
    return (flat(loc), flat(len16), flat(off), flat(pads), blk_group.astype(jnp.int32),
            blk_end[-1:].astype(jnp.int32))


def _moe(h2_all, gates_all, tile_counts, x1_all, mod, w_gate, w_up, w_down, ln2g, ln2b, n_ctx,
         tokens_per_mod_row):
    n_tok = h2_all.shape[0]
    n_tiles = n_tok // MOE_ST
    n_ctx_tiles = n_ctx // MOE_ST
    max_rows = n_tok + n_tiles * N_EGROUPS * (MOE_UNIT - 1) + N_EGROUPS * (MOE_TM - 1)
    n_blocks = -(-max_rows // MOE_TM)
    n_rows = n_blocks * MOE_TM
    loc, len16, off, pads, blk_group, n_used = _moe_plan(tile_counts, n_blocks)

    tile = lambda w: pl.BlockSpec((MOE_ST, w), lambda s, *_: (s, 0))
    anyspec = pl.BlockSpec(memory_space=pl.ANY)
    xs, gs = pl.pallas_call(
        functools.partial(_moe_sort_kernel, n_blocks),
        grid_spec=pltpu.PrefetchScalarGridSpec(
            num_scalar_prefetch=4, grid=(n_tiles,),
            in_specs=[tile(D_MODEL), tile(LANES)],
            out_specs=[anyspec, anyspec],
            scratch_shapes=[pltpu.VMEM((2, MOE_SLOTS, D_MODEL), BF16), pltpu.VMEM((2, MOE_SLOTS, LANES), F32),
                            pltpu.VMEM((MOE_TM, D_MODEL), BF16), pltpu.VMEM((MOE_TM, LANES), F32),
                            pltpu.SemaphoreType.DMA((2, 2)), pltpu.SemaphoreType.DMA((2,))]),
        out_shape=[jax.ShapeDtypeStruct((n_rows, D_MODEL), BF16),
                   jax.ShapeDtypeStruct((n_rows, LANES), F32)],
        compiler_params=_cparams(("arbitrary",)),
        name="moe_sort",
    )(loc, len16, off, pads, h2_all, gates_all)

    blk = lambda w: pl.BlockSpec((MOE_TM, w), lambda i, bg, nb: (jnp.minimum(i, nb[0] - 1), 0))
    wspec = lambda a, b, mode: pl.BlockSpec((N_EPG, a, b), lambda i, bg, nb: (bg[i], 0, 0),
                                            pipeline_mode=mode)
    o_sorted = pl.pallas_call(
        _moe_expert_kernel,
        grid_spec=pltpu.PrefetchScalarGridSpec(
            num_scalar_prefetch=2, grid=(n_blocks,),
            in_specs=[blk(D_MODEL), blk(LANES), wspec(D_MODEL, D_EXPERT, None),
                      wspec(D_MODEL, D_EXPERT, None), wspec(D_EXPERT, D_MODEL, None)],
            out_specs=pl.BlockSpec((MOE_TM, D_MODEL), lambda i, bg, nb: (i, 0))),
        out_shape=jax.ShapeDtypeStruct((n_rows, D_MODEL), BF16),
        compiler_params=_cparams(("arbitrary",)),
        name="moe_experts",
    )(blk_group, n_used, xs, gs, w_gate, w_up, w_down)

    lat_per_row = tokens_per_mod_row // MOE_ST

    def mod_idx(s, *_):
        return (jnp.where(s < n_ctx_tiles, 0, 1 + (s - n_ctx_tiles) // lat_per_row), 0, 0)

    vec = pl.BlockSpec((1, D_MODEL), lambda s, *_: (0, 0))
    return pl.pallas_call(
        functools.partial(_moe_combine_kernel, n_ctx_tiles=n_ctx_tiles),
        grid_spec=pltpu.PrefetchScalarGridSpec(
            num_scalar_prefetch=3, grid=(n_tiles,),
            in_specs=[tile(LANES), tile(D_MODEL), pl.BlockSpec((1, 1, 6 * D_MODEL), mod_idx), vec, vec,
                      anyspec],
            out_specs=[pl.BlockSpec((MOE_ST, D_MODEL), lambda s, *_: (jnp.minimum(s, n_ctx_tiles - 1), 0)),
                       pl.BlockSpec((MOE_ST, D_MODEL), lambda s, *_: (jnp.maximum(s - n_ctx_tiles, 0), 0))],
            scratch_shapes=[pltpu.VMEM((2, MOE_SLOTS, D_MODEL), BF16), pltpu.SemaphoreType.DMA((2,))]),
        out_shape=[jax.ShapeDtypeStruct((n_ctx, D_MODEL), F32),
                   jax.ShapeDtypeStruct((n_tok - n_ctx, D_MODEL), F32)],
        compiler_params=_cparams(("arbitrary",)),
        name="moe_combine",
    )(loc, len16, off, gates_all, x1_all, mod.reshape(mod.shape[0], 1, 6 * D_MODEL), ln2g, ln2b, o_sorted)


def _grid_pos_embed(n_tokens):
    rows = n_tokens // GRID_W
    row = np.repeat(np.arange(rows, dtype=np.float64), GRID_W)
    col = np.tile(np.arange(GRID_W, dtype=np.float64), rows)
    quarter = D_MODEL // 4
    omega = 1.0 / (POS_BASE ** (np.arange(quarter, dtype=np.float64) / quarter))
    er = row[:, None] * omega
    ec = col[:, None] * omega
    return jnp.asarray(np.concatenate([np.sin(er), np.cos(er), np.sin(ec), np.cos(ec)], axis=-1), F32)


def _mixers(x, pos, mod3, h0_re, h0_im, tabs, filt, s5ops, wts, tm):
    bsz, n_tok, _ = x.shape
    shared = mod3.shape[0] == 1
    x3 = x.reshape(1, bsz * n_tok, D_MODEL) if shared else x
    proj_hy, u_s5 = _in_proj(x3, pos, mod3, wts['w_in'], tm)
    y_hy = _hyena(proj_hy.reshape(bsz, n_tok, 3 * D_HY), tabs, filt,
                  wts['hy_conv_w'], wts['hy_conv_b'], wts['hy_fbias'])
    y_s5, f_re, f_im = _s5(u_s5.reshape(bsz, n_tok, D_S5), s5ops, h0_re, h0_im)
    return y_hy.reshape(bsz * n_tok, D_HY), y_s5.reshape(bsz * n_tok, D_S5), f_re, f_im


def kernel(x_prompt, x_sample, state_s5_re, state_s5_im, c, c_ctx, w_ada, b_ada, w_in, hy_conv_w, hy_conv_b, hy_f_w1, hy_f_b1, hy_f_w2, hy_f_b2, hy_f_w3, hy_freq, hy_fbias, s5_a_re, s5_a_im, s5_log_dt, s5_b_re, s5_b_im, s5_c_re, s5_c_im, s5_d, s5_w_glu, s5_b_glu, out_norm_g, w_out, ln1_g, ln1_b, moe_w_r1, moe_b_r1, moe_w_r2, moe_b_r2, moe_w_gate, moe_w_up, moe_w_down, ln2_g, ln2_b):
    b_ctx, l_ctx, _ = x_prompt.shape
    b_lat, l_lat, _ = x_sample.shape
    g, p = S5_GROUPS, S5_STATE
    assert w_ada.shape[0] == 1, "single-layer trunk"
    l = 0

    nrow = 16
    cond = jnp.concatenate([c_ctx[None, :], c, jnp.zeros((nrow - 1 - b_lat, D_MODEL), F32)], axis=0)
    mod = _ada(cond, w_ada[l], b_ada[l])
    mod_ctx = mod[0:1].reshape(1, 1, 6 * D_MODEL)
    mod_lat = mod[1:1 + b_lat].reshape(b_lat, 1, 6 * D_MODEL)

    wr = jnp.concatenate([moe_w_r2[l].transpose(1, 0, 2).reshape(D_MODEL, N_EXPERTS), moe_w_r1[l]], axis=1)
    wr = jnp.pad(wr, ((0, 0), (0, LANES - wr.shape[1])))
    br = jnp.concatenate([moe_b_r2[l].reshape(-1), moe_b_r1[l]])
    br = jnp.pad(br, (0, LANES - br.shape[0])).reshape(1, LANES)
    wr_hi, wr_lo = _split(wr)

    wts = {
        'w_in': w_in[l], 'hy_conv_w': hy_conv_w[l], 'hy_conv_b': hy_conv_b[l],
        'hy_fbias': hy_fbias[l], 'w_glu': s5_w_glu[l].astype(BF16), 'b_glu': s5_b_glu[l].reshape(1, -1),
        'out_norm_g': out_norm_g[l].reshape(1, -1), 'w_out': w_out[l].astype(BF16),
        'ln1_g': ln1_g[l].reshape(1, -1), 'ln1_b': ln1_b[l].reshape(1, -1),
        'wr_hi': wr_hi, 'wr_lo': wr_lo, 'br': br,
        'w_gate': moe_w_gate[l], 'w_up': moe_w_up[l], 'w_down': moe_w_down[l],
        'ln2_g': ln2_g[l].reshape(1, -1), 'ln2_b': ln2_b[l].reshape(1, -1),
    }

    s5ops = _s5_operators(s5_a_re[l], s5_a_im[l], s5_log_dt[l], s5_b_re[l], s5_b_im[l],
                          s5_c_re[l], s5_c_im[l], s5_d[l])
    tabs_ctx = _tables(l_ctx)
    tabs_lat = _tables(l_lat)
    filt_args = (hy_f_w1[l], hy_f_b1[l], hy_f_w2[l], hy_f_b2[l], hy_f_w3[l], hy_freq[l])
    filt_ctx = _hyena_filters(l_ctx, tabs_ctx, *filt_args)
    filt_lat = _hyena_filters(l_lat, tabs_lat, *filt_args)

    zero = jnp.zeros((g, b_ctx, 2 * p), F32)
    yhy_c, ys5_c, f_re, f_im = _mixers(x_prompt, None, mod_ctx, zero, zero, tabs_ctx, filt_ctx, s5ops, wts, 1024)
    unpack = lambda f: f.reshape(g, b_ctx, 2, p).transpose(1, 2, 0, 3)[:, None]
    new_re, new_im = unpack(f_re), unpack(f_im)

    pack = lambda s: s[:, l].transpose(2, 0, 1, 3).reshape(g, b_lat, 2 * p)
    pos = _grid_pos_embed(l_lat)
    yhy_l, ys5_l, _, _ = _mixers(x_sample, pos, mod_lat, pack(state_s5_re), pack(state_s5_im),
                                 tabs_lat, filt_lat, s5ops, wts, 1024)

    n_ctx = b_ctx * l_ctx
    x1_all, h2_all, gates_all, tile_counts = _out_proj(
        x_prompt.reshape(n_ctx, D_MODEL), x_sample.reshape(b_lat * l_lat, D_MODEL), pos,
        yhy_c, yhy_l, ys5_c, ys5_l, mod, wts['w_glu'], wts['b_glu'], wts['out_norm_g'], wts['w_out'],
        wts['ln1_g'], wts['ln1_b'], wts['wr_hi'], wts['wr_lo'], wts['br'], OUT_TM)
    y_ctx, y_lat = _moe(h2_all, gates_all, tile_counts, x1_all, mod,
                        wts['w_gate'], wts['w_up'], wts['w_down'], wts['ln2_g'], wts['ln2_b'],
                        n_ctx, l_lat)
    return (y_ctx.reshape(x_prompt.shape), y_lat.reshape(x_sample.shape), new_re, new_im)
```

```python
import functools
import math

import numpy as np
import jax
import jax.numpy as jnp
from jax import lax
from jax.experimental import pallas as pl
from jax.experimental.pallas import tpu as pltpu

F32 = jnp.float32
BF16 = jnp.bfloat16

D_MODEL = 1024
DEPTH = 1
GRID_W = 64
POS_BASE = 10000.0
D_HY = 512
D_S5 = 512
S5_CH = 16
S5_GROUPS = 32
S5_STATE = 64
S5_CHUNK = 16
S5_ROW = S5_CHUNK * S5_CH
HY_BANDS = 16
HY_EMB = 1 + 2 * HY_BANDS
HY_HID = 64
HY_MIN_DECAY = math.log(1e-2) / 1.5
HY_MAX_DECAY = math.log(1e-2) / 0.3
N_EGROUPS = 4
N_EPG = 4
N_EXPERTS = 16
D_EXPERT = 512
LN_EPS = 1e-5
ALPHA = (2.0 * DEPTH) ** 0.25
LANES = 128
S5_GB = LANES // S5_CH
S5OPS_GB = 8
HY_CW = 512
MOE_ST = 512
MOE_SLOTS = 640
MOE_UNIT = 16
MOE_BIG = 64
MOE_TM = 512
VMEM_LIMIT = 60000 * 1024


def _cparams(sem):
    return pltpu.CompilerParams(dimension_semantics=sem, vmem_limit_bytes=VMEM_LIMIT)


def _split(x):
    hi = x.astype(BF16)
    lo = (x - hi.astype(F32)).astype(BF16)
    return hi, lo


def _dot(a, b):
    return jnp.dot(a, b, preferred_element_type=F32)


def _dot_t(a, b):
    return lax.dot_general(a, b, (((1,), (1,)), ((), ())), preferred_element_type=F32)


def _mm3(a, b):
    ah, al = _split(a)
    bh, bl = _split(b)
    return _dot(ah, bh) + _dot(al, bh) + _dot(ah, bl)


def _mm3_t(a, b):
    ah, al = _split(a)
    bh, bl = _split(b)
    return _dot_t(ah, bh) + _dot_t(al, bh) + _dot_t(ah, bl)


def _norm(x):
    xc = x - jnp.mean(x, axis=-1, keepdims=True)
    return xc * lax.rsqrt(jnp.mean(xc * xc, axis=-1, keepdims=True) + LN_EPS)


def _rms(y):
    return y * lax.rsqrt(jnp.mean(y * y, axis=-1, keepdims=True) + LN_EPS)


def _ada_kernel(cond_ref, w_ref, b_ref, o_ref):
    c = jax.nn.silu(cond_ref[...])
    o_ref[...] = _mm3(c, w_ref[...]) + b_ref[...]


def _ada(cond, w_ada, b_ada):
    nb = cond.shape[0]
    n = w_ada.shape[1]
    tn = 1024
    return pl.pallas_call(
        _ada_kernel,
        grid=(n // tn,),
        in_specs=[pl.BlockSpec((nb, D_MODEL), lambda j: (0, 0)),
                  pl.BlockSpec((D_MODEL, tn), lambda j: (0, j)),
                  pl.BlockSpec((1, tn), lambda j: (0, j))],
        out_specs=pl.BlockSpec((nb, tn), lambda j: (0, j)),
        out_shape=jax.ShapeDtypeStruct((nb, n), F32),
        compiler_params=_cparams(("arbitrary",)),
        name="ada",
    )(cond, w_ada, b_ada.reshape(1, n))


RADIX2_MIN_HALF = 256


def _dft_tables(n_half):
    n = 2 * n_half
    idx = np.arange(n_half, dtype=np.int64)
    m = (idx[:, None] * idx[None, :]) % n
    ang = 2.0 * np.pi * m.astype(np.float64) / n
    cm = np.cos(ang)
    sm = -np.sin(ang)
    sm[0, :] = 1.0 - 2.0 * (idx % 2)
    return cm, sm


def _dense_tables(n_tok):
    n_half = n_tok // 2
    f = np.concatenate([np.arange(n_half), n_tok - np.arange(n_half)]).astype(np.int64)
    sidx = np.arange(n_tok, dtype=np.int64)
    ang = 2.0 * np.pi * ((f[:, None] * sidx[None, :]) % (2 * n_tok)).astype(np.float64) / (2 * n_tok)
    cd = np.cos(ang)
    sd = -np.sin(ang)
    half = 2.0 * np.pi * ((n_half * sidx) % (2 * n_tok)).astype(np.float64) / (2 * n_tok)
    sd[0, :] = np.cos(half)
    sd[n_half, :] = -np.sin(half)
    return cd, sd


def _tables(n_tok):
    n_half = n_tok // 2
    bf = lambda t: jnp.asarray(t.astype(np.float32)).astype(BF16)
    c, s = _dense_tables(n_tok) if n_half < RADIX2_MIN_HALF else _dft_tables(n_half)
    stacks = (bf(np.concatenate([c, s], axis=0)), bf(np.concatenate([c.T, s.T], axis=1)))
    if n_half < RADIX2_MIN_HALF:
        return stacks
    ang = np.pi * np.arange(n_half, dtype=np.float64) / n_tok
    tw = [jnp.asarray(np.broadcast_to(v[:, None], (n_half, HY_CW)).astype(np.float32))
          for v in (np.cos(ang), -np.sin(ang))]
    return stacks + (tw[0], tw[1])


def _put_cols(ref, x):
    for j in range(ref.shape[0]):
        ref[j] = x[:, LANES * j:LANES * (j + 1)]


def _get_cols(ref):
    return jnp.concatenate([ref[j] for j in range(ref.shape[0])], axis=1)


def _get_parity(ref, parity):
    n_half = ref.shape[1] // 2
    return jnp.concatenate([ref[j, pl.ds(parity, n_half, stride=2), :] for j in range(ref.shape[0])], axis=1)


def _put_parity(ref, parity, x):
    n_half = ref.shape[1] // 2
    for j in range(ref.shape[0]):
        ref[j, pl.ds(parity, n_half, stride=2), :] = x[:, LANES * j:LANES * (j + 1)]


def _set_row0(x, v):
    first = lax.broadcasted_iota(jnp.int32, (8, x.shape[1]), 0) == 0
    return jnp.concatenate([jnp.where(first, v, x[:8]), x[8:]], axis=0)


def _rfft_packed(x_ref, tabs):
    n_tok = x_ref.shape[1]
    n_half = n_tok // 2
    if len(tabs) == 2:
        r = _dot(tabs[0], _get_cols(x_ref).astype(BF16))
        return r[:n_half], r[n_tok:n_tok + n_half], r[n_half:n_tok], r[n_tok + n_half:]
    fwd, _, tw_re, tw_im = tabs
    e = _dot(fwd, _get_parity(x_ref, 0).astype(BF16))
    o = _dot(fwd, _get_parity(x_ref, 1).astype(BF16))
    e_re, e_im, o_re, o_im = e[:n_half], e[n_half:], o[:n_half], o[n_half:]
    t_re = tw_re * o_re - tw_im * o_im
    t_im = tw_re * o_im + tw_im * o_re
    a_im = _set_row0(e_im + t_im, e_im[0:1])
    b_im = _set_row0(t_im - e_im, -o_im[0:1])
    return e_re + t_re, a_im, e_re - t_re, b_im


def _irfft_packed(y_ref, ya_re, ya_im, yb_re, yb_im, tabs):
    if len(tabs) == 2:
        y = jnp.concatenate([ya_re, yb_re, ya_im, yb_im], axis=0).astype(BF16)
        _put_cols(y_ref, _dot(tabs[1], y))
        return
    _, inv, tw_re, tw_im = tabs
    p_e = ya_re + yb_re
    q_e = _set_row0(ya_im - yb_im, ya_im[0:1])
    _put_parity(y_ref, 0, _dot(inv, jnp.concatenate([p_e, q_e], axis=0).astype(BF16)))
    ra_re = ya_re * tw_re + ya_im * tw_im
    ra_im = ya_im * tw_re - ya_re * tw_im
    rb_re = yb_im * tw_im - yb_re * tw_re
    rb_im = -(yb_re * tw_im + yb_im * tw_re)
    p_o = ra_re + rb_re
    q_o = _set_row0(ra_im - rb_im, -yb_im[0:1])
    _put_parity(y_ref, 1, _dot(inv, jnp.concatenate([p_o, q_o], axis=0).astype(BF16)))


def _filt_kernel(n_tok, n_tab, z_ref, t_ref, w1_ref, b1_ref, w2_ref, b2_ref, fr_ref, w3f_ref, w3b_ref,
                 dl_ref, *rest):
    tabs = tuple(r[...] for r in rest[:n_tab])
    kar_ref, kai_ref, kbr_ref, kbi_ref, p_ref, q_ref = rest[n_tab:]
    fr = fr_ref[...]
    h = jnp.sin(fr * (_mm3(z_ref[...], w1_ref[...]) + b1_ref[...]))
    h = jnp.sin(fr * (_mm3(h, w2_ref[...]) + b2_ref[...]))
    decay = jnp.exp(-t_ref[...] * dl_ref[...])
    row = lax.broadcasted_iota(jnp.int32, decay.shape, 0)
    hf = _mm3(h, w3f_ref[...]) * decay
    hb = jnp.where(row == 0, 0.0, _mm3(h, w3b_ref[...]) * decay)
    _put_cols(p_ref, hf + hb)
    _put_cols(q_ref, hf - hb)
    pa_re, pa_im, pb_re, _ = _rfft_packed(p_ref, tabs)
    _, qa_im, _, qb_im = _rfft_packed(q_ref, tabs)
    row0 = lax.broadcasted_iota(jnp.int32, pa_re.shape, 0) == 0
    inv_n = 1.0 / (2 * n_tok)
    w_re = jnp.where(row0, inv_n, 2.0 * inv_n)
    kar_ref[...] = w_re * pa_re
    kbr_ref[...] = w_re * pb_re
    kai_ref[...] = (2.0 * inv_n) * _set_row0(qa_im, pa_im[0:1])
    kbi_ref[...] = (2.0 * inv_n) * qb_im


def _hyena_filters(n_tok, tabs, hy_f_w1, hy_f_b1, hy_f_w2, hy_f_b2, hy_f_w3, hy_freq):
    n_half = n_tok // 2
    tt = np.linspace(0.0, 1.0, n_tok)[:, None]
    ang = (2.0 * np.pi * np.arange(n_tok) / n_tok)[:, None] * np.linspace(1e-4, HY_BANDS - 1, HY_BANDS)[None, :]
    z = np.concatenate([tt, np.cos(ang), -np.sin(ang), np.zeros((n_tok, LANES - HY_EMB))], axis=-1)
    z, t = jnp.asarray(z, F32), jnp.asarray(tt, F32)
    deltas = jnp.asarray(np.abs(np.linspace(HY_MIN_DECAY, HY_MAX_DECAY, D_HY))[None, :], F32)
    w1 = jnp.pad(hy_f_w1, ((0, LANES - HY_EMB), (0, 0)))
    ncb = D_HY // HY_CW
    full = lambda j: (0, 0)
    out_sd = jax.ShapeDtypeStruct((n_half, 2 * D_HY), F32)
    tab_specs = [pl.BlockSpec(t_.shape, full, pipeline_mode=pl.Buffered(1)) for t_ in tabs]
    return pl.pallas_call(
        functools.partial(_filt_kernel, n_tok, len(tabs)),
        grid=(2 * ncb,),
        in_specs=[pl.BlockSpec((n_tok, LANES), full),
                  pl.BlockSpec((n_tok, 1), full),
                  pl.BlockSpec((LANES, HY_HID), full),
                  pl.BlockSpec((1, HY_HID), full),
                  pl.BlockSpec((HY_HID, HY_HID), full),
                  pl.BlockSpec((1, HY_HID), full),
                  pl.BlockSpec((1, HY_HID), full),
                  pl.BlockSpec((HY_HID, HY_CW), lambda j: (0, 2 * ncb * (j // ncb) + j % ncb)),
                  pl.BlockSpec((HY_HID, HY_CW), lambda j: (0, 2 * ncb * (j // ncb) + ncb + j % ncb)),
                  pl.BlockSpec((1, HY_CW), lambda j: (0, j % ncb))] + tab_specs,
        out_specs=[pl.BlockSpec((n_half, HY_CW), lambda j: (0, j))] * 4,
        out_shape=[out_sd] * 4,
        scratch_shapes=[pltpu.VMEM((HY_CW // LANES, n_tok, LANES), F32)] * 2,
        compiler_params=_cparams(("arbitrary",)),
        name=f"filt{n_tok}",
    )(z, t, w1, hy_f_b1.reshape(1, -1), hy_f_w2, hy_f_b2.reshape(1, -1), hy_freq.reshape(1, -1),
      hy_f_w3, hy_f_w3, deltas, *tabs)


def _hyena_kernel(n_tab, pv_ref, p1_ref, p2_ref, cwv_ref, cw1_ref, cw2_ref, cbv_ref, cb1_ref, cb2_ref,
                  fbias_ref, *rest):
    tabs = tuple(r[...] for r in rest[:n_tab])
    (kar0_ref, kai0_ref, kbr0_ref, kbi0_ref, kar1_ref, kai1_ref, kbr1_ref, kbi1_ref,
     o_ref, u_ref, y_ref) = rest[n_tab:]
    n_tok = pv_ref.shape[1]
    row = lax.broadcasted_iota(jnp.int32, (n_tok, pv_ref.shape[2]), 0)

    def short_conv(p_ref, cw_ref, cb_ref):
        p = p_ref[0]
        prev = jnp.where(row == 0, 0.0, pltpu.roll(p, 1, axis=0))
        nxt = jnp.where(row == n_tok - 1, 0.0, pltpu.roll(p, n_tok - 1, axis=0))
        return cb_ref[...] + prev * cw_ref[0:1, :] + p * cw_ref[1:2, :] + nxt * cw_ref[2:3, :]

    def fftconv(u, kar_ref, kai_ref, kbr_ref, kbi_ref, skip):
        _put_cols(u_ref, u)
        ua_re, ua_im, ub_re, ub_im = _rfft_packed(u_ref, tabs)
        ka_re, ka_im, kb_re, kb_im = kar_ref[...], kai_ref[...], kbr_ref[...], kbi_ref[...]
        zero_row = jnp.zeros_like(ka_im[0:1])
        kaz = _set_row0(ka_im, zero_row)
        kbz = _set_row0(kb_im, zero_row)
        ya_re = ua_re * ka_re - ua_im * kaz
        yb_re = ub_re * kb_re - ub_im * kbz
        h_re = ua_im[0:1] * ka_im[0:1] - ub_im[0:1] * kb_im[0:1]
        h_im = ua_im[0:1] * kb_im[0:1] + ub_im[0:1] * ka_im[0:1]
        ya_im = _set_row0(ua_re * ka_im + ua_im * ka_re, h_re)
        yb_im = _set_row0(ub_re * kb_im + ub_im * kb_re, h_im)
        _irfft_packed(y_ref, ya_re, ya_im, yb_re, yb_im, tabs)
        return _get_cols(y_ref) + u * skip

    v = short_conv(pv_ref, cwv_ref, cbv_ref)
    x1 = short_conv(p1_ref, cw1_ref, cb1_ref)
    z = x1 * fftconv(v, kar0_ref, kai0_ref, kbr0_ref, kbi0_ref, fbias_ref[0:1, :])
    x2 = short_conv(p2_ref, cw2_ref, cb2_ref)
    o_ref[0] = x2 * fftconv(z, kar1_ref, kai1_ref, kbr1_ref, kbi1_ref, fbias_ref[1:2, :])


def _hyena(proj_hy, tabs, filt, hy_conv_w, hy_conv_b, hy_fbias):
    bsz, n_tok, _ = proj_hy.shape
    n_half = n_tok // 2
    ncb = D_HY // HY_CW
    cb = hy_conv_b.reshape(1, -1)
    tab_specs = [pl.BlockSpec(t.shape, lambda b, c: (0, 0), pipeline_mode=pl.Buffered(1)) for t in tabs]

    def pspec(k):
        return pl.BlockSpec((1, n_tok, HY_CW), lambda b, c: (b, 0, k * ncb + c))

    def cwspec(k):
        return pl.BlockSpec((3, HY_CW), lambda b, c: (0, k * ncb + c))

    def cbspec(k):
        return pl.BlockSpec((1, HY_CW), lambda b, c: (0, k * ncb + c))

    def fspec(o):
        mode = pl.Buffered(1) if ncb == 1 else None
        return pl.BlockSpec((n_half, HY_CW), lambda b, c: (0, o * ncb + c), pipeline_mode=mode)

    return pl.pallas_call(
        functools.partial(_hyena_kernel, len(tabs)),
        grid=(bsz, ncb),
        in_specs=[pspec(0), pspec(1), pspec(2), cwspec(0), cwspec(1), cwspec(2),
                  cbspec(0), cbspec(1), cbspec(2),
                  pl.BlockSpec((2, HY_CW), lambda b, c: (0, c))] + tab_specs + [fspec(0)] * 4 + [fspec(1)] * 4,
        out_specs=pl.BlockSpec((1, n_tok, HY_CW), lambda b, c: (b, 0, c)),
        out_shape=jax.ShapeDtypeStruct((bsz, n_tok, D_HY), F32),
        scratch_shapes=[pltpu.VMEM((HY_CW // LANES, n_tok, LANES), F32)] * 2,
        compiler_params=_cparams(("arbitrary", "arbitrary")),
        name=f"hyena{n_tok}",
    )(proj_hy, proj_hy, proj_hy, hy_conv_w, hy_conv_w, hy_conv_w, cb, cb, cb, hy_fbias,
      *tabs, *filt, *filt)


def _s5ops_kernel(*refs):
    for g in range(S5OPS_GB):
        _s5ops_group(g, *refs)


def _s5ops_group(g, are_ref, aim_ref, ldt_ref, btr_ref, bti_ref, cre_ref, cim_ref, d_ref,
                 opa_ref, opb_ref, g_ref, atr_ref, ati_ref, er_ref, ei_ref):
    a_re, a_im = are_ref[g], aim_ref[g]
    dt = jnp.exp(ldt_ref[g])
    mag = jnp.exp(a_re * dt)
    ab_re = mag * jnp.cos(a_im * dt)
    ab_im = mag * jnp.sin(a_im * dt)
    n_re, n_im = ab_re - 1.0, ab_im
    den = a_re * a_re + a_im * a_im
    q_re = (n_re * a_re + n_im * a_im) / den
    q_im = (n_im * a_re - n_re * a_im) / den
    bt_re, bt_im = btr_ref[g], bti_ref[g]
    bb_re = q_re * bt_re - q_im * bt_im
    bb_im = q_re * bt_im + q_im * bt_re
    c_re, c_im = cre_ref[g, 0:S5_CH, :], cim_ref[g, 0:S5_CH, :]
    pw = [(jnp.ones_like(ab_re), jnp.zeros_like(ab_re))]
    for _ in range(S5_CHUNK):
        pr, pi = pw[-1]
        pw.append((pr * ab_re - pi * ab_im, pr * ab_im + pi * ab_re))
    lane = lax.broadcasted_iota(jnp.int32, ab_re.shape, 1)
    fwd = lane < S5_STATE
    for s in range(S5_CHUNK):
        e_re = jnp.where(fwd, pw[S5_CHUNK - 1 - s][0], pw[s][0])
        e_im = jnp.where(fwd, pw[S5_CHUNK - 1 - s][1], pw[s][1])
        er_ref[g, pl.ds(S5_CH * s, S5_CH), :] = e_re * bb_re - e_im * bb_im
        ei_ref[g, pl.ds(S5_CH * s, S5_CH), :] = e_re * bb_im + e_im * bb_re
        g_re = jnp.where(fwd, pw[s + 1][0], pw[S5_CHUNK - s][0])
        g_im = jnp.where(fwd, pw[s + 1][1], pw[S5_CHUNK - s][1])
        g_ref[g, pl.ds(S5_CH * s, S5_CH), 0:LANES] = (c_re * g_re - c_im * g_im).astype(BF16)
        g_ref[g, pl.ds(S5_CH * s, S5_CH), LANES:2 * LANES] = (-(c_re * g_im + c_im * g_re)).astype(BF16)
    atr_ref[g] = pw[S5_CHUNK][0]
    ati_ref[g] = pw[S5_CHUNK][1]
    er, ei = er_ref[g], ei_ref[g]
    opa_ref[g, :, 2 * LANES:3 * LANES], opb_ref[g, :, 0:LANES] = _split(er)
    opa_ref[g, :, 3 * LANES:4 * LANES], opb_ref[g, :, LANES:2 * LANES] = _split(ei)
    lane2 = lax.broadcasted_iota(jnp.int32, er.shape, 1)
    row2 = lax.broadcasted_iota(jnp.int32, er.shape, 0)
    f2 = lane2 < S5_STATE
    zero = jnp.zeros_like(er)

    cp_re, cp_im = cre_ref[g], cim_ref[g]
    kf = _mm3_t(jnp.where(f2, er, zero), cp_re) - _mm3_t(jnp.where(f2, ei, zero), cp_im)
    kb = _mm3_t(jnp.where(f2, zero, er), cp_re) - _mm3_t(jnp.where(f2, zero, ei), cp_im)
    d_row = d_ref[g]
    steps_per_vreg = LANES // S5_CH
    for half in range(S5_CHUNK // steps_per_vreg):
        acc = zero
        for tt in range(steps_per_vreg):
            t = half * steps_per_vreg + tt
            nf = S5_CH * (S5_CHUNK - 1 - t)
            nb = S5_CH * t
            col_f = jnp.concatenate([kf[nf:], zero[:nf]], axis=0) if nf else kf
            col_b = jnp.concatenate([zero[:nb], kb[:S5_ROW - nb]], axis=0) if nb else kb
            diag = jnp.where((row2 // S5_CH == t) & (row2 % S5_CH == lane2), d_row, 0.0)
            col = col_f + col_b + diag
            r = pltpu.roll(col, S5_CH * tt, axis=1) if tt else col
            acc = jnp.where((lane2 >= S5_CH * tt) & (lane2 < S5_CH * (tt + 1)), r, acc)
        opa_ref[g, :, LANES * half:LANES * (half + 1)] = acc.astype(BF16)


def _s5_operators(s5_a_re, s5_a_im, s5_log_dt, s5_b_re, s5_b_im, s5_c_re, s5_c_im, s5_d):
    g, p, h = S5_GROUPS, S5_STATE, S5_CH
    cat = lambda x: jnp.concatenate([x[0], x[1]], axis=-1)
    a_re = cat(s5_a_re).reshape(g, 1, 2 * p)
    a_im = cat(s5_a_im).reshape(g, 1, 2 * p)
    ldt = cat(jnp.broadcast_to(s5_log_dt[:, :, None], (2, g, p))).reshape(g, 1, 2 * p)
    bt_re = cat(jnp.swapaxes(s5_b_re, -1, -2))
    bt_im = cat(jnp.swapaxes(s5_b_im, -1, -2))
    cpad = lambda c: jnp.pad(jnp.concatenate([c, c], axis=-1), ((0, 0), (0, LANES - h), (0, 0)))
    c_re, c_im = cpad(s5_c_re), cpad(s5_c_im)
    d_row = jnp.pad(s5_d.reshape(g, 1, h), ((0, 0), (0, 0), (0, LANES - h)))
    vec = pl.BlockSpec((S5OPS_GB, 1, 2 * p), lambda i: (i, 0, 0))
    hp = pl.BlockSpec((S5OPS_GB, h, 2 * p), lambda i: (i, 0, 0))
    sq = pl.BlockSpec((S5OPS_GB, LANES, 2 * p), lambda i: (i, 0, 0))
    wide = lambda n: pl.BlockSpec((S5OPS_GB, S5_ROW, n), lambda i: (i, 0, 0))
    vec_sd = jax.ShapeDtypeStruct((g, 1, 2 * p), F32)
    return pl.pallas_call(
        _s5ops_kernel,
        grid=(g // S5OPS_GB,),
        in_specs=[vec, vec, vec, hp, hp, sq, sq, vec],
        out_specs=[wide(2 * S5_ROW), wide(S5_ROW), wide(S5_ROW), vec, vec],
        out_shape=[jax.ShapeDtypeStruct((g, S5_ROW, 2 * S5_ROW), BF16),
                   jax.ShapeDtypeStruct((g, S5_ROW, S5_ROW), BF16),
                   jax.ShapeDtypeStruct((g, S5_ROW, S5_ROW), BF16), vec_sd, vec_sd],
        scratch_shapes=[pltpu.VMEM((S5OPS_GB, S5_ROW, 2 * p), F32)] * 2,
        compiler_params=_cparams(("arbitrary",)),
        name="s5ops",
    )(a_re, a_im, ldt, bt_re, bt_im, c_re, c_im, d_row)


def _block_transpose(xs):
    n = len(xs)
    lane = lax.broadcasted_iota(jnp.int32, xs[0].shape, 1)
    xs = list(xs)
    d = n // 2
    while d:
        keep = ((lane // S5_CH) & d) == 0
        for i in range(n):
            if i & d:
                continue
            lo, hi = xs[i], xs[i + d]
            xs[i] = jnp.where(keep, lo, pltpu.roll(hi, S5_CH * d, axis=1))
            xs[i + d] = jnp.where(keep, pltpu.roll(lo, LANES - S5_CH * d, axis=1), hi)
        d //= 2
    return xs


def _s5_kernel(bsz, n_chunks, u_ref, opa_ref, opb_ref, g_ref,
               atr_ref, ati_ref, h0r_ref, h0i_ref, y_ref, fr_ref, fi_ref,
               ua_ref, ub_ref, ya_ref, yb_ref, sr_ref, si_ref, xfr_ref, xfi_ref, xbr_ref, xbi_ref):
    nc = n_chunks
    spv = LANES // S5_CH
    rsub = min(nc, 32)

    def to_chunks(b, carry):
        for half, dst in ((0, ua_ref), (1, ub_ref)):
            for r0 in range(0, nc, rsub):
                xs = [u_ref[b, pl.ds(S5_CHUNK * r0 + half * spv + tt, rsub, stride=S5_CHUNK), :]
                      for tt in range(spv)]
                for k, blk in enumerate(_block_transpose(xs)):
                    dst[k, pl.ds(r0 * bsz + b, rsub, stride=bsz), :] = blk
        return carry

    lax.fori_loop(0, bsz, to_chunks, 0, unroll=2)

    lane = lax.broadcasted_iota(jnp.int32, (bsz, 2 * S5_STATE), 1)
    fwd = lane < S5_STATE
    lane_all = lax.broadcasted_iota(jnp.int32, (bsz * nc, 2 * S5_STATE), 1)
    fwd_all = lane_all < S5_STATE

    def group(k, slot):
        u = jnp.concatenate([ua_ref[k], ub_ref[k]], axis=1)
        uh, ul = _split(u)
        wide = _dot(uh, opa_ref[k])
        inj = (wide[:, S5_ROW:] + _dot(ul, opa_ref[k, :, S5_ROW:2 * S5_ROW]) + _dot(uh, opb_ref[k]))
        sr_ref[slot] = inj[:, :LANES]
        si_ref[slot] = inj[:, LANES:]
        at_re, at_im = atr_ref[k], ati_ref[k]
        y_intra = wide[:, :S5_ROW]

        def step(i, xc):
            x_re, x_im = xc
            rf = pl.ds(pl.multiple_of(i * bsz, bsz), bsz)
            rb = pl.ds(pl.multiple_of((nc - 1 - i) * bsz, bsz), bsz)
            xfr_ref[slot, rf, :] = x_re
            xfi_ref[slot, rf, :] = x_im
            xbr_ref[slot, rb, :] = x_re
            xbi_ref[slot, rb, :] = x_im
            s_re = jnp.where(fwd, sr_ref[slot, rf, :], sr_ref[slot, rb, :])
            s_im = jnp.where(fwd, si_ref[slot, rf, :], si_ref[slot, rb, :])
            return (at_re * x_re - at_im * x_im + s_re, at_re * x_im + at_im * x_re + s_im)

        x_re, x_im = lax.fori_loop(0, nc, step, (h0r_ref[k], h0i_ref[k]), unroll=True)
        fr_ref[k] = x_re
        fi_ref[k] = x_im
        xp = jnp.concatenate([jnp.where(fwd_all, xfr_ref[slot], xbr_ref[slot]),
                              jnp.where(fwd_all, xfi_ref[slot], xbi_ref[slot])], axis=1).astype(BF16)
        y = y_intra + _dot_t(xp, g_ref[k])
        ya_ref[k] = y[:, :LANES]
        yb_ref[k] = y[:, LANES:]

    def group_pair(j, carry):
        group(2 * j, 0)
        group(2 * j + 1, 1)
        return carry

    lax.fori_loop(0, S5_GB // 2, group_pair, 0)

    def to_tokens(b, carry):
        for half, src in ((0, ya_ref), (1, yb_ref)):
            for r0 in range(0, nc, rsub):
                ys = [src[k, pl.ds(r0 * bsz + b, rsub, stride=bsz), :] for k in range(S5_GB)]
                for tt, blk in enumerate(_block_transpose(ys)):
                    y_ref[b, pl.ds(S5_CHUNK * r0 + half * spv + tt, rsub, stride=S5_CHUNK), :] = blk
        return carry

    lax.fori_loop(0, bsz, to_tokens, 0, unroll=2)


def _s5(u, ops, h0_re, h0_im):
    bsz, n_tok, _ = u.shape
    g, p = S5_GROUPS, S5_STATE
    nc = n_tok // S5_CHUNK
    rows = nc * bsz
    tok = pl.BlockSpec((bsz, n_tok, LANES), lambda j: (0, 0, j))
    gspec = lambda shape: pl.BlockSpec((S5_GB,) + shape, lambda j: (j, 0, 0))
    return pl.pallas_call(
        functools.partial(_s5_kernel, bsz, nc),
        grid=(g // S5_GB,),
        in_specs=[tok, gspec((S5_ROW, 2 * S5_ROW)), gspec((S5_ROW, S5_ROW)), gspec((S5_ROW, S5_ROW)),
                  gspec((1, 2 * p)), gspec((1, 2 * p)), gspec((bsz, 2 * p)), gspec((bsz, 2 * p))],
        out_specs=[tok, gspec((bsz, 2 * p)), gspec((bsz, 2 * p))],
        out_shape=[jax.ShapeDtypeStruct((bsz, n_tok, D_S5), F32),
                   jax.ShapeDtypeStruct((g, bsz, 2 * p), F32),
                   jax.ShapeDtypeStruct((g, bsz, 2 * p), F32)],
        scratch_shapes=([pltpu.VMEM((S5_GB, rows, LANES), F32)] * 4
                        + [pltpu.VMEM((2, rows, 2 * p), F32)] * 6),
        compiler_params=_cparams(("arbitrary",)),
        name=f"s5_{n_tok}",
    )(u, *ops, h0_re, h0_im)


def _in_kernel(has_pos, *refs):
    if has_pos:
        x_ref, pos_ref, mod_ref, w_ref, hy_ref, s5_ref = refs
        x = x_ref[0] + pos_ref[...]
    else:
        x_ref, mod_ref, w_ref, hy_ref, s5_ref = refs
        x = x_ref[0]
    sh1 = mod_ref[0, :, 0:D_MODEL]
    sc1 = mod_ref[0, :, D_MODEL:2 * D_MODEL]
    h = _norm(x) * (1.0 + sc1) + sh1
    proj = _dot(h.astype(BF16), w_ref[...].astype(BF16))
    hy_ref[0] = proj[:, :3 * D_HY]
    s5_ref[0] = proj[:, 3 * D_HY:]


def _in_proj(x3, pos, mod3, w_in, tm):
    nb, lt, _ = x3.shape
    has_pos = pos is not None
    per_batch = mod3.shape[0] > 1
    midx = (lambda b, i: (b, 0, 0)) if per_batch else (lambda b, i: (0, 0, 0))
    in_specs = [pl.BlockSpec((1, tm, D_MODEL), lambda b, i: (b, i, 0))]
    args = [x3]
    if has_pos:
        in_specs.append(pl.BlockSpec((tm, D_MODEL), lambda b, i: (i, 0)))
        args.append(pos)
    in_specs += [pl.BlockSpec((1, 1, 6 * D_MODEL), midx),
                 pl.BlockSpec((D_MODEL, 3 * D_HY + D_S5), lambda b, i: (0, 0), pipeline_mode=pl.Buffered(1))]
    args += [mod3, w_in]
    return pl.pallas_call(
        functools.partial(_in_kernel, has_pos),
        grid=(nb, lt // tm),
        in_specs=in_specs,
        out_specs=[pl.BlockSpec((1, tm, 3 * D_HY), lambda b, i: (b, i, 0)),
                   pl.BlockSpec((1, tm, D_S5), lambda b, i: (b, i, 0))],
        out_shape=[jax.ShapeDtypeStruct((nb, lt, 3 * D_HY), F32),
                   jax.ShapeDtypeStruct((nb, lt, D_S5), F32)],
        compiler_params=_cparams(("arbitrary", "arbitrary")),
        name=f"in_proj{nb}",
    )(*args)


def _route(logits):
    lane = lax.broadcasted_iota(jnp.int32, logits.shape, 1)
    lane_f = lane.astype(F32)
    neg = -jnp.inf
    big = float(LANES)
    m1 = (lane >= N_EXPERTS) & (lane < N_EXPERTS + N_EGROUPS)
    l1 = jnp.where(m1, logits, neg)
    top1 = jnp.max(l1, axis=-1, keepdims=True)
    grp = jnp.min(jnp.where(l1 == top1, lane_f, big), axis=-1, keepdims=True) - float(N_EXPERTS)
    den = jnp.sum(jnp.where(m1, jnp.exp(logits - top1), 0.0), axis=-1, keepdims=True)
    p_grp = 1.0 / den
    lo = grp * float(N_EPG)
    m2 = (lane_f >= lo) & (lane_f < lo + float(N_EPG))
    l2 = jnp.where(m2, logits, neg)
    v1 = jnp.max(l2, axis=-1, keepdims=True)
    i1 = jnp.min(jnp.where(l2 == v1, lane_f, big), axis=-1, keepdims=True)
    l2b = jnp.where(lane_f == i1, neg, l2)
    v2 = jnp.max(l2b, axis=-1, keepdims=True)
    i2 = jnp.min(jnp.where(l2b == v2, lane_f, big), axis=-1, keepdims=True)
    e = jnp.exp(v2 - v1)
    w1 = 1.0 / (1.0 + e)
    w2 = e / (1.0 + e)
    gates = jnp.where(lane_f == i1, w1 * p_grp, 0.0) + jnp.where(lane_f == i2, w2 * p_grp, 0.0)
    return jnp.where(lane_f == grp + float(N_EXPERTS), 1.0, gates)


def _out_kernel(n_ctx_blocks, xc_ref, xl_ref, pos_ref, yhyc_ref, yhyl_ref, ys5c_ref, ys5l_ref, mod_ref,
                wglu_ref, bglu_ref, ong_ref, wout_ref, ln1g_ref, ln1b_ref, wrh_ref, wrl_ref, br_ref,
                x1_ref, h2_ref, gate_ref, cnt_ref):
    is_ctx = pl.program_id(0) < n_ctx_blocks
    x = jnp.where(is_ctx, xc_ref[...], xl_ref[...] + pos_ref[...])
    y = jnp.where(is_ctx, ys5c_ref[...], ys5l_ref[...])
    y_hy = jnp.where(is_ctx, yhyc_ref[...], yhyl_ref[...])
    s5 = jax.nn.gelu(y) * jax.nn.sigmoid(_dot(y.astype(BF16), wglu_ref[...]) + bglu_ref[...])
    m_hy = _rms(y_hy) * ong_ref[:, 0:D_HY]
    m_s5 = _rms(s5) * ong_ref[:, D_HY:]
    o = (_dot(m_hy.astype(BF16), wout_ref[0:D_HY, :]) + _dot(m_s5.astype(BF16), wout_ref[D_HY:, :]))
    g1 = mod_ref[0, :, 2 * D_MODEL:3 * D_MODEL]
    sh2 = mod_ref[0, :, 3 * D_MODEL:4 * D_MODEL]
    sc2 = mod_ref[0, :, 4 * D_MODEL:5 * D_MODEL]
    x1 = _norm(ALPHA * x + g1 * o) * ln1g_ref[...] + ln1b_ref[...]
    x1_ref[...] = x1
    h2 = _norm(x1) * (1.0 + sc2) + sh2
    h2_ref[...] = h2.astype(BF16)
    hh, hl = _split(h2)
    logits = (_dot(hh, wrh_ref[...]) + _dot(hl, wrh_ref[...]) + _dot(hh, wrl_ref[...]) + br_ref[...])
    gates = _route(logits)
    gate_ref[...] = gates
    cnt_ref[0] = jnp.sum(gates, axis=0, keepdims=True)


def _out_proj(xc, xl, pos, yhy_c, yhy_l, ys5_c, ys5_l, mod, wglu_bf, bglu, ong, wout_bf, ln1g, ln1b,
              wr_hi, wr_lo, br, tm):
    n_ctx, n_lat = xc.shape[0], xl.shape[0]
    l_lat = pos.shape[0]
    ncb, nlb, npb = n_ctx // tm, n_lat // tm, l_lat // tm
    ctx = lambda w: pl.BlockSpec((tm, w), lambda i: (jnp.minimum(i, ncb - 1), 0))
    lat = lambda w: pl.BlockSpec((tm, w), lambda i: (jnp.maximum(i - ncb, 0), 0))
    full = lambda shape: pl.BlockSpec(shape, lambda i: (0,) * len(shape))
    out = lambda w: pl.BlockSpec((tm, w), lambda i: (i, 0))
    mod_idx = lambda i: (jnp.where(i < ncb, 0, 1 + jnp.maximum(i - ncb, 0) // npb), 0, 0)
    n_all = n_ctx + n_lat
    return pl.pallas_call(
        functools.partial(_out_kernel, ncb),
        grid=(ncb + nlb,),
        in_specs=[ctx(D_MODEL), lat(D_MODEL),
                  pl.BlockSpec((tm, D_MODEL), lambda i: (jnp.maximum(i - ncb, 0) % npb, 0)),
                  ctx(D_HY), lat(D_HY), ctx(D_S5), lat(D_S5),
                  pl.BlockSpec((1, 1, 6 * D_MODEL), mod_idx),
                  full((D_S5, D_S5)), full((1, D_S5)), full((1, D_MODEL)), full((D_MODEL, D_MODEL)),
                  full((1, D_MODEL)), full((1, D_MODEL)), full((D_MODEL, LANES)), full((D_MODEL, LANES)),
                  full((1, LANES))],
        out_specs=[out(D_MODEL), out(D_MODEL), out(LANES), pl.BlockSpec((1, 1, LANES), lambda i: (i, 0, 0))],
        out_shape=[jax.ShapeDtypeStruct((n_all, D_MODEL), F32),
                   jax.ShapeDtypeStruct((n_all, D_MODEL), BF16),
                   jax.ShapeDtypeStruct((n_all, LANES), F32),
                   jax.ShapeDtypeStruct((n_all // tm, 1, LANES), F32)],
        compiler_params=_cparams(("arbitrary",)),
        name="out_proj",
    )(xc, xl, pos, yhy_c, yhy_l, ys5_c, ys5_l, mod.reshape(mod.shape[0], 1, 6 * D_MODEL),
      wglu_bf, bglu, ong, wout_bf, ln1g, ln1b, wr_hi, wr_lo, br)


def _perm_t(gates, loc_ref, s):
    n = gates.shape[0]
    lane = lax.broadcasted_iota(jnp.int32, gates.shape, 1)
    oh = jnp.where((lane >= N_EXPERTS) & (lane < N_EXPERTS + N_EGROUPS), gates, 0.0)
    r = lax.broadcasted_iota(jnp.int32, (n, n), 0)
    c = lax.broadcasted_iota(jnp.int32, (n, n), 1)
    earlier = jnp.where(c < r, 1.0, 0.0).astype(BF16)
    cum = _dot(earlier, oh.astype(BF16))
    rank = jnp.sum(cum * oh, axis=-1, keepdims=True)
    lane1 = lax.broadcasted_iota(jnp.int32, (1, LANES), 1)
    locv = jnp.zeros((1, LANES), F32)
    for grp in range(N_EGROUPS):
        locv = jnp.where(lane1 == N_EXPERTS + grp, loc_ref[N_EGROUPS * s + grp].astype(F32), locv)
    dest = rank + jnp.sum(oh * locv, axis=-1, keepdims=True)
    slot = lax.broadcasted_iota(jnp.int32, (n, MOE_SLOTS), 1).astype(F32)
    return jnp.where(slot == dest, 1.0, 0.0)


def _segment_copies(s, loc_ref, len_ref, off_ref, make):
    for grp in range(N_EGROUPS):
        loc = loc_ref[N_EGROUPS * s + grp]
        off = off_ref[N_EGROUPS * s + grp]
        length = len_ref[N_EGROUPS * s + grp]
        n_big = length // MOE_BIG
        done = n_big * MOE_BIG

        def big(i, carry):
            make(pl.multiple_of(loc + MOE_BIG * i, MOE_UNIT), pl.multiple_of(off + MOE_BIG * i, MOE_UNIT), MOE_BIG)
            return carry

        def unit(i, carry):
            make(pl.multiple_of(loc + done + MOE_UNIT * i, MOE_UNIT),
                 pl.multiple_of(off + done + MOE_UNIT * i, MOE_UNIT), MOE_UNIT)
            return carry

        lax.fori_loop(0, n_big, big, 0)
        lax.fori_loop(0, (length - done) // MOE_UNIT, unit, 0)


def _pad_copies(pad_ref, n_blocks, zx_v, zg_v, xs_hbm, gs_hbm, sem, op):
    def unit(row, rows):
        getattr(pltpu.make_async_copy(zx_v.at[pl.ds(0, rows), :], xs_hbm.at[pl.ds(row, rows), :], sem.at[0]), op)()
        getattr(pltpu.make_async_copy(zg_v.at[pl.ds(0, rows), :], gs_hbm.at[pl.ds(row, rows), :], sem.at[1]), op)()

    for grp in range(N_EGROUPS):
        start = pad_ref[grp]

        def body(i, carry):
            unit(pl.multiple_of(start + MOE_UNIT * i, MOE_UNIT), MOE_UNIT)
            return carry

        lax.fori_loop(0, pad_ref[N_EGROUPS + grp], body, 0)

    def tail(b, carry):
        unit(pl.multiple_of(b * MOE_TM, MOE_TM), MOE_TM)
        return carry

    lax.fori_loop(pad_ref[2 * N_EGROUPS], n_blocks, tail, 0)


def _moe_sort_kernel(n_blocks, loc_ref, len_ref, off_ref, pad_ref, h_ref, gate_ref, xs_hbm, gs_hbm,
                     xs_v, gs_v, zx_v, zg_v, sem, zsem):
    s = pl.program_id(0)
    slot = s % 2

    @pl.when(s == 0)
    def _():
        zx_v[...] = jnp.zeros_like(zx_v)
        zg_v[...] = jnp.zeros_like(zg_v)
        _pad_copies(pad_ref, n_blocks, zx_v, zg_v, xs_hbm, gs_hbm, zsem, 'start')

    gates = gate_ref[...]
    p = _perm_t(gates, loc_ref, s).T.astype(BF16)
    xs_v[slot] = _dot(p, h_ref[...]).astype(BF16)
    g_hi = gates.astype(BF16)
    r1 = gates - g_hi.astype(F32)
    g_mid = r1.astype(BF16)
    g_lo = (r1 - g_mid.astype(F32)).astype(BF16)
    parts = _dot(p, jnp.concatenate([g_hi, g_mid, g_lo], axis=1))
    gs_v[slot] = parts[:, :LANES] + parts[:, LANES:2 * LANES] + parts[:, 2 * LANES:]

    def copies(buf):
        def x_copy(lr, gr, rows):
            return pltpu.make_async_copy(xs_v.at[buf, pl.ds(lr, rows), :],
                                         xs_hbm.at[pl.ds(gr, rows), :], sem.at[0, buf])

        def g_copy(lr, gr, rows):
            return pltpu.make_async_copy(gs_v.at[buf, pl.ds(lr, rows), :],
                                         gs_hbm.at[pl.ds(gr, rows), :], sem.at[1, buf])

        def start(lr, gr, rows):
            x_copy(lr, gr, rows).start()
            g_copy(lr, gr, rows).start()

        def wait(lr, gr, rows):
            x_copy(lr, gr, rows).wait()
            g_copy(lr, gr, rows).wait()

        return start, wait

    _segment_copies(s, loc_ref, len_ref, off_ref, copies(slot)[0])

    @pl.when(s > 0)
    def _():
        _segment_copies(s - 1, loc_ref, len_ref, off_ref, copies(1 - slot)[1])

    @pl.when(s == pl.num_programs(0) - 1)
    def _():
        _segment_copies(s, loc_ref, len_ref, off_ref, copies(slot)[1])
        _pad_copies(pad_ref, n_blocks, zx_v, zg_v, xs_hbm, gs_hbm, zsem, 'wait')


def _moe_expert_kernel(bg_ref, nb_ref, xs_ref, gs_ref, wg_ref, wu_ref, wd_ref, o_ref):
    i = pl.program_id(0)

    @pl.when(i < nb_ref[0])
    def _():
        grp = bg_ref[i]
        x = xs_ref[...]
        gates = gs_ref[...]
        lane = lax.broadcasted_iota(jnp.int32, gates.shape, 1)
        acc = jnp.zeros(o_ref.shape, F32)
        for e in range(N_EPG):
            a = _dot(x, wg_ref[e].astype(BF16))
            u = _dot(x, wu_ref[e].astype(BF16))
            ge = jnp.sum(jnp.where(lane == N_EPG * grp + e, gates, 0.0), axis=-1, keepdims=True)
            hid = jax.nn.silu(a) * u * ge
            acc = acc + _dot(hid.astype(BF16), wd_ref[e].astype(BF16))
        o_ref[...] = acc.astype(BF16)

    @pl.when(i >= nb_ref[0])
    def _():
        o_ref[...] = jnp.zeros_like(o_ref)


def _moe_combine_kernel(loc_ref, len_ref, off_ref, gate_ref, x1_ref, mod_ref, ln2g_ref, ln2b_ref, o_hbm,
                        ctx_ref, lat_ref, o_v, sem, *, n_ctx_tiles):
    s = pl.program_id(0)
    slot = s % 2

    def copies(buf):
        def o_copy(lr, gr, rows):
            return pltpu.make_async_copy(o_hbm.at[pl.ds(gr, rows), :],
                                         o_v.at[buf, pl.ds(lr, rows), :], sem.at[buf])

        return (lambda lr, gr, rows: o_copy(lr, gr, rows).start()), (lambda lr, gr, rows: o_copy(lr, gr, rows).wait())

    @pl.when(s == 0)
    def _():
        o_v[...] = jnp.zeros_like(o_v)
        _segment_copies(s, loc_ref, len_ref, off_ref, copies(slot)[0])

    @pl.when(s + 1 < pl.num_programs(0))
    def _():
        _segment_copies(s + 1, loc_ref, len_ref, off_ref, copies(1 - slot)[0])

    pt = _perm_t(gate_ref[...], loc_ref, s).astype(BF16)
    _segment_copies(s, loc_ref, len_ref, off_ref, copies(slot)[1])
    f = _dot(pt, o_v[slot])
    g2 = mod_ref[0, :, 5 * D_MODEL:6 * D_MODEL]
    x2 = _norm(ALPHA * x1_ref[...] + g2 * f) * ln2g_ref[...] + ln2b_ref[...]

    @pl.when(s < n_ctx_tiles)
    def _():
        ctx_ref[...] = x2

    @pl.when(s >= n_ctx_tiles)
    def _():
        lat_ref[...] = x2


def _moe_plan(tile_counts, n_blocks):
    cnt = tile_counts[:, 0, N_EXPERTS:N_EXPERTS + N_EGROUPS].astype(jnp.int32)
    len16 = ((cnt + MOE_UNIT - 1) // MOE_UNIT) * MOE_UNIT
    loc = jnp.cumsum(len16, axis=1) - len16
    rows_g = jnp.sum(len16, axis=0)
    reg_g = ((rows_g + MOE_TM - 1) // MOE_TM) * MOE_TM
    reg_start = jnp.cumsum(reg_g) - reg_g
    off = reg_start[None, :] + jnp.cumsum(len16, axis=0) - len16
    blk_end = jnp.cumsum(reg_g // MOE_TM)
    bi = jnp.arange(n_blocks, dtype=jnp.int32)
    blk_group = jnp.minimum(jnp.sum((bi[:, None] >= blk_end[None, :]).astype(jnp.int32), axis=1),
                            N_EGROUPS - 1)
    flat = lambda a: a.reshape(-1).astype(jnp.int32)
    pads = jnp.concatenate([reg_start + rows_g, (reg_g - rows_g) // MOE_UNIT, blk_end[-1:]])
    return (flat(loc), flat(len16), flat(off), flat(pads), blk_group.astype(jnp.int32),
            blk_end[-1:].astype(jnp.int32))


def _moe(h2_all, gates_all, tile_counts, x1_all, mod, w_gate, w_up, w_down, ln2g, ln2b, n_ctx,
         tokens_per_mod_row):
    n_tok = h2_all.shape[0]
    n_tiles = n_tok // MOE_ST
    n_ctx_tiles = n_ctx // MOE_ST
    max_rows = n_tok + n_tiles * N_EGROUPS * (MOE_UNIT - 1) + N_EGROUPS * (MOE_TM - 1)
    n_blocks = -(-max_rows // MOE_TM)
    n_rows = n_blocks * MOE_TM
    loc, len16, off, pads, blk_group, n_used = _moe_plan(tile_counts, n_blocks)

    tile = lambda w: pl.BlockSpec((MOE_ST, w), lambda s, *_: (s, 0))
    anyspec = pl.BlockSpec(memory_space=pl.ANY)
    xs, gs = pl.pallas_call(
        functools.partial(_moe_sort_kernel, n_blocks),
        grid_spec=pltpu.PrefetchScalarGridSpec(
            num_scalar_prefetch=4, grid=(n_tiles,),
            in_specs=[tile(D_MODEL), tile(LANES)],
            out_specs=[anyspec, anyspec],
            scratch_shapes=[pltpu.VMEM((2, MOE_SLOTS, D_MODEL), BF16), pltpu.VMEM((2, MOE_SLOTS, LANES), F32),
                            pltpu.VMEM((MOE_TM, D_MODEL), BF16), pltpu.VMEM((MOE_TM, LANES), F32),
                            pltpu.SemaphoreType.DMA((2, 2)), pltpu.SemaphoreType.DMA((2,))]),
        out_shape=[jax.ShapeDtypeStruct((n_rows, D_MODEL), BF16),
                   jax.ShapeDtypeStruct((n_rows, LANES), F32)],
        compiler_params=_cparams(("arbitrary",)),
        name="moe_sort",
    )(loc, len16, off, pads, h2_all, gates_all)

    blk = lambda w: pl.BlockSpec((MOE_TM, w), lambda i, bg, nb: (jnp.minimum(i, nb[0] - 1), 0))
    wspec = lambda a, b, mode: pl.BlockSpec((N_EPG, a, b), lambda i, bg, nb: (bg[i], 0, 0),
                                            pipeline_mode=mode)
    o_sorted = pl.pallas_call(
        _moe_expert_kernel,
        grid_spec=pltpu.PrefetchScalarGridSpec(
            num_scalar_prefetch=2, grid=(n_blocks,),
            in_specs=[blk(D_MODEL), blk(LANES), wspec(D_MODEL, D_EXPERT, None),
                      wspec(D_MODEL, D_EXPERT, None), wspec(D_EXPERT, D_MODEL, None)],
            out_specs=pl.BlockSpec((MOE_TM, D_MODEL), lambda i, bg, nb: (i, 0))),
        out_shape=jax.ShapeDtypeStruct((n_rows, D_MODEL), BF16),
        compiler_params=_cparams(("arbitrary",)),
        name="moe_experts",
    )(blk_group, n_used, xs, gs, w_gate, w_up, w_down)

    lat_per_row = tokens_per_mod_row // MOE_ST

    def mod_idx(s, *_):
        return (jnp.where(s < n_ctx_tiles, 0, 1 + (s - n_ctx_tiles) // lat_per_row), 0, 0)

    vec = pl.BlockSpec((1, D_MODEL), lambda s, *_: (0, 0))
    return pl.pallas_call(
        functools.partial(_moe_combine_kernel, n_ctx_tiles=n_ctx_tiles),
        grid_spec=pltpu.PrefetchScalarGridSpec(
            num_scalar_prefetch=3, grid=(n_tiles,),
            in_specs=[tile(LANES), tile(D_MODEL), pl.BlockSpec((1, 1, 6 * D_MODEL), mod_idx), vec, vec,
                      anyspec],
            out_specs=[pl.BlockSpec((MOE_ST, D_MODEL), lambda s, *_: (jnp.minimum(s, n_ctx_tiles - 1), 0)),
                       pl.BlockSpec((MOE_ST, D_MODEL), lambda s, *_: (jnp.maximum(s - n_ctx_tiles, 0), 0))],
            scratch_shapes=[pltpu.VMEM((2, MOE_SLOTS, D_MODEL), BF16), pltpu.SemaphoreType.DMA((2,))]),
        out_shape=[jax.ShapeDtypeStruct((n_ctx, D_MODEL), F32),
                   jax.ShapeDtypeStruct((n_tok - n_ctx, D_MODEL), F32)],
        compiler_params=_cparams(("arbitrary",)),
        name="moe_combine",
    )(loc, len16, off, gates_all, x1_all, mod.reshape(mod.shape[0], 1, 6 * D_MODEL), ln2g, ln2b, o_sorted)


def _grid_pos_embed(n_tokens):
    rows = n_tokens // GRID_W
    row = np.repeat(np.arange(rows, dtype=np.float64), GRID_W)
    col = np.tile(np.arange(GRID_W, dtype=np.float64), rows)
    quarter = D_MODEL // 4
    omega = 1.0 / (POS_BASE ** (np.arange(quarter, dtype=np.float64) / quarter))
    er = row[:, None] * omega
    ec = col[:, None] * omega
    return jnp.asarray(np.concatenate([np.sin(er), np.cos(er), np.sin(ec), np.cos(ec)], axis=-1), F32)


def _mixers(x, pos, mod3, h0_re, h0_im, tabs, filt, s5ops, wts, tm):
    bsz, n_tok, _ = x.shape
    shared = mod3.shape[0] == 1
    x3 = x.reshape(1, bsz * n_tok, D_MODEL) if shared else x
    proj_hy, u_s5 = _in_proj(x3, pos, mod3, wts['w_in'], tm)
    y_hy = _hyena(proj_hy.reshape(bsz, n_tok, 3 * D_HY), tabs, filt,
                  wts['hy_conv_w'], wts['hy_conv_b'], wts['hy_fbias'])
    y_s5, f_re, f_im = _s5(u_s5.reshape(bsz, n_tok, D_S5), s5ops, h0_re, h0_im)
    return y_hy.reshape(bsz * n_tok, D_HY), y_s5.reshape(bsz * n_tok, D_S5), f_re, f_im


def kernel(x_prompt, x_sample, state_s5_re, state_s5_im, c, c_ctx, w_ada, b_ada, w_in, hy_conv_w, hy_conv_b, hy_f_w1, hy_f_b1, hy_f_w2, hy_f_b2, hy_f_w3, hy_freq, hy_fbias, s5_a_re, s5_a_im, s5_log_dt, s5_b_re, s5_b_im, s5_c_re, s5_c_im, s5_d, s5_w_glu, s5_b_glu, out_norm_g, w_out, ln1_g, ln1_b, moe_w_r1, moe_b_r1, moe_w_r2, moe_b_r2, moe_w_gate, moe_w_up, moe_w_down, ln2_g, ln2_b):
    b_ctx, l_ctx, _ = x_prompt.shape
    b_lat, l_lat, _ = x_sample.shape
    g, p = S5_GROUPS, S5_STATE
    assert w_ada.shape[0] == 1, "single-layer trunk"
    l = 0

    nrow = 16
    cond = jnp.concatenate([c_ctx[None, :], c, jnp.zeros((nrow - 1 - b_lat, D_MODEL), F32)], axis=0)
    mod = _ada(cond, w_ada[l], b_ada[l])
    mod_ctx = mod[0:1].reshape(1, 1, 6 * D_MODEL)
    mod_lat = mod[1:1 + b_lat].reshape(b_lat, 1, 6 * D_MODEL)

    wr = jnp.concatenate([moe_w_r2[l].transpose(1, 0, 2).reshape(D_MODEL, N_EXPERTS), moe_w_r1[l]], axis=1)
    wr = jnp.pad(wr, ((0, 0), (0, LANES - wr.shape[1])))
    br = jnp.concatenate([moe_b_r2[l].reshape(-1), moe_b_r1[l]])
    br = jnp.pad(br, (0, LANES - br.shape[0])).reshape(1, LANES)
    wr_hi, wr_lo = _split(wr)

    wts = {
        'w_in': w_in[l], 'hy_conv_w': hy_conv_w[l], 'hy_conv_b': hy_conv_b[l],
        'hy_fbias': hy_fbias[l], 'w_glu': s5_w_glu[l].astype(BF16), 'b_glu': s5_b_glu[l].reshape(1, -1),
        'out_norm_g': out_norm_g[l].reshape(1, -1), 'w_out': w_out[l].astype(BF16),
        'ln1_g': ln1_g[l].reshape(1, -1), 'ln1_b': ln1_b[l].reshape(1, -1),
        'wr_hi': wr_hi, 'wr_lo': wr_lo, 'br': br,
        'w_gate': moe_w_gate[l], 'w_up': moe_w_up[l], 'w_down': moe_w_down[l],
        'ln2_g': ln2_g[l].reshape(1, -1), 'ln2_b': ln2_b[l].reshape(1, -1),
    }

    s5ops = _s5_operators(s5_a_re[l], s5_a_im[l], s5_log_dt[l], s5_b_re[l], s5_b_im[l],
                          s5_c_re[l], s5_c_im[l], s5_d[l])
    tabs_ctx = _tables(l_ctx)
    tabs_lat = _tables(l_lat)
    filt_args = (hy_f_w1[l], hy_f_b1[l], hy_f_w2[l], hy_f_b2[l], hy_f_w3[l], hy_freq[l])
    filt_ctx = _hyena_filters(l_ctx, tabs_ctx, *filt_args)
    filt_lat = _hyena_filters(l_lat, tabs_lat, *filt_args)

    zero = jnp.zeros((g, b_ctx, 2 * p), F32)
    yhy_c, ys5_c, f_re, f_im = _mixers(x_prompt, None, mod_ctx, zero, zero, tabs_ctx, filt_ctx, s5ops, wts, 1024)
    unpack = lambda f: f.reshape(g, b_ctx, 2, p).transpose(1, 2, 0, 3)[:, None]
    new_re, new_im = unpack(f_re), unpack(f_im)

    pack = lambda s: s[:, l].transpose(2, 0, 1, 3).reshape(g, b_lat, 2 * p)
    pos = _grid_pos_embed(l_lat)
    yhy_l, ys5_l, _, _ = _mixers(x_sample, pos, mod_lat, pack(state_s5_re), pack(state_s5_im),
                                 tabs_lat, filt_lat, s5ops, wts, 1024)

    n_ctx = b_ctx * l_ctx
    x1_all, h2_all, gates_all, tile_counts = _out_proj(
        x_prompt.reshape(n_ctx, D_MODEL), x_sample.reshape(b_lat * l_lat, D_MODEL), pos,
        yhy_c, yhy_l, ys5_c, ys5_l, mod, wts['w_glu'], wts['b_glu'], wts['out_norm_g'], wts['w_out'],
        wts['ln1_g'], wts['ln1_b'], wts['wr_hi'], wts['wr_lo'], wts['br'], MOE_ST)
    y_ctx, y_lat = _moe(h2_all, gates_all, tile_counts, x1_all, mod,
                        wts['w_gate'], wts['w_up'], wts['w_down'], wts['ln2_g'], wts['ln2_b'],
                        n_ctx, l_lat)
    return (y_ctx.reshape(x_prompt.shape), y_lat.reshape(x_sample.shape), new_re, new_im)
```

```python
import functools
import math

import numpy as np
import jax
import jax.numpy as jnp
from jax import lax
from jax.experimental import pallas as pl
from jax.experimental.pallas import tpu as pltpu

F32 = jnp.float32
BF16 = jnp.bfloat16

D_MODEL = 1024
DEPTH = 1
GRID_W = 64
POS_BASE = 10000.0
D_HY = 512
D_S5 = 512
S5_CH = 16
S5_GROUPS = 32
S5_STATE = 64
S5_CHUNK = 16
S5_ROW = S5_CHUNK * S5_CH
HY_BANDS = 16
HY_EMB = 1 + 2 * HY_BANDS
HY_HID = 64
HY_MIN_DECAY = math.log(1e-2) / 1.5
HY_MAX_DECAY = math.log(1e-2) / 0.3
N_EGROUPS = 4
N_EPG = 4
N_EXPERTS = 16
D_EXPERT = 512
LN_EPS = 1e-5
ALPHA = (2.0 * DEPTH) ** 0.25
LANES = 128
S5_GB = LANES // S5_CH
S5OPS_GB = 4
HY_CW = 512
MOE_ST = 512
MOE_SLOTS = 640
MOE_UNIT = 16
MOE_BIG = 64
MOE_TM = 512
VMEM_LIMIT = 60000 * 1024


def _cparams(sem):
    return pltpu.CompilerParams(dimension_semantics=sem, vmem_limit_bytes=VMEM_LIMIT)


def _split(x):
    hi = x.astype(BF16)
    lo = (x - hi.astype(F32)).astype(BF16)
    return hi, lo


def _dot(a, b):
    return jnp.dot(a, b, preferred_element_type=F32)


def _dot_t(a, b):
    return lax.dot_general(a, b, (((1,), (1,)), ((), ())), preferred_element_type=F32)


def _mm3(a, b):
    ah, al = _split(a)
    bh, bl = _split(b)
    return _dot(ah, bh) + _dot(al, bh) + _dot(ah, bl)


def _mm3_t(a, b):
    ah, al = _split(a)
    bh, bl = _split(b)
    return _dot_t(ah, bh) + _dot_t(al, bh) + _dot_t(ah, bl)


def _norm(x):
    xc = x - jnp.mean(x, axis=-1, keepdims=True)
    return xc * lax.rsqrt(jnp.mean(xc * xc, axis=-1, keepdims=True) + LN_EPS)


def _rms(y):
    return y * lax.rsqrt(jnp.mean(y * y, axis=-1, keepdims=True) + LN_EPS)


def _ada_kernel(cond_ref, w_ref, b_ref, o_ref):
    c = jax.nn.silu(cond_ref[...])
    o_ref[...] = _mm3(c, w_ref[...]) + b_ref[...]


def _ada(cond, w_ada, b_ada):
    nb = cond.shape[0]
    n = w_ada.shape[1]
    tn = 1024
    return pl.pallas_call(
        _ada_kernel,
        grid=(n // tn,),
        in_specs=[pl.BlockSpec((nb, D_MODEL), lambda j: (0, 0)),
                  pl.BlockSpec((D_MODEL, tn), lambda j: (0, j)),
                  pl.BlockSpec((1, tn), lambda j: (0, j))],
        out_specs=pl.BlockSpec((nb, tn), lambda j: (0, j)),
        out_shape=jax.ShapeDtypeStruct((nb, n), F32),
        compiler_params=_cparams(("arbitrary",)),
        name="ada",
    )(cond, w_ada, b_ada.reshape(1, n))


RADIX2_MIN_HALF = 256


def _dft_tables(n_half):
    n = 2 * n_half
    idx = np.arange(n_half, dtype=np.int64)
    m = (idx[:, None] * idx[None, :]) % n
    ang = 2.0 * np.pi * m.astype(np.float64) / n
    cm = np.cos(ang)
    sm = -np.sin(ang)
    sm[0, :] = 1.0 - 2.0 * (idx % 2)
    return cm, sm


def _dense_tables(n_tok):
    n_half = n_tok // 2
    f = np.concatenate([np.arange(n_half), n_tok - np.arange(n_half)]).astype(np.int64)
    sidx = np.arange(n_tok, dtype=np.int64)
    ang = 2.0 * np.pi * ((f[:, None] * sidx[None, :]) % (2 * n_tok)).astype(np.float64) / (2 * n_tok)
    cd = np.cos(ang)
    sd = -np.sin(ang)
    half = 2.0 * np.pi * ((n_half * sidx) % (2 * n_tok)).astype(np.float64) / (2 * n_tok)
    sd[0, :] = np.cos(half)
    sd[n_half, :] = -np.sin(half)
    return cd, sd


def _tables(n_tok):
    n_half = n_tok // 2
    bf = lambda t: jnp.asarray(t.astype(np.float32)).astype(BF16)
    c, s = _dense_tables(n_tok) if n_half < RADIX2_MIN_HALF else _dft_tables(n_half)
    stacks = (bf(np.concatenate([c, s], axis=0)), bf(np.concatenate([c.T, s.T], axis=1)))
    if n_half < RADIX2_MIN_HALF:
        return stacks
    ang = np.pi * np.arange(n_half, dtype=np.float64) / n_tok
    tw = [jnp.asarray(np.broadcast_to(v[:, None], (n_half, HY_CW)).astype(np.float32))
          for v in (np.cos(ang), -np.sin(ang))]
    return stacks + (tw[0], tw[1])


def _put_cols(ref, x):
    for j in range(ref.shape[0]):
        ref[j] = x[:, LANES * j:LANES * (j + 1)]


def _get_cols(ref):
    return jnp.concatenate([ref[j] for j in range(ref.shape[0])], axis=1)


def _get_parity(ref, parity):
    n_half = ref.shape[1] // 2
    return jnp.concatenate([ref[j, pl.ds(parity, n_half, stride=2), :] for j in range(ref.shape[0])], axis=1)


def _put_parity(ref, parity, x):
    n_half = ref.shape[1] // 2
    for j in range(ref.shape[0]):
        ref[j, pl.ds(parity, n_half, stride=2), :] = x[:, LANES * j:LANES * (j + 1)]


def _set_row0(x, v):
    first = lax.broadcasted_iota(jnp.int32, (8, x.shape[1]), 0) == 0
    return jnp.concatenate([jnp.where(first, v, x[:8]), x[8:]], axis=0)


def _rfft_packed(x_ref, tabs):
    n_tok = x_ref.shape[1]
    n_half = n_tok // 2
    if len(tabs) == 2:
        r = _dot(tabs[0], _get_cols(x_ref).astype(BF16))
        return r[:n_half], r[n_tok:n_tok + n_half], r[n_half:n_tok], r[n_tok + n_half:]
    fwd, _, tw_re, tw_im = tabs
    e = _dot(fwd, _get_parity(x_ref, 0).astype(BF16))
    o = _dot(fwd, _get_parity(x_ref, 1).astype(BF16))
    e_re, e_im, o_re, o_im = e[:n_half], e[n_half:], o[:n_half], o[n_half:]
    t_re = tw_re * o_re - tw_im * o_im
    t_im = tw_re * o_im + tw_im * o_re
    a_im = _set_row0(e_im + t_im, e_im[0:1])
    b_im = _set_row0(t_im - e_im, -o_im[0:1])
    return e_re + t_re, a_im, e_re - t_re, b_im


def _irfft_packed(y_ref, ya_re, ya_im, yb_re, yb_im, tabs):
    if len(tabs) == 2:
        y = jnp.concatenate([ya_re, yb_re, ya_im, yb_im], axis=0).astype(BF16)
        _put_cols(y_ref, _dot(tabs[1], y))
        return
    _, inv, tw_re, tw_im = tabs
    p_e = ya_re + yb_re
    q_e = _set_row0(ya_im - yb_im, ya_im[0:1])
    _put_parity(y_ref, 0, _dot(inv, jnp.concatenate([p_e, q_e], axis=0).astype(BF16)))
    ra_re = ya_re * tw_re + ya_im * tw_im
    ra_im = ya_im * tw_re - ya_re * tw_im
    rb_re = yb_im * tw_im - yb_re * tw_re
    rb_im = -(yb_re * tw_im + yb_im * tw_re)
    p_o = ra_re + rb_re
    q_o = _set_row0(ra_im - rb_im, -yb_im[0:1])
    _put_parity(y_ref, 1, _dot(inv, jnp.concatenate([p_o, q_o], axis=0).astype(BF16)))


def _filt_kernel(n_tok, n_tab, z_ref, t_ref, w1_ref, b1_ref, w2_ref, b2_ref, fr_ref, w3f_ref, w3b_ref,
                 dl_ref, *rest):
    tabs = tuple(r[...] for r in rest[:n_tab])
    kar_ref, kai_ref, kbr_ref, kbi_ref, p_ref, q_ref = rest[n_tab:]
    fr = fr_ref[...]
    h = jnp.sin(fr * (_mm3(z_ref[...], w1_ref[...]) + b1_ref[...]))
    h = jnp.sin(fr * (_mm3(h, w2_ref[...]) + b2_ref[...]))
    decay = jnp.exp(-t_ref[...] * dl_ref[...])
    row = lax.broadcasted_iota(jnp.int32, decay.shape, 0)
    hf = _mm3(h, w3f_ref[...]) * decay
    hb = jnp.where(row == 0, 0.0, _mm3(h, w3b_ref[...]) * decay)
    _put_cols(p_ref, hf + hb)
    _put_cols(q_ref, hf - hb)
    pa_re, pa_im, pb_re, _ = _rfft_packed(p_ref, tabs)
    _, qa_im, _, qb_im = _rfft_packed(q_ref, tabs)
    row0 = lax.broadcasted_iota(jnp.int32, pa_re.shape, 0) == 0
    inv_n = 1.0 / (2 * n_tok)
    w_re = jnp.where(row0, inv_n, 2.0 * inv_n)
    kar_ref[...] = w_re * pa_re
    kbr_ref[...] = w_re * pb_re
    kai_ref[...] = (2.0 * inv_n) * _set_row0(qa_im, pa_im[0:1])
    kbi_ref[...] = (2.0 * inv_n) * qb_im


def _hyena_filters(n_tok, tabs, hy_f_w1, hy_f_b1, hy_f_w2, hy_f_b2, hy_f_w3, hy_freq):
    n_half = n_tok // 2
    tt = np.linspace(0.0, 1.0, n_tok)[:, None]
    ang = (2.0 * np.pi * np.arange(n_tok) / n_tok)[:, None] * np.linspace(1e-4, HY_BANDS - 1, HY_BANDS)[None, :]
    z = np.concatenate([tt, np.cos(ang), -np.sin(ang), np.zeros((n_tok, LANES - HY_EMB))], axis=-1)
    z, t = jnp.asarray(z, F32), jnp.asarray(tt, F32)
    deltas = jnp.asarray(np.abs(np.linspace(HY_MIN_DECAY, HY_MAX_DECAY, D_HY))[None, :], F32)
    w1 = jnp.pad(hy_f_w1, ((0, LANES - HY_EMB), (0, 0)))
    ncb = D_HY // HY_CW
    full = lambda j: (0, 0)
    out_sd = jax.ShapeDtypeStruct((n_half, 2 * D_HY), F32)
    tab_specs = [pl.BlockSpec(t_.shape, full, pipeline_mode=pl.Buffered(1)) for t_ in tabs]
    return pl.pallas_call(
        functools.partial(_filt_kernel, n_tok, len(tabs)),
        grid=(2 * ncb,),
        in_specs=[pl.BlockSpec((n_tok, LANES), full),
                  pl.BlockSpec((n_tok, 1), full),
                  pl.BlockSpec((LANES, HY_HID), full),
                  pl.BlockSpec((1, HY_HID), full),
                  pl.BlockSpec((HY_HID, HY_HID), full),
                  pl.BlockSpec((1, HY_HID), full),
                  pl.BlockSpec((1, HY_HID), full),
                  pl.BlockSpec((HY_HID, HY_CW), lambda j: (0, 2 * ncb * (j // ncb) + j % ncb)),
                  pl.BlockSpec((HY_HID, HY_CW), lambda j: (0, 2 * ncb * (j // ncb) + ncb + j % ncb)),
                  pl.BlockSpec((1, HY_CW), lambda j: (0, j % ncb))] + tab_specs,
        out_specs=[pl.BlockSpec((n_half, HY_CW), lambda j: (0, j))] * 4,
        out_shape=[out_sd] * 4,
        scratch_shapes=[pltpu.VMEM((HY_CW // LANES, n_tok, LANES), F32)] * 2,
        compiler_params=_cparams(("arbitrary",)),
        name=f"filt{n_tok}",
    )(z, t, w1, hy_f_b1.reshape(1, -1), hy_f_w2, hy_f_b2.reshape(1, -1), hy_freq.reshape(1, -1),
      hy_f_w3, hy_f_w3, deltas, *tabs)


def _hyena_kernel(n_tab, pv_ref, p1_ref, p2_ref, cwv_ref, cw1_ref, cw2_ref, cbv_ref, cb1_ref, cb2_ref,
                  fbias_ref, *rest):
    tabs = tuple(r[...] for r in rest[:n_tab])
    (kar0_ref, kai0_ref, kbr0_ref, kbi0_ref, kar1_ref, kai1_ref, kbr1_ref, kbi1_ref,
     o_ref, u_ref, y_ref) = rest[n_tab:]
    n_tok = pv_ref.shape[1]
    row = lax.broadcasted_iota(jnp.int32, (n_tok, pv_ref.shape[2]), 0)

    def short_conv(p_ref, cw_ref, cb_ref):
        p = p_ref[0]
        prev = jnp.where(row == 0, 0.0, pltpu.roll(p, 1, axis=0))
        nxt = jnp.where(row == n_tok - 1, 0.0, pltpu.roll(p, n_tok - 1, axis=0))
        return cb_ref[...] + prev * cw_ref[0:1, :] + p * cw_ref[1:2, :] + nxt * cw_ref[2:3, :]

    def fftconv(u, kar_ref, kai_ref, kbr_ref, kbi_ref, skip):
        _put_cols(u_ref, u)
        ua_re, ua_im, ub_re, ub_im = _rfft_packed(u_ref, tabs)
        ka_re, ka_im, kb_re, kb_im = kar_ref[...], kai_ref[...], kbr_ref[...], kbi_ref[...]
        zero_row = jnp.zeros_like(ka_im[0:1])
        kaz = _set_row0(ka_im, zero_row)
        kbz = _set_row0(kb_im, zero_row)
        ya_re = ua_re * ka_re - ua_im * kaz
        yb_re = ub_re * kb_re - ub_im * kbz
        h_re = ua_im[0:1] * ka_im[0:1] - ub_im[0:1] * kb_im[0:1]
        h_im = ua_im[0:1] * kb_im[0:1] + ub_im[0:1] * ka_im[0:1]
        ya_im = _set_row0(ua_re * ka_im + ua_im * ka_re, h_re)
        yb_im = _set_row0(ub_re * kb_im + ub_im * kb_re, h_im)
        _irfft_packed(y_ref, ya_re, ya_im, yb_re, yb_im, tabs)
        return _get_cols(y_ref) + u * skip

    v = short_conv(pv_ref, cwv_ref, cbv_ref)
    x1 = short_conv(p1_ref, cw1_ref, cb1_ref)
    z = x1 * fftconv(v, kar0_ref, kai0_ref, kbr0_ref, kbi0_ref, fbias_ref[0:1, :])
    x2 = short_conv(p2_ref, cw2_ref, cb2_ref)
    o_ref[0] = x2 * fftconv(z, kar1_ref, kai1_ref, kbr1_ref, kbi1_ref, fbias_ref[1:2, :])


def _hyena(proj_hy, tabs, filt, hy_conv_w, hy_conv_b, hy_fbias):
    bsz, n_tok, _ = proj_hy.shape
    n_half = n_tok // 2
    ncb = D_HY // HY_CW
    cb = hy_conv_b.reshape(1, -1)
    tab_specs = [pl.BlockSpec(t.shape, lambda b, c: (0, 0), pipeline_mode=pl.Buffered(1)) for t in tabs]

    def pspec(k):
        return pl.BlockSpec((1, n_tok, HY_CW), lambda b, c: (b, 0, k * ncb + c))

    def cwspec(k):
        return pl.BlockSpec((3, HY_CW), lambda b, c: (0, k * ncb + c))

    def cbspec(k):
        return pl.BlockSpec((1, HY_CW), lambda b, c: (0, k * ncb + c))

    def fspec(o):
        mode = pl.Buffered(1) if ncb == 1 else None
        return pl.BlockSpec((n_half, HY_CW), lambda b, c: (0, o * ncb + c), pipeline_mode=mode)

    return pl.pallas_call(
        functools.partial(_hyena_kernel, len(tabs)),
        grid=(bsz, ncb),
        in_specs=[pspec(0), pspec(1), pspec(2), cwspec(0), cwspec(1), cwspec(2),
                  cbspec(0), cbspec(1), cbspec(2),
                  pl.BlockSpec((2, HY_CW), lambda b, c: (0, c))] + tab_specs + [fspec(0)] * 4 + [fspec(1)] * 4,
        out_specs=pl.BlockSpec((1, n_tok, HY_CW), lambda b, c: (b, 0, c)),
        out_shape=jax.ShapeDtypeStruct((bsz, n_tok, D_HY), F32),
        scratch_shapes=[pltpu.VMEM((HY_CW // LANES, n_tok, LANES), F32)] * 2,
        compiler_params=_cparams(("arbitrary", "arbitrary")),
        name=f"hyena{n_tok}",
    )(proj_hy, proj_hy, proj_hy, hy_conv_w, hy_conv_w, hy_conv_w, cb, cb, cb, hy_fbias,
      *tabs, *filt, *filt)


def _s5ops_kernel(*refs):
    for g in range(S5OPS_GB):
        _s5ops_group(g, *refs)


def _s5ops_group(g, are_ref, aim_ref, ldt_ref, btr_ref, bti_ref, cre_ref, cim_ref, d_ref,
                 opa_ref, opb_ref, g_ref, atr_ref, ati_ref, er_ref, ei_ref):
    a_re, a_im = are_ref[g], aim_ref[g]
    dt = jnp.exp(ldt_ref[g])
    mag = jnp.exp(a_re * dt)
    ab_re = mag * jnp.cos(a_im * dt)
    ab_im = mag * jnp.sin(a_im * dt)
    n_re, n_im = ab_re - 1.0, ab_im
    den = a_re * a_re + a_im * a_im
    q_re = (n_re * a_re + n_im * a_im) / den
    q_im = (n_im * a_re - n_re * a_im) / den
    bt_re, bt_im = btr_ref[g], bti_ref[g]
    bb_re = q_re * bt_re - q_im * bt_im
    bb_im = q_re * bt_im + q_im * bt_re
    c_re, c_im = cre_ref[g, 0:S5_CH, :], cim_ref[g, 0:S5_CH, :]
    pw = [(jnp.ones_like(ab_re), jnp.zeros_like(ab_re))]
    for _ in range(S5_CHUNK):
        pr, pi = pw[-1]
        pw.append((pr * ab_re - pi * ab_im, pr * ab_im + pi * ab_re))
    lane = lax.broadcasted_iota(jnp.int32, ab_re.shape, 1)
    fwd = lane < S5_STATE
    for s in range(S5_CHUNK):
        e_re = jnp.where(fwd, pw[S5_CHUNK - 1 - s][0], pw[s][0])
        e_im = jnp.where(fwd, pw[S5_CHUNK - 1 - s][1], pw[s][1])
        er_ref[g, pl.ds(S5_CH * s, S5_CH), :] = e_re * bb_re - e_im * bb_im
        ei_ref[g, pl.ds(S5_CH * s, S5_CH), :] = e_re * bb_im + e_im * bb_re
        g_re = jnp.where(fwd, pw[s + 1][0], pw[S5_CHUNK - s][0])
        g_im = jnp.where(fwd, pw[s + 1][1], pw[S5_CHUNK - s][1])
        g_ref[g, pl.ds(S5_CH * s, S5_CH), 0:LANES] = (c_re * g_re - c_im * g_im).astype(BF16)
        g_ref[g, pl.ds(S5_CH * s, S5_CH), LANES:2 * LANES] = (-(c_re * g_im + c_im * g_re)).astype(BF16)
    atr_ref[g] = pw[S5_CHUNK][0]
    ati_ref[g] = pw[S5_CHUNK][1]
    er, ei = er_ref[g], ei_ref[g]
    opa_ref[g, :, 2 * LANES:3 * LANES], opb_ref[g, :, 0:LANES] = _split(er)
    opa_ref[g, :, 3 * LANES:4 * LANES], opb_ref[g, :, LANES:2 * LANES] = _split(ei)
    lane2 = lax.broadcasted_iota(jnp.int32, er.shape, 1)
    row2 = lax.broadcasted_iota(jnp.int32, er.shape, 0)
    f2 = lane2 < S5_STATE
    zero = jnp.zeros_like(er)

    cp_re, cp_im = cre_ref[g], cim_ref[g]
    kf = _mm3_t(jnp.where(f2, er, zero), cp_re) - _mm3_t(jnp.where(f2, ei, zero), cp_im)
    kb = _mm3_t(jnp.where(f2, zero, er), cp_re) - _mm3_t(jnp.where(f2, zero, ei), cp_im)
    d_row = d_ref[g]
    steps_per_vreg = LANES // S5_CH
    for half in range(S5_CHUNK // steps_per_vreg):
        acc = zero
        for tt in range(steps_per_vreg):
            t = half * steps_per_vreg + tt
            nf = S5_CH * (S5_CHUNK - 1 - t)
            nb = S5_CH * t
            col_f = jnp.concatenate([kf[nf:], zero[:nf]], axis=0) if nf else kf
            col_b = jnp.concatenate([zero[:nb], kb[:S5_ROW - nb]], axis=0) if nb else kb
            diag = jnp.where((row2 // S5_CH == t) & (row2 % S5_CH == lane2), d_row, 0.0)
            col = col_f + col_b + diag
            r = pltpu.roll(col, S5_CH * tt, axis=1) if tt else col
            acc = jnp.where((lane2 >= S5_CH * tt) & (lane2 < S5_CH * (tt + 1)), r, acc)
        opa_ref[g, :, LANES * half:LANES * (half + 1)] = acc.astype(BF16)


def _s5_operators(s5_a_re, s5_a_im, s5_log_dt, s5_b_re, s5_b_im, s5_c_re, s5_c_im, s5_d):
    g, p, h = S5_GROUPS, S5_STATE, S5_CH
    cat = lambda x: jnp.concatenate([x[0], x[1]], axis=-1)
    a_re = cat(s5_a_re).reshape(g, 1, 2 * p)
    a_im = cat(s5_a_im).reshape(g, 1, 2 * p)
    ldt = cat(jnp.broadcast_to(s5_log_dt[:, :, None], (2, g, p))).reshape(g, 1, 2 * p)
    bt_re = cat(jnp.swapaxes(s5_b_re, -1, -2))
    bt_im = cat(jnp.swapaxes(s5_b_im, -1, -2))
    cpad = lambda c: jnp.pad(jnp.concatenate([c, c], axis=-1), ((0, 0), (0, LANES - h), (0, 0)))
    c_re, c_im = cpad(s5_c_re), cpad(s5_c_im)
    d_row = jnp.pad(s5_d.reshape(g, 1, h), ((0, 0), (0, 0), (0, LANES - h)))
    vec = pl.BlockSpec((S5OPS_GB, 1, 2 * p), lambda i: (i, 0, 0))
    hp = pl.BlockSpec((S5OPS_GB, h, 2 * p), lambda i: (i, 0, 0))
    sq = pl.BlockSpec((S5OPS_GB, LANES, 2 * p), lambda i: (i, 0, 0))
    wide = lambda n: pl.BlockSpec((S5OPS_GB, S5_ROW, n), lambda i: (i, 0, 0))
    vec_sd = jax.ShapeDtypeStruct((g, 1, 2 * p), F32)
    return pl.pallas_call(
        _s5ops_kernel,
        grid=(g // S5OPS_GB,),
        in_specs=[vec, vec, vec, hp, hp, sq, sq, vec],
        out_specs=[wide(2 * S5_ROW), wide(S5_ROW), wide(S5_ROW), vec, vec],
        out_shape=[jax.ShapeDtypeStruct((g, S5_ROW, 2 * S5_ROW), BF16),
                   jax.ShapeDtypeStruct((g, S5_ROW, S5_ROW), BF16),
                   jax.ShapeDtypeStruct((g, S5_ROW, S5_ROW), BF16), vec_sd, vec_sd],
        scratch_shapes=[pltpu.VMEM((S5OPS_GB, S5_ROW, 2 * p), F32)] * 2,
        compiler_params=_cparams(("arbitrary",)),
        name="s5ops",
    )(a_re, a_im, ldt, bt_re, bt_im, c_re, c_im, d_row)


def _block_transpose(xs):
    n = len(xs)
    lane = lax.broadcasted_iota(jnp.int32, xs[0].shape, 1)
    xs = list(xs)
    d = n // 2
    while d:
        keep = ((lane // S5_CH) & d) == 0
        for i in range(n):
            if i & d:
                continue
            lo, hi = xs[i], xs[i + d]
            xs[i] = jnp.where(keep, lo, pltpu.roll(hi, S5_CH * d, axis=1))
            xs[i + d] = jnp.where(keep, pltpu.roll(lo, LANES - S5_CH * d, axis=1), hi)
        d //= 2
    return xs


def _s5_kernel(bsz, n_chunks, u_ref, opa_ref, opb_ref, g_ref,
               atr_ref, ati_ref, h0r_ref, h0i_ref, y_ref, fr_ref, fi_ref,
               ua_ref, ub_ref, ya_ref, yb_ref, sr_ref, si_ref, xfr_ref, xfi_ref, xbr_ref, xbi_ref):
    nc = n_chunks
    spv = LANES // S5_CH
    rsub = min(nc, 32)

    def to_chunks(b, carry):
        for half, dst in ((0, ua_ref), (1, ub_ref)):
            for r0 in range(0, nc, rsub):
                xs = [u_ref[b, pl.ds(S5_CHUNK * r0 + half * spv + tt, rsub, stride=S5_CHUNK), :]
                      for tt in range(spv)]
                for k, blk in enumerate(_block_transpose(xs)):
                    dst[k, pl.ds(r0 * bsz + b, rsub, stride=bsz), :] = blk
        return carry

    lax.fori_loop(0, bsz, to_chunks, 0, unroll=2)

    lane = lax.broadcasted_iota(jnp.int32, (bsz, 2 * S5_STATE), 1)
    fwd = lane < S5_STATE
    lane_all = lax.broadcasted_iota(jnp.int32, (bsz * nc, 2 * S5_STATE), 1)
    fwd_all = lane_all < S5_STATE

    def group(k, slot):
        u = jnp.concatenate([ua_ref[k], ub_ref[k]], axis=1)
        uh, ul = _split(u)
        wide = _dot(uh, opa_ref[k])
        inj = (wide[:, S5_ROW:] + _dot(ul, opa_ref[k, :, S5_ROW:2 * S5_ROW]) + _dot(uh, opb_ref[k]))
        sr_ref[slot] = inj[:, :LANES]
        si_ref[slot] = inj[:, LANES:]
        at_re, at_im = atr_ref[k], ati_ref[k]
        y_intra = wide[:, :S5_ROW]

        def step(i, xc):
            x_re, x_im = xc
            rf = pl.ds(pl.multiple_of(i * bsz, bsz), bsz)
            rb = pl.ds(pl.multiple_of((nc - 1 - i) * bsz, bsz), bsz)
            xfr_ref[slot, rf, :] = x_re
            xfi_ref[slot, rf, :] = x_im
            xbr_ref[slot, rb, :] = x_re
            xbi_ref[slot, rb, :] = x_im
            s_re = jnp.where(fwd, sr_ref[slot, rf, :], sr_ref[slot, rb, :])
            s_im = jnp.where(fwd, si_ref[slot, rf, :], si_ref[slot, rb, :])
            return (at_re * x_re - at_im * x_im + s_re, at_re * x_im + at_im * x_re + s_im)

        x_re, x_im = lax.fori_loop(0, nc, step, (h0r_ref[k], h0i_ref[k]), unroll=True)
        fr_ref[k] = x_re
        fi_ref[k] = x_im
        xp = jnp.concatenate([jnp.where(fwd_all, xfr_ref[slot], xbr_ref[slot]),
                              jnp.where(fwd_all, xfi_ref[slot], xbi_ref[slot])], axis=1).astype(BF16)
        y = y_intra + _dot_t(xp, g_ref[k])
        ya_ref[k] = y[:, :LANES]
        yb_ref[k] = y[:, LANES:]

    def group_pair(j, carry):
        group(2 * j, 0)
        group(2 * j + 1, 1)
        return carry

    lax.fori_loop(0, S5_GB // 2, group_pair, 0)

    def to_tokens(b, carry):
        for half, src in ((0, ya_ref), (1, yb_ref)):
            for r0 in range(0, nc, rsub):
                ys = [src[k, pl.ds(r0 * bsz + b, rsub, stride=bsz), :] for k in range(S5_GB)]
                for tt, blk in enumerate(_block_transpose(ys)):
                    y_ref[b, pl.ds(S5_CHUNK * r0 + half * spv + tt, rsub, stride=S5_CHUNK), :] = blk
        return carry

    lax.fori_loop(0, bsz, to_tokens, 0, unroll=2)


def _s5(u, ops, h0_re, h0_im):
    bsz, n_tok, _ = u.shape
    g, p = S5_GROUPS, S5_STATE
    nc = n_tok // S5_CHUNK
    rows = nc * bsz
    tok = pl.BlockSpec((bsz, n_tok, LANES), lambda j: (0, 0, j))
    gspec = lambda shape: pl.BlockSpec((S5_GB,) + shape, lambda j: (j, 0, 0))
    return pl.pallas_call(
        functools.partial(_s5_kernel, bsz, nc),
        grid=(g // S5_GB,),
        in_specs=[tok, gspec((S5_ROW, 2 * S5_ROW)), gspec((S5_ROW, S5_ROW)), gspec((S5_ROW, S5_ROW)),
                  gspec((1, 2 * p)), gspec((1, 2 * p)), gspec((bsz, 2 * p)), gspec((bsz, 2 * p))],
        out_specs=[tok, gspec((bsz, 2 * p)), gspec((bsz, 2 * p))],
        out_shape=[jax.ShapeDtypeStruct((bsz, n_tok, D_S5), F32),
                   jax.ShapeDtypeStruct((g, bsz, 2 * p), F32),
                   jax.ShapeDtypeStruct((g, bsz, 2 * p), F32)],
        scratch_shapes=([pltpu.VMEM((S5_GB, rows, LANES), F32)] * 4
                        + [pltpu.VMEM((2, rows, 2 * p), F32)] * 6),
        compiler_params=_cparams(("arbitrary",)),
        name=f"s5_{n_tok}",
    )(u, *ops, h0_re, h0_im)


def _in_kernel(has_pos, *refs):
    if has_pos:
        x_ref, pos_ref, mod_ref, w_ref, hy_ref, s5_ref = refs
        x = x_ref[0] + pos_ref[...]
    else:
        x_ref, mod_ref, w_ref, hy_ref, s5_ref = refs
        x = x_ref[0]
    sh1 = mod_ref[0, :, 0:D_MODEL]
    sc1 = mod_ref[0, :, D_MODEL:2 * D_MODEL]
    h = _norm(x) * (1.0 + sc1) + sh1
    proj = _dot(h.astype(BF16), w_ref[...].astype(BF16))
    hy_ref[0] = proj[:, :3 * D_HY]
    s5_ref[0] = proj[:, 3 * D_HY:]


def _in_proj(x3, pos, mod3, w_in, tm):
    nb, lt, _ = x3.shape
    has_pos = pos is not None
    per_batch = mod3.shape[0] > 1
    midx = (lambda b, i: (b, 0, 0)) if per_batch else (lambda b, i: (0, 0, 0))
    in_specs = [pl.BlockSpec((1, tm, D_MODEL), lambda b, i: (b, i, 0))]
    args = [x3]
    if has_pos:
        in_specs.append(pl.BlockSpec((tm, D_MODEL), lambda b, i: (i, 0)))
        args.append(pos)
    in_specs += [pl.BlockSpec((1, 1, 6 * D_MODEL), midx),
                 pl.BlockSpec((D_MODEL, 3 * D_HY + D_S5), lambda b, i: (0, 0), pipeline_mode=pl.Buffered(1))]
    args += [mod3, w_in]
    return pl.pallas_call(
        functools.partial(_in_kernel, has_pos),
        grid=(nb, lt // tm),
        in_specs=in_specs,
        out_specs=[pl.BlockSpec((1, tm, 3 * D_HY), lambda b, i: (b, i, 0)),
                   pl.BlockSpec((1, tm, D_S5), lambda b, i: (b, i, 0))],
        out_shape=[jax.ShapeDtypeStruct((nb, lt, 3 * D_HY), F32),
                   jax.ShapeDtypeStruct((nb, lt, D_S5), F32)],
        compiler_params=_cparams(("arbitrary", "arbitrary")),
        name=f"in_proj{nb}",
    )(*args)


def _route(logits):
    lane = lax.broadcasted_iota(jnp.int32, logits.shape, 1)
    lane_f = lane.astype(F32)
    neg = -jnp.inf
    big = float(LANES)
    m1 = (lane >= N_EXPERTS) & (lane < N_EXPERTS + N_EGROUPS)
    l1 = jnp.where(m1, logits, neg)
    top1 = jnp.max(l1, axis=-1, keepdims=True)
    grp = jnp.min(jnp.where(l1 == top1, lane_f, big), axis=-1, keepdims=True) - float(N_EXPERTS)
    den = jnp.sum(jnp.where(m1, jnp.exp(logits - top1), 0.0), axis=-1, keepdims=True)
    p_grp = 1.0 / den
    lo = grp * float(N_EPG)
    m2 = (lane_f >= lo) & (lane_f < lo + float(N_EPG))
    l2 = jnp.where(m2, logits, neg)
    v1 = jnp.max(l2, axis=-1, keepdims=True)
    i1 = jnp.min(jnp.where(l2 == v1, lane_f, big), axis=-1, keepdims=True)
    l2b = jnp.where(lane_f == i1, neg, l2)
    v2 = jnp.max(l2b, axis=-1, keepdims=True)
    i2 = jnp.min(jnp.where(l2b == v2, lane_f, big), axis=-1, keepdims=True)
    e = jnp.exp(v2 - v1)
    w1 = 1.0 / (1.0 + e)
    w2 = e / (1.0 + e)
    gates = jnp.where(lane_f == i1, w1 * p_grp, 0.0) + jnp.where(lane_f == i2, w2 * p_grp, 0.0)
    return jnp.where(lane_f == grp + float(N_EXPERTS), 1.0, gates)


def _out_kernel(n_ctx_blocks, xc_ref, xl_ref, pos_ref, yhyc_ref, yhyl_ref, ys5c_ref, ys5l_ref, mod_ref,
                wglu_ref, bglu_ref, ong_ref, wout_ref, ln1g_ref, ln1b_ref, wrh_ref, wrl_ref, br_ref,
                x1_ref, h2_ref, gate_ref, cnt_ref):
    is_ctx = pl.program_id(0) < n_ctx_blocks
    x = jnp.where(is_ctx, xc_ref[...], xl_ref[...] + pos_ref[...])
    y = jnp.where(is_ctx, ys5c_ref[...], ys5l_ref[...])
    y_hy = jnp.where(is_ctx, yhyc_ref[...], yhyl_ref[...])
    s5 = jax.nn.gelu(y) * jax.nn.sigmoid(_dot(y.astype(BF16), wglu_ref[...]) + bglu_ref[...])
    m_hy = _rms(y_hy) * ong_ref[:, 0:D_HY]
    m_s5 = _rms(s5) * ong_ref[:, D_HY:]
    o = (_dot(m_hy.astype(BF16), wout_ref[0:D_HY, :]) + _dot(m_s5.astype(BF16), wout_ref[D_HY:, :]))
    g1 = mod_ref[0, :, 2 * D_MODEL:3 * D_MODEL]
    sh2 = mod_ref[0, :, 3 * D_MODEL:4 * D_MODEL]
    sc2 = mod_ref[0, :, 4 * D_MODEL:5 * D_MODEL]
    x1 = _norm(ALPHA * x + g1 * o) * ln1g_ref[...] + ln1b_ref[...]
    x1_ref[...] = x1
    h2 = _norm(x1) * (1.0 + sc2) + sh2
    h2_ref[...] = h2.astype(BF16)
    hh, hl = _split(h2)
    logits = (_dot(hh, wrh_ref[...]) + _dot(hl, wrh_ref[...]) + _dot(hh, wrl_ref[...]) + br_ref[...])
    gates = _route(logits)
    gate_ref[...] = gates
    cnt_ref[0] = jnp.sum(gates, axis=0, keepdims=True)


def _out_proj(xc, xl, pos, yhy_c, yhy_l, ys5_c, ys5_l, mod, wglu_bf, bglu, ong, wout_bf, ln1g, ln1b,
              wr_hi, wr_lo, br, tm):
    n_ctx, n_lat = xc.shape[0], xl.shape[0]
    l_lat = pos.shape[0]
    ncb, nlb, npb = n_ctx // tm, n_lat // tm, l_lat // tm
    ctx = lambda w: pl.BlockSpec((tm, w), lambda i: (jnp.minimum(i, ncb - 1), 0))
    lat = lambda w: pl.BlockSpec((tm, w), lambda i: (jnp.maximum(i - ncb, 0), 0))
    full = lambda shape: pl.BlockSpec(shape, lambda i: (0,) * len(shape))
    out = lambda w: pl.BlockSpec((tm, w), lambda i: (i, 0))
    mod_idx = lambda i: (jnp.where(i < ncb, 0, 1 + jnp.maximum(i - ncb, 0) // npb), 0, 0)
    n_all = n_ctx + n_lat
    return pl.pallas_call(
        functools.partial(_out_kernel, ncb),
        grid=(ncb + nlb,),
        in_specs=[ctx(D_MODEL), lat(D_MODEL),
                  pl.BlockSpec((tm, D_MODEL), lambda i: (jnp.maximum(i - ncb, 0) % npb, 0)),
                  ctx(D_HY), lat(D_HY), ctx(D_S5), lat(D_S5),
                  pl.BlockSpec((1, 1, 6 * D_MODEL), mod_idx),
                  full((D_S5, D_S5)), full((1, D_S5)), full((1, D_MODEL)), full((D_MODEL, D_MODEL)),
                  full((1, D_MODEL)), full((1, D_MODEL)), full((D_MODEL, LANES)), full((D_MODEL, LANES)),
                  full((1, LANES))],
        out_specs=[out(D_MODEL), out(D_MODEL), out(LANES), pl.BlockSpec((1, 1, LANES), lambda i: (i, 0, 0))],
        out_shape=[jax.ShapeDtypeStruct((n_all, D_MODEL), F32),
                   jax.ShapeDtypeStruct((n_all, D_MODEL), BF16),
                   jax.ShapeDtypeStruct((n_all, LANES), F32),
                   jax.ShapeDtypeStruct((n_all // tm, 1, LANES), F32)],
        compiler_params=_cparams(("arbitrary",)),
        name="out_proj",
    )(xc, xl, pos, yhy_c, yhy_l, ys5_c, ys5_l, mod.reshape(mod.shape[0], 1, 6 * D_MODEL),
      wglu_bf, bglu, ong, wout_bf, ln1g, ln1b, wr_hi, wr_lo, br)


def _perm_t(gates, loc_ref, s):
    n = gates.shape[0]
    lane = lax.broadcasted_iota(jnp.int32, gates.shape, 1)
    oh = jnp.where((lane >= N_EXPERTS) & (lane < N_EXPERTS + N_EGROUPS), gates, 0.0)
    r = lax.broadcasted_iota(jnp.int32, (n, n), 0)
    c = lax.broadcasted_iota(jnp.int32, (n, n), 1)
    earlier = jnp.where(c < r, 1.0, 0.0).astype(BF16)
    cum = _dot(earlier, oh.astype(BF16))
    rank = jnp.sum(cum * oh, axis=-1, keepdims=True)
    lane1 = lax.broadcasted_iota(jnp.int32, (1, LANES), 1)
    locv = jnp.zeros((1, LANES), F32)
    for grp in range(N_EGROUPS):
        locv = jnp.where(lane1 == N_EXPERTS + grp, loc_ref[N_EGROUPS * s + grp].astype(F32), locv)
    dest = rank + jnp.sum(oh * locv, axis=-1, keepdims=True)
    slot = lax.broadcasted_iota(jnp.int32, (n, MOE_SLOTS), 1).astype(F32)
    return jnp.where(slot == dest, 1.0, 0.0)


def _segment_copies(s, loc_ref, len_ref, off_ref, make):
    for grp in range(N_EGROUPS):
        loc = loc_ref[N_EGROUPS * s + grp]
        off = off_ref[N_EGROUPS * s + grp]
        length = len_ref[N_EGROUPS * s + grp]
        n_big = length // MOE_BIG
        done = n_big * MOE_BIG

        def big(i, carry):
            make(pl.multiple_of(loc + MOE_BIG * i, MOE_UNIT), pl.multiple_of(off + MOE_BIG * i, MOE_UNIT), MOE_BIG,
                 grp % 2)
            return carry

        def unit(i, carry):
            make(pl.multiple_of(loc + done + MOE_UNIT * i, MOE_UNIT),
                 pl.multiple_of(off + done + MOE_UNIT * i, MOE_UNIT), MOE_UNIT, grp % 2)
            return carry

        lax.fori_loop(0, n_big, big, 0)
        lax.fori_loop(0, (length - done) // MOE_UNIT, unit, 0)


def _pad_copies(pad_ref, n_blocks, zx_v, zg_v, xs_hbm, gs_hbm, sem, op):
    def unit(row, rows):
        getattr(pltpu.make_async_copy(zx_v.at[pl.ds(0, rows), :], xs_hbm.at[pl.ds(row, rows), :], sem.at[0]), op)()
        getattr(pltpu.make_async_copy(zg_v.at[pl.ds(0, rows), :], gs_hbm.at[pl.ds(row, rows), :], sem.at[1]), op)()

    for grp in range(N_EGROUPS):
        start = pad_ref[grp]

        def body(i, carry):
            unit(pl.multiple_of(start + MOE_UNIT * i, MOE_UNIT), MOE_UNIT)
            return carry

        lax.fori_loop(0, pad_ref[N_EGROUPS + grp], body, 0)

    def tail(b, carry):
        unit(pl.multiple_of(b * MOE_TM, MOE_TM), MOE_TM)
        return carry

    lax.fori_loop(pad_ref[2 * N_EGROUPS], n_blocks, tail, 0)


def _moe_sort_kernel(n_blocks, loc_ref, len_ref, off_ref, pad_ref, h_ref, gate_ref, xs_hbm, gs_hbm,
                     xs_v, gs_v, zx_v, zg_v, sem, zsem):
    s = pl.program_id(0)
    slot = s % 2

    @pl.when(s == 0)
    def _():
        zx_v[...] = jnp.zeros_like(zx_v)
        zg_v[...] = jnp.zeros_like(zg_v)
        _pad_copies(pad_ref, n_blocks, zx_v, zg_v, xs_hbm, gs_hbm, zsem, 'start')

    gates = gate_ref[...]
    p = _perm_t(gates, loc_ref, s).T.astype(BF16)
    xs_v[slot] = _dot(p, h_ref[...]).astype(BF16)
    g_hi = gates.astype(BF16)
    r1 = gates - g_hi.astype(F32)
    g_mid = r1.astype(BF16)
    g_lo = (r1 - g_mid.astype(F32)).astype(BF16)
    parts = _dot(p, jnp.concatenate([g_hi, g_mid, g_lo], axis=1))
    gs_v[slot] = parts[:, :LANES] + parts[:, LANES:2 * LANES] + parts[:, 2 * LANES:]

    def copies(buf):
        def x_copy(lr, gr, rows):
            return pltpu.make_async_copy(xs_v.at[buf, pl.ds(lr, rows), :],
                                         xs_hbm.at[pl.ds(gr, rows), :], sem.at[0, buf])

        def g_copy(lr, gr, rows):
            return pltpu.make_async_copy(gs_v.at[buf, pl.ds(lr, rows), :],
                                         gs_hbm.at[pl.ds(gr, rows), :], sem.at[1, buf])

        def start(lr, gr, rows, prio):
            x_copy(lr, gr, rows).start(priority=prio)
            g_copy(lr, gr, rows).start(priority=1 - prio)

        def wait(lr, gr, rows, prio):
            x_copy(lr, gr, rows).wait()
            g_copy(lr, gr, rows).wait()

        return start, wait

    _segment_copies(s, loc_ref, len_ref, off_ref, copies(slot)[0])

    @pl.when(s > 0)
    def _():
        _segment_copies(s - 1, loc_ref, len_ref, off_ref, copies(1 - slot)[1])

    @pl.when(s == pl.num_programs(0) - 1)
    def _():
        _segment_copies(s, loc_ref, len_ref, off_ref, copies(slot)[1])
        _pad_copies(pad_ref, n_blocks, zx_v, zg_v, xs_hbm, gs_hbm, zsem, 'wait')


def _moe_expert_kernel(bg_ref, nb_ref, xs_ref, gs_ref, wg_ref, wu_ref, wd_ref, o_ref):
    i = pl.program_id(0)

    @pl.when(i < nb_ref[0])
    def _():
        grp = bg_ref[i]
        x = xs_ref[...]
        gates = gs_ref[...]
        lane = lax.broadcasted_iota(jnp.int32, gates.shape, 1)
        acc = jnp.zeros(o_ref.shape, F32)
        for e in range(N_EPG):
            a = _dot(x, wg_ref[e].astype(BF16))
            u = _dot(x, wu_ref[e].astype(BF16))
            ge = jnp.sum(jnp.where(lane == N_EPG * grp + e, gates, 0.0), axis=-1, keepdims=True)
            hid = jax.nn.silu(a) * u * ge
            acc = acc + _dot(hid.astype(BF16), wd_ref[e].astype(BF16))
        o_ref[...] = acc.astype(BF16)

    @pl.when(i >= nb_ref[0])
    def _():
        o_ref[...] = jnp.zeros_like(o_ref)


def _moe_combine_kernel(loc_ref, len_ref, off_ref, gate_ref, x1_ref, mod_ref, ln2g_ref, ln2b_ref, o_hbm,
                        ctx_ref, lat_ref, o_v, sem, *, n_ctx_tiles):
    s = pl.program_id(0)
    slot = s % 2

    def copies(buf):
        def o_copy(lr, gr, rows):
            return pltpu.make_async_copy(o_hbm.at[pl.ds(gr, rows), :],
                                         o_v.at[buf, pl.ds(lr, rows), :], sem.at[buf])

        return ((lambda lr, gr, rows, prio: o_copy(lr, gr, rows).start(priority=prio)),
                (lambda lr, gr, rows, prio: o_copy(lr, gr, rows).wait()))

    @pl.when(s == 0)
    def _():
        o_v[...] = jnp.zeros_like(o_v)
        _segment_copies(s, loc_ref, len_ref, off_ref, copies(slot)[0])

    @pl.when(s + 1 < pl.num_programs(0))
    def _():
        _segment_copies(s + 1, loc_ref, len_ref, off_ref, copies(1 - slot)[0])

    pt = _perm_t(gate_ref[...], loc_ref, s).astype(BF16)
    _segment_copies(s, loc_ref, len_ref, off_ref, copies(slot)[1])
    f = _dot(pt, o_v[slot])
    g2 = mod_ref[0, :, 5 * D_MODEL:6 * D_MODEL]
    x2 = _norm(ALPHA * x1_ref[...] + g2 * f) * ln2g_ref[...] + ln2b_ref[...]

    @pl.when(s < n_ctx_tiles)
    def _():
        ctx_ref[...] = x2

    @pl.when(s >= n_ctx_tiles)
    def _():
        lat_ref[...] = x2


def _moe_plan(tile_counts, n_blocks):
    cnt = tile_counts[:, 0, N_EXPERTS:N_EXPERTS + N_EGROUPS].astype(jnp.int32)
    len16 = ((cnt + MOE_UNIT - 1) // MOE_UNIT) * MOE_UNIT
    loc = jnp.cumsum(len16, axis=1) - len16
    rows_g = jnp.sum(len16, axis=0)
    reg_g = ((rows_g + MOE_TM - 1) // MOE_TM) * MOE_TM
    reg_start = jnp.cumsum(reg_g) - reg_g
    off = reg_start[None, :] + jnp.cumsum(len16, axis=0) - len16
    blk_end = jnp.cumsum(reg_g // MOE_TM)
    bi = jnp.arange(n_blocks, dtype=jnp.int32)
    blk_group = jnp.minimum(jnp.sum((bi[:, None] >= blk_end[None, :]).astype(jnp.int32), axis=1),
                            N_EGROUPS - 1)
    flat = lambda a: a.reshape(-1).astype(jnp.int32)
    pads = jnp.concatenate([reg_start + rows_g, (reg_g - rows_g) // MOE_UNIT, blk_end[-1:]])
    return (flat(loc), flat(len16), flat(off), flat(pads), blk_group.astype(jnp.int32),
            blk_end[-1:].astype(jnp.int32))


def _moe(h2_all, gates_all, tile_counts, x1_all, mod, w_gate, w_up, w_down, ln2g, ln2b, n_ctx,
         tokens_per_mod_row):
    n_tok = h2_all.shape[0]
    n_tiles = n_tok // MOE_ST
    n_ctx_tiles = n_ctx // MOE_ST
    max_rows = n_tok + n_tiles * N_EGROUPS * (MOE_UNIT - 1) + N_EGROUPS * (MOE_TM - 1)
    n_blocks = -(-max_rows // MOE_TM)
    n_rows = n_blocks * MOE_TM
    loc, len16, off, pads, blk_group, n_used = _moe_plan(tile_counts, n_blocks)

    tile = lambda w: pl.BlockSpec((MOE_ST, w), lambda s, *_: (s, 0))
    anyspec = pl.BlockSpec(memory_space=pl.ANY)
    xs, gs = pl.pallas_call(
        functools.partial(_moe_sort_kernel, n_blocks),
        grid_spec=pltpu.PrefetchScalarGridSpec(
            num_scalar_prefetch=4, grid=(n_tiles,),
            in_specs=[tile(D_MODEL), tile(LANES)],
            out_specs=[anyspec, anyspec],
            scratch_shapes=[pltpu.VMEM((2, MOE_SLOTS, D_MODEL), BF16), pltpu.VMEM((2, MOE_SLOTS, LANES), F32),
                            pltpu.VMEM((MOE_TM, D_MODEL), BF16), pltpu.VMEM((MOE_TM, LANES), F32),
                            pltpu.SemaphoreType.DMA((2, 2)), pltpu.SemaphoreType.DMA((2,))]),
        out_shape=[jax.ShapeDtypeStruct((n_rows, D_MODEL), BF16),
                   jax.ShapeDtypeStruct((n_rows, LANES), F32)],
        compiler_params=_cparams(("arbitrary",)),
        name="moe_sort",
    )(loc, len16, off, pads, h2_all, gates_all)

    blk = lambda w: pl.BlockSpec((MOE_TM, w), lambda i, bg, nb: (jnp.minimum(i, nb[0] - 1), 0))
    wspec = lambda a, b, mode: pl.BlockSpec((N_EPG, a, b), lambda i, bg, nb: (bg[i], 0, 0),
                                            pipeline_mode=mode)
    o_sorted = pl.pallas_call(
        _moe_expert_kernel,
        grid_spec=pltpu.PrefetchScalarGridSpec(
            num_scalar_prefetch=2, grid=(n_blocks,),
            in_specs=[blk(D_MODEL), blk(LANES), wspec(D_MODEL, D_EXPERT, None),
                      wspec(D_MODEL, D_EXPERT, None), wspec(D_EXPERT, D_MODEL, None)],
            out_specs=pl.BlockSpec((MOE_TM, D_MODEL), lambda i, bg, nb: (i, 0))),
        out_shape=jax.ShapeDtypeStruct((n_rows, D_MODEL), BF16),
        compiler_params=_cparams(("arbitrary",)),
        name="moe_experts",
    )(blk_group, n_used, xs, gs, w_gate, w_up, w_down)

    lat_per_row = tokens_per_mod_row // MOE_ST

    def mod_idx(s, *_):
        return (jnp.where(s < n_ctx_tiles, 0, 1 + (s - n_ctx_tiles) // lat_per_row), 0, 0)

    vec = pl.BlockSpec((1, D_MODEL), lambda s, *_: (0, 0))
    return pl.pallas_call(
        functools.partial(_moe_combine_kernel, n_ctx_tiles=n_ctx_tiles),
        grid_spec=pltpu.PrefetchScalarGridSpec(
            num_scalar_prefetch=3, grid=(n_tiles,),
            in_specs=[tile(LANES), tile(D_MODEL), pl.BlockSpec((1, 1, 6 * D_MODEL), mod_idx), vec, vec,
                      anyspec],
            out_specs=[pl.BlockSpec((MOE_ST, D_MODEL), lambda s, *_: (jnp.minimum(s, n_ctx_tiles - 1), 0)),
                       pl.BlockSpec((MOE_ST, D_MODEL), lambda s, *_: (jnp.maximum(s - n_ctx_tiles, 0), 0))],
            scratch_shapes=[pltpu.VMEM((2, MOE_SLOTS, D_MODEL), BF16), pltpu.SemaphoreType.DMA((2,))]),
        out_shape=[jax.ShapeDtypeStruct((n_ctx, D_MODEL), F32),
                   jax.ShapeDtypeStruct((n_tok - n_ctx, D_MODEL), F32)],
        compiler_params=_cparams(("arbitrary",)),
        name="moe_combine",
    )(loc, len16, off, gates_all, x1_all, mod.reshape(mod.shape[0], 1, 6 * D_MODEL), ln2g, ln2b, o_sorted)


def _grid_pos_embed(n_tokens):
    rows = n_tokens // GRID_W
    row = np.repeat(np.arange(rows, dtype=np.float64), GRID_W)
    col = np.tile(np.arange(GRID_W, dtype=np.float64), rows)
    quarter = D_MODEL // 4
    omega = 1.0 / (POS_BASE ** (np.arange(quarter, dtype=np.float64) / quarter))
    er = row[:, None] * omega
    ec = col[:, None] * omega
    return jnp.asarray(np.concatenate([np.sin(er), np.cos(er), np.sin(ec), np.cos(ec)], axis=-1), F32)


def _mixers(x, pos, mod3, h0_re, h0_im, tabs, filt, s5ops, wts, tm):
    bsz, n_tok, _ = x.shape
    shared = mod3.shape[0] == 1
    x3 = x.reshape(1, bsz * n_tok, D_MODEL) if shared else x
    proj_hy, u_s5 = _in_proj(x3, pos, mod3, wts['w_in'], tm)
    y_hy = _hyena(proj_hy.reshape(bsz, n_tok, 3 * D_HY), tabs, filt,
                  wts['hy_conv_w'], wts['hy_conv_b'], wts['hy_fbias'])
    y_s5, f_re, f_im = _s5(u_s5.reshape(bsz, n_tok, D_S5), s5ops, h0_re, h0_im)
    return y_hy.reshape(bsz * n_tok, D_HY), y_s5.reshape(bsz * n_tok, D_S5), f_re, f_im


def kernel(x_prompt, x_sample, state_s5_re, state_s5_im, c, c_ctx, w_ada, b_ada, w_in, hy_conv_w, hy_conv_b, hy_f_w1, hy_f_b1, hy_f_w2, hy_f_b2, hy_f_w3, hy_freq, hy_fbias, s5_a_re, s5_a_im, s5_log_dt, s5_b_re, s5_b_im, s5_c_re, s5_c_im, s5_d, s5_w_glu, s5_b_glu, out_norm_g, w_out, ln1_g, ln1_b, moe_w_r1, moe_b_r1, moe_w_r2, moe_b_r2, moe_w_gate, moe_w_up, moe_w_down, ln2_g, ln2_b):
    b_ctx, l_ctx, _ = x_prompt.shape
    b_lat, l_lat, _ = x_sample.shape
    g, p = S5_GROUPS, S5_STATE
    assert w_ada.shape[0] == 1, "single-layer trunk"
    l = 0

    nrow = 16
    cond = jnp.concatenate([c_ctx[None, :], c, jnp.zeros((nrow - 1 - b_lat, D_MODEL), F32)], axis=0)
    mod = _ada(cond, w_ada[l], b_ada[l])
    mod_ctx = mod[0:1].reshape(1, 1, 6 * D_MODEL)
    mod_lat = mod[1:1 + b_lat].reshape(b_lat, 1, 6 * D_MODEL)

    wr = jnp.concatenate([moe_w_r2[l].transpose(1, 0, 2).reshape(D_MODEL, N_EXPERTS), moe_w_r1[l]], axis=1)
    wr = jnp.pad(wr, ((0, 0), (0, LANES - wr.shape[1])))
    br = jnp.concatenate([moe_b_r2[l].reshape(-1), moe_b_r1[l]])
    br = jnp.pad(br, (0, LANES - br.shape[0])).reshape(1, LANES)
    wr_hi, wr_lo = _split(wr)

    wts = {
        'w_in': w_in[l], 'hy_conv_w': hy_conv_w[l], 'hy_conv_b': hy_conv_b[l],
        'hy_fbias': hy_fbias[l], 'w_glu': s5_w_glu[l].astype(BF16), 'b_glu': s5_b_glu[l].reshape(1, -1),
        'out_norm_g': out_norm_g[l].reshape(1, -1), 'w_out': w_out[l].astype(BF16),
        'ln1_g': ln1_g[l].reshape(1, -1), 'ln1_b': ln1_b[l].reshape(1, -1),
        'wr_hi': wr_hi, 'wr_lo': wr_lo, 'br': br,
        'w_gate': moe_w_gate[l], 'w_up': moe_w_up[l], 'w_down': moe_w_down[l],
        'ln2_g': ln2_g[l].reshape(1, -1), 'ln2_b': ln2_b[l].reshape(1, -1),
    }

    s5ops = _s5_operators(s5_a_re[l], s5_a_im[l], s5_log_dt[l], s5_b_re[l], s5_b_im[l],
                          s5_c_re[l], s5_c_im[l], s5_d[l])
    tabs_ctx = _tables(l_ctx)
    tabs_lat = _tables(l_lat)
    filt_args = (hy_f_w1[l], hy_f_b1[l], hy_f_w2[l], hy_f_b2[l], hy_f_w3[l], hy_freq[l])
    filt_ctx = _hyena_filters(l_ctx, tabs_ctx, *filt_args)
    filt_lat = _hyena_filters(l_lat, tabs_lat, *filt_args)

    zero = jnp.zeros((g, b_ctx, 2 * p), F32)
    yhy_c, ys5_c, f_re, f_im = _mixers(x_prompt, None, mod_ctx, zero, zero, tabs_ctx, filt_ctx, s5ops, wts, 1024)
    unpack = lambda f: f.reshape(g, b_ctx, 2, p).transpose(1, 2, 0, 3)[:, None]
    new_re, new_im = unpack(f_re), unpack(f_im)

    pack = lambda s: s[:, l].transpose(2, 0, 1, 3).reshape(g, b_lat, 2 * p)
    pos = _grid_pos_embed(l_lat)
    yhy_l, ys5_l, _, _ = _mixers(x_sample, pos, mod_lat, pack(state_s5_re), pack(state_s5_im),
                                 tabs_lat, filt_lat, s5ops, wts, 1024)

    n_ctx = b_ctx * l_ctx
    x1_all, h2_all, gates_all, tile_counts = _out_proj(
        x_prompt.reshape(n_ctx, D_MODEL), x_sample.reshape(b_lat * l_lat, D_MODEL), pos,
        yhy_c, yhy_l, ys5_c, ys5_l, mod, wts['w_glu'], wts['b_glu'], wts['out_norm_g'], wts['w_out'],
        wts['ln1_g'], wts['ln1_b'], wts['wr_hi'], wts['wr_lo'], wts['br'], MOE_ST)
    y_ctx, y_lat = _moe(h2_all, gates_all, tile_counts, x1_all, mod,
                        wts['w_gate'], wts['w_up'], wts['w_down'], wts['ln2_g'], wts['ln2_b'],
                        n_ctx, l_lat)
    return (y_ctx.reshape(x_prompt.shape), y_lat.reshape(x_sample.shape), new_re, new_im)
```
